```python
import math
import jax
import jax.numpy as jnp
from jax import lax
import numpy as np

D_MODEL = 1024
BATCH = 2
SEQ = 16384
DEPTH = 2

GRID_W = 64
CTX_LEN = 256
EPS = 1e-6
F32 = jnp.float32

DN_HEADS = 4
DN_DIM = 128
DN_WIDTH = DN_HEADS * DN_DIM
DN_CONV = 3
DN_CHUNK = 64
DN_COLS = 4 * DN_WIDTH + 4 * DN_HEADS
SWA_HEADS = 4
SWA_KV_HEADS = 2
SWA_DIM = 128
SWA_WINDOW = 128
SWA_BLOCK = 128
SWA_COLS = (SWA_HEADS + 2 * SWA_KV_HEADS) * SWA_DIM
ROPE_THETA = 10000.0
AB_COLS = DN_COLS + SWA_COLS
RET_HEADS = 4
RET_DIM = 128
RET_WIDTH = RET_HEADS * RET_DIM
RET_CHUNK = 128
RET_COLS = 4 * RET_WIDTH
HY_CH = 512
HY_ORDER = 2
HY_CONV = 3
HY_BANDS = 8
HY_EMB = 1 + 2 * HY_BANDS
HY_HID = 64
HY_SIN_FREQ = 1.0
HY_TARGET = 1e-2
HY_DECAY_SHORT = 0.3
HY_DECAY_LONG = 1.5
HY_COLS = (HY_ORDER + 1) * HY_CH
CD_COLS = RET_COLS + HY_COLS
MIX_WIDTH = DN_WIDTH + SWA_HEADS * SWA_DIM
FFN_HIDDEN = -(-8 * D_MODEL // (3 * 256)) * 256
N_EVEN = (DEPTH + 1) // 2
N_ODD = DEPTH // 2

kernel_name = 'hybrid_deltanet_swa_retention_hyena_dit'


def rmsnorm(x, w):
    xf = x.astype(F32)
    y = xf * lax.rsqrt(jnp.mean(xf * xf, axis=-1, keepdims=True) + EPS)
    return (y * w.astype(F32)).astype(x.dtype)


def l2norm(x):
    return x * lax.rsqrt(jnp.sum(x * x, axis=-1, keepdims=True) + EPS)


def modulate(h, shift, scale):
    return h * (1 + scale) + shift


def swiglu(h, w_in, w_out):
    gate, up = jnp.split(h @ w_in, 2, axis=-1)
    return (jax.nn.silu(gate) * up) @ w_out


def rotate_half(x, ang):
    cos = jnp.cos(ang)[:, None, :]
    sin = jnp.sin(ang)[:, None, :]
    x1, x2 = jnp.split(x, 2, axis=-1)
    return jnp.concatenate([x1 * cos - x2 * sin, x2 * cos + x1 * sin], axis=-1).astype(x.dtype)


def axial_angles(rows):
    m = SWA_DIM // 4
    inv = ROPE_THETA ** (-jnp.arange(m, dtype=F32) / m)
    row = jnp.repeat(jnp.arange(rows, dtype=F32), GRID_W)
    col = jnp.broadcast_to(jnp.arange(GRID_W, dtype=F32), (rows, GRID_W)).reshape(-1)
    return row[:, None] * inv, col[:, None] * inv


def axial_rope(x, ang_row, ang_col):
    h = x.shape[-1] // 2
    return jnp.concatenate([rotate_half(x[..., :h], ang_row), rotate_half(x[..., h:], ang_col)], axis=-1)


def short_conv(x, w):
    K, C = w.shape
    return lax.conv_general_dilated(x, w[:, None, :].astype(x.dtype), window_strides=(1,),
                                    padding=[(K // 2, K // 2)],
                                    dimension_numbers=('NWC', 'WIO', 'NWC'),
                                    feature_group_count=C)


def flip_time(t):
    return jnp.flip(t, axis=2)


def keep_time(t):
    return t


def delta_chunk_scan(q, k, v, g, beta, s0):
    B, H, L, dk = q.shape
    dv = v.shape[-1]
    C = DN_CHUNK
    N = L // C
    q = q.reshape(B, H, N, C, dk)
    k = k.reshape(B, H, N, C, dk)
    v = v.reshape(B, H, N, C, dv)
    g = g.reshape(B, H, N, C)
    beta = beta.reshape(B, H, N, C)
    gam = jnp.cumsum(g, axis=-1)
    idx = jnp.arange(C)
    incl = idx[:, None] >= idx[None, :]
    strict = idx[:, None] > idx[None, :]
    e = jnp.exp(jnp.where(incl, gam[..., :, None] - gam[..., None, :], 0.0))
    dec_incl = jnp.where(incl, e, 0.0)
    dec_strict = jnp.where(strict, e, 0.0)
    kb = k * beta[..., None]
    m = jnp.einsum('bhnid,bhnjd->bhnij', kb, k) * dec_strict
    a = m + jnp.eye(C, dtype=F32)
    rhs = jnp.concatenate([v * beta[..., None], kb * jnp.exp(gam)[..., None]], axis=-1)
    sol = lax.linalg.triangular_solve(a, rhs, left_side=True, lower=True, unit_diagonal=True)
    u, w = sol[..., :dv], sol[..., dv:]
    attn = jnp.einsum('bhnid,bhnjd->bhnij', q, k) * dec_incl
    q_dec = q * jnp.exp(gam)[..., None]
    k_dec = k * jnp.exp(gam[..., -1:] - gam)[..., None]
    chunk_dec = jnp.exp(gam[..., -1])

    def step(S, xs):
        u_n, w_n, a_n, qd_n, kd_n, cd_n = xs
        v_new = u_n - jnp.einsum('bhck,bhkv->bhcv', w_n, S)
        o = jnp.einsum('bhck,bhkv->bhcv', qd_n, S) + jnp.einsum('bhij,bhjv->bhiv', a_n, v_new)
        S = S * cd_n[..., None, None] + jnp.einsum('bhck,bhcv->bhkv', kd_n, v_new)
        return S, o

    xs = tuple(jnp.moveaxis(t, 2, 0) for t in (u, w, attn, q_dec, k_dec, chunk_dec))
    S, o = lax.scan(step, s0, xs)
    return jnp.moveaxis(o, 0, 2).reshape(B, H, L, dv), S


def dn_features(p, conv_w, a_log, dt_bias):
    B, L, _ = p.shape
    qkv = jax.nn.silu(short_conv(p[..., :3 * DN_WIDTH], conv_w)).astype(F32)
    heads = lambda t: t.reshape(B, L, DN_HEADS, DN_DIM).transpose(0, 2, 1, 3)
    q, k, v = (heads(t) for t in jnp.split(qkv, 3, axis=-1))
    q = l2norm(q) * DN_DIM ** -0.5
    k = l2norm(k)
    gate = p[..., 3 * DN_WIDTH:4 * DN_WIDTH]
    ab = p[..., 4 * DN_WIDTH:].astype(F32).reshape(B, L, 2, 2, DN_HEADS)
    g = -jnp.exp(a_log.astype(F32)) * jax.nn.softplus(ab[:, :, 0] + dt_bias.astype(F32))
    beta = jax.nn.sigmoid(ab[:, :, 1])
    return q, k, v, g.transpose(2, 0, 3, 1), beta.transpose(2, 0, 3, 1), gate


def head_norm_gate(o, gate, w):
    B, H, L, dv = o.shape
    y = rmsnorm(o.transpose(0, 2, 1, 3), w)
    return (y * jax.nn.silu(gate.astype(F32).reshape(B, L, H, dv))).reshape(B, L, H * dv)


def gated_deltanet(p_lat, p_ctx, conv_w, a_log, dt_bias, norm_w, need_ctx):
    lat = dn_features(p_lat, conv_w, a_log, dt_bias)
    cx = dn_features(p_ctx, conv_w, a_log, dt_bias)
    B = p_lat.shape[0]
    o_lat, o_ctx = [], []
    for d in range(2):
        fl = flip_time if d == 1 else keep_time
        s0 = jnp.zeros((B, DN_HEADS, DN_DIM, DN_DIM), F32)
        oc, s_ctx = delta_chunk_scan(fl(cx[0]), fl(cx[1]), fl(cx[2]), fl(cx[3][d]), fl(cx[4][d]), s0)
        ol, _ = delta_chunk_scan(fl(lat[0]), fl(lat[1]), fl(lat[2]), fl(lat[3][d]), fl(lat[4][d]), s_ctx)
        o_lat.append(fl(ol))
        o_ctx.append(fl(oc))
    out_lat = head_norm_gate(o_lat[0] + o_lat[1], lat[5], norm_w)
    out_ctx = head_norm_gate(o_ctx[0] + o_ctx[1], cx[5], norm_w) if need_ctx else None
    return out_lat, out_ctx


def gqa_qkv(p, q_norm_w, k_norm_w):
    B, L, _ = p.shape
    nq, nk = SWA_HEADS * SWA_DIM, SWA_KV_HEADS * SWA_DIM
    q = rmsnorm(p[..., :nq].reshape(B, L, SWA_HEADS, SWA_DIM), q_norm_w)
    k = rmsnorm(p[..., nq:nq + nk].reshape(B, L, SWA_KV_HEADS, SWA_DIM), k_norm_w)
    v = p[..., nq + nk:].reshape(B, L, SWA_KV_HEADS, SWA_DIM)
    return q, k, v


def banded_attention(q, k, v, kc, vc, sink):
    B, L, Hq, d = q.shape
    Hkv = k.shape[2]
    G = Hq // Hkv
    W = SWA_BLOCK
    NB = L // W
    qb = q.reshape(B, NB, W, Hkv, G, d)

    def band(t):
        tp = jnp.pad(t.reshape(B, NB, W, Hkv, d), ((0, 0), (1, 1), (0, 0), (0, 0), (0, 0)))
        return jnp.concatenate([tp[:, :-2], tp[:, 1:-1], tp[:, 2:]], axis=2)

    kb, vb = band(k), band(v)
    scale = d ** -0.5
    s_loc = jnp.einsum('bnqhgd,bnkhd->bnhgqk', qb, kb).astype(F32) * scale
    s_ctx = jnp.einsum('bnqhgd,bchd->bnhgqc', qb, kc).astype(F32) * scale
    rel = (jnp.arange(3 * W) - W)[None, :] - jnp.arange(W)[:, None]
    kblk = jnp.arange(NB)[:, None] + jnp.arange(3 * W)[None, :] // W - 1
    valid = (jnp.abs(rel) <= SWA_WINDOW)[None] & ((kblk >= 0) & (kblk < NB))[:, None, :]
    s_loc = jnp.where(valid[None, :, None, None], s_loc, -jnp.inf)
    s_sink = jnp.broadcast_to(sink.astype(F32).reshape(Hkv, G, 1, 1), s_loc.shape[:-1] + (1,))
    prob = jax.nn.softmax(jnp.concatenate([s_loc, s_ctx, s_sink], axis=-1), axis=-1).astype(v.dtype)
    n_ctx = kc.shape[1]
    o = (jnp.einsum('bnhgqk,bnkhd->bnqhgd', prob[..., :3 * W], vb)
         + jnp.einsum('bnhgqc,bchd->bnqhgd', prob[..., 3 * W:3 * W + n_ctx], vc))
    return o.reshape(B, L, Hq * d)


def context_attention(qc, kc, vc, sink):
    B, Cn, Hq, d = qc.shape
    Hkv = kc.shape[2]
    G = Hq // Hkv
    q = qc.reshape(B, Cn, Hkv, G, d)
    s = jnp.einsum('bqhgd,bkhd->bhgqk', q, kc).astype(F32) * d ** -0.5
    s_sink = jnp.broadcast_to(sink.astype(F32).reshape(Hkv, G, 1, 1), s.shape[:-1] + (1,))
    prob = jax.nn.softmax(jnp.concatenate([s, s_sink], axis=-1), axis=-1)[..., :-1].astype(vc.dtype)
    return jnp.einsum('bhgqk,bkhd->bqhgd', prob, vc).reshape(B, Cn, Hq * d)


def window_gqa(p_lat, p_ctx, ang_row, ang_col, q_norm_w, k_norm_w, sink, need_ctx):
    ql, kl, vl = gqa_qkv(p_lat, q_norm_w, k_norm_w)
    ql = axial_rope(ql, ang_row, ang_col)
    kl = axial_rope(kl, ang_row, ang_col)
    qc, kc, vc = gqa_qkv(p_ctx, q_norm_w, k_norm_w)
    o_lat = banded_attention(ql, kl, vl, kc, vc, sink)
    o_ctx = context_attention(qc, kc, vc, sink) if need_ctx else None
    return o_lat, o_ctx


def ab_mixer(p_lat, p_ctx, ang_row, ang_col, dn_conv_w, dn_a_log, dn_dt_bias, dn_norm_w,
             q_norm_w, k_norm_w, sink, need_ctx):
    a_lat, a_ctx = gated_deltanet(p_lat[..., :DN_COLS], p_ctx[..., :DN_COLS], dn_conv_w, dn_a_log,
                                  dn_dt_bias, dn_norm_w, need_ctx)
    b_lat, b_ctx = window_gqa(p_lat[..., DN_COLS:], p_ctx[..., DN_COLS:], ang_row, ang_col,
                              q_norm_w, k_norm_w, sink, need_ctx)
    o_lat = jnp.concatenate([a_lat, b_lat.astype(F32)], axis=-1)
    o_ctx = jnp.concatenate([a_ctx, b_ctx.astype(F32)], axis=-1) if need_ctx else None
    return o_lat, o_ctx


def retention_chunk_scan(q, k, v, log_gamma, s0):
    B, H, L, dk = q.shape
    dv = v.shape[-1]
    C = RET_CHUNK
    N = L // C
    q = q.reshape(B, H, N, C, dk)
    k = k.reshape(B, H, N, C, dk)
    v = v.reshape(B, H, N, C, dv)
    pos = jnp.arange(C, dtype=F32)
    rel = pos[:, None] - pos[None, :]
    incl = rel >= 0
    lg = log_gamma[:, None, None]
    dmat = jnp.where(incl, jnp.exp(lg * jnp.where(incl, rel, 0.0)), 0.0)
    scores = jnp.einsum('bhnid,bhnjd->bhnij', q, k) * dmat[:, None]
    o_in = jnp.einsum('bhnij,bhnje->bhnie', scores, v)
    q_dec = q * jnp.exp(log_gamma[:, None] * (pos + 1.0))[:, None, :, None]
    k_dec = k * jnp.exp(log_gamma[:, None] * (C - 1.0 - pos))[:, None, :, None]
    chunk_dec = jnp.exp(log_gamma * C)[None, :, None, None]

    def step(S, xs):
        qd, kd, vn = xs
        o = jnp.einsum('bhcd,bhde->bhce', qd, S)
        S = S * chunk_dec + jnp.einsum('bhcd,bhce->bhde', kd, vn)
        return S, o

    xs = tuple(jnp.moveaxis(t, 2, 0) for t in (q_dec, k_dec, v))
    S, o_x = lax.scan(step, s0, xs)
    o = o_in + jnp.moveaxis(o_x, 0, 2)
    return o.reshape(B, H, L, dv), S


def ret_features(p):
    B, L, _ = p.shape
    q, k, v, g = jnp.split(p, 4, axis=-1)
    inv = ROPE_THETA ** (-jnp.linspace(0.0, 1.0, RET_DIM // 2, dtype=F32))
    ang = jnp.arange(L, dtype=F32)[:, None] * inv
    heads = lambda t: t.reshape(B, L, RET_HEADS, RET_DIM).astype(F32)
    q = rotate_half(heads(q), ang)
    k = rotate_half(heads(k), ang) * RET_DIM ** -0.5
    tr = lambda t: t.transpose(0, 2, 1, 3)
    return tr(q), tr(k), tr(heads(v)), g


def head_groupnorm_gate(o, gate, w):
    B, H, L, dv = o.shape
    o = o.transpose(0, 2, 1, 3)
    mu = jnp.mean(o, axis=-1, keepdims=True)
    var = jnp.mean(jnp.square(o - mu), axis=-1, keepdims=True)
    y = (o - mu) * lax.rsqrt(var + EPS) * w.astype(F32).reshape(H, dv)
    return (y * jax.nn.silu(gate.astype(F32).reshape(B, L, H, dv))).reshape(B, L, H * dv)


def retention(p_lat, p_ctx, decay_logit, gn_w, need_ctx):
    lat, cx = ret_features(p_lat), ret_features(p_ctx)
    log_gamma = jax.nn.log_sigmoid(decay_logit.astype(F32))
    B = p_lat.shape[0]
    o_lat, o_ctx = [], []
    for d in range(2):
        fl = flip_time if d == 1 else keep_time
        s0 = jnp.zeros((B, RET_HEADS, RET_DIM, RET_DIM), F32)
        oc, s_ctx = retention_chunk_scan(fl(cx[0]), fl(cx[1]), fl(cx[2]), log_gamma[d], s0)
        ol, _ = retention_chunk_scan(fl(lat[0]), fl(lat[1]), fl(lat[2]), log_gamma[d], s_ctx)
        o_lat.append(fl(ol))
        o_ctx.append(fl(oc))
    out_lat = head_groupnorm_gate(o_lat[0] + o_lat[1], lat[3], gn_w)
    out_ctx = head_groupnorm_gate(o_ctx[0] + o_ctx[1], cx[3], gn_w) if need_ctx else None
    return out_lat, out_ctx


def hyena_filters(L, w1, b1, w2, b2, w3):
    pos = jnp.arange(L, dtype=F32)
    t = pos / max(L - 1, 1)
    bands = jnp.linspace(1e-4, HY_BANDS - 1, HY_BANDS, dtype=F32)
    phase = (2.0 * math.pi / L) * pos[:, None] * bands[None, :]
    z = jnp.concatenate([t[:, None], jnp.cos(phase), -jnp.sin(phase)], axis=-1)
    h = jnp.sin(HY_SIN_FREQ * (z @ w1.astype(F32) + b1.astype(F32)))
    h = jnp.sin(HY_SIN_FREQ * (h @ w2.astype(F32) + b2.astype(F32)))
    h = (h @ w3.astype(F32)).reshape(L, HY_ORDER, 2, HY_CH)
    rates = jnp.abs(jnp.linspace(math.log(HY_TARGET) / HY_DECAY_LONG, math.log(HY_TARGET) / HY_DECAY_SHORT,
                                 HY_CH, dtype=F32))
    h = h * jnp.exp(-t[:, None] * rates[None, :])[:, None, None, :]
    h = h / (jnp.sum(jnp.abs(h), axis=(0, 2), keepdims=True) + EPS)
    return h.transpose(1, 2, 0, 3)


def bidir_long_conv(u, h_fwd, h_bwd, skip):
    B, L, C = u.shape
    taps = jnp.concatenate([h_fwd, jnp.zeros((1, C), F32), h_bwd[:0:-1]], axis=0)
    y = jnp.fft.irfft(jnp.fft.rfft(u, n=2 * L, axis=1) * jnp.fft.rfft(taps, axis=0)[None],
                      n=2 * L, axis=1)[:, :L]
    return y + u * skip


def hyena(p, conv_w, w1, b1, w2, b2, w3, skip):
    L = p.shape[1]
    filt = hyena_filters(L, w1, b1, w2, b2, w3)
    u = short_conv(p, conv_w).astype(F32)
    parts = jnp.split(u, HY_ORDER + 1, axis=-1)
    z = parts[0]
    for n in range(HY_ORDER):
        z = parts[n + 1] * bidir_long_conv(z, filt[n, 0], filt[n, 1], skip[n].astype(F32))
    return z


def cd_mixer(p_lat, p_ctx, ret_decay_logit, ret_gn_w, hy_conv_w, hy_f_w1, hy_f_b1, hy_f_w2, hy_f_b2,
             hy_f_w3, hy_bias, need_ctx):
    c_lat, c_ctx = retention(p_lat[..., :RET_COLS], p_ctx[..., :RET_COLS], ret_decay_logit, ret_gn_w, need_ctx)
    hy = lambda p: hyena(p, hy_conv_w, hy_f_w1, hy_f_b1, hy_f_w2, hy_f_b2, hy_f_w3, hy_bias)
    o_lat = jnp.concatenate([c_lat, hy(p_lat[..., RET_COLS:])], axis=-1)
    o_ctx = jnp.concatenate([c_ctx, hy(p_ctx[..., RET_COLS:])], axis=-1) if need_ctx else None
    return o_lat, o_ctx


def setup_inputs(seed: int = 0) -> dict:
    key = jax.random.key(seed)
    keys = iter(jax.random.split(key, 40))

    def nrm(shape, scale):
        return jax.random.normal(next(keys), shape, F32) * scale

    def unif(shape, lo, hi):
        return jax.random.uniform(next(keys), shape, F32, lo, hi)

    dt = jnp.exp(unif((N_EVEN, 2, DN_HEADS), math.log(1e-3), math.log(1e-1)))
    ret_logit0 = jnp.log(2.0 ** (5.0 + jnp.arange(RET_HEADS, dtype=F32)) - 1.0)
    return {
        'x': nrm((BATCH, SEQ, D_MODEL), 1.0),
        'c': nrm((BATCH, D_MODEL), 1.0),
        'ctx': nrm((BATCH, CTX_LEN, D_MODEL), 1.0),
        'c_ctx': nrm((D_MODEL,), 1.0),
        'mod_w': nrm((DEPTH, D_MODEL, 6 * D_MODEL), 0.5 * D_MODEL ** -0.5),
        'mod_b': nrm((DEPTH, 6 * D_MODEL), 0.02),
        'norm_mix_w': 1.0 + nrm((DEPTH, D_MODEL), 0.02),
        'norm_ffn_w': 1.0 + nrm((DEPTH, D_MODEL), 0.02),
        'ffn_w_in': nrm((DEPTH, D_MODEL, 2 * FFN_HIDDEN), D_MODEL ** -0.5),
        'ffn_w_out': nrm((DEPTH, FFN_HIDDEN, D_MODEL), FFN_HIDDEN ** -0.5),
        'ab_w_in': nrm((N_EVEN, D_MODEL, AB_COLS), D_MODEL ** -0.5),
        'ab_w_out': nrm((N_EVEN, MIX_WIDTH, D_MODEL), MIX_WIDTH ** -0.5),
        'dn_conv_w': nrm((N_EVEN, DN_CONV, 3 * DN_WIDTH), DN_CONV ** -0.5),
        'dn_a_log': jnp.log(unif((N_EVEN, 2, DN_HEADS), 1.0, 16.0)),
        'dn_dt_bias': dt + jnp.log(-jnp.expm1(-dt)),
        'dn_norm_w': 1.0 + nrm((N_EVEN, DN_DIM), 0.02),
        'swa_q_norm_w': 1.0 + nrm((N_EVEN, SWA_DIM), 0.02),
        'swa_k_norm_w': 1.0 + nrm((N_EVEN, SWA_DIM), 0.02),
        'swa_sink': nrm((N_EVEN, SWA_HEADS), 0.5),
        'cd_w_in': nrm((N_ODD, D_MODEL, CD_COLS), D_MODEL ** -0.5),
        'cd_w_out': nrm((N_ODD, MIX_WIDTH, D_MODEL), MIX_WIDTH ** -0.5),
        'ret_decay_logit': ret_logit0 + nrm((N_ODD, 2, RET_HEADS), 0.05),
        'ret_gn_w': 1.0 + nrm((N_ODD, RET_WIDTH), 0.02),
        'hy_conv_w': nrm((N_ODD, HY_CONV, HY_COLS), HY_CONV ** -0.5),
        'hy_f_w1': nrm((N_ODD, HY_EMB, HY_HID), HY_EMB ** -0.5),
        'hy_f_b1': nrm((N_ODD, HY_HID), 0.1),
        'hy_f_w2': nrm((N_ODD, HY_HID, HY_HID), HY_HID ** -0.5),
        'hy_f_b2': nrm((N_ODD, HY_HID), 0.1),
        'hy_f_w3': nrm((N_ODD, HY_HID, HY_ORDER * 2 * HY_CH), HY_HID ** -0.5),
        'hy_bias': nrm((N_ODD, HY_ORDER, HY_CH), 0.5),
    }


def reference(x, c, ctx, c_ctx, mod_w, mod_b, norm_mix_w, norm_ffn_w, ffn_w_in, ffn_w_out,
              ab_w_in, ab_w_out, dn_conv_w, dn_a_log, dn_dt_bias, dn_norm_w, swa_q_norm_w, swa_k_norm_w,
              swa_sink, cd_w_in, cd_w_out, ret_decay_logit, ret_gn_w, hy_conv_w, hy_f_w1, hy_f_b1,
              hy_f_w2, hy_f_b2, hy_f_w3, hy_bias):
    L = x.shape[1]
    rows = L // GRID_W
    ang_row, ang_col = axial_angles(rows)
    c_act = jax.nn.silu(c)
    cc_act = jax.nn.silu(c_ctx)[None]
    h_ctx = ctx
    for layer in range(DEPTH):
        need_ctx = layer != DEPTH - 1
        i = layer // 2
        mod = jnp.split((c_act @ mod_w[layer] + mod_b[layer])[:, None], 6, axis=-1)
        mod_c = jnp.split((cc_act @ mod_w[layer] + mod_b[layer])[:, None], 6, axis=-1)
        hx = modulate(rmsnorm(x, norm_mix_w[layer]), mod[0], mod[1])
        hc = modulate(rmsnorm(h_ctx, norm_mix_w[layer]), mod_c[0], mod_c[1])
        if layer % 2 == 0:
            o_lat, o_ctx = ab_mixer(hx @ ab_w_in[i], hc @ ab_w_in[i], ang_row, ang_col, dn_conv_w[i],
                                    dn_a_log[i], dn_dt_bias[i], dn_norm_w[i], swa_q_norm_w[i],
                                    swa_k_norm_w[i], swa_sink[i], need_ctx)
            w_out = ab_w_out[i]
        else:
            o_lat, o_ctx = cd_mixer(hx @ cd_w_in[i], hc @ cd_w_in[i], ret_decay_logit[i], ret_gn_w[i],
                                    hy_conv_w[i], hy_f_w1[i], hy_f_b1[i], hy_f_w2[i], hy_f_b2[i],
                                    hy_f_w3[i], hy_bias[i], need_ctx)
            w_out = cd_w_out[i]
        x = x + mod[2] * (o_lat.astype(x.dtype) @ w_out)
        x = x + mod[5] * swiglu(modulate(rmsnorm(x, norm_ffn_w[layer]), mod[3], mod[4]),
                                ffn_w_in[layer], ffn_w_out[layer])
        if need_ctx:
            h_ctx = h_ctx + mod_c[2] * (o_ctx.astype(h_ctx.dtype) @ w_out)
            h_ctx = h_ctx + mod_c[5] * swiglu(modulate(rmsnorm(h_ctx, norm_ffn_w[layer]), mod_c[3], mod_c[4]),
                                              ffn_w_in[layer], ffn_w_out[layer])
    return x
```

```python
import functools
import math

import jax
import jax.numpy as jnp
import numpy as np
from jax import lax
from jax.experimental import pallas as pl
from jax.experimental.pallas import tpu as pltpu

F32 = jnp.float32
BF16 = jnp.bfloat16
EPS = 1e-6

D_MODEL = 1024
DEPTH = 2
GRID_W = 64

DN_HEADS = 4
DN_DIM = 128
DN_WIDTH = DN_HEADS * DN_DIM
DN_CHUNK = 64
DN_COLS = 4 * DN_WIDTH + 4 * DN_HEADS
SWA_HEADS = 4
SWA_KV_HEADS = 2
SWA_DIM = 128
SWA_WINDOW = 128
SWA_BLOCK = 128
ROPE_THETA = 10000.0
RET_HEADS = 4
RET_DIM = 128
RET_WIDTH = RET_HEADS * RET_DIM
RET_CHUNK = 128
RET_COLS = 4 * RET_WIDTH
HY_CH = 512
HY_ORDER = 2
HY_BANDS = 8
HY_SIN_FREQ = 1.0
HY_TARGET = 1e-2
HY_DECAY_SHORT = 0.3
HY_DECAY_LONG = 1.5

VMEM_LIMIT_BYTES = 56 * 1024 * 1024


def _cparams(sem):
    return pltpu.CompilerParams(dimension_semantics=sem, vmem_limit_bytes=VMEM_LIMIT_BYTES)


def _mod_body(c_ref, w_ref, b_ref, o_ref):
    a = c_ref[...]
    a = a * jax.nn.sigmoid(a)
    o_ref[...] = jnp.dot(a, w_ref[...], preferred_element_type=F32,
                         precision=lax.Precision.HIGHEST) + b_ref[...]


def modulation(c_rows, mod_w, mod_b):
    depth, d, n = mod_w.shape
    tn = 1536
    return pl.pallas_call(
        _mod_body,
        grid=(depth, n // tn),
        in_specs=[pl.BlockSpec((8, d), lambda l, j: (0, 0)),
                  pl.BlockSpec((None, d, tn), lambda l, j: (l, 0, j)),
                  pl.BlockSpec((None, 1, tn), lambda l, j: (l, 0, j))],
        out_specs=pl.BlockSpec((None, 8, tn), lambda l, j: (l, 0, j)),
        out_shape=jax.ShapeDtypeStruct((depth, 8, n), F32),
        compiler_params=_cparams(("parallel", "parallel")),
        name="modulation",
    )(c_rows, mod_w, mod_b.reshape(depth, 1, n))


def _norm_mod(x, nw, shift, scale):
    y = x * lax.rsqrt(jnp.mean(x * x, axis=-1, keepdims=True) + EPS)
    return (y * nw) * (1.0 + scale) + shift


def _inproj_body(x_ref, nw_ref, shift_ref, scale_ref, w_ref, o_ref, h_scr):
    @pl.when(pl.program_id(2) == 0)
    def _():
        h_scr[...] = _norm_mod(x_ref[...], nw_ref[...], shift_ref[...], scale_ref[...]).astype(BF16)

    o_ref[...] = jnp.dot(h_scr[...], w_ref[...], preferred_element_type=F32)


def in_projection(x, nw, shift, scale, w, tm, tn):
    b, l, d = x.shape
    n = w.shape[1]
    return pl.pallas_call(
        _inproj_body,
        grid=(b, l // tm, n // tn),
        in_specs=[pl.BlockSpec((None, tm, d), lambda bi, i, j: (bi, i, 0)),
                  pl.BlockSpec((1, d), lambda bi, i, j: (0, 0)),
                  pl.BlockSpec((None, 1, d), lambda bi, i, j: (bi, 0, 0)),
                  pl.BlockSpec((None, 1, d), lambda bi, i, j: (bi, 0, 0)),
                  pl.BlockSpec((d, tn), lambda bi, i, j: (0, j))],
        out_specs=pl.BlockSpec((None, tm, tn), lambda bi, i, j: (bi, i, j)),
        out_shape=jax.ShapeDtypeStruct((b, l, n), F32),
        scratch_shapes=[pltpu.VMEM((tm, d), BF16)],
        compiler_params=_cparams(("parallel", "parallel", "arbitrary")),
        name="in_projection",
    )(x, nw.reshape(1, d), shift, scale, w)


def _outffn_body(x_ref, o_ref, wo_ref, g2_ref, nw_ref, sh_ref, sc_ref, g5_ref,
                 wg_ref, wu_ref, wd_ref, y_ref, x1_scr, h_scr, acc_scr):
    k = pl.program_id(2)

    @pl.when(k == 0)
    def _():
        mix = jnp.dot(o_ref[...].astype(BF16), wo_ref[...], preferred_element_type=F32)
        x1 = x_ref[...] + g2_ref[...] * mix
        x1_scr[...] = x1
        h_scr[...] = _norm_mod(x1, nw_ref[...], sh_ref[...], sc_ref[...]).astype(BF16)
        acc_scr[...] = jnp.zeros_like(acc_scr)

    h = h_scr[...]
    g = jnp.dot(h, wg_ref[...], preferred_element_type=F32)
    u = jnp.dot(h, wu_ref[...], preferred_element_type=F32)
    a = (g * jax.nn.sigmoid(g) * u).astype(BF16)
    acc_scr[...] += jnp.dot(a, wd_ref[...], preferred_element_type=F32)

    @pl.when(k == pl.num_programs(2) - 1)
    def _():
        y_ref[...] = x1_scr[...] + g5_ref[...] * acc_scr[...]


def out_projection_ffn(x, o, wo, g2, nw, shift, scale, g5, wg, wu, wd, tm, th):
    b, l, d = x.shape
    hid = wg.shape[1]
    tok = lambda bi, i, k: (bi, i, 0)
    vec = lambda bi, i, k: (bi, 0, 0)
    return pl.pallas_call(
        _outffn_body,
        grid=(b, l // tm, hid // th),
        in_specs=[pl.BlockSpec((None, tm, d), tok),
                  pl.BlockSpec((None, tm, d), tok),
                  pl.BlockSpec((d, d), lambda bi, i, k: (0, 0)),
                  pl.BlockSpec((None, 1, d), vec),
                  pl.BlockSpec((1, d), lambda bi, i, k: (0, 0)),
                  pl.BlockSpec((None, 1, d), vec),
                  pl.BlockSpec((None, 1, d), vec),
                  pl.BlockSpec((None, 1, d), vec),
                  pl.BlockSpec((d, th), lambda bi, i, k: (0, k)),
                  pl.BlockSpec((d, th), lambda bi, i, k: (0, k)),
                  pl.BlockSpec((th, d), lambda bi, i, k: (k, 0))],
        out_specs=pl.BlockSpec((None, tm, d), tok),
        out_shape=jax.ShapeDtypeStruct((b, l, d), F32),
        scratch_shapes=[pltpu.VMEM((tm, d), F32), pltpu.VMEM((tm, d), BF16), pltpu.VMEM((tm, d), F32)],
        compiler_params=_cparams(("parallel", "parallel", "arbitrary")),
        name="out_projection_ffn",
    )(x, o, wo, g2, nw.reshape(1, d), shift, scale, g5, wg, wu, wd)


def rmsnorm(x, w):
    xf = x.astype(F32)
    y = xf * lax.rsqrt(jnp.mean(xf * xf, axis=-1, keepdims=True) + EPS)
    return (y * w.astype(F32)).astype(x.dtype)


def l2norm(x):
    return x * lax.rsqrt(jnp.sum(x * x, axis=-1, keepdims=True) + EPS)


def rotate_half(x, ang):
    cos = jnp.cos(ang)[:, None, :]
    sin = jnp.sin(ang)[:, None, :]
    x1, x2 = jnp.split(x, 2, axis=-1)
    return jnp.concatenate([x1 * cos - x2 * sin, x2 * cos + x1 * sin], axis=-1).astype(x.dtype)


def axial_angles(rows):
    m = SWA_DIM // 4
    inv = ROPE_THETA ** (-jnp.arange(m, dtype=F32) / m)
    row = jnp.repeat(jnp.arange(rows, dtype=F32), GRID_W)
    col = jnp.broadcast_to(jnp.arange(GRID_W, dtype=F32), (rows, GRID_W)).reshape(-1)
    return row[:, None] * inv, col[:, None] * inv


def axial_rope(x, ang_row, ang_col):
    h = x.shape[-1] // 2
    return jnp.concatenate([rotate_half(x[..., :h], ang_row), rotate_half(x[..., h:], ang_col)], axis=-1)


def short_conv(x, w):
    K, C = w.shape
    return lax.conv_general_dilated(x, w[:, None, :].astype(x.dtype), window_strides=(1,),
                                    padding=[(K // 2, K // 2)],
                                    dimension_numbers=('NWC', 'WIO', 'NWC'),
                                    feature_group_count=C)


def flip_time(t):
    return jnp.flip(t, axis=2)


def keep_time(t):
    return t


def delta_chunk_scan(q, k, v, g, beta, s0):
    B, H, L, dk = q.shape
    dv = v.shape[-1]
    C = DN_CHUNK
    N = L // C
    q = q.reshape(B, H, N, C, dk)
    k = k.reshape(B, H, N, C, dk)
    v = v.reshape(B, H, N, C, dv)
    g = g.reshape(B, H, N, C)
    beta = beta.reshape(B, H, N, C)
    gam = jnp.cumsum(g, axis=-1)
    idx = jnp.arange(C)
    incl = idx[:, None] >= idx[None, :]
    strict = idx[:, None] > idx[None, :]
    e = jnp.exp(jnp.where(incl, gam[..., :, None] - gam[..., None, :], 0.0))
    dec_incl = jnp.where(incl, e, 0.0)
    dec_strict = jnp.where(strict, e, 0.0)
    kb = k * beta[..., None]
    m = jnp.einsum('bhnid,bhnjd->bhnij', kb, k) * dec_strict
    a = m + jnp.eye(C, dtype=F32)
    rhs = jnp.concatenate([v * beta[..., None], kb * jnp.exp(gam)[..., None]], axis=-1)
    sol = lax.linalg.triangular_solve(a, rhs, left_side=True, lower=True, unit_diagonal=True)
    u, w = sol[..., :dv], sol[..., dv:]
    attn = jnp.einsum('bhnid,bhnjd->bhnij', q, k) * dec_incl
    q_dec = q * jnp.exp(gam)[..., None]
    k_dec = k * jnp.exp(gam[..., -1:] - gam)[..., None]
    chunk_dec = jnp.exp(gam[..., -1])

    def step(S, xs):
        u_n, w_n, a_n, qd_n, kd_n, cd_n = xs
        v_new = u_n - jnp.einsum('bhck,bhkv->bhcv', w_n, S)
        o = jnp.einsum('bhck,bhkv->bhcv', qd_n, S) + jnp.einsum('bhij,bhjv->bhiv', a_n, v_new)
        S = S * cd_n[..., None, None] + jnp.einsum('bhck,bhcv->bhkv', kd_n, v_new)
        return S, o

    xs = tuple(jnp.moveaxis(t, 2, 0) for t in (u, w, attn, q_dec, k_dec, chunk_dec))
    S, o = lax.scan(step, s0, xs)
    return jnp.moveaxis(o, 0, 2).reshape(B, H, L, dv), S


def dn_features(p, conv_w, a_log, dt_bias):
    B, L, _ = p.shape
    qkv = jax.nn.silu(short_conv(p[..., :3 * DN_WIDTH], conv_w)).astype(F32)
    heads = lambda t: t.reshape(B, L, DN_HEADS, DN_DIM).transpose(0, 2, 1, 3)
    q, k, v = (heads(t) for t in jnp.split(qkv, 3, axis=-1))
    q = l2norm(q) * DN_DIM ** -0.5
    k = l2norm(k)
    gate = p[..., 3 * DN_WIDTH:4 * DN_WIDTH]
    ab = p[..., 4 * DN_WIDTH:].astype(F32).reshape(B, L, 2, 2, DN_HEADS)
    g = -jnp.exp(a_log.astype(F32)) * jax.nn.softplus(ab[:, :, 0] + dt_bias.astype(F32))
    beta = jax.nn.sigmoid(ab[:, :, 1])
    return q, k, v, g.transpose(2, 0, 3, 1), beta.transpose(2, 0, 3, 1), gate


def head_norm_gate(o, gate, w):
    B, H, L, dv = o.shape
    y = rmsnorm(o.transpose(0, 2, 1, 3), w)
    return (y * jax.nn.silu(gate.astype(F32).reshape(B, L, H, dv))).reshape(B, L, H * dv)


def gated_deltanet(p_lat, p_ctx, conv_w, a_log, dt_bias, norm_w, need_ctx):
    lat = dn_features(p_lat, conv_w, a_log, dt_bias)
    cx = dn_features(p_ctx, conv_w, a_log, dt_bias)
    B = p_lat.shape[0]
    o_lat, o_ctx = [], []
    for d in range(2):
        fl = flip_time if d == 1 else keep_time
        s0 = jnp.zeros((B, DN_HEADS, DN_DIM, DN_DIM), F32)
        oc, s_ctx = delta_chunk_scan(fl(cx[0]), fl(cx[1]), fl(cx[2]), fl(cx[3][d]), fl(cx[4][d]), s0)
        ol, _ = delta_chunk_scan(fl(lat[0]), fl(lat[1]), fl(lat[2]), fl(lat[3][d]), fl(lat[4][d]), s_ctx)
        o_lat.append(fl(ol))
        o_ctx.append(fl(oc))
    out_lat = head_norm_gate(o_lat[0] + o_lat[1], lat[5], norm_w)
    out_ctx = head_norm_gate(o_ctx[0] + o_ctx[1], cx[5], norm_w) if need_ctx else None
    return out_lat, out_ctx


def gqa_qkv(p, q_norm_w, k_norm_w):
    B, L, _ = p.shape
    nq, nk = SWA_HEADS * SWA_DIM, SWA_KV_HEADS * SWA_DIM
    q = rmsnorm(p[..., :nq].reshape(B, L, SWA_HEADS, SWA_DIM), q_norm_w)
    k = rmsnorm(p[..., nq:nq + nk].reshape(B, L, SWA_KV_HEADS, SWA_DIM), k_norm_w)
    v = p[..., nq + nk:].reshape(B, L, SWA_KV_HEADS, SWA_DIM)
    return q, k, v


def banded_attention(q, k, v, kc, vc, sink):
    B, L, Hq, d = q.shape
    Hkv = k.shape[2]
    G = Hq // Hkv
    W = SWA_BLOCK
    NB = L // W
    qb = q.reshape(B, NB, W, Hkv, G, d)

    def band(t):
        tp = jnp.pad(t.reshape(B, NB, W, Hkv, d), ((0, 0), (1, 1), (0, 0), (0, 0), (0, 0)))
        return jnp.concatenate([tp[:, :-2], tp[:, 1:-1], tp[:, 2:]], axis=2)

    kb, vb = band(k), band(v)
    scale = d ** -0.5
    s_loc = jnp.einsum('bnqhgd,bnkhd->bnhgqk', qb, kb).astype(F32) * scale
    s_ctx = jnp.einsum('bnqhgd,bchd->bnhgqc', qb, kc).astype(F32) * scale
    rel = (jnp.arange(3 * W) - W)[None, :] - jnp.arange(W)[:, None]
    kblk = jnp.arange(NB)[:, None] + jnp.arange(3 * W)[None, :] // W - 1
    valid = (jnp.abs(rel) <= SWA_WINDOW)[None] & ((kblk >= 0) & (kblk < NB))[:, None, :]
    s_loc = jnp.where(valid[None, :, None, None], s_loc, -jnp.inf)
    s_sink = jnp.broadcast_to(sink.astype(F32).reshape(Hkv, G, 1, 1), s_loc.shape[:-1] + (1,))
    prob = jax.nn.softmax(jnp.concatenate([s_loc, s_ctx, s_sink], axis=-1), axis=-1).astype(v.dtype)
    n_ctx = kc.shape[1]
    o = (jnp.einsum('bnhgqk,bnkhd->bnqhgd', prob[..., :3 * W], vb)
         + jnp.einsum('bnhgqc,bchd->bnqhgd', prob[..., 3 * W:3 * W + n_ctx], vc))
    return o.reshape(B, L, Hq * d)


def context_attention(qc, kc, vc, sink):
    B, Cn, Hq, d = qc.shape
    Hkv = kc.shape[2]
    G = Hq // Hkv
    q = qc.reshape(B, Cn, Hkv, G, d)
    s = jnp.einsum('bqhgd,bkhd->bhgqk', q, kc).astype(F32) * d ** -0.5
    s_sink = jnp.broadcast_to(sink.astype(F32).reshape(Hkv, G, 1, 1), s.shape[:-1] + (1,))
    prob = jax.nn.softmax(jnp.concatenate([s, s_sink], axis=-1), axis=-1)[..., :-1].astype(vc.dtype)
    return jnp.einsum('bhgqk,bkhd->bqhgd', prob, vc).reshape(B, Cn, Hq * d)


def window_gqa(p_lat, p_ctx, ang_row, ang_col, q_norm_w, k_norm_w, sink, need_ctx):
    ql, kl, vl = gqa_qkv(p_lat, q_norm_w, k_norm_w)
    ql = axial_rope(ql, ang_row, ang_col)
    kl = axial_rope(kl, ang_row, ang_col)
    qc, kc, vc = gqa_qkv(p_ctx, q_norm_w, k_norm_w)
    o_lat = banded_attention(ql, kl, vl, kc, vc, sink)
    o_ctx = context_attention(qc, kc, vc, sink) if need_ctx else None
    return o_lat, o_ctx


def ab_mixer(p_lat, p_ctx, ang_row, ang_col, dn_conv_w, dn_a_log, dn_dt_bias, dn_norm_w,
             q_norm_w, k_norm_w, sink, need_ctx):
    a_lat, a_ctx = gated_deltanet(p_lat[..., :DN_COLS], p_ctx[..., :DN_COLS], dn_conv_w, dn_a_log,
                                  dn_dt_bias, dn_norm_w, need_ctx)
    b_lat, b_ctx = window_gqa(p_lat[..., DN_COLS:], p_ctx[..., DN_COLS:], ang_row, ang_col,
                              q_norm_w, k_norm_w, sink, need_ctx)
    o_lat = jnp.concatenate([a_lat, b_lat.astype(F32)], axis=-1)
    o_ctx = jnp.concatenate([a_ctx, b_ctx.astype(F32)], axis=-1) if need_ctx else None
    return o_lat, o_ctx


def retention_chunk_scan(q, k, v, log_gamma, s0):
    B, H, L, dk = q.shape
    dv = v.shape[-1]
    C = RET_CHUNK
    N = L // C
    q = q.reshape(B, H, N, C, dk)
    k = k.reshape(B, H, N, C, dk)
    v = v.reshape(B, H, N, C, dv)
    pos = jnp.arange(C, dtype=F32)
    rel = pos[:, None] - pos[None, :]
    incl = rel >= 0
    lg = log_gamma[:, None, None]
    dmat = jnp.where(incl, jnp.exp(lg * jnp.where(incl, rel, 0.0)), 0.0)
    scores = jnp.einsum('bhnid,bhnjd->bhnij', q, k) * dmat[:, None]
    o_in = jnp.einsum('bhnij,bhnje->bhnie', scores, v)
    q_dec = q * jnp.exp(log_gamma[:, None] * (pos + 1.0))[:, None, :, None]
    k_dec = k * jnp.exp(log_gamma[:, None] * (C - 1.0 - pos))[:, None, :, None]
    chunk_dec = jnp.exp(log_gamma * C)[None, :, None, None]

    def step(S, xs):
        qd, kd, vn = xs
        o = jnp.einsum('bhcd,bhde->bhce', qd, S)
        S = S * chunk_dec + jnp.einsum('bhcd,bhce->bhde', kd, vn)
        return S, o

    xs = tuple(jnp.moveaxis(t, 2, 0) for t in (q_dec, k_dec, v))
    S, o_x = lax.scan(step, s0, xs)
    o = o_in + jnp.moveaxis(o_x, 0, 2)
    return o.reshape(B, H, L, dv), S


def ret_features(p):
    B, L, _ = p.shape
    q, k, v, g = jnp.split(p, 4, axis=-1)
    inv = ROPE_THETA ** (-jnp.linspace(0.0, 1.0, RET_DIM // 2, dtype=F32))
    ang = jnp.arange(L, dtype=F32)[:, None] * inv
    heads = lambda t: t.reshape(B, L, RET_HEADS, RET_DIM).astype(F32)
    q = rotate_half(heads(q), ang)
    k = rotate_half(heads(k), ang) * RET_DIM ** -0.5
    tr = lambda t: t.transpose(0, 2, 1, 3)
    return tr(q), tr(k), tr(heads(v)), g


def head_groupnorm_gate(o, gate, w):
    B, H, L, dv = o.shape
    o = o.transpose(0, 2, 1, 3)
    mu = jnp.mean(o, axis=-1, keepdims=True)
    var = jnp.mean(jnp.square(o - mu), axis=-1, keepdims=True)
    y = (o - mu) * lax.rsqrt(var + EPS) * w.astype(F32).reshape(H, dv)
    return (y * jax.nn.silu(gate.astype(F32).reshape(B, L, H, dv))).reshape(B, L, H * dv)


def retention(p_lat, p_ctx, decay_logit, gn_w, need_ctx):
    lat, cx = ret_features(p_lat), ret_features(p_ctx)
    log_gamma = jax.nn.log_sigmoid(decay_logit.astype(F32))
    B = p_lat.shape[0]
    o_lat, o_ctx = [], []
    for d in range(2):
        fl = flip_time if d == 1 else keep_time
        s0 = jnp.zeros((B, RET_HEADS, RET_DIM, RET_DIM), F32)
        oc, s_ctx = retention_chunk_scan(fl(cx[0]), fl(cx[1]), fl(cx[2]), log_gamma[d], s0)
        ol, _ = retention_chunk_scan(fl(lat[0]), fl(lat[1]), fl(lat[2]), log_gamma[d], s_ctx)
        o_lat.append(fl(ol))
        o_ctx.append(fl(oc))
    out_lat = head_groupnorm_gate(o_lat[0] + o_lat[1], lat[3], gn_w)
    out_ctx = head_groupnorm_gate(o_ctx[0] + o_ctx[1], cx[3], gn_w) if need_ctx else None
    return out_lat, out_ctx


def hyena_filters(L, w1, b1, w2, b2, w3):
    pos = jnp.arange(L, dtype=F32)
    t = pos / max(L - 1, 1)
    bands = jnp.linspace(1e-4, HY_BANDS - 1, HY_BANDS, dtype=F32)
    phase = (2.0 * math.pi / L) * pos[:, None] * bands[None, :]
    z = jnp.concatenate([t[:, None], jnp.cos(phase), -jnp.sin(phase)], axis=-1)
    h = jnp.sin(HY_SIN_FREQ * (z @ w1.astype(F32) + b1.astype(F32)))
    h = jnp.sin(HY_SIN_FREQ * (h @ w2.astype(F32) + b2.astype(F32)))
    h = (h @ w3.astype(F32)).reshape(L, HY_ORDER, 2, HY_CH)
    rates = jnp.abs(jnp.linspace(math.log(HY_TARGET) / HY_DECAY_LONG, math.log(HY_TARGET) / HY_DECAY_SHORT,
                                 HY_CH, dtype=F32))
    h = h * jnp.exp(-t[:, None] * rates[None, :])[:, None, None, :]
    h = h / (jnp.sum(jnp.abs(h), axis=(0, 2), keepdims=True) + EPS)
    return h.transpose(1, 2, 0, 3)


def bidir_long_conv(u, h_fwd, h_bwd, skip):
    B, L, C = u.shape
    taps = jnp.concatenate([h_fwd, jnp.zeros((1, C), F32), h_bwd[:0:-1]], axis=0)
    y = jnp.fft.irfft(jnp.fft.rfft(u, n=2 * L, axis=1) * jnp.fft.rfft(taps, axis=0)[None],
                      n=2 * L, axis=1)[:, :L]
    return y + u * skip


def hyena(p, conv_w, w1, b1, w2, b2, w3, skip):
    L = p.shape[1]
    filt = hyena_filters(L, w1, b1, w2, b2, w3)
    u = short_conv(p, conv_w).astype(F32)
    parts = jnp.split(u, HY_ORDER + 1, axis=-1)
    z = parts[0]
    for n in range(HY_ORDER):
        z = parts[n + 1] * bidir_long_conv(z, filt[n, 0], filt[n, 1], skip[n].astype(F32))
    return z


def cd_mixer(p_lat, p_ctx, ret_decay_logit, ret_gn_w, hy_conv_w, hy_f_w1, hy_f_b1, hy_f_w2, hy_f_b2,
             hy_f_w3, hy_bias, need_ctx):
    c_lat, c_ctx = retention(p_lat[..., :RET_COLS], p_ctx[..., :RET_COLS], ret_decay_logit, ret_gn_w, need_ctx)
    hy = lambda p: hyena(p, hy_conv_w, hy_f_w1, hy_f_b1, hy_f_w2, hy_f_b2, hy_f_w3, hy_bias)
    o_lat = jnp.concatenate([c_lat, hy(p_lat[..., RET_COLS:])], axis=-1)
    o_ctx = jnp.concatenate([c_ctx, hy(p_ctx[..., RET_COLS:])], axis=-1) if need_ctx else None
    return o_lat, o_ctx


def _pad_cols(w, mult):
    n = w.shape[-1]
    pad = (-n) % mult
    return jnp.pad(w, ((0, 0), (0, pad))) if pad else w


def kernel(x, c, ctx, c_ctx, mod_w, mod_b, norm_mix_w, norm_ffn_w, ffn_w_in, ffn_w_out,
           ab_w_in, ab_w_out, dn_conv_w, dn_a_log, dn_dt_bias, dn_norm_w, swa_q_norm_w, swa_k_norm_w,
           swa_sink, cd_w_in, cd_w_out, ret_decay_logit, ret_gn_w, hy_conv_w, hy_f_w1, hy_f_b1,
           hy_f_w2, hy_f_b2, hy_f_w3, hy_bias):
    B, L, D = x.shape
    n_ctx = ctx.shape[1]
    rows = L // GRID_W
    ang_row, ang_col = axial_angles(rows)
    c_rows = jnp.zeros((8, D), F32).at[:B].set(c).at[B].set(c_ctx)
    mod_all = modulation(c_rows, mod_w, mod_b)
    hid = ffn_w_out.shape[1]
    h_ctx = ctx
    for layer in range(DEPTH):
        need_ctx = layer != DEPTH - 1
        i = layer // 2
        m = mod_all[layer].reshape(8, 6, D)
        mod = [m[:B, j][:, None, :] for j in range(6)]
        mod_c = [jnp.broadcast_to(m[B, j][None, None, :], (B, 1, D)) for j in range(6)]
        if layer % 2 == 0:
            w_in, w_out, cols = ab_w_in[i], ab_w_out[i], ab_w_in.shape[-1]
        else:
            w_in, w_out, cols = cd_w_in[i], cd_w_out[i], cd_w_in.shape[-1]
        w_in_p = _pad_cols(w_in, 128).astype(BF16)
        ncols = w_in_p.shape[1]
        tn = 640 if ncols % 640 == 0 else 512
        p_lat = in_projection(x, norm_mix_w[layer], mod[0], mod[1], w_in_p, 1024, tn)[..., :cols]
        p_ctx = in_projection(h_ctx, norm_mix_w[layer], mod_c[0], mod_c[1], w_in_p, n_ctx, tn)[..., :cols]
        if layer % 2 == 0:
            o_lat, o_ctx = ab_mixer(p_lat, p_ctx, ang_row, ang_col, dn_conv_w[i],
                                    dn_a_log[i], dn_dt_bias[i], dn_norm_w[i], swa_q_norm_w[i],
                                    swa_k_norm_w[i], swa_sink[i], need_ctx)
        else:
            o_lat, o_ctx = cd_mixer(p_lat, p_ctx, ret_decay_logit[i], ret_gn_w[i],
                                    hy_conv_w[i], hy_f_w1[i], hy_f_b1[i], hy_f_w2[i], hy_f_b2[i],
                                    hy_f_w3[i], hy_bias[i], need_ctx)
        wo = w_out.astype(BF16)
        wg = ffn_w_in[layer][:, :hid].astype(BF16)
        wu = ffn_w_in[layer][:, hid:].astype(BF16)
        wd = ffn_w_out[layer].astype(BF16)
        x = out_projection_ffn(x, o_lat, wo, mod[2], norm_ffn_w[layer], mod[3], mod[4], mod[5],
                               wg, wu, wd, 1024, 256)
        if need_ctx:
            h_ctx = out_projection_ffn(h_ctx, o_ctx, wo, mod_c[2], norm_ffn_w[layer], mod_c[3], mod_c[4],
                                       mod_c[5], wg, wu, wd, n_ctx, 256)
    return x
```

```python
import functools
import math

import jax
import jax.numpy as jnp
import numpy as np
from jax import lax
from jax.experimental import pallas as pl
from jax.experimental.pallas import tpu as pltpu

F32 = jnp.float32
BF16 = jnp.bfloat16
EPS = 1e-6

D_MODEL = 1024
DEPTH = 2
GRID_W = 64

DN_HEADS = 4
DN_DIM = 128
DN_WIDTH = DN_HEADS * DN_DIM
DN_CHUNK = 64
DN_COLS = 4 * DN_WIDTH + 4 * DN_HEADS
SWA_HEADS = 4
SWA_KV_HEADS = 2
SWA_DIM = 128
SWA_WINDOW = 128
SWA_BLOCK = 128
ROPE_THETA = 10000.0
RET_HEADS = 4
RET_DIM = 128
RET_WIDTH = RET_HEADS * RET_DIM
RET_CHUNK = 128
RET_COLS = 4 * RET_WIDTH
HY_CH = 512
HY_ORDER = 2
HY_BANDS = 8
HY_SIN_FREQ = 1.0
HY_TARGET = 1e-2
HY_DECAY_SHORT = 0.3
HY_DECAY_LONG = 1.5

VMEM_LIMIT_BYTES = 56 * 1024 * 1024


def _cparams(sem):
    return pltpu.CompilerParams(dimension_semantics=sem, vmem_limit_bytes=VMEM_LIMIT_BYTES)


def _mod_body(c_ref, w_ref, b_ref, o_ref):
    a = c_ref[...]
    a = a * jax.nn.sigmoid(a)
    o_ref[...] = jnp.dot(a, w_ref[...], preferred_element_type=F32,
                         precision=lax.Precision.HIGHEST) + b_ref[...]


def modulation(c_rows, mod_w, mod_b):
    depth, d, n = mod_w.shape
    tn = 1536
    return pl.pallas_call(
        _mod_body,
        grid=(depth, n // tn),
        in_specs=[pl.BlockSpec((8, d), lambda l, j: (0, 0)),
                  pl.BlockSpec((None, d, tn), lambda l, j: (l, 0, j)),
                  pl.BlockSpec((None, 1, tn), lambda l, j: (l, 0, j))],
        out_specs=pl.BlockSpec((None, 8, tn), lambda l, j: (l, 0, j)),
        out_shape=jax.ShapeDtypeStruct((depth, 8, n), F32),
        compiler_params=_cparams(("parallel", "parallel")),
        name="modulation",
    )(c_rows, mod_w, mod_b.reshape(depth, 1, n))


def _norm_mod(x, nw, shift, scale):
    y = x * lax.rsqrt(jnp.mean(x * x, axis=-1, keepdims=True) + EPS)
    return (y * nw) * (1.0 + scale) + shift


def _inproj_body(x_ref, nw_ref, shift_ref, scale_ref, w_ref, o_ref, h_scr):
    @pl.when(pl.program_id(2) == 0)
    def _():
        h_scr[...] = _norm_mod(x_ref[...], nw_ref[...], shift_ref[...], scale_ref[...]).astype(BF16)

    o_ref[...] = jnp.dot(h_scr[...], w_ref[...], preferred_element_type=F32)


def in_projection(x, nw, shift, scale, w, tm, tn):
    b, l, d = x.shape
    n = w.shape[1]
    return pl.pallas_call(
        _inproj_body,
        grid=(b, l // tm, n // tn),
        in_specs=[pl.BlockSpec((None, tm, d), lambda bi, i, j: (bi, i, 0)),
                  pl.BlockSpec((1, d), lambda bi, i, j: (0, 0)),
                  pl.BlockSpec((None, 1, d), lambda bi, i, j: (bi, 0, 0)),
                  pl.BlockSpec((None, 1, d), lambda bi, i, j: (bi, 0, 0)),
                  pl.BlockSpec((d, tn), lambda bi, i, j: (0, j))],
        out_specs=pl.BlockSpec((None, tm, tn), lambda bi, i, j: (bi, i, j)),
        out_shape=jax.ShapeDtypeStruct((b, l, n), F32),
        scratch_shapes=[pltpu.VMEM((tm, d), BF16)],
        compiler_params=_cparams(("parallel", "parallel", "arbitrary")),
        name="in_projection",
    )(x, nw.reshape(1, d), shift, scale, w)


def _outffn_body(x_ref, o_ref, wo_ref, g2_ref, nw_ref, sh_ref, sc_ref, g5_ref,
                 wg_ref, wu_ref, wd_ref, y_ref, x1_scr, h_scr, acc_scr):
    k = pl.program_id(2)

    @pl.when(k == 0)
    def _():
        mix = jnp.dot(o_ref[...].astype(BF16), wo_ref[...], preferred_element_type=F32)
        x1 = x_ref[...] + g2_ref[...] * mix
        x1_scr[...] = x1
        h_scr[...] = _norm_mod(x1, nw_ref[...], sh_ref[...], sc_ref[...]).astype(BF16)
        acc_scr[...] = jnp.zeros_like(acc_scr)

    h = h_scr[...]
    g = jnp.dot(h, wg_ref[...], preferred_element_type=F32)
    u = jnp.dot(h, wu_ref[...], preferred_element_type=F32)
    a = (g * jax.nn.sigmoid(g) * u).astype(BF16)
    acc_scr[...] += jnp.dot(a, wd_ref[...], preferred_element_type=F32)

    @pl.when(k == pl.num_programs(2) - 1)
    def _():
        y_ref[...] = x1_scr[...] + g5_ref[...] * acc_scr[...]


def out_projection_ffn(x, o, wo, g2, nw, shift, scale, g5, wg, wu, wd, tm, th):
    b, l, d = x.shape
    hid = wg.shape[1]
    tok = lambda bi, i, k: (bi, i, 0)
    vec = lambda bi, i, k: (bi, 0, 0)
    return pl.pallas_call(
        _outffn_body,
        grid=(b, l // tm, hid // th),
        in_specs=[pl.BlockSpec((None, tm, d), tok),
                  pl.BlockSpec((None, tm, d), tok),
                  pl.BlockSpec((d, d), lambda bi, i, k: (0, 0)),
                  pl.BlockSpec((None, 1, d), vec),
                  pl.BlockSpec((1, d), lambda bi, i, k: (0, 0)),
                  pl.BlockSpec((None, 1, d), vec),
                  pl.BlockSpec((None, 1, d), vec),
                  pl.BlockSpec((None, 1, d), vec),
                  pl.BlockSpec((d, th), lambda bi, i, k: (0, k)),
                  pl.BlockSpec((d, th), lambda bi, i, k: (0, k)),
                  pl.BlockSpec((th, d), lambda bi, i, k: (k, 0))],
        out_specs=pl.BlockSpec((None, tm, d), tok),
        out_shape=jax.ShapeDtypeStruct((b, l, d), F32),
        scratch_shapes=[pltpu.VMEM((tm, d), F32), pltpu.VMEM((tm, d), BF16), pltpu.VMEM((tm, d), F32)],
        compiler_params=_cparams(("parallel", "parallel", "arbitrary")),
        name="out_projection_ffn",
    )(x, o, wo, g2, nw.reshape(1, d), shift, scale, g5, wg, wu, wd)


NEG_BIG = -1e30


def _head_rms(x, w):
    return x * lax.rsqrt(jnp.mean(x * x, axis=-1, keepdims=True) + EPS) * w


def _swa_prep_body(*refs, use_rope):
    if use_rope:
        q_ref, k_ref, v_ref, qw_ref, kw_ref, cos_ref, sa_ref, sb_ref, qo_ref, ko_ref, vo_ref = refs
    else:
        q_ref, k_ref, v_ref, qw_ref, kw_ref, qo_ref, ko_ref, vo_ref = refs

    def prep(x, w):
        y = _head_rms(x, w)
        if use_rope:
            y = (y * cos_ref[...] + pltpu.roll(y, SWA_DIM - 32, 1) * sa_ref[...]
                 + pltpu.roll(y, 32, 1) * sb_ref[...])
        return y.astype(BF16)

    for h in range(SWA_HEADS):
        sl = slice(h * SWA_DIM, (h + 1) * SWA_DIM)
        qo_ref[:, sl] = prep(q_ref[:, sl], qw_ref[...])
    for h in range(SWA_KV_HEADS):
        sl = slice(h * SWA_DIM, (h + 1) * SWA_DIM)
        ko_ref[:, sl] = prep(k_ref[:, sl], kw_ref[...])
    vo_ref[...] = v_ref[...].astype(BF16)


def swa_prep(p, col0, qw, kw, rope, tl):
    b, l, _ = p.shape
    nq, nk = SWA_HEADS * SWA_DIM, SWA_KV_HEADS * SWA_DIM
    tok = lambda bi, i: (bi, i, 0)
    in_specs = [pl.BlockSpec((None, tl, nq), lambda bi, i: (bi, i, col0 // nq)),
                pl.BlockSpec((None, tl, nk), lambda bi, i: (bi, i, (col0 + nq) // nk)),
                pl.BlockSpec((None, tl, nk), lambda bi, i: (bi, i, (col0 + nq + nk) // nk)),
                pl.BlockSpec((1, SWA_DIM), lambda bi, i: (0, 0)),
                pl.BlockSpec((1, SWA_DIM), lambda bi, i: (0, 0))]
    args = [p, p, p, qw.reshape(1, SWA_DIM), kw.reshape(1, SWA_DIM)]
    if rope is not None:
        in_specs += [pl.BlockSpec((tl, SWA_DIM), lambda bi, i: (i, 0))] * 3
        args += list(rope)
    return pl.pallas_call(
        functools.partial(_swa_prep_body, use_rope=rope is not None),
        grid=(b, l // tl),
        in_specs=in_specs,
        out_specs=[pl.BlockSpec((None, tl, nq), tok), pl.BlockSpec((None, tl, nk), tok),
                   pl.BlockSpec((None, tl, nk), tok)],
        out_shape=[jax.ShapeDtypeStruct((b, l, nq), BF16), jax.ShapeDtypeStruct((b, l, nk), BF16),
                   jax.ShapeDtypeStruct((b, l, nk), BF16)],
        compiler_params=_cparams(("parallel", "parallel")),
        name="swa_prep",
    )(*args)


def _nt_dot(a, b):
    return lax.dot_general(a, b, (((1,), (1,)), ((), ())), preferred_element_type=F32)


def _swa_attend(q, keys, vals, masks, sink_col):
    scale = SWA_DIM ** -0.5
    scores = []
    for kk, mask in zip(keys, masks):
        s = _nt_dot(q, kk) * scale
        if mask is not None:
            s = jnp.where(mask, s, NEG_BIG)
        scores.append(s)
    m = sink_col
    for s in scores:
        m = jnp.maximum(m, jnp.max(s, axis=-1, keepdims=True))
    den = jnp.exp(sink_col - m)
    acc = None
    for s, vv in zip(scores, vals):
        pr = jnp.exp(s - m)
        den = den + jnp.sum(pr, axis=-1, keepdims=True)
        o = jnp.dot(pr.astype(BF16), vv, preferred_element_type=F32)
        acc = o if acc is None else acc + o
    return acc / den


def _swa_attn_body(sink_ref, q_ref, kp_ref, km_ref, kn_ref, vp_ref, vm_ref, vn_ref, kc_ref, vc_ref,
                   o_ref, *, nb):
    i = pl.program_id(1)
    n_blocks = pl.num_programs(1) * nb
    w, d = SWA_BLOCK, SWA_DIM
    grp = SWA_HEADS // SWA_KV_HEADS
    rows = lax.broadcasted_iota(jnp.int32, (grp * w, w), 0)
    qi = rows % w
    kj = lax.broadcasted_iota(jnp.int32, (grp * w, w), 1)
    for r in range(nb):
        blk = i * nb + r
        mask_prev = (kj >= qi) & (blk > 0)
        mask_next = (kj <= qi) & (blk < n_blocks - 1)
        rs = slice(r * w, (r + 1) * w)
        for h in range(SWA_KV_HEADS):
            hs = slice(h * d, (h + 1) * d)
            q = jnp.concatenate([q_ref[rs, (h * grp + g) * d:(h * grp + g + 1) * d] for g in range(grp)], axis=0)
            sink_col = jnp.zeros((grp * w, 1), F32)
            for g in range(grp):
                sink_col = jnp.where((rows[:, :1] // w) == g, sink_ref[h * grp + g], sink_col)
            if r > 0:
                k_prev, v_prev = km_ref[(r - 1) * w:r * w, hs], vm_ref[(r - 1) * w:r * w, hs]
            else:
                k_prev, v_prev = kp_ref[:, hs], vp_ref[:, hs]
            if r < nb - 1:
                k_next, v_next = km_ref[(r + 1) * w:(r + 2) * w, hs], vm_ref[(r + 1) * w:(r + 2) * w, hs]
            else:
                k_next, v_next = kn_ref[:, hs], vn_ref[:, hs]
            o = _swa_attend(q, [k_prev, km_ref[rs, hs], k_next, kc_ref[:, hs]],
                            [v_prev, vm_ref[rs, hs], v_next, vc_ref[:, hs]],
                            [mask_prev, None, mask_next, None], sink_col)
            for g in range(grp):
                o_ref[rs, (h * grp + g) * d:(h * grp + g + 1) * d] = o[g * w:(g + 1) * w].astype(o_ref.dtype)


def swa_attention(q, k, v, kc, vc, sink, nb):
    b, l, nq = q.shape
    nk = k.shape[-1]
    n_ctx = kc.shape[1]
    w = SWA_BLOCK
    tq = nb * w
    last = l // w - 1
    main = lambda bi, i, s: (bi, i, 0)
    prev = lambda bi, i, s: (bi, jnp.maximum(i * nb - 1, 0), 0)
    nxt = lambda bi, i, s: (bi, jnp.minimum((i + 1) * nb, last), 0)
    cx = lambda bi, i, s: (bi, 0, 0)
    return pl.pallas_call(
        functools.partial(_swa_attn_body, nb=nb),
        grid_spec=pltpu.PrefetchScalarGridSpec(
            num_scalar_prefetch=1,
            grid=(b, l // tq),
            in_specs=[pl.BlockSpec((None, tq, nq), main),
                      pl.BlockSpec((None, w, nk), prev), pl.BlockSpec((None, tq, nk), main),
                      pl.BlockSpec((None, w, nk), nxt),
                      pl.BlockSpec((None, w, nk), prev), pl.BlockSpec((None, tq, nk), main),
                      pl.BlockSpec((None, w, nk), nxt),
                      pl.BlockSpec((None, n_ctx, nk), cx), pl.BlockSpec((None, n_ctx, nk), cx)],
            out_specs=pl.BlockSpec((None, tq, nq), main)),
        out_shape=jax.ShapeDtypeStruct((b, l, nq), BF16),
        compiler_params=_cparams(("parallel", "parallel")),
        name="swa_attention",
    )(sink, q, k, k, k, v, v, v, kc, vc)


def _ctx_attn_body(sink_ref, q_ref, kc_ref, vc_ref, o_ref):
    n_ctx, d = q_ref.shape[0], SWA_DIM
    grp = SWA_HEADS // SWA_KV_HEADS
    rows = lax.broadcasted_iota(jnp.int32, (grp * n_ctx, 1), 0)
    for h in range(SWA_KV_HEADS):
        hs = slice(h * d, (h + 1) * d)
        q = jnp.concatenate([q_ref[:, (h * grp + g) * d:(h * grp + g + 1) * d] for g in range(grp)], axis=0)
        sink_col = jnp.zeros((grp * n_ctx, 1), F32)
        for g in range(grp):
            sink_col = jnp.where((rows // n_ctx) == g, sink_ref[h * grp + g], sink_col)
        o = _swa_attend(q, [kc_ref[:, hs]], [vc_ref[:, hs]], [None], sink_col)
        for g in range(grp):
            o_ref[:, (h * grp + g) * d:(h * grp + g + 1) * d] = o[g * n_ctx:(g + 1) * n_ctx].astype(o_ref.dtype)


def ctx_attention(qc, kc, vc, sink):
    b, n_ctx, nq = qc.shape
    nk = kc.shape[-1]
    cx = lambda bi, s: (bi, 0, 0)
    return pl.pallas_call(
        _ctx_attn_body,
        grid_spec=pltpu.PrefetchScalarGridSpec(
            num_scalar_prefetch=1, grid=(b,),
            in_specs=[pl.BlockSpec((None, n_ctx, nq), cx), pl.BlockSpec((None, n_ctx, nk), cx),
                      pl.BlockSpec((None, n_ctx, nk), cx)],
            out_specs=pl.BlockSpec((None, n_ctx, nq), cx)),
        out_shape=jax.ShapeDtypeStruct((b, n_ctx, nq), BF16),
        compiler_params=_cparams(("parallel",)),
        name="ctx_attention",
    )(sink, qc, kc, vc)


def axial_rope_tables(rows):
    ang_row, ang_col = axial_angles(rows)
    zero = jnp.zeros_like(ang_row)
    cos = jnp.concatenate([jnp.cos(ang_row)] * 2 + [jnp.cos(ang_col)] * 2, axis=-1)
    sa = jnp.concatenate([-jnp.sin(ang_row), zero, -jnp.sin(ang_col), zero], axis=-1)
    sb = jnp.concatenate([zero, jnp.sin(ang_row), zero, jnp.sin(ang_col)], axis=-1)
    return cos, sa, sb


def window_gqa_pallas(p_lat, p_ctx, col0, rope, q_norm_w, k_norm_w, sink, need_ctx):
    ql, kl, vl = swa_prep(p_lat, col0, q_norm_w, k_norm_w, rope, 1024)
    qc, kc, vc = swa_prep(p_ctx, col0, q_norm_w, k_norm_w, None, p_ctx.shape[1])
    o_lat = swa_attention(ql, kl, vl, kc, vc, sink, 4)
    o_ctx = ctx_attention(qc, kc, vc, sink) if need_ctx else None
    return o_lat, o_ctx


def _tn_dot(a, b):
    return lax.dot_general(a, b, (((0,), (0,)), ((), ())), preferred_element_type=F32)


def _log_sigmoid(x):
    return jnp.minimum(x, 0.0) - jnp.log(1.0 + jnp.exp(-jnp.abs(x)))


def _ret_scan_body(logit_ref, q_ref, k_ref, v_ref, cos_ref, sin_ref, s0_ref, o_ref, sf_ref, s_scr):
    h, d, n = pl.program_id(1), pl.program_id(2), pl.program_id(3)
    c = RET_CHUNK

    @pl.when(n == 0)
    def _():
        s_scr[...] = s0_ref[...]

    lg = _log_sigmoid(jnp.full((1, 1), logit_ref[d * RET_HEADS + h], F32))
    fwd = d == 0

    def rot(x):
        return x * cos_ref[...] + pltpu.roll(x, RET_DIM // 2, 1) * sin_ref[...]

    q = rot(q_ref[...])
    k = rot(k_ref[...]) * RET_DIM ** -0.5
    v = v_ref[...].astype(BF16)
    ii = lax.broadcasted_iota(jnp.int32, (c, c), 0)
    jj = lax.broadcasted_iota(jnp.int32, (c, c), 1)
    rel = jnp.where(fwd, ii - jj, jj - ii)
    incl = rel >= 0
    dmat = jnp.where(incl, jnp.exp(lg * jnp.where(incl, rel, 0).astype(F32)), 0.0)
    i1 = lax.broadcasted_iota(jnp.int32, (c, 1), 0)
    pos = jnp.where(fwd, i1, c - 1 - i1).astype(F32)
    scores = _nt_dot(q.astype(BF16), k.astype(BF16)) * dmat
    o_in = jnp.dot(scores.astype(BF16), v, preferred_element_type=F32)
    q_dec = (q * jnp.exp(lg * (pos + 1.0))).astype(BF16)
    k_dec = (k * jnp.exp(lg * (c - 1.0 - pos))).astype(BF16)
    s = s_scr[...]
    o_ref[...] = o_in + jnp.dot(q_dec, s.astype(BF16), preferred_element_type=F32)
    s_new = s * jnp.exp(lg * c) + _tn_dot(k_dec, v)
    s_scr[...] = s_new

    @pl.when(n == pl.num_programs(3) - 1)
    def _():
        sf_ref[...] = s_new


def retention_scan(p, decay_logit, cos, sin, s0):
    b, l, _ = p.shape
    c, dh, nh = RET_CHUNK, RET_DIM, RET_HEADS
    n = l // c
    chunk = lambda ni, di: jnp.where(di == 0, ni, n - 1 - ni)
    col = lambda off: (lambda bi, hi, di, ni, s: (bi, chunk(ni, di), off + hi))
    tab = lambda bi, hi, di, ni, s: (chunk(ni, di), 0)
    st = lambda bi, hi, di, ni, s: (bi, hi, di, 0, 0)
    return pl.pallas_call(
        _ret_scan_body,
        grid_spec=pltpu.PrefetchScalarGridSpec(
            num_scalar_prefetch=1,
            grid=(b, nh, 2, n),
            in_specs=[pl.BlockSpec((None, c, dh), col(0)), pl.BlockSpec((None, c, dh), col(nh)),
                      pl.BlockSpec((None, c, dh), col(2 * nh)),
                      pl.BlockSpec((c, dh), tab), pl.BlockSpec((c, dh), tab),
                      pl.BlockSpec((None, None, None, dh, dh), st)],
            out_specs=[pl.BlockSpec((None, None, c, dh), lambda bi, hi, di, ni, s: (di, bi, chunk(ni, di), hi)),
                       pl.BlockSpec((None, None, None, dh, dh), st)],
            scratch_shapes=[pltpu.VMEM((dh, dh), F32)]),
        out_shape=[jax.ShapeDtypeStruct((2, b, l, nh * dh), F32),
                   jax.ShapeDtypeStruct((b, nh, 2, dh, dh), F32)],
        compiler_params=_cparams(("parallel", "parallel", "arbitrary", "arbitrary")),
        name="retention_scan",
    )(decay_logit.reshape(-1), p, p, p, cos, sin, s0)


def _ret_final_body(of_ref, ob_ref, g_ref, w_ref, y_ref):
    for h in range(RET_HEADS):
        sl = slice(h * RET_DIM, (h + 1) * RET_DIM)
        o = of_ref[:, sl] + ob_ref[:, sl]
        mu = jnp.mean(o, axis=-1, keepdims=True)
        var = jnp.mean(jnp.square(o - mu), axis=-1, keepdims=True)
        y = (o - mu) * lax.rsqrt(var + EPS) * w_ref[:, sl]
        g = g_ref[:, sl]
        y_ref[:, sl] = (y * (g * jax.nn.sigmoid(g))).astype(y_ref.dtype)


def retention_finalize(o2, p, gn_w, tl):
    _, b, l, w = o2.shape
    tok = lambda bi, i: (bi, i, 0)
    return pl.pallas_call(
        _ret_final_body,
        grid=(b, l // tl),
        in_specs=[pl.BlockSpec((None, None, tl, w), lambda bi, i: (0, bi, i, 0)),
                  pl.BlockSpec((None, None, tl, w), lambda bi, i: (1, bi, i, 0)),
                  pl.BlockSpec((None, tl, w), lambda bi, i: (bi, i, 3)),
                  pl.BlockSpec((1, w), lambda bi, i: (0, 0))],
        out_specs=pl.BlockSpec((None, tl, w), tok),
        out_shape=jax.ShapeDtypeStruct((b, l, w), BF16),
        compiler_params=_cparams(("parallel", "parallel")),
        name="retention_finalize",
    )(o2, o2, p, gn_w.reshape(1, w))


def retention_rope_tables(l):
    inv = ROPE_THETA ** (-jnp.linspace(0.0, 1.0, RET_DIM // 2, dtype=F32))
    ang = jnp.arange(l, dtype=F32)[:, None] * inv
    return (jnp.concatenate([jnp.cos(ang), jnp.cos(ang)], axis=-1),
            jnp.concatenate([-jnp.sin(ang), jnp.sin(ang)], axis=-1))


def retention_pallas(p_lat, p_ctx, decay_logit, gn_w, need_ctx):
    b, l, _ = p_lat.shape
    n_ctx = p_ctx.shape[1]
    s0 = jnp.zeros((b, RET_HEADS, 2, RET_DIM, RET_DIM), F32)
    o_ctx, s_ctx = retention_scan(p_ctx, decay_logit, *retention_rope_tables(n_ctx), s0)
    o_lat, _ = retention_scan(p_lat, decay_logit, *retention_rope_tables(l), s_ctx)
    out_lat = retention_finalize(o_lat, p_lat, gn_w, 1024)
    out_ctx = retention_finalize(o_ctx, p_ctx, gn_w, n_ctx) if need_ctx else None
    return out_lat, out_ctx


def rmsnorm(x, w):
    xf = x.astype(F32)
    y = xf * lax.rsqrt(jnp.mean(xf * xf, axis=-1, keepdims=True) + EPS)
    return (y * w.astype(F32)).astype(x.dtype)


def l2norm(x):
    return x * lax.rsqrt(jnp.sum(x * x, axis=-1, keepdims=True) + EPS)


def rotate_half(x, ang):
    cos = jnp.cos(ang)[:, None, :]
    sin = jnp.sin(ang)[:, None, :]
    x1, x2 = jnp.split(x, 2, axis=-1)
    return jnp.concatenate([x1 * cos - x2 * sin, x2 * cos + x1 * sin], axis=-1).astype(x.dtype)


def axial_angles(rows):
    m = SWA_DIM // 4
    inv = ROPE_THETA ** (-jnp.arange(m, dtype=F32) / m)
    row = jnp.repeat(jnp.arange(rows, dtype=F32), GRID_W)
    col = jnp.broadcast_to(jnp.arange(GRID_W, dtype=F32), (rows, GRID_W)).reshape(-1)
    return row[:, None] * inv, col[:, None] * inv


def axial_rope(x, ang_row, ang_col):
    h = x.shape[-1] // 2
    return jnp.concatenate([rotate_half(x[..., :h], ang_row), rotate_half(x[..., h:], ang_col)], axis=-1)


def short_conv(x, w):
    K, C = w.shape
    return lax.conv_general_dilated(x, w[:, None, :].astype(x.dtype), window_strides=(1,),
                                    padding=[(K // 2, K // 2)],
                                    dimension_numbers=('NWC', 'WIO', 'NWC'),
                                    feature_group_count=C)


def flip_time(t):
    return jnp.flip(t, axis=2)


def keep_time(t):
    return t


def delta_chunk_scan(q, k, v, g, beta, s0):
    B, H, L, dk = q.shape
    dv = v.shape[-1]
    C = DN_CHUNK
    N = L // C
    q = q.reshape(B, H, N, C, dk)
    k = k.reshape(B, H, N, C, dk)
    v = v.reshape(B, H, N, C, dv)
    g = g.reshape(B, H, N, C)
    beta = beta.reshape(B, H, N, C)
    gam = jnp.cumsum(g, axis=-1)
    idx = jnp.arange(C)
    incl = idx[:, None] >= idx[None, :]
    strict = idx[:, None] > idx[None, :]
    e = jnp.exp(jnp.where(incl, gam[..., :, None] - gam[..., None, :], 0.0))
    dec_incl = jnp.where(incl, e, 0.0)
    dec_strict = jnp.where(strict, e, 0.0)
    kb = k * beta[..., None]
    m = jnp.einsum('bhnid,bhnjd->bhnij', kb, k) * dec_strict
    a = m + jnp.eye(C, dtype=F32)
    rhs = jnp.concatenate([v * beta[..., None], kb * jnp.exp(gam)[..., None]], axis=-1)
    sol = lax.linalg.triangular_solve(a, rhs, left_side=True, lower=True, unit_diagonal=True)
    u, w = sol[..., :dv], sol[..., dv:]
    attn = jnp.einsum('bhnid,bhnjd->bhnij', q, k) * dec_incl
    q_dec = q * jnp.exp(gam)[..., None]
    k_dec = k * jnp.exp(gam[..., -1:] - gam)[..., None]
    chunk_dec = jnp.exp(gam[..., -1])

    def step(S, xs):
        u_n, w_n, a_n, qd_n, kd_n, cd_n = xs
        v_new = u_n - jnp.einsum('bhck,bhkv->bhcv', w_n, S)
        o = jnp.einsum('bhck,bhkv->bhcv', qd_n, S) + jnp.einsum('bhij,bhjv->bhiv', a_n, v_new)
        S = S * cd_n[..., None, None] + jnp.einsum('bhck,bhcv->bhkv', kd_n, v_new)
        return S, o

    xs = tuple(jnp.moveaxis(t, 2, 0) for t in (u, w, attn, q_dec, k_dec, chunk_dec))
    S, o = lax.scan(step, s0, xs)
    return jnp.moveaxis(o, 0, 2).reshape(B, H, L, dv), S


def dn_features(p, conv_w, a_log, dt_bias):
    B, L, _ = p.shape
    qkv = jax.nn.silu(short_conv(p[..., :3 * DN_WIDTH], conv_w)).astype(F32)
    heads = lambda t: t.reshape(B, L, DN_HEADS, DN_DIM).transpose(0, 2, 1, 3)
    q, k, v = (heads(t) for t in jnp.split(qkv, 3, axis=-1))
    q = l2norm(q) * DN_DIM ** -0.5
    k = l2norm(k)
    gate = p[..., 3 * DN_WIDTH:4 * DN_WIDTH]
    ab = p[..., 4 * DN_WIDTH:].astype(F32).reshape(B, L, 2, 2, DN_HEADS)
    g = -jnp.exp(a_log.astype(F32)) * jax.nn.softplus(ab[:, :, 0] + dt_bias.astype(F32))
    beta = jax.nn.sigmoid(ab[:, :, 1])
    return q, k, v, g.transpose(2, 0, 3, 1), beta.transpose(2, 0, 3, 1), gate


def head_norm_gate(o, gate, w):
    B, H, L, dv = o.shape
    y = rmsnorm(o.transpose(0, 2, 1, 3), w)
    return (y * jax.nn.silu(gate.astype(F32).reshape(B, L, H, dv))).reshape(B, L, H * dv)


def gated_deltanet(p_lat, p_ctx, conv_w, a_log, dt_bias, norm_w, need_ctx):
    lat = dn_features(p_lat, conv_w, a_log, dt_bias)
    cx = dn_features(p_ctx, conv_w, a_log, dt_bias)
    B = p_lat.shape[0]
    o_lat, o_ctx = [], []
    for d in range(2):
        fl = flip_time if d == 1 else keep_time
        s0 = jnp.zeros((B, DN_HEADS, DN_DIM, DN_DIM), F32)
        oc, s_ctx = delta_chunk_scan(fl(cx[0]), fl(cx[1]), fl(cx[2]), fl(cx[3][d]), fl(cx[4][d]), s0)
        ol, _ = delta_chunk_scan(fl(lat[0]), fl(lat[1]), fl(lat[2]), fl(lat[3][d]), fl(lat[4][d]), s_ctx)
        o_lat.append(fl(ol))
        o_ctx.append(fl(oc))
    out_lat = head_norm_gate(o_lat[0] + o_lat[1], lat[5], norm_w)
    out_ctx = head_norm_gate(o_ctx[0] + o_ctx[1], cx[5], norm_w) if need_ctx else None
    return out_lat, out_ctx


def gqa_qkv(p, q_norm_w, k_norm_w):
    B, L, _ = p.shape
    nq, nk = SWA_HEADS * SWA_DIM, SWA_KV_HEADS * SWA_DIM
    q = rmsnorm(p[..., :nq].reshape(B, L, SWA_HEADS, SWA_DIM), q_norm_w)
    k = rmsnorm(p[..., nq:nq + nk].reshape(B, L, SWA_KV_HEADS, SWA_DIM), k_norm_w)
    v = p[..., nq + nk:].reshape(B, L, SWA_KV_HEADS, SWA_DIM)
    return q, k, v


def banded_attention(q, k, v, kc, vc, sink):
    B, L, Hq, d = q.shape
    Hkv = k.shape[2]
    G = Hq // Hkv
    W = SWA_BLOCK
    NB = L // W
    qb = q.reshape(B, NB, W, Hkv, G, d)

    def band(t):
        tp = jnp.pad(t.reshape(B, NB, W, Hkv, d), ((0, 0), (1, 1), (0, 0), (0, 0), (0, 0)))
        return jnp.concatenate([tp[:, :-2], tp[:, 1:-1], tp[:, 2:]], axis=2)

    kb, vb = band(k), band(v)
    scale = d ** -0.5
    s_loc = jnp.einsum('bnqhgd,bnkhd->bnhgqk', qb, kb).astype(F32) * scale
    s_ctx = jnp.einsum('bnqhgd,bchd->bnhgqc', qb, kc).astype(F32) * scale
    rel = (jnp.arange(3 * W) - W)[None, :] - jnp.arange(W)[:, None]
    kblk = jnp.arange(NB)[:, None] + jnp.arange(3 * W)[None, :] // W - 1
    valid = (jnp.abs(rel) <= SWA_WINDOW)[None] & ((kblk >= 0) & (kblk < NB))[:, None, :]
    s_loc = jnp.where(valid[None, :, None, None], s_loc, -jnp.inf)
    s_sink = jnp.broadcast_to(sink.astype(F32).reshape(Hkv, G, 1, 1), s_loc.shape[:-1] + (1,))
    prob = jax.nn.softmax(jnp.concatenate([s_loc, s_ctx, s_sink], axis=-1), axis=-1).astype(v.dtype)
    n_ctx = kc.shape[1]
    o = (jnp.einsum('bnhgqk,bnkhd->bnqhgd', prob[..., :3 * W], vb)
         + jnp.einsum('bnhgqc,bchd->bnqhgd', prob[..., 3 * W:3 * W + n_ctx], vc))
    return o.reshape(B, L, Hq * d)


def context_attention(qc, kc, vc, sink):
    B, Cn, Hq, d = qc.shape
    Hkv = kc.shape[2]
    G = Hq // Hkv
    q = qc.reshape(B, Cn, Hkv, G, d)
    s = jnp.einsum('bqhgd,bkhd->bhgqk', q, kc).astype(F32) * d ** -0.5
    s_sink = jnp.broadcast_to(sink.astype(F32).reshape(Hkv, G, 1, 1), s.shape[:-1] + (1,))
    prob = jax.nn.softmax(jnp.concatenate([s, s_sink], axis=-1), axis=-1)[..., :-1].astype(vc.dtype)
    return jnp.einsum('bhgqk,bkhd->bqhgd', prob, vc).reshape(B, Cn, Hq * d)


def window_gqa(p_lat, p_ctx, ang_row, ang_col, q_norm_w, k_norm_w, sink, need_ctx):
    ql, kl, vl = gqa_qkv(p_lat, q_norm_w, k_norm_w)
    ql = axial_rope(ql, ang_row, ang_col)
    kl = axial_rope(kl, ang_row, ang_col)
    qc, kc, vc = gqa_qkv(p_ctx, q_norm_w, k_norm_w)
    o_lat = banded_attention(ql, kl, vl, kc, vc, sink)
    o_ctx = context_attention(qc, kc, vc, sink) if need_ctx else None
    return o_lat, o_ctx


def ab_mixer(p_lat, p_ctx, ang_row, ang_col, dn_conv_w, dn_a_log, dn_dt_bias, dn_norm_w,
             q_norm_w, k_norm_w, sink, need_ctx):
    a_lat, a_ctx = gated_deltanet(p_lat[..., :DN_COLS], p_ctx[..., :DN_COLS], dn_conv_w, dn_a_log,
                                  dn_dt_bias, dn_norm_w, need_ctx)
    b_lat, b_ctx = window_gqa(p_lat[..., DN_COLS:], p_ctx[..., DN_COLS:], ang_row, ang_col,
                              q_norm_w, k_norm_w, sink, need_ctx)
    o_lat = jnp.concatenate([a_lat, b_lat.astype(F32)], axis=-1)
    o_ctx = jnp.concatenate([a_ctx, b_ctx.astype(F32)], axis=-1) if need_ctx else None
    return o_lat, o_ctx


def retention_chunk_scan(q, k, v, log_gamma, s0):
    B, H, L, dk = q.shape
    dv = v.shape[-1]
    C = RET_CHUNK
    N = L // C
    q = q.reshape(B, H, N, C, dk)
    k = k.reshape(B, H, N, C, dk)
    v = v.reshape(B, H, N, C, dv)
    pos = jnp.arange(C, dtype=F32)
    rel = pos[:, None] - pos[None, :]
    incl = rel >= 0
    lg = log_gamma[:, None, None]
    dmat = jnp.where(incl, jnp.exp(lg * jnp.where(incl, rel, 0.0)), 0.0)
    scores = jnp.einsum('bhnid,bhnjd->bhnij', q, k) * dmat[:, None]
    o_in = jnp.einsum('bhnij,bhnje->bhnie', scores, v)
    q_dec = q * jnp.exp(log_gamma[:, None] * (pos + 1.0))[:, None, :, None]
    k_dec = k * jnp.exp(log_gamma[:, None] * (C - 1.0 - pos))[:, None, :, None]
    chunk_dec = jnp.exp(log_gamma * C)[None, :, None, None]

    def step(S, xs):
        qd, kd, vn = xs
        o = jnp.einsum('bhcd,bhde->bhce', qd, S)
        S = S * chunk_dec + jnp.einsum('bhcd,bhce->bhde', kd, vn)
        return S, o

    xs = tuple(jnp.moveaxis(t, 2, 0) for t in (q_dec, k_dec, v))
    S, o_x = lax.scan(step, s0, xs)
    o = o_in + jnp.moveaxis(o_x, 0, 2)
    return o.reshape(B, H, L, dv), S


def ret_features(p):
    B, L, _ = p.shape
    q, k, v, g = jnp.split(p, 4, axis=-1)
    inv = ROPE_THETA ** (-jnp.linspace(0.0, 1.0, RET_DIM // 2, dtype=F32))
    ang = jnp.arange(L, dtype=F32)[:, None] * inv
    heads = lambda t: t.reshape(B, L, RET_HEADS, RET_DIM).astype(F32)
    q = rotate_half(heads(q), ang)
    k = rotate_half(heads(k), ang) * RET_DIM ** -0.5
    tr = lambda t: t.transpose(0, 2, 1, 3)
    return tr(q), tr(k), tr(heads(v)), g


def head_groupnorm_gate(o, gate, w):
    B, H, L, dv = o.shape
    o = o.transpose(0, 2, 1, 3)
    mu = jnp.mean(o, axis=-1, keepdims=True)
    var = jnp.mean(jnp.square(o - mu), axis=-1, keepdims=True)
    y = (o - mu) * lax.rsqrt(var + EPS) * w.astype(F32).reshape(H, dv)
    return (y * jax.nn.silu(gate.astype(F32).reshape(B, L, H, dv))).reshape(B, L, H * dv)


def retention(p_lat, p_ctx, decay_logit, gn_w, need_ctx):
    lat, cx = ret_features(p_lat), ret_features(p_ctx)
    log_gamma = jax.nn.log_sigmoid(decay_logit.astype(F32))
    B = p_lat.shape[0]
    o_lat, o_ctx = [], []
    for d in range(2):
        fl = flip_time if d == 1 else keep_time
        s0 = jnp.zeros((B, RET_HEADS, RET_DIM, RET_DIM), F32)
        oc, s_ctx = retention_chunk_scan(fl(cx[0]), fl(cx[1]), fl(cx[2]), log_gamma[d], s0)
        ol, _ = retention_chunk_scan(fl(lat[0]), fl(lat[1]), fl(lat[2]), log_gamma[d], s_ctx)
        o_lat.append(fl(ol))
        o_ctx.append(fl(oc))
    out_lat = head_groupnorm_gate(o_lat[0] + o_lat[1], lat[3], gn_w)
    out_ctx = head_groupnorm_gate(o_ctx[0] + o_ctx[1], cx[3], gn_w) if need_ctx else None
    return out_lat, out_ctx


def hyena_filters(L, w1, b1, w2, b2, w3):
    pos = jnp.arange(L, dtype=F32)
    t = pos / max(L - 1, 1)
    bands = jnp.linspace(1e-4, HY_BANDS - 1, HY_BANDS, dtype=F32)
    phase = (2.0 * math.pi / L) * pos[:, None] * bands[None, :]
    z = jnp.concatenate([t[:, None], jnp.cos(phase), -jnp.sin(phase)], axis=-1)
    h = jnp.sin(HY_SIN_FREQ * (z @ w1.astype(F32) + b1.astype(F32)))
    h = jnp.sin(HY_SIN_FREQ * (h @ w2.astype(F32) + b2.astype(F32)))
    h = (h @ w3.astype(F32)).reshape(L, HY_ORDER, 2, HY_CH)
    rates = jnp.abs(jnp.linspace(math.log(HY_TARGET) / HY_DECAY_LONG, math.log(HY_TARGET) / HY_DECAY_SHORT,
                                 HY_CH, dtype=F32))
    h = h * jnp.exp(-t[:, None] * rates[None, :])[:, None, None, :]
    h = h / (jnp.sum(jnp.abs(h), axis=(0, 2), keepdims=True) + EPS)
    return h.transpose(1, 2, 0, 3)


def bidir_long_conv(u, h_fwd, h_bwd, skip):
    B, L, C = u.shape
    taps = jnp.concatenate([h_fwd, jnp.zeros((1, C), F32), h_bwd[:0:-1]], axis=0)
    y = jnp.fft.irfft(jnp.fft.rfft(u, n=2 * L, axis=1) * jnp.fft.rfft(taps, axis=0)[None],
                      n=2 * L, axis=1)[:, :L]
    return y + u * skip


def hyena(p, conv_w, w1, b1, w2, b2, w3, skip):
    L = p.shape[1]
    filt = hyena_filters(L, w1, b1, w2, b2, w3)
    u = short_conv(p, conv_w).astype(F32)
    parts = jnp.split(u, HY_ORDER + 1, axis=-1)
    z = parts[0]
    for n in range(HY_ORDER):
        z = parts[n + 1] * bidir_long_conv(z, filt[n, 0], filt[n, 1], skip[n].astype(F32))
    return z


def cd_mixer(p_lat, p_ctx, ret_decay_logit, ret_gn_w, hy_conv_w, hy_f_w1, hy_f_b1, hy_f_w2, hy_f_b2,
             hy_f_w3, hy_bias, need_ctx):
    c_lat, c_ctx = retention(p_lat[..., :RET_COLS], p_ctx[..., :RET_COLS], ret_decay_logit, ret_gn_w, need_ctx)
    hy = lambda p: hyena(p, hy_conv_w, hy_f_w1, hy_f_b1, hy_f_w2, hy_f_b2, hy_f_w3, hy_bias)
    o_lat = jnp.concatenate([c_lat, hy(p_lat[..., RET_COLS:])], axis=-1)
    o_ctx = jnp.concatenate([c_ctx, hy(p_ctx[..., RET_COLS:])], axis=-1) if need_ctx else None
    return o_lat, o_ctx


def _pad_cols(w, mult):
    n = w.shape[-1]
    pad = (-n) % mult
    return jnp.pad(w, ((0, 0), (0, pad))) if pad else w


def kernel(x, c, ctx, c_ctx, mod_w, mod_b, norm_mix_w, norm_ffn_w, ffn_w_in, ffn_w_out,
           ab_w_in, ab_w_out, dn_conv_w, dn_a_log, dn_dt_bias, dn_norm_w, swa_q_norm_w, swa_k_norm_w,
           swa_sink, cd_w_in, cd_w_out, ret_decay_logit, ret_gn_w, hy_conv_w, hy_f_w1, hy_f_b1,
           hy_f_w2, hy_f_b2, hy_f_w3, hy_bias):
    B, L, D = x.shape
    n_ctx = ctx.shape[1]
    rows = L // GRID_W
    ang_row, ang_col = axial_angles(rows)
    c_rows = jnp.zeros((8, D), F32).at[:B].set(c).at[B].set(c_ctx)
    mod_all = modulation(c_rows, mod_w, mod_b)
    hid = ffn_w_out.shape[1]
    h_ctx = ctx
    for layer in range(DEPTH):
        need_ctx = layer != DEPTH - 1
        i = layer // 2
        m = mod_all[layer].reshape(8, 6, D)
        mod = [m[:B, j][:, None, :] for j in range(6)]
        mod_c = [jnp.broadcast_to(m[B, j][None, None, :], (B, 1, D)) for j in range(6)]
        if layer % 2 == 0:
            w = ab_w_in[i]
            w_in = jnp.concatenate([w[:, :4 * DN_WIDTH], w[:, DN_COLS:], w[:, 4 * DN_WIDTH:DN_COLS]], axis=1)
            w_out, cols = ab_w_out[i], ab_w_in.shape[-1]
        else:
            w_in, w_out, cols = cd_w_in[i], cd_w_out[i], cd_w_in.shape[-1]
        w_in_p = _pad_cols(w_in, 128).astype(BF16)
        ncols = w_in_p.shape[1]
        tn = 640 if ncols % 640 == 0 else 512
        p_lat = in_projection(x, norm_mix_w[layer], mod[0], mod[1], w_in_p, 1024, tn)
        p_ctx = in_projection(h_ctx, norm_mix_w[layer], mod_c[0], mod_c[1], w_in_p, n_ctx, tn)
        if layer % 2 == 0:
            swa0 = 4 * DN_WIDTH
            ab0 = swa0 + (SWA_HEADS + 2 * SWA_KV_HEADS) * SWA_DIM
            dn_cols = lambda p: jnp.concatenate([p[..., :swa0], p[..., ab0:ab0 + 4 * DN_HEADS]], axis=-1)
            a_lat, a_ctx = gated_deltanet(dn_cols(p_lat), dn_cols(p_ctx), dn_conv_w[i], dn_a_log[i],
                                          dn_dt_bias[i], dn_norm_w[i], need_ctx)
            b_lat, b_ctx = window_gqa_pallas(p_lat, p_ctx, swa0, axial_rope_tables(rows), swa_q_norm_w[i],
                                             swa_k_norm_w[i], swa_sink[i], need_ctx)
            o_lat = jnp.concatenate([a_lat, b_lat.astype(F32)], axis=-1)
            o_ctx = jnp.concatenate([a_ctx, b_ctx.astype(F32)], axis=-1) if need_ctx else None
        else:
            c_lat, c_ctx = retention_pallas(p_lat, p_ctx, ret_decay_logit[i], ret_gn_w[i], need_ctx)
            hy = lambda p: hyena(p[..., RET_COLS:cols], hy_conv_w[i], hy_f_w1[i], hy_f_b1[i], hy_f_w2[i],
                                 hy_f_b2[i], hy_f_w3[i], hy_bias[i])
            o_lat = jnp.concatenate([c_lat.astype(F32), hy(p_lat)], axis=-1)
            o_ctx = jnp.concatenate([c_ctx.astype(F32), hy(p_ctx)], axis=-1) if need_ctx else None
        wo = w_out.astype(BF16)
        wg = ffn_w_in[layer][:, :hid].astype(BF16)
        wu = ffn_w_in[layer][:, hid:].astype(BF16)
        wd = ffn_w_out[layer].astype(BF16)
        x = out_projection_ffn(x, o_lat, wo, mod[2], norm_ffn_w[layer], mod[3], mod[4], mod[5],
                               wg, wu, wd, 1024, 256)
        if need_ctx:
            h_ctx = out_projection_ffn(h_ctx, o_ctx, wo, mod_c[2], norm_ffn_w[layer], mod_c[3], mod_c[4],
                                       mod_c[5], wg, wu, wd, n_ctx, 256)
    return x
```

```python
import functools
import math

import jax
import jax.numpy as jnp
import numpy as np
from jax import lax
from jax.experimental import pallas as pl
from jax.experimental.pallas import tpu as pltpu

F32 = jnp.float32
BF16 = jnp.bfloat16
EPS = 1e-6

D_MODEL = 1024
DEPTH = 2
GRID_W = 64

DN_HEADS = 4
DN_DIM = 128
DN_WIDTH = DN_HEADS * DN_DIM
DN_CHUNK = 64
DN_COLS = 4 * DN_WIDTH + 4 * DN_HEADS
SWA_HEADS = 4
SWA_KV_HEADS = 2
SWA_DIM = 128
SWA_WINDOW = 128
SWA_BLOCK = 128
ROPE_THETA = 10000.0
RET_HEADS = 4
RET_DIM = 128
RET_WIDTH = RET_HEADS * RET_DIM
RET_CHUNK = 128
RET_COLS = 4 * RET_WIDTH
HY_CH = 512
HY_ORDER = 2
HY_BANDS = 8
HY_SIN_FREQ = 1.0
HY_TARGET = 1e-2
HY_DECAY_SHORT = 0.3
HY_DECAY_LONG = 1.5

VMEM_LIMIT_BYTES = 56 * 1024 * 1024


def _cparams(sem):
    return pltpu.CompilerParams(dimension_semantics=sem, vmem_limit_bytes=VMEM_LIMIT_BYTES)


def _mod_body(c_ref, w_ref, b_ref, o_ref):
    a = c_ref[...]
    a = a * jax.nn.sigmoid(a)
    o_ref[...] = jnp.dot(a, w_ref[...], preferred_element_type=F32,
                         precision=lax.Precision.HIGHEST) + b_ref[...]


def modulation(c_rows, mod_w, mod_b):
    depth, d, n = mod_w.shape
    tn = 1536
    return pl.pallas_call(
        _mod_body,
        grid=(depth, n // tn),
        in_specs=[pl.BlockSpec((8, d), lambda l, j: (0, 0)),
                  pl.BlockSpec((None, d, tn), lambda l, j: (l, 0, j)),
                  pl.BlockSpec((None, 1, tn), lambda l, j: (l, 0, j))],
        out_specs=pl.BlockSpec((None, 8, tn), lambda l, j: (l, 0, j)),
        out_shape=jax.ShapeDtypeStruct((depth, 8, n), F32),
        compiler_params=_cparams(("parallel", "parallel")),
        name="modulation",
    )(c_rows, mod_w, mod_b.reshape(depth, 1, n))


def _norm_mod(x, nw, shift, scale):
    y = x * lax.rsqrt(jnp.mean(x * x, axis=-1, keepdims=True) + EPS)
    return (y * nw) * (1.0 + scale) + shift


def _inproj_body(x_ref, nw_ref, shift_ref, scale_ref, w_ref, o_ref, h_scr):
    @pl.when(pl.program_id(2) == 0)
    def _():
        h_scr[...] = _norm_mod(x_ref[...], nw_ref[...], shift_ref[...], scale_ref[...]).astype(BF16)

    o_ref[...] = jnp.dot(h_scr[...], w_ref[...], preferred_element_type=F32)


def in_projection(x, nw, shift, scale, w, tm, tn):
    b, l, d = x.shape
    n = w.shape[1]
    return pl.pallas_call(
        _inproj_body,
        grid=(b, l // tm, n // tn),
        in_specs=[pl.BlockSpec((None, tm, d), lambda bi, i, j: (bi, i, 0)),
                  pl.BlockSpec((1, d), lambda bi, i, j: (0, 0)),
                  pl.BlockSpec((None, 1, d), lambda bi, i, j: (bi, 0, 0)),
                  pl.BlockSpec((None, 1, d), lambda bi, i, j: (bi, 0, 0)),
                  pl.BlockSpec((d, tn), lambda bi, i, j: (0, j))],
        out_specs=pl.BlockSpec((None, tm, tn), lambda bi, i, j: (bi, i, j)),
        out_shape=jax.ShapeDtypeStruct((b, l, n), F32),
        scratch_shapes=[pltpu.VMEM((tm, d), BF16)],
        compiler_params=_cparams(("parallel", "parallel", "arbitrary")),
        name="in_projection",
    )(x, nw.reshape(1, d), shift, scale, w)


def _outffn_body(x_ref, oa_ref, ob_ref, wo_ref, g2_ref, nw_ref, sh_ref, sc_ref, g5_ref,
                 wg_ref, wu_ref, wd_ref, y_ref, x1_scr, h_scr, acc_scr):
    k = pl.program_id(2)

    @pl.when(k == 0)
    def _():
        wa = oa_ref.shape[1]
        mix = (jnp.dot(oa_ref[...], wo_ref[:wa, :], preferred_element_type=F32)
               + jnp.dot(ob_ref[...], wo_ref[wa:, :], preferred_element_type=F32))
        x1 = x_ref[...] + g2_ref[...] * mix
        x1_scr[...] = x1
        h_scr[...] = _norm_mod(x1, nw_ref[...], sh_ref[...], sc_ref[...]).astype(BF16)
        acc_scr[...] = jnp.zeros_like(acc_scr)

    h = h_scr[...]
    g = jnp.dot(h, wg_ref[...], preferred_element_type=F32)
    u = jnp.dot(h, wu_ref[...], preferred_element_type=F32)
    a = (g * jax.nn.sigmoid(g) * u).astype(BF16)
    acc_scr[...] += jnp.dot(a, wd_ref[...], preferred_element_type=F32)

    @pl.when(k == pl.num_programs(2) - 1)
    def _():
        y_ref[...] = x1_scr[...] + g5_ref[...] * acc_scr[...]


def out_projection_ffn(x, oa, ob, wo, g2, nw, shift, scale, g5, wg, wu, wd, tm, th):
    b, l, d = x.shape
    wa, wb = oa.shape[-1], ob.shape[-1]
    hid = wg.shape[1]
    tok = lambda bi, i, k: (bi, i, 0)
    vec = lambda bi, i, k: (bi, 0, 0)
    return pl.pallas_call(
        _outffn_body,
        grid=(b, l // tm, hid // th),
        in_specs=[pl.BlockSpec((None, tm, d), tok),
                  pl.BlockSpec((None, tm, wa), tok),
                  pl.BlockSpec((None, tm, wb), tok),
                  pl.BlockSpec((d, d), lambda bi, i, k: (0, 0)),
                  pl.BlockSpec((None, 1, d), vec),
                  pl.BlockSpec((1, d), lambda bi, i, k: (0, 0)),
                  pl.BlockSpec((None, 1, d), vec),
                  pl.BlockSpec((None, 1, d), vec),
                  pl.BlockSpec((None, 1, d), vec),
                  pl.BlockSpec((d, th), lambda bi, i, k: (0, k)),
                  pl.BlockSpec((d, th), lambda bi, i, k: (0, k)),
                  pl.BlockSpec((th, d), lambda bi, i, k: (k, 0))],
        out_specs=pl.BlockSpec((None, tm, d), tok),
        out_shape=jax.ShapeDtypeStruct((b, l, d), F32),
        scratch_shapes=[pltpu.VMEM((tm, d), F32), pltpu.VMEM((tm, d), BF16), pltpu.VMEM((tm, d), F32)],
        compiler_params=_cparams(("parallel", "parallel", "arbitrary")),
        name="out_projection_ffn",
    )(x, oa, ob, wo, g2, nw.reshape(1, d), shift, scale, g5, wg, wu, wd)


NEG_BIG = -1e30


def _head_rms(x, w):
    return x * lax.rsqrt(jnp.mean(x * x, axis=-1, keepdims=True) + EPS) * w


def _swa_prep_body(*refs, use_rope):
    if use_rope:
        q_ref, k_ref, v_ref, qw_ref, kw_ref, cos_ref, sa_ref, sb_ref, qo_ref, ko_ref, vo_ref = refs
    else:
        q_ref, k_ref, v_ref, qw_ref, kw_ref, qo_ref, ko_ref, vo_ref = refs

    def prep(x, w):
        y = _head_rms(x, w)
        if use_rope:
            y = (y * cos_ref[...] + pltpu.roll(y, SWA_DIM - 32, 1) * sa_ref[...]
                 + pltpu.roll(y, 32, 1) * sb_ref[...])
        return y.astype(BF16)

    for h in range(SWA_HEADS):
        sl = slice(h * SWA_DIM, (h + 1) * SWA_DIM)
        qo_ref[:, sl] = prep(q_ref[:, sl], qw_ref[...])
    for h in range(SWA_KV_HEADS):
        sl = slice(h * SWA_DIM, (h + 1) * SWA_DIM)
        ko_ref[:, sl] = prep(k_ref[:, sl], kw_ref[...])
    vo_ref[...] = v_ref[...].astype(BF16)


def swa_prep(p, col0, qw, kw, rope, tl):
    b, l, _ = p.shape
    nq, nk = SWA_HEADS * SWA_DIM, SWA_KV_HEADS * SWA_DIM
    tok = lambda bi, i: (bi, i, 0)
    in_specs = [pl.BlockSpec((None, tl, nq), lambda bi, i: (bi, i, col0 // nq)),
                pl.BlockSpec((None, tl, nk), lambda bi, i: (bi, i, (col0 + nq) // nk)),
                pl.BlockSpec((None, tl, nk), lambda bi, i: (bi, i, (col0 + nq + nk) // nk)),
                pl.BlockSpec((1, SWA_DIM), lambda bi, i: (0, 0)),
                pl.BlockSpec((1, SWA_DIM), lambda bi, i: (0, 0))]
    args = [p, p, p, qw.reshape(1, SWA_DIM), kw.reshape(1, SWA_DIM)]
    if rope is not None:
        in_specs += [pl.BlockSpec((tl, SWA_DIM), lambda bi, i: (i, 0))] * 3
        args += list(rope)
    return pl.pallas_call(
        functools.partial(_swa_prep_body, use_rope=rope is not None),
        grid=(b, l // tl),
        in_specs=in_specs,
        out_specs=[pl.BlockSpec((None, tl, nq), tok), pl.BlockSpec((None, tl, nk), tok),
                   pl.BlockSpec((None, tl, nk), tok)],
        out_shape=[jax.ShapeDtypeStruct((b, l, nq), BF16), jax.ShapeDtypeStruct((b, l, nk), BF16),
                   jax.ShapeDtypeStruct((b, l, nk), BF16)],
        compiler_params=_cparams(("parallel", "parallel")),
        name="swa_prep",
    )(*args)


def _nt_dot(a, b):
    return lax.dot_general(a, b, (((1,), (1,)), ((), ())), preferred_element_type=F32)


def _swa_attend(q, keys, vals, masks, sink_col):
    scale = SWA_DIM ** -0.5
    scores = []
    for kk, mask in zip(keys, masks):
        s = _nt_dot(q, kk) * scale
        if mask is not None:
            s = jnp.where(mask, s, NEG_BIG)
        scores.append(s)
    m = sink_col
    for s in scores:
        m = jnp.maximum(m, jnp.max(s, axis=-1, keepdims=True))
    den = jnp.exp(sink_col - m)
    acc = None
    for s, vv in zip(scores, vals):
        pr = jnp.exp(s - m)
        den = den + jnp.sum(pr, axis=-1, keepdims=True)
        o = jnp.dot(pr.astype(BF16), vv, preferred_element_type=F32)
        acc = o if acc is None else acc + o
    return acc / den


def _swa_attn_body(sink_ref, q_ref, kp_ref, km_ref, kn_ref, vp_ref, vm_ref, vn_ref, kc_ref, vc_ref,
                   o_ref, *, nb):
    i = pl.program_id(1)
    n_blocks = pl.num_programs(1) * nb
    w, d = SWA_BLOCK, SWA_DIM
    grp = SWA_HEADS // SWA_KV_HEADS
    rows = lax.broadcasted_iota(jnp.int32, (grp * w, w), 0)
    qi = rows % w
    kj = lax.broadcasted_iota(jnp.int32, (grp * w, w), 1)
    for r in range(nb):
        blk = i * nb + r
        mask_prev = (kj >= qi) & (blk > 0)
        mask_next = (kj <= qi) & (blk < n_blocks - 1)
        rs = slice(r * w, (r + 1) * w)
        for h in range(SWA_KV_HEADS):
            hs = slice(h * d, (h + 1) * d)
            q = jnp.concatenate([q_ref[rs, (h * grp + g) * d:(h * grp + g + 1) * d] for g in range(grp)], axis=0)
            sink_col = jnp.zeros((grp * w, 1), F32)
            for g in range(grp):
                sink_col = jnp.where((rows[:, :1] // w) == g, sink_ref[h * grp + g], sink_col)
            if r > 0:
                k_prev, v_prev = km_ref[(r - 1) * w:r * w, hs], vm_ref[(r - 1) * w:r * w, hs]
            else:
                k_prev, v_prev = kp_ref[:, hs], vp_ref[:, hs]
            if r < nb - 1:
                k_next, v_next = km_ref[(r + 1) * w:(r + 2) * w, hs], vm_ref[(r + 1) * w:(r + 2) * w, hs]
            else:
                k_next, v_next = kn_ref[:, hs], vn_ref[:, hs]
            o = _swa_attend(q, [k_prev, km_ref[rs, hs], k_next, kc_ref[:, hs]],
                            [v_prev, vm_ref[rs, hs], v_next, vc_ref[:, hs]],
                            [mask_prev, None, mask_next, None], sink_col)
            for g in range(grp):
                o_ref[rs, (h * grp + g) * d:(h * grp + g + 1) * d] = o[g * w:(g + 1) * w].astype(o_ref.dtype)


def swa_attention(q, k, v, kc, vc, sink, nb):
    b, l, nq = q.shape
    nk = k.shape[-1]
    n_ctx = kc.shape[1]
    w = SWA_BLOCK
    tq = nb * w
    last = l // w - 1
    main = lambda bi, i, s: (bi, i, 0)
    prev = lambda bi, i, s: (bi, jnp.maximum(i * nb - 1, 0), 0)
    nxt = lambda bi, i, s: (bi, jnp.minimum((i + 1) * nb, last), 0)
    cx = lambda bi, i, s: (bi, 0, 0)
    return pl.pallas_call(
        functools.partial(_swa_attn_body, nb=nb),
        grid_spec=pltpu.PrefetchScalarGridSpec(
            num_scalar_prefetch=1,
            grid=(b, l // tq),
            in_specs=[pl.BlockSpec((None, tq, nq), main),
                      pl.BlockSpec((None, w, nk), prev), pl.BlockSpec((None, tq, nk), main),
                      pl.BlockSpec((None, w, nk), nxt),
                      pl.BlockSpec((None, w, nk), prev), pl.BlockSpec((None, tq, nk), main),
                      pl.BlockSpec((None, w, nk), nxt),
                      pl.BlockSpec((None, n_ctx, nk), cx), pl.BlockSpec((None, n_ctx, nk), cx)],
            out_specs=pl.BlockSpec((None, tq, nq), main)),
        out_shape=jax.ShapeDtypeStruct((b, l, nq), BF16),
        compiler_params=_cparams(("parallel", "parallel")),
        name="swa_attention",
    )(sink, q, k, k, k, v, v, v, kc, vc)


def _ctx_attn_body(sink_ref, q_ref, kc_ref, vc_ref, o_ref):
    n_ctx, d = q_ref.shape[0], SWA_DIM
    grp = SWA_HEADS // SWA_KV_HEADS
    rows = lax.broadcasted_iota(jnp.int32, (grp * n_ctx, 1), 0)
    for h in range(SWA_KV_HEADS):
        hs = slice(h * d, (h + 1) * d)
        q = jnp.concatenate([q_ref[:, (h * grp + g) * d:(h * grp + g + 1) * d] for g in range(grp)], axis=0)
        sink_col = jnp.zeros((grp * n_ctx, 1), F32)
        for g in range(grp):
            sink_col = jnp.where((rows // n_ctx) == g, sink_ref[h * grp + g], sink_col)
        o = _swa_attend(q, [kc_ref[:, hs]], [vc_ref[:, hs]], [None], sink_col)
        for g in range(grp):
            o_ref[:, (h * grp + g) * d:(h * grp + g + 1) * d] = o[g * n_ctx:(g + 1) * n_ctx].astype(o_ref.dtype)


def ctx_attention(qc, kc, vc, sink):
    b, n_ctx, nq = qc.shape
    nk = kc.shape[-1]
    cx = lambda bi, s: (bi, 0, 0)
    return pl.pallas_call(
        _ctx_attn_body,
        grid_spec=pltpu.PrefetchScalarGridSpec(
            num_scalar_prefetch=1, grid=(b,),
            in_specs=[pl.BlockSpec((None, n_ctx, nq), cx), pl.BlockSpec((None, n_ctx, nk), cx),
                      pl.BlockSpec((None, n_ctx, nk), cx)],
            out_specs=pl.BlockSpec((None, n_ctx, nq), cx)),
        out_shape=jax.ShapeDtypeStruct((b, n_ctx, nq), BF16),
        compiler_params=_cparams(("parallel",)),
        name="ctx_attention",
    )(sink, qc, kc, vc)


def axial_rope_tables(rows):
    ang_row, ang_col = axial_angles(rows)
    zero = jnp.zeros_like(ang_row)
    cos = jnp.concatenate([jnp.cos(ang_row)] * 2 + [jnp.cos(ang_col)] * 2, axis=-1)
    sa = jnp.concatenate([-jnp.sin(ang_row), zero, -jnp.sin(ang_col), zero], axis=-1)
    sb = jnp.concatenate([zero, jnp.sin(ang_row), zero, jnp.sin(ang_col)], axis=-1)
    return cos, sa, sb


def window_gqa_pallas(p_lat, p_ctx, col0, rope, q_norm_w, k_norm_w, sink, need_ctx):
    ql, kl, vl = swa_prep(p_lat, col0, q_norm_w, k_norm_w, rope, 1024)
    qc, kc, vc = swa_prep(p_ctx, col0, q_norm_w, k_norm_w, None, p_ctx.shape[1])
    o_lat = swa_attention(ql, kl, vl, kc, vc, sink, 4)
    o_ctx = ctx_attention(qc, kc, vc, sink) if need_ctx else None
    return o_lat, o_ctx


def _tn_dot(a, b):
    return lax.dot_general(a, b, (((0,), (0,)), ((), ())), preferred_element_type=F32)


def _log_sigmoid(x):
    return jnp.minimum(x, 0.0) - jnp.log(1.0 + jnp.exp(-jnp.abs(x)))


def _ret_scan_body(logit_ref, q_ref, k_ref, v_ref, cos_ref, sin_ref, s0_ref, o_ref, sf_ref, s_scr):
    h, d, n = pl.program_id(1), pl.program_id(2), pl.program_id(3)
    c = RET_CHUNK

    @pl.when(n == 0)
    def _():
        s_scr[...] = s0_ref[...]

    lg = _log_sigmoid(jnp.full((1, 1), logit_ref[d * RET_HEADS + h], F32))
    fwd = d == 0

    def rot(x):
        return x * cos_ref[...] + pltpu.roll(x, RET_DIM // 2, 1) * sin_ref[...]

    q = rot(q_ref[...])
    k = rot(k_ref[...]) * RET_DIM ** -0.5
    v = v_ref[...].astype(BF16)
    ii = lax.broadcasted_iota(jnp.int32, (c, c), 0)
    jj = lax.broadcasted_iota(jnp.int32, (c, c), 1)
    rel = jnp.where(fwd, ii - jj, jj - ii)
    incl = rel >= 0
    dmat = jnp.where(incl, jnp.exp(lg * jnp.where(incl, rel, 0).astype(F32)), 0.0)
    i1 = lax.broadcasted_iota(jnp.int32, (c, 1), 0)
    pos = jnp.where(fwd, i1, c - 1 - i1).astype(F32)
    scores = _nt_dot(q.astype(BF16), k.astype(BF16)) * dmat
    o_in = jnp.dot(scores.astype(BF16), v, preferred_element_type=F32)
    q_dec = (q * jnp.exp(lg * (pos + 1.0))).astype(BF16)
    k_dec = (k * jnp.exp(lg * (c - 1.0 - pos))).astype(BF16)
    s = s_scr[...]
    o_ref[...] = o_in + jnp.dot(q_dec, s.astype(BF16), preferred_element_type=F32)
    s_new = s * jnp.exp(lg * c) + _tn_dot(k_dec, v)
    s_scr[...] = s_new

    @pl.when(n == pl.num_programs(3) - 1)
    def _():
        sf_ref[...] = s_new


def retention_scan(p, decay_logit, cos, sin, s0):
    b, l, _ = p.shape
    c, dh, nh = RET_CHUNK, RET_DIM, RET_HEADS
    n = l // c
    chunk = lambda ni, di: jnp.where(di == 0, ni, n - 1 - ni)
    col = lambda off: (lambda bi, hi, di, ni, s: (bi, chunk(ni, di), off + hi))
    tab = lambda bi, hi, di, ni, s: (chunk(ni, di), 0)
    st = lambda bi, hi, di, ni, s: (bi, hi, di, 0, 0)
    return pl.pallas_call(
        _ret_scan_body,
        grid_spec=pltpu.PrefetchScalarGridSpec(
            num_scalar_prefetch=1,
            grid=(b, nh, 2, n),
            in_specs=[pl.BlockSpec((None, c, dh), col(0)), pl.BlockSpec((None, c, dh), col(nh)),
                      pl.BlockSpec((None, c, dh), col(2 * nh)),
                      pl.BlockSpec((c, dh), tab), pl.BlockSpec((c, dh), tab),
                      pl.BlockSpec((None, None, None, dh, dh), st)],
            out_specs=[pl.BlockSpec((None, None, c, dh), lambda bi, hi, di, ni, s: (di, bi, chunk(ni, di), hi)),
                       pl.BlockSpec((None, None, None, dh, dh), st)],
            scratch_shapes=[pltpu.VMEM((dh, dh), F32)]),
        out_shape=[jax.ShapeDtypeStruct((2, b, l, nh * dh), F32),
                   jax.ShapeDtypeStruct((b, nh, 2, dh, dh), F32)],
        compiler_params=_cparams(("parallel", "parallel", "arbitrary", "arbitrary")),
        name="retention_scan",
    )(decay_logit.reshape(-1), p, p, p, cos, sin, s0)


def _ret_final_body(of_ref, ob_ref, g_ref, w_ref, y_ref):
    for h in range(RET_HEADS):
        sl = slice(h * RET_DIM, (h + 1) * RET_DIM)
        o = of_ref[:, sl] + ob_ref[:, sl]
        mu = jnp.mean(o, axis=-1, keepdims=True)
        var = jnp.mean(jnp.square(o - mu), axis=-1, keepdims=True)
        y = (o - mu) * lax.rsqrt(var + EPS) * w_ref[:, sl]
        g = g_ref[:, sl]
        y_ref[:, sl] = (y * (g * jax.nn.sigmoid(g))).astype(y_ref.dtype)


def retention_finalize(o2, p, gn_w, tl):
    _, b, l, w = o2.shape
    tok = lambda bi, i: (bi, i, 0)
    return pl.pallas_call(
        _ret_final_body,
        grid=(b, l // tl),
        in_specs=[pl.BlockSpec((None, None, tl, w), lambda bi, i: (0, bi, i, 0)),
                  pl.BlockSpec((None, None, tl, w), lambda bi, i: (1, bi, i, 0)),
                  pl.BlockSpec((None, tl, w), lambda bi, i: (bi, i, 3)),
                  pl.BlockSpec((1, w), lambda bi, i: (0, 0))],
        out_specs=pl.BlockSpec((None, tl, w), tok),
        out_shape=jax.ShapeDtypeStruct((b, l, w), BF16),
        compiler_params=_cparams(("parallel", "parallel")),
        name="retention_finalize",
    )(o2, o2, p, gn_w.reshape(1, w))


def retention_rope_tables(l):
    inv = ROPE_THETA ** (-jnp.linspace(0.0, 1.0, RET_DIM // 2, dtype=F32))
    ang = jnp.arange(l, dtype=F32)[:, None] * inv
    return (jnp.concatenate([jnp.cos(ang), jnp.cos(ang)], axis=-1),
            jnp.concatenate([-jnp.sin(ang), jnp.sin(ang)], axis=-1))


def retention_pallas(p_lat, p_ctx, decay_logit, gn_w, need_ctx):
    b, l, _ = p_lat.shape
    n_ctx = p_ctx.shape[1]
    s0 = jnp.zeros((b, RET_HEADS, 2, RET_DIM, RET_DIM), F32)
    o_ctx, s_ctx = retention_scan(p_ctx, decay_logit, *retention_rope_tables(n_ctx), s0)
    o_lat, _ = retention_scan(p_lat, decay_logit, *retention_rope_tables(l), s_ctx)
    out_lat = retention_finalize(o_lat, p_lat, gn_w, 1024)
    out_ctx = retention_finalize(o_ctx, p_ctx, gn_w, n_ctx) if need_ctx else None
    return out_lat, out_ctx


DN_BLOCK = 128
DN_STREAMS = 2 * DN_HEADS


def _shift_rows(x, prev_row, next_row):
    n = x.shape[0]
    r = lax.broadcasted_iota(jnp.int32, (n, 1), 0)
    x_prev = jnp.where(r == 0, prev_row, pltpu.roll(x, 1, 0))
    x_next = jnp.where(r == n - 1, next_row, pltpu.roll(x, n - 1, 0))
    return x_prev, x_next


def _conv3(x_ref, xp_ref, xn_ref, w_ref):
    i = pl.program_id(1)
    prev_row = jnp.where(i > 0, xp_ref[7:8, :], 0.0)
    next_row = jnp.where(i < pl.num_programs(1) - 1, xn_ref[0:1, :], 0.0)
    x = x_ref[...]
    x_prev, x_next = _shift_rows(x, prev_row, next_row)
    return x_prev * w_ref[0:1, :] + x * w_ref[1:2, :] + x_next * w_ref[2:3, :]


def _dn_prep_body(x_ref, xp_ref, xn_ref, ab_ref, cw_ref, alog_ref, dtb_ref,
                  q_ref, k_ref, v_ref, gb_ref, gam_ref, gamt_ref):
    y = _conv3(x_ref, xp_ref, xn_ref, cw_ref)
    y = y * jax.nn.sigmoid(y)
    for h in range(DN_HEADS):
        for part, (ref, mul) in enumerate(((q_ref, DN_DIM ** -0.5), (k_ref, 1.0))):
            sl = slice(part * DN_WIDTH + h * DN_DIM, part * DN_WIDTH + (h + 1) * DN_DIM)
            t = y[:, sl]
            t = t * lax.rsqrt(jnp.sum(t * t, axis=-1, keepdims=True) + EPS)
            ref[:, h * DN_DIM:(h + 1) * DN_DIM] = t * mul if mul != 1.0 else t
    v_ref[...] = y[:, 2 * DN_WIDTH:]

    ab = ab_ref[...]
    z = ab + dtb_ref[...]
    softplus = jnp.maximum(z, 0.0) + jnp.log(1.0 + jnp.exp(-jnp.abs(z)))
    g = -jnp.exp(alog_ref[...]) * softplus
    lane = lax.broadcasted_iota(jnp.int32, ab.shape, 1)
    gb_ref[...] = jnp.where(lane < DN_STREAMS, g, jax.nn.sigmoid(ab))

    c = DN_BLOCK
    ii = lax.broadcasted_iota(jnp.int32, (c, c), 0)
    jj = lax.broadcasted_iota(jnp.int32, (c, c), 1)
    tri_f = (ii >= jj).astype(F32)
    tri_b = (ii <= jj).astype(F32)
    lane_c = lax.broadcasted_iota(jnp.int32, (c, ab.shape[1]), 1)
    for n in range(x_ref.shape[0] // c):
        gc = g[n * c:(n + 1) * c, :]
        cf = jnp.dot(tri_f, gc, preferred_element_type=F32, precision=lax.Precision.HIGHEST)
        cb = jnp.dot(tri_b, gc, preferred_element_type=F32, precision=lax.Precision.HIGHEST)
        gam = jnp.where(lane_c < DN_HEADS, cf, cb)
        gam_ref[n * c:(n + 1) * c, :] = gam
        gamt_ref[n] = gam.T[:DN_STREAMS, :]


def dn_prep(p, ab_col, conv_w, a_log, dt_bias, tl):
    b, l, _ = p.shape
    c3 = 3 * DN_WIDTH
    nblk8 = l // 8
    lanes = 128
    pad8 = lambda a: jnp.pad(a.reshape(1, DN_STREAMS).astype(F32), ((0, 0), (0, lanes - DN_STREAMS)))
    tok = lambda bi, i: (bi, i, 0)
    return pl.pallas_call(
        _dn_prep_body,
        grid=(b, l // tl),
        in_specs=[pl.BlockSpec((None, tl, c3), tok),
                  pl.BlockSpec((None, 8, c3), lambda bi, i: (bi, jnp.maximum(i * (tl // 8) - 1, 0), 0)),
                  pl.BlockSpec((None, 8, c3), lambda bi, i: (bi, jnp.minimum((i + 1) * (tl // 8), nblk8 - 1), 0)),
                  pl.BlockSpec((None, tl, lanes), lambda bi, i: (bi, i, ab_col // lanes)),
                  pl.BlockSpec((3, c3), lambda bi, i: (0, 0)),
                  pl.BlockSpec((1, lanes), lambda bi, i: (0, 0)),
                  pl.BlockSpec((1, lanes), lambda bi, i: (0, 0))],
        out_specs=[pl.BlockSpec((None, tl, DN_WIDTH), tok)] * 3
        + [pl.BlockSpec((None, tl, lanes), tok)] * 2
        + [pl.BlockSpec((None, tl // DN_BLOCK, DN_STREAMS, DN_BLOCK), lambda bi, i: (bi, i, 0, 0))],
        out_shape=[jax.ShapeDtypeStruct((b, l, DN_WIDTH), F32)] * 3
        + [jax.ShapeDtypeStruct((b, l, lanes), F32)] * 2
        + [jax.ShapeDtypeStruct((b, l // DN_BLOCK, DN_STREAMS, DN_BLOCK), F32)],
        compiler_params=_cparams(("parallel", "parallel")),
        name="dn_prep",
    )(p, p, p, p, conv_w, pad8(a_log), pad8(dt_bias))


def _split_bf16(a):
    hi = a.astype(BF16)
    return hi, (a - hi.astype(F32)).astype(BF16)


def _dot3(a, b):
    a_hi, a_lo = _split_bf16(a)
    b_hi, b_lo = _split_bf16(b)
    d = lambda x, y: jnp.dot(x, y, preferred_element_type=F32)
    return d(a_hi, b_hi) + (d(a_hi, b_lo) + d(a_lo, b_hi))


def _unit_triangular_inverse(m, ii, jj):
    c = m.shape[0]
    eye = (ii == jj).astype(F32)
    t = eye - jnp.where((ii // 2) == (jj // 2), m, 0.0)
    s = 2
    while s < c:
        off = jnp.where(((ii // (2 * s)) == (jj // (2 * s))) & ((ii // s) != (jj // s)), m, 0.0)
        t = t - _dot3(t, _dot3(off, t))
        s *= 2
    return t


def _dn_chunk(q, k, v, gam_col, gam_row, beta_col, s, fwd, ii, jj):
    c = q.shape[0]
    dv = v.shape[1]
    incl = (ii >= jj) if fwd else (ii <= jj)
    strict = (ii > jj) if fwd else (ii < jj)
    e = jnp.exp(jnp.where(incl, gam_col - gam_row, 0.0))
    dec_incl = jnp.where(incl, e, 0.0)
    dec_strict = jnp.where(strict, e, 0.0)
    kb = k * beta_col
    k16 = k.astype(BF16)
    m = _nt_dot(kb.astype(BF16), k16) * dec_strict
    t = _unit_triangular_inverse(m, ii, jj)
    eg = jnp.exp(gam_col)
    sol = _dot3(t, jnp.concatenate([v * beta_col, kb * eg], axis=-1))
    u, w = sol[:, :dv], sol[:, dv:]
    attn = _nt_dot(q.astype(BF16), k16) * dec_incl
    g_last = gam_col[c - 1:c, :] if fwd else gam_col[0:1, :]
    s16 = s.astype(BF16)
    v_new = u - jnp.dot(w.astype(BF16), s16, preferred_element_type=F32)
    v16 = v_new.astype(BF16)
    o = (jnp.dot((q * eg).astype(BF16), s16, preferred_element_type=F32)
         + jnp.dot(attn.astype(BF16), v16, preferred_element_type=F32))
    s_new = s * jnp.exp(g_last) + _tn_dot((k * jnp.exp(g_last - gam_col)).astype(BF16), v16)
    return o, s_new


def _dn_scan_body(qf_ref, kf_ref, vf_ref, gbf_ref, gamf_ref, gamtf_ref,
                  qb_ref, kb_ref, vb_ref, gbb_ref, gamb_ref, gamtb_ref, s0_ref,
                  of_ref, ob_ref, sf_ref, s_scr):
    n = pl.program_id(1)

    @pl.when(n == 0)
    def _():
        s_scr[...] = s0_ref[...]

    c = DN_BLOCK
    ii = lax.broadcasted_iota(jnp.int32, (c, c), 0)
    jj = lax.broadcasted_iota(jnp.int32, (c, c), 1)
    dirs = ((True, qf_ref, kf_ref, vf_ref, gbf_ref, gamf_ref, gamtf_ref, of_ref),
            (False, qb_ref, kb_ref, vb_ref, gbb_ref, gamb_ref, gamtb_ref, ob_ref))
    for d, (fwd, q_ref, k_ref, v_ref, gb_ref, gam_ref, gamt_ref, o_ref) in enumerate(dirs):
        for h in range(DN_HEADS):
            r = d * DN_HEADS + h
            hs = slice(h * DN_DIM, (h + 1) * DN_DIM)
            o, s_new = _dn_chunk(q_ref[:, hs], k_ref[:, hs], v_ref[:, hs], gam_ref[:, r:r + 1],
                                 gamt_ref[r:r + 1, :], gb_ref[:, DN_STREAMS + r:DN_STREAMS + r + 1],
                                 s_scr[r], fwd, ii, jj)
            o_ref[:, hs] = o
            s_scr[r] = s_new

    @pl.when(n == pl.num_programs(1) - 1)
    def _():
        sf_ref[...] = s_scr[...]


def dn_scan(q, k, v, gb, gam, gamt, s0):
    b, l, w = q.shape
    c = DN_BLOCK
    n = l // c
    lanes = gb.shape[-1]
    f3 = lambda bi, ni: (bi, ni, 0)
    b3 = lambda bi, ni: (bi, n - 1 - ni, 0)
    f4 = lambda bi, ni: (bi, ni, 0, 0)
    b4 = lambda bi, ni: (bi, n - 1 - ni, 0, 0)
    st = lambda bi, ni: (bi, 0, 0, 0)

    def specs(m3, m4):
        return [pl.BlockSpec((None, c, w), m3)] * 3 + [pl.BlockSpec((None, c, lanes), m3)] * 2 + [
            pl.BlockSpec((None, None, DN_STREAMS, c), m4)]

    return pl.pallas_call(
        _dn_scan_body,
        grid=(b, n),
        in_specs=specs(f3, f4) + specs(b3, b4) + [pl.BlockSpec((None, DN_STREAMS, DN_DIM, DN_DIM), st)],
        out_specs=[pl.BlockSpec((None, c, w), f3), pl.BlockSpec((None, c, w), b3),
                   pl.BlockSpec((None, DN_STREAMS, DN_DIM, DN_DIM), st)],
        out_shape=[jax.ShapeDtypeStruct((b, l, w), F32), jax.ShapeDtypeStruct((b, l, w), F32),
                   jax.ShapeDtypeStruct((b, DN_STREAMS, DN_DIM, DN_DIM), F32)],
        scratch_shapes=[pltpu.VMEM((DN_STREAMS, DN_DIM, DN_DIM), F32)],
        compiler_params=_cparams(("parallel", "arbitrary")),
        name="dn_scan",
    )(q, k, v, gb, gam, gamt, q, k, v, gb, gam, gamt, s0)


def _dn_final_body(of_ref, ob_ref, g_ref, w_ref, y_ref):
    for h in range(DN_HEADS):
        sl = slice(h * DN_DIM, (h + 1) * DN_DIM)
        o = of_ref[:, sl] + ob_ref[:, sl]
        y = _head_rms(o, w_ref[...])
        g = g_ref[:, sl]
        y_ref[:, sl] = (y * (g * jax.nn.sigmoid(g))).astype(y_ref.dtype)


def dn_finalize(o_f, o_b, p, gate_col, norm_w, tl):
    b, l, w = o_f.shape
    tok = lambda bi, i: (bi, i, 0)
    return pl.pallas_call(
        _dn_final_body,
        grid=(b, l // tl),
        in_specs=[pl.BlockSpec((None, tl, w), tok), pl.BlockSpec((None, tl, w), tok),
                  pl.BlockSpec((None, tl, w), lambda bi, i: (bi, i, gate_col // w)),
                  pl.BlockSpec((1, DN_DIM), lambda bi, i: (0, 0))],
        out_specs=pl.BlockSpec((None, tl, w), tok),
        out_shape=jax.ShapeDtypeStruct((b, l, w), BF16),
        compiler_params=_cparams(("parallel", "parallel")),
        name="dn_finalize",
    )(o_f, o_b, p, norm_w.reshape(1, DN_DIM))


def deltanet_pallas(p_lat, p_ctx, conv_w, a_log, dt_bias, norm_w, need_ctx, ab_col=3072, gate_col=1536):
    b = p_lat.shape[0]
    n_ctx = p_ctx.shape[1]
    s0 = jnp.zeros((b, DN_STREAMS, DN_DIM, DN_DIM), F32)
    fc = dn_prep(p_ctx, ab_col, conv_w, a_log, dt_bias, n_ctx)
    oc_f, oc_b, s_ctx = dn_scan(*fc, s0)
    fl = dn_prep(p_lat, ab_col, conv_w, a_log, dt_bias, 1024)
    ol_f, ol_b, _ = dn_scan(*fl, s_ctx)
    out_lat = dn_finalize(ol_f, ol_b, p_lat, gate_col, norm_w, 1024)
    out_ctx = dn_finalize(oc_f, oc_b, p_ctx, gate_col, norm_w, n_ctx) if need_ctx else None
    return out_lat, out_ctx


def rmsnorm(x, w):
    xf = x.astype(F32)
    y = xf * lax.rsqrt(jnp.mean(xf * xf, axis=-1, keepdims=True) + EPS)
    return (y * w.astype(F32)).astype(x.dtype)


def l2norm(x):
    return x * lax.rsqrt(jnp.sum(x * x, axis=-1, keepdims=True) + EPS)


def rotate_half(x, ang):
    cos = jnp.cos(ang)[:, None, :]
    sin = jnp.sin(ang)[:, None, :]
    x1, x2 = jnp.split(x, 2, axis=-1)
    return jnp.concatenate([x1 * cos - x2 * sin, x2 * cos + x1 * sin], axis=-1).astype(x.dtype)


def axial_angles(rows):
    m = SWA_DIM // 4
    inv = ROPE_THETA ** (-jnp.arange(m, dtype=F32) / m)
    row = jnp.repeat(jnp.arange(rows, dtype=F32), GRID_W)
    col = jnp.broadcast_to(jnp.arange(GRID_W, dtype=F32), (rows, GRID_W)).reshape(-1)
    return row[:, None] * inv, col[:, None] * inv


def axial_rope(x, ang_row, ang_col):
    h = x.shape[-1] // 2
    return jnp.concatenate([rotate_half(x[..., :h], ang_row), rotate_half(x[..., h:], ang_col)], axis=-1)


def short_conv(x, w):
    K, C = w.shape
    return lax.conv_general_dilated(x, w[:, None, :].astype(x.dtype), window_strides=(1,),
                                    padding=[(K // 2, K // 2)],
                                    dimension_numbers=('NWC', 'WIO', 'NWC'),
                                    feature_group_count=C)


def flip_time(t):
    return jnp.flip(t, axis=2)


def keep_time(t):
    return t


def delta_chunk_scan(q, k, v, g, beta, s0):
    B, H, L, dk = q.shape
    dv = v.shape[-1]
    C = DN_CHUNK
    N = L // C
    q = q.reshape(B, H, N, C, dk)
    k = k.reshape(B, H, N, C, dk)
    v = v.reshape(B, H, N, C, dv)
    g = g.reshape(B, H, N, C)
    beta = beta.reshape(B, H, N, C)
    gam = jnp.cumsum(g, axis=-1)
    idx = jnp.arange(C)
    incl = idx[:, None] >= idx[None, :]
    strict = idx[:, None] > idx[None, :]
    e = jnp.exp(jnp.where(incl, gam[..., :, None] - gam[..., None, :], 0.0))
    dec_incl = jnp.where(incl, e, 0.0)
    dec_strict = jnp.where(strict, e, 0.0)
    kb = k * beta[..., None]
    m = jnp.einsum('bhnid,bhnjd->bhnij', kb, k) * dec_strict
    a = m + jnp.eye(C, dtype=F32)
    rhs = jnp.concatenate([v * beta[..., None], kb * jnp.exp(gam)[..., None]], axis=-1)
    sol = lax.linalg.triangular_solve(a, rhs, left_side=True, lower=True, unit_diagonal=True)
    u, w = sol[..., :dv], sol[..., dv:]
    attn = jnp.einsum('bhnid,bhnjd->bhnij', q, k) * dec_incl
    q_dec = q * jnp.exp(gam)[..., None]
    k_dec = k * jnp.exp(gam[..., -1:] - gam)[..., None]
    chunk_dec = jnp.exp(gam[..., -1])

    def step(S, xs):
        u_n, w_n, a_n, qd_n, kd_n, cd_n = xs
        v_new = u_n - jnp.einsum('bhck,bhkv->bhcv', w_n, S)
        o = jnp.einsum('bhck,bhkv->bhcv', qd_n, S) + jnp.einsum('bhij,bhjv->bhiv', a_n, v_new)
        S = S * cd_n[..., None, None] + jnp.einsum('bhck,bhcv->bhkv', kd_n, v_new)
        return S, o

    xs = tuple(jnp.moveaxis(t, 2, 0) for t in (u, w, attn, q_dec, k_dec, chunk_dec))
    S, o = lax.scan(step, s0, xs)
    return jnp.moveaxis(o, 0, 2).reshape(B, H, L, dv), S


def dn_features(p, conv_w, a_log, dt_bias):
    B, L, _ = p.shape
    qkv = jax.nn.silu(short_conv(p[..., :3 * DN_WIDTH], conv_w)).astype(F32)
    heads = lambda t: t.reshape(B, L, DN_HEADS, DN_DIM).transpose(0, 2, 1, 3)
    q, k, v = (heads(t) for t in jnp.split(qkv, 3, axis=-1))
    q = l2norm(q) * DN_DIM ** -0.5
    k = l2norm(k)
    gate = p[..., 3 * DN_WIDTH:4 * DN_WIDTH]
    ab = p[..., 4 * DN_WIDTH:].astype(F32).reshape(B, L, 2, 2, DN_HEADS)
    g = -jnp.exp(a_log.astype(F32)) * jax.nn.softplus(ab[:, :, 0] + dt_bias.astype(F32))
    beta = jax.nn.sigmoid(ab[:, :, 1])
    return q, k, v, g.transpose(2, 0, 3, 1), beta.transpose(2, 0, 3, 1), gate


def head_norm_gate(o, gate, w):
    B, H, L, dv = o.shape
    y = rmsnorm(o.transpose(0, 2, 1, 3), w)
    return (y * jax.nn.silu(gate.astype(F32).reshape(B, L, H, dv))).reshape(B, L, H * dv)


def gated_deltanet(p_lat, p_ctx, conv_w, a_log, dt_bias, norm_w, need_ctx):
    lat = dn_features(p_lat, conv_w, a_log, dt_bias)
    cx = dn_features(p_ctx, conv_w, a_log, dt_bias)
    B = p_lat.shape[0]
    o_lat, o_ctx = [], []
    for d in range(2):
        fl = flip_time if d == 1 else keep_time
        s0 = jnp.zeros((B, DN_HEADS, DN_DIM, DN_DIM), F32)
        oc, s_ctx = delta_chunk_scan(fl(cx[0]), fl(cx[1]), fl(cx[2]), fl(cx[3][d]), fl(cx[4][d]), s0)
        ol, _ = delta_chunk_scan(fl(lat[0]), fl(lat[1]), fl(lat[2]), fl(lat[3][d]), fl(lat[4][d]), s_ctx)
        o_lat.append(fl(ol))
        o_ctx.append(fl(oc))
    out_lat = head_norm_gate(o_lat[0] + o_lat[1], lat[5], norm_w)
    out_ctx = head_norm_gate(o_ctx[0] + o_ctx[1], cx[5], norm_w) if need_ctx else None
    return out_lat, out_ctx


def gqa_qkv(p, q_norm_w, k_norm_w):
    B, L, _ = p.shape
    nq, nk = SWA_HEADS * SWA_DIM, SWA_KV_HEADS * SWA_DIM
    q = rmsnorm(p[..., :nq].reshape(B, L, SWA_HEADS, SWA_DIM), q_norm_w)
    k = rmsnorm(p[..., nq:nq + nk].reshape(B, L, SWA_KV_HEADS, SWA_DIM), k_norm_w)
    v = p[..., nq + nk:].reshape(B, L, SWA_KV_HEADS, SWA_DIM)
    return q, k, v


def banded_attention(q, k, v, kc, vc, sink):
    B, L, Hq, d = q.shape
    Hkv = k.shape[2]
    G = Hq // Hkv
    W = SWA_BLOCK
    NB = L // W
    qb = q.reshape(B, NB, W, Hkv, G, d)

    def band(t):
        tp = jnp.pad(t.reshape(B, NB, W, Hkv, d), ((0, 0), (1, 1), (0, 0), (0, 0), (0, 0)))
        return jnp.concatenate([tp[:, :-2], tp[:, 1:-1], tp[:, 2:]], axis=2)

    kb, vb = band(k), band(v)
    scale = d ** -0.5
    s_loc = jnp.einsum('bnqhgd,bnkhd->bnhgqk', qb, kb).astype(F32) * scale
    s_ctx = jnp.einsum('bnqhgd,bchd->bnhgqc', qb, kc).astype(F32) * scale
    rel = (jnp.arange(3 * W) - W)[None, :] - jnp.arange(W)[:, None]
    kblk = jnp.arange(NB)[:, None] + jnp.arange(3 * W)[None, :] // W - 1
    valid = (jnp.abs(rel) <= SWA_WINDOW)[None] & ((kblk >= 0) & (kblk < NB))[:, None, :]
    s_loc = jnp.where(valid[None, :, None, None], s_loc, -jnp.inf)
    s_sink = jnp.broadcast_to(sink.astype(F32).reshape(Hkv, G, 1, 1), s_loc.shape[:-1] + (1,))
    prob = jax.nn.softmax(jnp.concatenate([s_loc, s_ctx, s_sink], axis=-1), axis=-1).astype(v.dtype)
    n_ctx = kc.shape[1]
    o = (jnp.einsum('bnhgqk,bnkhd->bnqhgd', prob[..., :3 * W], vb)
         + jnp.einsum('bnhgqc,bchd->bnqhgd', prob[..., 3 * W:3 * W + n_ctx], vc))
    return o.reshape(B, L, Hq * d)


def context_attention(qc, kc, vc, sink):
    B, Cn, Hq, d = qc.shape
    Hkv = kc.shape[2]
    G = Hq // Hkv
    q = qc.reshape(B, Cn, Hkv, G, d)
    s = jnp.einsum('bqhgd,bkhd->bhgqk', q, kc).astype(F32) * d ** -0.5
    s_sink = jnp.broadcast_to(sink.astype(F32).reshape(Hkv, G, 1, 1), s.shape[:-1] + (1,))
    prob = jax.nn.softmax(jnp.concatenate([s, s_sink], axis=-1), axis=-1)[..., :-1].astype(vc.dtype)
    return jnp.einsum('bhgqk,bkhd->bqhgd', prob, vc).reshape(B, Cn, Hq * d)


def window_gqa(p_lat, p_ctx, ang_row, ang_col, q_norm_w, k_norm_w, sink, need_ctx):
    ql, kl, vl = gqa_qkv(p_lat, q_norm_w, k_norm_w)
    ql = axial_rope(ql, ang_row, ang_col)
    kl = axial_rope(kl, ang_row, ang_col)
    qc, kc, vc = gqa_qkv(p_ctx, q_norm_w, k_norm_w)
    o_lat = banded_attention(ql, kl, vl, kc, vc, sink)
    o_ctx = context_attention(qc, kc, vc, sink) if need_ctx else None
    return o_lat, o_ctx


def ab_mixer(p_lat, p_ctx, ang_row, ang_col, dn_conv_w, dn_a_log, dn_dt_bias, dn_norm_w,
             q_norm_w, k_norm_w, sink, need_ctx):
    a_lat, a_ctx = gated_deltanet(p_lat[..., :DN_COLS], p_ctx[..., :DN_COLS], dn_conv_w, dn_a_log,
                                  dn_dt_bias, dn_norm_w, need_ctx)
    b_lat, b_ctx = window_gqa(p_lat[..., DN_COLS:], p_ctx[..., DN_COLS:], ang_row, ang_col,
                              q_norm_w, k_norm_w, sink, need_ctx)
    o_lat = jnp.concatenate([a_lat, b_lat.astype(F32)], axis=-1)
    o_ctx = jnp.concatenate([a_ctx, b_ctx.astype(F32)], axis=-1) if need_ctx else None
    return o_lat, o_ctx


def retention_chunk_scan(q, k, v, log_gamma, s0):
    B, H, L, dk = q.shape
    dv = v.shape[-1]
    C = RET_CHUNK
    N = L // C
    q = q.reshape(B, H, N, C, dk)
    k = k.reshape(B, H, N, C, dk)
    v = v.reshape(B, H, N, C, dv)
    pos = jnp.arange(C, dtype=F32)
    rel = pos[:, None] - pos[None, :]
    incl = rel >= 0
    lg = log_gamma[:, None, None]
    dmat = jnp.where(incl, jnp.exp(lg * jnp.where(incl, rel, 0.0)), 0.0)
    scores = jnp.einsum('bhnid,bhnjd->bhnij', q, k) * dmat[:, None]
    o_in = jnp.einsum('bhnij,bhnje->bhnie', scores, v)
    q_dec = q * jnp.exp(log_gamma[:, None] * (pos + 1.0))[:, None, :, None]
    k_dec = k * jnp.exp(log_gamma[:, None] * (C - 1.0 - pos))[:, None, :, None]
    chunk_dec = jnp.exp(log_gamma * C)[None, :, None, None]

    def step(S, xs):
        qd, kd, vn = xs
        o = jnp.einsum('bhcd,bhde->bhce', qd, S)
        S = S * chunk_dec + jnp.einsum('bhcd,bhce->bhde', kd, vn)
        return S, o

    xs = tuple(jnp.moveaxis(t, 2, 0) for t in (q_dec, k_dec, v))
    S, o_x = lax.scan(step, s0, xs)
    o = o_in + jnp.moveaxis(o_x, 0, 2)
    return o.reshape(B, H, L, dv), S


def ret_features(p):
    B, L, _ = p.shape
    q, k, v, g = jnp.split(p, 4, axis=-1)
    inv = ROPE_THETA ** (-jnp.linspace(0.0, 1.0, RET_DIM // 2, dtype=F32))
    ang = jnp.arange(L, dtype=F32)[:, None] * inv
    heads = lambda t: t.reshape(B, L, RET_HEADS, RET_DIM).astype(F32)
    q = rotate_half(heads(q), ang)
    k = rotate_half(heads(k), ang) * RET_DIM ** -0.5
    tr = lambda t: t.transpose(0, 2, 1, 3)
    return tr(q), tr(k), tr(heads(v)), g


def head_groupnorm_gate(o, gate, w):
    B, H, L, dv = o.shape
    o = o.transpose(0, 2, 1, 3)
    mu = jnp.mean(o, axis=-1, keepdims=True)
    var = jnp.mean(jnp.square(o - mu), axis=-1, keepdims=True)
    y = (o - mu) * lax.rsqrt(var + EPS) * w.astype(F32).reshape(H, dv)
    return (y * jax.nn.silu(gate.astype(F32).reshape(B, L, H, dv))).reshape(B, L, H * dv)


def retention(p_lat, p_ctx, decay_logit, gn_w, need_ctx):
    lat, cx = ret_features(p_lat), ret_features(p_ctx)
    log_gamma = jax.nn.log_sigmoid(decay_logit.astype(F32))
    B = p_lat.shape[0]
    o_lat, o_ctx = [], []
    for d in range(2):
        fl = flip_time if d == 1 else keep_time
        s0 = jnp.zeros((B, RET_HEADS, RET_DIM, RET_DIM), F32)
        oc, s_ctx = retention_chunk_scan(fl(cx[0]), fl(cx[1]), fl(cx[2]), log_gamma[d], s0)
        ol, _ = retention_chunk_scan(fl(lat[0]), fl(lat[1]), fl(lat[2]), log_gamma[d], s_ctx)
        o_lat.append(fl(ol))
        o_ctx.append(fl(oc))
    out_lat = head_groupnorm_gate(o_lat[0] + o_lat[1], lat[3], gn_w)
    out_ctx = head_groupnorm_gate(o_ctx[0] + o_ctx[1], cx[3], gn_w) if need_ctx else None
    return out_lat, out_ctx


def hyena_filters(L, w1, b1, w2, b2, w3):
    pos = jnp.arange(L, dtype=F32)
    t = pos / max(L - 1, 1)
    bands = jnp.linspace(1e-4, HY_BANDS - 1, HY_BANDS, dtype=F32)
    phase = (2.0 * math.pi / L) * pos[:, None] * bands[None, :]
    z = jnp.concatenate([t[:, None], jnp.cos(phase), -jnp.sin(phase)], axis=-1)
    h = jnp.sin(HY_SIN_FREQ * (z @ w1.astype(F32) + b1.astype(F32)))
    h = jnp.sin(HY_SIN_FREQ * (h @ w2.astype(F32) + b2.astype(F32)))
    h = (h @ w3.astype(F32)).reshape(L, HY_ORDER, 2, HY_CH)
    rates = jnp.abs(jnp.linspace(math.log(HY_TARGET) / HY_DECAY_LONG, math.log(HY_TARGET) / HY_DECAY_SHORT,
                                 HY_CH, dtype=F32))
    h = h * jnp.exp(-t[:, None] * rates[None, :])[:, None, None, :]
    h = h / (jnp.sum(jnp.abs(h), axis=(0, 2), keepdims=True) + EPS)
    return h.transpose(1, 2, 0, 3)


def bidir_long_conv(u, h_fwd, h_bwd, skip):
    B, L, C = u.shape
    taps = jnp.concatenate([h_fwd, jnp.zeros((1, C), F32), h_bwd[:0:-1]], axis=0)
    y = jnp.fft.irfft(jnp.fft.rfft(u, n=2 * L, axis=1) * jnp.fft.rfft(taps, axis=0)[None],
                      n=2 * L, axis=1)[:, :L]
    return y + u * skip


def hyena(p, conv_w, w1, b1, w2, b2, w3, skip):
    L = p.shape[1]
    filt = hyena_filters(L, w1, b1, w2, b2, w3)
    u = short_conv(p, conv_w).astype(F32)
    parts = jnp.split(u, HY_ORDER + 1, axis=-1)
    z = parts[0]
    for n in range(HY_ORDER):
        z = parts[n + 1] * bidir_long_conv(z, filt[n, 0], filt[n, 1], skip[n].astype(F32))
    return z


def cd_mixer(p_lat, p_ctx, ret_decay_logit, ret_gn_w, hy_conv_w, hy_f_w1, hy_f_b1, hy_f_w2, hy_f_b2,
             hy_f_w3, hy_bias, need_ctx):
    c_lat, c_ctx = retention(p_lat[..., :RET_COLS], p_ctx[..., :RET_COLS], ret_decay_logit, ret_gn_w, need_ctx)
    hy = lambda p: hyena(p, hy_conv_w, hy_f_w1, hy_f_b1, hy_f_w2, hy_f_b2, hy_f_w3, hy_bias)
    o_lat = jnp.concatenate([c_lat, hy(p_lat[..., RET_COLS:])], axis=-1)
    o_ctx = jnp.concatenate([c_ctx, hy(p_ctx[..., RET_COLS:])], axis=-1) if need_ctx else None
    return o_lat, o_ctx


def _pad_cols(w, mult):
    n = w.shape[-1]
    pad = (-n) % mult
    return jnp.pad(w, ((0, 0), (0, pad))) if pad else w


def kernel(x, c, ctx, c_ctx, mod_w, mod_b, norm_mix_w, norm_ffn_w, ffn_w_in, ffn_w_out,
           ab_w_in, ab_w_out, dn_conv_w, dn_a_log, dn_dt_bias, dn_norm_w, swa_q_norm_w, swa_k_norm_w,
           swa_sink, cd_w_in, cd_w_out, ret_decay_logit, ret_gn_w, hy_conv_w, hy_f_w1, hy_f_b1,
           hy_f_w2, hy_f_b2, hy_f_w3, hy_bias):
    B, L, D = x.shape
    n_ctx = ctx.shape[1]
    rows = L // GRID_W
    ang_row, ang_col = axial_angles(rows)
    c_rows = jnp.zeros((8, D), F32).at[:B].set(c).at[B].set(c_ctx)
    mod_all = modulation(c_rows, mod_w, mod_b)
    hid = ffn_w_out.shape[1]
    h_ctx = ctx
    for layer in range(DEPTH):
        need_ctx = layer != DEPTH - 1
        i = layer // 2
        m = mod_all[layer].reshape(8, 6, D)
        mod = [m[:B, j][:, None, :] for j in range(6)]
        mod_c = [jnp.broadcast_to(m[B, j][None, None, :], (B, 1, D)) for j in range(6)]
        if layer % 2 == 0:
            w = ab_w_in[i]
            w_in = jnp.concatenate([w[:, :4 * DN_WIDTH], w[:, DN_COLS:], w[:, 4 * DN_WIDTH:DN_COLS]], axis=1)
            w_out, cols = ab_w_out[i], ab_w_in.shape[-1]
        else:
            w_in, w_out, cols = cd_w_in[i], cd_w_out[i], cd_w_in.shape[-1]
        w_in_p = _pad_cols(w_in, 128).astype(BF16)
        ncols = w_in_p.shape[1]
        tn = 640 if ncols % 640 == 0 else 512
        p_lat = in_projection(x, norm_mix_w[layer], mod[0], mod[1], w_in_p, 1024, tn)
        p_ctx = in_projection(h_ctx, norm_mix_w[layer], mod_c[0], mod_c[1], w_in_p, n_ctx, tn)
        if layer % 2 == 0:
            swa0 = 4 * DN_WIDTH
            ab0 = swa0 + (SWA_HEADS + 2 * SWA_KV_HEADS) * SWA_DIM
            a_lat, a_ctx = deltanet_pallas(p_lat, p_ctx, dn_conv_w[i], dn_a_log[i], dn_dt_bias[i],
                                           dn_norm_w[i], need_ctx, ab_col=ab0, gate_col=3 * DN_WIDTH)
            b_lat, b_ctx = window_gqa_pallas(p_lat, p_ctx, swa0, axial_rope_tables(rows), swa_q_norm_w[i],
                                             swa_k_norm_w[i], swa_sink[i], need_ctx)
        else:
            a_lat, a_ctx = retention_pallas(p_lat, p_ctx, ret_decay_logit[i], ret_gn_w[i], need_ctx)
            hy = lambda p: hyena(p[..., RET_COLS:cols], hy_conv_w[i], hy_f_w1[i], hy_f_b1[i], hy_f_w2[i],
                                 hy_f_b2[i], hy_f_w3[i], hy_bias[i]).astype(BF16)
            b_lat = hy(p_lat)
            b_ctx = hy(p_ctx) if need_ctx else None
        wo = w_out.astype(BF16)
        wg = ffn_w_in[layer][:, :hid].astype(BF16)
        wu = ffn_w_in[layer][:, hid:].astype(BF16)
        wd = ffn_w_out[layer].astype(BF16)
        x = out_projection_ffn(x, a_lat, b_lat, wo, mod[2], norm_ffn_w[layer], mod[3], mod[4], mod[5],
                               wg, wu, wd, 1024, 256)
        if need_ctx:
            h_ctx = out_projection_ffn(h_ctx, a_ctx, b_ctx, wo, mod_c[2], norm_ffn_w[layer], mod_c[3], mod_c[4],
                                       mod_c[5], wg, wu, wd, n_ctx, 256)
    return x
```

```python
import functools
import math

import jax
import jax.numpy as jnp
import ml_dtypes
import numpy as np
from jax import lax
from jax.experimental import pallas as pl
from jax.experimental.pallas import tpu as pltpu

F32 = jnp.float32
BF16 = jnp.bfloat16
EPS = 1e-6

D_MODEL = 1024
DEPTH = 2
GRID_W = 64

DN_HEADS = 4
DN_DIM = 128
DN_WIDTH = DN_HEADS * DN_DIM
DN_CHUNK = 64
DN_COLS = 4 * DN_WIDTH + 4 * DN_HEADS
SWA_HEADS = 4
SWA_KV_HEADS = 2
SWA_DIM = 128
SWA_WINDOW = 128
SWA_BLOCK = 128
ROPE_THETA = 10000.0
RET_HEADS = 4
RET_DIM = 128
RET_WIDTH = RET_HEADS * RET_DIM
RET_CHUNK = 128
RET_COLS = 4 * RET_WIDTH
HY_CH = 512
HY_ORDER = 2
HY_BANDS = 8
HY_SIN_FREQ = 1.0
HY_TARGET = 1e-2
HY_DECAY_SHORT = 0.3
HY_DECAY_LONG = 1.5

VMEM_LIMIT_BYTES = 56 * 1024 * 1024


def _cparams(sem):
    return pltpu.CompilerParams(dimension_semantics=sem, vmem_limit_bytes=VMEM_LIMIT_BYTES)


def _mod_body(c_ref, w_ref, b_ref, o_ref):
    a = c_ref[...]
    a = a * jax.nn.sigmoid(a)
    o_ref[...] = jnp.dot(a, w_ref[...], preferred_element_type=F32,
                         precision=lax.Precision.HIGHEST) + b_ref[...]


def modulation(c_rows, mod_w, mod_b):
    depth, d, n = mod_w.shape
    tn = 1536
    return pl.pallas_call(
        _mod_body,
        grid=(depth, n // tn),
        in_specs=[pl.BlockSpec((8, d), lambda l, j: (0, 0)),
                  pl.BlockSpec((None, d, tn), lambda l, j: (l, 0, j)),
                  pl.BlockSpec((None, 1, tn), lambda l, j: (l, 0, j))],
        out_specs=pl.BlockSpec((None, 8, tn), lambda l, j: (l, 0, j)),
        out_shape=jax.ShapeDtypeStruct((depth, 8, n), F32),
        compiler_params=_cparams(("parallel", "parallel")),
        name="modulation",
    )(c_rows, mod_w, mod_b.reshape(depth, 1, n))


def _norm_mod(x, nw, shift, scale):
    y = x * lax.rsqrt(jnp.mean(x * x, axis=-1, keepdims=True) + EPS)
    return (y * nw) * (1.0 + scale) + shift


def _inproj_body(x_ref, nw_ref, shift_ref, scale_ref, w_ref, o_ref, h_scr):
    @pl.when(pl.program_id(2) == 0)
    def _():
        h_scr[...] = _norm_mod(x_ref[...], nw_ref[...], shift_ref[...], scale_ref[...]).astype(BF16)

    o_ref[...] = jnp.dot(h_scr[...], w_ref[...], preferred_element_type=F32)


def in_projection(x, nw, shift, scale, w, tm, tn):
    b, l, d = x.shape
    n = w.shape[1]
    return pl.pallas_call(
        _inproj_body,
        grid=(b, l // tm, n // tn),
        in_specs=[pl.BlockSpec((None, tm, d), lambda bi, i, j: (bi, i, 0)),
                  pl.BlockSpec((1, d), lambda bi, i, j: (0, 0)),
                  pl.BlockSpec((None, 1, d), lambda bi, i, j: (bi, 0, 0)),
                  pl.BlockSpec((None, 1, d), lambda bi, i, j: (bi, 0, 0)),
                  pl.BlockSpec((d, tn), lambda bi, i, j: (0, j))],
        out_specs=pl.BlockSpec((None, tm, tn), lambda bi, i, j: (bi, i, j)),
        out_shape=jax.ShapeDtypeStruct((b, l, n), F32),
        scratch_shapes=[pltpu.VMEM((tm, d), BF16)],
        compiler_params=_cparams(("parallel", "parallel", "arbitrary")),
        name="in_projection",
    )(x, nw.reshape(1, d), shift, scale, w)


def _outffn_body(x_ref, oa_ref, ob_ref, wo_ref, g2_ref, nw_ref, sh_ref, sc_ref, g5_ref,
                 wg_ref, wu_ref, wd_ref, y_ref, x1_scr, h_scr, acc_scr):
    k = pl.program_id(2)

    @pl.when(k == 0)
    def _():
        wa = oa_ref.shape[1]
        mix = (jnp.dot(oa_ref[...], wo_ref[:wa, :], preferred_element_type=F32)
               + jnp.dot(ob_ref[...], wo_ref[wa:, :], preferred_element_type=F32))
        x1 = x_ref[...] + g2_ref[...] * mix
        x1_scr[...] = x1
        h_scr[...] = _norm_mod(x1, nw_ref[...], sh_ref[...], sc_ref[...]).astype(BF16)
        acc_scr[...] = jnp.zeros_like(acc_scr)

    h = h_scr[...]
    g = jnp.dot(h, wg_ref[...], preferred_element_type=F32)
    u = jnp.dot(h, wu_ref[...], preferred_element_type=F32)
    a = (g * jax.nn.sigmoid(g) * u).astype(BF16)
    acc_scr[...] += jnp.dot(a, wd_ref[...], preferred_element_type=F32)

    @pl.when(k == pl.num_programs(2) - 1)
    def _():
        y_ref[...] = x1_scr[...] + g5_ref[...] * acc_scr[...]


def out_projection_ffn(x, oa, ob, wo, g2, nw, shift, scale, g5, wg, wu, wd, tm, th):
    b, l, d = x.shape
    wa, wb = oa.shape[-1], ob.shape[-1]
    hid = wg.shape[1]
    tok = lambda bi, i, k: (bi, i, 0)
    vec = lambda bi, i, k: (bi, 0, 0)
    return pl.pallas_call(
        _outffn_body,
        grid=(b, l // tm, hid // th),
        in_specs=[pl.BlockSpec((None, tm, d), tok),
                  pl.BlockSpec((None, tm, wa), tok),
                  pl.BlockSpec((None, tm, wb), tok),
                  pl.BlockSpec((d, d), lambda bi, i, k: (0, 0)),
                  pl.BlockSpec((None, 1, d), vec),
                  pl.BlockSpec((1, d), lambda bi, i, k: (0, 0)),
                  pl.BlockSpec((None, 1, d), vec),
                  pl.BlockSpec((None, 1, d), vec),
                  pl.BlockSpec((None, 1, d), vec),
                  pl.BlockSpec((d, th), lambda bi, i, k: (0, k)),
                  pl.BlockSpec((d, th), lambda bi, i, k: (0, k)),
                  pl.BlockSpec((th, d), lambda bi, i, k: (k, 0))],
        out_specs=pl.BlockSpec((None, tm, d), tok),
        out_shape=jax.ShapeDtypeStruct((b, l, d), F32),
        scratch_shapes=[pltpu.VMEM((tm, d), F32), pltpu.VMEM((tm, d), BF16), pltpu.VMEM((tm, d), F32)],
        compiler_params=_cparams(("parallel", "parallel", "arbitrary")),
        name="out_projection_ffn",
    )(x, oa, ob, wo, g2, nw.reshape(1, d), shift, scale, g5, wg, wu, wd)


NEG_BIG = -1e30


def _head_rms(x, w):
    return x * lax.rsqrt(jnp.mean(x * x, axis=-1, keepdims=True) + EPS) * w


def _swa_prep_body(*refs, use_rope):
    if use_rope:
        q_ref, k_ref, v_ref, qw_ref, kw_ref, cos_ref, sa_ref, sb_ref, qo_ref, ko_ref, vo_ref = refs
    else:
        q_ref, k_ref, v_ref, qw_ref, kw_ref, qo_ref, ko_ref, vo_ref = refs

    def prep(x, w):
        y = _head_rms(x, w)
        if use_rope:
            y = (y * cos_ref[...] + pltpu.roll(y, SWA_DIM - 32, 1) * sa_ref[...]
                 + pltpu.roll(y, 32, 1) * sb_ref[...])
        return y.astype(BF16)

    for h in range(SWA_HEADS):
        sl = slice(h * SWA_DIM, (h + 1) * SWA_DIM)
        qo_ref[:, sl] = prep(q_ref[:, sl], qw_ref[...])
    for h in range(SWA_KV_HEADS):
        sl = slice(h * SWA_DIM, (h + 1) * SWA_DIM)
        ko_ref[:, sl] = prep(k_ref[:, sl], kw_ref[...])
    vo_ref[...] = v_ref[...].astype(BF16)


def swa_prep(p, col0, qw, kw, rope, tl):
    b, l, _ = p.shape
    nq, nk = SWA_HEADS * SWA_DIM, SWA_KV_HEADS * SWA_DIM
    tok = lambda bi, i: (bi, i, 0)
    in_specs = [pl.BlockSpec((None, tl, nq), lambda bi, i: (bi, i, col0 // nq)),
                pl.BlockSpec((None, tl, nk), lambda bi, i: (bi, i, (col0 + nq) // nk)),
                pl.BlockSpec((None, tl, nk), lambda bi, i: (bi, i, (col0 + nq + nk) // nk)),
                pl.BlockSpec((1, SWA_DIM), lambda bi, i: (0, 0)),
                pl.BlockSpec((1, SWA_DIM), lambda bi, i: (0, 0))]
    args = [p, p, p, qw.reshape(1, SWA_DIM), kw.reshape(1, SWA_DIM)]
    if rope is not None:
        in_specs += [pl.BlockSpec((tl, SWA_DIM), lambda bi, i: (i, 0))] * 3
        args += list(rope)
    return pl.pallas_call(
        functools.partial(_swa_prep_body, use_rope=rope is not None),
        grid=(b, l // tl),
        in_specs=in_specs,
        out_specs=[pl.BlockSpec((None, tl, nq), tok), pl.BlockSpec((None, tl, nk), tok),
                   pl.BlockSpec((None, tl, nk), tok)],
        out_shape=[jax.ShapeDtypeStruct((b, l, nq), BF16), jax.ShapeDtypeStruct((b, l, nk), BF16),
                   jax.ShapeDtypeStruct((b, l, nk), BF16)],
        compiler_params=_cparams(("parallel", "parallel")),
        name="swa_prep",
    )(*args)


def _nt_dot(a, b):
    return lax.dot_general(a, b, (((1,), (1,)), ((), ())), preferred_element_type=F32)


def _swa_attend(q, keys, vals, masks, sink_col):
    scale = SWA_DIM ** -0.5
    scores = []
    for kk, mask in zip(keys, masks):
        s = _nt_dot(q, kk) * scale
        if mask is not None:
            s = jnp.where(mask, s, NEG_BIG)
        scores.append(s)
    m = sink_col
    for s in scores:
        m = jnp.maximum(m, jnp.max(s, axis=-1, keepdims=True))
    den = jnp.exp(sink_col - m)
    acc = None
    for s, vv in zip(scores, vals):
        pr = jnp.exp(s - m)
        den = den + jnp.sum(pr, axis=-1, keepdims=True)
        o = jnp.dot(pr.astype(BF16), vv, preferred_element_type=F32)
        acc = o if acc is None else acc + o
    return acc / den


def _swa_attn_body(sink_ref, q_ref, kp_ref, km_ref, kn_ref, vp_ref, vm_ref, vn_ref, kc_ref, vc_ref,
                   o_ref, *, nb):
    i = pl.program_id(1)
    n_blocks = pl.num_programs(1) * nb
    w, d = SWA_BLOCK, SWA_DIM
    grp = SWA_HEADS // SWA_KV_HEADS
    rows = lax.broadcasted_iota(jnp.int32, (grp * w, w), 0)
    qi = rows % w
    kj = lax.broadcasted_iota(jnp.int32, (grp * w, w), 1)
    for r in range(nb):
        blk = i * nb + r
        mask_prev = (kj >= qi) & (blk > 0)
        mask_next = (kj <= qi) & (blk < n_blocks - 1)
        rs = slice(r * w, (r + 1) * w)
        for h in range(SWA_KV_HEADS):
            hs = slice(h * d, (h + 1) * d)
            q = jnp.concatenate([q_ref[rs, (h * grp + g) * d:(h * grp + g + 1) * d] for g in range(grp)], axis=0)
            sink_col = jnp.zeros((grp * w, 1), F32)
            for g in range(grp):
                sink_col = jnp.where((rows[:, :1] // w) == g, sink_ref[h * grp + g], sink_col)
            if r > 0:
                k_prev, v_prev = km_ref[(r - 1) * w:r * w, hs], vm_ref[(r - 1) * w:r * w, hs]
            else:
                k_prev, v_prev = kp_ref[:, hs], vp_ref[:, hs]
            if r < nb - 1:
                k_next, v_next = km_ref[(r + 1) * w:(r + 2) * w, hs], vm_ref[(r + 1) * w:(r + 2) * w, hs]
            else:
                k_next, v_next = kn_ref[:, hs], vn_ref[:, hs]
            o = _swa_attend(q, [k_prev, km_ref[rs, hs], k_next, kc_ref[:, hs]],
                            [v_prev, vm_ref[rs, hs], v_next, vc_ref[:, hs]],
                            [mask_prev, None, mask_next, None], sink_col)
            for g in range(grp):
                o_ref[rs, (h * grp + g) * d:(h * grp + g + 1) * d] = o[g * w:(g + 1) * w].astype(o_ref.dtype)


def swa_attention(q, k, v, kc, vc, sink, nb):
    b, l, nq = q.shape
    nk = k.shape[-1]
    n_ctx = kc.shape[1]
    w = SWA_BLOCK
    tq = nb * w
    last = l // w - 1
    main = lambda bi, i, s: (bi, i, 0)
    prev = lambda bi, i, s: (bi, jnp.maximum(i * nb - 1, 0), 0)
    nxt = lambda bi, i, s: (bi, jnp.minimum((i + 1) * nb, last), 0)
    cx = lambda bi, i, s: (bi, 0, 0)
    return pl.pallas_call(
        functools.partial(_swa_attn_body, nb=nb),
        grid_spec=pltpu.PrefetchScalarGridSpec(
            num_scalar_prefetch=1,
            grid=(b, l // tq),
            in_specs=[pl.BlockSpec((None, tq, nq), main),
                      pl.BlockSpec((None, w, nk), prev), pl.BlockSpec((None, tq, nk), main),
                      pl.BlockSpec((None, w, nk), nxt),
                      pl.BlockSpec((None, w, nk), prev), pl.BlockSpec((None, tq, nk), main),
                      pl.BlockSpec((None, w, nk), nxt),
                      pl.BlockSpec((None, n_ctx, nk), cx), pl.BlockSpec((None, n_ctx, nk), cx)],
            out_specs=pl.BlockSpec((None, tq, nq), main)),
        out_shape=jax.ShapeDtypeStruct((b, l, nq), BF16),
        compiler_params=_cparams(("parallel", "parallel")),
        name="swa_attention",
    )(sink, q, k, k, k, v, v, v, kc, vc)


def _ctx_attn_body(sink_ref, q_ref, kc_ref, vc_ref, o_ref):
    n_ctx, d = q_ref.shape[0], SWA_DIM
    grp = SWA_HEADS // SWA_KV_HEADS
    rows = lax.broadcasted_iota(jnp.int32, (grp * n_ctx, 1), 0)
    for h in range(SWA_KV_HEADS):
        hs = slice(h * d, (h + 1) * d)
        q = jnp.concatenate([q_ref[:, (h * grp + g) * d:(h * grp + g + 1) * d] for g in range(grp)], axis=0)
        sink_col = jnp.zeros((grp * n_ctx, 1), F32)
        for g in range(grp):
            sink_col = jnp.where((rows // n_ctx) == g, sink_ref[h * grp + g], sink_col)
        o = _swa_attend(q, [kc_ref[:, hs]], [vc_ref[:, hs]], [None], sink_col)
        for g in range(grp):
            o_ref[:, (h * grp + g) * d:(h * grp + g + 1) * d] = o[g * n_ctx:(g + 1) * n_ctx].astype(o_ref.dtype)


def ctx_attention(qc, kc, vc, sink):
    b, n_ctx, nq = qc.shape
    nk = kc.shape[-1]
    cx = lambda bi, s: (bi, 0, 0)
    return pl.pallas_call(
        _ctx_attn_body,
        grid_spec=pltpu.PrefetchScalarGridSpec(
            num_scalar_prefetch=1, grid=(b,),
            in_specs=[pl.BlockSpec((None, n_ctx, nq), cx), pl.BlockSpec((None, n_ctx, nk), cx),
                      pl.BlockSpec((None, n_ctx, nk), cx)],
            out_specs=pl.BlockSpec((None, n_ctx, nq), cx)),
        out_shape=jax.ShapeDtypeStruct((b, n_ctx, nq), BF16),
        compiler_params=_cparams(("parallel",)),
        name="ctx_attention",
    )(sink, qc, kc, vc)


def axial_rope_tables(rows):
    ang_row, ang_col = axial_angles(rows)
    zero = jnp.zeros_like(ang_row)
    cos = jnp.concatenate([jnp.cos(ang_row)] * 2 + [jnp.cos(ang_col)] * 2, axis=-1)
    sa = jnp.concatenate([-jnp.sin(ang_row), zero, -jnp.sin(ang_col), zero], axis=-1)
    sb = jnp.concatenate([zero, jnp.sin(ang_row), zero, jnp.sin(ang_col)], axis=-1)
    return cos, sa, sb


def window_gqa_pallas(p_lat, p_ctx, col0, rope, q_norm_w, k_norm_w, sink, need_ctx):
    ql, kl, vl = swa_prep(p_lat, col0, q_norm_w, k_norm_w, rope, 1024)
    qc, kc, vc = swa_prep(p_ctx, col0, q_norm_w, k_norm_w, None, p_ctx.shape[1])
    o_lat = swa_attention(ql, kl, vl, kc, vc, sink, 4)
    o_ctx = ctx_attention(qc, kc, vc, sink) if need_ctx else None
    return o_lat, o_ctx


def _tn_dot(a, b):
    return lax.dot_general(a, b, (((0,), (0,)), ((), ())), preferred_element_type=F32)


def _log_sigmoid(x):
    return jnp.minimum(x, 0.0) - jnp.log(1.0 + jnp.exp(-jnp.abs(x)))


RET_STREAMS = 2 * RET_HEADS


def _ret_scan_body(logit_ref, qf_ref, kf_ref, vf_ref, cosf_ref, sinf_ref, qb_ref, kb_ref, vb_ref, cosb_ref,
                   sinb_ref, s0_ref, of_ref, ob_ref, sf_ref, s_scr):
    n = pl.program_id(1)
    c, dh = RET_CHUNK, RET_DIM

    @pl.when(n == 0)
    def _():
        s_scr[...] = s0_ref[...]

    ii = lax.broadcasted_iota(jnp.int32, (c, c), 0)
    jj = lax.broadcasted_iota(jnp.int32, (c, c), 1)
    i1 = lax.broadcasted_iota(jnp.int32, (c, 1), 0)
    rel = {True: (ii - jj).astype(F32), False: (jj - ii).astype(F32)}
    pos = {True: i1.astype(F32), False: (c - 1 - i1).astype(F32)}
    streams = [(d == 0, h) for d in range(2) for h in range(RET_HEADS)]
    srcs = {True: (qf_ref, kf_ref, vf_ref, cosf_ref, sinf_ref), False: (qb_ref, kb_ref, vb_ref, cosb_ref, sinb_ref)}

    def rot(x, fwd):
        return x * srcs[fwd][3][...] + pltpu.roll(x, dh // 2, 1) * srcs[fwd][4][...]

    hs = lambda h: slice(h * dh, (h + 1) * dh)
    lg = [_log_sigmoid(jnp.full((1, 1), logit_ref[r], F32)) for r in range(RET_STREAMS)]
    q = [rot(srcs[f][0][:, hs(h)], f) for f, h in streams]
    k = [rot(srcs[f][1][:, hs(h)], f) * dh ** -0.5 for f, h in streams]
    v = [srcs[f][2][:, hs(h)].astype(BF16) for f, h in streams]
    dmat = [jnp.where(rel[f] >= 0, jnp.exp(lg[r] * jnp.maximum(rel[f], 0.0)), 0.0)
            for r, (f, _) in enumerate(streams)]
    scores = [_nt_dot(q[r].astype(BF16), k[r].astype(BF16)) * dmat[r] for r in range(RET_STREAMS)]
    q_dec = [(q[r] * jnp.exp(lg[r] * (pos[f] + 1.0))).astype(BF16) for r, (f, _) in enumerate(streams)]
    k_dec = [(k[r] * jnp.exp(lg[r] * (c - 1.0 - pos[f]))).astype(BF16) for r, (f, _) in enumerate(streams)]
    s = [s_scr[r] for r in range(RET_STREAMS)]
    o = [jnp.dot(scores[r].astype(BF16), v[r], preferred_element_type=F32)
         + jnp.dot(q_dec[r], s[r].astype(BF16), preferred_element_type=F32) for r in range(RET_STREAMS)]
    s_new = [s[r] * jnp.exp(lg[r] * c) + _tn_dot(k_dec[r], v[r]) for r in range(RET_STREAMS)]
    for r, (f, h) in enumerate(streams):
        (of_ref if f else ob_ref)[:, hs(h)] = o[r]
        s_scr[r] = s_new[r]

    @pl.when(n == pl.num_programs(1) - 1)
    def _():
        sf_ref[...] = s_scr[...]


def retention_scan(p, col0, decay_logit, cos, sin, s0):
    b, l, _ = p.shape
    c, dh, w = RET_CHUNK, RET_DIM, RET_WIDTH
    n = l // c
    cb = col0 // w
    fcol = lambda off: (lambda bi, ni, s: (bi, ni, cb + off))
    bcol = lambda off: (lambda bi, ni, s: (bi, n - 1 - ni, cb + off))
    ftab = lambda bi, ni, s: (ni, 0)
    btab = lambda bi, ni, s: (n - 1 - ni, 0)
    st = lambda bi, ni, s: (bi, 0, 0, 0)
    tok = lambda col: [pl.BlockSpec((None, c, w), col(j)) for j in range(3)]
    return pl.pallas_call(
        _ret_scan_body,
        grid_spec=pltpu.PrefetchScalarGridSpec(
            num_scalar_prefetch=1,
            grid=(b, n),
            in_specs=tok(fcol) + [pl.BlockSpec((c, dh), ftab)] * 2 + tok(bcol) + [pl.BlockSpec((c, dh), btab)] * 2
            + [pl.BlockSpec((None, RET_STREAMS, dh, dh), st)],
            out_specs=[pl.BlockSpec((None, c, w), lambda bi, ni, s: (bi, ni, 0)),
                       pl.BlockSpec((None, c, w), lambda bi, ni, s: (bi, n - 1 - ni, 0)),
                       pl.BlockSpec((None, RET_STREAMS, dh, dh), st)],
            scratch_shapes=[pltpu.VMEM((RET_STREAMS, dh, dh), F32)]),
        out_shape=[jax.ShapeDtypeStruct((b, l, w), F32), jax.ShapeDtypeStruct((b, l, w), F32),
                   jax.ShapeDtypeStruct((b, RET_STREAMS, dh, dh), F32)],
        compiler_params=_cparams(("parallel", "arbitrary")),
        name="retention_scan",
    )(decay_logit.reshape(-1), p, p, p, cos, sin, p, p, p, cos, sin, s0)


def _ret_final_body(of_ref, ob_ref, g_ref, w_ref, y_ref):
    for h in range(RET_HEADS):
        sl = slice(h * RET_DIM, (h + 1) * RET_DIM)
        o = of_ref[:, sl] + ob_ref[:, sl]
        mu = jnp.mean(o, axis=-1, keepdims=True)
        var = jnp.mean(jnp.square(o - mu), axis=-1, keepdims=True)
        y = (o - mu) * lax.rsqrt(var + EPS) * w_ref[:, sl]
        g = g_ref[:, sl]
        y_ref[:, sl] = (y * (g * jax.nn.sigmoid(g))).astype(y_ref.dtype)


def retention_finalize(o_f, o_b, p, gate_col, gn_w, tl):
    b, l, w = o_f.shape
    tok = lambda bi, i: (bi, i, 0)
    return pl.pallas_call(
        _ret_final_body,
        grid=(b, l // tl),
        in_specs=[pl.BlockSpec((None, tl, w), tok),
                  pl.BlockSpec((None, tl, w), tok),
                  pl.BlockSpec((None, tl, w), lambda bi, i: (bi, i, gate_col // w)),
                  pl.BlockSpec((1, w), lambda bi, i: (0, 0))],
        out_specs=pl.BlockSpec((None, tl, w), tok),
        out_shape=jax.ShapeDtypeStruct((b, l, w), BF16),
        compiler_params=_cparams(("parallel", "parallel")),
        name="retention_finalize",
    )(o_f, o_b, p, gn_w.reshape(1, w))


def retention_rope_tables(l):
    inv = ROPE_THETA ** (-jnp.linspace(0.0, 1.0, RET_DIM // 2, dtype=F32))
    ang = jnp.arange(l, dtype=F32)[:, None] * inv
    return (jnp.concatenate([jnp.cos(ang), jnp.cos(ang)], axis=-1),
            jnp.concatenate([-jnp.sin(ang), jnp.sin(ang)], axis=-1))


def retention_pallas(p_lat, p_ctx, decay_logit, gn_w, need_ctx, col0=0):
    b, l, _ = p_lat.shape
    n_ctx = p_ctx.shape[1]
    gate_col = col0 + 3 * RET_WIDTH
    s0 = jnp.zeros((b, RET_STREAMS, RET_DIM, RET_DIM), F32)
    oc_f, oc_b, s_ctx = retention_scan(p_ctx, col0, decay_logit, *retention_rope_tables(n_ctx), s0)
    ol_f, ol_b, _ = retention_scan(p_lat, col0, decay_logit, *retention_rope_tables(l), s_ctx)
    out_lat = retention_finalize(ol_f, ol_b, p_lat, gate_col, gn_w, 1024)
    out_ctx = retention_finalize(oc_f, oc_b, p_ctx, gate_col, gn_w, n_ctx) if need_ctx else None
    return out_lat, out_ctx


DN_BLOCK = 128
DN_STREAMS = 2 * DN_HEADS


def _shift_rows(x, prev_row, next_row):
    n = x.shape[0]
    r = lax.broadcasted_iota(jnp.int32, (n, 1), 0)
    x_prev = jnp.where(r == 0, prev_row, pltpu.roll(x, 1, 0))
    x_next = jnp.where(r == n - 1, next_row, pltpu.roll(x, n - 1, 0))
    return x_prev, x_next


def _conv3(x_ref, xp_ref, xn_ref, w_ref):
    i = pl.program_id(1)
    prev_row = jnp.where(i > 0, xp_ref[7:8, :], 0.0)
    next_row = jnp.where(i < pl.num_programs(1) - 1, xn_ref[0:1, :], 0.0)
    x = x_ref[...]
    x_prev, x_next = _shift_rows(x, prev_row, next_row)
    return x_prev * w_ref[0:1, :] + x * w_ref[1:2, :] + x_next * w_ref[2:3, :]


def _dn_prep_body(x_ref, xp_ref, xn_ref, ab_ref, cw_ref, alog_ref, dtb_ref,
                  q_ref, k_ref, v_ref, gb_ref, gam_ref, gamt_ref):
    y = _conv3(x_ref, xp_ref, xn_ref, cw_ref)
    y = y * jax.nn.sigmoid(y)
    for h in range(DN_HEADS):
        for part, (ref, mul) in enumerate(((q_ref, DN_DIM ** -0.5), (k_ref, 1.0))):
            sl = slice(part * DN_WIDTH + h * DN_DIM, part * DN_WIDTH + (h + 1) * DN_DIM)
            t = y[:, sl]
            t = t * lax.rsqrt(jnp.sum(t * t, axis=-1, keepdims=True) + EPS)
            ref[:, h * DN_DIM:(h + 1) * DN_DIM] = t * mul if mul != 1.0 else t
    v_ref[...] = y[:, 2 * DN_WIDTH:]

    ab = ab_ref[...]
    z = ab + dtb_ref[...]
    softplus = jnp.maximum(z, 0.0) + jnp.log(1.0 + jnp.exp(-jnp.abs(z)))
    g = -jnp.exp(alog_ref[...]) * softplus
    lane = lax.broadcasted_iota(jnp.int32, ab.shape, 1)
    gb_ref[...] = jnp.where(lane < DN_STREAMS, g, jax.nn.sigmoid(ab))

    c = DN_BLOCK
    ii = lax.broadcasted_iota(jnp.int32, (c, c), 0)
    jj = lax.broadcasted_iota(jnp.int32, (c, c), 1)
    tri_f = (ii >= jj).astype(F32)
    tri_b = (ii <= jj).astype(F32)
    lane_c = lax.broadcasted_iota(jnp.int32, (c, ab.shape[1]), 1)
    for n in range(x_ref.shape[0] // c):
        gc = g[n * c:(n + 1) * c, :]
        cf = jnp.dot(tri_f, gc, preferred_element_type=F32, precision=lax.Precision.HIGHEST)
        cb = jnp.dot(tri_b, gc, preferred_element_type=F32, precision=lax.Precision.HIGHEST)
        gam = jnp.where(lane_c < DN_HEADS, cf, cb)
        gam_ref[n * c:(n + 1) * c, :] = gam
        gamt_ref[n] = gam.T[:DN_STREAMS, :]


def dn_prep(p, ab_col, conv_w, a_log, dt_bias, tl):
    b, l, _ = p.shape
    c3 = 3 * DN_WIDTH
    nblk8 = l // 8
    lanes = 128
    pad8 = lambda a: jnp.pad(a.reshape(1, DN_STREAMS).astype(F32), ((0, 0), (0, lanes - DN_STREAMS)))
    tok = lambda bi, i: (bi, i, 0)
    return pl.pallas_call(
        _dn_prep_body,
        grid=(b, l // tl),
        in_specs=[pl.BlockSpec((None, tl, c3), tok),
                  pl.BlockSpec((None, 8, c3), lambda bi, i: (bi, jnp.maximum(i * (tl // 8) - 1, 0), 0)),
                  pl.BlockSpec((None, 8, c3), lambda bi, i: (bi, jnp.minimum((i + 1) * (tl // 8), nblk8 - 1), 0)),
                  pl.BlockSpec((None, tl, lanes), lambda bi, i: (bi, i, ab_col // lanes)),
                  pl.BlockSpec((3, c3), lambda bi, i: (0, 0)),
                  pl.BlockSpec((1, lanes), lambda bi, i: (0, 0)),
                  pl.BlockSpec((1, lanes), lambda bi, i: (0, 0))],
        out_specs=[pl.BlockSpec((None, tl, DN_WIDTH), tok)] * 3
        + [pl.BlockSpec((None, tl, lanes), tok)] * 2
        + [pl.BlockSpec((None, tl // DN_BLOCK, DN_STREAMS, DN_BLOCK), lambda bi, i: (bi, i, 0, 0))],
        out_shape=[jax.ShapeDtypeStruct((b, l, DN_WIDTH), F32)] * 3
        + [jax.ShapeDtypeStruct((b, l, lanes), F32)] * 2
        + [jax.ShapeDtypeStruct((b, l // DN_BLOCK, DN_STREAMS, DN_BLOCK), F32)],
        compiler_params=_cparams(("parallel", "parallel")),
        name="dn_prep",
    )(p, p, p, p, conv_w, pad8(a_log), pad8(dt_bias))


def _split_bf16(a):
    hi = a.astype(BF16)
    return hi, (a - hi.astype(F32)).astype(BF16)


def _dot3(a, b):
    a_hi, a_lo = _split_bf16(a)
    b_hi, b_lo = _split_bf16(b)
    d = lambda x, y: jnp.dot(x, y, preferred_element_type=F32)
    return d(a_hi, b_hi) + (d(a_hi, b_lo) + d(a_lo, b_hi))


def _unit_triangular_inverses(ms, ii, jj):
    c = ms[0].shape[0]
    eye = (ii == jj).astype(F32)
    pair = (ii // 2) == (jj // 2)
    ts = [eye - jnp.where(pair, m, 0.0) for m in ms]
    s = 2
    while s < c:
        sub = ((ii // (2 * s)) == (jj // (2 * s))) & ((ii // s) != (jj // s))
        xs = [_dot3(jnp.where(sub, m, 0.0), t) for m, t in zip(ms, ts)]
        ts = [t - _dot3(t, x) for t, x in zip(ts, xs)]
        s *= 2
    return ts


def _dn_scan_body(qf_ref, kf_ref, vf_ref, gbf_ref, gamf_ref, gamtf_ref,
                  qb_ref, kb_ref, vb_ref, gbb_ref, gamb_ref, gamtb_ref, s0_ref,
                  of_ref, ob_ref, sf_ref, s_scr):
    n = pl.program_id(1)

    @pl.when(n == 0)
    def _():
        s_scr[...] = s0_ref[...]

    c, dh = DN_BLOCK, DN_DIM
    ii = lax.broadcasted_iota(jnp.int32, (c, c), 0)
    jj = lax.broadcasted_iota(jnp.int32, (c, c), 1)
    incl = {True: ii >= jj, False: ii <= jj}
    strict = {True: ii > jj, False: ii < jj}
    srcs = {True: (qf_ref, kf_ref, vf_ref, gbf_ref, gamf_ref, gamtf_ref),
            False: (qb_ref, kb_ref, vb_ref, gbb_ref, gamb_ref, gamtb_ref)}
    streams = [(d == 0, h) for d in range(2) for h in range(DN_HEADS)]
    rng = range(DN_STREAMS)
    hs = lambda h: slice(h * dh, (h + 1) * dh)

    q = [srcs[f][0][:, hs(h)] for f, h in streams]
    k = [srcs[f][1][:, hs(h)] for f, h in streams]
    v = [srcs[f][2][:, hs(h)] for f, h in streams]
    beta = [srcs[f][3][:, DN_STREAMS + r:DN_STREAMS + r + 1] for r, (f, _) in enumerate(streams)]
    gcol = [srcs[f][4][:, r:r + 1] for r, (f, _) in enumerate(streams)]
    grow = [srcs[f][5][r:r + 1, :] for r, (f, _) in enumerate(streams)]
    g_last = [gcol[r][c - 1:c, :] if f else gcol[r][0:1, :] for r, (f, _) in enumerate(streams)]

    e = [jnp.exp(jnp.where(incl[f], gcol[r] - grow[r], 0.0)) for r, (f, _) in enumerate(streams)]
    kb = [k[r] * beta[r] for r in rng]
    k16 = [k[r].astype(BF16) for r in rng]
    m = [_nt_dot(kb[r].astype(BF16), k16[r]) * jnp.where(strict[f], e[r], 0.0) for r, (f, _) in enumerate(streams)]
    attn = [(_nt_dot(q[r].astype(BF16), k16[r]) * jnp.where(incl[f], e[r], 0.0)).astype(BF16)
            for r, (f, _) in enumerate(streams)]
    t = _unit_triangular_inverses(m, ii, jj)
    eg = [jnp.exp(gcol[r]) for r in rng]
    sol = [_dot3(t[r], jnp.concatenate([v[r] * beta[r], kb[r] * eg[r]], axis=-1)) for r in rng]
    s = [s_scr[r] for r in rng]
    s16 = [s[r].astype(BF16) for r in rng]
    v_new = [(sol[r][:, :dh] - jnp.dot(sol[r][:, dh:].astype(BF16), s16[r], preferred_element_type=F32)).astype(BF16)
             for r in rng]
    o = [jnp.dot((q[r] * eg[r]).astype(BF16), s16[r], preferred_element_type=F32)
         + jnp.dot(attn[r], v_new[r], preferred_element_type=F32) for r in rng]
    s_new = [s[r] * jnp.exp(g_last[r]) + _tn_dot((k[r] * jnp.exp(g_last[r] - gcol[r])).astype(BF16), v_new[r])
             for r in rng]
    for r, (f, h) in enumerate(streams):
        (of_ref if f else ob_ref)[:, hs(h)] = o[r]
        s_scr[r] = s_new[r]

    @pl.when(n == pl.num_programs(1) - 1)
    def _():
        sf_ref[...] = s_scr[...]


def dn_scan(q, k, v, gb, gam, gamt, s0):
    b, l, w = q.shape
    c = DN_BLOCK
    n = l // c
    lanes = gb.shape[-1]
    f3 = lambda bi, ni: (bi, ni, 0)
    b3 = lambda bi, ni: (bi, n - 1 - ni, 0)
    f4 = lambda bi, ni: (bi, ni, 0, 0)
    b4 = lambda bi, ni: (bi, n - 1 - ni, 0, 0)
    st = lambda bi, ni: (bi, 0, 0, 0)

    def specs(m3, m4):
        return [pl.BlockSpec((None, c, w), m3)] * 3 + [pl.BlockSpec((None, c, lanes), m3)] * 2 + [
            pl.BlockSpec((None, None, DN_STREAMS, c), m4)]

    return pl.pallas_call(
        _dn_scan_body,
        grid=(b, n),
        in_specs=specs(f3, f4) + specs(b3, b4) + [pl.BlockSpec((None, DN_STREAMS, DN_DIM, DN_DIM), st)],
        out_specs=[pl.BlockSpec((None, c, w), f3), pl.BlockSpec((None, c, w), b3),
                   pl.BlockSpec((None, DN_STREAMS, DN_DIM, DN_DIM), st)],
        out_shape=[jax.ShapeDtypeStruct((b, l, w), F32), jax.ShapeDtypeStruct((b, l, w), F32),
                   jax.ShapeDtypeStruct((b, DN_STREAMS, DN_DIM, DN_DIM), F32)],
        scratch_shapes=[pltpu.VMEM((DN_STREAMS, DN_DIM, DN_DIM), F32)],
        compiler_params=_cparams(("parallel", "arbitrary")),
        name="dn_scan",
    )(q, k, v, gb, gam, gamt, q, k, v, gb, gam, gamt, s0)


def _dn_final_body(of_ref, ob_ref, g_ref, w_ref, y_ref):
    for h in range(DN_HEADS):
        sl = slice(h * DN_DIM, (h + 1) * DN_DIM)
        o = of_ref[:, sl] + ob_ref[:, sl]
        y = _head_rms(o, w_ref[...])
        g = g_ref[:, sl]
        y_ref[:, sl] = (y * (g * jax.nn.sigmoid(g))).astype(y_ref.dtype)


def dn_finalize(o_f, o_b, p, gate_col, norm_w, tl):
    b, l, w = o_f.shape
    tok = lambda bi, i: (bi, i, 0)
    return pl.pallas_call(
        _dn_final_body,
        grid=(b, l // tl),
        in_specs=[pl.BlockSpec((None, tl, w), tok), pl.BlockSpec((None, tl, w), tok),
                  pl.BlockSpec((None, tl, w), lambda bi, i: (bi, i, gate_col // w)),
                  pl.BlockSpec((1, DN_DIM), lambda bi, i: (0, 0))],
        out_specs=pl.BlockSpec((None, tl, w), tok),
        out_shape=jax.ShapeDtypeStruct((b, l, w), BF16),
        compiler_params=_cparams(("parallel", "parallel")),
        name="dn_finalize",
    )(o_f, o_b, p, norm_w.reshape(1, DN_DIM))


def deltanet_pallas(p_lat, p_ctx, conv_w, a_log, dt_bias, norm_w, need_ctx, ab_col=3072, gate_col=1536):
    b = p_lat.shape[0]
    n_ctx = p_ctx.shape[1]
    s0 = jnp.zeros((b, DN_STREAMS, DN_DIM, DN_DIM), F32)
    fc = dn_prep(p_ctx, ab_col, conv_w, a_log, dt_bias, n_ctx)
    oc_f, oc_b, s_ctx = dn_scan(*fc, s0)
    fl = dn_prep(p_lat, ab_col, conv_w, a_log, dt_bias, 1024)
    ol_f, ol_b, _ = dn_scan(*fl, s_ctx)
    out_lat = dn_finalize(ol_f, ol_b, p_lat, gate_col, norm_w, 1024)
    out_ctx = dn_finalize(oc_f, oc_b, p_ctx, gate_col, norm_w, n_ctx) if need_ctx else None
    return out_lat, out_ctx


def _hy_factors(n):
    n2 = 256 if n >= 32768 else 128
    return n // n2, n2


def _bf16_pair(m):
    hi = m.astype(ml_dtypes.bfloat16)
    lo = (m - hi.astype(np.float64)).astype(ml_dtypes.bfloat16)
    return hi, lo


def _hy_constants(l):
    n = 2 * l
    n1, n2 = _hy_factors(n)
    a1 = 2.0 * np.pi * np.outer(np.arange(n1), np.arange(n1)) / n1
    c1, s1 = np.cos(a1), np.sin(a1)
    h = n1 // 2
    rows_fwd = np.block([[c1[:, :h], s1[:, :h]], [-s1[:, :h], c1[:, :h]]])
    rows_taps = np.concatenate([c1, -s1], axis=0)
    rows_inv = np.block([[c1[:h], -s1[:h]], [s1[:h], c1[:h]]]) / n
    a2 = 2.0 * np.pi * np.outer(np.arange(n2), np.arange(n2)) / n2
    c2, s2 = np.cos(a2), np.sin(a2)
    slab_fwd = np.block([[c2, s2], [-s2, c2]])
    slab_inv = np.block([[c2, -s2], [s2, c2]])
    th = 2.0 * np.pi * np.outer(np.arange(n1), np.arange(n2)) / n
    lanes = lambda t: jnp.broadcast_to(jnp.asarray(t, F32)[:, :, None], (n1, n2, 128))
    return dict(n1=n1, n2=n2, rows_fwd=_bf16_pair(rows_fwd), rows_taps=_bf16_pair(rows_taps),
                rows_inv=_bf16_pair(rows_inv), slab_fwd=_bf16_pair(slab_fwd), slab_inv=_bf16_pair(slab_inv),
                tw_cos=lanes(np.cos(th)), tw_sin=lanes(np.sin(th)))


def _const_dot3(m_hi, m_lo, x):
    x_hi, x_lo = _split_bf16(x)
    d = lambda a, b: jnp.dot(a, b, preferred_element_type=F32)
    return d(m_hi, x_hi) + (d(m_hi, x_lo) + d(m_lo, x_hi))


def _hy_prep_body(x_ref, xp_ref, xn_ref, w_ref, v_ref, x1_ref, x2_ref):
    y = _conv3(x_ref, xp_ref, xn_ref, w_ref)
    v_ref[...] = y[:, :HY_CH]
    x1_ref[...] = y[:, HY_CH:2 * HY_CH]
    x2_ref[...] = y[:, 2 * HY_CH:]


def hy_prep(p, col0, conv_w, tl):
    b, l, _ = p.shape
    c3 = 3 * HY_CH
    nblk8 = l // 8
    cb = col0 // c3
    tok = lambda bi, i: (bi, i, 0)
    return pl.pallas_call(
        _hy_prep_body,
        grid=(b, l // tl),
        in_specs=[pl.BlockSpec((None, tl, c3), lambda bi, i: (bi, i, cb)),
                  pl.BlockSpec((None, 8, c3), lambda bi, i: (bi, jnp.maximum(i * (tl // 8) - 1, 0), cb)),
                  pl.BlockSpec((None, 8, c3), lambda bi, i: (bi, jnp.minimum((i + 1) * (tl // 8), nblk8 - 1), cb)),
                  pl.BlockSpec((3, c3), lambda bi, i: (0, 0))],
        out_specs=[pl.BlockSpec((None, tl, HY_CH), tok)] * 3,
        out_shape=[jax.ShapeDtypeStruct((b, l, HY_CH), F32)] * 3,
        compiler_params=_cparams(("parallel", "parallel")),
        name="hy_prep",
    )(p, p, p, conv_w)


def _hy_filter_body(z_ref, w1_ref, b1_ref, w2_ref, b2_ref, w3_ref, rates_ref, taps_ref, sum_ref, *, zero_row):
    i = pl.program_id(0)
    hp = lax.Precision.HIGHEST
    z = z_ref[...]
    h = jnp.sin(HY_SIN_FREQ * (jnp.dot(z, w1_ref[...], preferred_element_type=F32, precision=hp) + b1_ref[...]))
    h = jnp.sin(HY_SIN_FREQ * (jnp.dot(h, w2_ref[...], preferred_element_type=F32, precision=hp) + b2_ref[...]))
    h = jnp.dot(h, w3_ref[...], preferred_element_type=F32, precision=hp)
    h = h * jnp.exp(-z[:, 0:1] * rates_ref[...])

    @pl.when(i == 0)
    def _():
        sum_ref[...] = jnp.zeros_like(sum_ref)

    sum_ref[...] += jnp.sum(jnp.abs(h), axis=0, keepdims=True)
    tr = z.shape[0]
    row = i * tr + lax.broadcasted_iota(jnp.int32, (tr, 1), 0)
    h = jnp.where(row == zero_row, 0.0, h)
    for o in range(HY_ORDER):
        taps_ref[o] = h[:, o * HY_CH:(o + 1) * HY_CH]


def hy_filter_taps(l, w1, b1, w2, b2, w3, tr=512):
    n = 2 * l
    r = np.arange(n)
    pos = np.where(r < l, r, n - r).astype(np.float64)
    pos[l] = 0.0
    pos = jnp.asarray(pos, F32)
    t = pos / max(l - 1, 1)
    bands = jnp.linspace(1e-4, HY_BANDS - 1, HY_BANDS, dtype=F32)
    phase = (2.0 * math.pi / l) * pos[:, None] * bands[None, :]
    kf = 32
    z = jnp.concatenate([t[:, None], jnp.cos(phase), -jnp.sin(phase)], axis=-1)
    z = jnp.pad(z, ((0, 0), (0, kf - z.shape[1])))
    w1p = jnp.pad(w1.astype(F32), ((0, kf - w1.shape[0]), (0, 0)))
    hid = w1.shape[1]
    w3d = w3.astype(F32).reshape(hid, HY_ORDER, 2, HY_CH).transpose(2, 0, 1, 3).reshape(2, hid, HY_ORDER * HY_CH)
    rates = jnp.abs(jnp.linspace(math.log(HY_TARGET) / HY_DECAY_LONG, math.log(HY_TARGET) / HY_DECAY_SHORT,
                                 HY_CH, dtype=F32))
    rates = jnp.tile(rates, HY_ORDER).reshape(1, HY_ORDER * HY_CH)
    fixed = lambda i: (0, 0)
    return pl.pallas_call(
        functools.partial(_hy_filter_body, zero_row=l),
        grid=(n // tr,),
        in_specs=[pl.BlockSpec((tr, kf), lambda i: (i, 0)),
                  pl.BlockSpec((kf, hid), fixed), pl.BlockSpec((1, hid), fixed),
                  pl.BlockSpec((hid, hid), fixed), pl.BlockSpec((1, hid), fixed),
                  pl.BlockSpec((None, hid, HY_ORDER * HY_CH), lambda i: (i // (l // tr), 0, 0)),
                  pl.BlockSpec((1, HY_ORDER * HY_CH), fixed)],
        out_specs=[pl.BlockSpec((HY_ORDER, tr, HY_CH), lambda i: (0, i, 0)),
                   pl.BlockSpec((1, HY_ORDER * HY_CH), fixed)],
        out_shape=[jax.ShapeDtypeStruct((HY_ORDER, n, HY_CH), F32),
                   jax.ShapeDtypeStruct((1, HY_ORDER * HY_CH), F32)],
        compiler_params=_cparams(("arbitrary",)),
        name="hy_filter_taps",
    )(z, w1p, b1.astype(F32).reshape(1, hid), w2.astype(F32), b2.astype(F32).reshape(1, hid), w3d, rates)


def _hy_rows_body(*refs, mode):
    if mode == "taps":
        mh_ref, ml_ref, x_ref, sum_ref, o_ref = refs
        x = x_ref[...] * (1.0 / (sum_ref[...] + EPS))
    elif mode == "fwd":
        mh_ref, ml_ref, x_ref, o_ref = refs
        x = jnp.concatenate([x_ref[0], x_ref[1]], axis=0)
    else:
        mh_ref, ml_ref, x_ref, xn_ref, z_ref, skip_ref, o_ref = refs
        x = jnp.concatenate([x_ref[0], x_ref[1]], axis=0)
    y = _const_dot3(mh_ref[...], ml_ref[...], x)
    half = y.shape[0] // 2
    if mode == "inv":
        for bi in range(2):
            conv = y[bi * half:(bi + 1) * half]
            o_ref[bi] = (xn_ref[bi] * (conv + z_ref[bi] * skip_ref[...])).astype(o_ref.dtype)
    else:
        o_ref[0] = y[:half]
        o_ref[1] = y[half:]


def hy_rows_pass(mode, mat, x, extra=(), out_dtype=F32, tile=2048):
    mh, ml = mat
    cols = x.shape[-1]
    col3 = lambda j: (0, 0, j)
    mspec = pl.BlockSpec(mh.shape, lambda j: (0, 0))
    if mode == "taps":
        n1 = x.shape[0]
        in_specs = [mspec, mspec, pl.BlockSpec((n1, tile), lambda j: (0, j)), pl.BlockSpec((1, tile), lambda j: (0, 0))]
        out_rows = n1
    elif mode == "fwd":
        n1 = 2 * x.shape[1]
        in_specs = [mspec, mspec, pl.BlockSpec((2, n1 // 2, tile), col3)]
        out_rows = n1
    else:
        n1 = x.shape[1]
        in_specs = [mspec, mspec, pl.BlockSpec((2, n1, tile), col3), pl.BlockSpec((2, n1 // 2, tile), col3),
                    pl.BlockSpec((2, n1 // 2, tile), col3), pl.BlockSpec((1, tile), lambda j: (0, 0))]
        out_rows = n1 // 2
    return pl.pallas_call(
        functools.partial(_hy_rows_body, mode=mode),
        grid=(cols // tile,),
        in_specs=in_specs,
        out_specs=pl.BlockSpec((2, out_rows, tile), col3),
        out_shape=jax.ShapeDtypeStruct((2, out_rows, cols), out_dtype),
        compiler_params=_cparams(("parallel",)),
        name="hy_rows_" + mode,
    )(mh, ml, x, *extra)


def _hy_slab_body(*refs, with_filter):
    if with_filter:
        fh_ref, fl_ref, ih_ref, il_ref, tc_ref, ts_ref, a_ref, h_ref, o_ref = refs
    else:
        fh_ref, fl_ref, tc_ref, ts_ref, a_ref, o_ref = refs
    reps = a_ref.shape[-1] // tc_ref.shape[-1]
    tc = jnp.concatenate([tc_ref[...]] * reps, axis=1)
    ts = jnp.concatenate([ts_ref[...]] * reps, axis=1)
    ar, ai = a_ref[0], a_ref[1]
    n2 = ar.shape[0]
    x = _const_dot3(fh_ref[...], fl_ref[...],
                    jnp.concatenate([ar * tc + ai * ts, ai * tc - ar * ts], axis=0))
    xr, xi = x[:n2], x[n2:]
    if not with_filter:
        o_ref[0] = xr
        o_ref[1] = xi
        return
    hr, hi = h_ref[0], h_ref[1]
    y = _const_dot3(ih_ref[...], il_ref[...],
                    jnp.concatenate([xr * hr - xi * hi, xr * hi + xi * hr], axis=0))
    yr, yi = y[:n2], y[n2:]
    o_ref[0] = yr * tc - yi * ts
    o_ref[1] = yi * tc + yr * ts


def hy_slab_pass(consts, a, h=None):
    _, n1, n2, c = a.shape
    fh, fl = consts["slab_fwd"]
    mspec = pl.BlockSpec(fh.shape, lambda k: (0, 0))
    tw = pl.BlockSpec((None, n2, 128), lambda k: (k, 0, 0))
    slab = pl.BlockSpec((2, None, n2, c), lambda k: (0, k, 0, 0))
    if h is None:
        args, in_specs = [fh, fl, consts["tw_cos"], consts["tw_sin"], a], [mspec, mspec, tw, tw, slab]
    else:
        ih, il = consts["slab_inv"]
        args = [fh, fl, ih, il, consts["tw_cos"], consts["tw_sin"], a, h]
        in_specs = [mspec] * 4 + [tw, tw, slab, slab]
    return pl.pallas_call(
        functools.partial(_hy_slab_body, with_filter=h is not None),
        grid=(n1,),
        in_specs=in_specs,
        out_specs=slab,
        out_shape=jax.ShapeDtypeStruct(a.shape, F32),
        compiler_params=_cparams(("parallel",)),
        name="hy_slab_conv" if h is not None else "hy_slab_fwd",
    )(*args)


def hyena_pallas(p, conv_w, w1, b1, w2, b2, w3, skip, col0=RET_COLS):
    b, l, _ = p.shape
    assert b == 2, "the two batch rows are packed as one complex signal"
    c = HY_CH
    consts = _hy_constants(l)
    n1, n2 = consts["n1"], consts["n2"]
    cols = n2 * c
    taps, sums = hy_filter_taps(l, w1, b1, w2, b2, w3)
    spectra = []
    for o in range(HY_ORDER):
        a_h = hy_rows_pass("taps", consts["rows_taps"], taps[o].reshape(n1, cols),
                           extra=(jnp.tile(sums[:, o * c:(o + 1) * c], (1, 2048 // c)),))
        spectra.append(hy_slab_pass(consts, a_h.reshape(2, n1, n2, c)))
    v, x1, x2 = hy_prep(p, col0, conv_w, 1024)
    z = v
    for o, xn in enumerate((x1, x2)):
        a = hy_rows_pass("fwd", consts["rows_fwd"], z.reshape(2, n1 // 2, cols))
        bm = hy_slab_pass(consts, a.reshape(2, n1, n2, c), spectra[o])
        last = o == HY_ORDER - 1
        z = hy_rows_pass("inv", consts["rows_inv"], bm.reshape(2, n1, cols),
                         extra=(xn.reshape(2, n1 // 2, cols), z.reshape(2, n1 // 2, cols),
                                jnp.tile(skip[o].astype(F32).reshape(1, c), (1, 2048 // c))),
                         out_dtype=BF16 if last else F32).reshape(b, l, c)
    return z


def rmsnorm(x, w):
    xf = x.astype(F32)
    y = xf * lax.rsqrt(jnp.mean(xf * xf, axis=-1, keepdims=True) + EPS)
    return (y * w.astype(F32)).astype(x.dtype)


def l2norm(x):
    return x * lax.rsqrt(jnp.sum(x * x, axis=-1, keepdims=True) + EPS)


def rotate_half(x, ang):
    cos = jnp.cos(ang)[:, None, :]
    sin = jnp.sin(ang)[:, None, :]
    x1, x2 = jnp.split(x, 2, axis=-1)
    return jnp.concatenate([x1 * cos - x2 * sin, x2 * cos + x1 * sin], axis=-1).astype(x.dtype)


def axial_angles(rows):
    m = SWA_DIM // 4
    inv = ROPE_THETA ** (-jnp.arange(m, dtype=F32) / m)
    row = jnp.repeat(jnp.arange(rows, dtype=F32), GRID_W)
    col = jnp.broadcast_to(jnp.arange(GRID_W, dtype=F32), (rows, GRID_W)).reshape(-1)
    return row[:, None] * inv, col[:, None] * inv


def axial_rope(x, ang_row, ang_col):
    h = x.shape[-1] // 2
    return jnp.concatenate([rotate_half(x[..., :h], ang_row), rotate_half(x[..., h:], ang_col)], axis=-1)


def short_conv(x, w):
    K, C = w.shape
    return lax.conv_general_dilated(x, w[:, None, :].astype(x.dtype), window_strides=(1,),
                                    padding=[(K // 2, K // 2)],
                                    dimension_numbers=('NWC', 'WIO', 'NWC'),
                                    feature_group_count=C)


def flip_time(t):
    return jnp.flip(t, axis=2)


def keep_time(t):
    return t


def delta_chunk_scan(q, k, v, g, beta, s0):
    B, H, L, dk = q.shape
    dv = v.shape[-1]
    C = DN_CHUNK
    N = L // C
    q = q.reshape(B, H, N, C, dk)
    k = k.reshape(B, H, N, C, dk)
    v = v.reshape(B, H, N, C, dv)
    g = g.reshape(B, H, N, C)
    beta = beta.reshape(B, H, N, C)
    gam = jnp.cumsum(g, axis=-1)
    idx = jnp.arange(C)
    incl = idx[:, None] >= idx[None, :]
    strict = idx[:, None] > idx[None, :]
    e = jnp.exp(jnp.where(incl, gam[..., :, None] - gam[..., None, :], 0.0))
    dec_incl = jnp.where(incl, e, 0.0)
    dec_strict = jnp.where(strict, e, 0.0)
    kb = k * beta[..., None]
    m = jnp.einsum('bhnid,bhnjd->bhnij', kb, k) * dec_strict
    a = m + jnp.eye(C, dtype=F32)
    rhs = jnp.concatenate([v * beta[..., None], kb * jnp.exp(gam)[..., None]], axis=-1)
    sol = lax.linalg.triangular_solve(a, rhs, left_side=True, lower=True, unit_diagonal=True)
    u, w = sol[..., :dv], sol[..., dv:]
    attn = jnp.einsum('bhnid,bhnjd->bhnij', q, k) * dec_incl
    q_dec = q * jnp.exp(gam)[..., None]
    k_dec = k * jnp.exp(gam[..., -1:] - gam)[..., None]
    chunk_dec = jnp.exp(gam[..., -1])

    def step(S, xs):
        u_n, w_n, a_n, qd_n, kd_n, cd_n = xs
        v_new = u_n - jnp.einsum('bhck,bhkv->bhcv', w_n, S)
        o = jnp.einsum('bhck,bhkv->bhcv', qd_n, S) + jnp.einsum('bhij,bhjv->bhiv', a_n, v_new)
        S = S * cd_n[..., None, None] + jnp.einsum('bhck,bhcv->bhkv', kd_n, v_new)
        return S, o

    xs = tuple(jnp.moveaxis(t, 2, 0) for t in (u, w, attn, q_dec, k_dec, chunk_dec))
    S, o = lax.scan(step, s0, xs)
    return jnp.moveaxis(o, 0, 2).reshape(B, H, L, dv), S


def dn_features(p, conv_w, a_log, dt_bias):
    B, L, _ = p.shape
    qkv = jax.nn.silu(short_conv(p[..., :3 * DN_WIDTH], conv_w)).astype(F32)
    heads = lambda t: t.reshape(B, L, DN_HEADS, DN_DIM).transpose(0, 2, 1, 3)
    q, k, v = (heads(t) for t in jnp.split(qkv, 3, axis=-1))
    q = l2norm(q) * DN_DIM ** -0.5
    k = l2norm(k)
    gate = p[..., 3 * DN_WIDTH:4 * DN_WIDTH]
    ab = p[..., 4 * DN_WIDTH:].astype(F32).reshape(B, L, 2, 2, DN_HEADS)
    g = -jnp.exp(a_log.astype(F32)) * jax.nn.softplus(ab[:, :, 0] + dt_bias.astype(F32))
    beta = jax.nn.sigmoid(ab[:, :, 1])
    return q, k, v, g.transpose(2, 0, 3, 1), beta.transpose(2, 0, 3, 1), gate


def head_norm_gate(o, gate, w):
    B, H, L, dv = o.shape
    y = rmsnorm(o.transpose(0, 2, 1, 3), w)
    return (y * jax.nn.silu(gate.astype(F32).reshape(B, L, H, dv))).reshape(B, L, H * dv)


def gated_deltanet(p_lat, p_ctx, conv_w, a_log, dt_bias, norm_w, need_ctx):
    lat = dn_features(p_lat, conv_w, a_log, dt_bias)
    cx = dn_features(p_ctx, conv_w, a_log, dt_bias)
    B = p_lat.shape[0]
    o_lat, o_ctx = [], []
    for d in range(2):
        fl = flip_time if d == 1 else keep_time
        s0 = jnp.zeros((B, DN_HEADS, DN_DIM, DN_DIM), F32)
        oc, s_ctx = delta_chunk_scan(fl(cx[0]), fl(cx[1]), fl(cx[2]), fl(cx[3][d]), fl(cx[4][d]), s0)
        ol, _ = delta_chunk_scan(fl(lat[0]), fl(lat[1]), fl(lat[2]), fl(lat[3][d]), fl(lat[4][d]), s_ctx)
        o_lat.append(fl(ol))
        o_ctx.append(fl(oc))
    out_lat = head_norm_gate(o_lat[0] + o_lat[1], lat[5], norm_w)
    out_ctx = head_norm_gate(o_ctx[0] + o_ctx[1], cx[5], norm_w) if need_ctx else None
    return out_lat, out_ctx


def gqa_qkv(p, q_norm_w, k_norm_w):
    B, L, _ = p.shape
    nq, nk = SWA_HEADS * SWA_DIM, SWA_KV_HEADS * SWA_DIM
    q = rmsnorm(p[..., :nq].reshape(B, L, SWA_HEADS, SWA_DIM), q_norm_w)
    k = rmsnorm(p[..., nq:nq + nk].reshape(B, L, SWA_KV_HEADS, SWA_DIM), k_norm_w)
    v = p[..., nq + nk:].reshape(B, L, SWA_KV_HEADS, SWA_DIM)
    return q, k, v


def banded_attention(q, k, v, kc, vc, sink):
    B, L, Hq, d = q.shape
    Hkv = k.shape[2]
    G = Hq // Hkv
    W = SWA_BLOCK
    NB = L // W
    qb = q.reshape(B, NB, W, Hkv, G, d)

    def band(t):
        tp = jnp.pad(t.reshape(B, NB, W, Hkv, d), ((0, 0), (1, 1), (0, 0), (0, 0), (0, 0)))
        return jnp.concatenate([tp[:, :-2], tp[:, 1:-1], tp[:, 2:]], axis=2)

    kb, vb = band(k), band(v)
    scale = d ** -0.5
    s_loc = jnp.einsum('bnqhgd,bnkhd->bnhgqk', qb, kb).astype(F32) * scale
    s_ctx = jnp.einsum('bnqhgd,bchd->bnhgqc', qb, kc).astype(F32) * scale
    rel = (jnp.arange(3 * W) - W)[None, :] - jnp.arange(W)[:, None]
    kblk = jnp.arange(NB)[:, None] + jnp.arange(3 * W)[None, :] // W - 1
    valid = (jnp.abs(rel) <= SWA_WINDOW)[None] & ((kblk >= 0) & (kblk < NB))[:, None, :]
    s_loc = jnp.where(valid[None, :, None, None], s_loc, -jnp.inf)
    s_sink = jnp.broadcast_to(sink.astype(F32).reshape(Hkv, G, 1, 1), s_loc.shape[:-1] + (1,))
    prob = jax.nn.softmax(jnp.concatenate([s_loc, s_ctx, s_sink], axis=-1), axis=-1).astype(v.dtype)
    n_ctx = kc.shape[1]
    o = (jnp.einsum('bnhgqk,bnkhd->bnqhgd', prob[..., :3 * W], vb)
         + jnp.einsum('bnhgqc,bchd->bnqhgd', prob[..., 3 * W:3 * W + n_ctx], vc))
    return o.reshape(B, L, Hq * d)


def context_attention(qc, kc, vc, sink):
    B, Cn, Hq, d = qc.shape
    Hkv = kc.shape[2]
    G = Hq // Hkv
    q = qc.reshape(B, Cn, Hkv, G, d)
    s = jnp.einsum('bqhgd,bkhd->bhgqk', q, kc).astype(F32) * d ** -0.5
    s_sink = jnp.broadcast_to(sink.astype(F32).reshape(Hkv, G, 1, 1), s.shape[:-1] + (1,))
    prob = jax.nn.softmax(jnp.concatenate([s, s_sink], axis=-1), axis=-1)[..., :-1].astype(vc.dtype)
    return jnp.einsum('bhgqk,bkhd->bqhgd', prob, vc).reshape(B, Cn, Hq * d)


def window_gqa(p_lat, p_ctx, ang_row, ang_col, q_norm_w, k_norm_w, sink, need_ctx):
    ql, kl, vl = gqa_qkv(p_lat, q_norm_w, k_norm_w)
    ql = axial_rope(ql, ang_row, ang_col)
    kl = axial_rope(kl, ang_row, ang_col)
    qc, kc, vc = gqa_qkv(p_ctx, q_norm_w, k_norm_w)
    o_lat = banded_attention(ql, kl, vl, kc, vc, sink)
    o_ctx = context_attention(qc, kc, vc, sink) if need_ctx else None
    return o_lat, o_ctx


def ab_mixer(p_lat, p_ctx, ang_row, ang_col, dn_conv_w, dn_a_log, dn_dt_bias, dn_norm_w,
             q_norm_w, k_norm_w, sink, need_ctx):
    a_lat, a_ctx = gated_deltanet(p_lat[..., :DN_COLS], p_ctx[..., :DN_COLS], dn_conv_w, dn_a_log,
                                  dn_dt_bias, dn_norm_w, need_ctx)
    b_lat, b_ctx = window_gqa(p_lat[..., DN_COLS:], p_ctx[..., DN_COLS:], ang_row, ang_col,
                              q_norm_w, k_norm_w, sink, need_ctx)
    o_lat = jnp.concatenate([a_lat, b_lat.astype(F32)], axis=-1)
    o_ctx = jnp.concatenate([a_ctx, b_ctx.astype(F32)], axis=-1) if need_ctx else None
    return o_lat, o_ctx


def retention_chunk_scan(q, k, v, log_gamma, s0):
    B, H, L, dk = q.shape
    dv = v.shape[-1]
    C = RET_CHUNK
    N = L // C
    q = q.reshape(B, H, N, C, dk)
    k = k.reshape(B, H, N, C, dk)
    v = v.reshape(B, H, N, C, dv)
    pos = jnp.arange(C, dtype=F32)
    rel = pos[:, None] - pos[None, :]
    incl = rel >= 0
    lg = log_gamma[:, None, None]
    dmat = jnp.where(incl, jnp.exp(lg * jnp.where(incl, rel, 0.0)), 0.0)
    scores = jnp.einsum('bhnid,bhnjd->bhnij', q, k) * dmat[:, None]
    o_in = jnp.einsum('bhnij,bhnje->bhnie', scores, v)
    q_dec = q * jnp.exp(log_gamma[:, None] * (pos + 1.0))[:, None, :, None]
    k_dec = k * jnp.exp(log_gamma[:, None] * (C - 1.0 - pos))[:, None, :, None]
    chunk_dec = jnp.exp(log_gamma * C)[None, :, None, None]

    def step(S, xs):
        qd, kd, vn = xs
        o = jnp.einsum('bhcd,bhde->bhce', qd, S)
        S = S * chunk_dec + jnp.einsum('bhcd,bhce->bhde', kd, vn)
        return S, o

    xs = tuple(jnp.moveaxis(t, 2, 0) for t in (q_dec, k_dec, v))
    S, o_x = lax.scan(step, s0, xs)
    o = o_in + jnp.moveaxis(o_x, 0, 2)
    return o.reshape(B, H, L, dv), S


def ret_features(p):
    B, L, _ = p.shape
    q, k, v, g = jnp.split(p, 4, axis=-1)
    inv = ROPE_THETA ** (-jnp.linspace(0.0, 1.0, RET_DIM // 2, dtype=F32))
    ang = jnp.arange(L, dtype=F32)[:, None] * inv
    heads = lambda t: t.reshape(B, L, RET_HEADS, RET_DIM).astype(F32)
    q = rotate_half(heads(q), ang)
    k = rotate_half(heads(k), ang) * RET_DIM ** -0.5
    tr = lambda t: t.transpose(0, 2, 1, 3)
    return tr(q), tr(k), tr(heads(v)), g


def head_groupnorm_gate(o, gate, w):
    B, H, L, dv = o.shape
    o = o.transpose(0, 2, 1, 3)
    mu = jnp.mean(o, axis=-1, keepdims=True)
    var = jnp.mean(jnp.square(o - mu), axis=-1, keepdims=True)
    y = (o - mu) * lax.rsqrt(var + EPS) * w.astype(F32).reshape(H, dv)
    return (y * jax.nn.silu(gate.astype(F32).reshape(B, L, H, dv))).reshape(B, L, H * dv)


def retention(p_lat, p_ctx, decay_logit, gn_w, need_ctx):
    lat, cx = ret_features(p_lat), ret_features(p_ctx)
    log_gamma = jax.nn.log_sigmoid(decay_logit.astype(F32))
    B = p_lat.shape[0]
    o_lat, o_ctx = [], []
    for d in range(2):
        fl = flip_time if d == 1 else keep_time
        s0 = jnp.zeros((B, RET_HEADS, RET_DIM, RET_DIM), F32)
        oc, s_ctx = retention_chunk_scan(fl(cx[0]), fl(cx[1]), fl(cx[2]), log_gamma[d], s0)
        ol, _ = retention_chunk_scan(fl(lat[0]), fl(lat[1]), fl(lat[2]), log_gamma[d], s_ctx)
        o_lat.append(fl(ol))
        o_ctx.append(fl(oc))
    out_lat = head_groupnorm_gate(o_lat[0] + o_lat[1], lat[3], gn_w)
    out_ctx = head_groupnorm_gate(o_ctx[0] + o_ctx[1], cx[3], gn_w) if need_ctx else None
    return out_lat, out_ctx


def hyena_filters(L, w1, b1, w2, b2, w3):
    pos = jnp.arange(L, dtype=F32)
    t = pos / max(L - 1, 1)
    bands = jnp.linspace(1e-4, HY_BANDS - 1, HY_BANDS, dtype=F32)
    phase = (2.0 * math.pi / L) * pos[:, None] * bands[None, :]
    z = jnp.concatenate([t[:, None], jnp.cos(phase), -jnp.sin(phase)], axis=-1)
    h = jnp.sin(HY_SIN_FREQ * (z @ w1.astype(F32) + b1.astype(F32)))
    h = jnp.sin(HY_SIN_FREQ * (h @ w2.astype(F32) + b2.astype(F32)))
    h = (h @ w3.astype(F32)).reshape(L, HY_ORDER, 2, HY_CH)
    rates = jnp.abs(jnp.linspace(math.log(HY_TARGET) / HY_DECAY_LONG, math.log(HY_TARGET) / HY_DECAY_SHORT,
                                 HY_CH, dtype=F32))
    h = h * jnp.exp(-t[:, None] * rates[None, :])[:, None, None, :]
    h = h / (jnp.sum(jnp.abs(h), axis=(0, 2), keepdims=True) + EPS)
    return h.transpose(1, 2, 0, 3)


def bidir_long_conv(u, h_fwd, h_bwd, skip):
    B, L, C = u.shape
    taps = jnp.concatenate([h_fwd, jnp.zeros((1, C), F32), h_bwd[:0:-1]], axis=0)
    y = jnp.fft.irfft(jnp.fft.rfft(u, n=2 * L, axis=1) * jnp.fft.rfft(taps, axis=0)[None],
                      n=2 * L, axis=1)[:, :L]
    return y + u * skip


def hyena(p, conv_w, w1, b1, w2, b2, w3, skip):
    L = p.shape[1]
    filt = hyena_filters(L, w1, b1, w2, b2, w3)
    u = short_conv(p, conv_w).astype(F32)
    parts = jnp.split(u, HY_ORDER + 1, axis=-1)
    z = parts[0]
    for n in range(HY_ORDER):
        z = parts[n + 1] * bidir_long_conv(z, filt[n, 0], filt[n, 1], skip[n].astype(F32))
    return z


def cd_mixer(p_lat, p_ctx, ret_decay_logit, ret_gn_w, hy_conv_w, hy_f_w1, hy_f_b1, hy_f_w2, hy_f_b2,
             hy_f_w3, hy_bias, need_ctx):
    c_lat, c_ctx = retention(p_lat[..., :RET_COLS], p_ctx[..., :RET_COLS], ret_decay_logit, ret_gn_w, need_ctx)
    hy = lambda p: hyena(p, hy_conv_w, hy_f_w1, hy_f_b1, hy_f_w2, hy_f_b2, hy_f_w3, hy_bias)
    o_lat = jnp.concatenate([c_lat, hy(p_lat[..., RET_COLS:])], axis=-1)
    o_ctx = jnp.concatenate([c_ctx, hy(p_ctx[..., RET_COLS:])], axis=-1) if need_ctx else None
    return o_lat, o_ctx


def _pad_cols(w, mult):
    n = w.shape[-1]
    pad = (-n) % mult
    return jnp.pad(w, ((0, 0), (0, pad))) if pad else w


def kernel(x, c, ctx, c_ctx, mod_w, mod_b, norm_mix_w, norm_ffn_w, ffn_w_in, ffn_w_out,
           ab_w_in, ab_w_out, dn_conv_w, dn_a_log, dn_dt_bias, dn_norm_w, swa_q_norm_w, swa_k_norm_w,
           swa_sink, cd_w_in, cd_w_out, ret_decay_logit, ret_gn_w, hy_conv_w, hy_f_w1, hy_f_b1,
           hy_f_w2, hy_f_b2, hy_f_w3, hy_bias):
    B, L, D = x.shape
    n_ctx = ctx.shape[1]
    rows = L // GRID_W
    ang_row, ang_col = axial_angles(rows)
    c_rows = jnp.zeros((8, D), F32).at[:B].set(c).at[B].set(c_ctx)
    mod_all = modulation(c_rows, mod_w, mod_b)
    hid = ffn_w_out.shape[1]
    h_ctx = ctx
    for layer in range(DEPTH):
        need_ctx = layer != DEPTH - 1
        i = layer // 2
        m = mod_all[layer].reshape(8, 6, D)
        mod = [m[:B, j][:, None, :] for j in range(6)]
        mod_c = [jnp.broadcast_to(m[B, j][None, None, :], (B, 1, D)) for j in range(6)]
        if layer % 2 == 0:
            w = ab_w_in[i]
            w_in = jnp.concatenate([w[:, :4 * DN_WIDTH], w[:, DN_COLS:], w[:, 4 * DN_WIDTH:DN_COLS]], axis=1)
            w_out, cols = ab_w_out[i], ab_w_in.shape[-1]
        else:
            w = cd_w_in[i]
            w_in = jnp.concatenate([w[:, RET_COLS:], w[:, :RET_COLS]], axis=1)
            w_out, cols = cd_w_out[i], cd_w_in.shape[-1]
        w_in_p = _pad_cols(w_in, 128).astype(BF16)
        ncols = w_in_p.shape[1]
        tn = 640 if ncols % 640 == 0 else 512
        p_lat = in_projection(x, norm_mix_w[layer], mod[0], mod[1], w_in_p, 1024, tn)
        p_ctx = in_projection(h_ctx, norm_mix_w[layer], mod_c[0], mod_c[1], w_in_p, n_ctx, tn)
        if layer % 2 == 0:
            swa0 = 4 * DN_WIDTH
            ab0 = swa0 + (SWA_HEADS + 2 * SWA_KV_HEADS) * SWA_DIM
            a_lat, a_ctx = deltanet_pallas(p_lat, p_ctx, dn_conv_w[i], dn_a_log[i], dn_dt_bias[i],
                                           dn_norm_w[i], need_ctx, ab_col=ab0, gate_col=3 * DN_WIDTH)
            b_lat, b_ctx = window_gqa_pallas(p_lat, p_ctx, swa0, axial_rope_tables(rows), swa_q_norm_w[i],
                                             swa_k_norm_w[i], swa_sink[i], need_ctx)
        else:
            a_lat, a_ctx = retention_pallas(p_lat, p_ctx, ret_decay_logit[i], ret_gn_w[i], need_ctx,
                                            col0=(HY_ORDER + 1) * HY_CH)
            assert not need_ctx, "the last layer's context outputs reach no latent token"
            b_lat = hyena_pallas(p_lat, hy_conv_w[i], hy_f_w1[i], hy_f_b1[i], hy_f_w2[i], hy_f_b2[i],
                                 hy_f_w3[i], hy_bias[i], col0=0)
            b_ctx = None
        wo = w_out.astype(BF16)
        wg = ffn_w_in[layer][:, :hid].astype(BF16)
        wu = ffn_w_in[layer][:, hid:].astype(BF16)
        wd = ffn_w_out[layer].astype(BF16)
        x = out_projection_ffn(x, a_lat, b_lat, wo, mod[2], norm_ffn_w[layer], mod[3], mod[4], mod[5],
                               wg, wu, wd, 1024, 256)
        if need_ctx:
            h_ctx = out_projection_ffn(h_ctx, a_ctx, b_ctx, wo, mod_c[2], norm_ffn_w[layer], mod_c[3], mod_c[4],
                                       mod_c[5], wg, wu, wd, n_ctx, 256)
    return x
```

```python
import functools
import math

import jax
import jax.numpy as jnp
import ml_dtypes
import numpy as np
from jax import lax
from jax.experimental import pallas as pl
from jax.experimental.pallas import tpu as pltpu

F32 = jnp.float32
BF16 = jnp.bfloat16
EPS = 1e-6

D_MODEL = 1024
DEPTH = 2
GRID_W = 64

DN_HEADS = 4
DN_DIM = 128
DN_WIDTH = DN_HEADS * DN_DIM
DN_CHUNK = 64
DN_COLS = 4 * DN_WIDTH + 4 * DN_HEADS
SWA_HEADS = 4
SWA_KV_HEADS = 2
SWA_DIM = 128
SWA_WINDOW = 128
SWA_BLOCK = 128
ROPE_THETA = 10000.0
RET_HEADS = 4
RET_DIM = 128
RET_WIDTH = RET_HEADS * RET_DIM
RET_CHUNK = 128
RET_COLS = 4 * RET_WIDTH
HY_CH = 512
HY_ORDER = 2
HY_BANDS = 8
HY_SIN_FREQ = 1.0
HY_TARGET = 1e-2
HY_DECAY_SHORT = 0.3
HY_DECAY_LONG = 1.5

VMEM_LIMIT_BYTES = 56 * 1024 * 1024


def _cparams(sem):
    return pltpu.CompilerParams(dimension_semantics=sem, vmem_limit_bytes=VMEM_LIMIT_BYTES)


def _mod_body(c_ref, w_ref, b_ref, o_ref):
    a = c_ref[...]
    a = a * jax.nn.sigmoid(a)
    o_ref[...] = jnp.dot(a, w_ref[...], preferred_element_type=F32,
                         precision=lax.Precision.HIGHEST) + b_ref[...]


def modulation(c_rows, mod_w, mod_b):
    depth, d, n = mod_w.shape
    tn = 1536
    return pl.pallas_call(
        _mod_body,
        grid=(depth, n // tn),
        in_specs=[pl.BlockSpec((8, d), lambda l, j: (0, 0)),
                  pl.BlockSpec((None, d, tn), lambda l, j: (l, 0, j)),
                  pl.BlockSpec((None, 1, tn), lambda l, j: (l, 0, j))],
        out_specs=pl.BlockSpec((None, 8, tn), lambda l, j: (l, 0, j)),
        out_shape=jax.ShapeDtypeStruct((depth, 8, n), F32),
        compiler_params=_cparams(("parallel", "parallel")),
        name="modulation",
    )(c_rows, mod_w, mod_b.reshape(depth, 1, n))


def _norm_mod(x, nw, shift, scale):
    y = x * lax.rsqrt(jnp.mean(x * x, axis=-1, keepdims=True) + EPS)
    return (y * nw) * (1.0 + scale) + shift


def _resident(shape):
    return pl.BlockSpec(shape, lambda *_: (0,) * len(shape), pipeline_mode=pl.Buffered(1))


def _inproj_body(x_ref, nw_ref, shift_ref, scale_ref, w_ref, o_ref):
    h = _norm_mod(x_ref[...], nw_ref[...], shift_ref[...], scale_ref[...]).astype(BF16)
    o_ref[...] = jnp.dot(h, w_ref[...], preferred_element_type=F32)


def in_projection(x, nw, shift, scale, w, tm):
    b, l, d = x.shape
    n = w.shape[1]
    return pl.pallas_call(
        _inproj_body,
        grid=(b, l // tm),
        in_specs=[pl.BlockSpec((None, tm, d), lambda bi, i: (bi, i, 0)),
                  _resident((1, d)),
                  pl.BlockSpec((None, 1, d), lambda bi, i: (bi, 0, 0)),
                  pl.BlockSpec((None, 1, d), lambda bi, i: (bi, 0, 0)),
                  _resident((d, n))],
        out_specs=pl.BlockSpec((None, tm, n), lambda bi, i: (bi, i, 0)),
        out_shape=jax.ShapeDtypeStruct((b, l, n), F32),
        compiler_params=_cparams(("parallel", "parallel")),
        name="in_projection",
    )(x, nw.reshape(1, d), shift, scale, w)


def _outffn_body(x_ref, oa_ref, ob_ref, wo_ref, g2_ref, nw_ref, sh_ref, sc_ref, g5_ref,
                 wg_ref, wu_ref, wd_ref, y_ref, x1_scr, h_scr, acc_scr, *, th):
    wa = oa_ref.shape[1]
    mix = (jnp.dot(oa_ref[...].astype(BF16), wo_ref[:wa, :], preferred_element_type=F32)
           + jnp.dot(ob_ref[...].astype(BF16), wo_ref[wa:, :], preferred_element_type=F32))
    x1 = x_ref[...] + g2_ref[...] * mix
    x1_scr[...] = x1
    h_scr[...] = _norm_mod(x1, nw_ref[...], sh_ref[...], sc_ref[...]).astype(BF16)
    for k in range(wg_ref.shape[1] // th):
        ks = slice(k * th, (k + 1) * th)
        g = jnp.dot(h_scr[...], wg_ref[:, ks], preferred_element_type=F32)
        u = jnp.dot(h_scr[...], wu_ref[:, ks], preferred_element_type=F32)
        a = (g * jax.nn.sigmoid(g) * u).astype(BF16)
        part = jnp.dot(a, wd_ref[ks, :], preferred_element_type=F32)
        if k == 0:
            acc_scr[...] = part
        else:
            acc_scr[...] += part
    y_ref[...] = x1_scr[...] + g5_ref[...] * acc_scr[...]


def out_projection_ffn(x, oa, ob, wo, g2, nw, shift, scale, g5, wg, wu, wd, tm, th):
    b, l, d = x.shape
    wa, wb = oa.shape[-1], ob.shape[-1]
    hid = wg.shape[1]
    tok = lambda bi, i: (bi, i, 0)
    vec = pl.BlockSpec((None, 1, d), lambda bi, i: (bi, 0, 0))
    return pl.pallas_call(
        functools.partial(_outffn_body, th=th),
        grid=(b, l // tm),
        in_specs=[pl.BlockSpec((None, tm, d), tok),
                  pl.BlockSpec((None, tm, wa), tok),
                  pl.BlockSpec((None, tm, wb), tok),
                  _resident((d, d)), vec, _resident((1, d)), vec, vec, vec,
                  _resident((d, hid)), _resident((d, hid)), _resident((hid, d))],
        out_specs=pl.BlockSpec((None, tm, d), tok),
        out_shape=jax.ShapeDtypeStruct((b, l, d), F32),
        scratch_shapes=[pltpu.VMEM((tm, d), F32), pltpu.VMEM((tm, d), BF16), pltpu.VMEM((tm, d), F32)],
        compiler_params=_cparams(("parallel", "parallel")),
        name="out_projection_ffn",
    )(x, oa, ob, wo, g2, nw.reshape(1, d), shift, scale, g5, wg, wu, wd)


NEG_BIG = -1e30


def _head_rms(x, w):
    return x * lax.rsqrt(jnp.mean(x * x, axis=-1, keepdims=True) + EPS) * w


def _swa_prep_body(*refs, use_rope):
    if use_rope:
        q_ref, k_ref, v_ref, qw_ref, kw_ref, cos_ref, sa_ref, sb_ref, qo_ref, ko_ref, vo_ref = refs
    else:
        q_ref, k_ref, v_ref, qw_ref, kw_ref, qo_ref, ko_ref, vo_ref = refs

    def prep(x, w):
        y = _head_rms(x, w)
        if use_rope:
            y = (y * cos_ref[...] + pltpu.roll(y, SWA_DIM - 32, 1) * sa_ref[...]
                 + pltpu.roll(y, 32, 1) * sb_ref[...])
        return y.astype(BF16)

    for h in range(SWA_HEADS):
        sl = slice(h * SWA_DIM, (h + 1) * SWA_DIM)
        qo_ref[:, sl] = prep(q_ref[:, sl], qw_ref[...])
    for h in range(SWA_KV_HEADS):
        sl = slice(h * SWA_DIM, (h + 1) * SWA_DIM)
        ko_ref[:, sl] = prep(k_ref[:, sl], kw_ref[...])
    vo_ref[...] = v_ref[...].astype(BF16)


def swa_prep(p, col0, qw, kw, rope, tl):
    b, l, _ = p.shape
    nq, nk = SWA_HEADS * SWA_DIM, SWA_KV_HEADS * SWA_DIM
    tok = lambda bi, i: (bi, i, 0)
    in_specs = [pl.BlockSpec((None, tl, nq), lambda bi, i: (bi, i, col0 // nq)),
                pl.BlockSpec((None, tl, nk), lambda bi, i: (bi, i, (col0 + nq) // nk)),
                pl.BlockSpec((None, tl, nk), lambda bi, i: (bi, i, (col0 + nq + nk) // nk)),
                pl.BlockSpec((1, SWA_DIM), lambda bi, i: (0, 0)),
                pl.BlockSpec((1, SWA_DIM), lambda bi, i: (0, 0))]
    args = [p, p, p, qw.reshape(1, SWA_DIM), kw.reshape(1, SWA_DIM)]
    if rope is not None:
        in_specs += [pl.BlockSpec((tl, SWA_DIM), lambda bi, i: (i, 0))] * 3
        args += list(rope)
    return pl.pallas_call(
        functools.partial(_swa_prep_body, use_rope=rope is not None),
        grid=(b, l // tl),
        in_specs=in_specs,
        out_specs=[pl.BlockSpec((None, tl, nq), tok), pl.BlockSpec((None, tl, nk), tok),
                   pl.BlockSpec((None, tl, nk), tok)],
        out_shape=[jax.ShapeDtypeStruct((b, l, nq), BF16), jax.ShapeDtypeStruct((b, l, nk), BF16),
                   jax.ShapeDtypeStruct((b, l, nk), BF16)],
        compiler_params=_cparams(("parallel", "parallel")),
        name="swa_prep",
    )(*args)


def _nt_dot(a, b):
    return lax.dot_general(a, b, (((1,), (1,)), ((), ())), preferred_element_type=F32)


def _swa_attend(q, keys, vals, masks, sink_col):
    scale = SWA_DIM ** -0.5
    scores = []
    for kk, mask in zip(keys, masks):
        s = _nt_dot(q, kk) * scale
        if mask is not None:
            s = jnp.where(mask, s, NEG_BIG)
        scores.append(s)
    m = sink_col
    for s in scores:
        m = jnp.maximum(m, jnp.max(s, axis=-1, keepdims=True))
    den = jnp.exp(sink_col - m)
    acc = None
    for s, vv in zip(scores, vals):
        pr = jnp.exp(s - m)
        den = den + jnp.sum(pr, axis=-1, keepdims=True)
        o = jnp.dot(pr.astype(BF16), vv, preferred_element_type=F32)
        acc = o if acc is None else acc + o
    return acc / den


def _swa_attn_body(sink_ref, q_ref, kp_ref, km_ref, kn_ref, vp_ref, vm_ref, vn_ref, kc_ref, vc_ref,
                   o_ref, *, nb):
    i = pl.program_id(1)
    n_blocks = pl.num_programs(1) * nb
    w, d = SWA_BLOCK, SWA_DIM
    grp = SWA_HEADS // SWA_KV_HEADS
    rows = lax.broadcasted_iota(jnp.int32, (grp * w, w), 0)
    qi = rows % w
    kj = lax.broadcasted_iota(jnp.int32, (grp * w, w), 1)
    for r in range(nb):
        blk = i * nb + r
        mask_prev = (kj >= qi) & (blk > 0)
        mask_next = (kj <= qi) & (blk < n_blocks - 1)
        rs = slice(r * w, (r + 1) * w)
        for h in range(SWA_KV_HEADS):
            hs = slice(h * d, (h + 1) * d)
            q = jnp.concatenate([q_ref[rs, (h * grp + g) * d:(h * grp + g + 1) * d] for g in range(grp)], axis=0)
            sink_col = jnp.zeros((grp * w, 1), F32)
            for g in range(grp):
                sink_col = jnp.where((rows[:, :1] // w) == g, sink_ref[h * grp + g], sink_col)
            if r > 0:
                k_prev, v_prev = km_ref[(r - 1) * w:r * w, hs], vm_ref[(r - 1) * w:r * w, hs]
            else:
                k_prev, v_prev = kp_ref[:, hs], vp_ref[:, hs]
            if r < nb - 1:
                k_next, v_next = km_ref[(r + 1) * w:(r + 2) * w, hs], vm_ref[(r + 1) * w:(r + 2) * w, hs]
            else:
                k_next, v_next = kn_ref[:, hs], vn_ref[:, hs]
            o = _swa_attend(q, [k_prev, km_ref[rs, hs], k_next, kc_ref[:, hs]],
                            [v_prev, vm_ref[rs, hs], v_next, vc_ref[:, hs]],
                            [mask_prev, None, mask_next, None], sink_col)
            for g in range(grp):
                o_ref[rs, (h * grp + g) * d:(h * grp + g + 1) * d] = o[g * w:(g + 1) * w].astype(o_ref.dtype)


def swa_attention(q, k, v, kc, vc, sink, nb):
    b, l, nq = q.shape
    nk = k.shape[-1]
    n_ctx = kc.shape[1]
    w = SWA_BLOCK
    tq = nb * w
    last = l // w - 1
    main = lambda bi, i, s: (bi, i, 0)
    prev = lambda bi, i, s: (bi, jnp.maximum(i * nb - 1, 0), 0)
    nxt = lambda bi, i, s: (bi, jnp.minimum((i + 1) * nb, last), 0)
    cx = lambda bi, i, s: (bi, 0, 0)
    return pl.pallas_call(
        functools.partial(_swa_attn_body, nb=nb),
        grid_spec=pltpu.PrefetchScalarGridSpec(
            num_scalar_prefetch=1,
            grid=(b, l // tq),
            in_specs=[pl.BlockSpec((None, tq, nq), main),
                      pl.BlockSpec((None, w, nk), prev), pl.BlockSpec((None, tq, nk), main),
                      pl.BlockSpec((None, w, nk), nxt),
                      pl.BlockSpec((None, w, nk), prev), pl.BlockSpec((None, tq, nk), main),
                      pl.BlockSpec((None, w, nk), nxt),
                      pl.BlockSpec((None, n_ctx, nk), cx), pl.BlockSpec((None, n_ctx, nk), cx)],
            out_specs=pl.BlockSpec((None, tq, nq), main)),
        out_shape=jax.ShapeDtypeStruct((b, l, nq), BF16),
        compiler_params=_cparams(("parallel", "parallel")),
        name="swa_attention",
    )(sink, q, k, k, k, v, v, v, kc, vc)


def _ctx_attn_body(sink_ref, q_ref, kc_ref, vc_ref, o_ref):
    n_ctx, d = q_ref.shape[0], SWA_DIM
    grp = SWA_HEADS // SWA_KV_HEADS
    rows = lax.broadcasted_iota(jnp.int32, (grp * n_ctx, 1), 0)
    for h in range(SWA_KV_HEADS):
        hs = slice(h * d, (h + 1) * d)
        q = jnp.concatenate([q_ref[:, (h * grp + g) * d:(h * grp + g + 1) * d] for g in range(grp)], axis=0)
        sink_col = jnp.zeros((grp * n_ctx, 1), F32)
        for g in range(grp):
            sink_col = jnp.where((rows // n_ctx) == g, sink_ref[h * grp + g], sink_col)
        o = _swa_attend(q, [kc_ref[:, hs]], [vc_ref[:, hs]], [None], sink_col)
        for g in range(grp):
            o_ref[:, (h * grp + g) * d:(h * grp + g + 1) * d] = o[g * n_ctx:(g + 1) * n_ctx].astype(o_ref.dtype)


def ctx_attention(qc, kc, vc, sink):
    b, n_ctx, nq = qc.shape
    nk = kc.shape[-1]
    cx = lambda bi, s: (bi, 0, 0)
    return pl.pallas_call(
        _ctx_attn_body,
        grid_spec=pltpu.PrefetchScalarGridSpec(
            num_scalar_prefetch=1, grid=(b,),
            in_specs=[pl.BlockSpec((None, n_ctx, nq), cx), pl.BlockSpec((None, n_ctx, nk), cx),
                      pl.BlockSpec((None, n_ctx, nk), cx)],
            out_specs=pl.BlockSpec((None, n_ctx, nq), cx)),
        out_shape=jax.ShapeDtypeStruct((b, n_ctx, nq), BF16),
        compiler_params=_cparams(("parallel",)),
        name="ctx_attention",
    )(sink, qc, kc, vc)


def axial_rope_tables(rows):
    ang_row, ang_col = axial_angles(rows)
    zero = jnp.zeros_like(ang_row)
    cos = jnp.concatenate([jnp.cos(ang_row)] * 2 + [jnp.cos(ang_col)] * 2, axis=-1)
    sa = jnp.concatenate([-jnp.sin(ang_row), zero, -jnp.sin(ang_col), zero], axis=-1)
    sb = jnp.concatenate([zero, jnp.sin(ang_row), zero, jnp.sin(ang_col)], axis=-1)
    return cos, sa, sb


def window_gqa_pallas(p_lat, p_ctx, col0, rope, q_norm_w, k_norm_w, sink, need_ctx):
    ql, kl, vl = swa_prep(p_lat, col0, q_norm_w, k_norm_w, rope, 1024)
    qc, kc, vc = swa_prep(p_ctx, col0, q_norm_w, k_norm_w, None, p_ctx.shape[1])
    o_lat = swa_attention(ql, kl, vl, kc, vc, sink, 4)
    o_ctx = ctx_attention(qc, kc, vc, sink) if need_ctx else None
    return o_lat, o_ctx


def _tn_dot(a, b):
    return lax.dot_general(a, b, (((0,), (0,)), ((), ())), preferred_element_type=F32)


def _log_sigmoid(x):
    return jnp.minimum(x, 0.0) - jnp.log(1.0 + jnp.exp(-jnp.abs(x)))


RET_STREAMS = 2 * RET_HEADS


def _ret_scan_body(logit_ref, qf_ref, kf_ref, vf_ref, cosf_ref, sinf_ref, qb_ref, kb_ref, vb_ref, cosb_ref,
                   sinb_ref, s0_ref, of_ref, ob_ref, sf_ref, s_scr):
    n = pl.program_id(1)
    c, dh = RET_CHUNK, RET_DIM

    @pl.when(n == 0)
    def _():
        s_scr[...] = s0_ref[...]

    ii = lax.broadcasted_iota(jnp.int32, (c, c), 0)
    jj = lax.broadcasted_iota(jnp.int32, (c, c), 1)
    i1 = lax.broadcasted_iota(jnp.int32, (c, 1), 0)
    rel = {True: (ii - jj).astype(F32), False: (jj - ii).astype(F32)}
    pos = {True: i1.astype(F32), False: (c - 1 - i1).astype(F32)}
    streams = [(d == 0, h) for d in range(2) for h in range(RET_HEADS)]
    srcs = {True: (qf_ref, kf_ref, vf_ref, cosf_ref, sinf_ref), False: (qb_ref, kb_ref, vb_ref, cosb_ref, sinb_ref)}

    def rot(x, fwd):
        return x * srcs[fwd][3][...] + pltpu.roll(x, dh // 2, 1) * srcs[fwd][4][...]

    hs = lambda h: slice(h * dh, (h + 1) * dh)
    lg = [_log_sigmoid(jnp.full((1, 1), logit_ref[r], F32)) for r in range(RET_STREAMS)]
    q = [rot(srcs[f][0][:, hs(h)], f) for f, h in streams]
    k = [rot(srcs[f][1][:, hs(h)], f) * dh ** -0.5 for f, h in streams]
    v = [srcs[f][2][:, hs(h)].astype(BF16) for f, h in streams]
    dmat = [jnp.where(rel[f] >= 0, jnp.exp(lg[r] * jnp.maximum(rel[f], 0.0)), 0.0)
            for r, (f, _) in enumerate(streams)]
    scores = [_nt_dot(q[r].astype(BF16), k[r].astype(BF16)) * dmat[r] for r in range(RET_STREAMS)]
    q_dec = [(q[r] * jnp.exp(lg[r] * (pos[f] + 1.0))).astype(BF16) for r, (f, _) in enumerate(streams)]
    k_dec = [(k[r] * jnp.exp(lg[r] * (c - 1.0 - pos[f]))).astype(BF16) for r, (f, _) in enumerate(streams)]
    s = [s_scr[r] for r in range(RET_STREAMS)]
    o = [jnp.dot(scores[r].astype(BF16), v[r], preferred_element_type=F32)
         + jnp.dot(q_dec[r], s[r].astype(BF16), preferred_element_type=F32) for r in range(RET_STREAMS)]
    s_new = [s[r] * jnp.exp(lg[r] * c) + _tn_dot(k_dec[r], v[r]) for r in range(RET_STREAMS)]
    for r, (f, h) in enumerate(streams):
        (of_ref if f else ob_ref)[:, hs(h)] = o[r]
        s_scr[r] = s_new[r]

    @pl.when(n == pl.num_programs(1) - 1)
    def _():
        sf_ref[...] = s_scr[...]


def retention_scan(p, col0, decay_logit, cos, sin, s0):
    b, l, _ = p.shape
    c, dh, w = RET_CHUNK, RET_DIM, RET_WIDTH
    n = l // c
    cb = col0 // w
    fcol = lambda off: (lambda bi, ni, s: (bi, ni, cb + off))
    bcol = lambda off: (lambda bi, ni, s: (bi, n - 1 - ni, cb + off))
    ftab = lambda bi, ni, s: (ni, 0)
    btab = lambda bi, ni, s: (n - 1 - ni, 0)
    st = lambda bi, ni, s: (bi, 0, 0, 0)
    tok = lambda col: [pl.BlockSpec((None, c, w), col(j)) for j in range(3)]
    return pl.pallas_call(
        _ret_scan_body,
        grid_spec=pltpu.PrefetchScalarGridSpec(
            num_scalar_prefetch=1,
            grid=(b, n),
            in_specs=tok(fcol) + [pl.BlockSpec((c, dh), ftab)] * 2 + tok(bcol) + [pl.BlockSpec((c, dh), btab)] * 2
            + [pl.BlockSpec((None, RET_STREAMS, dh, dh), st)],
            out_specs=[pl.BlockSpec((None, c, w), lambda bi, ni, s: (bi, ni, 0)),
                       pl.BlockSpec((None, c, w), lambda bi, ni, s: (bi, n - 1 - ni, 0)),
                       pl.BlockSpec((None, RET_STREAMS, dh, dh), st)],
            scratch_shapes=[pltpu.VMEM((RET_STREAMS, dh, dh), F32)]),
        out_shape=[jax.ShapeDtypeStruct((b, l, w), F32), jax.ShapeDtypeStruct((b, l, w), F32),
                   jax.ShapeDtypeStruct((b, RET_STREAMS, dh, dh), F32)],
        compiler_params=_cparams(("parallel", "arbitrary")),
        name="retention_scan",
    )(decay_logit.reshape(-1), p, p, p, cos, sin, p, p, p, cos, sin, s0)


def _ret_final_body(of_ref, ob_ref, g_ref, w_ref, y_ref):
    for h in range(RET_HEADS):
        sl = slice(h * RET_DIM, (h + 1) * RET_DIM)
        o = of_ref[:, sl] + ob_ref[:, sl]
        mu = jnp.mean(o, axis=-1, keepdims=True)
        var = jnp.mean(jnp.square(o - mu), axis=-1, keepdims=True)
        y = (o - mu) * lax.rsqrt(var + EPS) * w_ref[:, sl]
        g = g_ref[:, sl]
        y_ref[:, sl] = (y * (g * jax.nn.sigmoid(g))).astype(y_ref.dtype)


def retention_finalize(o_f, o_b, p, gate_col, gn_w, tl):
    b, l, w = o_f.shape
    tok = lambda bi, i: (bi, i, 0)
    return pl.pallas_call(
        _ret_final_body,
        grid=(b, l // tl),
        in_specs=[pl.BlockSpec((None, tl, w), tok),
                  pl.BlockSpec((None, tl, w), tok),
                  pl.BlockSpec((None, tl, w), lambda bi, i: (bi, i, gate_col // w)),
                  pl.BlockSpec((1, w), lambda bi, i: (0, 0))],
        out_specs=pl.BlockSpec((None, tl, w), tok),
        out_shape=jax.ShapeDtypeStruct((b, l, w), BF16),
        compiler_params=_cparams(("parallel", "parallel")),
        name="retention_finalize",
    )(o_f, o_b, p, gn_w.reshape(1, w))


def retention_rope_tables(l):
    inv = ROPE_THETA ** (-jnp.linspace(0.0, 1.0, RET_DIM // 2, dtype=F32))
    ang = jnp.arange(l, dtype=F32)[:, None] * inv
    return (jnp.concatenate([jnp.cos(ang), jnp.cos(ang)], axis=-1),
            jnp.concatenate([-jnp.sin(ang), jnp.sin(ang)], axis=-1))


def retention_pallas(p_lat, p_ctx, decay_logit, gn_w, need_ctx, col0=0):
    b, l, _ = p_lat.shape
    n_ctx = p_ctx.shape[1]
    gate_col = col0 + 3 * RET_WIDTH
    s0 = jnp.zeros((b, RET_STREAMS, RET_DIM, RET_DIM), F32)
    oc_f, oc_b, s_ctx = retention_scan(p_ctx, col0, decay_logit, *retention_rope_tables(n_ctx), s0)
    ol_f, ol_b, _ = retention_scan(p_lat, col0, decay_logit, *retention_rope_tables(l), s_ctx)
    out_lat = retention_finalize(ol_f, ol_b, p_lat, gate_col, gn_w, 1024)
    out_ctx = retention_finalize(oc_f, oc_b, p_ctx, gate_col, gn_w, n_ctx) if need_ctx else None
    return out_lat, out_ctx


DN_BLOCK = 128
DN_STREAMS = 2 * DN_HEADS


def _shift_rows(x, prev_row, next_row):
    n = x.shape[0]
    r = lax.broadcasted_iota(jnp.int32, (n, 1), 0)
    x_prev = jnp.where(r == 0, prev_row, pltpu.roll(x, 1, 0))
    x_next = jnp.where(r == n - 1, next_row, pltpu.roll(x, n - 1, 0))
    return x_prev, x_next


def _conv3(x_ref, xp_ref, xn_ref, w_ref):
    i = pl.program_id(1)
    prev_row = jnp.where(i > 0, xp_ref[7:8, :], 0.0)
    next_row = jnp.where(i < pl.num_programs(1) - 1, xn_ref[0:1, :], 0.0)
    x = x_ref[...]
    x_prev, x_next = _shift_rows(x, prev_row, next_row)
    return x_prev * w_ref[0:1, :] + x * w_ref[1:2, :] + x_next * w_ref[2:3, :]


def _dn_prep_body(x_ref, xp_ref, xn_ref, ab_ref, cw_ref, alog_ref, dtb_ref,
                  q_ref, k_ref, v_ref, gb_ref, gam_ref, gamt_ref):
    y = _conv3(x_ref, xp_ref, xn_ref, cw_ref)
    y = y * jax.nn.sigmoid(y)
    for h in range(DN_HEADS):
        for part, (ref, mul) in enumerate(((q_ref, DN_DIM ** -0.5), (k_ref, 1.0))):
            sl = slice(part * DN_WIDTH + h * DN_DIM, part * DN_WIDTH + (h + 1) * DN_DIM)
            t = y[:, sl]
            t = t * lax.rsqrt(jnp.sum(t * t, axis=-1, keepdims=True) + EPS)
            ref[:, h * DN_DIM:(h + 1) * DN_DIM] = t * mul if mul != 1.0 else t
    v_ref[...] = y[:, 2 * DN_WIDTH:]

    ab = ab_ref[...]
    z = ab + dtb_ref[...]
    softplus = jnp.maximum(z, 0.0) + jnp.log(1.0 + jnp.exp(-jnp.abs(z)))
    g = -jnp.exp(alog_ref[...]) * softplus
    lane = lax.broadcasted_iota(jnp.int32, ab.shape, 1)
    gb_ref[...] = jnp.where(lane < DN_STREAMS, g, jax.nn.sigmoid(ab))

    c = DN_BLOCK
    ii = lax.broadcasted_iota(jnp.int32, (c, c), 0)
    jj = lax.broadcasted_iota(jnp.int32, (c, c), 1)
    tri_f = (ii >= jj).astype(F32)
    tri_b = (ii <= jj).astype(F32)
    lane_c = lax.broadcasted_iota(jnp.int32, (c, ab.shape[1]), 1)
    for n in range(x_ref.shape[0] // c):
        gc = g[n * c:(n + 1) * c, :]
        cf = jnp.dot(tri_f, gc, preferred_element_type=F32, precision=lax.Precision.HIGHEST)
        cb = jnp.dot(tri_b, gc, preferred_element_type=F32, precision=lax.Precision.HIGHEST)
        gam = jnp.where(lane_c < DN_HEADS, cf, cb)
        gam_ref[n * c:(n + 1) * c, :] = gam
        gamt_ref[n] = gam.T[:DN_STREAMS, :]


def dn_prep(p, ab_col, conv_w, a_log, dt_bias, tl):
    b, l, _ = p.shape
    c3 = 3 * DN_WIDTH
    nblk8 = l // 8
    lanes = 128
    pad8 = lambda a: jnp.pad(a.reshape(1, DN_STREAMS).astype(F32), ((0, 0), (0, lanes - DN_STREAMS)))
    tok = lambda bi, i: (bi, i, 0)
    return pl.pallas_call(
        _dn_prep_body,
        grid=(b, l // tl),
        in_specs=[pl.BlockSpec((None, tl, c3), tok),
                  pl.BlockSpec((None, 8, c3), lambda bi, i: (bi, jnp.maximum(i * (tl // 8) - 1, 0), 0)),
                  pl.BlockSpec((None, 8, c3), lambda bi, i: (bi, jnp.minimum((i + 1) * (tl // 8), nblk8 - 1), 0)),
                  pl.BlockSpec((None, tl, lanes), lambda bi, i: (bi, i, ab_col // lanes)),
                  pl.BlockSpec((3, c3), lambda bi, i: (0, 0)),
                  pl.BlockSpec((1, lanes), lambda bi, i: (0, 0)),
                  pl.BlockSpec((1, lanes), lambda bi, i: (0, 0))],
        out_specs=[pl.BlockSpec((None, tl, DN_WIDTH), tok)] * 3
        + [pl.BlockSpec((None, tl, lanes), tok)] * 2
        + [pl.BlockSpec((None, tl // DN_BLOCK, DN_STREAMS, DN_BLOCK), lambda bi, i: (bi, i, 0, 0))],
        out_shape=[jax.ShapeDtypeStruct((b, l, DN_WIDTH), F32)] * 3
        + [jax.ShapeDtypeStruct((b, l, lanes), F32)] * 2
        + [jax.ShapeDtypeStruct((b, l // DN_BLOCK, DN_STREAMS, DN_BLOCK), F32)],
        compiler_params=_cparams(("parallel", "parallel")),
        name="dn_prep",
    )(p, p, p, p, conv_w, pad8(a_log), pad8(dt_bias))


def _split_bf16(a):
    hi = a.astype(BF16)
    return hi, (a - hi.astype(F32)).astype(BF16)


def _dot3(a, b):
    a_hi, a_lo = _split_bf16(a)
    b_hi, b_lo = _split_bf16(b)
    d = lambda x, y: jnp.dot(x, y, preferred_element_type=F32)
    return d(a_hi, b_hi) + (d(a_hi, b_lo) + d(a_lo, b_hi))


def _unit_triangular_inverses(ms, ii, jj):
    c = ms[0].shape[0]
    eye = (ii == jj).astype(F32)
    pair = (ii // 2) == (jj // 2)
    ts = [eye - jnp.where(pair, m, 0.0) for m in ms]
    s = 2
    while s < c:
        sub = ((ii // (2 * s)) == (jj // (2 * s))) & ((ii // s) != (jj // s))
        xs = [_dot3(jnp.where(sub, m, 0.0), t) for m, t in zip(ms, ts)]
        ts = [t - _dot3(t, x) for t, x in zip(ts, xs)]
        s *= 2
    return ts


def _dn_scan_body(qf_ref, kf_ref, vf_ref, gbf_ref, gamf_ref, gamtf_ref,
                  qb_ref, kb_ref, vb_ref, gbb_ref, gamb_ref, gamtb_ref, s0_ref,
                  of_ref, ob_ref, sf_ref, s_scr):
    n = pl.program_id(1)

    @pl.when(n == 0)
    def _():
        s_scr[...] = s0_ref[...]

    c, dh = DN_BLOCK, DN_DIM
    ii = lax.broadcasted_iota(jnp.int32, (c, c), 0)
    jj = lax.broadcasted_iota(jnp.int32, (c, c), 1)
    incl = {True: ii >= jj, False: ii <= jj}
    strict = {True: ii > jj, False: ii < jj}
    srcs = {True: (qf_ref, kf_ref, vf_ref, gbf_ref, gamf_ref, gamtf_ref),
            False: (qb_ref, kb_ref, vb_ref, gbb_ref, gamb_ref, gamtb_ref)}
    streams = [(d == 0, h) for d in range(2) for h in range(DN_HEADS)]
    rng = range(DN_STREAMS)
    hs = lambda h: slice(h * dh, (h + 1) * dh)

    q = [srcs[f][0][:, hs(h)] for f, h in streams]
    k = [srcs[f][1][:, hs(h)] for f, h in streams]
    v = [srcs[f][2][:, hs(h)] for f, h in streams]
    beta = [srcs[f][3][:, DN_STREAMS + r:DN_STREAMS + r + 1] for r, (f, _) in enumerate(streams)]
    gcol = [srcs[f][4][:, r:r + 1] for r, (f, _) in enumerate(streams)]
    grow = [srcs[f][5][r:r + 1, :] for r, (f, _) in enumerate(streams)]
    g_last = [gcol[r][c - 1:c, :] if f else gcol[r][0:1, :] for r, (f, _) in enumerate(streams)]

    e = [jnp.exp(jnp.where(incl[f], gcol[r] - grow[r], 0.0)) for r, (f, _) in enumerate(streams)]
    kb = [k[r] * beta[r] for r in rng]
    k16 = [k[r].astype(BF16) for r in rng]
    m = [_nt_dot(kb[r].astype(BF16), k16[r]) * jnp.where(strict[f], e[r], 0.0) for r, (f, _) in enumerate(streams)]
    attn = [(_nt_dot(q[r].astype(BF16), k16[r]) * jnp.where(incl[f], e[r], 0.0)).astype(BF16)
            for r, (f, _) in enumerate(streams)]
    t = _unit_triangular_inverses(m, ii, jj)
    eg = [jnp.exp(gcol[r]) for r in rng]
    sol = [_dot3(t[r], jnp.concatenate([v[r] * beta[r], kb[r] * eg[r]], axis=-1)) for r in rng]
    s = [s_scr[r] for r in rng]
    s16 = [s[r].astype(BF16) for r in rng]
    v_new = [(sol[r][:, :dh] - jnp.dot(sol[r][:, dh:].astype(BF16), s16[r], preferred_element_type=F32)).astype(BF16)
             for r in rng]
    o = [jnp.dot((q[r] * eg[r]).astype(BF16), s16[r], preferred_element_type=F32)
         + jnp.dot(attn[r], v_new[r], preferred_element_type=F32) for r in rng]
    s_new = [s[r] * jnp.exp(g_last[r]) + _tn_dot((k[r] * jnp.exp(g_last[r] - gcol[r])).astype(BF16), v_new[r])
             for r in rng]
    for r, (f, h) in enumerate(streams):
        (of_ref if f else ob_ref)[:, hs(h)] = o[r]
        s_scr[r] = s_new[r]

    @pl.when(n == pl.num_programs(1) - 1)
    def _():
        sf_ref[...] = s_scr[...]


def dn_scan(q, k, v, gb, gam, gamt, s0):
    b, l, w = q.shape
    c = DN_BLOCK
    n = l // c
    lanes = gb.shape[-1]
    f3 = lambda bi, ni: (bi, ni, 0)
    b3 = lambda bi, ni: (bi, n - 1 - ni, 0)
    f4 = lambda bi, ni: (bi, ni, 0, 0)
    b4 = lambda bi, ni: (bi, n - 1 - ni, 0, 0)
    st = lambda bi, ni: (bi, 0, 0, 0)

    def specs(m3, m4):
        return [pl.BlockSpec((None, c, w), m3)] * 3 + [pl.BlockSpec((None, c, lanes), m3)] * 2 + [
            pl.BlockSpec((None, None, DN_STREAMS, c), m4)]

    return pl.pallas_call(
        _dn_scan_body,
        grid=(b, n),
        in_specs=specs(f3, f4) + specs(b3, b4) + [pl.BlockSpec((None, DN_STREAMS, DN_DIM, DN_DIM), st)],
        out_specs=[pl.BlockSpec((None, c, w), f3), pl.BlockSpec((None, c, w), b3),
                   pl.BlockSpec((None, DN_STREAMS, DN_DIM, DN_DIM), st)],
        out_shape=[jax.ShapeDtypeStruct((b, l, w), F32), jax.ShapeDtypeStruct((b, l, w), F32),
                   jax.ShapeDtypeStruct((b, DN_STREAMS, DN_DIM, DN_DIM), F32)],
        scratch_shapes=[pltpu.VMEM((DN_STREAMS, DN_DIM, DN_DIM), F32)],
        compiler_params=_cparams(("parallel", "arbitrary")),
        name="dn_scan",
    )(q, k, v, gb, gam, gamt, q, k, v, gb, gam, gamt, s0)


def _dn_final_body(of_ref, ob_ref, g_ref, w_ref, y_ref):
    for h in range(DN_HEADS):
        sl = slice(h * DN_DIM, (h + 1) * DN_DIM)
        o = of_ref[:, sl] + ob_ref[:, sl]
        y = _head_rms(o, w_ref[...])
        g = g_ref[:, sl]
        y_ref[:, sl] = (y * (g * jax.nn.sigmoid(g))).astype(y_ref.dtype)


def dn_finalize(o_f, o_b, p, gate_col, norm_w, tl):
    b, l, w = o_f.shape
    tok = lambda bi, i: (bi, i, 0)
    return pl.pallas_call(
        _dn_final_body,
        grid=(b, l // tl),
        in_specs=[pl.BlockSpec((None, tl, w), tok), pl.BlockSpec((None, tl, w), tok),
                  pl.BlockSpec((None, tl, w), lambda bi, i: (bi, i, gate_col // w)),
                  pl.BlockSpec((1, DN_DIM), lambda bi, i: (0, 0))],
        out_specs=pl.BlockSpec((None, tl, w), tok),
        out_shape=jax.ShapeDtypeStruct((b, l, w), BF16),
        compiler_params=_cparams(("parallel", "parallel")),
        name="dn_finalize",
    )(o_f, o_b, p, norm_w.reshape(1, DN_DIM))


def deltanet_pallas(p_lat, p_ctx, conv_w, a_log, dt_bias, norm_w, need_ctx, ab_col=3072, gate_col=1536):
    b = p_lat.shape[0]
    n_ctx = p_ctx.shape[1]
    s0 = jnp.zeros((b, DN_STREAMS, DN_DIM, DN_DIM), F32)
    fc = dn_prep(p_ctx, ab_col, conv_w, a_log, dt_bias, n_ctx)
    oc_f, oc_b, s_ctx = dn_scan(*fc, s0)
    fl = dn_prep(p_lat, ab_col, conv_w, a_log, dt_bias, 1024)
    ol_f, ol_b, _ = dn_scan(*fl, s_ctx)
    out_lat = dn_finalize(ol_f, ol_b, p_lat, gate_col, norm_w, 1024)
    out_ctx = dn_finalize(oc_f, oc_b, p_ctx, gate_col, norm_w, n_ctx) if need_ctx else None
    return out_lat, out_ctx


def _hy_factors(n):
    n2 = 256 if n >= 32768 else 128
    return n // n2, n2


def _bf16_pair(m):
    hi = m.astype(ml_dtypes.bfloat16)
    lo = (m - hi.astype(np.float64)).astype(ml_dtypes.bfloat16)
    return hi, lo


def _hy_constants(l):
    n = 2 * l
    n1, n2 = _hy_factors(n)
    a1 = 2.0 * np.pi * np.outer(np.arange(n1), np.arange(n1)) / n1
    c1, s1 = np.cos(a1), np.sin(a1)
    h = n1 // 2
    rows_fwd = np.block([[c1[:, :h], s1[:, :h]], [-s1[:, :h], c1[:, :h]]])
    rows_taps = np.concatenate([c1, -s1], axis=0)
    rows_inv = np.block([[c1[:h], -s1[:h]], [s1[:h], c1[:h]]]) / n
    a2 = 2.0 * np.pi * np.outer(np.arange(n2), np.arange(n2)) / n2
    c2, s2 = np.cos(a2), np.sin(a2)
    slab_fwd = np.block([[c2, s2], [-s2, c2]])
    slab_inv = np.block([[c2, -s2], [s2, c2]])
    th = 2.0 * np.pi * np.outer(np.arange(n1), np.arange(n2)) / n
    lanes = lambda t: jnp.broadcast_to(jnp.asarray(t, F32)[:, :, None], (n1, n2, 128))
    return dict(n1=n1, n2=n2, rows_fwd=_bf16_pair(rows_fwd), rows_taps=_bf16_pair(rows_taps),
                rows_inv=_bf16_pair(rows_inv), slab_fwd=_bf16_pair(slab_fwd), slab_inv=_bf16_pair(slab_inv),
                tw_cos=lanes(np.cos(th)), tw_sin=lanes(np.sin(th)))


def _const_dot3(m_hi, m_lo, x):
    x_hi, x_lo = _split_bf16(x)
    d = lambda a, b: jnp.dot(a, b, preferred_element_type=F32)
    return d(m_hi, x_hi) + (d(m_hi, x_lo) + d(m_lo, x_hi))


def _hy_prep_body(x_ref, xp_ref, xn_ref, w_ref, v_ref, x1_ref, x2_ref):
    y = _conv3(x_ref, xp_ref, xn_ref, w_ref)
    v_ref[...] = y[:, :HY_CH]
    x1_ref[...] = y[:, HY_CH:2 * HY_CH]
    x2_ref[...] = y[:, 2 * HY_CH:]


def hy_prep(p, col0, conv_w, tl):
    b, l, _ = p.shape
    c3 = 3 * HY_CH
    nblk8 = l // 8
    cb = col0 // c3
    tok = lambda bi, i: (bi, i, 0)
    return pl.pallas_call(
        _hy_prep_body,
        grid=(b, l // tl),
        in_specs=[pl.BlockSpec((None, tl, c3), lambda bi, i: (bi, i, cb)),
                  pl.BlockSpec((None, 8, c3), lambda bi, i: (bi, jnp.maximum(i * (tl // 8) - 1, 0), cb)),
                  pl.BlockSpec((None, 8, c3), lambda bi, i: (bi, jnp.minimum((i + 1) * (tl // 8), nblk8 - 1), cb)),
                  pl.BlockSpec((3, c3), lambda bi, i: (0, 0))],
        out_specs=[pl.BlockSpec((None, tl, HY_CH), tok)] * 3,
        out_shape=[jax.ShapeDtypeStruct((b, l, HY_CH), F32)] * 3,
        compiler_params=_cparams(("parallel", "parallel")),
        name="hy_prep",
    )(p, p, p, conv_w)


def _hy_filter_body(z_ref, w1_ref, b1_ref, w2_ref, b2_ref, w3_ref, rates_ref, taps_ref, sum_ref, *, zero_row):
    i = pl.program_id(0)
    hp = lax.Precision.HIGHEST
    z = z_ref[...]
    h = jnp.sin(HY_SIN_FREQ * (jnp.dot(z, w1_ref[...], preferred_element_type=F32, precision=hp) + b1_ref[...]))
    h = jnp.sin(HY_SIN_FREQ * (jnp.dot(h, w2_ref[...], preferred_element_type=F32, precision=hp) + b2_ref[...]))
    h = jnp.dot(h, w3_ref[...], preferred_element_type=F32, precision=hp)
    h = h * jnp.exp(-z[:, 0:1] * rates_ref[...])

    @pl.when(i == 0)
    def _():
        sum_ref[...] = jnp.zeros_like(sum_ref)

    sum_ref[...] += jnp.sum(jnp.abs(h), axis=0, keepdims=True)
    tr = z.shape[0]
    row = i * tr + lax.broadcasted_iota(jnp.int32, (tr, 1), 0)
    h = jnp.where(row == zero_row, 0.0, h)
    for o in range(HY_ORDER):
        taps_ref[o] = h[:, o * HY_CH:(o + 1) * HY_CH]


def hy_filter_taps(l, w1, b1, w2, b2, w3, tr=512):
    n = 2 * l
    r = np.arange(n)
    pos = np.where(r < l, r, n - r).astype(np.float64)
    pos[l] = 0.0
    pos = jnp.asarray(pos, F32)
    t = pos / max(l - 1, 1)
    bands = jnp.linspace(1e-4, HY_BANDS - 1, HY_BANDS, dtype=F32)
    phase = (2.0 * math.pi / l) * pos[:, None] * bands[None, :]
    kf = 32
    z = jnp.concatenate([t[:, None], jnp.cos(phase), -jnp.sin(phase)], axis=-1)
    z = jnp.pad(z, ((0, 0), (0, kf - z.shape[1])))
    w1p = jnp.pad(w1.astype(F32), ((0, kf - w1.shape[0]), (0, 0)))
    hid = w1.shape[1]
    w3d = w3.astype(F32).reshape(hid, HY_ORDER, 2, HY_CH).transpose(2, 0, 1, 3).reshape(2, hid, HY_ORDER * HY_CH)
    rates = jnp.abs(jnp.linspace(math.log(HY_TARGET) / HY_DECAY_LONG, math.log(HY_TARGET) / HY_DECAY_SHORT,
                                 HY_CH, dtype=F32))
    rates = jnp.tile(rates, HY_ORDER).reshape(1, HY_ORDER * HY_CH)
    fixed = lambda i: (0, 0)
    return pl.pallas_call(
        functools.partial(_hy_filter_body, zero_row=l),
        grid=(n // tr,),
        in_specs=[pl.BlockSpec((tr, kf), lambda i: (i, 0)),
                  pl.BlockSpec((kf, hid), fixed), pl.BlockSpec((1, hid), fixed),
                  pl.BlockSpec((hid, hid), fixed), pl.BlockSpec((1, hid), fixed),
                  pl.BlockSpec((None, hid, HY_ORDER * HY_CH), lambda i: (i // (l // tr), 0, 0)),
                  pl.BlockSpec((1, HY_ORDER * HY_CH), fixed)],
        out_specs=[pl.BlockSpec((HY_ORDER, tr, HY_CH), lambda i: (0, i, 0)),
                   pl.BlockSpec((1, HY_ORDER * HY_CH), fixed)],
        out_shape=[jax.ShapeDtypeStruct((HY_ORDER, n, HY_CH), F32),
                   jax.ShapeDtypeStruct((1, HY_ORDER * HY_CH), F32)],
        compiler_params=_cparams(("arbitrary",)),
        name="hy_filter_taps",
    )(z, w1p, b1.astype(F32).reshape(1, hid), w2.astype(F32), b2.astype(F32).reshape(1, hid), w3d, rates)


HY_ROWS_GROUP = 8


def _hy_rows_body(*refs, mode):
    if mode == "taps":
        mh_ref, ml_ref, x_ref, sum_ref, o_ref = refs
        scale = 1.0 / (sum_ref[...] + EPS)
    elif mode == "fwd":
        mh_ref, ml_ref, x_ref, o_ref = refs
    else:
        mh_ref, ml_ref, x_ref, xn_ref, z_ref, skip_ref, o_ref = refs
    for j in range(HY_ROWS_GROUP):
        x = jnp.concatenate([x_ref[0, :, j, :], x_ref[1, :, j, :]], axis=0)
        if mode == "taps":
            x = x * scale
        y = _const_dot3(mh_ref[...], ml_ref[...], x)
        half = y.shape[0] // 2
        for bi in range(2):
            yb = y[bi * half:(bi + 1) * half]
            if mode == "inv":
                yb = xn_ref[bi, :, j, :] * (yb + z_ref[bi, :, j, :] * skip_ref[...])
            o_ref[bi, :, j, :] = yb


def hy_rows_pass(mode, mat, x, extra=()):
    mh, ml = mat
    n2, c = x.shape[-2:]
    g = HY_ROWS_GROUP
    blk = lambda rows: pl.BlockSpec((2, rows, g, c), lambda j: (0, 0, j, 0))
    mspec = pl.BlockSpec(mh.shape, lambda j: (0, 0))
    vec = pl.BlockSpec((1, c), lambda j: (0, 0))
    if mode == "taps":
        n1 = 2 * x.shape[1]
        in_specs = [mspec, mspec, blk(n1 // 2), vec]
        out_rows = n1
    elif mode == "fwd":
        n1 = 2 * x.shape[1]
        in_specs = [mspec, mspec, blk(n1 // 2)]
        out_rows = n1
    else:
        n1 = x.shape[1]
        in_specs = [mspec, mspec, blk(n1), blk(n1 // 2), blk(n1 // 2), vec]
        out_rows = n1 // 2
    return pl.pallas_call(
        functools.partial(_hy_rows_body, mode=mode),
        grid=(n2 // g,),
        in_specs=in_specs,
        out_specs=blk(out_rows),
        out_shape=jax.ShapeDtypeStruct((2, out_rows, n2, c), F32),
        compiler_params=_cparams(("parallel",)),
        name="hy_rows_" + mode,
    )(mh, ml, x, *extra)


def _hy_slab_body(*refs, with_filter):
    if with_filter:
        fh_ref, fl_ref, ih_ref, il_ref, tc_ref, ts_ref, a_ref, h_ref, o_ref = refs
    else:
        fh_ref, fl_ref, tc_ref, ts_ref, a_ref, o_ref = refs
    reps = a_ref.shape[-1] // tc_ref.shape[-1]
    n2 = a_ref.shape[2]
    for s in range(a_ref.shape[1]):
        tc = jnp.concatenate([tc_ref[s]] * reps, axis=1)
        ts = jnp.concatenate([ts_ref[s]] * reps, axis=1)
        ar, ai = a_ref[0, s], a_ref[1, s]
        x = _const_dot3(fh_ref[...], fl_ref[...],
                        jnp.concatenate([ar * tc + ai * ts, ai * tc - ar * ts], axis=0))
        xr, xi = x[:n2], x[n2:]
        if not with_filter:
            o_ref[0, s] = xr
            o_ref[1, s] = xi
            continue
        hr, hi = h_ref[0, s], h_ref[1, s]
        y = _const_dot3(ih_ref[...], il_ref[...],
                        jnp.concatenate([xr * hr - xi * hi, xr * hi + xi * hr], axis=0))
        yr, yi = y[:n2], y[n2:]
        o_ref[0, s] = yr * tc - yi * ts
        o_ref[1, s] = yi * tc + yr * ts


HY_SLABS_PER_STEP = 2


def hy_slab_pass(consts, a, h=None):
    _, n1, n2, c = a.shape
    kb = HY_SLABS_PER_STEP
    fh, fl = consts["slab_fwd"]
    mspec = _resident(fh.shape)
    tw = pl.BlockSpec((kb, n2, 128), lambda k: (k, 0, 0))
    slab = pl.BlockSpec((2, kb, n2, c), lambda k: (0, k, 0, 0))
    if h is None:
        args, in_specs = [fh, fl, consts["tw_cos"], consts["tw_sin"], a], [mspec, mspec, tw, tw, slab]
    else:
        ih, il = consts["slab_inv"]
        args = [fh, fl, ih, il, consts["tw_cos"], consts["tw_sin"], a, h]
        in_specs = [mspec] * 4 + [tw, tw, slab, slab]
    return pl.pallas_call(
        functools.partial(_hy_slab_body, with_filter=h is not None),
        grid=(n1 // kb,),
        in_specs=in_specs,
        out_specs=slab,
        out_shape=jax.ShapeDtypeStruct(a.shape, F32),
        compiler_params=_cparams(("parallel",)),
        name="hy_slab_conv" if h is not None else "hy_slab_fwd",
    )(*args)


def hyena_pallas(p, conv_w, w1, b1, w2, b2, w3, skip, col0=RET_COLS):
    b, l, _ = p.shape
    assert b == 2, "the two batch rows are packed as one complex signal"
    c = HY_CH
    consts = _hy_constants(l)
    n1, n2 = consts["n1"], consts["n2"]
    taps, sums = hy_filter_taps(l, w1, b1, w2, b2, w3)
    spectra = []
    for o in range(HY_ORDER):
        a_h = hy_rows_pass("taps", consts["rows_taps"], taps[o].reshape(2, n1 // 2, n2, c),
                           extra=(sums[:, o * c:(o + 1) * c],))
        spectra.append(hy_slab_pass(consts, a_h))
    split = lambda t: t.reshape(2, n1 // 2, n2, c)
    v, x1, x2 = hy_prep(p, col0, conv_w, 1024)
    z = split(v)
    for o, xn in enumerate((x1, x2)):
        a = hy_rows_pass("fwd", consts["rows_fwd"], z)
        bm = hy_slab_pass(consts, a, spectra[o])
        z = hy_rows_pass("inv", consts["rows_inv"], bm,
                         extra=(split(xn), z, skip[o].astype(F32).reshape(1, c)))
    return z.reshape(b, l, c)


def rmsnorm(x, w):
    xf = x.astype(F32)
    y = xf * lax.rsqrt(jnp.mean(xf * xf, axis=-1, keepdims=True) + EPS)
    return (y * w.astype(F32)).astype(x.dtype)


def l2norm(x):
    return x * lax.rsqrt(jnp.sum(x * x, axis=-1, keepdims=True) + EPS)


def rotate_half(x, ang):
    cos = jnp.cos(ang)[:, None, :]
    sin = jnp.sin(ang)[:, None, :]
    x1, x2 = jnp.split(x, 2, axis=-1)
    return jnp.concatenate([x1 * cos - x2 * sin, x2 * cos + x1 * sin], axis=-1).astype(x.dtype)


def axial_angles(rows):
    m = SWA_DIM // 4
    inv = ROPE_THETA ** (-jnp.arange(m, dtype=F32) / m)
    row = jnp.repeat(jnp.arange(rows, dtype=F32), GRID_W)
    col = jnp.broadcast_to(jnp.arange(GRID_W, dtype=F32), (rows, GRID_W)).reshape(-1)
    return row[:, None] * inv, col[:, None] * inv


def axial_rope(x, ang_row, ang_col):
    h = x.shape[-1] // 2
    return jnp.concatenate([rotate_half(x[..., :h], ang_row), rotate_half(x[..., h:], ang_col)], axis=-1)


def short_conv(x, w):
    K, C = w.shape
    return lax.conv_general_dilated(x, w[:, None, :].astype(x.dtype), window_strides=(1,),
                                    padding=[(K // 2, K // 2)],
                                    dimension_numbers=('NWC', 'WIO', 'NWC'),
                                    feature_group_count=C)


def flip_time(t):
    return jnp.flip(t, axis=2)


def keep_time(t):
    return t


def delta_chunk_scan(q, k, v, g, beta, s0):
    B, H, L, dk = q.shape
    dv = v.shape[-1]
    C = DN_CHUNK
    N = L // C
    q = q.reshape(B, H, N, C, dk)
    k = k.reshape(B, H, N, C, dk)
    v = v.reshape(B, H, N, C, dv)
    g = g.reshape(B, H, N, C)
    beta = beta.reshape(B, H, N, C)
    gam = jnp.cumsum(g, axis=-1)
    idx = jnp.arange(C)
    incl = idx[:, None] >= idx[None, :]
    strict = idx[:, None] > idx[None, :]
    e = jnp.exp(jnp.where(incl, gam[..., :, None] - gam[..., None, :], 0.0))
    dec_incl = jnp.where(incl, e, 0.0)
    dec_strict = jnp.where(strict, e, 0.0)
    kb = k * beta[..., None]
    m = jnp.einsum('bhnid,bhnjd->bhnij', kb, k) * dec_strict
    a = m + jnp.eye(C, dtype=F32)
    rhs = jnp.concatenate([v * beta[..., None], kb * jnp.exp(gam)[..., None]], axis=-1)
    sol = lax.linalg.triangular_solve(a, rhs, left_side=True, lower=True, unit_diagonal=True)
    u, w = sol[..., :dv], sol[..., dv:]
    attn = jnp.einsum('bhnid,bhnjd->bhnij', q, k) * dec_incl
    q_dec = q * jnp.exp(gam)[..., None]
    k_dec = k * jnp.exp(gam[..., -1:] - gam)[..., None]
    chunk_dec = jnp.exp(gam[..., -1])

    def step(S, xs):
        u_n, w_n, a_n, qd_n, kd_n, cd_n = xs
        v_new = u_n - jnp.einsum('bhck,bhkv->bhcv', w_n, S)
        o = jnp.einsum('bhck,bhkv->bhcv', qd_n, S) + jnp.einsum('bhij,bhjv->bhiv', a_n, v_new)
        S = S * cd_n[..., None, None] + jnp.einsum('bhck,bhcv->bhkv', kd_n, v_new)
        return S, o

    xs = tuple(jnp.moveaxis(t, 2, 0) for t in (u, w, attn, q_dec, k_dec, chunk_dec))
    S, o = lax.scan(step, s0, xs)
    return jnp.moveaxis(o, 0, 2).reshape(B, H, L, dv), S


def dn_features(p, conv_w, a_log, dt_bias):
    B, L, _ = p.shape
    qkv = jax.nn.silu(short_conv(p[..., :3 * DN_WIDTH], conv_w)).astype(F32)
    heads = lambda t: t.reshape(B, L, DN_HEADS, DN_DIM).transpose(0, 2, 1, 3)
    q, k, v = (heads(t) for t in jnp.split(qkv, 3, axis=-1))
    q = l2norm(q) * DN_DIM ** -0.5
    k = l2norm(k)
    gate = p[..., 3 * DN_WIDTH:4 * DN_WIDTH]
    ab = p[..., 4 * DN_WIDTH:].astype(F32).reshape(B, L, 2, 2, DN_HEADS)
    g = -jnp.exp(a_log.astype(F32)) * jax.nn.softplus(ab[:, :, 0] + dt_bias.astype(F32))
    beta = jax.nn.sigmoid(ab[:, :, 1])
    return q, k, v, g.transpose(2, 0, 3, 1), beta.transpose(2, 0, 3, 1), gate


def head_norm_gate(o, gate, w):
    B, H, L, dv = o.shape
    y = rmsnorm(o.transpose(0, 2, 1, 3), w)
    return (y * jax.nn.silu(gate.astype(F32).reshape(B, L, H, dv))).reshape(B, L, H * dv)


def gated_deltanet(p_lat, p_ctx, conv_w, a_log, dt_bias, norm_w, need_ctx):
    lat = dn_features(p_lat, conv_w, a_log, dt_bias)
    cx = dn_features(p_ctx, conv_w, a_log, dt_bias)
    B = p_lat.shape[0]
    o_lat, o_ctx = [], []
    for d in range(2):
        fl = flip_time if d == 1 else keep_time
        s0 = jnp.zeros((B, DN_HEADS, DN_DIM, DN_DIM), F32)
        oc, s_ctx = delta_chunk_scan(fl(cx[0]), fl(cx[1]), fl(cx[2]), fl(cx[3][d]), fl(cx[4][d]), s0)
        ol, _ = delta_chunk_scan(fl(lat[0]), fl(lat[1]), fl(lat[2]), fl(lat[3][d]), fl(lat[4][d]), s_ctx)
        o_lat.append(fl(ol))
        o_ctx.append(fl(oc))
    out_lat = head_norm_gate(o_lat[0] + o_lat[1], lat[5], norm_w)
    out_ctx = head_norm_gate(o_ctx[0] + o_ctx[1], cx[5], norm_w) if need_ctx else None
    return out_lat, out_ctx


def gqa_qkv(p, q_norm_w, k_norm_w):
    B, L, _ = p.shape
    nq, nk = SWA_HEADS * SWA_DIM, SWA_KV_HEADS * SWA_DIM
    q = rmsnorm(p[..., :nq].reshape(B, L, SWA_HEADS, SWA_DIM), q_norm_w)
    k = rmsnorm(p[..., nq:nq + nk].reshape(B, L, SWA_KV_HEADS, SWA_DIM), k_norm_w)
    v = p[..., nq + nk:].reshape(B, L, SWA_KV_HEADS, SWA_DIM)
    return q, k, v


def banded_attention(q, k, v, kc, vc, sink):
    B, L, Hq, d = q.shape
    Hkv = k.shape[2]
    G = Hq // Hkv
    W = SWA_BLOCK
    NB = L // W
    qb = q.reshape(B, NB, W, Hkv, G, d)

    def band(t):
        tp = jnp.pad(t.reshape(B, NB, W, Hkv, d), ((0, 0), (1, 1), (0, 0), (0, 0), (0, 0)))
        return jnp.concatenate([tp[:, :-2], tp[:, 1:-1], tp[:, 2:]], axis=2)

    kb, vb = band(k), band(v)
    scale = d ** -0.5
    s_loc = jnp.einsum('bnqhgd,bnkhd->bnhgqk', qb, kb).astype(F32) * scale
    s_ctx = jnp.einsum('bnqhgd,bchd->bnhgqc', qb, kc).astype(F32) * scale
    rel = (jnp.arange(3 * W) - W)[None, :] - jnp.arange(W)[:, None]
    kblk = jnp.arange(NB)[:, None] + jnp.arange(3 * W)[None, :] // W - 1
    valid = (jnp.abs(rel) <= SWA_WINDOW)[None] & ((kblk >= 0) & (kblk < NB))[:, None, :]
    s_loc = jnp.where(valid[None, :, None, None], s_loc, -jnp.inf)
    s_sink = jnp.broadcast_to(sink.astype(F32).reshape(Hkv, G, 1, 1), s_loc.shape[:-1] + (1,))
    prob = jax.nn.softmax(jnp.concatenate([s_loc, s_ctx, s_sink], axis=-1), axis=-1).astype(v.dtype)
    n_ctx = kc.shape[1]
    o = (jnp.einsum('bnhgqk,bnkhd->bnqhgd', prob[..., :3 * W], vb)
         + jnp.einsum('bnhgqc,bchd->bnqhgd', prob[..., 3 * W:3 * W + n_ctx], vc))
    return o.reshape(B, L, Hq * d)


def context_attention(qc, kc, vc, sink):
    B, Cn, Hq, d = qc.shape
    Hkv = kc.shape[2]
    G = Hq // Hkv
    q = qc.reshape(B, Cn, Hkv, G, d)
    s = jnp.einsum('bqhgd,bkhd->bhgqk', q, kc).astype(F32) * d ** -0.5
    s_sink = jnp.broadcast_to(sink.astype(F32).reshape(Hkv, G, 1, 1), s.shape[:-1] + (1,))
    prob = jax.nn.softmax(jnp.concatenate([s, s_sink], axis=-1), axis=-1)[..., :-1].astype(vc.dtype)
    return jnp.einsum('bhgqk,bkhd->bqhgd', prob, vc).reshape(B, Cn, Hq * d)


def window_gqa(p_lat, p_ctx, ang_row, ang_col, q_norm_w, k_norm_w, sink, need_ctx):
    ql, kl, vl = gqa_qkv(p_lat, q_norm_w, k_norm_w)
    ql = axial_rope(ql, ang_row, ang_col)
    kl = axial_rope(kl, ang_row, ang_col)
    qc, kc, vc = gqa_qkv(p_ctx, q_norm_w, k_norm_w)
    o_lat = banded_attention(ql, kl, vl, kc, vc, sink)
    o_ctx = context_attention(qc, kc, vc, sink) if need_ctx else None
    return o_lat, o_ctx


def ab_mixer(p_lat, p_ctx, ang_row, ang_col, dn_conv_w, dn_a_log, dn_dt_bias, dn_norm_w,
             q_norm_w, k_norm_w, sink, need_ctx):
    a_lat, a_ctx = gated_deltanet(p_lat[..., :DN_COLS], p_ctx[..., :DN_COLS], dn_conv_w, dn_a_log,
                                  dn_dt_bias, dn_norm_w, need_ctx)
    b_lat, b_ctx = window_gqa(p_lat[..., DN_COLS:], p_ctx[..., DN_COLS:], ang_row, ang_col,
                              q_norm_w, k_norm_w, sink, need_ctx)
    o_lat = jnp.concatenate([a_lat, b_lat.astype(F32)], axis=-1)
    o_ctx = jnp.concatenate([a_ctx, b_ctx.astype(F32)], axis=-1) if need_ctx else None
    return o_lat, o_ctx


def retention_chunk_scan(q, k, v, log_gamma, s0):
    B, H, L, dk = q.shape
    dv = v.shape[-1]
    C = RET_CHUNK
    N = L // C
    q = q.reshape(B, H, N, C, dk)
    k = k.reshape(B, H, N, C, dk)
    v = v.reshape(B, H, N, C, dv)
    pos = jnp.arange(C, dtype=F32)
    rel = pos[:, None] - pos[None, :]
    incl = rel >= 0
    lg = log_gamma[:, None, None]
    dmat = jnp.where(incl, jnp.exp(lg * jnp.where(incl, rel, 0.0)), 0.0)
    scores = jnp.einsum('bhnid,bhnjd->bhnij', q, k) * dmat[:, None]
    o_in = jnp.einsum('bhnij,bhnje->bhnie', scores, v)
    q_dec = q * jnp.exp(log_gamma[:, None] * (pos + 1.0))[:, None, :, None]
    k_dec = k * jnp.exp(log_gamma[:, None] * (C - 1.0 - pos))[:, None, :, None]
    chunk_dec = jnp.exp(log_gamma * C)[None, :, None, None]

    def step(S, xs):
        qd, kd, vn = xs
        o = jnp.einsum('bhcd,bhde->bhce', qd, S)
        S = S * chunk_dec + jnp.einsum('bhcd,bhce->bhde', kd, vn)
        return S, o

    xs = tuple(jnp.moveaxis(t, 2, 0) for t in (q_dec, k_dec, v))
    S, o_x = lax.scan(step, s0, xs)
    o = o_in + jnp.moveaxis(o_x, 0, 2)
    return o.reshape(B, H, L, dv), S


def ret_features(p):
    B, L, _ = p.shape
    q, k, v, g = jnp.split(p, 4, axis=-1)
    inv = ROPE_THETA ** (-jnp.linspace(0.0, 1.0, RET_DIM // 2, dtype=F32))
    ang = jnp.arange(L, dtype=F32)[:, None] * inv
    heads = lambda t: t.reshape(B, L, RET_HEADS, RET_DIM).astype(F32)
    q = rotate_half(heads(q), ang)
    k = rotate_half(heads(k), ang) * RET_DIM ** -0.5
    tr = lambda t: t.transpose(0, 2, 1, 3)
    return tr(q), tr(k), tr(heads(v)), g


def head_groupnorm_gate(o, gate, w):
    B, H, L, dv = o.shape
    o = o.transpose(0, 2, 1, 3)
    mu = jnp.mean(o, axis=-1, keepdims=True)
    var = jnp.mean(jnp.square(o - mu), axis=-1, keepdims=True)
    y = (o - mu) * lax.rsqrt(var + EPS) * w.astype(F32).reshape(H, dv)
    return (y * jax.nn.silu(gate.astype(F32).reshape(B, L, H, dv))).reshape(B, L, H * dv)


def retention(p_lat, p_ctx, decay_logit, gn_w, need_ctx):
    lat, cx = ret_features(p_lat), ret_features(p_ctx)
    log_gamma = jax.nn.log_sigmoid(decay_logit.astype(F32))
    B = p_lat.shape[0]
    o_lat, o_ctx = [], []
    for d in range(2):
        fl = flip_time if d == 1 else keep_time
        s0 = jnp.zeros((B, RET_HEADS, RET_DIM, RET_DIM), F32)
        oc, s_ctx = retention_chunk_scan(fl(cx[0]), fl(cx[1]), fl(cx[2]), log_gamma[d], s0)
        ol, _ = retention_chunk_scan(fl(lat[0]), fl(lat[1]), fl(lat[2]), log_gamma[d], s_ctx)
        o_lat.append(fl(ol))
        o_ctx.append(fl(oc))
    out_lat = head_groupnorm_gate(o_lat[0] + o_lat[1], lat[3], gn_w)
    out_ctx = head_groupnorm_gate(o_ctx[0] + o_ctx[1], cx[3], gn_w) if need_ctx else None
    return out_lat, out_ctx


def hyena_filters(L, w1, b1, w2, b2, w3):
    pos = jnp.arange(L, dtype=F32)
    t = pos / max(L - 1, 1)
    bands = jnp.linspace(1e-4, HY_BANDS - 1, HY_BANDS, dtype=F32)
    phase = (2.0 * math.pi / L) * pos[:, None] * bands[None, :]
    z = jnp.concatenate([t[:, None], jnp.cos(phase), -jnp.sin(phase)], axis=-1)
    h = jnp.sin(HY_SIN_FREQ * (z @ w1.astype(F32) + b1.astype(F32)))
    h = jnp.sin(HY_SIN_FREQ * (h @ w2.astype(F32) + b2.astype(F32)))
    h = (h @ w3.astype(F32)).reshape(L, HY_ORDER, 2, HY_CH)
    rates = jnp.abs(jnp.linspace(math.log(HY_TARGET) / HY_DECAY_LONG, math.log(HY_TARGET) / HY_DECAY_SHORT,
                                 HY_CH, dtype=F32))
    h = h * jnp.exp(-t[:, None] * rates[None, :])[:, None, None, :]
    h = h / (jnp.sum(jnp.abs(h), axis=(0, 2), keepdims=True) + EPS)
    return h.transpose(1, 2, 0, 3)


def bidir_long_conv(u, h_fwd, h_bwd, skip):
    B, L, C = u.shape
    taps = jnp.concatenate([h_fwd, jnp.zeros((1, C), F32), h_bwd[:0:-1]], axis=0)
    y = jnp.fft.irfft(jnp.fft.rfft(u, n=2 * L, axis=1) * jnp.fft.rfft(taps, axis=0)[None],
                      n=2 * L, axis=1)[:, :L]
    return y + u * skip


def hyena(p, conv_w, w1, b1, w2, b2, w3, skip):
    L = p.shape[1]
    filt = hyena_filters(L, w1, b1, w2, b2, w3)
    u = short_conv(p, conv_w).astype(F32)
    parts = jnp.split(u, HY_ORDER + 1, axis=-1)
    z = parts[0]
    for n in range(HY_ORDER):
        z = parts[n + 1] * bidir_long_conv(z, filt[n, 0], filt[n, 1], skip[n].astype(F32))
    return z


def cd_mixer(p_lat, p_ctx, ret_decay_logit, ret_gn_w, hy_conv_w, hy_f_w1, hy_f_b1, hy_f_w2, hy_f_b2,
             hy_f_w3, hy_bias, need_ctx):
    c_lat, c_ctx = retention(p_lat[..., :RET_COLS], p_ctx[..., :RET_COLS], ret_decay_logit, ret_gn_w, need_ctx)
    hy = lambda p: hyena(p, hy_conv_w, hy_f_w1, hy_f_b1, hy_f_w2, hy_f_b2, hy_f_w3, hy_bias)
    o_lat = jnp.concatenate([c_lat, hy(p_lat[..., RET_COLS:])], axis=-1)
    o_ctx = jnp.concatenate([c_ctx, hy(p_ctx[..., RET_COLS:])], axis=-1) if need_ctx else None
    return o_lat, o_ctx


def _pad_cols(w, mult):
    n = w.shape[-1]
    pad = (-n) % mult
    return jnp.pad(w, ((0, 0), (0, pad))) if pad else w


def kernel(x, c, ctx, c_ctx, mod_w, mod_b, norm_mix_w, norm_ffn_w, ffn_w_in, ffn_w_out,
           ab_w_in, ab_w_out, dn_conv_w, dn_a_log, dn_dt_bias, dn_norm_w, swa_q_norm_w, swa_k_norm_w,
           swa_sink, cd_w_in, cd_w_out, ret_decay_logit, ret_gn_w, hy_conv_w, hy_f_w1, hy_f_b1,
           hy_f_w2, hy_f_b2, hy_f_w3, hy_bias):
    B, L, D = x.shape
    n_ctx = ctx.shape[1]
    rows = L // GRID_W
    ang_row, ang_col = axial_angles(rows)
    c_rows = jnp.zeros((8, D), F32).at[:B].set(c).at[B].set(c_ctx)
    mod_all = modulation(c_rows, mod_w, mod_b)
    hid = ffn_w_out.shape[1]
    h_ctx = ctx
    for layer in range(DEPTH):
        need_ctx = layer != DEPTH - 1
        i = layer // 2
        m = mod_all[layer].reshape(8, 6, D)
        mod = [m[:B, j][:, None, :] for j in range(6)]
        mod_c = [jnp.broadcast_to(m[B, j][None, None, :], (B, 1, D)) for j in range(6)]
        if layer % 2 == 0:
            w = ab_w_in[i]
            w_in = jnp.concatenate([w[:, :4 * DN_WIDTH], w[:, DN_COLS:], w[:, 4 * DN_WIDTH:DN_COLS]], axis=1)
            w_out, cols = ab_w_out[i], ab_w_in.shape[-1]
        else:
            w = cd_w_in[i]
            w_in = jnp.concatenate([w[:, RET_COLS:], w[:, :RET_COLS]], axis=1)
            w_out, cols = cd_w_out[i], cd_w_in.shape[-1]
        w_in_p = _pad_cols(w_in, 128).astype(BF16)
        p_lat = in_projection(x, norm_mix_w[layer], mod[0], mod[1], w_in_p, 512)
        p_ctx = in_projection(h_ctx, norm_mix_w[layer], mod_c[0], mod_c[1], w_in_p, n_ctx)
        if layer % 2 == 0:
            swa0 = 4 * DN_WIDTH
            ab0 = swa0 + (SWA_HEADS + 2 * SWA_KV_HEADS) * SWA_DIM
            a_lat, a_ctx = deltanet_pallas(p_lat, p_ctx, dn_conv_w[i], dn_a_log[i], dn_dt_bias[i],
                                           dn_norm_w[i], need_ctx, ab_col=ab0, gate_col=3 * DN_WIDTH)
            b_lat, b_ctx = window_gqa_pallas(p_lat, p_ctx, swa0, axial_rope_tables(rows), swa_q_norm_w[i],
                                             swa_k_norm_w[i], swa_sink[i], need_ctx)
        else:
            a_lat, a_ctx = retention_pallas(p_lat, p_ctx, ret_decay_logit[i], ret_gn_w[i], need_ctx,
                                            col0=(HY_ORDER + 1) * HY_CH)
            assert not need_ctx, "the last layer's context outputs reach no latent token"
            b_lat = hyena_pallas(p_lat, hy_conv_w[i], hy_f_w1[i], hy_f_b1[i], hy_f_w2[i], hy_f_b2[i],
                                 hy_f_w3[i], hy_bias[i], col0=0)
            b_ctx = None
        wo = w_out.astype(BF16)
        wg = ffn_w_in[layer][:, :hid].astype(BF16)
        wu = ffn_w_in[layer][:, hid:].astype(BF16)
        wd = ffn_w_out[layer].astype(BF16)
        x = out_projection_ffn(x, a_lat, b_lat, wo, mod[2], norm_ffn_w[layer], mod[3], mod[4], mod[5],
                               wg, wu, wd, 512, 256)
        if need_ctx:
            h_ctx = out_projection_ffn(h_ctx, a_ctx, b_ctx, wo, mod_c[2], norm_ffn_w[layer], mod_c[3], mod_c[4],
                                       mod_c[5], wg, wu, wd, n_ctx, 256)
    return x
```

```python
import functools
import math

import jax
import jax.numpy as jnp
import ml_dtypes
import numpy as np
from jax import lax
from jax.experimental import pallas as pl
from jax.experimental.pallas import tpu as pltpu

F32 = jnp.float32
BF16 = jnp.bfloat16
EPS = 1e-6

D_MODEL = 1024
DEPTH = 2
GRID_W = 64

DN_HEADS = 4
DN_DIM = 128
DN_WIDTH = DN_HEADS * DN_DIM
DN_CHUNK = 64
DN_COLS = 4 * DN_WIDTH + 4 * DN_HEADS
SWA_HEADS = 4
SWA_KV_HEADS = 2
SWA_DIM = 128
SWA_WINDOW = 128
SWA_BLOCK = 128
ROPE_THETA = 10000.0
RET_HEADS = 4
RET_DIM = 128
RET_WIDTH = RET_HEADS * RET_DIM
RET_CHUNK = 128
RET_COLS = 4 * RET_WIDTH
HY_CH = 512
HY_ORDER = 2
HY_BANDS = 8
HY_SIN_FREQ = 1.0
HY_TARGET = 1e-2
HY_DECAY_SHORT = 0.3
HY_DECAY_LONG = 1.5

VMEM_LIMIT_BYTES = 56 * 1024 * 1024


def _cparams(sem):
    return pltpu.CompilerParams(dimension_semantics=sem, vmem_limit_bytes=VMEM_LIMIT_BYTES)


def _mod_body(c_ref, w_ref, b_ref, o_ref):
    a = c_ref[...]
    a = a * jax.nn.sigmoid(a)
    o_ref[...] = jnp.dot(a, w_ref[...], preferred_element_type=F32,
                         precision=lax.Precision.HIGHEST) + b_ref[...]


def modulation(c_rows, mod_w, mod_b):
    depth, d, n = mod_w.shape
    tn = 1536
    return pl.pallas_call(
        _mod_body,
        grid=(depth, n // tn),
        in_specs=[pl.BlockSpec((8, d), lambda l, j: (0, 0)),
                  pl.BlockSpec((None, d, tn), lambda l, j: (l, 0, j)),
                  pl.BlockSpec((None, 1, tn), lambda l, j: (l, 0, j))],
        out_specs=pl.BlockSpec((None, 8, tn), lambda l, j: (l, 0, j)),
        out_shape=jax.ShapeDtypeStruct((depth, 8, n), F32),
        compiler_params=_cparams(("parallel", "parallel")),
        name="modulation",
    )(c_rows, mod_w, mod_b.reshape(depth, 1, n))


def _norm_mod(x, nw, shift, scale):
    y = x * lax.rsqrt(jnp.mean(x * x, axis=-1, keepdims=True) + EPS)
    return (y * nw) * (1.0 + scale) + shift


def _resident(shape):
    return pl.BlockSpec(shape, lambda *_: (0,) * len(shape), pipeline_mode=pl.Buffered(1))


def _inproj_body(x_ref, nw_ref, shift_ref, scale_ref, w_ref, o_ref):
    h = _norm_mod(x_ref[...], nw_ref[...], shift_ref[...], scale_ref[...]).astype(BF16)
    o_ref[...] = jnp.dot(h, w_ref[...], preferred_element_type=F32)


def in_projection(x, nw, shift, scale, w, tm):
    b, l, d = x.shape
    n = w.shape[1]
    return pl.pallas_call(
        _inproj_body,
        grid=(b, l // tm),
        in_specs=[pl.BlockSpec((None, tm, d), lambda bi, i: (bi, i, 0)),
                  _resident((1, d)),
                  pl.BlockSpec((None, 1, d), lambda bi, i: (bi, 0, 0)),
                  pl.BlockSpec((None, 1, d), lambda bi, i: (bi, 0, 0)),
                  _resident((d, n))],
        out_specs=pl.BlockSpec((None, tm, n), lambda bi, i: (bi, i, 0)),
        out_shape=jax.ShapeDtypeStruct((b, l, n), F32),
        compiler_params=_cparams(("parallel", "parallel")),
        name="in_projection",
    )(x, nw.reshape(1, d), shift, scale, w)


def _outffn_body(x_ref, oa_ref, ob_ref, wo_ref, g2_ref, nw_ref, sh_ref, sc_ref, g5_ref,
                 wg_ref, wu_ref, wd_ref, y_ref, x1_scr, h_scr, acc_scr, *, th):
    wa = oa_ref.shape[1]
    mix = (jnp.dot(oa_ref[...].astype(BF16), wo_ref[:wa, :], preferred_element_type=F32)
           + jnp.dot(ob_ref[...].astype(BF16), wo_ref[wa:, :], preferred_element_type=F32))
    x1 = x_ref[...] + g2_ref[...] * mix
    x1_scr[...] = x1
    h_scr[...] = _norm_mod(x1, nw_ref[...], sh_ref[...], sc_ref[...]).astype(BF16)
    for k in range(wg_ref.shape[1] // th):
        ks = slice(k * th, (k + 1) * th)
        g = jnp.dot(h_scr[...], wg_ref[:, ks], preferred_element_type=F32)
        u = jnp.dot(h_scr[...], wu_ref[:, ks], preferred_element_type=F32)
        a = (g * jax.nn.sigmoid(g) * u).astype(BF16)
        part = jnp.dot(a, wd_ref[ks, :], preferred_element_type=F32)
        if k == 0:
            acc_scr[...] = part
        else:
            acc_scr[...] += part
    y_ref[...] = x1_scr[...] + g5_ref[...] * acc_scr[...]


def out_projection_ffn(x, oa, ob, wo, g2, nw, shift, scale, g5, wg, wu, wd, tm, th):
    b, l, d = x.shape
    wa, wb = oa.shape[-1], ob.shape[-1]
    hid = wg.shape[1]
    tok = lambda bi, i: (bi, i, 0)
    vec = pl.BlockSpec((None, 1, d), lambda bi, i: (bi, 0, 0))
    return pl.pallas_call(
        functools.partial(_outffn_body, th=th),
        grid=(b, l // tm),
        in_specs=[pl.BlockSpec((None, tm, d), tok),
                  pl.BlockSpec((None, tm, wa), tok),
                  pl.BlockSpec((None, tm, wb), tok),
                  _resident((d, d)), vec, _resident((1, d)), vec, vec, vec,
                  _resident((d, hid)), _resident((d, hid)), _resident((hid, d))],
        out_specs=pl.BlockSpec((None, tm, d), tok),
        out_shape=jax.ShapeDtypeStruct((b, l, d), F32),
        scratch_shapes=[pltpu.VMEM((tm, d), F32), pltpu.VMEM((tm, d), BF16), pltpu.VMEM((tm, d), F32)],
        compiler_params=_cparams(("parallel", "parallel")),
        name="out_projection_ffn",
    )(x, oa, ob, wo, g2, nw.reshape(1, d), shift, scale, g5, wg, wu, wd)


NEG_BIG = -1e30


def _head_rms(x, w):
    return x * lax.rsqrt(jnp.mean(x * x, axis=-1, keepdims=True) + EPS) * w


def _swa_prep_body(*refs, use_rope):
    if use_rope:
        q_ref, k_ref, v_ref, qw_ref, kw_ref, cos_ref, sa_ref, sb_ref, qo_ref, ko_ref, vo_ref = refs
    else:
        q_ref, k_ref, v_ref, qw_ref, kw_ref, qo_ref, ko_ref, vo_ref = refs

    def prep(x, w):
        y = _head_rms(x, w)
        if use_rope:
            y = (y * cos_ref[...] + pltpu.roll(y, SWA_DIM - 32, 1) * sa_ref[...]
                 + pltpu.roll(y, 32, 1) * sb_ref[...])
        return y.astype(BF16)

    for h in range(SWA_HEADS):
        sl = slice(h * SWA_DIM, (h + 1) * SWA_DIM)
        qo_ref[:, sl] = prep(q_ref[:, sl], qw_ref[...])
    for h in range(SWA_KV_HEADS):
        sl = slice(h * SWA_DIM, (h + 1) * SWA_DIM)
        ko_ref[:, sl] = prep(k_ref[:, sl], kw_ref[...])
    vo_ref[...] = v_ref[...].astype(BF16)


def swa_prep(p, col0, qw, kw, rope, tl):
    b, l, _ = p.shape
    nq, nk = SWA_HEADS * SWA_DIM, SWA_KV_HEADS * SWA_DIM
    tok = lambda bi, i: (bi, i, 0)
    in_specs = [pl.BlockSpec((None, tl, nq), lambda bi, i: (bi, i, col0 // nq)),
                pl.BlockSpec((None, tl, nk), lambda bi, i: (bi, i, (col0 + nq) // nk)),
                pl.BlockSpec((None, tl, nk), lambda bi, i: (bi, i, (col0 + nq + nk) // nk)),
                pl.BlockSpec((1, SWA_DIM), lambda bi, i: (0, 0)),
                pl.BlockSpec((1, SWA_DIM), lambda bi, i: (0, 0))]
    args = [p, p, p, qw.reshape(1, SWA_DIM), kw.reshape(1, SWA_DIM)]
    if rope is not None:
        in_specs += [pl.BlockSpec((tl, SWA_DIM), lambda bi, i: (i, 0))] * 3
        args += list(rope)
    return pl.pallas_call(
        functools.partial(_swa_prep_body, use_rope=rope is not None),
        grid=(b, l // tl),
        in_specs=in_specs,
        out_specs=[pl.BlockSpec((None, tl, nq), tok), pl.BlockSpec((None, tl, nk), tok),
                   pl.BlockSpec((None, tl, nk), tok)],
        out_shape=[jax.ShapeDtypeStruct((b, l, nq), BF16), jax.ShapeDtypeStruct((b, l, nk), BF16),
                   jax.ShapeDtypeStruct((b, l, nk), BF16)],
        compiler_params=_cparams(("parallel", "parallel")),
        name="swa_prep",
    )(*args)


def _nt_dot(a, b):
    return lax.dot_general(a, b, (((1,), (1,)), ((), ())), preferred_element_type=F32)


def _swa_attend(q, keys, vals, masks, sink_col):
    scale = SWA_DIM ** -0.5
    scores = []
    for kk, mask in zip(keys, masks):
        s = _nt_dot(q, kk) * scale
        if mask is not None:
            s = jnp.where(mask, s, NEG_BIG)
        scores.append(s)
    m = sink_col
    for s in scores:
        m = jnp.maximum(m, jnp.max(s, axis=-1, keepdims=True))
    den = jnp.exp(sink_col - m)
    acc = None
    for s, vv in zip(scores, vals):
        pr = jnp.exp(s - m)
        den = den + jnp.sum(pr, axis=-1, keepdims=True)
        o = jnp.dot(pr.astype(BF16), vv, preferred_element_type=F32)
        acc = o if acc is None else acc + o
    return acc / den


def _swa_attn_body(sink_ref, q_ref, kp_ref, km_ref, kn_ref, vp_ref, vm_ref, vn_ref, kc_ref, vc_ref,
                   o_ref, *, nb):
    i = pl.program_id(1)
    n_blocks = pl.num_programs(1) * nb
    w, d = SWA_BLOCK, SWA_DIM
    grp = SWA_HEADS // SWA_KV_HEADS
    rows = lax.broadcasted_iota(jnp.int32, (grp * w, w), 0)
    qi = rows % w
    kj = lax.broadcasted_iota(jnp.int32, (grp * w, w), 1)
    for r in range(nb):
        blk = i * nb + r
        mask_prev = (kj >= qi) & (blk > 0)
        mask_next = (kj <= qi) & (blk < n_blocks - 1)
        rs = slice(r * w, (r + 1) * w)
        for h in range(SWA_KV_HEADS):
            hs = slice(h * d, (h + 1) * d)
            q = jnp.concatenate([q_ref[rs, (h * grp + g) * d:(h * grp + g + 1) * d] for g in range(grp)], axis=0)
            sink_col = jnp.zeros((grp * w, 1), F32)
            for g in range(grp):
                sink_col = jnp.where((rows[:, :1] // w) == g, sink_ref[h * grp + g], sink_col)
            if r > 0:
                k_prev, v_prev = km_ref[(r - 1) * w:r * w, hs], vm_ref[(r - 1) * w:r * w, hs]
            else:
                k_prev, v_prev = kp_ref[:, hs], vp_ref[:, hs]
            if r < nb - 1:
                k_next, v_next = km_ref[(r + 1) * w:(r + 2) * w, hs], vm_ref[(r + 1) * w:(r + 2) * w, hs]
            else:
                k_next, v_next = kn_ref[:, hs], vn_ref[:, hs]
            o = _swa_attend(q, [k_prev, km_ref[rs, hs], k_next, kc_ref[:, hs]],
                            [v_prev, vm_ref[rs, hs], v_next, vc_ref[:, hs]],
                            [mask_prev, None, mask_next, None], sink_col)
            for g in range(grp):
                o_ref[rs, (h * grp + g) * d:(h * grp + g + 1) * d] = o[g * w:(g + 1) * w].astype(o_ref.dtype)


def swa_attention(q, k, v, kc, vc, sink, nb):
    b, l, nq = q.shape
    nk = k.shape[-1]
    n_ctx = kc.shape[1]
    w = SWA_BLOCK
    tq = nb * w
    last = l // w - 1
    main = lambda bi, i, s: (bi, i, 0)
    prev = lambda bi, i, s: (bi, jnp.maximum(i * nb - 1, 0), 0)
    nxt = lambda bi, i, s: (bi, jnp.minimum((i + 1) * nb, last), 0)
    cx = lambda bi, i, s: (bi, 0, 0)
    return pl.pallas_call(
        functools.partial(_swa_attn_body, nb=nb),
        grid_spec=pltpu.PrefetchScalarGridSpec(
            num_scalar_prefetch=1,
            grid=(b, l // tq),
            in_specs=[pl.BlockSpec((None, tq, nq), main),
                      pl.BlockSpec((None, w, nk), prev), pl.BlockSpec((None, tq, nk), main),
                      pl.BlockSpec((None, w, nk), nxt),
                      pl.BlockSpec((None, w, nk), prev), pl.BlockSpec((None, tq, nk), main),
                      pl.BlockSpec((None, w, nk), nxt),
                      pl.BlockSpec((None, n_ctx, nk), cx), pl.BlockSpec((None, n_ctx, nk), cx)],
            out_specs=pl.BlockSpec((None, tq, nq), main)),
        out_shape=jax.ShapeDtypeStruct((b, l, nq), BF16),
        compiler_params=_cparams(("parallel", "parallel")),
        name="swa_attention",
    )(sink, q, k, k, k, v, v, v, kc, vc)


def _ctx_attn_body(sink_ref, q_ref, kc_ref, vc_ref, o_ref):
    n_ctx, d = q_ref.shape[0], SWA_DIM
    grp = SWA_HEADS // SWA_KV_HEADS
    rows = lax.broadcasted_iota(jnp.int32, (grp * n_ctx, 1), 0)
    for h in range(SWA_KV_HEADS):
        hs = slice(h * d, (h + 1) * d)
        q = jnp.concatenate([q_ref[:, (h * grp + g) * d:(h * grp + g + 1) * d] for g in range(grp)], axis=0)
        sink_col = jnp.zeros((grp * n_ctx, 1), F32)
        for g in range(grp):
            sink_col = jnp.where((rows // n_ctx) == g, sink_ref[h * grp + g], sink_col)
        o = _swa_attend(q, [kc_ref[:, hs]], [vc_ref[:, hs]], [None], sink_col)
        for g in range(grp):
            o_ref[:, (h * grp + g) * d:(h * grp + g + 1) * d] = o[g * n_ctx:(g + 1) * n_ctx].astype(o_ref.dtype)


def ctx_attention(qc, kc, vc, sink):
    b, n_ctx, nq = qc.shape
    nk = kc.shape[-1]
    cx = lambda bi, s: (bi, 0, 0)
    return pl.pallas_call(
        _ctx_attn_body,
        grid_spec=pltpu.PrefetchScalarGridSpec(
            num_scalar_prefetch=1, grid=(b,),
            in_specs=[pl.BlockSpec((None, n_ctx, nq), cx), pl.BlockSpec((None, n_ctx, nk), cx),
                      pl.BlockSpec((None, n_ctx, nk), cx)],
            out_specs=pl.BlockSpec((None, n_ctx, nq), cx)),
        out_shape=jax.ShapeDtypeStruct((b, n_ctx, nq), BF16),
        compiler_params=_cparams(("parallel",)),
        name="ctx_attention",
    )(sink, qc, kc, vc)


def axial_rope_tables(rows):
    ang_row, ang_col = axial_angles(rows)
    zero = jnp.zeros_like(ang_row)
    cos = jnp.concatenate([jnp.cos(ang_row)] * 2 + [jnp.cos(ang_col)] * 2, axis=-1)
    sa = jnp.concatenate([-jnp.sin(ang_row), zero, -jnp.sin(ang_col), zero], axis=-1)
    sb = jnp.concatenate([zero, jnp.sin(ang_row), zero, jnp.sin(ang_col)], axis=-1)
    return cos, sa, sb


def window_gqa_pallas(p_lat, p_ctx, col0, rope, q_norm_w, k_norm_w, sink, need_ctx):
    ql, kl, vl = swa_prep(p_lat, col0, q_norm_w, k_norm_w, rope, 1024)
    qc, kc, vc = swa_prep(p_ctx, col0, q_norm_w, k_norm_w, None, p_ctx.shape[1])
    o_lat = swa_attention(ql, kl, vl, kc, vc, sink, 4)
    o_ctx = ctx_attention(qc, kc, vc, sink) if need_ctx else None
    return o_lat, o_ctx


def _tn_dot(a, b):
    return lax.dot_general(a, b, (((0,), (0,)), ((), ())), preferred_element_type=F32)


def _log_sigmoid(x):
    return jnp.minimum(x, 0.0) - jnp.log(1.0 + jnp.exp(-jnp.abs(x)))


RET_STREAMS = 2 * RET_HEADS


def _ret_scan_body(logit_ref, qf_ref, kf_ref, vf_ref, cosf_ref, sinf_ref, qb_ref, kb_ref, vb_ref, cosb_ref,
                   sinb_ref, s0_ref, of_ref, ob_ref, sf_ref, s_scr):
    n = pl.program_id(1)
    c, dh = RET_CHUNK, RET_DIM

    @pl.when(n == 0)
    def _():
        s_scr[...] = s0_ref[...]

    ii = lax.broadcasted_iota(jnp.int32, (c, c), 0)
    jj = lax.broadcasted_iota(jnp.int32, (c, c), 1)
    i1 = lax.broadcasted_iota(jnp.int32, (c, 1), 0)
    rel = {True: (ii - jj).astype(F32), False: (jj - ii).astype(F32)}
    pos = {True: i1.astype(F32), False: (c - 1 - i1).astype(F32)}
    streams = [(d == 0, h) for d in range(2) for h in range(RET_HEADS)]
    srcs = {True: (qf_ref, kf_ref, vf_ref, cosf_ref, sinf_ref), False: (qb_ref, kb_ref, vb_ref, cosb_ref, sinb_ref)}

    def rot(x, fwd):
        return x * srcs[fwd][3][...] + pltpu.roll(x, dh // 2, 1) * srcs[fwd][4][...]

    hs = lambda h: slice(h * dh, (h + 1) * dh)
    lg = [_log_sigmoid(jnp.full((1, 1), logit_ref[r], F32)) for r in range(RET_STREAMS)]
    q = [rot(srcs[f][0][:, hs(h)], f) for f, h in streams]
    k = [rot(srcs[f][1][:, hs(h)], f) * dh ** -0.5 for f, h in streams]
    v = [srcs[f][2][:, hs(h)].astype(BF16) for f, h in streams]
    dmat = [jnp.where(rel[f] >= 0, jnp.exp(lg[r] * jnp.maximum(rel[f], 0.0)), 0.0)
            for r, (f, _) in enumerate(streams)]
    scores = [_nt_dot(q[r].astype(BF16), k[r].astype(BF16)) * dmat[r] for r in range(RET_STREAMS)]
    q_dec = [(q[r] * jnp.exp(lg[r] * (pos[f] + 1.0))).astype(BF16) for r, (f, _) in enumerate(streams)]
    k_dec = [(k[r] * jnp.exp(lg[r] * (c - 1.0 - pos[f]))).astype(BF16) for r, (f, _) in enumerate(streams)]
    s = [s_scr[r] for r in range(RET_STREAMS)]
    o = [jnp.dot(scores[r].astype(BF16), v[r], preferred_element_type=F32)
         + jnp.dot(q_dec[r], s[r].astype(BF16), preferred_element_type=F32) for r in range(RET_STREAMS)]
    s_new = [s[r] * jnp.exp(lg[r] * c) + _tn_dot(k_dec[r], v[r]) for r in range(RET_STREAMS)]
    for r, (f, h) in enumerate(streams):
        (of_ref if f else ob_ref)[:, hs(h)] = o[r]
        s_scr[r] = s_new[r]

    @pl.when(n == pl.num_programs(1) - 1)
    def _():
        sf_ref[...] = s_scr[...]


def retention_scan(p, col0, decay_logit, cos, sin, s0):
    b, l, _ = p.shape
    c, dh, w = RET_CHUNK, RET_DIM, RET_WIDTH
    n = l // c
    cb = col0 // w
    fcol = lambda off: (lambda bi, ni, s: (bi, ni, cb + off))
    bcol = lambda off: (lambda bi, ni, s: (bi, n - 1 - ni, cb + off))
    ftab = lambda bi, ni, s: (ni, 0)
    btab = lambda bi, ni, s: (n - 1 - ni, 0)
    st = lambda bi, ni, s: (bi, 0, 0, 0)
    tok = lambda col: [pl.BlockSpec((None, c, w), col(j)) for j in range(3)]
    return pl.pallas_call(
        _ret_scan_body,
        grid_spec=pltpu.PrefetchScalarGridSpec(
            num_scalar_prefetch=1,
            grid=(b, n),
            in_specs=tok(fcol) + [pl.BlockSpec((c, dh), ftab)] * 2 + tok(bcol) + [pl.BlockSpec((c, dh), btab)] * 2
            + [pl.BlockSpec((None, RET_STREAMS, dh, dh), st)],
            out_specs=[pl.BlockSpec((None, c, w), lambda bi, ni, s: (bi, ni, 0)),
                       pl.BlockSpec((None, c, w), lambda bi, ni, s: (bi, n - 1 - ni, 0)),
                       pl.BlockSpec((None, RET_STREAMS, dh, dh), st)],
            scratch_shapes=[pltpu.VMEM((RET_STREAMS, dh, dh), F32)]),
        out_shape=[jax.ShapeDtypeStruct((b, l, w), F32), jax.ShapeDtypeStruct((b, l, w), F32),
                   jax.ShapeDtypeStruct((b, RET_STREAMS, dh, dh), F32)],
        compiler_params=_cparams(("parallel", "arbitrary")),
        name="retention_scan",
    )(decay_logit.reshape(-1), p, p, p, cos, sin, p, p, p, cos, sin, s0)


def _ret_final_body(of_ref, ob_ref, g_ref, w_ref, y_ref):
    for h in range(RET_HEADS):
        sl = slice(h * RET_DIM, (h + 1) * RET_DIM)
        o = of_ref[:, sl] + ob_ref[:, sl]
        mu = jnp.mean(o, axis=-1, keepdims=True)
        var = jnp.mean(jnp.square(o - mu), axis=-1, keepdims=True)
        y = (o - mu) * lax.rsqrt(var + EPS) * w_ref[:, sl]
        g = g_ref[:, sl]
        y_ref[:, sl] = (y * (g * jax.nn.sigmoid(g))).astype(y_ref.dtype)


def retention_finalize(o_f, o_b, p, gate_col, gn_w, tl):
    b, l, w = o_f.shape
    tok = lambda bi, i: (bi, i, 0)
    return pl.pallas_call(
        _ret_final_body,
        grid=(b, l // tl),
        in_specs=[pl.BlockSpec((None, tl, w), tok),
                  pl.BlockSpec((None, tl, w), tok),
                  pl.BlockSpec((None, tl, w), lambda bi, i: (bi, i, gate_col // w)),
                  pl.BlockSpec((1, w), lambda bi, i: (0, 0))],
        out_specs=pl.BlockSpec((None, tl, w), tok),
        out_shape=jax.ShapeDtypeStruct((b, l, w), BF16),
        compiler_params=_cparams(("parallel", "parallel")),
        name="retention_finalize",
    )(o_f, o_b, p, gn_w.reshape(1, w))


def retention_rope_tables(l):
    inv = ROPE_THETA ** (-jnp.linspace(0.0, 1.0, RET_DIM // 2, dtype=F32))
    ang = jnp.arange(l, dtype=F32)[:, None] * inv
    return (jnp.concatenate([jnp.cos(ang), jnp.cos(ang)], axis=-1),
            jnp.concatenate([-jnp.sin(ang), jnp.sin(ang)], axis=-1))


def retention_pallas(p_lat, p_ctx, decay_logit, gn_w, need_ctx, col0=0):
    b, l, _ = p_lat.shape
    n_ctx = p_ctx.shape[1]
    gate_col = col0 + 3 * RET_WIDTH
    s0 = jnp.zeros((b, RET_STREAMS, RET_DIM, RET_DIM), F32)
    oc_f, oc_b, s_ctx = retention_scan(p_ctx, col0, decay_logit, *retention_rope_tables(n_ctx), s0)
    ol_f, ol_b, _ = retention_scan(p_lat, col0, decay_logit, *retention_rope_tables(l), s_ctx)
    out_lat = retention_finalize(ol_f, ol_b, p_lat, gate_col, gn_w, 1024)
    out_ctx = retention_finalize(oc_f, oc_b, p_ctx, gate_col, gn_w, n_ctx) if need_ctx else None
    return out_lat, out_ctx


DN_BLOCK = 128
DN_STREAMS = 2 * DN_HEADS
DN_SUBS = 2


def _shift_rows(x, prev_row, next_row):
    n = x.shape[0]
    r = lax.broadcasted_iota(jnp.int32, (n, 1), 0)
    x_prev = jnp.where(r == 0, prev_row, pltpu.roll(x, 1, 0))
    x_next = jnp.where(r == n - 1, next_row, pltpu.roll(x, n - 1, 0))
    return x_prev, x_next


def _conv3(x_ref, xp_ref, xn_ref, w_ref):
    i = pl.program_id(1)
    prev_row = jnp.where(i > 0, xp_ref[7:8, :], 0.0)
    next_row = jnp.where(i < pl.num_programs(1) - 1, xn_ref[0:1, :], 0.0)
    x = x_ref[...]
    x_prev, x_next = _shift_rows(x, prev_row, next_row)
    return x_prev * w_ref[0:1, :] + x * w_ref[1:2, :] + x_next * w_ref[2:3, :]


def _dn_prep_body(x_ref, xp_ref, xn_ref, ab_ref, cw_ref, alog_ref, dtb_ref,
                  q_ref, k_ref, v_ref, gb_ref, gam_ref, gamt_ref):
    y = _conv3(x_ref, xp_ref, xn_ref, cw_ref)
    y = y * jax.nn.sigmoid(y)
    for h in range(DN_HEADS):
        for part, (ref, mul) in enumerate(((q_ref, DN_DIM ** -0.5), (k_ref, 1.0))):
            sl = slice(part * DN_WIDTH + h * DN_DIM, part * DN_WIDTH + (h + 1) * DN_DIM)
            t = y[:, sl]
            t = t * lax.rsqrt(jnp.sum(t * t, axis=-1, keepdims=True) + EPS)
            ref[:, h * DN_DIM:(h + 1) * DN_DIM] = t * mul if mul != 1.0 else t
    v_ref[...] = y[:, 2 * DN_WIDTH:]

    ab = ab_ref[...]
    z = ab + dtb_ref[...]
    softplus = jnp.maximum(z, 0.0) + jnp.log(1.0 + jnp.exp(-jnp.abs(z)))
    g = -jnp.exp(alog_ref[...]) * softplus
    lane = lax.broadcasted_iota(jnp.int32, ab.shape, 1)
    gb_ref[...] = jnp.where(lane < DN_STREAMS, g, jax.nn.sigmoid(ab))

    c = DN_BLOCK
    ii = lax.broadcasted_iota(jnp.int32, (c, c), 0)
    jj = lax.broadcasted_iota(jnp.int32, (c, c), 1)
    tri_f = (ii >= jj).astype(F32)
    tri_b = (ii <= jj).astype(F32)
    lane_c = lax.broadcasted_iota(jnp.int32, (c, ab.shape[1]), 1)
    for n in range(x_ref.shape[0] // c):
        gc = g[n * c:(n + 1) * c, :]
        cf = jnp.dot(tri_f, gc, preferred_element_type=F32, precision=lax.Precision.HIGHEST)
        cb = jnp.dot(tri_b, gc, preferred_element_type=F32, precision=lax.Precision.HIGHEST)
        gam = jnp.where(lane_c < DN_HEADS, cf, cb)
        gam_ref[n * c:(n + 1) * c, :] = gam
        gamt_ref[n] = gam.T[:DN_STREAMS, :]


def dn_prep(p, ab_col, conv_w, a_log, dt_bias, tl):
    b, l, _ = p.shape
    c3 = 3 * DN_WIDTH
    nblk8 = l // 8
    lanes = 128
    pad8 = lambda a: jnp.pad(a.reshape(1, DN_STREAMS).astype(F32), ((0, 0), (0, lanes - DN_STREAMS)))
    tok = lambda bi, i: (bi, i, 0)
    return pl.pallas_call(
        _dn_prep_body,
        grid=(b, l // tl),
        in_specs=[pl.BlockSpec((None, tl, c3), tok),
                  pl.BlockSpec((None, 8, c3), lambda bi, i: (bi, jnp.maximum(i * (tl // 8) - 1, 0), 0)),
                  pl.BlockSpec((None, 8, c3), lambda bi, i: (bi, jnp.minimum((i + 1) * (tl // 8), nblk8 - 1), 0)),
                  pl.BlockSpec((None, tl, lanes), lambda bi, i: (bi, i, ab_col // lanes)),
                  pl.BlockSpec((3, c3), lambda bi, i: (0, 0)),
                  pl.BlockSpec((1, lanes), lambda bi, i: (0, 0)),
                  pl.BlockSpec((1, lanes), lambda bi, i: (0, 0))],
        out_specs=[pl.BlockSpec((None, tl, DN_WIDTH), tok)] * 3
        + [pl.BlockSpec((None, tl, lanes), tok)] * 2
        + [pl.BlockSpec((None, tl // DN_BLOCK, DN_STREAMS, DN_BLOCK), lambda bi, i: (bi, i, 0, 0))],
        out_shape=[jax.ShapeDtypeStruct((b, l, DN_WIDTH), F32)] * 3
        + [jax.ShapeDtypeStruct((b, l, lanes), F32)] * 2
        + [jax.ShapeDtypeStruct((b, l // DN_BLOCK, DN_STREAMS, DN_BLOCK), F32)],
        compiler_params=_cparams(("parallel", "parallel")),
        name="dn_prep",
    )(p, p, p, p, conv_w, pad8(a_log), pad8(dt_bias))


def _dot16(a, b):
    return jnp.dot(a.astype(BF16), b.astype(BF16), preferred_element_type=F32)


def _unit_triangular_inverses(ms, ii, jj):
    c = ms[0].shape[0]
    eye = (ii == jj).astype(F32)
    pair = (ii // 2) == (jj // 2)
    ts = [eye - jnp.where(pair, m, 0.0) for m in ms]
    s = 2
    while s < c:
        sub = ((ii // (2 * s)) == (jj // (2 * s))) & ((ii // s) != (jj // s))
        t16 = [t.astype(BF16) for t in ts]
        xs = [_dot16(jnp.where(sub, m, 0.0), t) for m, t in zip(ms, t16)]
        ts = [t - _dot16(tb, x) for t, tb, x in zip(ts, t16, xs)]
        s *= 2
    return ts


def _dn_scan_body(qf_ref, kf_ref, vf_ref, gbf_ref, gamf_ref, gamtf_ref,
                  qb_ref, kb_ref, vb_ref, gbb_ref, gamb_ref, gamtb_ref, s0_ref,
                  of_ref, ob_ref, sf_ref, s_scr):
    n = pl.program_id(1)

    @pl.when(n == 0)
    def _():
        s_scr[...] = s0_ref[...]

    c, dh = DN_BLOCK, DN_DIM
    ii = lax.broadcasted_iota(jnp.int32, (c, c), 0)
    jj = lax.broadcasted_iota(jnp.int32, (c, c), 1)
    incl = {True: ii >= jj, False: ii <= jj}
    strict = {True: ii > jj, False: ii < jj}
    srcs = {True: (qf_ref, kf_ref, vf_ref, gbf_ref, gamf_ref, gamtf_ref),
            False: (qb_ref, kb_ref, vb_ref, gbb_ref, gamb_ref, gamtb_ref)}
    streams = [(d == 0, h) for d in range(2) for h in range(DN_HEADS)]
    hs = lambda h: slice(h * dh, (h + 1) * dh)
    items = [(r, f, h, j, (j if f else DN_SUBS - 1 - j)) for j in range(DN_SUBS) for r, (f, h) in enumerate(streams)]
    rows = lambda blk: slice(blk * c, (blk + 1) * c)
    rng = range(len(items))

    q = [srcs[f][0][rows(bk), hs(h)] for _, f, h, _, bk in items]
    k = [srcs[f][1][rows(bk), hs(h)] for _, f, h, _, bk in items]
    v = [srcs[f][2][rows(bk), hs(h)] for _, f, h, _, bk in items]
    beta = [srcs[f][3][rows(bk), DN_STREAMS + r:DN_STREAMS + r + 1] for r, f, _, _, bk in items]
    gcol = [srcs[f][4][rows(bk), r:r + 1] for r, f, _, _, bk in items]
    grow = [srcs[f][5][bk, r:r + 1, :] for r, f, _, _, bk in items]
    g_last = [gcol[i][c - 1:c, :] if items[i][1] else gcol[i][0:1, :] for i in rng]

    e = [jnp.exp(jnp.where(incl[items[i][1]], gcol[i] - grow[i], 0.0)) for i in rng]
    kb = [k[i] * beta[i] for i in rng]
    k16 = [k[i].astype(BF16) for i in rng]
    m = [_nt_dot(kb[i].astype(BF16), k16[i]) * jnp.where(strict[items[i][1]], e[i], 0.0) for i in rng]
    attn = [(_nt_dot(q[i].astype(BF16), k16[i]) * jnp.where(incl[items[i][1]], e[i], 0.0)).astype(BF16) for i in rng]
    t = _unit_triangular_inverses(m, ii, jj)
    eg = [jnp.exp(gcol[i]) for i in rng]
    sol = [_dot16(t[i], jnp.concatenate([v[i] * beta[i], kb[i] * eg[i]], axis=-1)) for i in rng]
    u = [sol[i][:, :dh] for i in rng]
    w16 = [sol[i][:, dh:].astype(BF16) for i in rng]
    q_dec = [(q[i] * eg[i]).astype(BF16) for i in rng]
    k_dec = [(k[i] * jnp.exp(g_last[i] - gcol[i])).astype(BF16) for i in rng]
    blk_dec = [jnp.exp(g_last[i]) for i in rng]

    s = [s_scr[r] for r in range(DN_STREAMS)]
    for j in range(DN_SUBS):
        idx = [i for i in rng if items[i][3] == j]
        s16 = [s[r].astype(BF16) for r in range(DN_STREAMS)]
        v_new = [(u[i] - jnp.dot(w16[i], s16[items[i][0]], preferred_element_type=F32)).astype(BF16) for i in idx]
        o = [jnp.dot(q_dec[i], s16[items[i][0]], preferred_element_type=F32)
             + jnp.dot(attn[i], vn, preferred_element_type=F32) for i, vn in zip(idx, v_new)]
        s = [s[items[i][0]] * blk_dec[i] + _tn_dot(k_dec[i], vn) for i, vn in zip(idx, v_new)]
        for i, oi in zip(idx, o):
            _, f, h, _, bk = items[i]
            (of_ref if f else ob_ref)[rows(bk), hs(h)] = oi
    for r in range(DN_STREAMS):
        s_scr[r] = s[r]

    @pl.when(n == pl.num_programs(1) - 1)
    def _():
        sf_ref[...] = s_scr[...]


def dn_scan(q, k, v, gb, gam, gamt, s0):
    b, l, w = q.shape
    c = DN_BLOCK * DN_SUBS
    n = l // c
    lanes = gb.shape[-1]
    f3 = lambda bi, ni: (bi, ni, 0)
    b3 = lambda bi, ni: (bi, n - 1 - ni, 0)
    f4 = lambda bi, ni: (bi, ni, 0, 0)
    b4 = lambda bi, ni: (bi, n - 1 - ni, 0, 0)
    st = lambda bi, ni: (bi, 0, 0, 0)

    def specs(m3, m4):
        return [pl.BlockSpec((None, c, w), m3)] * 3 + [pl.BlockSpec((None, c, lanes), m3)] * 2 + [
            pl.BlockSpec((None, DN_SUBS, DN_STREAMS, DN_BLOCK), m4)]

    return pl.pallas_call(
        _dn_scan_body,
        grid=(b, n),
        in_specs=specs(f3, f4) + specs(b3, b4) + [pl.BlockSpec((None, DN_STREAMS, DN_DIM, DN_DIM), st)],
        out_specs=[pl.BlockSpec((None, c, w), f3), pl.BlockSpec((None, c, w), b3),
                   pl.BlockSpec((None, DN_STREAMS, DN_DIM, DN_DIM), st)],
        out_shape=[jax.ShapeDtypeStruct((b, l, w), F32), jax.ShapeDtypeStruct((b, l, w), F32),
                   jax.ShapeDtypeStruct((b, DN_STREAMS, DN_DIM, DN_DIM), F32)],
        scratch_shapes=[pltpu.VMEM((DN_STREAMS, DN_DIM, DN_DIM), F32)],
        compiler_params=_cparams(("parallel", "arbitrary")),
        name="dn_scan",
    )(q, k, v, gb, gam, gamt, q, k, v, gb, gam, gamt, s0)


def _dn_final_body(of_ref, ob_ref, g_ref, w_ref, y_ref):
    for h in range(DN_HEADS):
        sl = slice(h * DN_DIM, (h + 1) * DN_DIM)
        o = of_ref[:, sl] + ob_ref[:, sl]
        y = _head_rms(o, w_ref[...])
        g = g_ref[:, sl]
        y_ref[:, sl] = (y * (g * jax.nn.sigmoid(g))).astype(y_ref.dtype)


def dn_finalize(o_f, o_b, p, gate_col, norm_w, tl):
    b, l, w = o_f.shape
    tok = lambda bi, i: (bi, i, 0)
    return pl.pallas_call(
        _dn_final_body,
        grid=(b, l // tl),
        in_specs=[pl.BlockSpec((None, tl, w), tok), pl.BlockSpec((None, tl, w), tok),
                  pl.BlockSpec((None, tl, w), lambda bi, i: (bi, i, gate_col // w)),
                  pl.BlockSpec((1, DN_DIM), lambda bi, i: (0, 0))],
        out_specs=pl.BlockSpec((None, tl, w), tok),
        out_shape=jax.ShapeDtypeStruct((b, l, w), BF16),
        compiler_params=_cparams(("parallel", "parallel")),
        name="dn_finalize",
    )(o_f, o_b, p, norm_w.reshape(1, DN_DIM))


def deltanet_pallas(p_lat, p_ctx, conv_w, a_log, dt_bias, norm_w, need_ctx, ab_col=3072, gate_col=1536):
    b = p_lat.shape[0]
    n_ctx = p_ctx.shape[1]
    s0 = jnp.zeros((b, DN_STREAMS, DN_DIM, DN_DIM), F32)
    fc = dn_prep(p_ctx, ab_col, conv_w, a_log, dt_bias, n_ctx)
    oc_f, oc_b, s_ctx = dn_scan(*fc, s0)
    fl = dn_prep(p_lat, ab_col, conv_w, a_log, dt_bias, 1024)
    ol_f, ol_b, _ = dn_scan(*fl, s_ctx)
    out_lat = dn_finalize(ol_f, ol_b, p_lat, gate_col, norm_w, 1024)
    out_ctx = dn_finalize(oc_f, oc_b, p_ctx, gate_col, norm_w, n_ctx) if need_ctx else None
    return out_lat, out_ctx


def _hy_factors(n):
    n2 = 256 if n >= 32768 else 128
    return n // n2, n2


def _bf16_const(m):
    return m.astype(ml_dtypes.bfloat16)


def _hy_constants(l):
    n = 2 * l
    n1, n2 = _hy_factors(n)
    a1 = 2.0 * np.pi * np.outer(np.arange(n1), np.arange(n1)) / n1
    c1, s1 = np.cos(a1), np.sin(a1)
    h = n1 // 2
    rows_fwd = np.block([[c1[:, :h], s1[:, :h]], [-s1[:, :h], c1[:, :h]]])
    rows_taps = np.concatenate([c1, -s1], axis=0)
    rows_inv = np.block([[c1[:h], -s1[:h]], [s1[:h], c1[:h]]]) / n
    a2 = 2.0 * np.pi * np.outer(np.arange(n2), np.arange(n2)) / n2
    c2, s2 = np.cos(a2), np.sin(a2)
    slab_fwd = np.block([[c2, s2], [-s2, c2]])
    slab_inv = np.block([[c2, -s2], [s2, c2]])
    th = 2.0 * np.pi * np.outer(np.arange(n1), np.arange(n2)) / n
    lanes = lambda t: jnp.broadcast_to(jnp.asarray(t, F32)[:, :, None], (n1, n2, 128))
    return dict(n1=n1, n2=n2, rows_fwd=_bf16_const(rows_fwd), rows_taps=_bf16_const(rows_taps),
                rows_inv=_bf16_const(rows_inv), slab_fwd=_bf16_const(slab_fwd), slab_inv=_bf16_const(slab_inv),
                tw_cos=lanes(np.cos(th)), tw_sin=lanes(np.sin(th)))


def _hy_prep_body(x_ref, xp_ref, xn_ref, w_ref, v_ref, x1_ref, x2_ref):
    y = _conv3(x_ref, xp_ref, xn_ref, w_ref)
    v_ref[...] = y[:, :HY_CH]
    x1_ref[...] = y[:, HY_CH:2 * HY_CH]
    x2_ref[...] = y[:, 2 * HY_CH:]


def hy_prep(p, col0, conv_w, tl):
    b, l, _ = p.shape
    c3 = 3 * HY_CH
    nblk8 = l // 8
    cb = col0 // c3
    tok = lambda bi, i: (bi, i, 0)
    return pl.pallas_call(
        _hy_prep_body,
        grid=(b, l // tl),
        in_specs=[pl.BlockSpec((None, tl, c3), lambda bi, i: (bi, i, cb)),
                  pl.BlockSpec((None, 8, c3), lambda bi, i: (bi, jnp.maximum(i * (tl // 8) - 1, 0), cb)),
                  pl.BlockSpec((None, 8, c3), lambda bi, i: (bi, jnp.minimum((i + 1) * (tl // 8), nblk8 - 1), cb)),
                  pl.BlockSpec((3, c3), lambda bi, i: (0, 0))],
        out_specs=[pl.BlockSpec((None, tl, HY_CH), tok)] * 3,
        out_shape=[jax.ShapeDtypeStruct((b, l, HY_CH), F32)] * 3,
        compiler_params=_cparams(("parallel", "parallel")),
        name="hy_prep",
    )(p, p, p, conv_w)


def _split_bf16(a):
    hi = a.astype(BF16)
    return hi, (a - hi.astype(F32)).astype(BF16)


def _hy_filter_body(t_ref, zt_ref, w1t_ref, b1_ref, w2t_ref, b2_ref, w3_ref, rates_ref, taps_ref, sum_ref, *,
                    zero_row):
    i = pl.program_id(0)
    hp = lax.Precision.HIGHEST
    h = jnp.sin(HY_SIN_FREQ * (jnp.dot(w1t_ref[...], zt_ref[...], preferred_element_type=F32, precision=hp)
                               + b1_ref[...]))
    h = jnp.sin(HY_SIN_FREQ * (jnp.dot(w2t_ref[...], h, preferred_element_type=F32, precision=hp) + b2_ref[...]))
    h_hi, h_lo = _split_bf16(h)
    w_hi, w_lo = _split_bf16(w3_ref[...])
    h = _tn_dot(h_hi, w_hi) + (_tn_dot(h_hi, w_lo) + _tn_dot(h_lo, w_hi))
    h = h * jnp.exp(-t_ref[...] * rates_ref[...])

    @pl.when(i == 0)
    def _():
        sum_ref[...] = jnp.zeros_like(sum_ref)

    sum_ref[...] += jnp.sum(jnp.abs(h), axis=0, keepdims=True)
    tr = h.shape[0]
    row = i * tr + lax.broadcasted_iota(jnp.int32, (tr, 1), 0)
    h = jnp.where(row == zero_row, 0.0, h)
    for o in range(HY_ORDER):
        taps_ref[o] = h[:, o * HY_CH:(o + 1) * HY_CH]


def hy_filter_taps(l, w1, b1, w2, b2, w3, tr=512):
    n = 2 * l
    r = np.arange(n)
    pos = np.where(r < l, r, n - r).astype(np.float64)
    pos[l] = 0.0
    t = pos / max(l - 1, 1)
    bands = np.linspace(1e-4, HY_BANDS - 1, HY_BANDS)
    phase = (2.0 * math.pi / l) * pos[:, None] * bands[None, :]
    kf = 32
    z = np.concatenate([t[:, None], np.cos(phase), -np.sin(phase)], axis=-1)
    zt = np.pad(z, ((0, 0), (0, kf - z.shape[1]))).T.astype(np.float32)
    t_col = t[:, None].astype(np.float32)
    w1t = jnp.pad(w1.astype(F32), ((0, kf - w1.shape[0]), (0, 0))).T
    hid = w1.shape[1]
    w3d = w3.astype(F32).reshape(hid, HY_ORDER, 2, HY_CH).transpose(2, 0, 1, 3).reshape(2, hid, HY_ORDER * HY_CH)
    rates = jnp.abs(jnp.linspace(math.log(HY_TARGET) / HY_DECAY_LONG, math.log(HY_TARGET) / HY_DECAY_SHORT,
                                 HY_CH, dtype=F32))
    rates = jnp.tile(rates, HY_ORDER).reshape(1, HY_ORDER * HY_CH)
    fixed = lambda i: (0, 0)
    return pl.pallas_call(
        functools.partial(_hy_filter_body, zero_row=l),
        grid=(n // tr,),
        in_specs=[pl.BlockSpec((tr, 1), lambda i: (i, 0)), pl.BlockSpec((kf, tr), lambda i: (0, i)),
                  pl.BlockSpec((hid, kf), fixed), pl.BlockSpec((hid, 1), fixed),
                  pl.BlockSpec((hid, hid), fixed), pl.BlockSpec((hid, 1), fixed),
                  pl.BlockSpec((None, hid, HY_ORDER * HY_CH), lambda i: (i // (l // tr), 0, 0)),
                  pl.BlockSpec((1, HY_ORDER * HY_CH), fixed)],
        out_specs=[pl.BlockSpec((HY_ORDER, tr, HY_CH), lambda i: (0, i, 0)),
                   pl.BlockSpec((1, HY_ORDER * HY_CH), fixed)],
        out_shape=[jax.ShapeDtypeStruct((HY_ORDER, n, HY_CH), F32),
                   jax.ShapeDtypeStruct((1, HY_ORDER * HY_CH), F32)],
        compiler_params=_cparams(("arbitrary",)),
        name="hy_filter_taps",
    )(t_col, zt, w1t, b1.astype(F32).reshape(hid, 1), w2.astype(F32).T, b2.astype(F32).reshape(hid, 1), w3d, rates)


HY_ROWS_GROUP = 8


def _hy_rows_body(*refs, mode):
    if mode == "taps":
        m_ref, x_ref, sum_ref, o_ref = refs
        scale = 1.0 / (sum_ref[...] + EPS)
    elif mode == "fwd":
        m_ref, x_ref, o_ref = refs
    else:
        m_ref, x_ref, xn_ref, z_ref, skip_ref, o_ref = refs
    for j in range(HY_ROWS_GROUP):
        x = jnp.concatenate([x_ref[0, :, j, :], x_ref[1, :, j, :]], axis=0)
        if mode == "taps":
            x = x * scale
        y = _dot16(m_ref[...], x)
        half = y.shape[0] // 2
        for bi in range(2):
            yb = y[bi * half:(bi + 1) * half]
            if mode == "inv":
                yb = xn_ref[bi, :, j, :] * (yb + z_ref[bi, :, j, :] * skip_ref[...])
            o_ref[bi, :, j, :] = yb


def hy_rows_pass(mode, mat, x, extra=()):
    n2, c = x.shape[-2:]
    g = HY_ROWS_GROUP
    blk = lambda rows: pl.BlockSpec((2, rows, g, c), lambda j: (0, 0, j, 0))
    mspec = _resident(mat.shape)
    vec = pl.BlockSpec((1, c), lambda j: (0, 0))
    if mode == "taps":
        n1 = 2 * x.shape[1]
        in_specs = [mspec, blk(n1 // 2), vec]
        out_rows = n1
    elif mode == "fwd":
        n1 = 2 * x.shape[1]
        in_specs = [mspec, blk(n1 // 2)]
        out_rows = n1
    else:
        n1 = x.shape[1]
        in_specs = [mspec, blk(n1), blk(n1 // 2), blk(n1 // 2), vec]
        out_rows = n1 // 2
    return pl.pallas_call(
        functools.partial(_hy_rows_body, mode=mode),
        grid=(n2 // g,),
        in_specs=in_specs,
        out_specs=blk(out_rows),
        out_shape=jax.ShapeDtypeStruct((2, out_rows, n2, c), F32),
        compiler_params=_cparams(("parallel",)),
        name="hy_rows_" + mode,
    )(mat, x, *extra)


def _hy_slab_body(*refs, with_filter):
    if with_filter:
        f_ref, i_ref, tc_ref, ts_ref, a_ref, h_ref, o_ref = refs
    else:
        f_ref, tc_ref, ts_ref, a_ref, o_ref = refs
    reps = a_ref.shape[-1] // tc_ref.shape[-1]
    n2 = a_ref.shape[2]
    for s in range(a_ref.shape[1]):
        tc = jnp.concatenate([tc_ref[s]] * reps, axis=1)
        ts = jnp.concatenate([ts_ref[s]] * reps, axis=1)
        ar, ai = a_ref[0, s], a_ref[1, s]
        x = _dot16(f_ref[...], jnp.concatenate([ar * tc + ai * ts, ai * tc - ar * ts], axis=0))
        xr, xi = x[:n2], x[n2:]
        if not with_filter:
            o_ref[0, s] = xr
            o_ref[1, s] = xi
            continue
        hr, hi = h_ref[0, s], h_ref[1, s]
        y = _dot16(i_ref[...], jnp.concatenate([xr * hr - xi * hi, xr * hi + xi * hr], axis=0))
        yr, yi = y[:n2], y[n2:]
        o_ref[0, s] = yr * tc - yi * ts
        o_ref[1, s] = yi * tc + yr * ts


HY_SLABS_PER_STEP = 2


def hy_slab_pass(consts, a, h=None):
    _, n1, n2, c = a.shape
    kb = HY_SLABS_PER_STEP
    fwd = consts["slab_fwd"]
    mspec = _resident(fwd.shape)
    tw = pl.BlockSpec((kb, n2, 128), lambda k: (k, 0, 0))
    slab = pl.BlockSpec((2, kb, n2, c), lambda k: (0, k, 0, 0))
    if h is None:
        args, in_specs = [fwd, consts["tw_cos"], consts["tw_sin"], a], [mspec, tw, tw, slab]
    else:
        args = [fwd, consts["slab_inv"], consts["tw_cos"], consts["tw_sin"], a, h]
        in_specs = [mspec, mspec, tw, tw, slab, slab]
    return pl.pallas_call(
        functools.partial(_hy_slab_body, with_filter=h is not None),
        grid=(n1 // kb,),
        in_specs=in_specs,
        out_specs=slab,
        out_shape=jax.ShapeDtypeStruct(a.shape, F32),
        compiler_params=_cparams(("parallel",)),
        name="hy_slab_conv" if h is not None else "hy_slab_fwd",
    )(*args)


def hyena_pallas(p, conv_w, w1, b1, w2, b2, w3, skip, col0=RET_COLS):
    b, l, _ = p.shape
    assert b == 2, "the two batch rows are packed as one complex signal"
    c = HY_CH
    consts = _hy_constants(l)
    n1, n2 = consts["n1"], consts["n2"]
    taps, sums = hy_filter_taps(l, w1, b1, w2, b2, w3)
    spectra = []
    for o in range(HY_ORDER):
        a_h = hy_rows_pass("taps", consts["rows_taps"], taps[o].reshape(2, n1 // 2, n2, c),
                           extra=(sums[:, o * c:(o + 1) * c],))
        spectra.append(hy_slab_pass(consts, a_h))
    split = lambda t: t.reshape(2, n1 // 2, n2, c)
    v, x1, x2 = hy_prep(p, col0, conv_w, 1024)
    z = split(v)
    for o, xn in enumerate((x1, x2)):
        a = hy_rows_pass("fwd", consts["rows_fwd"], z)
        bm = hy_slab_pass(consts, a, spectra[o])
        z = hy_rows_pass("inv", consts["rows_inv"], bm,
                         extra=(split(xn), z, skip[o].astype(F32).reshape(1, c)))
    return z.reshape(b, l, c)


def rmsnorm(x, w):
    xf = x.astype(F32)
    y = xf * lax.rsqrt(jnp.mean(xf * xf, axis=-1, keepdims=True) + EPS)
    return (y * w.astype(F32)).astype(x.dtype)


def l2norm(x):
    return x * lax.rsqrt(jnp.sum(x * x, axis=-1, keepdims=True) + EPS)


def rotate_half(x, ang):
    cos = jnp.cos(ang)[:, None, :]
    sin = jnp.sin(ang)[:, None, :]
    x1, x2 = jnp.split(x, 2, axis=-1)
    return jnp.concatenate([x1 * cos - x2 * sin, x2 * cos + x1 * sin], axis=-1).astype(x.dtype)


def axial_angles(rows):
    m = SWA_DIM // 4
    inv = ROPE_THETA ** (-jnp.arange(m, dtype=F32) / m)
    row = jnp.repeat(jnp.arange(rows, dtype=F32), GRID_W)
    col = jnp.broadcast_to(jnp.arange(GRID_W, dtype=F32), (rows, GRID_W)).reshape(-1)
    return row[:, None] * inv, col[:, None] * inv


def axial_rope(x, ang_row, ang_col):
    h = x.shape[-1] // 2
    return jnp.concatenate([rotate_half(x[..., :h], ang_row), rotate_half(x[..., h:], ang_col)], axis=-1)


def short_conv(x, w):
    K, C = w.shape
    return lax.conv_general_dilated(x, w[:, None, :].astype(x.dtype), window_strides=(1,),
                                    padding=[(K // 2, K // 2)],
                                    dimension_numbers=('NWC', 'WIO', 'NWC'),
                                    feature_group_count=C)


def flip_time(t):
    return jnp.flip(t, axis=2)


def keep_time(t):
    return t


def delta_chunk_scan(q, k, v, g, beta, s0):
    B, H, L, dk = q.shape
    dv = v.shape[-1]
    C = DN_CHUNK
    N = L // C
    q = q.reshape(B, H, N, C, dk)
    k = k.reshape(B, H, N, C, dk)
    v = v.reshape(B, H, N, C, dv)
    g = g.reshape(B, H, N, C)
    beta = beta.reshape(B, H, N, C)
    gam = jnp.cumsum(g, axis=-1)
    idx = jnp.arange(C)
    incl = idx[:, None] >= idx[None, :]
    strict = idx[:, None] > idx[None, :]
    e = jnp.exp(jnp.where(incl, gam[..., :, None] - gam[..., None, :], 0.0))
    dec_incl = jnp.where(incl, e, 0.0)
    dec_strict = jnp.where(strict, e, 0.0)
    kb = k * beta[..., None]
    m = jnp.einsum('bhnid,bhnjd->bhnij', kb, k) * dec_strict
    a = m + jnp.eye(C, dtype=F32)
    rhs = jnp.concatenate([v * beta[..., None], kb * jnp.exp(gam)[..., None]], axis=-1)
    sol = lax.linalg.triangular_solve(a, rhs, left_side=True, lower=True, unit_diagonal=True)
    u, w = sol[..., :dv], sol[..., dv:]
    attn = jnp.einsum('bhnid,bhnjd->bhnij', q, k) * dec_incl
    q_dec = q * jnp.exp(gam)[..., None]
    k_dec = k * jnp.exp(gam[..., -1:] - gam)[..., None]
    chunk_dec = jnp.exp(gam[..., -1])

    def step(S, xs):
        u_n, w_n, a_n, qd_n, kd_n, cd_n = xs
        v_new = u_n - jnp.einsum('bhck,bhkv->bhcv', w_n, S)
        o = jnp.einsum('bhck,bhkv->bhcv', qd_n, S) + jnp.einsum('bhij,bhjv->bhiv', a_n, v_new)
        S = S * cd_n[..., None, None] + jnp.einsum('bhck,bhcv->bhkv', kd_n, v_new)
        return S, o

    xs = tuple(jnp.moveaxis(t, 2, 0) for t in (u, w, attn, q_dec, k_dec, chunk_dec))
    S, o = lax.scan(step, s0, xs)
    return jnp.moveaxis(o, 0, 2).reshape(B, H, L, dv), S


def dn_features(p, conv_w, a_log, dt_bias):
    B, L, _ = p.shape
    qkv = jax.nn.silu(short_conv(p[..., :3 * DN_WIDTH], conv_w)).astype(F32)
    heads = lambda t: t.reshape(B, L, DN_HEADS, DN_DIM).transpose(0, 2, 1, 3)
    q, k, v = (heads(t) for t in jnp.split(qkv, 3, axis=-1))
    q = l2norm(q) * DN_DIM ** -0.5
    k = l2norm(k)
    gate = p[..., 3 * DN_WIDTH:4 * DN_WIDTH]
    ab = p[..., 4 * DN_WIDTH:].astype(F32).reshape(B, L, 2, 2, DN_HEADS)
    g = -jnp.exp(a_log.astype(F32)) * jax.nn.softplus(ab[:, :, 0] + dt_bias.astype(F32))
    beta = jax.nn.sigmoid(ab[:, :, 1])
    return q, k, v, g.transpose(2, 0, 3, 1), beta.transpose(2, 0, 3, 1), gate


def head_norm_gate(o, gate, w):
    B, H, L, dv = o.shape
    y = rmsnorm(o.transpose(0, 2, 1, 3), w)
    return (y * jax.nn.silu(gate.astype(F32).reshape(B, L, H, dv))).reshape(B, L, H * dv)


def gated_deltanet(p_lat, p_ctx, conv_w, a_log, dt_bias, norm_w, need_ctx):
    lat = dn_features(p_lat, conv_w, a_log, dt_bias)
    cx = dn_features(p_ctx, conv_w, a_log, dt_bias)
    B = p_lat.shape[0]
    o_lat, o_ctx = [], []
    for d in range(2):
        fl = flip_time if d == 1 else keep_time
        s0 = jnp.zeros((B, DN_HEADS, DN_DIM, DN_DIM), F32)
        oc, s_ctx = delta_chunk_scan(fl(cx[0]), fl(cx[1]), fl(cx[2]), fl(cx[3][d]), fl(cx[4][d]), s0)
        ol, _ = delta_chunk_scan(fl(lat[0]), fl(lat[1]), fl(lat[2]), fl(lat[3][d]), fl(lat[4][d]), s_ctx)
        o_lat.append(fl(ol))
        o_ctx.append(fl(oc))
    out_lat = head_norm_gate(o_lat[0] + o_lat[1], lat[5], norm_w)
    out_ctx = head_norm_gate(o_ctx[0] + o_ctx[1], cx[5], norm_w) if need_ctx else None
    return out_lat, out_ctx


def gqa_qkv(p, q_norm_w, k_norm_w):
    B, L, _ = p.shape
    nq, nk = SWA_HEADS * SWA_DIM, SWA_KV_HEADS * SWA_DIM
    q = rmsnorm(p[..., :nq].reshape(B, L, SWA_HEADS, SWA_DIM), q_norm_w)
    k = rmsnorm(p[..., nq:nq + nk].reshape(B, L, SWA_KV_HEADS, SWA_DIM), k_norm_w)
    v = p[..., nq + nk:].reshape(B, L, SWA_KV_HEADS, SWA_DIM)
    return q, k, v


def banded_attention(q, k, v, kc, vc, sink):
    B, L, Hq, d = q.shape
    Hkv = k.shape[2]
    G = Hq // Hkv
    W = SWA_BLOCK
    NB = L // W
    qb = q.reshape(B, NB, W, Hkv, G, d)

    def band(t):
        tp = jnp.pad(t.reshape(B, NB, W, Hkv, d), ((0, 0), (1, 1), (0, 0), (0, 0), (0, 0)))
        return jnp.concatenate([tp[:, :-2], tp[:, 1:-1], tp[:, 2:]], axis=2)

    kb, vb = band(k), band(v)
    scale = d ** -0.5
    s_loc = jnp.einsum('bnqhgd,bnkhd->bnhgqk', qb, kb).astype(F32) * scale
    s_ctx = jnp.einsum('bnqhgd,bchd->bnhgqc', qb, kc).astype(F32) * scale
    rel = (jnp.arange(3 * W) - W)[None, :] - jnp.arange(W)[:, None]
    kblk = jnp.arange(NB)[:, None] + jnp.arange(3 * W)[None, :] // W - 1
    valid = (jnp.abs(rel) <= SWA_WINDOW)[None] & ((kblk >= 0) & (kblk < NB))[:, None, :]
    s_loc = jnp.where(valid[None, :, None, None], s_loc, -jnp.inf)
    s_sink = jnp.broadcast_to(sink.astype(F32).reshape(Hkv, G, 1, 1), s_loc.shape[:-1] + (1,))
    prob = jax.nn.softmax(jnp.concatenate([s_loc, s_ctx, s_sink], axis=-1), axis=-1).astype(v.dtype)
    n_ctx = kc.shape[1]
    o = (jnp.einsum('bnhgqk,bnkhd->bnqhgd', prob[..., :3 * W], vb)
         + jnp.einsum('bnhgqc,bchd->bnqhgd', prob[..., 3 * W:3 * W + n_ctx], vc))
    return o.reshape(B, L, Hq * d)


def context_attention(qc, kc, vc, sink):
    B, Cn, Hq, d = qc.shape
    Hkv = kc.shape[2]
    G = Hq // Hkv
    q = qc.reshape(B, Cn, Hkv, G, d)
    s = jnp.einsum('bqhgd,bkhd->bhgqk', q, kc).astype(F32) * d ** -0.5
    s_sink = jnp.broadcast_to(sink.astype(F32).reshape(Hkv, G, 1, 1), s.shape[:-1] + (1,))
    prob = jax.nn.softmax(jnp.concatenate([s, s_sink], axis=-1), axis=-1)[..., :-1].astype(vc.dtype)
    return jnp.einsum('bhgqk,bkhd->bqhgd', prob, vc).reshape(B, Cn, Hq * d)


def window_gqa(p_lat, p_ctx, ang_row, ang_col, q_norm_w, k_norm_w, sink, need_ctx):
    ql, kl, vl = gqa_qkv(p_lat, q_norm_w, k_norm_w)
    ql = axial_rope(ql, ang_row, ang_col)
    kl = axial_rope(kl, ang_row, ang_col)
    qc, kc, vc = gqa_qkv(p_ctx, q_norm_w, k_norm_w)
    o_lat = banded_attention(ql, kl, vl, kc, vc, sink)
    o_ctx = context_attention(qc, kc, vc, sink) if need_ctx else None
    return o_lat, o_ctx


def ab_mixer(p_lat, p_ctx, ang_row, ang_col, dn_conv_w, dn_a_log, dn_dt_bias, dn_norm_w,
             q_norm_w, k_norm_w, sink, need_ctx):
    a_lat, a_ctx = gated_deltanet(p_lat[..., :DN_COLS], p_ctx[..., :DN_COLS], dn_conv_w, dn_a_log,
                                  dn_dt_bias, dn_norm_w, need_ctx)
    b_lat, b_ctx = window_gqa(p_lat[..., DN_COLS:], p_ctx[..., DN_COLS:], ang_row, ang_col,
                              q_norm_w, k_norm_w, sink, need_ctx)
    o_lat = jnp.concatenate([a_lat, b_lat.astype(F32)], axis=-1)
    o_ctx = jnp.concatenate([a_ctx, b_ctx.astype(F32)], axis=-1) if need_ctx else None
    return o_lat, o_ctx


def retention_chunk_scan(q, k, v, log_gamma, s0):
    B, H, L, dk = q.shape
    dv = v.shape[-1]
    C = RET_CHUNK
    N = L // C
    q = q.reshape(B, H, N, C, dk)
    k = k.reshape(B, H, N, C, dk)
    v = v.reshape(B, H, N, C, dv)
    pos = jnp.arange(C, dtype=F32)
    rel = pos[:, None] - pos[None, :]
    incl = rel >= 0
    lg = log_gamma[:, None, None]
    dmat = jnp.where(incl, jnp.exp(lg * jnp.where(incl, rel, 0.0)), 0.0)
    scores = jnp.einsum('bhnid,bhnjd->bhnij', q, k) * dmat[:, None]
    o_in = jnp.einsum('bhnij,bhnje->bhnie', scores, v)
    q_dec = q * jnp.exp(log_gamma[:, None] * (pos + 1.0))[:, None, :, None]
    k_dec = k * jnp.exp(log_gamma[:, None] * (C - 1.0 - pos))[:, None, :, None]
    chunk_dec = jnp.exp(log_gamma * C)[None, :, None, None]

    def step(S, xs):
        qd, kd, vn = xs
        o = jnp.einsum('bhcd,bhde->bhce', qd, S)
        S = S * chunk_dec + jnp.einsum('bhcd,bhce->bhde', kd, vn)
        return S, o

    xs = tuple(jnp.moveaxis(t, 2, 0) for t in (q_dec, k_dec, v))
    S, o_x = lax.scan(step, s0, xs)
    o = o_in + jnp.moveaxis(o_x, 0, 2)
    return o.reshape(B, H, L, dv), S


def ret_features(p):
    B, L, _ = p.shape
    q, k, v, g = jnp.split(p, 4, axis=-1)
    inv = ROPE_THETA ** (-jnp.linspace(0.0, 1.0, RET_DIM // 2, dtype=F32))
    ang = jnp.arange(L, dtype=F32)[:, None] * inv
    heads = lambda t: t.reshape(B, L, RET_HEADS, RET_DIM).astype(F32)
    q = rotate_half(heads(q), ang)
    k = rotate_half(heads(k), ang) * RET_DIM ** -0.5
    tr = lambda t: t.transpose(0, 2, 1, 3)
    return tr(q), tr(k), tr(heads(v)), g


def head_groupnorm_gate(o, gate, w):
    B, H, L, dv = o.shape
    o = o.transpose(0, 2, 1, 3)
    mu = jnp.mean(o, axis=-1, keepdims=True)
    var = jnp.mean(jnp.square(o - mu), axis=-1, keepdims=True)
    y = (o - mu) * lax.rsqrt(var + EPS) * w.astype(F32).reshape(H, dv)
    return (y * jax.nn.silu(gate.astype(F32).reshape(B, L, H, dv))).reshape(B, L, H * dv)


def retention(p_lat, p_ctx, decay_logit, gn_w, need_ctx):
    lat, cx = ret_features(p_lat), ret_features(p_ctx)
    log_gamma = jax.nn.log_sigmoid(decay_logit.astype(F32))
    B = p_lat.shape[0]
    o_lat, o_ctx = [], []
    for d in range(2):
        fl = flip_time if d == 1 else keep_time
        s0 = jnp.zeros((B, RET_HEADS, RET_DIM, RET_DIM), F32)
        oc, s_ctx = retention_chunk_scan(fl(cx[0]), fl(cx[1]), fl(cx[2]), log_gamma[d], s0)
        ol, _ = retention_chunk_scan(fl(lat[0]), fl(lat[1]), fl(lat[2]), log_gamma[d], s_ctx)
        o_lat.append(fl(ol))
        o_ctx.append(fl(oc))
    out_lat = head_groupnorm_gate(o_lat[0] + o_lat[1], lat[3], gn_w)
    out_ctx = head_groupnorm_gate(o_ctx[0] + o_ctx[1], cx[3], gn_w) if need_ctx else None
    return out_lat, out_ctx


def hyena_filters(L, w1, b1, w2, b2, w3):
    pos = jnp.arange(L, dtype=F32)
    t = pos / max(L - 1, 1)
    bands = jnp.linspace(1e-4, HY_BANDS - 1, HY_BANDS, dtype=F32)
    phase = (2.0 * math.pi / L) * pos[:, None] * bands[None, :]
    z = jnp.concatenate([t[:, None], jnp.cos(phase), -jnp.sin(phase)], axis=-1)
    h = jnp.sin(HY_SIN_FREQ * (z @ w1.astype(F32) + b1.astype(F32)))
    h = jnp.sin(HY_SIN_FREQ * (h @ w2.astype(F32) + b2.astype(F32)))
    h = (h @ w3.astype(F32)).reshape(L, HY_ORDER, 2, HY_CH)
    rates = jnp.abs(jnp.linspace(math.log(HY_TARGET) / HY_DECAY_LONG, math.log(HY_TARGET) / HY_DECAY_SHORT,
                                 HY_CH, dtype=F32))
    h = h * jnp.exp(-t[:, None] * rates[None, :])[:, None, None, :]
    h = h / (jnp.sum(jnp.abs(h), axis=(0, 2), keepdims=True) + EPS)
    return h.transpose(1, 2, 0, 3)


def bidir_long_conv(u, h_fwd, h_bwd, skip):
    B, L, C = u.shape
    taps = jnp.concatenate([h_fwd, jnp.zeros((1, C), F32), h_bwd[:0:-1]], axis=0)
    y = jnp.fft.irfft(jnp.fft.rfft(u, n=2 * L, axis=1) * jnp.fft.rfft(taps, axis=0)[None],
                      n=2 * L, axis=1)[:, :L]
    return y + u * skip


def hyena(p, conv_w, w1, b1, w2, b2, w3, skip):
    L = p.shape[1]
    filt = hyena_filters(L, w1, b1, w2, b2, w3)
    u = short_conv(p, conv_w).astype(F32)
    parts = jnp.split(u, HY_ORDER + 1, axis=-1)
    z = parts[0]
    for n in range(HY_ORDER):
        z = parts[n + 1] * bidir_long_conv(z, filt[n, 0], filt[n, 1], skip[n].astype(F32))
    return z


def cd_mixer(p_lat, p_ctx, ret_decay_logit, ret_gn_w, hy_conv_w, hy_f_w1, hy_f_b1, hy_f_w2, hy_f_b2,
             hy_f_w3, hy_bias, need_ctx):
    c_lat, c_ctx = retention(p_lat[..., :RET_COLS], p_ctx[..., :RET_COLS], ret_decay_logit, ret_gn_w, need_ctx)
    hy = lambda p: hyena(p, hy_conv_w, hy_f_w1, hy_f_b1, hy_f_w2, hy_f_b2, hy_f_w3, hy_bias)
    o_lat = jnp.concatenate([c_lat, hy(p_lat[..., RET_COLS:])], axis=-1)
    o_ctx = jnp.concatenate([c_ctx, hy(p_ctx[..., RET_COLS:])], axis=-1) if need_ctx else None
    return o_lat, o_ctx


def _pad_cols(w, mult):
    n = w.shape[-1]
    pad = (-n) % mult
    return jnp.pad(w, ((0, 0), (0, pad))) if pad else w


def kernel(x, c, ctx, c_ctx, mod_w, mod_b, norm_mix_w, norm_ffn_w, ffn_w_in, ffn_w_out,
           ab_w_in, ab_w_out, dn_conv_w, dn_a_log, dn_dt_bias, dn_norm_w, swa_q_norm_w, swa_k_norm_w,
           swa_sink, cd_w_in, cd_w_out, ret_decay_logit, ret_gn_w, hy_conv_w, hy_f_w1, hy_f_b1,
           hy_f_w2, hy_f_b2, hy_f_w3, hy_bias):
    B, L, D = x.shape
    n_ctx = ctx.shape[1]
    rows = L // GRID_W
    ang_row, ang_col = axial_angles(rows)
    c_rows = jnp.zeros((8, D), F32).at[:B].set(c).at[B].set(c_ctx)
    mod_all = modulation(c_rows, mod_w, mod_b)
    hid = ffn_w_out.shape[1]
    h_ctx = ctx
    for layer in range(DEPTH):
        need_ctx = layer != DEPTH - 1
        i = layer // 2
        m = mod_all[layer].reshape(8, 6, D)
        mod = [m[:B, j][:, None, :] for j in range(6)]
        mod_c = [jnp.broadcast_to(m[B, j][None, None, :], (B, 1, D)) for j in range(6)]
        if layer % 2 == 0:
            w = ab_w_in[i]
            w_in = jnp.concatenate([w[:, :4 * DN_WIDTH], w[:, DN_COLS:], w[:, 4 * DN_WIDTH:DN_COLS]], axis=1)
            w_out, cols = ab_w_out[i], ab_w_in.shape[-1]
        else:
            w = cd_w_in[i]
            w_in = jnp.concatenate([w[:, RET_COLS:], w[:, :RET_COLS]], axis=1)
            w_out, cols = cd_w_out[i], cd_w_in.shape[-1]
        w_in_p = _pad_cols(w_in, 128).astype(BF16)
        p_lat = in_projection(x, norm_mix_w[layer], mod[0], mod[1], w_in_p, 512)
        p_ctx = in_projection(h_ctx, norm_mix_w[layer], mod_c[0], mod_c[1], w_in_p, n_ctx)
        if layer % 2 == 0:
            swa0 = 4 * DN_WIDTH
            ab0 = swa0 + (SWA_HEADS + 2 * SWA_KV_HEADS) * SWA_DIM
            a_lat, a_ctx = deltanet_pallas(p_lat, p_ctx, dn_conv_w[i], dn_a_log[i], dn_dt_bias[i],
                                           dn_norm_w[i], need_ctx, ab_col=ab0, gate_col=3 * DN_WIDTH)
            b_lat, b_ctx = window_gqa_pallas(p_lat, p_ctx, swa0, axial_rope_tables(rows), swa_q_norm_w[i],
                                             swa_k_norm_w[i], swa_sink[i], need_ctx)
        else:
            a_lat, a_ctx = retention_pallas(p_lat, p_ctx, ret_decay_logit[i], ret_gn_w[i], need_ctx,
                                            col0=(HY_ORDER + 1) * HY_CH)
            assert not need_ctx, "the last layer's context outputs reach no latent token"
            b_lat = hyena_pallas(p_lat, hy_conv_w[i], hy_f_w1[i], hy_f_b1[i], hy_f_w2[i], hy_f_b2[i],
                                 hy_f_w3[i], hy_bias[i], col0=0)
            b_ctx = None
        wo = w_out.astype(BF16)
        wg = ffn_w_in[layer][:, :hid].astype(BF16)
        wu = ffn_w_in[layer][:, hid:].astype(BF16)
        wd = ffn_w_out[layer].astype(BF16)
        x = out_projection_ffn(x, a_lat, b_lat, wo, mod[2], norm_ffn_w[layer], mod[3], mod[4], mod[5],
                               wg, wu, wd, 512, 256)
        if need_ctx:
            h_ctx = out_projection_ffn(h_ctx, a_ctx, b_ctx, wo, mod_c[2], norm_ffn_w[layer], mod_c[3], mod_c[4],
                                       mod_c[5], wg, wu, wd, n_ctx, 256)
    return x
```

```python
import functools
import math

import jax
import jax.numpy as jnp
import ml_dtypes
import numpy as np
from jax import lax
from jax.experimental import pallas as pl
from jax.experimental.pallas import tpu as pltpu

F32 = jnp.float32
BF16 = jnp.bfloat16
EPS = 1e-6

D_MODEL = 1024
DEPTH = 2
GRID_W = 64

DN_HEADS = 4
DN_DIM = 128
DN_WIDTH = DN_HEADS * DN_DIM
DN_CHUNK = 64
DN_COLS = 4 * DN_WIDTH + 4 * DN_HEADS
SWA_HEADS = 4
SWA_KV_HEADS = 2
SWA_DIM = 128
SWA_WINDOW = 128
SWA_BLOCK = 128
ROPE_THETA = 10000.0
RET_HEADS = 4
RET_DIM = 128
RET_WIDTH = RET_HEADS * RET_DIM
RET_CHUNK = 128
RET_COLS = 4 * RET_WIDTH
HY_CH = 512
HY_ORDER = 2
HY_BANDS = 8
HY_SIN_FREQ = 1.0
HY_TARGET = 1e-2
HY_DECAY_SHORT = 0.3
HY_DECAY_LONG = 1.5

VMEM_LIMIT_BYTES = 56 * 1024 * 1024


def _cparams(sem):
    return pltpu.CompilerParams(dimension_semantics=sem, vmem_limit_bytes=VMEM_LIMIT_BYTES)


def _mod_body(c_ref, w_ref, b_ref, o_ref):
    a = c_ref[...]
    a = a * jax.nn.sigmoid(a)
    o_ref[...] = jnp.dot(a, w_ref[...], preferred_element_type=F32,
                         precision=lax.Precision.HIGHEST) + b_ref[...]


def modulation(c_rows, mod_w, mod_b):
    depth, d, n = mod_w.shape
    tn = 1536
    return pl.pallas_call(
        _mod_body,
        grid=(depth, n // tn),
        in_specs=[pl.BlockSpec((8, d), lambda l, j: (0, 0)),
                  pl.BlockSpec((None, d, tn), lambda l, j: (l, 0, j)),
                  pl.BlockSpec((None, 1, tn), lambda l, j: (l, 0, j))],
        out_specs=pl.BlockSpec((None, 8, tn), lambda l, j: (l, 0, j)),
        out_shape=jax.ShapeDtypeStruct((depth, 8, n), F32),
        compiler_params=_cparams(("parallel", "parallel")),
        name="modulation",
    )(c_rows, mod_w, mod_b.reshape(depth, 1, n))


def _norm_mod(x, nw, shift, scale):
    y = x * lax.rsqrt(jnp.mean(x * x, axis=-1, keepdims=True) + EPS)
    return (y * nw) * (1.0 + scale) + shift


def _resident(shape):
    return pl.BlockSpec(shape, lambda *_: (0,) * len(shape), pipeline_mode=pl.Buffered(1))


def _inproj_body(x_ref, nw_ref, shift_ref, scale_ref, w_ref, o_ref):
    h = _norm_mod(x_ref[...], nw_ref[...], shift_ref[...], scale_ref[...]).astype(BF16)
    o_ref[...] = jnp.dot(h, w_ref[...], preferred_element_type=F32)


def in_projection(x, nw, shift, scale, w, tm):
    b, l, d = x.shape
    n = w.shape[1]
    return pl.pallas_call(
        _inproj_body,
        grid=(b, l // tm),
        in_specs=[pl.BlockSpec((None, tm, d), lambda bi, i: (bi, i, 0)),
                  _resident((1, d)),
                  pl.BlockSpec((None, 1, d), lambda bi, i: (bi, 0, 0)),
                  pl.BlockSpec((None, 1, d), lambda bi, i: (bi, 0, 0)),
                  _resident((d, n))],
        out_specs=pl.BlockSpec((None, tm, n), lambda bi, i: (bi, i, 0)),
        out_shape=jax.ShapeDtypeStruct((b, l, n), F32),
        compiler_params=_cparams(("parallel", "parallel")),
        name="in_projection",
    )(x, nw.reshape(1, d), shift, scale, w)


def _outffn_body(x_ref, oa_ref, ob_ref, wo_ref, g2_ref, nw_ref, sh_ref, sc_ref, g5_ref,
                 wg_ref, wu_ref, wd_ref, y_ref, x1_scr, h_scr, acc_scr, *, th):
    wa = oa_ref.shape[1]
    mix = (jnp.dot(oa_ref[...].astype(BF16), wo_ref[:wa, :], preferred_element_type=F32)
           + jnp.dot(ob_ref[...].astype(BF16), wo_ref[wa:, :], preferred_element_type=F32))
    x1 = x_ref[...] + g2_ref[...] * mix
    x1_scr[...] = x1
    h_scr[...] = _norm_mod(x1, nw_ref[...], sh_ref[...], sc_ref[...]).astype(BF16)
    for k in range(wg_ref.shape[1] // th):
        ks = slice(k * th, (k + 1) * th)
        g = jnp.dot(h_scr[...], wg_ref[:, ks], preferred_element_type=F32)
        u = jnp.dot(h_scr[...], wu_ref[:, ks], preferred_element_type=F32)
        a = (g * jax.nn.sigmoid(g) * u).astype(BF16)
        part = jnp.dot(a, wd_ref[ks, :], preferred_element_type=F32)
        if k == 0:
            acc_scr[...] = part
        else:
            acc_scr[...] += part
    y_ref[...] = x1_scr[...] + g5_ref[...] * acc_scr[...]


def out_projection_ffn(x, oa, ob, wo, g2, nw, shift, scale, g5, wg, wu, wd, tm, th):
    b, l, d = x.shape
    wa, wb = oa.shape[-1], ob.shape[-1]
    hid = wg.shape[1]
    tok = lambda bi, i: (bi, i, 0)
    vec = pl.BlockSpec((None, 1, d), lambda bi, i: (bi, 0, 0))
    return pl.pallas_call(
        functools.partial(_outffn_body, th=th),
        grid=(b, l // tm),
        in_specs=[pl.BlockSpec((None, tm, d), tok),
                  pl.BlockSpec((None, tm, wa), tok),
                  pl.BlockSpec((None, tm, wb), tok),
                  _resident((d, d)), vec, _resident((1, d)), vec, vec, vec,
                  _resident((d, hid)), _resident((d, hid)), _resident((hid, d))],
        out_specs=pl.BlockSpec((None, tm, d), tok),
        out_shape=jax.ShapeDtypeStruct((b, l, d), F32),
        scratch_shapes=[pltpu.VMEM((tm, d), F32), pltpu.VMEM((tm, d), BF16), pltpu.VMEM((tm, d), F32)],
        compiler_params=_cparams(("parallel", "parallel")),
        name="out_projection_ffn",
    )(x, oa, ob, wo, g2, nw.reshape(1, d), shift, scale, g5, wg, wu, wd)


NEG_BIG = -1e30


def _head_rms(x, w):
    return x * lax.rsqrt(jnp.mean(x * x, axis=-1, keepdims=True) + EPS) * w


def _swa_prep_body(*refs, use_rope):
    if use_rope:
        q_ref, k_ref, v_ref, qw_ref, kw_ref, rt_ref, ct_ref, qo_ref, ko_ref, vo_ref = refs
        tl = q_ref.shape[0]
        nr = tl // GRID_W

        def table(kind):
            rowp = jnp.broadcast_to(rt_ref[kind][:, None, :], (nr, GRID_W, SWA_DIM))
            colp = jnp.broadcast_to(ct_ref[kind][None, :, :], (nr, GRID_W, SWA_DIM))
            return (rowp + colp).reshape(tl, SWA_DIM)

        cos, sa, sb = table(0), table(1), table(2)
    else:
        q_ref, k_ref, v_ref, qw_ref, kw_ref, qo_ref, ko_ref, vo_ref = refs

    def prep(x, w):
        y = _head_rms(x, w)
        if use_rope:
            y = y * cos + pltpu.roll(y, SWA_DIM - 32, 1) * sa + pltpu.roll(y, 32, 1) * sb
        return y.astype(BF16)

    for h in range(SWA_HEADS):
        sl = slice(h * SWA_DIM, (h + 1) * SWA_DIM)
        qo_ref[:, sl] = prep(q_ref[:, sl], qw_ref[...])
    for h in range(SWA_KV_HEADS):
        sl = slice(h * SWA_DIM, (h + 1) * SWA_DIM)
        ko_ref[:, sl] = prep(k_ref[:, sl], kw_ref[...])
    vo_ref[...] = v_ref[...].astype(BF16)


def swa_prep(p, col0, qw, kw, rope, tl):
    b, l, _ = p.shape
    nq, nk = SWA_HEADS * SWA_DIM, SWA_KV_HEADS * SWA_DIM
    tok = lambda bi, i: (bi, i, 0)
    in_specs = [pl.BlockSpec((None, tl, nq), lambda bi, i: (bi, i, col0 // nq)),
                pl.BlockSpec((None, tl, nk), lambda bi, i: (bi, i, (col0 + nq) // nk)),
                pl.BlockSpec((None, tl, nk), lambda bi, i: (bi, i, (col0 + nq + nk) // nk)),
                pl.BlockSpec((1, SWA_DIM), lambda bi, i: (0, 0)),
                pl.BlockSpec((1, SWA_DIM), lambda bi, i: (0, 0))]
    args = [p, p, p, qw.reshape(1, SWA_DIM), kw.reshape(1, SWA_DIM)]
    if rope is not None:
        in_specs += [pl.BlockSpec((3, tl // GRID_W, SWA_DIM), lambda bi, i: (0, i, 0)),
                     pl.BlockSpec((3, GRID_W, SWA_DIM), lambda bi, i: (0, 0, 0))]
        args += list(rope)
    return pl.pallas_call(
        functools.partial(_swa_prep_body, use_rope=rope is not None),
        grid=(b, l // tl),
        in_specs=in_specs,
        out_specs=[pl.BlockSpec((None, tl, nq), tok), pl.BlockSpec((None, tl, nk), tok),
                   pl.BlockSpec((None, tl, nk), tok)],
        out_shape=[jax.ShapeDtypeStruct((b, l, nq), BF16), jax.ShapeDtypeStruct((b, l, nk), BF16),
                   jax.ShapeDtypeStruct((b, l, nk), BF16)],
        compiler_params=_cparams(("parallel", "parallel")),
        name="swa_prep",
    )(*args)


def _nt_dot(a, b):
    return lax.dot_general(a, b, (((1,), (1,)), ((), ())), preferred_element_type=F32)


def _swa_attend(q, keys, vals, masks, sink_col):
    scale = SWA_DIM ** -0.5
    scores = []
    for kk, mask in zip(keys, masks):
        s = _nt_dot(q, kk) * scale
        if mask is not None:
            s = jnp.where(mask, s, NEG_BIG)
        scores.append(s)
    m = sink_col
    for s in scores:
        m = jnp.maximum(m, jnp.max(s, axis=-1, keepdims=True))
    den = jnp.exp(sink_col - m)
    acc = None
    for s, vv in zip(scores, vals):
        pr = jnp.exp(s - m)
        den = den + jnp.sum(pr, axis=-1, keepdims=True)
        o = jnp.dot(pr.astype(BF16), vv, preferred_element_type=F32)
        acc = o if acc is None else acc + o
    return acc / den


def _swa_attn_body(sink_ref, q_ref, kp_ref, km_ref, kn_ref, vp_ref, vm_ref, vn_ref, kc_ref, vc_ref,
                   o_ref, *, nb):
    i = pl.program_id(1)
    n_blocks = pl.num_programs(1) * nb
    w, d = SWA_BLOCK, SWA_DIM
    grp = SWA_HEADS // SWA_KV_HEADS
    rows = lax.broadcasted_iota(jnp.int32, (grp * w, w), 0)
    qi = rows % w
    kj = lax.broadcasted_iota(jnp.int32, (grp * w, w), 1)
    for r in range(nb):
        blk = i * nb + r
        mask_prev = (kj >= qi) & (blk > 0)
        mask_next = (kj <= qi) & (blk < n_blocks - 1)
        rs = slice(r * w, (r + 1) * w)
        for h in range(SWA_KV_HEADS):
            hs = slice(h * d, (h + 1) * d)
            q = jnp.concatenate([q_ref[rs, (h * grp + g) * d:(h * grp + g + 1) * d] for g in range(grp)], axis=0)
            sink_col = jnp.zeros((grp * w, 1), F32)
            for g in range(grp):
                sink_col = jnp.where((rows[:, :1] // w) == g, sink_ref[h * grp + g], sink_col)
            if r > 0:
                k_prev, v_prev = km_ref[(r - 1) * w:r * w, hs], vm_ref[(r - 1) * w:r * w, hs]
            else:
                k_prev, v_prev = kp_ref[:, hs], vp_ref[:, hs]
            if r < nb - 1:
                k_next, v_next = km_ref[(r + 1) * w:(r + 2) * w, hs], vm_ref[(r + 1) * w:(r + 2) * w, hs]
            else:
                k_next, v_next = kn_ref[:, hs], vn_ref[:, hs]
            o = _swa_attend(q, [k_prev, km_ref[rs, hs], k_next, kc_ref[:, hs]],
                            [v_prev, vm_ref[rs, hs], v_next, vc_ref[:, hs]],
                            [mask_prev, None, mask_next, None], sink_col)
            for g in range(grp):
                o_ref[rs, (h * grp + g) * d:(h * grp + g + 1) * d] = o[g * w:(g + 1) * w].astype(o_ref.dtype)


def swa_attention(q, k, v, kc, vc, sink, nb):
    b, l, nq = q.shape
    nk = k.shape[-1]
    n_ctx = kc.shape[1]
    w = SWA_BLOCK
    tq = nb * w
    last = l // w - 1
    main = lambda bi, i, s: (bi, i, 0)
    prev = lambda bi, i, s: (bi, jnp.maximum(i * nb - 1, 0), 0)
    nxt = lambda bi, i, s: (bi, jnp.minimum((i + 1) * nb, last), 0)
    cx = lambda bi, i, s: (bi, 0, 0)
    return pl.pallas_call(
        functools.partial(_swa_attn_body, nb=nb),
        grid_spec=pltpu.PrefetchScalarGridSpec(
            num_scalar_prefetch=1,
            grid=(b, l // tq),
            in_specs=[pl.BlockSpec((None, tq, nq), main),
                      pl.BlockSpec((None, w, nk), prev), pl.BlockSpec((None, tq, nk), main),
                      pl.BlockSpec((None, w, nk), nxt),
                      pl.BlockSpec((None, w, nk), prev), pl.BlockSpec((None, tq, nk), main),
                      pl.BlockSpec((None, w, nk), nxt),
                      pl.BlockSpec((None, n_ctx, nk), cx), pl.BlockSpec((None, n_ctx, nk), cx)],
            out_specs=pl.BlockSpec((None, tq, nq), main)),
        out_shape=jax.ShapeDtypeStruct((b, l, nq), BF16),
        compiler_params=_cparams(("parallel", "parallel")),
        name="swa_attention",
    )(sink, q, k, k, k, v, v, v, kc, vc)


def _ctx_attn_body(sink_ref, q_ref, kc_ref, vc_ref, o_ref):
    n_ctx, d = q_ref.shape[0], SWA_DIM
    grp = SWA_HEADS // SWA_KV_HEADS
    rows = lax.broadcasted_iota(jnp.int32, (grp * n_ctx, 1), 0)
    for h in range(SWA_KV_HEADS):
        hs = slice(h * d, (h + 1) * d)
        q = jnp.concatenate([q_ref[:, (h * grp + g) * d:(h * grp + g + 1) * d] for g in range(grp)], axis=0)
        sink_col = jnp.zeros((grp * n_ctx, 1), F32)
        for g in range(grp):
            sink_col = jnp.where((rows // n_ctx) == g, sink_ref[h * grp + g], sink_col)
        o = _swa_attend(q, [kc_ref[:, hs]], [vc_ref[:, hs]], [None], sink_col)
        for g in range(grp):
            o_ref[:, (h * grp + g) * d:(h * grp + g + 1) * d] = o[g * n_ctx:(g + 1) * n_ctx].astype(o_ref.dtype)


def ctx_attention(qc, kc, vc, sink):
    b, n_ctx, nq = qc.shape
    nk = kc.shape[-1]
    cx = lambda bi, s: (bi, 0, 0)
    return pl.pallas_call(
        _ctx_attn_body,
        grid_spec=pltpu.PrefetchScalarGridSpec(
            num_scalar_prefetch=1, grid=(b,),
            in_specs=[pl.BlockSpec((None, n_ctx, nq), cx), pl.BlockSpec((None, n_ctx, nk), cx),
                      pl.BlockSpec((None, n_ctx, nk), cx)],
            out_specs=pl.BlockSpec((None, n_ctx, nq), cx)),
        out_shape=jax.ShapeDtypeStruct((b, n_ctx, nq), BF16),
        compiler_params=_cparams(("parallel",)),
        name="ctx_attention",
    )(sink, qc, kc, vc)


def axial_rope_tables(rows):
    m = SWA_DIM // 4
    inv = ROPE_THETA ** (-np.arange(m, dtype=np.float64) / m)

    def tables(count, lane0):
        ang = np.arange(count, dtype=np.float64)[:, None] * inv
        out = np.zeros((3, count, SWA_DIM))
        out[0, :, lane0:lane0 + m] = out[0, :, lane0 + m:lane0 + 2 * m] = np.cos(ang)
        out[1, :, lane0:lane0 + m] = -np.sin(ang)
        out[2, :, lane0 + m:lane0 + 2 * m] = np.sin(ang)
        return out.astype(np.float32)

    return tables(rows, 0), tables(GRID_W, SWA_DIM // 2)


def window_gqa_pallas(p_lat, p_ctx, col0, rope, q_norm_w, k_norm_w, sink, need_ctx):
    ql, kl, vl = swa_prep(p_lat, col0, q_norm_w, k_norm_w, rope, 1024)
    qc, kc, vc = swa_prep(p_ctx, col0, q_norm_w, k_norm_w, None, p_ctx.shape[1])
    o_lat = swa_attention(ql, kl, vl, kc, vc, sink, 4)
    o_ctx = ctx_attention(qc, kc, vc, sink) if need_ctx else None
    return o_lat, o_ctx


def _tn_dot(a, b):
    return lax.dot_general(a, b, (((0,), (0,)), ((), ())), preferred_element_type=F32)


def _log_sigmoid(x):
    return jnp.minimum(x, 0.0) - jnp.log(1.0 + jnp.exp(-jnp.abs(x)))


RET_STREAMS = 2 * RET_HEADS


RET_SUBS = 2


def _ret_scan_body(logit_ref, qf_ref, kf_ref, vf_ref, rotf_ref, qb_ref, kb_ref, vb_ref, rotb_ref, rotc_ref,
                   s0_ref, of_ref, ob_ref, sf_ref, s_scr):
    n = pl.program_id(1)
    c, dh = RET_CHUNK, RET_DIM

    @pl.when(n == 0)
    def _():
        s_scr[...] = s0_ref[...]

    ii = lax.broadcasted_iota(jnp.int32, (c, c), 0)
    jj = lax.broadcasted_iota(jnp.int32, (c, c), 1)
    i1 = lax.broadcasted_iota(jnp.int32, (c, 1), 0)
    rel = {True: (ii - jj).astype(F32), False: (jj - ii).astype(F32)}
    pos = {True: i1.astype(F32), False: (c - 1 - i1).astype(F32)}
    streams = [(d == 0, h) for d in range(2) for h in range(RET_HEADS)]
    srcs = {True: (qf_ref, kf_ref, vf_ref, rotf_ref), False: (qb_ref, kb_ref, vb_ref, rotb_ref)}
    items = [(r, f, h, j, (j if f else RET_SUBS - 1 - j)) for j in range(RET_SUBS) for r, (f, h) in enumerate(streams)]
    rows = lambda blk: slice(blk * c, (blk + 1) * c)
    hs = lambda h: slice(h * dh, (h + 1) * dh)
    rng = range(len(items))

    cos_i, sin_i = rotc_ref[0], rotc_ref[1]
    tables = {}
    for f in (True, False):
        for bk in range(RET_SUBS):
            a = srcs[f][3][bk]
            tables[f, bk] = (a[0:1] * cos_i - a[1:2] * sin_i, a[2:3] * cos_i + a[3:4] * sin_i)

    def rot(x, f, bk):
        cos_t, sin_t = tables[f, bk]
        return x * cos_t + pltpu.roll(x, dh // 2, 1) * sin_t

    lg = [_log_sigmoid(jnp.full((1, 1), logit_ref[r], F32)) for r in range(RET_STREAMS)]
    q = [rot(srcs[f][0][rows(bk), hs(h)], f, bk) for _, f, h, _, bk in items]
    k = [rot(srcs[f][1][rows(bk), hs(h)], f, bk) * dh ** -0.5 for _, f, h, _, bk in items]
    v = [srcs[f][2][rows(bk), hs(h)].astype(BF16) for _, f, h, _, bk in items]
    dmat = [jnp.where(rel[f] >= 0, jnp.exp(lg[r] * jnp.maximum(rel[f], 0.0)), 0.0)
            for r, (f, _) in enumerate(streams)]
    scores = [(_nt_dot(q[i].astype(BF16), k[i].astype(BF16)) * dmat[items[i][0]]).astype(BF16) for i in rng]
    o_in = [jnp.dot(scores[i], v[i], preferred_element_type=F32) for i in rng]
    q_dec = [(q[i] * jnp.exp(lg[items[i][0]] * (pos[items[i][1]] + 1.0))).astype(BF16) for i in rng]
    k_dec = [(k[i] * jnp.exp(lg[items[i][0]] * (c - 1.0 - pos[items[i][1]]))).astype(BF16) for i in rng]
    kv = [_tn_dot(k_dec[i], v[i]) for i in rng]
    chunk_dec = [jnp.exp(lg[r] * c) for r in range(RET_STREAMS)]

    s = [s_scr[r] for r in range(RET_STREAMS)]
    for j in range(RET_SUBS):
        idx = [i for i in rng if items[i][3] == j]
        o = [o_in[i] + jnp.dot(q_dec[i], s[items[i][0]].astype(BF16), preferred_element_type=F32) for i in idx]
        s = [s[items[i][0]] * chunk_dec[items[i][0]] + kv[i] for i in idx]
        for i, oi in zip(idx, o):
            _, f, h, _, bk = items[i]
            (of_ref if f else ob_ref)[rows(bk), hs(h)] = oi
    for r in range(RET_STREAMS):
        s_scr[r] = s[r]

    @pl.when(n == pl.num_programs(1) - 1)
    def _():
        sf_ref[...] = s_scr[...]


def retention_scan(p, col0, decay_logit, rot_chunk, rot_in, s0):
    b, l, _ = p.shape
    c, dh, w = RET_CHUNK * RET_SUBS, RET_DIM, RET_WIDTH
    n = l // c
    cb = col0 // w
    fcol = lambda off: (lambda bi, ni, s: (bi, ni, cb + off))
    bcol = lambda off: (lambda bi, ni, s: (bi, n - 1 - ni, cb + off))
    rot = lambda m: pl.BlockSpec((RET_SUBS, 4, dh), m)
    st = lambda bi, ni, s: (bi, 0, 0, 0)
    tok = lambda col: [pl.BlockSpec((None, c, w), col(j)) for j in range(3)]
    return pl.pallas_call(
        _ret_scan_body,
        grid_spec=pltpu.PrefetchScalarGridSpec(
            num_scalar_prefetch=1,
            grid=(b, n),
            in_specs=tok(fcol) + [rot(lambda bi, ni, s: (ni, 0, 0))]
            + tok(bcol) + [rot(lambda bi, ni, s: (n - 1 - ni, 0, 0))]
            + [pl.BlockSpec((2, RET_CHUNK, dh), lambda bi, ni, s: (0, 0, 0)),
               pl.BlockSpec((None, RET_STREAMS, dh, dh), st)],
            out_specs=[pl.BlockSpec((None, c, w), lambda bi, ni, s: (bi, ni, 0)),
                       pl.BlockSpec((None, c, w), lambda bi, ni, s: (bi, n - 1 - ni, 0)),
                       pl.BlockSpec((None, RET_STREAMS, dh, dh), st)],
            scratch_shapes=[pltpu.VMEM((RET_STREAMS, dh, dh), F32)]),
        out_shape=[jax.ShapeDtypeStruct((b, l, w), F32), jax.ShapeDtypeStruct((b, l, w), F32),
                   jax.ShapeDtypeStruct((b, RET_STREAMS, dh, dh), F32)],
        compiler_params=_cparams(("parallel", "arbitrary")),
        name="retention_scan",
    )(decay_logit.reshape(-1), p, p, p, rot_chunk, p, p, p, rot_chunk, rot_in, s0)


def _ret_final_body(of_ref, ob_ref, g_ref, w_ref, y_ref):
    for h in range(RET_HEADS):
        sl = slice(h * RET_DIM, (h + 1) * RET_DIM)
        o = of_ref[:, sl] + ob_ref[:, sl]
        mu = jnp.mean(o, axis=-1, keepdims=True)
        var = jnp.mean(jnp.square(o - mu), axis=-1, keepdims=True)
        y = (o - mu) * lax.rsqrt(var + EPS) * w_ref[:, sl]
        g = g_ref[:, sl]
        y_ref[:, sl] = (y * (g * jax.nn.sigmoid(g))).astype(y_ref.dtype)


def retention_finalize(o_f, o_b, p, gate_col, gn_w, tl):
    b, l, w = o_f.shape
    tok = lambda bi, i: (bi, i, 0)
    return pl.pallas_call(
        _ret_final_body,
        grid=(b, l // tl),
        in_specs=[pl.BlockSpec((None, tl, w), tok),
                  pl.BlockSpec((None, tl, w), tok),
                  pl.BlockSpec((None, tl, w), lambda bi, i: (bi, i, gate_col // w)),
                  pl.BlockSpec((1, w), lambda bi, i: (0, 0))],
        out_specs=pl.BlockSpec((None, tl, w), tok),
        out_shape=jax.ShapeDtypeStruct((b, l, w), BF16),
        compiler_params=_cparams(("parallel", "parallel")),
        name="retention_finalize",
    )(o_f, o_b, p, gn_w.reshape(1, w))


def retention_rope_tables(l):
    inv = ROPE_THETA ** (-np.linspace(0.0, 1.0, RET_DIM // 2))
    inv2 = np.concatenate([inv, inv])
    sign = np.concatenate([-np.ones(RET_DIM // 2), np.ones(RET_DIM // 2)])
    a = (RET_CHUNK * np.arange(l // RET_CHUNK, dtype=np.float64))[:, None] * inv2
    b = np.arange(RET_CHUNK, dtype=np.float64)[:, None] * inv2
    chunk = np.stack([np.cos(a), np.sin(a), sign * np.sin(a), sign * np.cos(a)], axis=1)
    return chunk.astype(np.float32), np.stack([np.cos(b), np.sin(b)]).astype(np.float32)


def retention_pallas(p_lat, p_ctx, decay_logit, gn_w, need_ctx, col0=0):
    b, l, _ = p_lat.shape
    n_ctx = p_ctx.shape[1]
    gate_col = col0 + 3 * RET_WIDTH
    s0 = jnp.zeros((b, RET_STREAMS, RET_DIM, RET_DIM), F32)
    oc_f, oc_b, s_ctx = retention_scan(p_ctx, col0, decay_logit, *retention_rope_tables(n_ctx), s0)
    ol_f, ol_b, _ = retention_scan(p_lat, col0, decay_logit, *retention_rope_tables(l), s_ctx)
    out_lat = retention_finalize(ol_f, ol_b, p_lat, gate_col, gn_w, 1024)
    out_ctx = retention_finalize(oc_f, oc_b, p_ctx, gate_col, gn_w, n_ctx) if need_ctx else None
    return out_lat, out_ctx


DN_BLOCK = 128
DN_STREAMS = 2 * DN_HEADS
DN_SUBS = 2


def _shift_rows(x, prev_row, next_row):
    n = x.shape[0]
    r = lax.broadcasted_iota(jnp.int32, (n, 1), 0)
    x_prev = jnp.where(r == 0, prev_row, pltpu.roll(x, 1, 0))
    x_next = jnp.where(r == n - 1, next_row, pltpu.roll(x, n - 1, 0))
    return x_prev, x_next


def _conv3(x_ref, xp_ref, xn_ref, w_ref):
    i = pl.program_id(1)
    prev_row = jnp.where(i > 0, xp_ref[7:8, :], 0.0)
    next_row = jnp.where(i < pl.num_programs(1) - 1, xn_ref[0:1, :], 0.0)
    x = x_ref[...]
    x_prev, x_next = _shift_rows(x, prev_row, next_row)
    return x_prev * w_ref[0:1, :] + x * w_ref[1:2, :] + x_next * w_ref[2:3, :]


def _dn_prep_body(x_ref, xp_ref, xn_ref, ab_ref, cw_ref, alog_ref, dtb_ref,
                  q_ref, k_ref, v_ref, gb_ref, gam_ref, gamt_ref):
    y = _conv3(x_ref, xp_ref, xn_ref, cw_ref)
    y = y * jax.nn.sigmoid(y)
    for h in range(DN_HEADS):
        for part, (ref, mul) in enumerate(((q_ref, DN_DIM ** -0.5), (k_ref, 1.0))):
            sl = slice(part * DN_WIDTH + h * DN_DIM, part * DN_WIDTH + (h + 1) * DN_DIM)
            t = y[:, sl]
            t = t * lax.rsqrt(jnp.sum(t * t, axis=-1, keepdims=True) + EPS)
            ref[:, h * DN_DIM:(h + 1) * DN_DIM] = t * mul if mul != 1.0 else t
    v_ref[...] = y[:, 2 * DN_WIDTH:]

    ab = ab_ref[...]
    z = ab + dtb_ref[...]
    softplus = jnp.maximum(z, 0.0) + jnp.log(1.0 + jnp.exp(-jnp.abs(z)))
    g = -jnp.exp(alog_ref[...]) * softplus
    lane = lax.broadcasted_iota(jnp.int32, ab.shape, 1)
    gb_ref[...] = jnp.where(lane < DN_STREAMS, g, jax.nn.sigmoid(ab))

    c = DN_BLOCK
    ii = lax.broadcasted_iota(jnp.int32, (c, c), 0)
    jj = lax.broadcasted_iota(jnp.int32, (c, c), 1)
    tri_f = (ii >= jj).astype(F32)
    tri_b = (ii <= jj).astype(F32)
    lane_c = lax.broadcasted_iota(jnp.int32, (c, ab.shape[1]), 1)
    for n in range(x_ref.shape[0] // c):
        gc = g[n * c:(n + 1) * c, :]
        cf = jnp.dot(tri_f, gc, preferred_element_type=F32, precision=lax.Precision.HIGHEST)
        cb = jnp.dot(tri_b, gc, preferred_element_type=F32, precision=lax.Precision.HIGHEST)
        gam = jnp.where(lane_c < DN_HEADS, cf, cb)
        gam_ref[n * c:(n + 1) * c, :] = gam
        gamt_ref[n] = gam.T[:DN_STREAMS, :]


def dn_prep(p, ab_col, conv_w, a_log, dt_bias, tl):
    b, l, _ = p.shape
    c3 = 3 * DN_WIDTH
    nblk8 = l // 8
    lanes = 128
    pad8 = lambda a: jnp.pad(a.reshape(1, DN_STREAMS).astype(F32), ((0, 0), (0, lanes - DN_STREAMS)))
    tok = lambda bi, i: (bi, i, 0)
    return pl.pallas_call(
        _dn_prep_body,
        grid=(b, l // tl),
        in_specs=[pl.BlockSpec((None, tl, c3), tok),
                  pl.BlockSpec((None, 8, c3), lambda bi, i: (bi, jnp.maximum(i * (tl // 8) - 1, 0), 0)),
                  pl.BlockSpec((None, 8, c3), lambda bi, i: (bi, jnp.minimum((i + 1) * (tl // 8), nblk8 - 1), 0)),
                  pl.BlockSpec((None, tl, lanes), lambda bi, i: (bi, i, ab_col // lanes)),
                  pl.BlockSpec((3, c3), lambda bi, i: (0, 0)),
                  pl.BlockSpec((1, lanes), lambda bi, i: (0, 0)),
                  pl.BlockSpec((1, lanes), lambda bi, i: (0, 0))],
        out_specs=[pl.BlockSpec((None, tl, DN_WIDTH), tok)] * 3
        + [pl.BlockSpec((None, tl, lanes), tok)] * 2
        + [pl.BlockSpec((None, tl // DN_BLOCK, DN_STREAMS, DN_BLOCK), lambda bi, i: (bi, i, 0, 0))],
        out_shape=[jax.ShapeDtypeStruct((b, l, DN_WIDTH), F32)] * 3
        + [jax.ShapeDtypeStruct((b, l, lanes), F32)] * 2
        + [jax.ShapeDtypeStruct((b, l // DN_BLOCK, DN_STREAMS, DN_BLOCK), F32)],
        compiler_params=_cparams(("parallel", "parallel")),
        name="dn_prep",
    )(p, p, p, p, conv_w, pad8(a_log), pad8(dt_bias))


def _dot16(a, b):
    return jnp.dot(a.astype(BF16), b.astype(BF16), preferred_element_type=F32)


def _unit_triangular_inverses(ms, ii, jj):
    c = ms[0].shape[0]
    eye = (ii == jj).astype(F32)
    pair = (ii // 2) == (jj // 2)
    ts = [eye - jnp.where(pair, m, 0.0) for m in ms]
    s = 2
    while s < c:
        sub = ((ii // (2 * s)) == (jj // (2 * s))) & ((ii // s) != (jj // s))
        t16 = [t.astype(BF16) for t in ts]
        xs = [_dot16(jnp.where(sub, m, 0.0), t) for m, t in zip(ms, t16)]
        ts = [t - _dot16(tb, x) for t, tb, x in zip(ts, t16, xs)]
        s *= 2
    return ts


def _dn_scan_body(qf_ref, kf_ref, vf_ref, gbf_ref, gamf_ref, gamtf_ref,
                  qb_ref, kb_ref, vb_ref, gbb_ref, gamb_ref, gamtb_ref, s0_ref,
                  of_ref, ob_ref, sf_ref, s_scr):
    n = pl.program_id(1)

    @pl.when(n == 0)
    def _():
        s_scr[...] = s0_ref[...]

    c, dh = DN_BLOCK, DN_DIM
    ii = lax.broadcasted_iota(jnp.int32, (c, c), 0)
    jj = lax.broadcasted_iota(jnp.int32, (c, c), 1)
    incl = {True: ii >= jj, False: ii <= jj}
    strict = {True: ii > jj, False: ii < jj}
    srcs = {True: (qf_ref, kf_ref, vf_ref, gbf_ref, gamf_ref, gamtf_ref),
            False: (qb_ref, kb_ref, vb_ref, gbb_ref, gamb_ref, gamtb_ref)}
    streams = [(d == 0, h) for d in range(2) for h in range(DN_HEADS)]
    hs = lambda h: slice(h * dh, (h + 1) * dh)
    items = [(r, f, h, j, (j if f else DN_SUBS - 1 - j)) for j in range(DN_SUBS) for r, (f, h) in enumerate(streams)]
    rows = lambda blk: slice(blk * c, (blk + 1) * c)
    rng = range(len(items))

    q = [srcs[f][0][rows(bk), hs(h)] for _, f, h, _, bk in items]
    k = [srcs[f][1][rows(bk), hs(h)] for _, f, h, _, bk in items]
    v = [srcs[f][2][rows(bk), hs(h)] for _, f, h, _, bk in items]
    beta = [srcs[f][3][rows(bk), DN_STREAMS + r:DN_STREAMS + r + 1] for r, f, _, _, bk in items]
    gcol = [srcs[f][4][rows(bk), r:r + 1] for r, f, _, _, bk in items]
    grow = [srcs[f][5][bk, r:r + 1, :] for r, f, _, _, bk in items]
    g_last = [gcol[i][c - 1:c, :] if items[i][1] else gcol[i][0:1, :] for i in rng]

    e = [jnp.exp(jnp.where(incl[items[i][1]], gcol[i] - grow[i], 0.0)) for i in rng]
    kb = [k[i] * beta[i] for i in rng]
    k16 = [k[i].astype(BF16) for i in rng]
    m = [_nt_dot(kb[i].astype(BF16), k16[i]) * jnp.where(strict[items[i][1]], e[i], 0.0) for i in rng]
    attn = [(_nt_dot(q[i].astype(BF16), k16[i]) * jnp.where(incl[items[i][1]], e[i], 0.0)).astype(BF16) for i in rng]
    t = _unit_triangular_inverses(m, ii, jj)
    eg = [jnp.exp(gcol[i]) for i in rng]
    sol = [_dot16(t[i], jnp.concatenate([v[i] * beta[i], kb[i] * eg[i]], axis=-1)) for i in rng]
    u = [sol[i][:, :dh] for i in rng]
    w16 = [sol[i][:, dh:].astype(BF16) for i in rng]
    q_dec = [(q[i] * eg[i]).astype(BF16) for i in rng]
    k_dec = [(k[i] * jnp.exp(g_last[i] - gcol[i])).astype(BF16) for i in rng]
    blk_dec = [jnp.exp(g_last[i]) for i in rng]

    s = [s_scr[r] for r in range(DN_STREAMS)]
    for j in range(DN_SUBS):
        idx = [i for i in rng if items[i][3] == j]
        s16 = [s[r].astype(BF16) for r in range(DN_STREAMS)]
        v_new = [(u[i] - jnp.dot(w16[i], s16[items[i][0]], preferred_element_type=F32)).astype(BF16) for i in idx]
        o = [jnp.dot(q_dec[i], s16[items[i][0]], preferred_element_type=F32)
             + jnp.dot(attn[i], vn, preferred_element_type=F32) for i, vn in zip(idx, v_new)]
        s = [s[items[i][0]] * blk_dec[i] + _tn_dot(k_dec[i], vn) for i, vn in zip(idx, v_new)]
        for i, oi in zip(idx, o):
            _, f, h, _, bk = items[i]
            (of_ref if f else ob_ref)[rows(bk), hs(h)] = oi
    for r in range(DN_STREAMS):
        s_scr[r] = s[r]

    @pl.when(n == pl.num_programs(1) - 1)
    def _():
        sf_ref[...] = s_scr[...]


def dn_scan(q, k, v, gb, gam, gamt, s0):
    b, l, w = q.shape
    c = DN_BLOCK * DN_SUBS
    n = l // c
    lanes = gb.shape[-1]
    f3 = lambda bi, ni: (bi, ni, 0)
    b3 = lambda bi, ni: (bi, n - 1 - ni, 0)
    f4 = lambda bi, ni: (bi, ni, 0, 0)
    b4 = lambda bi, ni: (bi, n - 1 - ni, 0, 0)
    st = lambda bi, ni: (bi, 0, 0, 0)

    def specs(m3, m4):
        return [pl.BlockSpec((None, c, w), m3)] * 3 + [pl.BlockSpec((None, c, lanes), m3)] * 2 + [
            pl.BlockSpec((None, DN_SUBS, DN_STREAMS, DN_BLOCK), m4)]

    return pl.pallas_call(
        _dn_scan_body,
        grid=(b, n),
        in_specs=specs(f3, f4) + specs(b3, b4) + [pl.BlockSpec((None, DN_STREAMS, DN_DIM, DN_DIM), st)],
        out_specs=[pl.BlockSpec((None, c, w), f3), pl.BlockSpec((None, c, w), b3),
                   pl.BlockSpec((None, DN_STREAMS, DN_DIM, DN_DIM), st)],
        out_shape=[jax.ShapeDtypeStruct((b, l, w), F32), jax.ShapeDtypeStruct((b, l, w), F32),
                   jax.ShapeDtypeStruct((b, DN_STREAMS, DN_DIM, DN_DIM), F32)],
        scratch_shapes=[pltpu.VMEM((DN_STREAMS, DN_DIM, DN_DIM), F32)],
        compiler_params=_cparams(("parallel", "arbitrary")),
        name="dn_scan",
    )(q, k, v, gb, gam, gamt, q, k, v, gb, gam, gamt, s0)


def _dn_final_body(of_ref, ob_ref, g_ref, w_ref, y_ref):
    for h in range(DN_HEADS):
        sl = slice(h * DN_DIM, (h + 1) * DN_DIM)
        o = of_ref[:, sl] + ob_ref[:, sl]
        y = _head_rms(o, w_ref[...])
        g = g_ref[:, sl]
        y_ref[:, sl] = (y * (g * jax.nn.sigmoid(g))).astype(y_ref.dtype)


def dn_finalize(o_f, o_b, p, gate_col, norm_w, tl):
    b, l, w = o_f.shape
    tok = lambda bi, i: (bi, i, 0)
    return pl.pallas_call(
        _dn_final_body,
        grid=(b, l // tl),
        in_specs=[pl.BlockSpec((None, tl, w), tok), pl.BlockSpec((None, tl, w), tok),
                  pl.BlockSpec((None, tl, w), lambda bi, i: (bi, i, gate_col // w)),
                  pl.BlockSpec((1, DN_DIM), lambda bi, i: (0, 0))],
        out_specs=pl.BlockSpec((None, tl, w), tok),
        out_shape=jax.ShapeDtypeStruct((b, l, w), BF16),
        compiler_params=_cparams(("parallel", "parallel")),
        name="dn_finalize",
    )(o_f, o_b, p, norm_w.reshape(1, DN_DIM))


def deltanet_pallas(p_lat, p_ctx, conv_w, a_log, dt_bias, norm_w, need_ctx, ab_col=3072, gate_col=1536):
    b = p_lat.shape[0]
    n_ctx = p_ctx.shape[1]
    s0 = jnp.zeros((b, DN_STREAMS, DN_DIM, DN_DIM), F32)
    fc = dn_prep(p_ctx, ab_col, conv_w, a_log, dt_bias, n_ctx)
    oc_f, oc_b, s_ctx = dn_scan(*fc, s0)
    fl = dn_prep(p_lat, ab_col, conv_w, a_log, dt_bias, 1024)
    ol_f, ol_b, _ = dn_scan(*fl, s_ctx)
    out_lat = dn_finalize(ol_f, ol_b, p_lat, gate_col, norm_w, 1024)
    out_ctx = dn_finalize(oc_f, oc_b, p_ctx, gate_col, norm_w, n_ctx) if need_ctx else None
    return out_lat, out_ctx


def _hy_factors(n):
    n2 = 256 if n >= 32768 else 128
    return n // n2, n2


def _bf16_const(m):
    return m.astype(ml_dtypes.bfloat16)


def _hy_constants(l):
    n = 2 * l
    n1, n2 = _hy_factors(n)
    a1 = 2.0 * np.pi * np.outer(np.arange(n1), np.arange(n1)) / n1
    c1, s1 = np.cos(a1), np.sin(a1)
    h = n1 // 2
    rows_fwd = np.block([[c1[:, :h], s1[:, :h]], [-s1[:, :h], c1[:, :h]]])
    rows_taps = np.concatenate([c1, -s1], axis=0)
    rows_inv = np.block([[c1[:h], -s1[:h]], [s1[:h], c1[:h]]]) / n
    a2 = 2.0 * np.pi * np.outer(np.arange(n2), np.arange(n2)) / n2
    c2, s2 = np.cos(a2), np.sin(a2)
    slab_fwd = np.block([[c2, s2], [-s2, c2]])
    slab_inv = np.block([[c2, -s2], [s2, c2]])
    th = 2.0 * np.pi * np.outer(np.arange(n1), np.arange(n2)) / n
    lanes = lambda t: jnp.broadcast_to(jnp.asarray(t, F32)[:, :, None], (n1, n2, 128))
    return dict(n1=n1, n2=n2, rows_fwd=_bf16_const(rows_fwd), rows_taps=_bf16_const(rows_taps),
                rows_inv=_bf16_const(rows_inv), slab_fwd=_bf16_const(slab_fwd), slab_inv=_bf16_const(slab_inv),
                tw_cos=lanes(np.cos(th)), tw_sin=lanes(np.sin(th)))


def _hy_prep_body(x_ref, xp_ref, xn_ref, w_ref, v_ref, x1_ref, x2_ref):
    y = _conv3(x_ref, xp_ref, xn_ref, w_ref)
    v_ref[...] = y[:, :HY_CH]
    x1_ref[...] = y[:, HY_CH:2 * HY_CH]
    x2_ref[...] = y[:, 2 * HY_CH:]


def hy_prep(p, col0, conv_w, tl):
    b, l, _ = p.shape
    c3 = 3 * HY_CH
    nblk8 = l // 8
    cb = col0 // c3
    tok = lambda bi, i: (bi, i, 0)
    return pl.pallas_call(
        _hy_prep_body,
        grid=(b, l // tl),
        in_specs=[pl.BlockSpec((None, tl, c3), lambda bi, i: (bi, i, cb)),
                  pl.BlockSpec((None, 8, c3), lambda bi, i: (bi, jnp.maximum(i * (tl // 8) - 1, 0), cb)),
                  pl.BlockSpec((None, 8, c3), lambda bi, i: (bi, jnp.minimum((i + 1) * (tl // 8), nblk8 - 1), cb)),
                  pl.BlockSpec((3, c3), lambda bi, i: (0, 0))],
        out_specs=[pl.BlockSpec((None, tl, HY_CH), tok)] * 3,
        out_shape=[jax.ShapeDtypeStruct((b, l, HY_CH), F32)] * 3,
        compiler_params=_cparams(("parallel", "parallel")),
        name="hy_prep",
    )(p, p, p, conv_w)


def _split_bf16(a):
    hi = a.astype(BF16)
    return hi, (a - hi.astype(F32)).astype(BF16)


def _hy_filter_body(t_ref, zt_ref, w1t_ref, b1_ref, w2t_ref, b2_ref, w3_ref, rates_ref, *out_refs, zero_row):
    taps_refs, sum_ref = out_refs[:-1], out_refs[-1]
    i = pl.program_id(0)
    hp = lax.Precision.HIGHEST
    h = jnp.sin(HY_SIN_FREQ * (jnp.dot(w1t_ref[...], zt_ref[...], preferred_element_type=F32, precision=hp)
                               + b1_ref[...]))
    h = jnp.sin(HY_SIN_FREQ * (jnp.dot(w2t_ref[...], h, preferred_element_type=F32, precision=hp) + b2_ref[...]))
    h_hi, h_lo = _split_bf16(h)
    w_hi, w_lo = _split_bf16(w3_ref[...])
    h = _tn_dot(h_hi, w_hi) + (_tn_dot(h_hi, w_lo) + _tn_dot(h_lo, w_hi))
    h = h * jnp.exp(-t_ref[...] * rates_ref[...])

    @pl.when(i == 0)
    def _():
        sum_ref[...] = jnp.zeros_like(sum_ref)

    sum_ref[...] += jnp.sum(jnp.abs(h), axis=0, keepdims=True)
    tr = h.shape[0]
    row = i * tr + lax.broadcasted_iota(jnp.int32, (tr, 1), 0)
    h = jnp.where(row == zero_row, 0.0, h)
    for o in range(HY_ORDER):
        taps_refs[o][...] = h[:, o * HY_CH:(o + 1) * HY_CH]


def hy_filter_taps(l, w1, b1, w2, b2, w3, tr=512):
    n = 2 * l
    r = np.arange(n)
    pos = np.where(r < l, r, n - r).astype(np.float64)
    pos[l] = 0.0
    t = pos / max(l - 1, 1)
    bands = np.linspace(1e-4, HY_BANDS - 1, HY_BANDS)
    phase = (2.0 * math.pi / l) * pos[:, None] * bands[None, :]
    kf = 32
    z = np.concatenate([t[:, None], np.cos(phase), -np.sin(phase)], axis=-1)
    zt = np.pad(z, ((0, 0), (0, kf - z.shape[1]))).T.astype(np.float32)
    t_col = t[:, None].astype(np.float32)
    w1t = jnp.pad(w1.astype(F32), ((0, kf - w1.shape[0]), (0, 0))).T
    hid = w1.shape[1]
    w3d = w3.astype(F32).reshape(hid, HY_ORDER, 2, HY_CH).transpose(2, 0, 1, 3).reshape(2, hid, HY_ORDER * HY_CH)
    rates = jnp.abs(jnp.linspace(math.log(HY_TARGET) / HY_DECAY_LONG, math.log(HY_TARGET) / HY_DECAY_SHORT,
                                 HY_CH, dtype=F32))
    rates = jnp.tile(rates, HY_ORDER).reshape(1, HY_ORDER * HY_CH)
    fixed = lambda i: (0, 0)
    return pl.pallas_call(
        functools.partial(_hy_filter_body, zero_row=l),
        grid=(n // tr,),
        in_specs=[pl.BlockSpec((tr, 1), lambda i: (i, 0)), pl.BlockSpec((kf, tr), lambda i: (0, i)),
                  pl.BlockSpec((hid, kf), fixed), pl.BlockSpec((hid, 1), fixed),
                  pl.BlockSpec((hid, hid), fixed), pl.BlockSpec((hid, 1), fixed),
                  pl.BlockSpec((None, hid, HY_ORDER * HY_CH), lambda i: (i // (l // tr), 0, 0)),
                  pl.BlockSpec((1, HY_ORDER * HY_CH), fixed)],
        out_specs=[pl.BlockSpec((tr, HY_CH), lambda i: (i, 0))] * HY_ORDER
        + [pl.BlockSpec((1, HY_ORDER * HY_CH), fixed)],
        out_shape=[jax.ShapeDtypeStruct((n, HY_CH), F32)] * HY_ORDER
        + [jax.ShapeDtypeStruct((1, HY_ORDER * HY_CH), F32)],
        compiler_params=_cparams(("arbitrary",)),
        name="hy_filter_taps",
    )(t_col, zt, w1t, b1.astype(F32).reshape(hid, 1), w2.astype(F32).T, b2.astype(F32).reshape(hid, 1), w3d, rates)


HY_ROWS_GROUP = 8


def _hy_rows_body(*refs, mode):
    if mode == "taps":
        m_ref, x_ref, sum_ref, o_ref = refs
        scale = 1.0 / (sum_ref[...] + EPS)
    elif mode == "fwd":
        m_ref, x_ref, o_ref = refs
    else:
        m_ref, x_ref, xn_ref, z_ref, skip_ref, o_ref = refs
    for j in range(HY_ROWS_GROUP):
        x = jnp.concatenate([x_ref[0, :, j, :], x_ref[1, :, j, :]], axis=0)
        if mode == "taps":
            x = x * scale
        y = _dot16(m_ref[...], x)
        half = y.shape[0] // 2
        for bi in range(2):
            yb = y[bi * half:(bi + 1) * half]
            if mode == "inv":
                yb = xn_ref[bi, :, j, :] * (yb + z_ref[bi, :, j, :] * skip_ref[...])
            o_ref[bi, :, j, :] = yb


def hy_rows_pass(mode, mat, x, extra=()):
    n2, c = x.shape[-2:]
    g = HY_ROWS_GROUP
    blk = lambda rows: pl.BlockSpec((2, rows, g, c), lambda j: (0, 0, j, 0))
    mspec = _resident(mat.shape)
    vec = pl.BlockSpec((1, c), lambda j: (0, 0))
    if mode == "taps":
        n1 = 2 * x.shape[1]
        in_specs = [mspec, blk(n1 // 2), vec]
        out_rows = n1
    elif mode == "fwd":
        n1 = 2 * x.shape[1]
        in_specs = [mspec, blk(n1 // 2)]
        out_rows = n1
    else:
        n1 = x.shape[1]
        in_specs = [mspec, blk(n1), blk(n1 // 2), blk(n1 // 2), vec]
        out_rows = n1 // 2
    return pl.pallas_call(
        functools.partial(_hy_rows_body, mode=mode),
        grid=(n2 // g,),
        in_specs=in_specs,
        out_specs=blk(out_rows),
        out_shape=jax.ShapeDtypeStruct((2, out_rows, n2, c), F32),
        compiler_params=_cparams(("parallel",)),
        name="hy_rows_" + mode,
    )(mat, x, *extra)


def _hy_slab_body(*refs, with_filter):
    if with_filter:
        f_ref, i_ref, tc_ref, ts_ref, a_ref, h_ref, o_ref = refs
    else:
        f_ref, tc_ref, ts_ref, a_ref, o_ref = refs
    reps = a_ref.shape[-1] // tc_ref.shape[-1]
    n2 = a_ref.shape[2]
    for s in range(a_ref.shape[1]):
        tc = jnp.concatenate([tc_ref[s]] * reps, axis=1)
        ts = jnp.concatenate([ts_ref[s]] * reps, axis=1)
        ar, ai = a_ref[0, s], a_ref[1, s]
        x = _dot16(f_ref[...], jnp.concatenate([ar * tc + ai * ts, ai * tc - ar * ts], axis=0))
        xr, xi = x[:n2], x[n2:]
        if not with_filter:
            o_ref[0, s] = xr.astype(o_ref.dtype)
            o_ref[1, s] = xi.astype(o_ref.dtype)
            continue
        hr, hi = h_ref[0, s].astype(F32), h_ref[1, s].astype(F32)
        y = _dot16(i_ref[...], jnp.concatenate([xr * hr - xi * hi, xr * hi + xi * hr], axis=0))
        yr, yi = y[:n2], y[n2:]
        o_ref[0, s] = yr * tc - yi * ts
        o_ref[1, s] = yi * tc + yr * ts


HY_SLABS_PER_STEP = 2


def hy_slab_pass(consts, a, h=None):
    _, n1, n2, c = a.shape
    kb = HY_SLABS_PER_STEP
    fwd = consts["slab_fwd"]
    mspec = _resident(fwd.shape)
    tw = pl.BlockSpec((kb, n2, 128), lambda k: (k, 0, 0))
    slab = pl.BlockSpec((2, kb, n2, c), lambda k: (0, k, 0, 0))
    if h is None:
        args, in_specs = [fwd, consts["tw_cos"], consts["tw_sin"], a], [mspec, tw, tw, slab]
    else:
        args = [fwd, consts["slab_inv"], consts["tw_cos"], consts["tw_sin"], a, h]
        in_specs = [mspec, mspec, tw, tw, slab, slab]
    return pl.pallas_call(
        functools.partial(_hy_slab_body, with_filter=h is not None),
        grid=(n1 // kb,),
        in_specs=in_specs,
        out_specs=slab,
        out_shape=jax.ShapeDtypeStruct(a.shape, F32 if h is not None else BF16),
        compiler_params=_cparams(("parallel",)),
        name="hy_slab_conv" if h is not None else "hy_slab_fwd",
    )(*args)


def hyena_pallas(p, conv_w, w1, b1, w2, b2, w3, skip, col0=RET_COLS):
    b, l, _ = p.shape
    assert b == 2, "the two batch rows are packed as one complex signal"
    c = HY_CH
    consts = _hy_constants(l)
    n1, n2 = consts["n1"], consts["n2"]
    *taps, sums = hy_filter_taps(l, w1, b1, w2, b2, w3)
    spectra = []
    for o in range(HY_ORDER):
        a_h = hy_rows_pass("taps", consts["rows_taps"], taps[o].reshape(2, n1 // 2, n2, c),
                           extra=(sums[:, o * c:(o + 1) * c],))
        spectra.append(hy_slab_pass(consts, a_h))
    split = lambda t: t.reshape(2, n1 // 2, n2, c)
    v, x1, x2 = hy_prep(p, col0, conv_w, 1024)
    z = split(v)
    for o, xn in enumerate((x1, x2)):
        a = hy_rows_pass("fwd", consts["rows_fwd"], z)
        bm = hy_slab_pass(consts, a, spectra[o])
        z = hy_rows_pass("inv", consts["rows_inv"], bm,
                         extra=(split(xn), z, skip[o].astype(F32).reshape(1, c)))
    return z.reshape(b, l, c)


def rmsnorm(x, w):
    xf = x.astype(F32)
    y = xf * lax.rsqrt(jnp.mean(xf * xf, axis=-1, keepdims=True) + EPS)
    return (y * w.astype(F32)).astype(x.dtype)


def l2norm(x):
    return x * lax.rsqrt(jnp.sum(x * x, axis=-1, keepdims=True) + EPS)


def rotate_half(x, ang):
    cos = jnp.cos(ang)[:, None, :]
    sin = jnp.sin(ang)[:, None, :]
    x1, x2 = jnp.split(x, 2, axis=-1)
    return jnp.concatenate([x1 * cos - x2 * sin, x2 * cos + x1 * sin], axis=-1).astype(x.dtype)


def axial_angles(rows):
    m = SWA_DIM // 4
    inv = ROPE_THETA ** (-jnp.arange(m, dtype=F32) / m)
    row = jnp.repeat(jnp.arange(rows, dtype=F32), GRID_W)
    col = jnp.broadcast_to(jnp.arange(GRID_W, dtype=F32), (rows, GRID_W)).reshape(-1)
    return row[:, None] * inv, col[:, None] * inv


def axial_rope(x, ang_row, ang_col):
    h = x.shape[-1] // 2
    return jnp.concatenate([rotate_half(x[..., :h], ang_row), rotate_half(x[..., h:], ang_col)], axis=-1)


def short_conv(x, w):
    K, C = w.shape
    return lax.conv_general_dilated(x, w[:, None, :].astype(x.dtype), window_strides=(1,),
                                    padding=[(K // 2, K // 2)],
                                    dimension_numbers=('NWC', 'WIO', 'NWC'),
                                    feature_group_count=C)


def flip_time(t):
    return jnp.flip(t, axis=2)


def keep_time(t):
    return t


def delta_chunk_scan(q, k, v, g, beta, s0):
    B, H, L, dk = q.shape
    dv = v.shape[-1]
    C = DN_CHUNK
    N = L // C
    q = q.reshape(B, H, N, C, dk)
    k = k.reshape(B, H, N, C, dk)
    v = v.reshape(B, H, N, C, dv)
    g = g.reshape(B, H, N, C)
    beta = beta.reshape(B, H, N, C)
    gam = jnp.cumsum(g, axis=-1)
    idx = jnp.arange(C)
    incl = idx[:, None] >= idx[None, :]
    strict = idx[:, None] > idx[None, :]
    e = jnp.exp(jnp.where(incl, gam[..., :, None] - gam[..., None, :], 0.0))
    dec_incl = jnp.where(incl, e, 0.0)
    dec_strict = jnp.where(strict, e, 0.0)
    kb = k * beta[..., None]
    m = jnp.einsum('bhnid,bhnjd->bhnij', kb, k) * dec_strict
    a = m + jnp.eye(C, dtype=F32)
    rhs = jnp.concatenate([v * beta[..., None], kb * jnp.exp(gam)[..., None]], axis=-1)
    sol = lax.linalg.triangular_solve(a, rhs, left_side=True, lower=True, unit_diagonal=True)
    u, w = sol[..., :dv], sol[..., dv:]
    attn = jnp.einsum('bhnid,bhnjd->bhnij', q, k) * dec_incl
    q_dec = q * jnp.exp(gam)[..., None]
    k_dec = k * jnp.exp(gam[..., -1:] - gam)[..., None]
    chunk_dec = jnp.exp(gam[..., -1])

    def step(S, xs):
        u_n, w_n, a_n, qd_n, kd_n, cd_n = xs
        v_new = u_n - jnp.einsum('bhck,bhkv->bhcv', w_n, S)
        o = jnp.einsum('bhck,bhkv->bhcv', qd_n, S) + jnp.einsum('bhij,bhjv->bhiv', a_n, v_new)
        S = S * cd_n[..., None, None] + jnp.einsum('bhck,bhcv->bhkv', kd_n, v_new)
        return S, o

    xs = tuple(jnp.moveaxis(t, 2, 0) for t in (u, w, attn, q_dec, k_dec, chunk_dec))
    S, o = lax.scan(step, s0, xs)
    return jnp.moveaxis(o, 0, 2).reshape(B, H, L, dv), S


def dn_features(p, conv_w, a_log, dt_bias):
    B, L, _ = p.shape
    qkv = jax.nn.silu(short_conv(p[..., :3 * DN_WIDTH], conv_w)).astype(F32)
    heads = lambda t: t.reshape(B, L, DN_HEADS, DN_DIM).transpose(0, 2, 1, 3)
    q, k, v = (heads(t) for t in jnp.split(qkv, 3, axis=-1))
    q = l2norm(q) * DN_DIM ** -0.5
    k = l2norm(k)
    gate = p[..., 3 * DN_WIDTH:4 * DN_WIDTH]
    ab = p[..., 4 * DN_WIDTH:].astype(F32).reshape(B, L, 2, 2, DN_HEADS)
    g = -jnp.exp(a_log.astype(F32)) * jax.nn.softplus(ab[:, :, 0] + dt_bias.astype(F32))
    beta = jax.nn.sigmoid(ab[:, :, 1])
    return q, k, v, g.transpose(2, 0, 3, 1), beta.transpose(2, 0, 3, 1), gate


def head_norm_gate(o, gate, w):
    B, H, L, dv = o.shape
    y = rmsnorm(o.transpose(0, 2, 1, 3), w)
    return (y * jax.nn.silu(gate.astype(F32).reshape(B, L, H, dv))).reshape(B, L, H * dv)


def gated_deltanet(p_lat, p_ctx, conv_w, a_log, dt_bias, norm_w, need_ctx):
    lat = dn_features(p_lat, conv_w, a_log, dt_bias)
    cx = dn_features(p_ctx, conv_w, a_log, dt_bias)
    B = p_lat.shape[0]
    o_lat, o_ctx = [], []
    for d in range(2):
        fl = flip_time if d == 1 else keep_time
        s0 = jnp.zeros((B, DN_HEADS, DN_DIM, DN_DIM), F32)
        oc, s_ctx = delta_chunk_scan(fl(cx[0]), fl(cx[1]), fl(cx[2]), fl(cx[3][d]), fl(cx[4][d]), s0)
        ol, _ = delta_chunk_scan(fl(lat[0]), fl(lat[1]), fl(lat[2]), fl(lat[3][d]), fl(lat[4][d]), s_ctx)
        o_lat.append(fl(ol))
        o_ctx.append(fl(oc))
    out_lat = head_norm_gate(o_lat[0] + o_lat[1], lat[5], norm_w)
    out_ctx = head_norm_gate(o_ctx[0] + o_ctx[1], cx[5], norm_w) if need_ctx else None
    return out_lat, out_ctx


def gqa_qkv(p, q_norm_w, k_norm_w):
    B, L, _ = p.shape
    nq, nk = SWA_HEADS * SWA_DIM, SWA_KV_HEADS * SWA_DIM
    q = rmsnorm(p[..., :nq].reshape(B, L, SWA_HEADS, SWA_DIM), q_norm_w)
    k = rmsnorm(p[..., nq:nq + nk].reshape(B, L, SWA_KV_HEADS, SWA_DIM), k_norm_w)
    v = p[..., nq + nk:].reshape(B, L, SWA_KV_HEADS, SWA_DIM)
    return q, k, v


def banded_attention(q, k, v, kc, vc, sink):
    B, L, Hq, d = q.shape
    Hkv = k.shape[2]
    G = Hq // Hkv
    W = SWA_BLOCK
    NB = L // W
    qb = q.reshape(B, NB, W, Hkv, G, d)

    def band(t):
        tp = jnp.pad(t.reshape(B, NB, W, Hkv, d), ((0, 0), (1, 1), (0, 0), (0, 0), (0, 0)))
        return jnp.concatenate([tp[:, :-2], tp[:, 1:-1], tp[:, 2:]], axis=2)

    kb, vb = band(k), band(v)
    scale = d ** -0.5
    s_loc = jnp.einsum('bnqhgd,bnkhd->bnhgqk', qb, kb).astype(F32) * scale
    s_ctx = jnp.einsum('bnqhgd,bchd->bnhgqc', qb, kc).astype(F32) * scale
    rel = (jnp.arange(3 * W) - W)[None, :] - jnp.arange(W)[:, None]
    kblk = jnp.arange(NB)[:, None] + jnp.arange(3 * W)[None, :] // W - 1
    valid = (jnp.abs(rel) <= SWA_WINDOW)[None] & ((kblk >= 0) & (kblk < NB))[:, None, :]
    s_loc = jnp.where(valid[None, :, None, None], s_loc, -jnp.inf)
    s_sink = jnp.broadcast_to(sink.astype(F32).reshape(Hkv, G, 1, 1), s_loc.shape[:-1] + (1,))
    prob = jax.nn.softmax(jnp.concatenate([s_loc, s_ctx, s_sink], axis=-1), axis=-1).astype(v.dtype)
    n_ctx = kc.shape[1]
    o = (jnp.einsum('bnhgqk,bnkhd->bnqhgd', prob[..., :3 * W], vb)
         + jnp.einsum('bnhgqc,bchd->bnqhgd', prob[..., 3 * W:3 * W + n_ctx], vc))
    return o.reshape(B, L, Hq * d)


def context_attention(qc, kc, vc, sink):
    B, Cn, Hq, d = qc.shape
    Hkv = kc.shape[2]
    G = Hq // Hkv
    q = qc.reshape(B, Cn, Hkv, G, d)
    s = jnp.einsum('bqhgd,bkhd->bhgqk', q, kc).astype(F32) * d ** -0.5
    s_sink = jnp.broadcast_to(sink.astype(F32).reshape(Hkv, G, 1, 1), s.shape[:-1] + (1,))
    prob = jax.nn.softmax(jnp.concatenate([s, s_sink], axis=-1), axis=-1)[..., :-1].astype(vc.dtype)
    return jnp.einsum('bhgqk,bkhd->bqhgd', prob, vc).reshape(B, Cn, Hq * d)


def window_gqa(p_lat, p_ctx, ang_row, ang_col, q_norm_w, k_norm_w, sink, need_ctx):
    ql, kl, vl = gqa_qkv(p_lat, q_norm_w, k_norm_w)
    ql = axial_rope(ql, ang_row, ang_col)
    kl = axial_rope(kl, ang_row, ang_col)
    qc, kc, vc = gqa_qkv(p_ctx, q_norm_w, k_norm_w)
    o_lat = banded_attention(ql, kl, vl, kc, vc, sink)
    o_ctx = context_attention(qc, kc, vc, sink) if need_ctx else None
    return o_lat, o_ctx


def ab_mixer(p_lat, p_ctx, ang_row, ang_col, dn_conv_w, dn_a_log, dn_dt_bias, dn_norm_w,
             q_norm_w, k_norm_w, sink, need_ctx):
    a_lat, a_ctx = gated_deltanet(p_lat[..., :DN_COLS], p_ctx[..., :DN_COLS], dn_conv_w, dn_a_log,
                                  dn_dt_bias, dn_norm_w, need_ctx)
    b_lat, b_ctx = window_gqa(p_lat[..., DN_COLS:], p_ctx[..., DN_COLS:], ang_row, ang_col,
                              q_norm_w, k_norm_w, sink, need_ctx)
    o_lat = jnp.concatenate([a_lat, b_lat.astype(F32)], axis=-1)
    o_ctx = jnp.concatenate([a_ctx, b_ctx.astype(F32)], axis=-1) if need_ctx else None
    return o_lat, o_ctx


def retention_chunk_scan(q, k, v, log_gamma, s0):
    B, H, L, dk = q.shape
    dv = v.shape[-1]
    C = RET_CHUNK
    N = L // C
    q = q.reshape(B, H, N, C, dk)
    k = k.reshape(B, H, N, C, dk)
    v = v.reshape(B, H, N, C, dv)
    pos = jnp.arange(C, dtype=F32)
    rel = pos[:, None] - pos[None, :]
    incl = rel >= 0
    lg = log_gamma[:, None, None]
    dmat = jnp.where(incl, jnp.exp(lg * jnp.where(incl, rel, 0.0)), 0.0)
    scores = jnp.einsum('bhnid,bhnjd->bhnij', q, k) * dmat[:, None]
    o_in = jnp.einsum('bhnij,bhnje->bhnie', scores, v)
    q_dec = q * jnp.exp(log_gamma[:, None] * (pos + 1.0))[:, None, :, None]
    k_dec = k * jnp.exp(log_gamma[:, None] * (C - 1.0 - pos))[:, None, :, None]
    chunk_dec = jnp.exp(log_gamma * C)[None, :, None, None]

    def step(S, xs):
        qd, kd, vn = xs
        o = jnp.einsum('bhcd,bhde->bhce', qd, S)
        S = S * chunk_dec + jnp.einsum('bhcd,bhce->bhde', kd, vn)
        return S, o

    xs = tuple(jnp.moveaxis(t, 2, 0) for t in (q_dec, k_dec, v))
    S, o_x = lax.scan(step, s0, xs)
    o = o_in + jnp.moveaxis(o_x, 0, 2)
    return o.reshape(B, H, L, dv), S


def ret_features(p):
    B, L, _ = p.shape
    q, k, v, g = jnp.split(p, 4, axis=-1)
    inv = ROPE_THETA ** (-jnp.linspace(0.0, 1.0, RET_DIM // 2, dtype=F32))
    ang = jnp.arange(L, dtype=F32)[:, None] * inv
    heads = lambda t: t.reshape(B, L, RET_HEADS, RET_DIM).astype(F32)
    q = rotate_half(heads(q), ang)
    k = rotate_half(heads(k), ang) * RET_DIM ** -0.5
    tr = lambda t: t.transpose(0, 2, 1, 3)
    return tr(q), tr(k), tr(heads(v)), g


def head_groupnorm_gate(o, gate, w):
    B, H, L, dv = o.shape
    o = o.transpose(0, 2, 1, 3)
    mu = jnp.mean(o, axis=-1, keepdims=True)
    var = jnp.mean(jnp.square(o - mu), axis=-1, keepdims=True)
    y = (o - mu) * lax.rsqrt(var + EPS) * w.astype(F32).reshape(H, dv)
    return (y * jax.nn.silu(gate.astype(F32).reshape(B, L, H, dv))).reshape(B, L, H * dv)


def retention(p_lat, p_ctx, decay_logit, gn_w, need_ctx):
    lat, cx = ret_features(p_lat), ret_features(p_ctx)
    log_gamma = jax.nn.log_sigmoid(decay_logit.astype(F32))
    B = p_lat.shape[0]
    o_lat, o_ctx = [], []
    for d in range(2):
        fl = flip_time if d == 1 else keep_time
        s0 = jnp.zeros((B, RET_HEADS, RET_DIM, RET_DIM), F32)
        oc, s_ctx = retention_chunk_scan(fl(cx[0]), fl(cx[1]), fl(cx[2]), log_gamma[d], s0)
        ol, _ = retention_chunk_scan(fl(lat[0]), fl(lat[1]), fl(lat[2]), log_gamma[d], s_ctx)
        o_lat.append(fl(ol))
        o_ctx.append(fl(oc))
    out_lat = head_groupnorm_gate(o_lat[0] + o_lat[1], lat[3], gn_w)
    out_ctx = head_groupnorm_gate(o_ctx[0] + o_ctx[1], cx[3], gn_w) if need_ctx else None
    return out_lat, out_ctx


def hyena_filters(L, w1, b1, w2, b2, w3):
    pos = jnp.arange(L, dtype=F32)
    t = pos / max(L - 1, 1)
    bands = jnp.linspace(1e-4, HY_BANDS - 1, HY_BANDS, dtype=F32)
    phase = (2.0 * math.pi / L) * pos[:, None] * bands[None, :]
    z = jnp.concatenate([t[:, None], jnp.cos(phase), -jnp.sin(phase)], axis=-1)
    h = jnp.sin(HY_SIN_FREQ * (z @ w1.astype(F32) + b1.astype(F32)))
    h = jnp.sin(HY_SIN_FREQ * (h @ w2.astype(F32) + b2.astype(F32)))
    h = (h @ w3.astype(F32)).reshape(L, HY_ORDER, 2, HY_CH)
    rates = jnp.abs(jnp.linspace(math.log(HY_TARGET) / HY_DECAY_LONG, math.log(HY_TARGET) / HY_DECAY_SHORT,
                                 HY_CH, dtype=F32))
    h = h * jnp.exp(-t[:, None] * rates[None, :])[:, None, None, :]
    h = h / (jnp.sum(jnp.abs(h), axis=(0, 2), keepdims=True) + EPS)
    return h.transpose(1, 2, 0, 3)


def bidir_long_conv(u, h_fwd, h_bwd, skip):
    B, L, C = u.shape
    taps = jnp.concatenate([h_fwd, jnp.zeros((1, C), F32), h_bwd[:0:-1]], axis=0)
    y = jnp.fft.irfft(jnp.fft.rfft(u, n=2 * L, axis=1) * jnp.fft.rfft(taps, axis=0)[None],
                      n=2 * L, axis=1)[:, :L]
    return y + u * skip


def hyena(p, conv_w, w1, b1, w2, b2, w3, skip):
    L = p.shape[1]
    filt = hyena_filters(L, w1, b1, w2, b2, w3)
    u = short_conv(p, conv_w).astype(F32)
    parts = jnp.split(u, HY_ORDER + 1, axis=-1)
    z = parts[0]
    for n in range(HY_ORDER):
        z = parts[n + 1] * bidir_long_conv(z, filt[n, 0], filt[n, 1], skip[n].astype(F32))
    return z


def cd_mixer(p_lat, p_ctx, ret_decay_logit, ret_gn_w, hy_conv_w, hy_f_w1, hy_f_b1, hy_f_w2, hy_f_b2,
             hy_f_w3, hy_bias, need_ctx):
    c_lat, c_ctx = retention(p_lat[..., :RET_COLS], p_ctx[..., :RET_COLS], ret_decay_logit, ret_gn_w, need_ctx)
    hy = lambda p: hyena(p, hy_conv_w, hy_f_w1, hy_f_b1, hy_f_w2, hy_f_b2, hy_f_w3, hy_bias)
    o_lat = jnp.concatenate([c_lat, hy(p_lat[..., RET_COLS:])], axis=-1)
    o_ctx = jnp.concatenate([c_ctx, hy(p_ctx[..., RET_COLS:])], axis=-1) if need_ctx else None
    return o_lat, o_ctx


def _pad_cols(w, mult):
    n = w.shape[-1]
    pad = (-n) % mult
    return jnp.pad(w, ((0, 0), (0, pad))) if pad else w


def kernel(x, c, ctx, c_ctx, mod_w, mod_b, norm_mix_w, norm_ffn_w, ffn_w_in, ffn_w_out,
           ab_w_in, ab_w_out, dn_conv_w, dn_a_log, dn_dt_bias, dn_norm_w, swa_q_norm_w, swa_k_norm_w,
           swa_sink, cd_w_in, cd_w_out, ret_decay_logit, ret_gn_w, hy_conv_w, hy_f_w1, hy_f_b1,
           hy_f_w2, hy_f_b2, hy_f_w3, hy_bias):
    B, L, D = x.shape
    n_ctx = ctx.shape[1]
    rows = L // GRID_W
    ang_row, ang_col = axial_angles(rows)
    c_rows = jnp.zeros((8, D), F32).at[:B].set(c).at[B].set(c_ctx)
    mod_all = modulation(c_rows, mod_w, mod_b)
    hid = ffn_w_out.shape[1]
    h_ctx = ctx
    for layer in range(DEPTH):
        need_ctx = layer != DEPTH - 1
        i = layer // 2
        m = mod_all[layer].reshape(8, 6, D)
        mod = [m[:B, j][:, None, :] for j in range(6)]
        mod_c = [jnp.broadcast_to(m[B, j][None, None, :], (B, 1, D)) for j in range(6)]
        if layer % 2 == 0:
            w = ab_w_in[i]
            w_in = jnp.concatenate([w[:, :4 * DN_WIDTH], w[:, DN_COLS:], w[:, 4 * DN_WIDTH:DN_COLS]], axis=1)
            w_out, cols = ab_w_out[i], ab_w_in.shape[-1]
        else:
            w = cd_w_in[i]
            w_in = jnp.concatenate([w[:, RET_COLS:], w[:, :RET_COLS]], axis=1)
            w_out, cols = cd_w_out[i], cd_w_in.shape[-1]
        w_in_p = _pad_cols(w_in, 128).astype(BF16)
        p_lat = in_projection(x, norm_mix_w[layer], mod[0], mod[1], w_in_p, 512)
        p_ctx = in_projection(h_ctx, norm_mix_w[layer], mod_c[0], mod_c[1], w_in_p, n_ctx)
        if layer % 2 == 0:
            swa0 = 4 * DN_WIDTH
            ab0 = swa0 + (SWA_HEADS + 2 * SWA_KV_HEADS) * SWA_DIM
            a_lat, a_ctx = deltanet_pallas(p_lat, p_ctx, dn_conv_w[i], dn_a_log[i], dn_dt_bias[i],
                                           dn_norm_w[i], need_ctx, ab_col=ab0, gate_col=3 * DN_WIDTH)
            b_lat, b_ctx = window_gqa_pallas(p_lat, p_ctx, swa0, axial_rope_tables(rows), swa_q_norm_w[i],
                                             swa_k_norm_w[i], swa_sink[i], need_ctx)
        else:
            a_lat, a_ctx = retention_pallas(p_lat, p_ctx, ret_decay_logit[i], ret_gn_w[i], need_ctx,
                                            col0=(HY_ORDER + 1) * HY_CH)
            assert not need_ctx, "the last layer's context outputs reach no latent token"
            b_lat = hyena_pallas(p_lat, hy_conv_w[i], hy_f_w1[i], hy_f_b1[i], hy_f_w2[i], hy_f_b2[i],
                                 hy_f_w3[i], hy_bias[i], col0=0)
            b_ctx = None
        wo = w_out.astype(BF16)
        wg = ffn_w_in[layer][:, :hid].astype(BF16)
        wu = ffn_w_in[layer][:, hid:].astype(BF16)
        wd = ffn_w_out[layer].astype(BF16)
        x = out_projection_ffn(x, a_lat, b_lat, wo, mod[2], norm_ffn_w[layer], mod[3], mod[4], mod[5],
                               wg, wu, wd, 512, 256)
        if need_ctx:
            h_ctx = out_projection_ffn(h_ctx, a_ctx, b_ctx, wo, mod_c[2], norm_ffn_w[layer], mod_c[3], mod_c[4],
                                       mod_c[5], wg, wu, wd, n_ctx, 256)
    return x
```

```python
import functools
import math

import jax
import jax.numpy as jnp
import ml_dtypes
import numpy as np
from jax import lax
from jax.experimental import pallas as pl
from jax.experimental.pallas import tpu as pltpu

F32 = jnp.float32
BF16 = jnp.bfloat16
EPS = 1e-6

DEPTH = 2
GRID_W = 64

DN_HEADS = 4
DN_DIM = 128
DN_WIDTH = DN_HEADS * DN_DIM
DN_COLS = 4 * DN_WIDTH + 4 * DN_HEADS
SWA_HEADS = 4
SWA_KV_HEADS = 2
SWA_DIM = 128
SWA_BLOCK = 128
ROPE_THETA = 10000.0
RET_HEADS = 4
RET_DIM = 128
RET_WIDTH = RET_HEADS * RET_DIM
RET_CHUNK = 128
RET_COLS = 4 * RET_WIDTH
HY_CH = 512
HY_ORDER = 2
HY_BANDS = 8
HY_SIN_FREQ = 1.0
HY_TARGET = 1e-2
HY_DECAY_SHORT = 0.3
HY_DECAY_LONG = 1.5

VMEM_LIMIT_BYTES = 56 * 1024 * 1024
SUBLANES, LANES = 8, 128

DENSE_TILE = 512
FFN_HIDDEN_CHUNK = 256
PREP_TILE = 1024
SWA_BLOCKS_PER_STEP = 4
MOD_COL_TILE = 1536
FILTER_ROW_TILE = 512


def _cparams(sem):
    return pltpu.CompilerParams(dimension_semantics=sem, vmem_limit_bytes=VMEM_LIMIT_BYTES)


def _mod_body(c_ref, w_ref, b_ref, o_ref):
    a = c_ref[...]
    a = a * jax.nn.sigmoid(a)
    o_ref[...] = jnp.dot(a, w_ref[...], preferred_element_type=F32,
                         precision=lax.Precision.HIGHEST) + b_ref[...]


def modulation(c_rows, mod_w, mod_b):
    depth, d, n = mod_w.shape
    tn = MOD_COL_TILE
    return pl.pallas_call(
        _mod_body,
        grid=(depth, n // tn),
        in_specs=[pl.BlockSpec((8, d), lambda l, j: (0, 0)),
                  pl.BlockSpec((None, d, tn), lambda l, j: (l, 0, j)),
                  pl.BlockSpec((None, 1, tn), lambda l, j: (l, 0, j))],
        out_specs=pl.BlockSpec((None, 8, tn), lambda l, j: (l, 0, j)),
        out_shape=jax.ShapeDtypeStruct((depth, 8, n), F32),
        compiler_params=_cparams(("parallel", "parallel")),
        name="modulation",
    )(c_rows, mod_w, mod_b.reshape(depth, 1, n))


def _norm_mod(x, nw, shift, scale):
    y = x * lax.rsqrt(jnp.mean(x * x, axis=-1, keepdims=True) + EPS)
    return (y * nw) * (1.0 + scale) + shift


def _resident(shape):
    return pl.BlockSpec(shape, lambda *_: (0,) * len(shape), pipeline_mode=pl.Buffered(1))


def _inproj_body(x_ref, nw_ref, shift_ref, scale_ref, w_ref, o_ref):
    h = _norm_mod(x_ref[...], nw_ref[...], shift_ref[...], scale_ref[...]).astype(BF16)
    o_ref[...] = jnp.dot(h, w_ref[...], preferred_element_type=F32)


def in_projection(x, nw, shift, scale, w, tm):
    b, l, d = x.shape
    n = w.shape[1]
    return pl.pallas_call(
        _inproj_body,
        grid=(b, l // tm),
        in_specs=[pl.BlockSpec((None, tm, d), lambda bi, i: (bi, i, 0)),
                  _resident((1, d)),
                  pl.BlockSpec((None, 1, d), lambda bi, i: (bi, 0, 0)),
                  pl.BlockSpec((None, 1, d), lambda bi, i: (bi, 0, 0)),
                  _resident((d, n))],
        out_specs=pl.BlockSpec((None, tm, n), lambda bi, i: (bi, i, 0)),
        out_shape=jax.ShapeDtypeStruct((b, l, n), F32),
        compiler_params=_cparams(("parallel", "parallel")),
        name="in_projection",
    )(x, nw.reshape(1, d), shift, scale, w)


def _outffn_body(x_ref, oa_ref, ob_ref, wo_ref, g2_ref, nw_ref, sh_ref, sc_ref, g5_ref,
                 wg_ref, wu_ref, wd_ref, y_ref, x1_scr, h_scr, acc_scr, *, th):
    wa = oa_ref.shape[1]
    mix = (jnp.dot(oa_ref[...].astype(BF16), wo_ref[:wa, :], preferred_element_type=F32)
           + jnp.dot(ob_ref[...].astype(BF16), wo_ref[wa:, :], preferred_element_type=F32))
    x1 = x_ref[...] + g2_ref[...] * mix
    x1_scr[...] = x1
    h_scr[...] = _norm_mod(x1, nw_ref[...], sh_ref[...], sc_ref[...]).astype(BF16)
    for k in range(wg_ref.shape[1] // th):
        ks = slice(k * th, (k + 1) * th)
        g = jnp.dot(h_scr[...], wg_ref[:, ks], preferred_element_type=F32)
        u = jnp.dot(h_scr[...], wu_ref[:, ks], preferred_element_type=F32)
        a = (g * jax.nn.sigmoid(g) * u).astype(BF16)
        part = jnp.dot(a, wd_ref[ks, :], preferred_element_type=F32)
        if k == 0:
            acc_scr[...] = part
        else:
            acc_scr[...] += part
    y_ref[...] = x1_scr[...] + g5_ref[...] * acc_scr[...]


def out_projection_ffn(x, oa, ob, wo, g2, nw, shift, scale, g5, wg, wu, wd, tm, th):
    b, l, d = x.shape
    wa, wb = oa.shape[-1], ob.shape[-1]
    hid = wg.shape[1]
    tok = lambda bi, i: (bi, i, 0)
    vec = pl.BlockSpec((None, 1, d), lambda bi, i: (bi, 0, 0))
    return pl.pallas_call(
        functools.partial(_outffn_body, th=th),
        grid=(b, l // tm),
        in_specs=[pl.BlockSpec((None, tm, d), tok),
                  pl.BlockSpec((None, tm, wa), tok),
                  pl.BlockSpec((None, tm, wb), tok),
                  _resident((d, d)), vec, _resident((1, d)), vec, vec, vec,
                  _resident((d, hid)), _resident((d, hid)), _resident((hid, d))],
        out_specs=pl.BlockSpec((None, tm, d), tok),
        out_shape=jax.ShapeDtypeStruct((b, l, d), F32),
        scratch_shapes=[pltpu.VMEM((tm, d), F32), pltpu.VMEM((tm, d), BF16), pltpu.VMEM((tm, d), F32)],
        compiler_params=_cparams(("parallel", "parallel")),
        name="out_projection_ffn",
    )(x, oa, ob, wo, g2, nw.reshape(1, d), shift, scale, g5, wg, wu, wd)


NEG_BIG = -1e30


def _head_rms(x, w):
    return x * lax.rsqrt(jnp.mean(x * x, axis=-1, keepdims=True) + EPS) * w


def _swa_prep_body(*refs, use_rope):
    if use_rope:
        q_ref, k_ref, v_ref, qw_ref, kw_ref, rt_ref, ct_ref, qo_ref, ko_ref, vo_ref = refs
        tl = q_ref.shape[0]
        nr = tl // GRID_W

        def table(kind):
            rowp = jnp.broadcast_to(rt_ref[kind][:, None, :], (nr, GRID_W, SWA_DIM))
            colp = jnp.broadcast_to(ct_ref[kind][None, :, :], (nr, GRID_W, SWA_DIM))
            return (rowp + colp).reshape(tl, SWA_DIM)

        cos, sa, sb = table(0), table(1), table(2)
    else:
        q_ref, k_ref, v_ref, qw_ref, kw_ref, qo_ref, ko_ref, vo_ref = refs

    def prep(x, w):
        y = _head_rms(x, w)
        if use_rope:
            y = y * cos + pltpu.roll(y, SWA_DIM - 32, 1) * sa + pltpu.roll(y, 32, 1) * sb
        return y.astype(BF16)

    for h in range(SWA_HEADS):
        sl = slice(h * SWA_DIM, (h + 1) * SWA_DIM)
        qo_ref[:, sl] = prep(q_ref[:, sl], qw_ref[...])
    for h in range(SWA_KV_HEADS):
        sl = slice(h * SWA_DIM, (h + 1) * SWA_DIM)
        ko_ref[:, sl] = prep(k_ref[:, sl], kw_ref[...])
    vo_ref[...] = v_ref[...].astype(BF16)


def swa_prep(p, col0, qw, kw, rope, tl):
    b, l, _ = p.shape
    nq, nk = SWA_HEADS * SWA_DIM, SWA_KV_HEADS * SWA_DIM
    tok = lambda bi, i: (bi, i, 0)
    in_specs = [pl.BlockSpec((None, tl, nq), lambda bi, i: (bi, i, col0 // nq)),
                pl.BlockSpec((None, tl, nk), lambda bi, i: (bi, i, (col0 + nq) // nk)),
                pl.BlockSpec((None, tl, nk), lambda bi, i: (bi, i, (col0 + nq + nk) // nk)),
                pl.BlockSpec((1, SWA_DIM), lambda bi, i: (0, 0)),
                pl.BlockSpec((1, SWA_DIM), lambda bi, i: (0, 0))]
    args = [p, p, p, qw.reshape(1, SWA_DIM), kw.reshape(1, SWA_DIM)]
    if rope is not None:
        in_specs += [pl.BlockSpec((3, tl // GRID_W, SWA_DIM), lambda bi, i: (0, i, 0)),
                     pl.BlockSpec((3, GRID_W, SWA_DIM), lambda bi, i: (0, 0, 0))]
        args += list(rope)
    return pl.pallas_call(
        functools.partial(_swa_prep_body, use_rope=rope is not None),
        grid=(b, l // tl),
        in_specs=in_specs,
        out_specs=[pl.BlockSpec((None, tl, nq), tok), pl.BlockSpec((None, tl, nk), tok),
                   pl.BlockSpec((None, tl, nk), tok)],
        out_shape=[jax.ShapeDtypeStruct((b, l, nq), BF16), jax.ShapeDtypeStruct((b, l, nk), BF16),
                   jax.ShapeDtypeStruct((b, l, nk), BF16)],
        compiler_params=_cparams(("parallel", "parallel")),
        name="swa_prep",
    )(*args)


def _nt_dot(a, b):
    return lax.dot_general(a, b, (((1,), (1,)), ((), ())), preferred_element_type=F32)


def _swa_attend(q, keys, vals, masks, sink_col):
    scale = SWA_DIM ** -0.5
    scores = []
    for kk, mask in zip(keys, masks):
        s = _nt_dot(q, kk) * scale
        if mask is not None:
            s = jnp.where(mask, s, NEG_BIG)
        scores.append(s)
    m = sink_col
    for s in scores:
        m = jnp.maximum(m, jnp.max(s, axis=-1, keepdims=True))
    den = jnp.exp(sink_col - m)
    acc = None
    for s, vv in zip(scores, vals):
        pr = jnp.exp(s - m)
        den = den + jnp.sum(pr, axis=-1, keepdims=True)
        o = jnp.dot(pr.astype(BF16), vv, preferred_element_type=F32)
        acc = o if acc is None else acc + o
    return acc / den


def _swa_attn_body(sink_ref, q_ref, kp_ref, km_ref, kn_ref, vp_ref, vm_ref, vn_ref, kc_ref, vc_ref,
                   o_ref, *, nb):
    i = pl.program_id(1)
    n_blocks = pl.num_programs(1) * nb
    w, d = SWA_BLOCK, SWA_DIM
    grp = SWA_HEADS // SWA_KV_HEADS
    rows = lax.broadcasted_iota(jnp.int32, (grp * w, w), 0)
    qi = rows % w
    kj = lax.broadcasted_iota(jnp.int32, (grp * w, w), 1)
    for r in range(nb):
        blk = i * nb + r
        mask_prev = (kj >= qi) & (blk > 0)
        mask_next = (kj <= qi) & (blk < n_blocks - 1)
        rs = slice(r * w, (r + 1) * w)
        for h in range(SWA_KV_HEADS):
            hs = slice(h * d, (h + 1) * d)
            q = jnp.concatenate([q_ref[rs, (h * grp + g) * d:(h * grp + g + 1) * d] for g in range(grp)], axis=0)
            sink_col = jnp.zeros((grp * w, 1), F32)
            for g in range(grp):
                sink_col = jnp.where((rows[:, :1] // w) == g, sink_ref[h * grp + g], sink_col)
            if r > 0:
                k_prev, v_prev = km_ref[(r - 1) * w:r * w, hs], vm_ref[(r - 1) * w:r * w, hs]
            else:
                k_prev, v_prev = kp_ref[:, hs], vp_ref[:, hs]
            if r < nb - 1:
                k_next, v_next = km_ref[(r + 1) * w:(r + 2) * w, hs], vm_ref[(r + 1) * w:(r + 2) * w, hs]
            else:
                k_next, v_next = kn_ref[:, hs], vn_ref[:, hs]
            o = _swa_attend(q, [k_prev, km_ref[rs, hs], k_next, kc_ref[:, hs]],
                            [v_prev, vm_ref[rs, hs], v_next, vc_ref[:, hs]],
                            [mask_prev, None, mask_next, None], sink_col)
            for g in range(grp):
                o_ref[rs, (h * grp + g) * d:(h * grp + g + 1) * d] = o[g * w:(g + 1) * w].astype(o_ref.dtype)


def swa_attention(q, k, v, kc, vc, sink, nb):
    b, l, nq = q.shape
    nk = k.shape[-1]
    n_ctx = kc.shape[1]
    w = SWA_BLOCK
    tq = nb * w
    last = l // w - 1
    main = lambda bi, i, s: (bi, i, 0)
    prev = lambda bi, i, s: (bi, jnp.maximum(i * nb - 1, 0), 0)
    nxt = lambda bi, i, s: (bi, jnp.minimum((i + 1) * nb, last), 0)
    cx = lambda bi, i, s: (bi, 0, 0)
    return pl.pallas_call(
        functools.partial(_swa_attn_body, nb=nb),
        grid_spec=pltpu.PrefetchScalarGridSpec(
            num_scalar_prefetch=1,
            grid=(b, l // tq),
            in_specs=[pl.BlockSpec((None, tq, nq), main),
                      pl.BlockSpec((None, w, nk), prev), pl.BlockSpec((None, tq, nk), main),
                      pl.BlockSpec((None, w, nk), nxt),
                      pl.BlockSpec((None, w, nk), prev), pl.BlockSpec((None, tq, nk), main),
                      pl.BlockSpec((None, w, nk), nxt),
                      pl.BlockSpec((None, n_ctx, nk), cx), pl.BlockSpec((None, n_ctx, nk), cx)],
            out_specs=pl.BlockSpec((None, tq, nq), main)),
        out_shape=jax.ShapeDtypeStruct((b, l, nq), BF16),
        compiler_params=_cparams(("parallel", "parallel")),
        name="swa_attention",
    )(sink, q, k, k, k, v, v, v, kc, vc)


def _ctx_attn_body(sink_ref, q_ref, kc_ref, vc_ref, o_ref):
    n_ctx, d = q_ref.shape[0], SWA_DIM
    grp = SWA_HEADS // SWA_KV_HEADS
    rows = lax.broadcasted_iota(jnp.int32, (grp * n_ctx, 1), 0)
    for h in range(SWA_KV_HEADS):
        hs = slice(h * d, (h + 1) * d)
        q = jnp.concatenate([q_ref[:, (h * grp + g) * d:(h * grp + g + 1) * d] for g in range(grp)], axis=0)
        sink_col = jnp.zeros((grp * n_ctx, 1), F32)
        for g in range(grp):
            sink_col = jnp.where((rows // n_ctx) == g, sink_ref[h * grp + g], sink_col)
        o = _swa_attend(q, [kc_ref[:, hs]], [vc_ref[:, hs]], [None], sink_col)
        for g in range(grp):
            o_ref[:, (h * grp + g) * d:(h * grp + g + 1) * d] = o[g * n_ctx:(g + 1) * n_ctx].astype(o_ref.dtype)


def ctx_attention(qc, kc, vc, sink):
    b, n_ctx, nq = qc.shape
    nk = kc.shape[-1]
    cx = lambda bi, s: (bi, 0, 0)
    return pl.pallas_call(
        _ctx_attn_body,
        grid_spec=pltpu.PrefetchScalarGridSpec(
            num_scalar_prefetch=1, grid=(b,),
            in_specs=[pl.BlockSpec((None, n_ctx, nq), cx), pl.BlockSpec((None, n_ctx, nk), cx),
                      pl.BlockSpec((None, n_ctx, nk), cx)],
            out_specs=pl.BlockSpec((None, n_ctx, nq), cx)),
        out_shape=jax.ShapeDtypeStruct((b, n_ctx, nq), BF16),
        compiler_params=_cparams(("parallel",)),
        name="ctx_attention",
    )(sink, qc, kc, vc)


def axial_rope_tables(rows):
    m = SWA_DIM // 4
    inv = ROPE_THETA ** (-np.arange(m, dtype=np.float64) / m)

    def tables(count, lane0):
        ang = np.arange(count, dtype=np.float64)[:, None] * inv
        out = np.zeros((3, count, SWA_DIM))
        out[0, :, lane0:lane0 + m] = out[0, :, lane0 + m:lane0 + 2 * m] = np.cos(ang)
        out[1, :, lane0:lane0 + m] = -np.sin(ang)
        out[2, :, lane0 + m:lane0 + 2 * m] = np.sin(ang)
        return out.astype(np.float32)

    return tables(rows, 0), tables(GRID_W, SWA_DIM // 2)


def window_gqa_pallas(p_lat, p_ctx, col0, rope, q_norm_w, k_norm_w, sink, need_ctx):
    ql, kl, vl = swa_prep(p_lat, col0, q_norm_w, k_norm_w, rope, PREP_TILE)
    qc, kc, vc = swa_prep(p_ctx, col0, q_norm_w, k_norm_w, None, p_ctx.shape[1])
    o_lat = swa_attention(ql, kl, vl, kc, vc, sink, SWA_BLOCKS_PER_STEP)
    o_ctx = ctx_attention(qc, kc, vc, sink) if need_ctx else None
    return o_lat, o_ctx


def _tn_dot(a, b):
    return lax.dot_general(a, b, (((0,), (0,)), ((), ())), preferred_element_type=F32)


def _log_sigmoid(x):
    return jnp.minimum(x, 0.0) - jnp.log(1.0 + jnp.exp(-jnp.abs(x)))


RET_STREAMS = 2 * RET_HEADS


RET_SUBS = 2


def _ret_scan_body(logit_ref, qf_ref, kf_ref, vf_ref, rotf_ref, qb_ref, kb_ref, vb_ref, rotb_ref, rotc_ref,
                   s0_ref, of_ref, ob_ref, sf_ref, s_scr):
    n = pl.program_id(1)
    c, dh = RET_CHUNK, RET_DIM

    @pl.when(n == 0)
    def _():
        s_scr[...] = s0_ref[...]

    ii = lax.broadcasted_iota(jnp.int32, (c, c), 0)
    jj = lax.broadcasted_iota(jnp.int32, (c, c), 1)
    i1 = lax.broadcasted_iota(jnp.int32, (c, 1), 0)
    rel = {True: (ii - jj).astype(F32), False: (jj - ii).astype(F32)}
    pos = {True: i1.astype(F32), False: (c - 1 - i1).astype(F32)}
    streams = [(d == 0, h) for d in range(2) for h in range(RET_HEADS)]
    srcs = {True: (qf_ref, kf_ref, vf_ref, rotf_ref), False: (qb_ref, kb_ref, vb_ref, rotb_ref)}
    items = [(r, f, h, j, (j if f else RET_SUBS - 1 - j)) for j in range(RET_SUBS) for r, (f, h) in enumerate(streams)]
    rows = lambda blk: slice(blk * c, (blk + 1) * c)
    hs = lambda h: slice(h * dh, (h + 1) * dh)
    rng = range(len(items))

    cos_i, sin_i = rotc_ref[0], rotc_ref[1]
    tables = {}
    for f in (True, False):
        for bk in range(RET_SUBS):
            a = srcs[f][3][bk]
            tables[f, bk] = (a[0:1] * cos_i - a[1:2] * sin_i, a[2:3] * cos_i + a[3:4] * sin_i)

    def rot(x, f, bk):
        cos_t, sin_t = tables[f, bk]
        return x * cos_t + pltpu.roll(x, dh // 2, 1) * sin_t

    lg = [_log_sigmoid(jnp.full((1, 1), logit_ref[r], F32)) for r in range(RET_STREAMS)]
    q = [rot(srcs[f][0][rows(bk), hs(h)], f, bk) for _, f, h, _, bk in items]
    k = [rot(srcs[f][1][rows(bk), hs(h)], f, bk) * dh ** -0.5 for _, f, h, _, bk in items]
    v = [srcs[f][2][rows(bk), hs(h)].astype(BF16) for _, f, h, _, bk in items]
    dmat = [jnp.where(rel[f] >= 0, jnp.exp(lg[r] * jnp.maximum(rel[f], 0.0)), 0.0)
            for r, (f, _) in enumerate(streams)]
    scores = [(_nt_dot(q[i].astype(BF16), k[i].astype(BF16)) * dmat[items[i][0]]).astype(BF16) for i in rng]
    o_in = [jnp.dot(scores[i], v[i], preferred_element_type=F32) for i in rng]
    q_dec = [(q[i] * jnp.exp(lg[items[i][0]] * (pos[items[i][1]] + 1.0))).astype(BF16) for i in rng]
    k_dec = [(k[i] * jnp.exp(lg[items[i][0]] * (c - 1.0 - pos[items[i][1]]))).astype(BF16) for i in rng]
    kv = [_tn_dot(k_dec[i], v[i]) for i in rng]
    chunk_dec = [jnp.exp(lg[r] * c) for r in range(RET_STREAMS)]

    s = [s_scr[r] for r in range(RET_STREAMS)]
    for j in range(RET_SUBS):
        idx = [i for i in rng if items[i][3] == j]
        o = [o_in[i] + jnp.dot(q_dec[i], s[items[i][0]].astype(BF16), preferred_element_type=F32) for i in idx]
        s = [s[items[i][0]] * chunk_dec[items[i][0]] + kv[i] for i in idx]
        for i, oi in zip(idx, o):
            _, f, h, _, bk = items[i]
            (of_ref if f else ob_ref)[rows(bk), hs(h)] = oi
    for r in range(RET_STREAMS):
        s_scr[r] = s[r]

    @pl.when(n == pl.num_programs(1) - 1)
    def _():
        sf_ref[...] = s_scr[...]


def retention_scan(p, col0, decay_logit, rot_chunk, rot_in, s0):
    b, l, _ = p.shape
    c, dh, w = RET_CHUNK * RET_SUBS, RET_DIM, RET_WIDTH
    n = l // c
    cb = col0 // w
    fcol = lambda off: (lambda bi, ni, s: (bi, ni, cb + off))
    bcol = lambda off: (lambda bi, ni, s: (bi, n - 1 - ni, cb + off))
    rot = lambda m: pl.BlockSpec((RET_SUBS, 4, dh), m)
    st = lambda bi, ni, s: (bi, 0, 0, 0)
    tok = lambda col: [pl.BlockSpec((None, c, w), col(j)) for j in range(3)]
    return pl.pallas_call(
        _ret_scan_body,
        grid_spec=pltpu.PrefetchScalarGridSpec(
            num_scalar_prefetch=1,
            grid=(b, n),
            in_specs=tok(fcol) + [rot(lambda bi, ni, s: (ni, 0, 0))]
            + tok(bcol) + [rot(lambda bi, ni, s: (n - 1 - ni, 0, 0))]
            + [pl.BlockSpec((2, RET_CHUNK, dh), lambda bi, ni, s: (0, 0, 0)),
               pl.BlockSpec((None, RET_STREAMS, dh, dh), st)],
            out_specs=[pl.BlockSpec((None, c, w), lambda bi, ni, s: (bi, ni, 0)),
                       pl.BlockSpec((None, c, w), lambda bi, ni, s: (bi, n - 1 - ni, 0)),
                       pl.BlockSpec((None, RET_STREAMS, dh, dh), st)],
            scratch_shapes=[pltpu.VMEM((RET_STREAMS, dh, dh), F32)]),
        out_shape=[jax.ShapeDtypeStruct((b, l, w), F32), jax.ShapeDtypeStruct((b, l, w), F32),
                   jax.ShapeDtypeStruct((b, RET_STREAMS, dh, dh), F32)],
        compiler_params=_cparams(("parallel", "arbitrary")),
        name="retention_scan",
    )(decay_logit.reshape(-1), p, p, p, rot_chunk, p, p, p, rot_chunk, rot_in, s0)


def _ret_final_body(of_ref, ob_ref, g_ref, w_ref, y_ref):
    for h in range(RET_HEADS):
        sl = slice(h * RET_DIM, (h + 1) * RET_DIM)
        o = of_ref[:, sl] + ob_ref[:, sl]
        mu = jnp.mean(o, axis=-1, keepdims=True)
        var = jnp.mean(jnp.square(o - mu), axis=-1, keepdims=True)
        y = (o - mu) * lax.rsqrt(var + EPS) * w_ref[:, sl]
        g = g_ref[:, sl]
        y_ref[:, sl] = (y * (g * jax.nn.sigmoid(g))).astype(y_ref.dtype)


def retention_finalize(o_f, o_b, p, gate_col, gn_w, tl):
    b, l, w = o_f.shape
    tok = lambda bi, i: (bi, i, 0)
    return pl.pallas_call(
        _ret_final_body,
        grid=(b, l // tl),
        in_specs=[pl.BlockSpec((None, tl, w), tok),
                  pl.BlockSpec((None, tl, w), tok),
                  pl.BlockSpec((None, tl, w), lambda bi, i: (bi, i, gate_col // w)),
                  pl.BlockSpec((1, w), lambda bi, i: (0, 0))],
        out_specs=pl.BlockSpec((None, tl, w), tok),
        out_shape=jax.ShapeDtypeStruct((b, l, w), BF16),
        compiler_params=_cparams(("parallel", "parallel")),
        name="retention_finalize",
    )(o_f, o_b, p, gn_w.reshape(1, w))


def retention_rope_tables(l):
    inv = ROPE_THETA ** (-np.linspace(0.0, 1.0, RET_DIM // 2))
    inv2 = np.concatenate([inv, inv])
    sign = np.concatenate([-np.ones(RET_DIM // 2), np.ones(RET_DIM // 2)])
    a = (RET_CHUNK * np.arange(l // RET_CHUNK, dtype=np.float64))[:, None] * inv2
    b = np.arange(RET_CHUNK, dtype=np.float64)[:, None] * inv2
    chunk = np.stack([np.cos(a), np.sin(a), sign * np.sin(a), sign * np.cos(a)], axis=1)
    return chunk.astype(np.float32), np.stack([np.cos(b), np.sin(b)]).astype(np.float32)


def retention_pallas(p_lat, p_ctx, decay_logit, gn_w, need_ctx, col0=0):
    b, l, _ = p_lat.shape
    n_ctx = p_ctx.shape[1]
    gate_col = col0 + 3 * RET_WIDTH
    s0 = jnp.zeros((b, RET_STREAMS, RET_DIM, RET_DIM), F32)
    oc_f, oc_b, s_ctx = retention_scan(p_ctx, col0, decay_logit, *retention_rope_tables(n_ctx), s0)
    ol_f, ol_b, _ = retention_scan(p_lat, col0, decay_logit, *retention_rope_tables(l), s_ctx)
    out_lat = retention_finalize(ol_f, ol_b, p_lat, gate_col, gn_w, PREP_TILE)
    out_ctx = retention_finalize(oc_f, oc_b, p_ctx, gate_col, gn_w, n_ctx) if need_ctx else None
    return out_lat, out_ctx


DN_BLOCK = 128
DN_STREAMS = 2 * DN_HEADS
DN_SUBS = 2


def _shift_rows(x, prev_row, next_row):
    n = x.shape[0]
    r = lax.broadcasted_iota(jnp.int32, (SUBLANES, 1), 0)
    down, up = pltpu.roll(x, 1, 0), pltpu.roll(x, n - 1, 0)
    x_prev = jnp.concatenate([jnp.where(r == 0, prev_row, down[:SUBLANES]), down[SUBLANES:]], axis=0)
    x_next = jnp.concatenate([up[:n - SUBLANES], jnp.where(r == SUBLANES - 1, next_row, up[n - SUBLANES:])], axis=0)
    return x_prev, x_next


def _conv3(x_ref, xp_ref, xn_ref, w_ref):
    i = pl.program_id(1)
    prev_row = jnp.where(i > 0, xp_ref[SUBLANES - 1:SUBLANES, :], 0.0)
    next_row = jnp.where(i < pl.num_programs(1) - 1, xn_ref[0:1, :], 0.0)
    x = x_ref[...]
    x_prev, x_next = _shift_rows(x, prev_row, next_row)
    return x_prev * w_ref[0:1, :] + x * w_ref[1:2, :] + x_next * w_ref[2:3, :]


def _dn_prep_body(x_ref, xp_ref, xn_ref, ab_ref, cw_ref, alog_ref, dtb_ref,
                  q_ref, k_ref, v_ref, gb_ref, gam_ref, gamt_ref):
    y = _conv3(x_ref, xp_ref, xn_ref, cw_ref)
    y = y * jax.nn.sigmoid(y)
    for h in range(DN_HEADS):
        for part, (ref, mul) in enumerate(((q_ref, DN_DIM ** -0.5), (k_ref, 1.0))):
            sl = slice(part * DN_WIDTH + h * DN_DIM, part * DN_WIDTH + (h + 1) * DN_DIM)
            t = y[:, sl]
            t = t * lax.rsqrt(jnp.sum(t * t, axis=-1, keepdims=True) + EPS)
            ref[:, h * DN_DIM:(h + 1) * DN_DIM] = t * mul if mul != 1.0 else t
    v_ref[...] = y[:, 2 * DN_WIDTH:]

    ab = ab_ref[...]
    z = ab + dtb_ref[...]
    softplus = jnp.maximum(z, 0.0) + jnp.log(1.0 + jnp.exp(-jnp.abs(z)))
    g = -jnp.exp(alog_ref[...]) * softplus
    lane = lax.broadcasted_iota(jnp.int32, ab.shape, 1)
    gb_ref[...] = jnp.where(lane < DN_STREAMS, g, jax.nn.sigmoid(ab))

    c = DN_BLOCK
    ii = lax.broadcasted_iota(jnp.int32, (c, c), 0)
    jj = lax.broadcasted_iota(jnp.int32, (c, c), 1)
    tri_f = (ii >= jj).astype(F32)
    lane_c = lax.broadcasted_iota(jnp.int32, (c, ab.shape[1]), 1)
    for n in range(x_ref.shape[0] // c):
        gc = g[n * c:(n + 1) * c, :]
        cf = jnp.dot(tri_f, gc, preferred_element_type=F32, precision=lax.Precision.HIGHEST)
        cb = cf[c - 1:c, :] - cf + gc
        gam = jnp.where(lane_c < DN_HEADS, cf, cb)
        gam_ref[n * c:(n + 1) * c, :] = gam
        gamt_ref[n] = gam.T[:DN_STREAMS, :]


def dn_prep(p, ab_col, conv_w, a_log, dt_bias, tl):
    b, l, _ = p.shape
    c3 = 3 * DN_WIDTH
    nblk8 = l // 8
    lanes = 128
    pad8 = lambda a: jnp.pad(a.reshape(1, DN_STREAMS).astype(F32), ((0, 0), (0, lanes - DN_STREAMS)))
    tok = lambda bi, i: (bi, i, 0)
    return pl.pallas_call(
        _dn_prep_body,
        grid=(b, l // tl),
        in_specs=[pl.BlockSpec((None, tl, c3), tok),
                  pl.BlockSpec((None, 8, c3), lambda bi, i: (bi, jnp.maximum(i * (tl // 8) - 1, 0), 0)),
                  pl.BlockSpec((None, 8, c3), lambda bi, i: (bi, jnp.minimum((i + 1) * (tl // 8), nblk8 - 1), 0)),
                  pl.BlockSpec((None, tl, lanes), lambda bi, i: (bi, i, ab_col // lanes)),
                  pl.BlockSpec((3, c3), lambda bi, i: (0, 0)),
                  pl.BlockSpec((1, lanes), lambda bi, i: (0, 0)),
                  pl.BlockSpec((1, lanes), lambda bi, i: (0, 0))],
        out_specs=[pl.BlockSpec((None, tl, DN_WIDTH), tok)] * 3
        + [pl.BlockSpec((None, tl, lanes), tok)] * 2
        + [pl.BlockSpec((None, tl // DN_BLOCK, DN_STREAMS, DN_BLOCK), lambda bi, i: (bi, i, 0, 0))],
        out_shape=[jax.ShapeDtypeStruct((b, l, DN_WIDTH), F32)] * 3
        + [jax.ShapeDtypeStruct((b, l, lanes), F32)] * 2
        + [jax.ShapeDtypeStruct((b, l // DN_BLOCK, DN_STREAMS, DN_BLOCK), F32)],
        compiler_params=_cparams(("parallel", "parallel")),
        name="dn_prep",
    )(p, p, p, p, conv_w, pad8(a_log), pad8(dt_bias))


def _dot16(a, b):
    return jnp.dot(a.astype(BF16), b.astype(BF16), preferred_element_type=F32)


def _unit_triangular_inverses(ms, ii, jj):
    c = ms[0].shape[0]
    eye = (ii == jj).astype(F32)
    pair = (ii // 2) == (jj // 2)
    ts = [eye - jnp.where(pair, m, 0.0) for m in ms]
    s = 2
    while s < c:
        sub = ((ii // (2 * s)) == (jj // (2 * s))) & ((ii // s) != (jj // s))
        t16 = [t.astype(BF16) for t in ts]
        xs = [_dot16(jnp.where(sub, m, 0.0), t) for m, t in zip(ms, t16)]
        ts = [t - _dot16(tb, x) for t, tb, x in zip(ts, t16, xs)]
        s *= 2
    return ts


def _dn_scan_body(qf_ref, kf_ref, vf_ref, gbf_ref, gamf_ref, gamtf_ref,
                  qb_ref, kb_ref, vb_ref, gbb_ref, gamb_ref, gamtb_ref, s0_ref,
                  of_ref, ob_ref, sf_ref, s_scr):
    n = pl.program_id(1)

    @pl.when(n == 0)
    def _():
        s_scr[...] = s0_ref[...]

    c, dh = DN_BLOCK, DN_DIM
    ii = lax.broadcasted_iota(jnp.int32, (c, c), 0)
    jj = lax.broadcasted_iota(jnp.int32, (c, c), 1)
    incl = {True: ii >= jj, False: ii <= jj}
    strict = {True: ii > jj, False: ii < jj}
    srcs = {True: (qf_ref, kf_ref, vf_ref, gbf_ref, gamf_ref, gamtf_ref),
            False: (qb_ref, kb_ref, vb_ref, gbb_ref, gamb_ref, gamtb_ref)}
    streams = [(d == 0, h) for d in range(2) for h in range(DN_HEADS)]
    hs = lambda h: slice(h * dh, (h + 1) * dh)
    items = [(r, f, h, j, (j if f else DN_SUBS - 1 - j)) for j in range(DN_SUBS) for r, (f, h) in enumerate(streams)]
    rows = lambda blk: slice(blk * c, (blk + 1) * c)
    rng = range(len(items))

    q = [srcs[f][0][rows(bk), hs(h)] for _, f, h, _, bk in items]
    k = [srcs[f][1][rows(bk), hs(h)] for _, f, h, _, bk in items]
    v = [srcs[f][2][rows(bk), hs(h)] for _, f, h, _, bk in items]
    beta = [srcs[f][3][rows(bk), DN_STREAMS + r:DN_STREAMS + r + 1] for r, f, _, _, bk in items]
    gcol = [srcs[f][4][rows(bk), r:r + 1] for r, f, _, _, bk in items]
    grow = [srcs[f][5][bk, r:r + 1, :] for r, f, _, _, bk in items]
    g_last = [gcol[i][c - 1:c, :] if items[i][1] else gcol[i][0:1, :] for i in rng]

    e = [jnp.exp(jnp.where(incl[items[i][1]], gcol[i] - grow[i], 0.0)) for i in rng]
    kb = [k[i] * beta[i] for i in rng]
    k16 = [k[i].astype(BF16) for i in rng]
    m = [_nt_dot(kb[i].astype(BF16), k16[i]) * jnp.where(strict[items[i][1]], e[i], 0.0) for i in rng]
    attn = [(_nt_dot(q[i].astype(BF16), k16[i]) * jnp.where(incl[items[i][1]], e[i], 0.0)).astype(BF16) for i in rng]
    t = _unit_triangular_inverses(m, ii, jj)
    eg = [jnp.exp(gcol[i]) for i in rng]
    sol = [_dot16(t[i], jnp.concatenate([v[i] * beta[i], kb[i] * eg[i]], axis=-1)) for i in rng]
    u = [sol[i][:, :dh] for i in rng]
    w16 = [sol[i][:, dh:].astype(BF16) for i in rng]
    q_dec = [(q[i] * eg[i]).astype(BF16) for i in rng]
    k_dec = [(k[i] * jnp.exp(g_last[i] - gcol[i])).astype(BF16) for i in rng]
    blk_dec = [jnp.exp(g_last[i]) for i in rng]

    s = [s_scr[r] for r in range(DN_STREAMS)]
    for j in range(DN_SUBS):
        idx = [i for i in rng if items[i][3] == j]
        s16 = [s[r].astype(BF16) for r in range(DN_STREAMS)]
        v_new = [(u[i] - jnp.dot(w16[i], s16[items[i][0]], preferred_element_type=F32)).astype(BF16) for i in idx]
        o = [jnp.dot(q_dec[i], s16[items[i][0]], preferred_element_type=F32)
             + jnp.dot(attn[i], vn, preferred_element_type=F32) for i, vn in zip(idx, v_new)]
        s = [s[items[i][0]] * blk_dec[i] + _tn_dot(k_dec[i], vn) for i, vn in zip(idx, v_new)]
        for i, oi in zip(idx, o):
            _, f, h, _, bk = items[i]
            (of_ref if f else ob_ref)[rows(bk), hs(h)] = oi
    for r in range(DN_STREAMS):
        s_scr[r] = s[r]

    @pl.when(n == pl.num_programs(1) - 1)
    def _():
        sf_ref[...] = s_scr[...]


def dn_scan(q, k, v, gb, gam, gamt, s0):
    b, l, w = q.shape
    c = DN_BLOCK * DN_SUBS
    n = l // c
    lanes = gb.shape[-1]
    f3 = lambda bi, ni: (bi, ni, 0)
    b3 = lambda bi, ni: (bi, n - 1 - ni, 0)
    f4 = lambda bi, ni: (bi, ni, 0, 0)
    b4 = lambda bi, ni: (bi, n - 1 - ni, 0, 0)
    st = lambda bi, ni: (bi, 0, 0, 0)

    def specs(m3, m4):
        return [pl.BlockSpec((None, c, w), m3)] * 3 + [pl.BlockSpec((None, c, lanes), m3)] * 2 + [
            pl.BlockSpec((None, DN_SUBS, DN_STREAMS, DN_BLOCK), m4)]

    return pl.pallas_call(
        _dn_scan_body,
        grid=(b, n),
        in_specs=specs(f3, f4) + specs(b3, b4) + [pl.BlockSpec((None, DN_STREAMS, DN_DIM, DN_DIM), st)],
        out_specs=[pl.BlockSpec((None, c, w), f3), pl.BlockSpec((None, c, w), b3),
                   pl.BlockSpec((None, DN_STREAMS, DN_DIM, DN_DIM), st)],
        out_shape=[jax.ShapeDtypeStruct((b, l, w), F32), jax.ShapeDtypeStruct((b, l, w), F32),
                   jax.ShapeDtypeStruct((b, DN_STREAMS, DN_DIM, DN_DIM), F32)],
        scratch_shapes=[pltpu.VMEM((DN_STREAMS, DN_DIM, DN_DIM), F32)],
        compiler_params=_cparams(("parallel", "arbitrary")),
        name="dn_scan",
    )(q, k, v, gb, gam, gamt, q, k, v, gb, gam, gamt, s0)


def _dn_final_body(of_ref, ob_ref, g_ref, w_ref, y_ref):
    for h in range(DN_HEADS):
        sl = slice(h * DN_DIM, (h + 1) * DN_DIM)
        o = of_ref[:, sl] + ob_ref[:, sl]
        y = _head_rms(o, w_ref[...])
        g = g_ref[:, sl]
        y_ref[:, sl] = (y * (g * jax.nn.sigmoid(g))).astype(y_ref.dtype)


def dn_finalize(o_f, o_b, p, gate_col, norm_w, tl):
    b, l, w = o_f.shape
    tok = lambda bi, i: (bi, i, 0)
    return pl.pallas_call(
        _dn_final_body,
        grid=(b, l // tl),
        in_specs=[pl.BlockSpec((None, tl, w), tok), pl.BlockSpec((None, tl, w), tok),
                  pl.BlockSpec((None, tl, w), lambda bi, i: (bi, i, gate_col // w)),
                  pl.BlockSpec((1, DN_DIM), lambda bi, i: (0, 0))],
        out_specs=pl.BlockSpec((None, tl, w), tok),
        out_shape=jax.ShapeDtypeStruct((b, l, w), BF16),
        compiler_params=_cparams(("parallel", "parallel")),
        name="dn_finalize",
    )(o_f, o_b, p, norm_w.reshape(1, DN_DIM))


def deltanet_pallas(p_lat, p_ctx, conv_w, a_log, dt_bias, norm_w, need_ctx, ab_col=3072, gate_col=1536):
    b = p_lat.shape[0]
    n_ctx = p_ctx.shape[1]
    s0 = jnp.zeros((b, DN_STREAMS, DN_DIM, DN_DIM), F32)
    fc = dn_prep(p_ctx, ab_col, conv_w, a_log, dt_bias, n_ctx)
    oc_f, oc_b, s_ctx = dn_scan(*fc, s0)
    fl = dn_prep(p_lat, ab_col, conv_w, a_log, dt_bias, PREP_TILE)
    ol_f, ol_b, _ = dn_scan(*fl, s_ctx)
    out_lat = dn_finalize(ol_f, ol_b, p_lat, gate_col, norm_w, PREP_TILE)
    out_ctx = dn_finalize(oc_f, oc_b, p_ctx, gate_col, norm_w, n_ctx) if need_ctx else None
    return out_lat, out_ctx


def _hy_factors(n):
    n2 = 256 if n >= 32768 else 128
    return n // n2, n2


def _bf16_const(m):
    return m.astype(ml_dtypes.bfloat16)


def _hy_constants(l):
    n = 2 * l
    n1, n2 = _hy_factors(n)
    a1 = 2.0 * np.pi * np.outer(np.arange(n1), np.arange(n1)) / n1
    c1, s1 = np.cos(a1), np.sin(a1)
    h = n1 // 2
    rows_fwd = np.block([[c1[:, :h], s1[:, :h]], [-s1[:, :h], c1[:, :h]]])
    rows_taps = np.concatenate([c1, -s1], axis=0)
    rows_inv = np.block([[c1[:h], -s1[:h]], [s1[:h], c1[:h]]]) / n
    a2 = 2.0 * np.pi * np.outer(np.arange(n2), np.arange(n2)) / n2
    c2, s2 = np.cos(a2), np.sin(a2)
    slab_fwd = np.block([[c2, s2], [-s2, c2]])
    slab_inv = np.block([[c2, -s2], [s2, c2]])
    th = 2.0 * np.pi * np.outer(np.arange(n1), np.arange(n2)) / n
    lanes = lambda t: jnp.broadcast_to(jnp.asarray(t, F32)[:, :, None], (n1, n2, 128))
    return dict(n1=n1, n2=n2, rows_fwd=_bf16_const(rows_fwd), rows_taps=_bf16_const(rows_taps),
                rows_inv=_bf16_const(rows_inv), slab_fwd=_bf16_const(slab_fwd), slab_inv=_bf16_const(slab_inv),
                tw_cos=lanes(np.cos(th)), tw_sin=lanes(np.sin(th)))


def _hy_prep_body(x_ref, xp_ref, xn_ref, w_ref, v_ref, x1_ref, x2_ref):
    y = _conv3(x_ref, xp_ref, xn_ref, w_ref)
    v_ref[...] = y[:, :HY_CH]
    x1_ref[...] = y[:, HY_CH:2 * HY_CH]
    x2_ref[...] = y[:, 2 * HY_CH:]


def hy_prep(p, col0, conv_w, tl):
    b, l, _ = p.shape
    c3 = 3 * HY_CH
    nblk8 = l // 8
    cb = col0 // c3
    tok = lambda bi, i: (bi, i, 0)
    return pl.pallas_call(
        _hy_prep_body,
        grid=(b, l // tl),
        in_specs=[pl.BlockSpec((None, tl, c3), lambda bi, i: (bi, i, cb)),
                  pl.BlockSpec((None, 8, c3), lambda bi, i: (bi, jnp.maximum(i * (tl // 8) - 1, 0), cb)),
                  pl.BlockSpec((None, 8, c3), lambda bi, i: (bi, jnp.minimum((i + 1) * (tl // 8), nblk8 - 1), cb)),
                  pl.BlockSpec((3, c3), lambda bi, i: (0, 0))],
        out_specs=[pl.BlockSpec((None, tl, HY_CH), tok)] * 3,
        out_shape=[jax.ShapeDtypeStruct((b, l, HY_CH), F32)] * 3,
        compiler_params=_cparams(("parallel", "parallel")),
        name="hy_prep",
    )(p, p, p, conv_w)


def _split_bf16(a):
    hi = a.astype(BF16)
    return hi, (a - hi.astype(F32)).astype(BF16)


def _hy_filter_body(t_ref, zt_ref, w1t_ref, b1_ref, w2t_ref, b2_ref, w3_ref, rates_ref, *out_refs, zero_row):
    taps_refs, sum_ref = out_refs[:-1], out_refs[-1]
    i = pl.program_id(0)
    hp = lax.Precision.HIGHEST
    h = jnp.sin(HY_SIN_FREQ * (jnp.dot(w1t_ref[...], zt_ref[...], preferred_element_type=F32, precision=hp)
                               + b1_ref[...]))
    h = jnp.sin(HY_SIN_FREQ * (jnp.dot(w2t_ref[...], h, preferred_element_type=F32, precision=hp) + b2_ref[...]))
    h_hi, h_lo = _split_bf16(h)
    w_hi, w_lo = _split_bf16(w3_ref[...])
    h = _tn_dot(h_hi, w_hi) + (_tn_dot(h_hi, w_lo) + _tn_dot(h_lo, w_hi))
    h = h * jnp.exp(-t_ref[...] * rates_ref[...])

    @pl.when(i == 0)
    def _():
        sum_ref[...] = jnp.zeros_like(sum_ref)

    sum_ref[...] += jnp.sum(jnp.abs(h), axis=0, keepdims=True)
    tr = h.shape[0]
    row = i * tr + lax.broadcasted_iota(jnp.int32, (tr, 1), 0)
    h = jnp.where(row == zero_row, 0.0, h)
    for o in range(HY_ORDER):
        taps_refs[o][...] = h[:, o * HY_CH:(o + 1) * HY_CH]


def hy_filter_taps(l, w1, b1, w2, b2, w3, tr=FILTER_ROW_TILE):
    n = 2 * l
    r = np.arange(n)
    pos = np.where(r < l, r, n - r).astype(np.float64)
    pos[l] = 0.0
    t = pos / max(l - 1, 1)
    bands = np.linspace(1e-4, HY_BANDS - 1, HY_BANDS)
    phase = (2.0 * math.pi / l) * pos[:, None] * bands[None, :]
    kf = 32
    z = np.concatenate([t[:, None], np.cos(phase), -np.sin(phase)], axis=-1)
    zt = np.pad(z, ((0, 0), (0, kf - z.shape[1]))).T.astype(np.float32)
    t_col = t[:, None].astype(np.float32)
    w1t = jnp.pad(w1.astype(F32), ((0, kf - w1.shape[0]), (0, 0))).T
    hid = w1.shape[1]
    w3d = w3.astype(F32).reshape(hid, HY_ORDER, 2, HY_CH).transpose(2, 0, 1, 3).reshape(2, hid, HY_ORDER * HY_CH)
    rates = jnp.abs(jnp.linspace(math.log(HY_TARGET) / HY_DECAY_LONG, math.log(HY_TARGET) / HY_DECAY_SHORT,
                                 HY_CH, dtype=F32))
    rates = jnp.tile(rates, HY_ORDER).reshape(1, HY_ORDER * HY_CH)
    fixed = lambda i: (0, 0)
    return pl.pallas_call(
        functools.partial(_hy_filter_body, zero_row=l),
        grid=(n // tr,),
        in_specs=[pl.BlockSpec((tr, 1), lambda i: (i, 0)), pl.BlockSpec((kf, tr), lambda i: (0, i)),
                  pl.BlockSpec((hid, kf), fixed), pl.BlockSpec((hid, 1), fixed),
                  pl.BlockSpec((hid, hid), fixed), pl.BlockSpec((hid, 1), fixed),
                  pl.BlockSpec((None, hid, HY_ORDER * HY_CH), lambda i: (i // (l // tr), 0, 0)),
                  pl.BlockSpec((1, HY_ORDER * HY_CH), fixed)],
        out_specs=[pl.BlockSpec((tr, HY_CH), lambda i: (i, 0))] * HY_ORDER
        + [pl.BlockSpec((1, HY_ORDER * HY_CH), fixed)],
        out_shape=[jax.ShapeDtypeStruct((n, HY_CH), F32)] * HY_ORDER
        + [jax.ShapeDtypeStruct((1, HY_ORDER * HY_CH), F32)],
        compiler_params=_cparams(("arbitrary",)),
        name="hy_filter_taps",
    )(t_col, zt, w1t, b1.astype(F32).reshape(hid, 1), w2.astype(F32).T, b2.astype(F32).reshape(hid, 1), w3d, rates)


HY_ROWS_GROUP = {"fwd": 16, "taps": 16, "inv": 8}


def _hy_rows_body(*refs, mode):
    if mode == "taps":
        m_ref, x_ref, sum_ref, o_ref = refs
        scale = 1.0 / (sum_ref[...] + EPS)
    elif mode == "fwd":
        m_ref, x_ref, o_ref = refs
    else:
        m_ref, x_ref, xn_ref, z_ref, skip_ref, o_ref = refs
    for j in range(HY_ROWS_GROUP[mode]):
        x = jnp.concatenate([x_ref[0, :, j, :], x_ref[1, :, j, :]], axis=0)
        if mode == "taps":
            x = x * scale
        y = _dot16(m_ref[...], x)
        half = y.shape[0] // 2
        for bi in range(2):
            yb = y[bi * half:(bi + 1) * half]
            if mode == "inv":
                yb = xn_ref[bi, :, j, :] * (yb + z_ref[bi, :, j, :] * skip_ref[...])
            o_ref[bi, :, j, :] = yb


def hy_rows_pass(mode, mat, x, extra=()):
    n2, c = x.shape[-2:]
    g = HY_ROWS_GROUP[mode]
    blk = lambda rows: pl.BlockSpec((2, rows, g, c), lambda j: (0, 0, j, 0))
    mspec = _resident(mat.shape)
    vec = pl.BlockSpec((1, c), lambda j: (0, 0))
    if mode == "taps":
        n1 = 2 * x.shape[1]
        in_specs = [mspec, blk(n1 // 2), vec]
        out_rows = n1
    elif mode == "fwd":
        n1 = 2 * x.shape[1]
        in_specs = [mspec, blk(n1 // 2)]
        out_rows = n1
    else:
        n1 = x.shape[1]
        in_specs = [mspec, blk(n1), blk(n1 // 2), blk(n1 // 2), vec]
        out_rows = n1 // 2
    return pl.pallas_call(
        functools.partial(_hy_rows_body, mode=mode),
        grid=(n2 // g,),
        in_specs=in_specs,
        out_specs=blk(out_rows),
        out_shape=jax.ShapeDtypeStruct((2, out_rows, n2, c), F32),
        compiler_params=_cparams(("parallel",)),
        name="hy_rows_" + mode,
    )(mat, x, *extra)


def _hy_slab_body(*refs, with_filter):
    if with_filter:
        f_ref, i_ref, tc_ref, ts_ref, a_ref, h_ref, o_ref = refs
    else:
        f_ref, tc_ref, ts_ref, a_ref, o_ref = refs
    reps = a_ref.shape[-1] // tc_ref.shape[-1]
    n2 = a_ref.shape[2]
    for s in range(a_ref.shape[1]):
        tc = jnp.concatenate([tc_ref[s]] * reps, axis=1)
        ts = jnp.concatenate([ts_ref[s]] * reps, axis=1)
        ar, ai = a_ref[0, s], a_ref[1, s]
        x = _dot16(f_ref[...], jnp.concatenate([ar * tc + ai * ts, ai * tc - ar * ts], axis=0))
        xr, xi = x[:n2], x[n2:]
        if not with_filter:
            o_ref[0, s] = xr.astype(o_ref.dtype)
            o_ref[1, s] = xi.astype(o_ref.dtype)
            continue
        hr, hi = h_ref[0, s].astype(F32), h_ref[1, s].astype(F32)
        y = _dot16(i_ref[...], jnp.concatenate([xr * hr - xi * hi, xr * hi + xi * hr], axis=0))
        yr, yi = y[:n2], y[n2:]
        o_ref[0, s] = yr * tc - yi * ts
        o_ref[1, s] = yi * tc + yr * ts


HY_SLABS_PER_STEP = 4


def hy_slab_pass(consts, a, h=None):
    _, n1, n2, c = a.shape
    kb = HY_SLABS_PER_STEP
    fwd = consts["slab_fwd"]
    mspec = _resident(fwd.shape)
    tw = pl.BlockSpec((kb, n2, 128), lambda k: (k, 0, 0))
    slab = pl.BlockSpec((2, kb, n2, c), lambda k: (0, k, 0, 0))
    if h is None:
        args, in_specs = [fwd, consts["tw_cos"], consts["tw_sin"], a], [mspec, tw, tw, slab]
    else:
        args = [fwd, consts["slab_inv"], consts["tw_cos"], consts["tw_sin"], a, h]
        in_specs = [mspec, mspec, tw, tw, slab, slab]
    return pl.pallas_call(
        functools.partial(_hy_slab_body, with_filter=h is not None),
        grid=(n1 // kb,),
        in_specs=in_specs,
        out_specs=slab,
        out_shape=jax.ShapeDtypeStruct(a.shape, F32 if h is not None else BF16),
        compiler_params=_cparams(("parallel",)),
        name="hy_slab_conv" if h is not None else "hy_slab_fwd",
    )(*args)


def hyena_pallas(p, conv_w, w1, b1, w2, b2, w3, skip, col0=RET_COLS):
    b, l, _ = p.shape
    assert b == 2, "the two batch rows are packed as one complex signal"
    c = HY_CH
    consts = _hy_constants(l)
    n1, n2 = consts["n1"], consts["n2"]
    *taps, sums = hy_filter_taps(l, w1, b1, w2, b2, w3)
    spectra = []
    for o in range(HY_ORDER):
        a_h = hy_rows_pass("taps", consts["rows_taps"], taps[o].reshape(2, n1 // 2, n2, c),
                           extra=(sums[:, o * c:(o + 1) * c],))
        spectra.append(hy_slab_pass(consts, a_h))
    split = lambda t: t.reshape(2, n1 // 2, n2, c)
    v, x1, x2 = hy_prep(p, col0, conv_w, PREP_TILE)
    z = split(v)
    for o, xn in enumerate((x1, x2)):
        a = hy_rows_pass("fwd", consts["rows_fwd"], z)
        bm = hy_slab_pass(consts, a, spectra[o])
        z = hy_rows_pass("inv", consts["rows_inv"], bm,
                         extra=(split(xn), z, skip[o].astype(F32).reshape(1, c)))
    return z.reshape(b, l, c)


def _pad_cols(w, mult):
    n = w.shape[-1]
    pad = (-n) % mult
    return jnp.pad(w, ((0, 0), (0, pad))) if pad else w


def kernel(x, c, ctx, c_ctx, mod_w, mod_b, norm_mix_w, norm_ffn_w, ffn_w_in, ffn_w_out,
           ab_w_in, ab_w_out, dn_conv_w, dn_a_log, dn_dt_bias, dn_norm_w, swa_q_norm_w, swa_k_norm_w,
           swa_sink, cd_w_in, cd_w_out, ret_decay_logit, ret_gn_w, hy_conv_w, hy_f_w1, hy_f_b1,
           hy_f_w2, hy_f_b2, hy_f_w3, hy_bias):
    B, L, D = x.shape
    n_ctx = ctx.shape[1]
    rows = L // GRID_W
    c_rows = jnp.zeros((SUBLANES, D), F32).at[:B].set(c).at[B].set(c_ctx)
    mod_all = modulation(c_rows, mod_w, mod_b)
    hid = ffn_w_out.shape[1]
    h_ctx = ctx
    for layer in range(DEPTH):
        need_ctx = layer != DEPTH - 1
        i = layer // 2
        m = mod_all[layer].reshape(SUBLANES, 6, D)
        mod = [m[:B, j][:, None, :] for j in range(6)]
        mod_c = [jnp.broadcast_to(m[B, j][None, None, :], (B, 1, D)) for j in range(6)]
        if layer % 2 == 0:
            w = ab_w_in[i]
            w_in = jnp.concatenate([w[:, :4 * DN_WIDTH], w[:, DN_COLS:], w[:, 4 * DN_WIDTH:DN_COLS]], axis=1)
            w_out = ab_w_out[i]
        else:
            w = cd_w_in[i]
            w_in = jnp.concatenate([w[:, RET_COLS:], w[:, :RET_COLS]], axis=1)
            w_out = cd_w_out[i]
        w_in_p = _pad_cols(w_in, LANES).astype(BF16)
        p_lat = in_projection(x, norm_mix_w[layer], mod[0], mod[1], w_in_p, DENSE_TILE)
        p_ctx = in_projection(h_ctx, norm_mix_w[layer], mod_c[0], mod_c[1], w_in_p, n_ctx)
        if layer % 2 == 0:
            swa0 = 4 * DN_WIDTH
            ab0 = swa0 + (SWA_HEADS + 2 * SWA_KV_HEADS) * SWA_DIM
            a_lat, a_ctx = deltanet_pallas(p_lat, p_ctx, dn_conv_w[i], dn_a_log[i], dn_dt_bias[i],
                                           dn_norm_w[i], need_ctx, ab_col=ab0, gate_col=3 * DN_WIDTH)
            b_lat, b_ctx = window_gqa_pallas(p_lat, p_ctx, swa0, axial_rope_tables(rows), swa_q_norm_w[i],
                                             swa_k_norm_w[i], swa_sink[i], need_ctx)
        else:
            a_lat, a_ctx = retention_pallas(p_lat, p_ctx, ret_decay_logit[i], ret_gn_w[i], need_ctx,
                                            col0=(HY_ORDER + 1) * HY_CH)
            assert not need_ctx, "the last layer's context outputs reach no latent token"
            b_lat = hyena_pallas(p_lat, hy_conv_w[i], hy_f_w1[i], hy_f_b1[i], hy_f_w2[i], hy_f_b2[i],
                                 hy_f_w3[i], hy_bias[i], col0=0)
            b_ctx = None
        wo = w_out.astype(BF16)
        wg = ffn_w_in[layer][:, :hid].astype(BF16)
        wu = ffn_w_in[layer][:, hid:].astype(BF16)
        wd = ffn_w_out[layer].astype(BF16)
        x = out_projection_ffn(x, a_lat, b_lat, wo, mod[2], norm_ffn_w[layer], mod[3], mod[4], mod[5],
                               wg, wu, wd, DENSE_TILE, FFN_HIDDEN_CHUNK)
        if need_ctx:
            h_ctx = out_projection_ffn(h_ctx, a_ctx, b_ctx, wo, mod_c[2], norm_ffn_w[layer], mod_c[3], mod_c[4],
                                       mod_c[5], wg, wu, wd, n_ctx, FFN_HIDDEN_CHUNK)
    return x
```

```python
import functools
import math

import jax
import jax.numpy as jnp
import ml_dtypes
import numpy as np
from jax import lax
from jax.experimental import pallas as pl
from jax.experimental.pallas import tpu as pltpu

F32 = jnp.float32
BF16 = jnp.bfloat16
EPS = 1e-6

DEPTH = 2
GRID_W = 64

DN_HEADS = 4
DN_DIM = 128
DN_WIDTH = DN_HEADS * DN_DIM
DN_COLS = 4 * DN_WIDTH + 4 * DN_HEADS
SWA_HEADS = 4
SWA_KV_HEADS = 2
SWA_DIM = 128
SWA_BLOCK = 128
ROPE_THETA = 10000.0
RET_HEADS = 4
RET_DIM = 128
RET_WIDTH = RET_HEADS * RET_DIM
RET_CHUNK = 128
RET_COLS = 4 * RET_WIDTH
HY_CH = 512
HY_ORDER = 2
HY_BANDS = 8
HY_SIN_FREQ = 1.0
HY_TARGET = 1e-2
HY_DECAY_SHORT = 0.3
HY_DECAY_LONG = 1.5

VMEM_LIMIT_BYTES = 56 * 1024 * 1024
SUBLANES, LANES = 8, 128

DENSE_TILE = 512
FFN_HIDDEN_CHUNK = 256
PREP_TILE = 1024
SWA_BLOCKS_PER_STEP = 4
MOD_COL_TILE = 1536


def _cparams(sem):
    return pltpu.CompilerParams(dimension_semantics=sem, vmem_limit_bytes=VMEM_LIMIT_BYTES)


def _mod_body(c_ref, w_ref, b_ref, o_ref):
    a = c_ref[...]
    a = a * jax.nn.sigmoid(a)
    o_ref[...] = jnp.dot(a, w_ref[...], preferred_element_type=F32,
                         precision=lax.Precision.HIGHEST) + b_ref[...]


def modulation(c_rows, mod_w, mod_b):
    depth, d, n = mod_w.shape
    tn = MOD_COL_TILE
    return pl.pallas_call(
        _mod_body,
        grid=(depth, n // tn),
        in_specs=[pl.BlockSpec((8, d), lambda l, j: (0, 0)),
                  pl.BlockSpec((None, d, tn), lambda l, j: (l, 0, j)),
                  pl.BlockSpec((None, 1, tn), lambda l, j: (l, 0, j))],
        out_specs=pl.BlockSpec((None, 8, tn), lambda l, j: (l, 0, j)),
        out_shape=jax.ShapeDtypeStruct((depth, 8, n), F32),
        compiler_params=_cparams(("parallel", "parallel")),
        name="modulation",
    )(c_rows, mod_w, mod_b.reshape(depth, 1, n))


def _norm_mod(x, nw, shift, scale):
    y = x * lax.rsqrt(jnp.mean(x * x, axis=-1, keepdims=True) + EPS)
    return (y * nw) * (1.0 + scale) + shift


def _resident(shape):
    return pl.BlockSpec(shape, lambda *_: (0,) * len(shape), pipeline_mode=pl.Buffered(1))


def _inproj_body(x_ref, nw_ref, shift_ref, scale_ref, w_ref, o_ref):
    h = _norm_mod(x_ref[...], nw_ref[...], shift_ref[...], scale_ref[...]).astype(BF16)
    o_ref[...] = jnp.dot(h, w_ref[...], preferred_element_type=F32)


def in_projection(x, nw, shift, scale, w, tm):
    b, l, d = x.shape
    n = w.shape[1]
    return pl.pallas_call(
        _inproj_body,
        grid=(b, l // tm),
        in_specs=[pl.BlockSpec((None, tm, d), lambda bi, i: (bi, i, 0)),
                  _resident((1, d)),
                  pl.BlockSpec((None, 1, d), lambda bi, i: (bi, 0, 0)),
                  pl.BlockSpec((None, 1, d), lambda bi, i: (bi, 0, 0)),
                  _resident((d, n))],
        out_specs=pl.BlockSpec((None, tm, n), lambda bi, i: (bi, i, 0)),
        out_shape=jax.ShapeDtypeStruct((b, l, n), F32),
        compiler_params=_cparams(("parallel", "parallel")),
        name="in_projection",
    )(x, nw.reshape(1, d), shift, scale, w)


def _outffn_body(x_ref, oa_ref, ob_ref, wo_ref, g2_ref, nw_ref, sh_ref, sc_ref, g5_ref,
                 wg_ref, wu_ref, wd_ref, y_ref, x1_scr, h_scr, acc_scr, *, th):
    wa = oa_ref.shape[1]
    mix = (jnp.dot(oa_ref[...].astype(BF16), wo_ref[:wa, :], preferred_element_type=F32)
           + jnp.dot(ob_ref[...].astype(BF16), wo_ref[wa:, :], preferred_element_type=F32))
    x1 = x_ref[...] + g2_ref[...] * mix
    x1_scr[...] = x1
    h_scr[...] = _norm_mod(x1, nw_ref[...], sh_ref[...], sc_ref[...]).astype(BF16)
    for k in range(wg_ref.shape[1] // th):
        ks = slice(k * th, (k + 1) * th)
        g = jnp.dot(h_scr[...], wg_ref[:, ks], preferred_element_type=F32)
        u = jnp.dot(h_scr[...], wu_ref[:, ks], preferred_element_type=F32)
        a = (g * jax.nn.sigmoid(g) * u).astype(BF16)
        part = jnp.dot(a, wd_ref[ks, :], preferred_element_type=F32)
        if k == 0:
            acc_scr[...] = part
        else:
            acc_scr[...] += part
    y_ref[...] = x1_scr[...] + g5_ref[...] * acc_scr[...]


def out_projection_ffn(x, oa, ob, wo, g2, nw, shift, scale, g5, wg, wu, wd, tm, th):
    b, l, d = x.shape
    wa, wb = oa.shape[-1], ob.shape[-1]
    hid = wg.shape[1]
    tok = lambda bi, i: (bi, i, 0)
    vec = pl.BlockSpec((None, 1, d), lambda bi, i: (bi, 0, 0))
    return pl.pallas_call(
        functools.partial(_outffn_body, th=th),
        grid=(b, l // tm),
        in_specs=[pl.BlockSpec((None, tm, d), tok),
                  pl.BlockSpec((None, tm, wa), tok),
                  pl.BlockSpec((None, tm, wb), tok),
                  _resident((d, d)), vec, _resident((1, d)), vec, vec, vec,
                  _resident((d, hid)), _resident((d, hid)), _resident((hid, d))],
        out_specs=pl.BlockSpec((None, tm, d), tok),
        out_shape=jax.ShapeDtypeStruct((b, l, d), F32),
        scratch_shapes=[pltpu.VMEM((tm, d), F32), pltpu.VMEM((tm, d), BF16), pltpu.VMEM((tm, d), F32)],
        compiler_params=_cparams(("parallel", "parallel")),
        name="out_projection_ffn",
    )(x, oa, ob, wo, g2, nw.reshape(1, d), shift, scale, g5, wg, wu, wd)


NEG_BIG = -1e30


def _head_rms(x, w):
    return x * lax.rsqrt(jnp.mean(x * x, axis=-1, keepdims=True) + EPS) * w


def _swa_prep_body(*refs, use_rope):
    if use_rope:
        q_ref, k_ref, v_ref, qw_ref, kw_ref, rt_ref, ct_ref, qo_ref, ko_ref, vo_ref = refs
        tl = q_ref.shape[0]
        nr = tl // GRID_W

        def table(kind):
            rowp = jnp.broadcast_to(rt_ref[kind][:, None, :], (nr, GRID_W, SWA_DIM))
            colp = jnp.broadcast_to(ct_ref[kind][None, :, :], (nr, GRID_W, SWA_DIM))
            return (rowp + colp).reshape(tl, SWA_DIM)

        cos, sa, sb = table(0), table(1), table(2)
    else:
        q_ref, k_ref, v_ref, qw_ref, kw_ref, qo_ref, ko_ref, vo_ref = refs

    def prep(x, w):
        y = _head_rms(x, w)
        if use_rope:
            y = y * cos + pltpu.roll(y, SWA_DIM - 32, 1) * sa + pltpu.roll(y, 32, 1) * sb
        return y.astype(BF16)

    for h in range(SWA_HEADS):
        sl = slice(h * SWA_DIM, (h + 1) * SWA_DIM)
        qo_ref[:, sl] = prep(q_ref[:, sl], qw_ref[...])
    for h in range(SWA_KV_HEADS):
        sl = slice(h * SWA_DIM, (h + 1) * SWA_DIM)
        ko_ref[:, sl] = prep(k_ref[:, sl], kw_ref[...])
    vo_ref[...] = v_ref[...].astype(BF16)


def swa_prep(p, col0, qw, kw, rope, tl):
    b, l, _ = p.shape
    nq, nk = SWA_HEADS * SWA_DIM, SWA_KV_HEADS * SWA_DIM
    tok = lambda bi, i: (bi, i, 0)
    in_specs = [pl.BlockSpec((None, tl, nq), lambda bi, i: (bi, i, col0 // nq)),
                pl.BlockSpec((None, tl, nk), lambda bi, i: (bi, i, (col0 + nq) // nk)),
                pl.BlockSpec((None, tl, nk), lambda bi, i: (bi, i, (col0 + nq + nk) // nk)),
                pl.BlockSpec((1, SWA_DIM), lambda bi, i: (0, 0)),
                pl.BlockSpec((1, SWA_DIM), lambda bi, i: (0, 0))]
    args = [p, p, p, qw.reshape(1, SWA_DIM), kw.reshape(1, SWA_DIM)]
    if rope is not None:
        in_specs += [pl.BlockSpec((3, tl // GRID_W, SWA_DIM), lambda bi, i: (0, i, 0)),
                     pl.BlockSpec((3, GRID_W, SWA_DIM), lambda bi, i: (0, 0, 0))]
        args += list(rope)
    return pl.pallas_call(
        functools.partial(_swa_prep_body, use_rope=rope is not None),
        grid=(b, l // tl),
        in_specs=in_specs,
        out_specs=[pl.BlockSpec((None, tl, nq), tok), pl.BlockSpec((None, tl, nk), tok),
                   pl.BlockSpec((None, tl, nk), tok)],
        out_shape=[jax.ShapeDtypeStruct((b, l, nq), BF16), jax.ShapeDtypeStruct((b, l, nk), BF16),
                   jax.ShapeDtypeStruct((b, l, nk), BF16)],
        compiler_params=_cparams(("parallel", "parallel")),
        name="swa_prep",
    )(*args)


def _nt_dot(a, b):
    return lax.dot_general(a, b, (((1,), (1,)), ((), ())), preferred_element_type=F32)


def _swa_attend(q, keys, vals, masks, sink_col):
    scale = SWA_DIM ** -0.5
    scores = []
    for kk, mask in zip(keys, masks):
        s = _nt_dot(q, kk) * scale
        if mask is not None:
            s = jnp.where(mask, s, NEG_BIG)
        scores.append(s)
    m = sink_col
    for s in scores:
        m = jnp.maximum(m, jnp.max(s, axis=-1, keepdims=True))
    den = jnp.exp(sink_col - m)
    acc = None
    for s, vv in zip(scores, vals):
        pr = jnp.exp(s - m)
        den = den + jnp.sum(pr, axis=-1, keepdims=True)
        o = jnp.dot(pr.astype(BF16), vv, preferred_element_type=F32)
        acc = o if acc is None else acc + o
    return acc / den


def _swa_attn_body(sink_ref, q_ref, kp_ref, km_ref, kn_ref, vp_ref, vm_ref, vn_ref, kc_ref, vc_ref,
                   o_ref, *, nb):
    i = pl.program_id(1)
    n_blocks = pl.num_programs(1) * nb
    w, d = SWA_BLOCK, SWA_DIM
    grp = SWA_HEADS // SWA_KV_HEADS
    rows = lax.broadcasted_iota(jnp.int32, (grp * w, w), 0)
    qi = rows % w
    kj = lax.broadcasted_iota(jnp.int32, (grp * w, w), 1)
    for r in range(nb):
        blk = i * nb + r
        mask_prev = (kj >= qi) & (blk > 0)
        mask_next = (kj <= qi) & (blk < n_blocks - 1)
        rs = slice(r * w, (r + 1) * w)
        for h in range(SWA_KV_HEADS):
            hs = slice(h * d, (h + 1) * d)
            q = jnp.concatenate([q_ref[rs, (h * grp + g) * d:(h * grp + g + 1) * d] for g in range(grp)], axis=0)
            sink_col = jnp.zeros((grp * w, 1), F32)
            for g in range(grp):
                sink_col = jnp.where((rows[:, :1] // w) == g, sink_ref[h * grp + g], sink_col)
            if r > 0:
                k_prev, v_prev = km_ref[(r - 1) * w:r * w, hs], vm_ref[(r - 1) * w:r * w, hs]
            else:
                k_prev, v_prev = kp_ref[:, hs], vp_ref[:, hs]
            if r < nb - 1:
                k_next, v_next = km_ref[(r + 1) * w:(r + 2) * w, hs], vm_ref[(r + 1) * w:(r + 2) * w, hs]
            else:
                k_next, v_next = kn_ref[:, hs], vn_ref[:, hs]
            o = _swa_attend(q, [k_prev, km_ref[rs, hs], k_next, kc_ref[:, hs]],
                            [v_prev, vm_ref[rs, hs], v_next, vc_ref[:, hs]],
                            [mask_prev, None, mask_next, None], sink_col)
            for g in range(grp):
                o_ref[rs, (h * grp + g) * d:(h * grp + g + 1) * d] = o[g * w:(g + 1) * w].astype(o_ref.dtype)


def swa_attention(q, k, v, kc, vc, sink, nb):
    b, l, nq = q.shape
    nk = k.shape[-1]
    n_ctx = kc.shape[1]
    w = SWA_BLOCK
    tq = nb * w
    last = l // w - 1
    main = lambda bi, i, s: (bi, i, 0)
    prev = lambda bi, i, s: (bi, jnp.maximum(i * nb - 1, 0), 0)
    nxt = lambda bi, i, s: (bi, jnp.minimum((i + 1) * nb, last), 0)
    cx = lambda bi, i, s: (bi, 0, 0)
    return pl.pallas_call(
        functools.partial(_swa_attn_body, nb=nb),
        grid_spec=pltpu.PrefetchScalarGridSpec(
            num_scalar_prefetch=1,
            grid=(b, l // tq),
            in_specs=[pl.BlockSpec((None, tq, nq), main),
                      pl.BlockSpec((None, w, nk), prev), pl.BlockSpec((None, tq, nk), main),
                      pl.BlockSpec((None, w, nk), nxt),
                      pl.BlockSpec((None, w, nk), prev), pl.BlockSpec((None, tq, nk), main),
                      pl.BlockSpec((None, w, nk), nxt),
                      pl.BlockSpec((None, n_ctx, nk), cx), pl.BlockSpec((None, n_ctx, nk), cx)],
            out_specs=pl.BlockSpec((None, tq, nq), main)),
        out_shape=jax.ShapeDtypeStruct((b, l, nq), BF16),
        compiler_params=_cparams(("parallel", "parallel")),
        name="swa_attention",
    )(sink, q, k, k, k, v, v, v, kc, vc)


def _ctx_attn_body(sink_ref, q_ref, kc_ref, vc_ref, o_ref):
    n_ctx, d = q_ref.shape[0], SWA_DIM
    grp = SWA_HEADS // SWA_KV_HEADS
    rows = lax.broadcasted_iota(jnp.int32, (grp * n_ctx, 1), 0)
    for h in range(SWA_KV_HEADS):
        hs = slice(h * d, (h + 1) * d)
        q = jnp.concatenate([q_ref[:, (h * grp + g) * d:(h * grp + g + 1) * d] for g in range(grp)], axis=0)
        sink_col = jnp.zeros((grp * n_ctx, 1), F32)
        for g in range(grp):
            sink_col = jnp.where((rows // n_ctx) == g, sink_ref[h * grp + g], sink_col)
        o = _swa_attend(q, [kc_ref[:, hs]], [vc_ref[:, hs]], [None], sink_col)
        for g in range(grp):
            o_ref[:, (h * grp + g) * d:(h * grp + g + 1) * d] = o[g * n_ctx:(g + 1) * n_ctx].astype(o_ref.dtype)


def ctx_attention(qc, kc, vc, sink):
    b, n_ctx, nq = qc.shape
    nk = kc.shape[-1]
    cx = lambda bi, s: (bi, 0, 0)
    return pl.pallas_call(
        _ctx_attn_body,
        grid_spec=pltpu.PrefetchScalarGridSpec(
            num_scalar_prefetch=1, grid=(b,),
            in_specs=[pl.BlockSpec((None, n_ctx, nq), cx), pl.BlockSpec((None, n_ctx, nk), cx),
                      pl.BlockSpec((None, n_ctx, nk), cx)],
            out_specs=pl.BlockSpec((None, n_ctx, nq), cx)),
        out_shape=jax.ShapeDtypeStruct((b, n_ctx, nq), BF16),
        compiler_params=_cparams(("parallel",)),
        name="ctx_attention",
    )(sink, qc, kc, vc)


def axial_rope_tables(rows):
    m = SWA_DIM // 4
    inv = ROPE_THETA ** (-np.arange(m, dtype=np.float64) / m)

    def tables(count, lane0):
        ang = np.arange(count, dtype=np.float64)[:, None] * inv
        out = np.zeros((3, count, SWA_DIM))
        out[0, :, lane0:lane0 + m] = out[0, :, lane0 + m:lane0 + 2 * m] = np.cos(ang)
        out[1, :, lane0:lane0 + m] = -np.sin(ang)
        out[2, :, lane0 + m:lane0 + 2 * m] = np.sin(ang)
        return out.astype(np.float32)

    return tables(rows, 0), tables(GRID_W, SWA_DIM // 2)


def window_gqa_pallas(p_lat, p_ctx, col0, rope, q_norm_w, k_norm_w, sink, need_ctx):
    ql, kl, vl = swa_prep(p_lat, col0, q_norm_w, k_norm_w, rope, PREP_TILE)
    qc, kc, vc = swa_prep(p_ctx, col0, q_norm_w, k_norm_w, None, p_ctx.shape[1])
    o_lat = swa_attention(ql, kl, vl, kc, vc, sink, SWA_BLOCKS_PER_STEP)
    o_ctx = ctx_attention(qc, kc, vc, sink) if need_ctx else None
    return o_lat, o_ctx


def _tn_dot(a, b):
    return lax.dot_general(a, b, (((0,), (0,)), ((), ())), preferred_element_type=F32)


def _log_sigmoid(x):
    return jnp.minimum(x, 0.0) - jnp.log(1.0 + jnp.exp(-jnp.abs(x)))


RET_STREAMS = 2 * RET_HEADS


RET_SUBS = 4


def _ret_scan_body(logit_ref, qf_ref, kf_ref, vf_ref, rotf_ref, qb_ref, kb_ref, vb_ref, rotb_ref, rotc_ref,
                   s0_ref, of_ref, ob_ref, sf_ref, s_scr):
    n = pl.program_id(1)
    c, dh = RET_CHUNK, RET_DIM

    @pl.when(n == 0)
    def _():
        s_scr[...] = s0_ref[...]

    ii = lax.broadcasted_iota(jnp.int32, (c, c), 0)
    jj = lax.broadcasted_iota(jnp.int32, (c, c), 1)
    i1 = lax.broadcasted_iota(jnp.int32, (c, 1), 0)
    rel = {True: (ii - jj).astype(F32), False: (jj - ii).astype(F32)}
    pos = {True: i1.astype(F32), False: (c - 1 - i1).astype(F32)}
    streams = [(d == 0, h) for d in range(2) for h in range(RET_HEADS)]
    srcs = {True: (qf_ref, kf_ref, vf_ref, rotf_ref), False: (qb_ref, kb_ref, vb_ref, rotb_ref)}
    subs = rotf_ref.shape[0]
    items = [(r, f, h, j, (j if f else subs - 1 - j)) for j in range(subs) for r, (f, h) in enumerate(streams)]
    rows = lambda blk: slice(blk * c, (blk + 1) * c)
    hs = lambda h: slice(h * dh, (h + 1) * dh)
    rng = range(len(items))

    cos_i, sin_i = rotc_ref[0], rotc_ref[1]
    tables = {}
    for f in (True, False):
        for bk in range(subs):
            a = srcs[f][3][bk]
            tables[f, bk] = (a[0:1] * cos_i - a[1:2] * sin_i, a[2:3] * cos_i + a[3:4] * sin_i)

    def rot(x, f, bk):
        cos_t, sin_t = tables[f, bk]
        return x * cos_t + pltpu.roll(x, dh // 2, 1) * sin_t

    lg = [_log_sigmoid(jnp.full((1, 1), logit_ref[r], F32)) for r in range(RET_STREAMS)]
    q = [rot(srcs[f][0][rows(bk), hs(h)], f, bk) for _, f, h, _, bk in items]
    k = [rot(srcs[f][1][rows(bk), hs(h)], f, bk) * dh ** -0.5 for _, f, h, _, bk in items]
    v = [srcs[f][2][rows(bk), hs(h)].astype(BF16) for _, f, h, _, bk in items]
    dmat = [jnp.where(rel[f] >= 0, jnp.exp(lg[r] * jnp.maximum(rel[f], 0.0)), 0.0)
            for r, (f, _) in enumerate(streams)]
    scores = [(_nt_dot(q[i].astype(BF16), k[i].astype(BF16)) * dmat[items[i][0]]).astype(BF16) for i in rng]
    o_in = [jnp.dot(scores[i], v[i], preferred_element_type=F32) for i in rng]
    q_dec = [(q[i] * jnp.exp(lg[items[i][0]] * (pos[items[i][1]] + 1.0))).astype(BF16) for i in rng]
    k_dec = [(k[i] * jnp.exp(lg[items[i][0]] * (c - 1.0 - pos[items[i][1]]))).astype(BF16) for i in rng]
    kv = [_tn_dot(k_dec[i], v[i]) for i in rng]
    chunk_dec = [jnp.exp(lg[r] * c) for r in range(RET_STREAMS)]

    s = [s_scr[r] for r in range(RET_STREAMS)]
    for j in range(subs):
        idx =[i for i in rng if items[i][3] == j]
        o = [o_in[i] + jnp.dot(q_dec[i], s[items[i][0]].astype(BF16), preferred_element_type=F32) for i in idx]
        s = [s[items[i][0]] * chunk_dec[items[i][0]] + kv[i] for i in idx]
        for i, oi in zip(idx, o):
            _, f, h, _, bk = items[i]
            (of_ref if f else ob_ref)[rows(bk), hs(h)] = oi.astype(of_ref.dtype)
    for r in range(RET_STREAMS):
        s_scr[r] = s[r]

    @pl.when(n == pl.num_programs(1) - 1)
    def _():
        sf_ref[...] = s_scr[...]


def retention_scan(p, col0, decay_logit, rot_chunk, rot_in, s0):
    b, l, _ = p.shape
    subs = min(RET_SUBS, l // RET_CHUNK)
    c, dh, w = RET_CHUNK * subs, RET_DIM, RET_WIDTH
    n = l // c
    cb = col0 // w
    fcol = lambda off: (lambda bi, ni, s: (bi, ni, cb + off))
    bcol = lambda off: (lambda bi, ni, s: (bi, n - 1 - ni, cb + off))
    rot = lambda m: pl.BlockSpec((subs, 4, dh), m)
    st = lambda bi, ni, s: (bi, 0, 0, 0)
    tok = lambda col: [pl.BlockSpec((None, c, w), col(j)) for j in range(3)]
    return pl.pallas_call(
        _ret_scan_body,
        grid_spec=pltpu.PrefetchScalarGridSpec(
            num_scalar_prefetch=1,
            grid=(b, n),
            in_specs=tok(fcol) + [rot(lambda bi, ni, s: (ni, 0, 0))]
            + tok(bcol) + [rot(lambda bi, ni, s: (n - 1 - ni, 0, 0))]
            + [pl.BlockSpec((2, RET_CHUNK, dh), lambda bi, ni, s: (0, 0, 0)),
               pl.BlockSpec((None, RET_STREAMS, dh, dh), st)],
            out_specs=[pl.BlockSpec((None, c, w), lambda bi, ni, s: (bi, ni, 0)),
                       pl.BlockSpec((None, c, w), lambda bi, ni, s: (bi, n - 1 - ni, 0)),
                       pl.BlockSpec((None, RET_STREAMS, dh, dh), st)],
            scratch_shapes=[pltpu.VMEM((RET_STREAMS, dh, dh), F32)]),
        out_shape=[jax.ShapeDtypeStruct((b, l, w), BF16), jax.ShapeDtypeStruct((b, l, w), BF16),
                   jax.ShapeDtypeStruct((b, RET_STREAMS, dh, dh), F32)],
        compiler_params=_cparams(("parallel", "arbitrary")),
        name="retention_scan",
    )(decay_logit.reshape(-1), p, p, p, rot_chunk, p, p, p, rot_chunk, rot_in, s0)


def _ret_final_body(of_ref, ob_ref, g_ref, w_ref, y_ref):
    for h in range(RET_HEADS):
        sl = slice(h * RET_DIM, (h + 1) * RET_DIM)
        o = of_ref[:, sl].astype(F32) + ob_ref[:, sl].astype(F32)
        mu = jnp.mean(o, axis=-1, keepdims=True)
        var = jnp.mean(jnp.square(o - mu), axis=-1, keepdims=True)
        y = (o - mu) * lax.rsqrt(var + EPS) * w_ref[:, sl]
        g = g_ref[:, sl]
        y_ref[:, sl] = (y * (g * jax.nn.sigmoid(g))).astype(y_ref.dtype)


def retention_finalize(o_f, o_b, p, gate_col, gn_w, tl):
    b, l, w = o_f.shape
    tok = lambda bi, i: (bi, i, 0)
    return pl.pallas_call(
        _ret_final_body,
        grid=(b, l // tl),
        in_specs=[pl.BlockSpec((None, tl, w), tok),
                  pl.BlockSpec((None, tl, w), tok),
                  pl.BlockSpec((None, tl, w), lambda bi, i: (bi, i, gate_col // w)),
                  pl.BlockSpec((1, w), lambda bi, i: (0, 0))],
        out_specs=pl.BlockSpec((None, tl, w), tok),
        out_shape=jax.ShapeDtypeStruct((b, l, w), BF16),
        compiler_params=_cparams(("parallel", "parallel")),
        name="retention_finalize",
    )(o_f, o_b, p, gn_w.reshape(1, w))


def retention_rope_tables(l):
    inv = ROPE_THETA ** (-np.linspace(0.0, 1.0, RET_DIM // 2))
    inv2 = np.concatenate([inv, inv])
    sign = np.concatenate([-np.ones(RET_DIM // 2), np.ones(RET_DIM // 2)])
    a = (RET_CHUNK * np.arange(l // RET_CHUNK, dtype=np.float64))[:, None] * inv2
    b = np.arange(RET_CHUNK, dtype=np.float64)[:, None] * inv2
    chunk = np.stack([np.cos(a), np.sin(a), sign * np.sin(a), sign * np.cos(a)], axis=1)
    return chunk.astype(np.float32), np.stack([np.cos(b), np.sin(b)]).astype(np.float32)


def retention_pallas(p_lat, p_ctx, decay_logit, gn_w, need_ctx, col0=0):
    b, l, _ = p_lat.shape
    n_ctx = p_ctx.shape[1]
    gate_col = col0 + 3 * RET_WIDTH
    s0 = jnp.zeros((b, RET_STREAMS, RET_DIM, RET_DIM), F32)
    oc_f, oc_b, s_ctx = retention_scan(p_ctx, col0, decay_logit, *retention_rope_tables(n_ctx), s0)
    ol_f, ol_b, _ = retention_scan(p_lat, col0, decay_logit, *retention_rope_tables(l), s_ctx)
    out_lat = retention_finalize(ol_f, ol_b, p_lat, gate_col, gn_w, PREP_TILE)
    out_ctx = retention_finalize(oc_f, oc_b, p_ctx, gate_col, gn_w, n_ctx) if need_ctx else None
    return out_lat, out_ctx


DN_BLOCK = 128
DN_STREAMS = 2 * DN_HEADS
DN_SUBS = 4


def _shift_rows(x, prev_row, next_row):
    n = x.shape[0]
    r = lax.broadcasted_iota(jnp.int32, (SUBLANES, 1), 0)
    down, up = pltpu.roll(x, 1, 0), pltpu.roll(x, n - 1, 0)
    x_prev = jnp.concatenate([jnp.where(r == 0, prev_row, down[:SUBLANES]), down[SUBLANES:]], axis=0)
    x_next = jnp.concatenate([up[:n - SUBLANES], jnp.where(r == SUBLANES - 1, next_row, up[n - SUBLANES:])], axis=0)
    return x_prev, x_next


def _conv3(x_ref, xp_ref, xn_ref, w_ref):
    i = pl.program_id(1)
    prev_row = jnp.where(i > 0, xp_ref[SUBLANES - 1:SUBLANES, :], 0.0)
    next_row = jnp.where(i < pl.num_programs(1) - 1, xn_ref[0:1, :], 0.0)
    x = x_ref[...]
    x_prev, x_next = _shift_rows(x, prev_row, next_row)
    return x_prev * w_ref[0:1, :] + x * w_ref[1:2, :] + x_next * w_ref[2:3, :]


def _dn_prep_body(x_ref, xp_ref, xn_ref, ab_ref, cw_ref, alog_ref, dtb_ref,
                  q_ref, k_ref, v_ref, gb_ref, gam_ref, gamt_ref):
    y = _conv3(x_ref, xp_ref, xn_ref, cw_ref)
    y = y * jax.nn.sigmoid(y)
    for h in range(DN_HEADS):
        for part, (ref, mul) in enumerate(((q_ref, DN_DIM ** -0.5), (k_ref, 1.0))):
            sl = slice(part * DN_WIDTH + h * DN_DIM, part * DN_WIDTH + (h + 1) * DN_DIM)
            t = y[:, sl]
            t = t * lax.rsqrt(jnp.sum(t * t, axis=-1, keepdims=True) + EPS)
            ref[:, h * DN_DIM:(h + 1) * DN_DIM] = t * mul if mul != 1.0 else t
    v_ref[...] = y[:, 2 * DN_WIDTH:]

    ab = ab_ref[...]
    z = ab + dtb_ref[...]
    softplus = jnp.maximum(z, 0.0) + jnp.log(1.0 + jnp.exp(-jnp.abs(z)))
    g = -jnp.exp(alog_ref[...]) * softplus
    lane = lax.broadcasted_iota(jnp.int32, ab.shape, 1)
    gb_ref[...] = jnp.where(lane < DN_STREAMS, g, jax.nn.sigmoid(ab))

    c = DN_BLOCK
    ii = lax.broadcasted_iota(jnp.int32, (c, c), 0)
    jj = lax.broadcasted_iota(jnp.int32, (c, c), 1)
    tri_f = (ii >= jj).astype(F32)
    lane_c = lax.broadcasted_iota(jnp.int32, (c, ab.shape[1]), 1)
    for n in range(x_ref.shape[0] // c):
        gc = g[n * c:(n + 1) * c, :]
        cf = jnp.dot(tri_f, gc, preferred_element_type=F32, precision=lax.Precision.HIGHEST)
        cb = cf[c - 1:c, :] - cf + gc
        gam = jnp.where(lane_c < DN_HEADS, cf, cb)
        gam_ref[n * c:(n + 1) * c, :] = gam
        gamt_ref[n] = gam.T[:DN_STREAMS, :]


def dn_prep(p, ab_col, conv_w, a_log, dt_bias, tl):
    b, l, _ = p.shape
    c3 = 3 * DN_WIDTH
    nblk8 = l // 8
    lanes = 128
    pad8 = lambda a: jnp.pad(a.reshape(1, DN_STREAMS).astype(F32), ((0, 0), (0, lanes - DN_STREAMS)))
    tok = lambda bi, i: (bi, i, 0)
    return pl.pallas_call(
        _dn_prep_body,
        grid=(b, l // tl),
        in_specs=[pl.BlockSpec((None, tl, c3), tok),
                  pl.BlockSpec((None, 8, c3), lambda bi, i: (bi, jnp.maximum(i * (tl // 8) - 1, 0), 0)),
                  pl.BlockSpec((None, 8, c3), lambda bi, i: (bi, jnp.minimum((i + 1) * (tl // 8), nblk8 - 1), 0)),
                  pl.BlockSpec((None, tl, lanes), lambda bi, i: (bi, i, ab_col // lanes)),
                  pl.BlockSpec((3, c3), lambda bi, i: (0, 0)),
                  pl.BlockSpec((1, lanes), lambda bi, i: (0, 0)),
                  pl.BlockSpec((1, lanes), lambda bi, i: (0, 0))],
        out_specs=[pl.BlockSpec((None, tl, DN_WIDTH), tok)] * 3
        + [pl.BlockSpec((None, tl, lanes), tok)] * 2
        + [pl.BlockSpec((None, tl // DN_BLOCK, DN_STREAMS, DN_BLOCK), lambda bi, i: (bi, i, 0, 0))],
        out_shape=[jax.ShapeDtypeStruct((b, l, DN_WIDTH), F32)] * 3
        + [jax.ShapeDtypeStruct((b, l, lanes), F32)] * 2
        + [jax.ShapeDtypeStruct((b, l // DN_BLOCK, DN_STREAMS, DN_BLOCK), F32)],
        compiler_params=_cparams(("parallel", "parallel")),
        name="dn_prep",
    )(p, p, p, p, conv_w, pad8(a_log), pad8(dt_bias))


def _dot16(a, b):
    return jnp.dot(a.astype(BF16), b.astype(BF16), preferred_element_type=F32)


def _unit_triangular_inverses(ms, ii, jj):
    c = ms[0].shape[0]
    eye = (ii == jj).astype(F32)
    pair = (ii // 2) == (jj // 2)
    ts = [eye - jnp.where(pair, m, 0.0) for m in ms]
    s = 2
    while s < c:
        sub = ((ii // (2 * s)) == (jj // (2 * s))) & ((ii // s) != (jj // s))
        t16 = [t.astype(BF16) for t in ts]
        xs = [_dot16(jnp.where(sub, m, 0.0), t) for m, t in zip(ms, t16)]
        ts = [t - _dot16(tb, x) for t, tb, x in zip(ts, t16, xs)]
        s *= 2
    return ts


def _dn_scan_body(qf_ref, kf_ref, vf_ref, gbf_ref, gamf_ref, gamtf_ref,
                  qb_ref, kb_ref, vb_ref, gbb_ref, gamb_ref, gamtb_ref, s0_ref,
                  of_ref, ob_ref, sf_ref, s_scr):
    n = pl.program_id(1)

    @pl.when(n == 0)
    def _():
        s_scr[...] = s0_ref[...]

    c, dh = DN_BLOCK, DN_DIM
    ii = lax.broadcasted_iota(jnp.int32, (c, c), 0)
    jj = lax.broadcasted_iota(jnp.int32, (c, c), 1)
    incl = {True: ii >= jj, False: ii <= jj}
    strict = {True: ii > jj, False: ii < jj}
    srcs = {True: (qf_ref, kf_ref, vf_ref, gbf_ref, gamf_ref, gamtf_ref),
            False: (qb_ref, kb_ref, vb_ref, gbb_ref, gamb_ref, gamtb_ref)}
    streams = [(d == 0, h) for d in range(2) for h in range(DN_HEADS)]
    hs = lambda h: slice(h * dh, (h + 1) * dh)
    subs = gamtf_ref.shape[0]
    items = [(r, f, h, j, (j if f else subs - 1 - j)) for j in range(subs) for r, (f, h) in enumerate(streams)]
    rows = lambda blk: slice(blk * c, (blk + 1) * c)
    rng = range(len(items))

    q = [srcs[f][0][rows(bk), hs(h)] for _, f, h, _, bk in items]
    k = [srcs[f][1][rows(bk), hs(h)] for _, f, h, _, bk in items]
    v = [srcs[f][2][rows(bk), hs(h)] for _, f, h, _, bk in items]
    beta = [srcs[f][3][rows(bk), DN_STREAMS + r:DN_STREAMS + r + 1] for r, f, _, _, bk in items]
    gcol = [srcs[f][4][rows(bk), r:r + 1] for r, f, _, _, bk in items]
    grow = [srcs[f][5][bk, r:r + 1, :] for r, f, _, _, bk in items]
    g_last = [gcol[i][c - 1:c, :] if items[i][1] else gcol[i][0:1, :] for i in rng]

    e = [jnp.exp(jnp.where(incl[items[i][1]], gcol[i] - grow[i], 0.0)) for i in rng]
    kb = [k[i] * beta[i] for i in rng]
    k16 = [k[i].astype(BF16) for i in rng]
    m = [_nt_dot(kb[i].astype(BF16), k16[i]) * jnp.where(strict[items[i][1]], e[i], 0.0) for i in rng]
    attn = [(_nt_dot(q[i].astype(BF16), k16[i]) * jnp.where(incl[items[i][1]], e[i], 0.0)).astype(BF16) for i in rng]
    t = _unit_triangular_inverses(m, ii, jj)
    eg = [jnp.exp(gcol[i]) for i in rng]
    sol = [_dot16(t[i], jnp.concatenate([v[i] * beta[i], kb[i] * eg[i]], axis=-1)) for i in rng]
    u = [sol[i][:, :dh] for i in rng]
    w16 = [sol[i][:, dh:].astype(BF16) for i in rng]
    q_dec = [(q[i] * eg[i]).astype(BF16) for i in rng]
    k_dec = [(k[i] * jnp.exp(g_last[i] - gcol[i])).astype(BF16) for i in rng]
    blk_dec = [jnp.exp(g_last[i]) for i in rng]

    s = [s_scr[r] for r in range(DN_STREAMS)]
    for j in range(subs):
        idx = [i for i in rng if items[i][3] == j]
        s16 =[s[r].astype(BF16) for r in range(DN_STREAMS)]
        v_new = [(u[i] - jnp.dot(w16[i], s16[items[i][0]], preferred_element_type=F32)).astype(BF16) for i in idx]
        o = [jnp.dot(q_dec[i], s16[items[i][0]], preferred_element_type=F32)
             + jnp.dot(attn[i], vn, preferred_element_type=F32) for i, vn in zip(idx, v_new)]
        s = [s[items[i][0]] * blk_dec[i] + _tn_dot(k_dec[i], vn) for i, vn in zip(idx, v_new)]
        for i, oi in zip(idx, o):
            _, f, h, _, bk = items[i]
            (of_ref if f else ob_ref)[rows(bk), hs(h)] = oi.astype(of_ref.dtype)
    for r in range(DN_STREAMS):
        s_scr[r] = s[r]

    @pl.when(n == pl.num_programs(1) - 1)
    def _():
        sf_ref[...] = s_scr[...]


def dn_scan(q, k, v, gb, gam, gamt, s0):
    b, l, w = q.shape
    subs = min(DN_SUBS, l // DN_BLOCK)
    c = DN_BLOCK * subs
    n = l // c
    lanes = gb.shape[-1]
    f3 = lambda bi, ni: (bi, ni, 0)
    b3 = lambda bi, ni: (bi, n - 1 - ni, 0)
    f4 = lambda bi, ni: (bi, ni, 0, 0)
    b4 = lambda bi, ni: (bi, n - 1 - ni, 0, 0)
    st = lambda bi, ni: (bi, 0, 0, 0)

    def specs(m3, m4):
        return [pl.BlockSpec((None, c, w), m3)] * 3 + [pl.BlockSpec((None, c, lanes), m3)] * 2 + [
            pl.BlockSpec((None, subs, DN_STREAMS, DN_BLOCK), m4)]

    return pl.pallas_call(
        _dn_scan_body,
        grid=(b, n),
        in_specs=specs(f3, f4) + specs(b3, b4) + [pl.BlockSpec((None, DN_STREAMS, DN_DIM, DN_DIM), st)],
        out_specs=[pl.BlockSpec((None, c, w), f3), pl.BlockSpec((None, c, w), b3),
                   pl.BlockSpec((None, DN_STREAMS, DN_DIM, DN_DIM), st)],
        out_shape=[jax.ShapeDtypeStruct((b, l, w), BF16), jax.ShapeDtypeStruct((b, l, w), BF16),
                   jax.ShapeDtypeStruct((b, DN_STREAMS, DN_DIM, DN_DIM), F32)],
        scratch_shapes=[pltpu.VMEM((DN_STREAMS, DN_DIM, DN_DIM), F32)],
        compiler_params=_cparams(("parallel", "arbitrary")),
        name="dn_scan",
    )(q, k, v, gb, gam, gamt, q, k, v, gb, gam, gamt, s0)


def _dn_final_body(of_ref, ob_ref, g_ref, w_ref, y_ref):
    for h in range(DN_HEADS):
        sl = slice(h * DN_DIM, (h + 1) * DN_DIM)
        o = of_ref[:, sl].astype(F32) + ob_ref[:, sl].astype(F32)
        y = _head_rms(o, w_ref[...])
        g = g_ref[:, sl]
        y_ref[:, sl] = (y * (g * jax.nn.sigmoid(g))).astype(y_ref.dtype)


def dn_finalize(o_f, o_b, p, gate_col, norm_w, tl):
    b, l, w = o_f.shape
    tok = lambda bi, i: (bi, i, 0)
    return pl.pallas_call(
        _dn_final_body,
        grid=(b, l // tl),
        in_specs=[pl.BlockSpec((None, tl, w), tok), pl.BlockSpec((None, tl, w), tok),
                  pl.BlockSpec((None, tl, w), lambda bi, i: (bi, i, gate_col // w)),
                  pl.BlockSpec((1, DN_DIM), lambda bi, i: (0, 0))],
        out_specs=pl.BlockSpec((None, tl, w), tok),
        out_shape=jax.ShapeDtypeStruct((b, l, w), BF16),
        compiler_params=_cparams(("parallel", "parallel")),
        name="dn_finalize",
    )(o_f, o_b, p, norm_w.reshape(1, DN_DIM))


def deltanet_pallas(p_lat, p_ctx, conv_w, a_log, dt_bias, norm_w, need_ctx, ab_col=3072, gate_col=1536):
    b = p_lat.shape[0]
    n_ctx = p_ctx.shape[1]
    s0 = jnp.zeros((b, DN_STREAMS, DN_DIM, DN_DIM), F32)
    fc = dn_prep(p_ctx, ab_col, conv_w, a_log, dt_bias, n_ctx)
    oc_f, oc_b, s_ctx = dn_scan(*fc, s0)
    fl = dn_prep(p_lat, ab_col, conv_w, a_log, dt_bias, PREP_TILE)
    ol_f, ol_b, _ = dn_scan(*fl, s_ctx)
    out_lat = dn_finalize(ol_f, ol_b, p_lat, gate_col, norm_w, PREP_TILE)
    out_ctx = dn_finalize(oc_f, oc_b, p_ctx, gate_col, norm_w, n_ctx) if need_ctx else None
    return out_lat, out_ctx


def _hy_factors(n):
    n2 = 256 if n >= 32768 else 128
    return n // n2, n2


def _bf16_const(m):
    return m.astype(ml_dtypes.bfloat16)


def _hy_constants(l):
    n = 2 * l
    n1, n2 = _hy_factors(n)
    a1 = 2.0 * np.pi * np.outer(np.arange(n1), np.arange(n1)) / n1
    c1, s1 = np.cos(a1), np.sin(a1)
    h = n1 // 2
    rows_fwd = np.block([[c1[:, :h], s1[:, :h]], [-s1[:, :h], c1[:, :h]]])
    rows_taps = np.concatenate([c1, -s1], axis=0)
    rows_inv = np.block([[c1[:h], -s1[:h]], [s1[:h], c1[:h]]]) / n
    a2 = 2.0 * np.pi * np.outer(np.arange(n2), np.arange(n2)) / n2
    c2, s2 = np.cos(a2), np.sin(a2)
    slab_fwd = np.block([[c2, s2], [-s2, c2]])
    slab_inv = np.block([[c2, -s2], [s2, c2]])
    th = 2.0 * np.pi * np.outer(np.arange(n1), np.arange(n2)) / n
    lanes = lambda t: jnp.broadcast_to(jnp.asarray(t, F32)[:, :, None], (n1, n2, 128))
    return dict(n1=n1, n2=n2, rows_fwd=_bf16_const(rows_fwd), rows_taps=_bf16_const(rows_taps),
                rows_inv=_bf16_const(rows_inv), slab_fwd=_bf16_const(slab_fwd), slab_inv=_bf16_const(slab_inv),
                tw_cos=lanes(np.cos(th)), tw_sin=lanes(np.sin(th)))


def _hy_prep_body(x_ref, xp_ref, xn_ref, w_ref, v_ref, x1_ref, x2_ref):
    y = _conv3(x_ref, xp_ref, xn_ref, w_ref)
    v_ref[...] = y[:, :HY_CH]
    x1_ref[...] = y[:, HY_CH:2 * HY_CH]
    x2_ref[...] = y[:, 2 * HY_CH:]


def hy_prep(p, col0, conv_w, tl):
    b, l, _ = p.shape
    c3 = 3 * HY_CH
    nblk8 = l // 8
    cb = col0 // c3
    tok = lambda bi, i: (bi, i, 0)
    return pl.pallas_call(
        _hy_prep_body,
        grid=(b, l // tl),
        in_specs=[pl.BlockSpec((None, tl, c3), lambda bi, i: (bi, i, cb)),
                  pl.BlockSpec((None, 8, c3), lambda bi, i: (bi, jnp.maximum(i * (tl // 8) - 1, 0), cb)),
                  pl.BlockSpec((None, 8, c3), lambda bi, i: (bi, jnp.minimum((i + 1) * (tl // 8), nblk8 - 1), cb)),
                  pl.BlockSpec((3, c3), lambda bi, i: (0, 0))],
        out_specs=[pl.BlockSpec((None, tl, HY_CH), tok)] * 3,
        out_shape=[jax.ShapeDtypeStruct((b, l, HY_CH), F32)] * 3,
        compiler_params=_cparams(("parallel", "parallel")),
        name="hy_prep",
    )(p, p, p, conv_w)


def _split_bf16(a):
    hi = a.astype(BF16)
    return hi, (a - hi.astype(F32)).astype(BF16)


HY_FILTER_GROUP = 8


def _hy_filter_body(t_ref, zt_ref, w1t_ref, b1_ref, w2t_ref, b2_ref, w3_ref, rates_ref, m_ref, *out_refs):
    a_refs, sum_ref = out_refs[:-1], out_refs[-1]
    i = pl.program_id(0)
    hp = lax.Precision.HIGHEST
    h = jnp.sin(HY_SIN_FREQ * (jnp.dot(w1t_ref[...], zt_ref[...], preferred_element_type=F32, precision=hp)
                               + b1_ref[...]))
    h = jnp.sin(HY_SIN_FREQ * (jnp.dot(w2t_ref[...], h, preferred_element_type=F32, precision=hp) + b2_ref[...]))
    half = h.shape[1] // 2
    taps = []
    for d in range(2):
        h_hi, h_lo = _split_bf16(h[:, d * half:(d + 1) * half])
        w_hi, w_lo = _split_bf16(w3_ref[d])
        td = _tn_dot(h_hi, w_hi) + (_tn_dot(h_hi, w_lo) + _tn_dot(h_lo, w_hi))
        taps.append(td * jnp.exp(-t_ref[d * half:(d + 1) * half, :] * rates_ref[...]))

    @pl.when(i == 0)
    def _():
        sum_ref[...] = jnp.zeros_like(sum_ref)

    sum_ref[...] += (jnp.sum(jnp.abs(taps[0]), axis=0, keepdims=True)
                     + jnp.sum(jnp.abs(taps[1]), axis=0, keepdims=True))
    row = lax.broadcasted_iota(jnp.int32, (half, 1), 0)
    taps[1] = jnp.where((row == 0) & (i == 0), 0.0, taps[1])
    nh = half // HY_FILTER_GROUP
    for j in range(HY_FILTER_GROUP):
        for o in range(HY_ORDER):
            cs = slice(o * HY_CH, (o + 1) * HY_CH)
            x = jnp.concatenate([taps[0][j * nh:(j + 1) * nh, cs], taps[1][j * nh:(j + 1) * nh, cs]], axis=0)
            y = _dot16(m_ref[...], x)
            a_refs[o][0, :, j, :] = y[:2 * nh]
            a_refs[o][1, :, j, :] = y[2 * nh:]


def hy_filter_rows(l, consts, w1, b1, w2, b2, w3):
    n = 2 * l
    n1, n2, g = consts["n1"], consts["n2"], HY_FILTER_GROUP
    tr = g * n1
    grp, d, j, m = np.meshgrid(np.arange(n2 // g), np.arange(2), np.arange(g), np.arange(n1 // 2), indexing="ij")
    r = (n2 * (m + (n1 // 2) * d) + g * grp + j).reshape(-1)
    pos = np.where(r < l, r, n - r).astype(np.float64)
    pos[r == l] = 0.0
    t = pos / max(l - 1, 1)
    bands = np.linspace(1e-4, HY_BANDS - 1, HY_BANDS)
    phase = (2.0 * math.pi / l) * pos[:, None] * bands[None, :]
    kf = 32
    z = np.concatenate([t[:, None], np.cos(phase), -np.sin(phase)], axis=-1)
    zt = np.pad(z, ((0, 0), (0, kf - z.shape[1]))).T.astype(np.float32)
    t_col = t[:, None].astype(np.float32)
    w1t = jnp.pad(w1.astype(F32), ((0, kf - w1.shape[0]), (0, 0))).T
    hid = w1.shape[1]
    w3d = w3.astype(F32).reshape(hid, HY_ORDER, 2, HY_CH).transpose(2, 0, 1, 3).reshape(2, hid, HY_ORDER * HY_CH)
    rates = jnp.abs(jnp.linspace(math.log(HY_TARGET) / HY_DECAY_LONG, math.log(HY_TARGET) / HY_DECAY_SHORT,
                                 HY_CH, dtype=F32))
    rates = jnp.tile(rates, HY_ORDER).reshape(1, HY_ORDER * HY_CH)
    fixed = lambda i: (0, 0)
    mat = consts["rows_taps"]
    return pl.pallas_call(
        _hy_filter_body,
        grid=(n // tr,),
        in_specs=[pl.BlockSpec((tr, 1), lambda i: (i, 0)), pl.BlockSpec((kf, tr), lambda i: (0, i)),
                  pl.BlockSpec((hid, kf), fixed), pl.BlockSpec((hid, 1), fixed),
                  pl.BlockSpec((hid, hid), fixed), pl.BlockSpec((hid, 1), fixed),
                  pl.BlockSpec((2, hid, HY_ORDER * HY_CH), lambda i: (0, 0, 0)),
                  pl.BlockSpec((1, HY_ORDER * HY_CH), fixed), pl.BlockSpec(mat.shape, fixed)],
        out_specs=[pl.BlockSpec((2, n1, g, HY_CH), lambda i: (0, 0, i, 0))] * HY_ORDER
        + [pl.BlockSpec((1, HY_ORDER * HY_CH), fixed)],
        out_shape=[jax.ShapeDtypeStruct((2, n1, n2, HY_CH), F32)] * HY_ORDER
        + [jax.ShapeDtypeStruct((1, HY_ORDER * HY_CH), F32)],
        compiler_params=_cparams(("arbitrary",)),
        name="hy_filter_rows",
    )(t_col, zt, w1t, b1.astype(F32).reshape(hid, 1), w2.astype(F32).T, b2.astype(F32).reshape(hid, 1), w3d, rates,
      mat)


HY_ROWS_GROUP = {"fwd": 16, "inv": 8}


def _hy_rows_body(*refs, mode):
    if mode == "fwd":
        m_ref, x_ref, o_ref = refs
    else:
        m_ref, x_ref, xn_ref, z_ref, skip_ref, o_ref = refs
    for j in range(HY_ROWS_GROUP[mode]):
        x = jnp.concatenate([x_ref[0, :, j, :], x_ref[1, :, j, :]], axis=0)
        y = _dot16(m_ref[...], x)
        half = y.shape[0] // 2
        for bi in range(2):
            yb = y[bi * half:(bi + 1) * half]
            if mode == "inv":
                yb = xn_ref[bi, :, j, :] * (yb + z_ref[bi, :, j, :] * skip_ref[...])
            o_ref[bi, :, j, :] = yb


def hy_rows_pass(mode, mat, x, extra=()):
    n2, c = x.shape[-2:]
    g = HY_ROWS_GROUP[mode]
    blk = lambda rows: pl.BlockSpec((2, rows, g, c), lambda j: (0, 0, j, 0))
    mspec = _resident(mat.shape)
    vec = pl.BlockSpec((1, c), lambda j: (0, 0))
    if mode == "fwd":
        n1 = 2 * x.shape[1]
        in_specs = [mspec, blk(n1 // 2)]
        out_rows = n1
    else:
        n1 = x.shape[1]
        in_specs = [mspec, blk(n1), blk(n1 // 2), blk(n1 // 2), vec]
        out_rows = n1 // 2
    return pl.pallas_call(
        functools.partial(_hy_rows_body, mode=mode),
        grid=(n2 // g,),
        in_specs=in_specs,
        out_specs=blk(out_rows),
        out_shape=jax.ShapeDtypeStruct((2, out_rows, n2, c), F32),
        compiler_params=_cparams(("parallel",)),
        name="hy_rows_" + mode,
    )(mat, x, *extra)


def _hy_slab_body(*refs, with_filter):
    if with_filter:
        f_ref, i_ref, tc_ref, ts_ref, a_ref, h_ref, o_ref = refs
    else:
        f_ref, tc_ref, ts_ref, a_ref, sum_ref, o_ref = refs
        scale = 1.0 / (sum_ref[...] + EPS)
    reps = a_ref.shape[-1] // tc_ref.shape[-1]
    n2 = a_ref.shape[2]
    for s in range(a_ref.shape[1]):
        tc = jnp.concatenate([tc_ref[s]] * reps, axis=1)
        ts = jnp.concatenate([ts_ref[s]] * reps, axis=1)
        ar, ai = a_ref[0, s], a_ref[1, s]
        x = _dot16(f_ref[...], jnp.concatenate([ar * tc + ai * ts, ai * tc - ar * ts], axis=0))
        xr, xi = x[:n2], x[n2:]
        if not with_filter:
            o_ref[0, s] = (xr * scale).astype(o_ref.dtype)
            o_ref[1, s] = (xi * scale).astype(o_ref.dtype)
            continue
        hr, hi = h_ref[0, s].astype(F32), h_ref[1, s].astype(F32)
        y = _dot16(i_ref[...], jnp.concatenate([xr * hr - xi * hi, xr * hi + xi * hr], axis=0))
        yr, yi = y[:n2], y[n2:]
        o_ref[0, s] = yr * tc - yi * ts
        o_ref[1, s] = yi * tc + yr * ts


HY_SLABS_PER_STEP = 4


def hy_slab_pass(consts, a, h=None, l1_sum=None):
    _, n1, n2, c = a.shape
    kb = HY_SLABS_PER_STEP
    fwd = consts["slab_fwd"]
    mspec = _resident(fwd.shape)
    tw = pl.BlockSpec((kb, n2, 128), lambda k: (k, 0, 0))
    slab = pl.BlockSpec((2, kb, n2, c), lambda k: (0, k, 0, 0))
    if h is None:
        args = [fwd, consts["tw_cos"], consts["tw_sin"], a, l1_sum]
        in_specs = [mspec, tw, tw, slab, pl.BlockSpec((1, c), lambda k: (0, 0))]
    else:
        args = [fwd, consts["slab_inv"], consts["tw_cos"], consts["tw_sin"], a, h]
        in_specs = [mspec, mspec, tw, tw, slab, slab]
    return pl.pallas_call(
        functools.partial(_hy_slab_body, with_filter=h is not None),
        grid=(n1 // kb,),
        in_specs=in_specs,
        out_specs=slab,
        out_shape=jax.ShapeDtypeStruct(a.shape, F32 if h is not None else BF16),
        compiler_params=_cparams(("parallel",)),
        name="hy_slab_conv" if h is not None else "hy_slab_fwd",
    )(*args)


def hyena_pallas(p, conv_w, w1, b1, w2, b2, w3, skip, col0=RET_COLS):
    b, l, _ = p.shape
    assert b == 2, "the two batch rows are packed as one complex signal"
    c = HY_CH
    consts = _hy_constants(l)
    n1, n2 = consts["n1"], consts["n2"]
    *a_h, sums = hy_filter_rows(l, consts, w1, b1, w2, b2, w3)
    spectra = [hy_slab_pass(consts, a_h[o], l1_sum=sums[:, o * c:(o + 1) * c]) for o in range(HY_ORDER)]
    split = lambda t: t.reshape(2, n1 // 2, n2, c)
    v, x1, x2 = hy_prep(p, col0, conv_w, PREP_TILE)
    z = split(v)
    for o, xn in enumerate((x1, x2)):
        a = hy_rows_pass("fwd", consts["rows_fwd"], z)
        bm = hy_slab_pass(consts, a, spectra[o])
        z = hy_rows_pass("inv", consts["rows_inv"], bm,
                         extra=(split(xn), z, skip[o].astype(F32).reshape(1, c)))
    return z.reshape(b, l, c)


def _pad_cols(w, mult):
    n = w.shape[-1]
    pad = (-n) % mult
    return jnp.pad(w, ((0, 0), (0, pad))) if pad else w


def kernel(x, c, ctx, c_ctx, mod_w, mod_b, norm_mix_w, norm_ffn_w, ffn_w_in, ffn_w_out,
           ab_w_in, ab_w_out, dn_conv_w, dn_a_log, dn_dt_bias, dn_norm_w, swa_q_norm_w, swa_k_norm_w,
           swa_sink, cd_w_in, cd_w_out, ret_decay_logit, ret_gn_w, hy_conv_w, hy_f_w1, hy_f_b1,
           hy_f_w2, hy_f_b2, hy_f_w3, hy_bias):
    B, L, D = x.shape
    n_ctx = ctx.shape[1]
    rows = L // GRID_W
    c_rows = jnp.zeros((SUBLANES, D), F32).at[:B].set(c).at[B].set(c_ctx)
    mod_all = modulation(c_rows, mod_w, mod_b)
    hid = ffn_w_out.shape[1]
    h_ctx = ctx
    for layer in range(DEPTH):
        need_ctx = layer != DEPTH - 1
        i = layer // 2
        m = mod_all[layer].reshape(SUBLANES, 6, D)
        mod = [m[:B, j][:, None, :] for j in range(6)]
        mod_c = [jnp.broadcast_to(m[B, j][None, None, :], (B, 1, D)) for j in range(6)]
        if layer % 2 == 0:
            w = ab_w_in[i]
            w_in = jnp.concatenate([w[:, :4 * DN_WIDTH], w[:, DN_COLS:], w[:, 4 * DN_WIDTH:DN_COLS]], axis=1)
            w_out = ab_w_out[i]
        else:
            w = cd_w_in[i]
            w_in = jnp.concatenate([w[:, RET_COLS:], w[:, :RET_COLS]], axis=1)
            w_out = cd_w_out[i]
        w_in_p = _pad_cols(w_in, LANES).astype(BF16)
        p_lat = in_projection(x, norm_mix_w[layer], mod[0], mod[1], w_in_p, DENSE_TILE)
        p_ctx = in_projection(h_ctx, norm_mix_w[layer], mod_c[0], mod_c[1], w_in_p, n_ctx)
        if layer % 2 == 0:
            swa0 = 4 * DN_WIDTH
            ab0 = swa0 + (SWA_HEADS + 2 * SWA_KV_HEADS) * SWA_DIM
            a_lat, a_ctx = deltanet_pallas(p_lat, p_ctx, dn_conv_w[i], dn_a_log[i], dn_dt_bias[i],
                                           dn_norm_w[i], need_ctx, ab_col=ab0, gate_col=3 * DN_WIDTH)
            b_lat, b_ctx = window_gqa_pallas(p_lat, p_ctx, swa0, axial_rope_tables(rows), swa_q_norm_w[i],
                                             swa_k_norm_w[i], swa_sink[i], need_ctx)
        else:
            a_lat, a_ctx = retention_pallas(p_lat, p_ctx, ret_decay_logit[i], ret_gn_w[i], need_ctx,
                                            col0=(HY_ORDER + 1) * HY_CH)
            assert not need_ctx, "the last layer's context outputs reach no latent token"
            b_lat = hyena_pallas(p_lat, hy_conv_w[i], hy_f_w1[i], hy_f_b1[i], hy_f_w2[i], hy_f_b2[i],
                                 hy_f_w3[i], hy_bias[i], col0=0)
            b_ctx = None
        wo = w_out.astype(BF16)
        wg = ffn_w_in[layer][:, :hid].astype(BF16)
        wu = ffn_w_in[layer][:, hid:].astype(BF16)
        wd = ffn_w_out[layer].astype(BF16)
        x = out_projection_ffn(x, a_lat, b_lat, wo, mod[2], norm_ffn_w[layer], mod[3], mod[4], mod[5],
                               wg, wu, wd, DENSE_TILE, FFN_HIDDEN_CHUNK)
        if need_ctx:
            h_ctx = out_projection_ffn(h_ctx, a_ctx, b_ctx, wo, mod_c[2], norm_ffn_w[layer], mod_c[3], mod_c[4],
                                       mod_c[5], wg, wu, wd, n_ctx, FFN_HIDDEN_CHUNK)
    return x
```

```python
import functools
import math

import jax
import jax.numpy as jnp
import ml_dtypes
import numpy as np
from jax import lax
from jax.experimental import pallas as pl
from jax.experimental.pallas import tpu as pltpu

F32 = jnp.float32
BF16 = jnp.bfloat16
EPS = 1e-6

DEPTH = 2
GRID_W = 64

DN_HEADS = 4
DN_DIM = 128
DN_WIDTH = DN_HEADS * DN_DIM
DN_COLS = 4 * DN_WIDTH + 4 * DN_HEADS
SWA_HEADS = 4
SWA_KV_HEADS = 2
SWA_DIM = 128
SWA_BLOCK = 128
ROPE_THETA = 10000.0
RET_HEADS = 4
RET_DIM = 128
RET_WIDTH = RET_HEADS * RET_DIM
RET_CHUNK = 128
RET_COLS = 4 * RET_WIDTH
HY_CH = 512
HY_ORDER = 2
HY_BANDS = 8
HY_SIN_FREQ = 1.0
HY_TARGET = 1e-2
HY_DECAY_SHORT = 0.3
HY_DECAY_LONG = 1.5

VMEM_LIMIT_BYTES = 56 * 1024 * 1024
SUBLANES, LANES = 8, 128

DENSE_TILE = 512
FFN_HIDDEN_CHUNK = 256
PREP_TILE = 1024
SWA_BLOCKS_PER_STEP = 4
MOD_COL_TILE = 1536


def _cparams(sem):
    return pltpu.CompilerParams(dimension_semantics=sem, vmem_limit_bytes=VMEM_LIMIT_BYTES)


def _mod_body(c_ref, w_ref, b_ref, o_ref):
    a = c_ref[...]
    a = a * jax.nn.sigmoid(a)
    o_ref[...] = jnp.dot(a, w_ref[...], preferred_element_type=F32,
                         precision=lax.Precision.HIGHEST) + b_ref[...]


def modulation(c_rows, mod_w, mod_b):
    depth, d, n = mod_w.shape
    tn = MOD_COL_TILE
    return pl.pallas_call(
        _mod_body,
        grid=(depth, n // tn),
        in_specs=[pl.BlockSpec((8, d), lambda l, j: (0, 0)),
                  pl.BlockSpec((None, d, tn), lambda l, j: (l, 0, j)),
                  pl.BlockSpec((None, 1, tn), lambda l, j: (l, 0, j))],
        out_specs=pl.BlockSpec((None, 8, tn), lambda l, j: (l, 0, j)),
        out_shape=jax.ShapeDtypeStruct((depth, 8, n), F32),
        compiler_params=_cparams(("parallel", "parallel")),
        name="modulation",
    )(c_rows, mod_w, mod_b.reshape(depth, 1, n))


def _norm_mod(x, nw, shift, scale):
    y = x * lax.rsqrt(jnp.mean(x * x, axis=-1, keepdims=True) + EPS)
    return (y * nw) * (1.0 + scale) + shift


def _resident(shape):
    return pl.BlockSpec(shape, lambda *_: (0,) * len(shape), pipeline_mode=pl.Buffered(1))


def _inproj_body(x_ref, nw_ref, shift_ref, scale_ref, w_ref, o_ref):
    h = _norm_mod(x_ref[...], nw_ref[...], shift_ref[...], scale_ref[...]).astype(BF16)
    o_ref[...] = jnp.dot(h, w_ref[...], preferred_element_type=F32)


def in_projection(x, nw, shift, scale, w, tm):
    b, l, d = x.shape
    n = w.shape[1]
    return pl.pallas_call(
        _inproj_body,
        grid=(b, l // tm),
        in_specs=[pl.BlockSpec((None, tm, d), lambda bi, i: (bi, i, 0)),
                  _resident((1, d)),
                  pl.BlockSpec((None, 1, d), lambda bi, i: (bi, 0, 0)),
                  pl.BlockSpec((None, 1, d), lambda bi, i: (bi, 0, 0)),
                  _resident((d, n))],
        out_specs=pl.BlockSpec((None, tm, n), lambda bi, i: (bi, i, 0)),
        out_shape=jax.ShapeDtypeStruct((b, l, n), F32),
        compiler_params=_cparams(("parallel", "parallel")),
        name="in_projection",
    )(x, nw.reshape(1, d), shift, scale, w)


def _outffn_body(x_ref, oa_ref, ob_ref, wo_ref, g2_ref, nw_ref, sh_ref, sc_ref, g5_ref,
                 wg_ref, wu_ref, wd_ref, y_ref, x1_scr, h_scr, acc_scr, *, th):
    wa = oa_ref.shape[1]
    mix = (jnp.dot(oa_ref[...].astype(BF16), wo_ref[:wa, :], preferred_element_type=F32)
           + jnp.dot(ob_ref[...].astype(BF16), wo_ref[wa:, :], preferred_element_type=F32))
    x1 = x_ref[...] + g2_ref[...] * mix
    x1_scr[...] = x1
    h_scr[...] = _norm_mod(x1, nw_ref[...], sh_ref[...], sc_ref[...]).astype(BF16)
    for k in range(wg_ref.shape[1] // th):
        ks = slice(k * th, (k + 1) * th)
        g = jnp.dot(h_scr[...], wg_ref[:, ks], preferred_element_type=F32)
        u = jnp.dot(h_scr[...], wu_ref[:, ks], preferred_element_type=F32)
        a = (g * jax.nn.sigmoid(g) * u).astype(BF16)
        part = jnp.dot(a, wd_ref[ks, :], preferred_element_type=F32)
        if k == 0:
            acc_scr[...] = part
        else:
            acc_scr[...] += part
    y_ref[...] = x1_scr[...] + g5_ref[...] * acc_scr[...]


def out_projection_ffn(x, oa, ob, wo, g2, nw, shift, scale, g5, wg, wu, wd, tm, th):
    b, l, d = x.shape
    wa, wb = oa.shape[-1], ob.shape[-1]
    hid = wg.shape[1]
    tok = lambda bi, i: (bi, i, 0)
    vec = pl.BlockSpec((None, 1, d), lambda bi, i: (bi, 0, 0))
    return pl.pallas_call(
        functools.partial(_outffn_body, th=th),
        grid=(b, l // tm),
        in_specs=[pl.BlockSpec((None, tm, d), tok),
                  pl.BlockSpec((None, tm, wa), tok),
                  pl.BlockSpec((None, tm, wb), tok),
                  _resident((d, d)), vec, _resident((1, d)), vec, vec, vec,
                  _resident((d, hid)), _resident((d, hid)), _resident((hid, d))],
        out_specs=pl.BlockSpec((None, tm, d), tok),
        out_shape=jax.ShapeDtypeStruct((b, l, d), F32),
        scratch_shapes=[pltpu.VMEM((tm, d), F32), pltpu.VMEM((tm, d), BF16), pltpu.VMEM((tm, d), F32)],
        compiler_params=_cparams(("parallel", "parallel")),
        name="out_projection_ffn",
    )(x, oa, ob, wo, g2, nw.reshape(1, d), shift, scale, g5, wg, wu, wd)


NEG_BIG = -1e30


def _head_rms(x, w):
    return x * lax.rsqrt(jnp.mean(x * x, axis=-1, keepdims=True) + EPS) * w


def _swa_prep_body(*refs, use_rope):
    if use_rope:
        q_ref, k_ref, v_ref, qw_ref, kw_ref, rt_ref, ct_ref, qo_ref, ko_ref, vo_ref = refs
        tl = q_ref.shape[0]
        nr = tl // GRID_W

        def table(kind):
            rowp = jnp.broadcast_to(rt_ref[kind][:, None, :], (nr, GRID_W, SWA_DIM))
            colp = jnp.broadcast_to(ct_ref[kind][None, :, :], (nr, GRID_W, SWA_DIM))
            return (rowp + colp).reshape(tl, SWA_DIM)

        cos, sa, sb = table(0), table(1), table(2)
    else:
        q_ref, k_ref, v_ref, qw_ref, kw_ref, qo_ref, ko_ref, vo_ref = refs

    def prep(x, w):
        y = _head_rms(x, w)
        if use_rope:
            y = y * cos + pltpu.roll(y, SWA_DIM - 32, 1) * sa + pltpu.roll(y, 32, 1) * sb
        return y.astype(BF16)

    for h in range(SWA_HEADS):
        sl = slice(h * SWA_DIM, (h + 1) * SWA_DIM)
        qo_ref[:, sl] = prep(q_ref[:, sl], qw_ref[...])
    for h in range(SWA_KV_HEADS):
        sl = slice(h * SWA_DIM, (h + 1) * SWA_DIM)
        ko_ref[:, sl] = prep(k_ref[:, sl], kw_ref[...])
    vo_ref[...] = v_ref[...].astype(BF16)


def swa_prep(p, col0, qw, kw, rope, tl):
    b, l, _ = p.shape
    nq, nk = SWA_HEADS * SWA_DIM, SWA_KV_HEADS * SWA_DIM
    tok = lambda bi, i: (bi, i, 0)
    in_specs = [pl.BlockSpec((None, tl, nq), lambda bi, i: (bi, i, col0 // nq)),
                pl.BlockSpec((None, tl, nk), lambda bi, i: (bi, i, (col0 + nq) // nk)),
                pl.BlockSpec((None, tl, nk), lambda bi, i: (bi, i, (col0 + nq + nk) // nk)),
                pl.BlockSpec((1, SWA_DIM), lambda bi, i: (0, 0)),
                pl.BlockSpec((1, SWA_DIM), lambda bi, i: (0, 0))]
    args = [p, p, p, qw.reshape(1, SWA_DIM), kw.reshape(1, SWA_DIM)]
    if rope is not None:
        in_specs += [pl.BlockSpec((3, tl // GRID_W, SWA_DIM), lambda bi, i: (0, i, 0)),
                     pl.BlockSpec((3, GRID_W, SWA_DIM), lambda bi, i: (0, 0, 0))]
        args += list(rope)
    return pl.pallas_call(
        functools.partial(_swa_prep_body, use_rope=rope is not None),
        grid=(b, l // tl),
        in_specs=in_specs,
        out_specs=[pl.BlockSpec((None, tl, nq), tok), pl.BlockSpec((None, tl, nk), tok),
                   pl.BlockSpec((None, tl, nk), tok)],
        out_shape=[jax.ShapeDtypeStruct((b, l, nq), BF16), jax.ShapeDtypeStruct((b, l, nk), BF16),
                   jax.ShapeDtypeStruct((b, l, nk), BF16)],
        compiler_params=_cparams(("parallel", "parallel")),
        name="swa_prep",
    )(*args)


def _nt_dot(a, b):
    return lax.dot_general(a, b, (((1,), (1,)), ((), ())), preferred_element_type=F32)


def _swa_attend(problems):
    def lane_tiles(x):
        return [x[:, c * LANES:(c + 1) * LANES] for c in range(x.shape[1] // LANES)]

    def masked_scores(q, keys, masks):
        out = []
        for kk, mask in zip(keys, masks):
            s = _nt_dot(q, kk)
            out.append(s if mask is None else jnp.where(mask, s, NEG_BIG))
        return out

    def row_max(scores, sink_col):
        m = sink_col + jnp.zeros((1, LANES), F32)
        for s in scores:
            for tile in lane_tiles(s):
                m = jnp.maximum(m, tile)
        return jnp.max(m, axis=-1, keepdims=True)

    def weighted(scores, vals, m, sink_col):
        part, acc = None, None
        for s, vv in zip(scores, vals):
            pr = jnp.exp(s - m)
            for tile in lane_tiles(pr):
                part = tile if part is None else part + tile
            o = jnp.dot(pr.astype(BF16), vv, preferred_element_type=F32)
            acc = o if acc is None else acc + o
        return acc, jnp.exp(sink_col - m) + jnp.sum(part, axis=-1, keepdims=True)

    scores = [masked_scores(q, keys, masks) for q, keys, _, masks, _ in problems]
    maxes = [row_max(s, p[4]) for s, p in zip(scores, problems)]
    outs = [weighted(s, p[2], m, p[4]) for s, p, m in zip(scores, problems, maxes)]
    return [acc / den for acc, den in outs]


def _swa_attn_body(sink_ref, q_ref, kp_ref, km_ref, kn_ref, vp_ref, vm_ref, vn_ref, kc_ref, vc_ref,
                   o_ref, *, nb):
    i = pl.program_id(1)
    n_blocks = pl.num_programs(1) * nb
    w, d = SWA_BLOCK, SWA_DIM
    grp = SWA_HEADS // SWA_KV_HEADS
    rows = lax.broadcasted_iota(jnp.int32, (grp * w, w), 0)
    qi = rows % w
    kj = lax.broadcasted_iota(jnp.int32, (grp * w, w), 1)
    sink_cols = []
    for h in range(SWA_KV_HEADS):
        sink_col = jnp.zeros((grp * w, 1), F32)
        for g in range(grp):
            sink_col = jnp.where((rows[:, :1] // w) == g, sink_ref[h * grp + g], sink_col)
        sink_cols.append(sink_col)
    problems, places = [], []
    for r in range(nb):
        blk = i * nb + r
        mask_prev = (kj >= qi) & (blk > 0)
        mask_next = (kj <= qi) & (blk < n_blocks - 1)
        rs = slice(r * w, (r + 1) * w)
        for h in range(SWA_KV_HEADS):
            hs = slice(h * d, (h + 1) * d)
            q = jnp.concatenate([q_ref[rs, (h * grp + g) * d:(h * grp + g + 1) * d] for g in range(grp)], axis=0)
            if r > 0:
                k_prev, v_prev = km_ref[(r - 1) * w:r * w, hs], vm_ref[(r - 1) * w:r * w, hs]
            else:
                k_prev, v_prev = kp_ref[:, hs], vp_ref[:, hs]
            if r < nb - 1:
                k_next, v_next = km_ref[(r + 1) * w:(r + 2) * w, hs], vm_ref[(r + 1) * w:(r + 2) * w, hs]
            else:
                k_next, v_next = kn_ref[:, hs], vn_ref[:, hs]
            problems.append((q, [k_prev, km_ref[rs, hs], k_next, kc_ref[:, hs]],
                             [v_prev, vm_ref[rs, hs], v_next, vc_ref[:, hs]],
                             [mask_prev, None, mask_next, None], sink_cols[h]))
            places.append((rs, h))
    for (rs, h), o in zip(places, _swa_attend(problems)):
        for g in range(grp):
            o_ref[rs, (h * grp + g) * d:(h * grp + g + 1) * d] = o[g * w:(g + 1) * w].astype(o_ref.dtype)


def swa_attention(q, k, v, kc, vc, sink, nb):
    b, l, nq = q.shape
    nk = k.shape[-1]
    n_ctx = kc.shape[1]
    w = SWA_BLOCK
    tq = nb * w
    last = l // w - 1
    main = lambda bi, i, s: (bi, i, 0)
    prev = lambda bi, i, s: (bi, jnp.maximum(i * nb - 1, 0), 0)
    nxt = lambda bi, i, s: (bi, jnp.minimum((i + 1) * nb, last), 0)
    cx = lambda bi, i, s: (bi, 0, 0)
    return pl.pallas_call(
        functools.partial(_swa_attn_body, nb=nb),
        grid_spec=pltpu.PrefetchScalarGridSpec(
            num_scalar_prefetch=1,
            grid=(b, l // tq),
            in_specs=[pl.BlockSpec((None, tq, nq), main),
                      pl.BlockSpec((None, w, nk), prev), pl.BlockSpec((None, tq, nk), main),
                      pl.BlockSpec((None, w, nk), nxt),
                      pl.BlockSpec((None, w, nk), prev), pl.BlockSpec((None, tq, nk), main),
                      pl.BlockSpec((None, w, nk), nxt),
                      pl.BlockSpec((None, n_ctx, nk), cx), pl.BlockSpec((None, n_ctx, nk), cx)],
            out_specs=pl.BlockSpec((None, tq, nq), main)),
        out_shape=jax.ShapeDtypeStruct((b, l, nq), BF16),
        compiler_params=_cparams(("parallel", "parallel")),
        name="swa_attention",
    )(sink, q, k, k, k, v, v, v, kc, vc)


def _ctx_attn_body(sink_ref, q_ref, kc_ref, vc_ref, o_ref):
    n_ctx, d = q_ref.shape[0], SWA_DIM
    grp = SWA_HEADS // SWA_KV_HEADS
    rows = lax.broadcasted_iota(jnp.int32, (grp * n_ctx, 1), 0)
    problems = []
    for h in range(SWA_KV_HEADS):
        hs = slice(h * d, (h + 1) * d)
        q = jnp.concatenate([q_ref[:, (h * grp + g) * d:(h * grp + g + 1) * d] for g in range(grp)], axis=0)
        sink_col = jnp.zeros((grp * n_ctx, 1), F32)
        for g in range(grp):
            sink_col = jnp.where((rows // n_ctx) == g, sink_ref[h * grp + g], sink_col)
        problems.append((q, [kc_ref[:, hs]], [vc_ref[:, hs]], [None], sink_col))
    for h, o in enumerate(_swa_attend(problems)):
        for g in range(grp):
            o_ref[:, (h * grp + g) * d:(h * grp + g + 1) * d] = o[g * n_ctx:(g + 1) * n_ctx].astype(o_ref.dtype)


def ctx_attention(qc, kc, vc, sink):
    b, n_ctx, nq = qc.shape
    nk = kc.shape[-1]
    cx = lambda bi, s: (bi, 0, 0)
    return pl.pallas_call(
        _ctx_attn_body,
        grid_spec=pltpu.PrefetchScalarGridSpec(
            num_scalar_prefetch=1, grid=(b,),
            in_specs=[pl.BlockSpec((None, n_ctx, nq), cx), pl.BlockSpec((None, n_ctx, nk), cx),
                      pl.BlockSpec((None, n_ctx, nk), cx)],
            out_specs=pl.BlockSpec((None, n_ctx, nq), cx)),
        out_shape=jax.ShapeDtypeStruct((b, n_ctx, nq), BF16),
        compiler_params=_cparams(("parallel",)),
        name="ctx_attention",
    )(sink, qc, kc, vc)


def axial_rope_tables(rows):
    m = SWA_DIM // 4
    inv = ROPE_THETA ** (-np.arange(m, dtype=np.float64) / m)

    def tables(count, lane0):
        ang = np.arange(count, dtype=np.float64)[:, None] * inv
        out = np.zeros((3, count, SWA_DIM))
        out[0, :, lane0:lane0 + m] = out[0, :, lane0 + m:lane0 + 2 * m] = np.cos(ang)
        out[1, :, lane0:lane0 + m] = -np.sin(ang)
        out[2, :, lane0 + m:lane0 + 2 * m] = np.sin(ang)
        return out.astype(np.float32)

    return tables(rows, 0), tables(GRID_W, SWA_DIM // 2)


def window_gqa_pallas(p_lat, p_ctx, col0, rope, q_norm_w, k_norm_w, sink, need_ctx):
    qw = q_norm_w.astype(F32) * SWA_DIM ** -0.5
    ql, kl, vl = swa_prep(p_lat, col0, qw, k_norm_w, rope, PREP_TILE)
    qc, kc, vc = swa_prep(p_ctx, col0, qw, k_norm_w, None, p_ctx.shape[1])
    o_lat = swa_attention(ql, kl, vl, kc, vc, sink, SWA_BLOCKS_PER_STEP)
    o_ctx = ctx_attention(qc, kc, vc, sink) if need_ctx else None
    return o_lat, o_ctx


def _tn_dot(a, b):
    return lax.dot_general(a, b, (((0,), (0,)), ((), ())), preferred_element_type=F32)


def _log_sigmoid(x):
    return jnp.minimum(x, 0.0) - jnp.log(1.0 + jnp.exp(-jnp.abs(x)))


RET_STREAMS = 2 * RET_HEADS


RET_SUBS = 4


def _ret_scan_body(logit_ref, qf_ref, kf_ref, vf_ref, rotf_ref, qb_ref, kb_ref, vb_ref, rotb_ref, rotc_ref,
                   s0_ref, of_ref, ob_ref, sf_ref, s_scr):
    n = pl.program_id(1)
    c, dh = RET_CHUNK, RET_DIM

    @pl.when(n == 0)
    def _():
        s_scr[...] = s0_ref[...]

    ii = lax.broadcasted_iota(jnp.int32, (c, c), 0)
    jj = lax.broadcasted_iota(jnp.int32, (c, c), 1)
    i1 = lax.broadcasted_iota(jnp.int32, (c, 1), 0)
    rel = {True: (ii - jj).astype(F32), False: (jj - ii).astype(F32)}
    pos = {True: i1.astype(F32), False: (c - 1 - i1).astype(F32)}
    streams = [(d == 0, h) for d in range(2) for h in range(RET_HEADS)]
    srcs = {True: (qf_ref, kf_ref, vf_ref, rotf_ref), False: (qb_ref, kb_ref, vb_ref, rotb_ref)}
    subs = rotf_ref.shape[0]
    items = [(r, f, h, j, (j if f else subs - 1 - j)) for j in range(subs) for r, (f, h) in enumerate(streams)]
    rows = lambda blk: slice(blk * c, (blk + 1) * c)
    hs = lambda h: slice(h * dh, (h + 1) * dh)
    rng = range(len(items))

    cos_i, sin_i = rotc_ref[0], rotc_ref[1]
    tables = {}
    for f in (True, False):
        for bk in range(subs):
            a = srcs[f][3][bk]
            tables[f, bk] = (a[0:1] * cos_i - a[1:2] * sin_i, a[2:3] * cos_i + a[3:4] * sin_i)

    def rot(x, f, bk):
        cos_t, sin_t = tables[f, bk]
        return x * cos_t + pltpu.roll(x, dh // 2, 1) * sin_t

    lg = [_log_sigmoid(jnp.full((1, 1), logit_ref[r], F32)) for r in range(RET_STREAMS)]
    q = [rot(srcs[f][0][rows(bk), hs(h)], f, bk) for _, f, h, _, bk in items]
    k = [rot(srcs[f][1][rows(bk), hs(h)], f, bk) * dh ** -0.5 for _, f, h, _, bk in items]
    v = [srcs[f][2][rows(bk), hs(h)].astype(BF16) for _, f, h, _, bk in items]
    dmat = [jnp.where(rel[f] >= 0, jnp.exp(lg[r] * jnp.maximum(rel[f], 0.0)), 0.0)
            for r, (f, _) in enumerate(streams)]
    scores = [(_nt_dot(q[i].astype(BF16), k[i].astype(BF16)) * dmat[items[i][0]]).astype(BF16) for i in rng]
    o_in = [jnp.dot(scores[i], v[i], preferred_element_type=F32) for i in rng]
    q_dec = [(q[i] * jnp.exp(lg[items[i][0]] * (pos[items[i][1]] + 1.0))).astype(BF16) for i in rng]
    k_dec = [(k[i] * jnp.exp(lg[items[i][0]] * (c - 1.0 - pos[items[i][1]]))).astype(BF16) for i in rng]
    kv = [_tn_dot(k_dec[i], v[i]) for i in rng]
    chunk_dec = [jnp.exp(lg[r] * c) for r in range(RET_STREAMS)]

    s = [s_scr[r] for r in range(RET_STREAMS)]
    for j in range(subs):
        idx =[i for i in rng if items[i][3] == j]
        o = [o_in[i] + jnp.dot(q_dec[i], s[items[i][0]].astype(BF16), preferred_element_type=F32) for i in idx]
        s = [s[items[i][0]] * chunk_dec[items[i][0]] + kv[i] for i in idx]
        for i, oi in zip(idx, o):
            _, f, h, _, bk = items[i]
            (of_ref if f else ob_ref)[rows(bk), hs(h)] = oi.astype(of_ref.dtype)
    for r in range(RET_STREAMS):
        s_scr[r] = s[r]

    @pl.when(n == pl.num_programs(1) - 1)
    def _():
        sf_ref[...] = s_scr[...]


def retention_scan(p, col0, decay_logit, rot_chunk, rot_in, s0):
    b, l, _ = p.shape
    subs = min(RET_SUBS, l // RET_CHUNK)
    c, dh, w = RET_CHUNK * subs, RET_DIM, RET_WIDTH
    n = l // c
    cb = col0 // w
    fcol = lambda off: (lambda bi, ni, s: (bi, ni, cb + off))
    bcol = lambda off: (lambda bi, ni, s: (bi, n - 1 - ni, cb + off))
    rot = lambda m: pl.BlockSpec((subs, 4, dh), m)
    st = lambda bi, ni, s: (bi, 0, 0, 0)
    tok = lambda col: [pl.BlockSpec((None, c, w), col(j)) for j in range(3)]
    return pl.pallas_call(
        _ret_scan_body,
        grid_spec=pltpu.PrefetchScalarGridSpec(
            num_scalar_prefetch=1,
            grid=(b, n),
            in_specs=tok(fcol) + [rot(lambda bi, ni, s: (ni, 0, 0))]
            + tok(bcol) + [rot(lambda bi, ni, s: (n - 1 - ni, 0, 0))]
            + [pl.BlockSpec((2, RET_CHUNK, dh), lambda bi, ni, s: (0, 0, 0)),
               pl.BlockSpec((None, RET_STREAMS, dh, dh), st)],
            out_specs=[pl.BlockSpec((None, c, w), lambda bi, ni, s: (bi, ni, 0)),
                       pl.BlockSpec((None, c, w), lambda bi, ni, s: (bi, n - 1 - ni, 0)),
                       pl.BlockSpec((None, RET_STREAMS, dh, dh), st)],
            scratch_shapes=[pltpu.VMEM((RET_STREAMS, dh, dh), F32)]),
        out_shape=[jax.ShapeDtypeStruct((b, l, w), BF16), jax.ShapeDtypeStruct((b, l, w), BF16),
                   jax.ShapeDtypeStruct((b, RET_STREAMS, dh, dh), F32)],
        compiler_params=_cparams(("parallel", "arbitrary")),
        name="retention_scan",
    )(decay_logit.reshape(-1), p, p, p, rot_chunk, p, p, p, rot_chunk, rot_in, s0)


def _ret_final_body(of_ref, ob_ref, g_ref, w_ref, y_ref):
    for h in range(RET_HEADS):
        sl = slice(h * RET_DIM, (h + 1) * RET_DIM)
        o = of_ref[:, sl].astype(F32) + ob_ref[:, sl].astype(F32)
        mu = jnp.mean(o, axis=-1, keepdims=True)
        var = jnp.mean(jnp.square(o - mu), axis=-1, keepdims=True)
        y = (o - mu) * lax.rsqrt(var + EPS) * w_ref[:, sl]
        g = g_ref[:, sl]
        y_ref[:, sl] = (y * (g * jax.nn.sigmoid(g))).astype(y_ref.dtype)


def retention_finalize(o_f, o_b, p, gate_col, gn_w, tl):
    b, l, w = o_f.shape
    tok = lambda bi, i: (bi, i, 0)
    return pl.pallas_call(
        _ret_final_body,
        grid=(b, l // tl),
        in_specs=[pl.BlockSpec((None, tl, w), tok),
                  pl.BlockSpec((None, tl, w), tok),
                  pl.BlockSpec((None, tl, w), lambda bi, i: (bi, i, gate_col // w)),
                  pl.BlockSpec((1, w), lambda bi, i: (0, 0))],
        out_specs=pl.BlockSpec((None, tl, w), tok),
        out_shape=jax.ShapeDtypeStruct((b, l, w), BF16),
        compiler_params=_cparams(("parallel", "parallel")),
        name="retention_finalize",
    )(o_f, o_b, p, gn_w.reshape(1, w))


def retention_rope_tables(l):
    inv = ROPE_THETA ** (-np.linspace(0.0, 1.0, RET_DIM // 2))
    inv2 = np.concatenate([inv, inv])
    sign = np.concatenate([-np.ones(RET_DIM // 2), np.ones(RET_DIM // 2)])
    a = (RET_CHUNK * np.arange(l // RET_CHUNK, dtype=np.float64))[:, None] * inv2
    b = np.arange(RET_CHUNK, dtype=np.float64)[:, None] * inv2
    chunk = np.stack([np.cos(a), np.sin(a), sign * np.sin(a), sign * np.cos(a)], axis=1)
    return chunk.astype(np.float32), np.stack([np.cos(b), np.sin(b)]).astype(np.float32)


def retention_pallas(p_lat, p_ctx, decay_logit, gn_w, need_ctx, col0=0):
    b, l, _ = p_lat.shape
    n_ctx = p_ctx.shape[1]
    gate_col = col0 + 3 * RET_WIDTH
    s0 = jnp.zeros((b, RET_STREAMS, RET_DIM, RET_DIM), F32)
    oc_f, oc_b, s_ctx = retention_scan(p_ctx, col0, decay_logit, *retention_rope_tables(n_ctx), s0)
    ol_f, ol_b, _ = retention_scan(p_lat, col0, decay_logit, *retention_rope_tables(l), s_ctx)
    out_lat = retention_finalize(ol_f, ol_b, p_lat, gate_col, gn_w, PREP_TILE)
    out_ctx = retention_finalize(oc_f, oc_b, p_ctx, gate_col, gn_w, n_ctx) if need_ctx else None
    return out_lat, out_ctx


DN_BLOCK = 128
DN_STREAMS = 2 * DN_HEADS
DN_SUBS = 4


def _shift_rows(x, prev_row, next_row):
    n = x.shape[0]
    r = lax.broadcasted_iota(jnp.int32, (SUBLANES, 1), 0)
    down, up = pltpu.roll(x, 1, 0), pltpu.roll(x, n - 1, 0)
    x_prev = jnp.concatenate([jnp.where(r == 0, prev_row, down[:SUBLANES]), down[SUBLANES:]], axis=0)
    x_next = jnp.concatenate([up[:n - SUBLANES], jnp.where(r == SUBLANES - 1, next_row, up[n - SUBLANES:])], axis=0)
    return x_prev, x_next


def _conv3(x_ref, xp_ref, xn_ref, w_ref):
    i = pl.program_id(1)
    prev_row = jnp.where(i > 0, xp_ref[SUBLANES - 1:SUBLANES, :], 0.0)
    next_row = jnp.where(i < pl.num_programs(1) - 1, xn_ref[0:1, :], 0.0)
    x = x_ref[...]
    x_prev, x_next = _shift_rows(x, prev_row, next_row)
    return x_prev * w_ref[0:1, :] + x * w_ref[1:2, :] + x_next * w_ref[2:3, :]


def _dn_prep_body(x_ref, xp_ref, xn_ref, ab_ref, cw_ref, alog_ref, dtb_ref,
                  q_ref, k_ref, v_ref, gb_ref, gam_ref, gamt_ref):
    y = _conv3(x_ref, xp_ref, xn_ref, cw_ref)
    y = y * jax.nn.sigmoid(y)
    for h in range(DN_HEADS):
        for part, (ref, mul) in enumerate(((q_ref, DN_DIM ** -0.5), (k_ref, 1.0))):
            sl = slice(part * DN_WIDTH + h * DN_DIM, part * DN_WIDTH + (h + 1) * DN_DIM)
            t = y[:, sl]
            t = t * lax.rsqrt(jnp.sum(t * t, axis=-1, keepdims=True) + EPS)
            ref[:, h * DN_DIM:(h + 1) * DN_DIM] = t * mul if mul != 1.0 else t
    v_ref[...] = y[:, 2 * DN_WIDTH:]

    ab = ab_ref[...]
    z = ab + dtb_ref[...]
    softplus = jnp.maximum(z, 0.0) + jnp.log(1.0 + jnp.exp(-jnp.abs(z)))
    g = -jnp.exp(alog_ref[...]) * softplus
    lane = lax.broadcasted_iota(jnp.int32, ab.shape, 1)
    gb_ref[...] = jnp.where(lane < DN_STREAMS, g, jax.nn.sigmoid(ab))

    c = DN_BLOCK
    ii = lax.broadcasted_iota(jnp.int32, (c, c), 0)
    jj = lax.broadcasted_iota(jnp.int32, (c, c), 1)
    tri_f = (ii >= jj).astype(F32)
    lane_c = lax.broadcasted_iota(jnp.int32, (c, ab.shape[1]), 1)
    for n in range(x_ref.shape[0] // c):
        gc = g[n * c:(n + 1) * c, :]
        cf = jnp.dot(tri_f, gc, preferred_element_type=F32, precision=lax.Precision.HIGHEST)
        cb = cf[c - 1:c, :] - cf + gc
        gam = jnp.where(lane_c < DN_HEADS, cf, cb)
        gam_ref[n * c:(n + 1) * c, :] = gam
        gamt_ref[n] = gam.T[:DN_STREAMS, :]


def dn_prep(p, ab_col, conv_w, a_log, dt_bias, tl):
    b, l, _ = p.shape
    c3 = 3 * DN_WIDTH
    nblk8 = l // 8
    lanes = 128
    pad8 = lambda a: jnp.pad(a.reshape(1, DN_STREAMS).astype(F32), ((0, 0), (0, lanes - DN_STREAMS)))
    tok = lambda bi, i: (bi, i, 0)
    return pl.pallas_call(
        _dn_prep_body,
        grid=(b, l // tl),
        in_specs=[pl.BlockSpec((None, tl, c3), tok),
                  pl.BlockSpec((None, 8, c3), lambda bi, i: (bi, jnp.maximum(i * (tl // 8) - 1, 0), 0)),
                  pl.BlockSpec((None, 8, c3), lambda bi, i: (bi, jnp.minimum((i + 1) * (tl // 8), nblk8 - 1), 0)),
                  pl.BlockSpec((None, tl, lanes), lambda bi, i: (bi, i, ab_col // lanes)),
                  pl.BlockSpec((3, c3), lambda bi, i: (0, 0)),
                  pl.BlockSpec((1, lanes), lambda bi, i: (0, 0)),
                  pl.BlockSpec((1, lanes), lambda bi, i: (0, 0))],
        out_specs=[pl.BlockSpec((None, tl, DN_WIDTH), tok)] * 3
        + [pl.BlockSpec((None, tl, lanes), tok)] * 2
        + [pl.BlockSpec((None, tl // DN_BLOCK, DN_STREAMS, DN_BLOCK), lambda bi, i: (bi, i, 0, 0))],
        out_shape=[jax.ShapeDtypeStruct((b, l, DN_WIDTH), F32)] * 3
        + [jax.ShapeDtypeStruct((b, l, lanes), F32)] * 2
        + [jax.ShapeDtypeStruct((b, l // DN_BLOCK, DN_STREAMS, DN_BLOCK), F32)],
        compiler_params=_cparams(("parallel", "parallel")),
        name="dn_prep",
    )(p, p, p, p, conv_w, pad8(a_log), pad8(dt_bias))


def _dot16(a, b):
    return jnp.dot(a.astype(BF16), b.astype(BF16), preferred_element_type=F32)


def _unit_triangular_inverses(ms, ii, jj):
    c = ms[0].shape[0]
    eye = (ii == jj).astype(F32)
    pair = (ii // 2) == (jj // 2)
    ts = [eye - jnp.where(pair, m, 0.0) for m in ms]
    s = 2
    while s < c:
        sub = ((ii // (2 * s)) == (jj // (2 * s))) & ((ii // s) != (jj // s))
        t16 = [t.astype(BF16) for t in ts]
        xs = [_dot16(jnp.where(sub, m, 0.0), t) for m, t in zip(ms, t16)]
        ts = [t - _dot16(tb, x) for t, tb, x in zip(ts, t16, xs)]
        s *= 2
    return ts


def _dn_scan_body(qf_ref, kf_ref, vf_ref, gbf_ref, gamf_ref, gamtf_ref,
                  qb_ref, kb_ref, vb_ref, gbb_ref, gamb_ref, gamtb_ref, s0_ref,
                  of_ref, ob_ref, sf_ref, s_scr):
    n = pl.program_id(1)

    @pl.when(n == 0)
    def _():
        s_scr[...] = s0_ref[...]

    c, dh = DN_BLOCK, DN_DIM
    ii = lax.broadcasted_iota(jnp.int32, (c, c), 0)
    jj = lax.broadcasted_iota(jnp.int32, (c, c), 1)
    incl = {True: ii >= jj, False: ii <= jj}
    strict = {True: ii > jj, False: ii < jj}
    srcs = {True: (qf_ref, kf_ref, vf_ref, gbf_ref, gamf_ref, gamtf_ref),
            False: (qb_ref, kb_ref, vb_ref, gbb_ref, gamb_ref, gamtb_ref)}
    streams = [(d == 0, h) for d in range(2) for h in range(DN_HEADS)]
    hs = lambda h: slice(h * dh, (h + 1) * dh)
    subs = gamtf_ref.shape[0]
    items = [(r, f, h, j, (j if f else subs - 1 - j)) for j in range(subs) for r, (f, h) in enumerate(streams)]
    rows = lambda blk: slice(blk * c, (blk + 1) * c)
    rng = range(len(items))

    q = [srcs[f][0][rows(bk), hs(h)] for _, f, h, _, bk in items]
    k = [srcs[f][1][rows(bk), hs(h)] for _, f, h, _, bk in items]
    v = [srcs[f][2][rows(bk), hs(h)] for _, f, h, _, bk in items]
    beta = [srcs[f][3][rows(bk), DN_STREAMS + r:DN_STREAMS + r + 1] for r, f, _, _, bk in items]
    gcol = [srcs[f][4][rows(bk), r:r + 1] for r, f, _, _, bk in items]
    grow = [srcs[f][5][bk, r:r + 1, :] for r, f, _, _, bk in items]
    g_last = [gcol[i][c - 1:c, :] if items[i][1] else gcol[i][0:1, :] for i in rng]

    e = [jnp.exp(jnp.where(incl[items[i][1]], gcol[i] - grow[i], 0.0)) for i in rng]
    kb = [k[i] * beta[i] for i in rng]
    k16 = [k[i].astype(BF16) for i in rng]
    m = [_nt_dot(kb[i].astype(BF16), k16[i]) * jnp.where(strict[items[i][1]], e[i], 0.0) for i in rng]
    attn = [(_nt_dot(q[i].astype(BF16), k16[i]) * jnp.where(incl[items[i][1]], e[i], 0.0)).astype(BF16) for i in rng]
    t = _unit_triangular_inverses(m, ii, jj)
    eg = [jnp.exp(gcol[i]) for i in rng]
    sol = [_dot16(t[i], jnp.concatenate([v[i] * beta[i], kb[i] * eg[i]], axis=-1)) for i in rng]
    u = [sol[i][:, :dh] for i in rng]
    w16 = [sol[i][:, dh:].astype(BF16) for i in rng]
    q_dec = [(q[i] * eg[i]).astype(BF16) for i in rng]
    k_dec = [(k[i] * jnp.exp(g_last[i] - gcol[i])).astype(BF16) for i in rng]
    blk_dec = [jnp.exp(g_last[i]) for i in rng]

    s = [s_scr[r] for r in range(DN_STREAMS)]
    for j in range(subs):
        idx = [i for i in rng if items[i][3] == j]
        s16 =[s[r].astype(BF16) for r in range(DN_STREAMS)]
        v_new = [(u[i] - jnp.dot(w16[i], s16[items[i][0]], preferred_element_type=F32)).astype(BF16) for i in idx]
        o = [jnp.dot(q_dec[i], s16[items[i][0]], preferred_element_type=F32)
             + jnp.dot(attn[i], vn, preferred_element_type=F32) for i, vn in zip(idx, v_new)]
        s = [s[items[i][0]] * blk_dec[i] + _tn_dot(k_dec[i], vn) for i, vn in zip(idx, v_new)]
        for i, oi in zip(idx, o):
            _, f, h, _, bk = items[i]
            (of_ref if f else ob_ref)[rows(bk), hs(h)] = oi.astype(of_ref.dtype)
    for r in range(DN_STREAMS):
        s_scr[r] = s[r]

    @pl.when(n == pl.num_programs(1) - 1)
    def _():
        sf_ref[...] = s_scr[...]


def dn_scan(q, k, v, gb, gam, gamt, s0):
    b, l, w = q.shape
    subs = min(DN_SUBS, l // DN_BLOCK)
    c = DN_BLOCK * subs
    n = l // c
    lanes = gb.shape[-1]
    f3 = lambda bi, ni: (bi, ni, 0)
    b3 = lambda bi, ni: (bi, n - 1 - ni, 0)
    f4 = lambda bi, ni: (bi, ni, 0, 0)
    b4 = lambda bi, ni: (bi, n - 1 - ni, 0, 0)
    st = lambda bi, ni: (bi, 0, 0, 0)

    def specs(m3, m4):
        return [pl.BlockSpec((None, c, w), m3)] * 3 + [pl.BlockSpec((None, c, lanes), m3)] * 2 + [
            pl.BlockSpec((None, subs, DN_STREAMS, DN_BLOCK), m4)]

    return pl.pallas_call(
        _dn_scan_body,
        grid=(b, n),
        in_specs=specs(f3, f4) + specs(b3, b4) + [pl.BlockSpec((None, DN_STREAMS, DN_DIM, DN_DIM), st)],
        out_specs=[pl.BlockSpec((None, c, w), f3), pl.BlockSpec((None, c, w), b3),
                   pl.BlockSpec((None, DN_STREAMS, DN_DIM, DN_DIM), st)],
        out_shape=[jax.ShapeDtypeStruct((b, l, w), BF16), jax.ShapeDtypeStruct((b, l, w), BF16),
                   jax.ShapeDtypeStruct((b, DN_STREAMS, DN_DIM, DN_DIM), F32)],
        scratch_shapes=[pltpu.VMEM((DN_STREAMS, DN_DIM, DN_DIM), F32)],
        compiler_params=_cparams(("parallel", "arbitrary")),
        name="dn_scan",
    )(q, k, v, gb, gam, gamt, q, k, v, gb, gam, gamt, s0)


def _dn_final_body(of_ref, ob_ref, g_ref, w_ref, y_ref):
    for h in range(DN_HEADS):
        sl = slice(h * DN_DIM, (h + 1) * DN_DIM)
        o = of_ref[:, sl].astype(F32) + ob_ref[:, sl].astype(F32)
        y = _head_rms(o, w_ref[...])
        g = g_ref[:, sl]
        y_ref[:, sl] = (y * (g * jax.nn.sigmoid(g))).astype(y_ref.dtype)


def dn_finalize(o_f, o_b, p, gate_col, norm_w, tl):
    b, l, w = o_f.shape
    tok = lambda bi, i: (bi, i, 0)
    return pl.pallas_call(
        _dn_final_body,
        grid=(b, l // tl),
        in_specs=[pl.BlockSpec((None, tl, w), tok), pl.BlockSpec((None, tl, w), tok),
                  pl.BlockSpec((None, tl, w), lambda bi, i: (bi, i, gate_col // w)),
                  pl.BlockSpec((1, DN_DIM), lambda bi, i: (0, 0))],
        out_specs=pl.BlockSpec((None, tl, w), tok),
        out_shape=jax.ShapeDtypeStruct((b, l, w), BF16),
        compiler_params=_cparams(("parallel", "parallel")),
        name="dn_finalize",
    )(o_f, o_b, p, norm_w.reshape(1, DN_DIM))


def deltanet_pallas(p_lat, p_ctx, conv_w, a_log, dt_bias, norm_w, need_ctx, ab_col=3072, gate_col=1536):
    b = p_lat.shape[0]
    n_ctx = p_ctx.shape[1]
    s0 = jnp.zeros((b, DN_STREAMS, DN_DIM, DN_DIM), F32)
    fc = dn_prep(p_ctx, ab_col, conv_w, a_log, dt_bias, n_ctx)
    oc_f, oc_b, s_ctx = dn_scan(*fc, s0)
    fl = dn_prep(p_lat, ab_col, conv_w, a_log, dt_bias, PREP_TILE)
    ol_f, ol_b, _ = dn_scan(*fl, s_ctx)
    out_lat = dn_finalize(ol_f, ol_b, p_lat, gate_col, norm_w, PREP_TILE)
    out_ctx = dn_finalize(oc_f, oc_b, p_ctx, gate_col, norm_w, n_ctx) if need_ctx else None
    return out_lat, out_ctx


def _hy_factors(n):
    n2 = 256 if n >= 32768 else 128
    return n // n2, n2


def _bf16_const(m):
    return m.astype(ml_dtypes.bfloat16)


def _hy_constants(l):
    n = 2 * l
    n1, n2 = _hy_factors(n)
    a1 = 2.0 * np.pi * np.outer(np.arange(n1), np.arange(n1)) / n1
    c1, s1 = np.cos(a1), np.sin(a1)
    h = n1 // 2
    rows_fwd = np.block([[c1[:, :h], s1[:, :h]], [-s1[:, :h], c1[:, :h]]])
    rows_taps = np.concatenate([c1, -s1], axis=0)
    rows_inv = np.block([[c1[:h], -s1[:h]], [s1[:h], c1[:h]]]) / n
    a2 = 2.0 * np.pi * np.outer(np.arange(n2), np.arange(n2)) / n2
    c2, s2 = np.cos(a2), np.sin(a2)
    slab_fwd = np.block([[c2, s2], [-s2, c2]])
    slab_inv = np.block([[c2, -s2], [s2, c2]])
    th = 2.0 * np.pi * np.outer(np.arange(n1), np.arange(n2)) / n
    lanes = lambda t: jnp.broadcast_to(jnp.asarray(t, F32)[:, :, None], (n1, n2, 128))
    return dict(n1=n1, n2=n2, rows_fwd=_bf16_const(rows_fwd), rows_taps=_bf16_const(rows_taps),
                rows_inv=_bf16_const(rows_inv), slab_fwd=_bf16_const(slab_fwd), slab_inv=_bf16_const(slab_inv),
                tw_cos=lanes(np.cos(th)), tw_sin=lanes(np.sin(th)))


def _hy_prep_body(x_ref, xp_ref, xn_ref, w_ref, v_ref, x1_ref, x2_ref):
    y = _conv3(x_ref, xp_ref, xn_ref, w_ref)
    v_ref[...] = y[:, :HY_CH]
    x1_ref[...] = y[:, HY_CH:2 * HY_CH]
    x2_ref[...] = y[:, 2 * HY_CH:]


def hy_prep(p, col0, conv_w, tl):
    b, l, _ = p.shape
    c3 = 3 * HY_CH
    nblk8 = l // 8
    cb = col0 // c3
    tok = lambda bi, i: (bi, i, 0)
    return pl.pallas_call(
        _hy_prep_body,
        grid=(b, l // tl),
        in_specs=[pl.BlockSpec((None, tl, c3), lambda bi, i: (bi, i, cb)),
                  pl.BlockSpec((None, 8, c3), lambda bi, i: (bi, jnp.maximum(i * (tl // 8) - 1, 0), cb)),
                  pl.BlockSpec((None, 8, c3), lambda bi, i: (bi, jnp.minimum((i + 1) * (tl // 8), nblk8 - 1), cb)),
                  pl.BlockSpec((3, c3), lambda bi, i: (0, 0))],
        out_specs=[pl.BlockSpec((None, tl, HY_CH), tok)] * 3,
        out_shape=[jax.ShapeDtypeStruct((b, l, HY_CH), F32)] * 3,
        compiler_params=_cparams(("parallel", "parallel")),
        name="hy_prep",
    )(p, p, p, conv_w)


def _split_bf16(a):
    hi = a.astype(BF16)
    return hi, (a - hi.astype(F32)).astype(BF16)


HY_FILTER_GROUP = 8


def _hy_filter_body(t_ref, zt_ref, w1t_ref, b1_ref, w2t_ref, b2_ref, w3_ref, rates_ref, m_ref, *out_refs):
    a_refs, sum_ref = out_refs[:-1], out_refs[-1]
    i = pl.program_id(0)
    hp = lax.Precision.HIGHEST
    h = jnp.sin(HY_SIN_FREQ * (jnp.dot(w1t_ref[...], zt_ref[...], preferred_element_type=F32, precision=hp)
                               + b1_ref[...]))
    h = jnp.sin(HY_SIN_FREQ * (jnp.dot(w2t_ref[...], h, preferred_element_type=F32, precision=hp) + b2_ref[...]))
    half = h.shape[1] // 2
    taps = []
    for d in range(2):
        h_hi, h_lo = _split_bf16(h[:, d * half:(d + 1) * half])
        w_hi, w_lo = _split_bf16(w3_ref[d])
        td = _tn_dot(h_hi, w_hi) + (_tn_dot(h_hi, w_lo) + _tn_dot(h_lo, w_hi))
        taps.append(td * jnp.exp(-t_ref[d * half:(d + 1) * half, :] * rates_ref[...]))

    @pl.when(i == 0)
    def _():
        sum_ref[...] = jnp.zeros_like(sum_ref)

    sum_ref[...] += (jnp.sum(jnp.abs(taps[0]), axis=0, keepdims=True)
                     + jnp.sum(jnp.abs(taps[1]), axis=0, keepdims=True))
    row = lax.broadcasted_iota(jnp.int32, (half, 1), 0)
    taps[1] = jnp.where((row == 0) & (i == 0), 0.0, taps[1])
    nh = half // HY_FILTER_GROUP
    for j in range(HY_FILTER_GROUP):
        for o in range(HY_ORDER):
            cs = slice(o * HY_CH, (o + 1) * HY_CH)
            x = jnp.concatenate([taps[0][j * nh:(j + 1) * nh, cs], taps[1][j * nh:(j + 1) * nh, cs]], axis=0)
            y = _dot16(m_ref[...], x)
            a_refs[o][0, :, j, :] = y[:2 * nh]
            a_refs[o][1, :, j, :] = y[2 * nh:]


def hy_filter_rows(l, consts, w1, b1, w2, b2, w3):
    n = 2 * l
    n1, n2, g = consts["n1"], consts["n2"], HY_FILTER_GROUP
    tr = g * n1
    grp, d, j, m = np.meshgrid(np.arange(n2 // g), np.arange(2), np.arange(g), np.arange(n1 // 2), indexing="ij")
    r = (n2 * (m + (n1 // 2) * d) + g * grp + j).reshape(-1)
    pos = np.where(r < l, r, n - r).astype(np.float64)
    pos[r == l] = 0.0
    t = pos / max(l - 1, 1)
    bands = np.linspace(1e-4, HY_BANDS - 1, HY_BANDS)
    phase = (2.0 * math.pi / l) * pos[:, None] * bands[None, :]
    kf = 32
    z = np.concatenate([t[:, None], np.cos(phase), -np.sin(phase)], axis=-1)
    zt = np.pad(z, ((0, 0), (0, kf - z.shape[1]))).T.astype(np.float32)
    t_col = t[:, None].astype(np.float32)
    w1t = jnp.pad(w1.astype(F32), ((0, kf - w1.shape[0]), (0, 0))).T
    hid = w1.shape[1]
    w3d = w3.astype(F32).reshape(hid, HY_ORDER, 2, HY_CH).transpose(2, 0, 1, 3).reshape(2, hid, HY_ORDER * HY_CH)
    rates = jnp.abs(jnp.linspace(math.log(HY_TARGET) / HY_DECAY_LONG, math.log(HY_TARGET) / HY_DECAY_SHORT,
                                 HY_CH, dtype=F32))
    rates = jnp.tile(rates, HY_ORDER).reshape(1, HY_ORDER * HY_CH)
    fixed = lambda i: (0, 0)
    mat = consts["rows_taps"]
    return pl.pallas_call(
        _hy_filter_body,
        grid=(n // tr,),
        in_specs=[pl.BlockSpec((tr, 1), lambda i: (i, 0)), pl.BlockSpec((kf, tr), lambda i: (0, i)),
                  pl.BlockSpec((hid, kf), fixed), pl.BlockSpec((hid, 1), fixed),
                  pl.BlockSpec((hid, hid), fixed), pl.BlockSpec((hid, 1), fixed),
                  pl.BlockSpec((2, hid, HY_ORDER * HY_CH), lambda i: (0, 0, 0)),
                  pl.BlockSpec((1, HY_ORDER * HY_CH), fixed), pl.BlockSpec(mat.shape, fixed)],
        out_specs=[pl.BlockSpec((2, n1, g, HY_CH), lambda i: (0, 0, i, 0))] * HY_ORDER
        + [pl.BlockSpec((1, HY_ORDER * HY_CH), fixed)],
        out_shape=[jax.ShapeDtypeStruct((2, n1, n2, HY_CH), F32)] * HY_ORDER
        + [jax.ShapeDtypeStruct((1, HY_ORDER * HY_CH), F32)],
        compiler_params=_cparams(("arbitrary",)),
        name="hy_filter_rows",
    )(t_col, zt, w1t, b1.astype(F32).reshape(hid, 1), w2.astype(F32).T, b2.astype(F32).reshape(hid, 1), w3d, rates,
      mat)


HY_ROWS_GROUP = {"fwd": 16, "inv": 8}


def _hy_rows_body(*refs, mode):
    if mode == "fwd":
        m_ref, x_ref, o_ref = refs
    else:
        m_ref, x_ref, xn_ref, z_ref, skip_ref, o_ref = refs
    for j in range(HY_ROWS_GROUP[mode]):
        x = jnp.concatenate([x_ref[0, :, j, :], x_ref[1, :, j, :]], axis=0)
        y = _dot16(m_ref[...], x)
        half = y.shape[0] // 2
        for bi in range(2):
            yb = y[bi * half:(bi + 1) * half]
            if mode == "inv":
                yb = xn_ref[bi, :, j, :] * (yb + z_ref[bi, :, j, :] * skip_ref[...])
            o_ref[bi, :, j, :] = yb


def hy_rows_pass(mode, mat, x, extra=()):
    n2, c = x.shape[-2:]
    g = HY_ROWS_GROUP[mode]
    blk = lambda rows: pl.BlockSpec((2, rows, g, c), lambda j: (0, 0, j, 0))
    mspec = _resident(mat.shape)
    vec = pl.BlockSpec((1, c), lambda j: (0, 0))
    if mode == "fwd":
        n1 = 2 * x.shape[1]
        in_specs = [mspec, blk(n1 // 2)]
        out_rows = n1
    else:
        n1 = x.shape[1]
        in_specs = [mspec, blk(n1), blk(n1 // 2), blk(n1 // 2), vec]
        out_rows = n1 // 2
    return pl.pallas_call(
        functools.partial(_hy_rows_body, mode=mode),
        grid=(n2 // g,),
        in_specs=in_specs,
        out_specs=blk(out_rows),
        out_shape=jax.ShapeDtypeStruct((2, out_rows, n2, c), F32),
        compiler_params=_cparams(("parallel",)),
        name="hy_rows_" + mode,
    )(mat, x, *extra)


def _hy_slab_body(*refs, with_filter):
    if with_filter:
        f_ref, i_ref, tc_ref, ts_ref, a_ref, h_ref, o_ref = refs
    else:
        f_ref, tc_ref, ts_ref, a_ref, sum_ref, o_ref = refs
        scale = 1.0 / (sum_ref[...] + EPS)
    reps = a_ref.shape[-1] // tc_ref.shape[-1]
    n2 = a_ref.shape[2]
    for s in range(a_ref.shape[1]):
        tc = jnp.concatenate([tc_ref[s]] * reps, axis=1)
        ts = jnp.concatenate([ts_ref[s]] * reps, axis=1)
        ar, ai = a_ref[0, s], a_ref[1, s]
        x = _dot16(f_ref[...], jnp.concatenate([ar * tc + ai * ts, ai * tc - ar * ts], axis=0))
        xr, xi = x[:n2], x[n2:]
        if not with_filter:
            o_ref[0, s] = (xr * scale).astype(o_ref.dtype)
            o_ref[1, s] = (xi * scale).astype(o_ref.dtype)
            continue
        hr, hi = h_ref[0, s].astype(F32), h_ref[1, s].astype(F32)
        y = _dot16(i_ref[...], jnp.concatenate([xr * hr - xi * hi, xr * hi + xi * hr], axis=0))
        yr, yi = y[:n2], y[n2:]
        o_ref[0, s] = yr * tc - yi * ts
        o_ref[1, s] = yi * tc + yr * ts


HY_SLABS_PER_STEP = 4


def hy_slab_pass(consts, a, h=None, l1_sum=None):
    _, n1, n2, c = a.shape
    kb = HY_SLABS_PER_STEP
    fwd = consts["slab_fwd"]
    mspec = _resident(fwd.shape)
    tw = pl.BlockSpec((kb, n2, 128), lambda k: (k, 0, 0))
    slab = pl.BlockSpec((2, kb, n2, c), lambda k: (0, k, 0, 0))
    if h is None:
        args = [fwd, consts["tw_cos"], consts["tw_sin"], a, l1_sum]
        in_specs = [mspec, tw, tw, slab, pl.BlockSpec((1, c), lambda k: (0, 0))]
    else:
        args = [fwd, consts["slab_inv"], consts["tw_cos"], consts["tw_sin"], a, h]
        in_specs = [mspec, mspec, tw, tw, slab, slab]
    return pl.pallas_call(
        functools.partial(_hy_slab_body, with_filter=h is not None),
        grid=(n1 // kb,),
        in_specs=in_specs,
        out_specs=slab,
        out_shape=jax.ShapeDtypeStruct(a.shape, F32 if h is not None else BF16),
        compiler_params=_cparams(("parallel",)),
        name="hy_slab_conv" if h is not None else "hy_slab_fwd",
    )(*args)


def hyena_pallas(p, conv_w, w1, b1, w2, b2, w3, skip, col0=RET_COLS):
    b, l, _ = p.shape
    assert b == 2, "the two batch rows are packed as one complex signal"
    c = HY_CH
    consts = _hy_constants(l)
    n1, n2 = consts["n1"], consts["n2"]
    *a_h, sums = hy_filter_rows(l, consts, w1, b1, w2, b2, w3)
    spectra = [hy_slab_pass(consts, a_h[o], l1_sum=sums[:, o * c:(o + 1) * c]) for o in range(HY_ORDER)]
    split = lambda t: t.reshape(2, n1 // 2, n2, c)
    v, x1, x2 = hy_prep(p, col0, conv_w, PREP_TILE)
    z = split(v)
    for o, xn in enumerate((x1, x2)):
        a = hy_rows_pass("fwd", consts["rows_fwd"], z)
        bm = hy_slab_pass(consts, a, spectra[o])
        z = hy_rows_pass("inv", consts["rows_inv"], bm,
                         extra=(split(xn), z, skip[o].astype(F32).reshape(1, c)))
    return z.reshape(b, l, c)


def _pad_cols(w, mult):
    n = w.shape[-1]
    pad = (-n) % mult
    return jnp.pad(w, ((0, 0), (0, pad))) if pad else w


def kernel(x, c, ctx, c_ctx, mod_w, mod_b, norm_mix_w, norm_ffn_w, ffn_w_in, ffn_w_out,
           ab_w_in, ab_w_out, dn_conv_w, dn_a_log, dn_dt_bias, dn_norm_w, swa_q_norm_w, swa_k_norm_w,
           swa_sink, cd_w_in, cd_w_out, ret_decay_logit, ret_gn_w, hy_conv_w, hy_f_w1, hy_f_b1,
           hy_f_w2, hy_f_b2, hy_f_w3, hy_bias):
    B, L, D = x.shape
    n_ctx = ctx.shape[1]
    rows = L // GRID_W
    c_rows = jnp.zeros((SUBLANES, D), F32).at[:B].set(c).at[B].set(c_ctx)
    mod_all = modulation(c_rows, mod_w, mod_b)
    hid = ffn_w_out.shape[1]
    h_ctx = ctx
    for layer in range(DEPTH):
        need_ctx = layer != DEPTH - 1
        i = layer // 2
        m = mod_all[layer].reshape(SUBLANES, 6, D)
        mod = [m[:B, j][:, None, :] for j in range(6)]
        mod_c = [jnp.broadcast_to(m[B, j][None, None, :], (B, 1, D)) for j in range(6)]
        if layer % 2 == 0:
            w = ab_w_in[i]
            w_in = jnp.concatenate([w[:, :4 * DN_WIDTH], w[:, DN_COLS:], w[:, 4 * DN_WIDTH:DN_COLS]], axis=1)
            w_out = ab_w_out[i]
        else:
            w = cd_w_in[i]
            w_in = jnp.concatenate([w[:, RET_COLS:], w[:, :RET_COLS]], axis=1)
            w_out = cd_w_out[i]
        w_in_p = _pad_cols(w_in, LANES).astype(BF16)
        p_lat = in_projection(x, norm_mix_w[layer], mod[0], mod[1], w_in_p, DENSE_TILE)
        p_ctx = in_projection(h_ctx, norm_mix_w[layer], mod_c[0], mod_c[1], w_in_p, n_ctx)
        if layer % 2 == 0:
            swa0 = 4 * DN_WIDTH
            ab0 = swa0 + (SWA_HEADS + 2 * SWA_KV_HEADS) * SWA_DIM
            a_lat, a_ctx = deltanet_pallas(p_lat, p_ctx, dn_conv_w[i], dn_a_log[i], dn_dt_bias[i],
                                           dn_norm_w[i], need_ctx, ab_col=ab0, gate_col=3 * DN_WIDTH)
            b_lat, b_ctx = window_gqa_pallas(p_lat, p_ctx, swa0, axial_rope_tables(rows), swa_q_norm_w[i],
                                             swa_k_norm_w[i], swa_sink[i], need_ctx)
        else:
            a_lat, a_ctx = retention_pallas(p_lat, p_ctx, ret_decay_logit[i], ret_gn_w[i], need_ctx,
                                            col0=(HY_ORDER + 1) * HY_CH)
            assert not need_ctx, "the last layer's context outputs reach no latent token"
            b_lat = hyena_pallas(p_lat, hy_conv_w[i], hy_f_w1[i], hy_f_b1[i], hy_f_w2[i], hy_f_b2[i],
                                 hy_f_w3[i], hy_bias[i], col0=0)
            b_ctx = None
        wo = w_out.astype(BF16)
        wg = ffn_w_in[layer][:, :hid].astype(BF16)
        wu = ffn_w_in[layer][:, hid:].astype(BF16)
        wd = ffn_w_out[layer].astype(BF16)
        x = out_projection_ffn(x, a_lat, b_lat, wo, mod[2], norm_ffn_w[layer], mod[3], mod[4], mod[5],
                               wg, wu, wd, DENSE_TILE, FFN_HIDDEN_CHUNK)
        if need_ctx:
            h_ctx = out_projection_ffn(h_ctx, a_ctx, b_ctx, wo, mod_c[2], norm_ffn_w[layer], mod_c[3], mod_c[4],
                                       mod_c[5], wg, wu, wd, n_ctx, FFN_HIDDEN_CHUNK)
    return x
```

```python
import functools
import math

import jax
import jax.numpy as jnp
import ml_dtypes
import numpy as np
from jax import lax
from jax.experimental import pallas as pl
from jax.experimental.pallas import tpu as pltpu

F32 = jnp.float32
BF16 = jnp.bfloat16
EPS = 1e-6

DEPTH = 2
GRID_W = 64

DN_HEADS = 4
DN_DIM = 128
DN_WIDTH = DN_HEADS * DN_DIM
DN_COLS = 4 * DN_WIDTH + 4 * DN_HEADS
SWA_HEADS = 4
SWA_KV_HEADS = 2
SWA_DIM = 128
SWA_BLOCK = 128
ROPE_THETA = 10000.0
RET_HEADS = 4
RET_DIM = 128
RET_WIDTH = RET_HEADS * RET_DIM
RET_CHUNK = 128
RET_COLS = 4 * RET_WIDTH
HY_CH = 512
HY_ORDER = 2
HY_BANDS = 8
HY_SIN_FREQ = 1.0
HY_TARGET = 1e-2
HY_DECAY_SHORT = 0.3
HY_DECAY_LONG = 1.5

VMEM_LIMIT_BYTES = 56 * 1024 * 1024
SUBLANES, LANES = 8, 128

DENSE_TILE = 512
INPROJ_TILE = 1024
FFN_HIDDEN_CHUNK = 256
PREP_TILE = 1024
SWA_BLOCKS_PER_STEP = 4
MOD_COL_TILE = 1536


def _cparams(sem):
    return pltpu.CompilerParams(dimension_semantics=sem, vmem_limit_bytes=VMEM_LIMIT_BYTES)


def _mod_body(c_ref, w_ref, b_ref, o_ref):
    a = c_ref[...]
    a = a * jax.nn.sigmoid(a)
    o_ref[...] = jnp.dot(a, w_ref[...], preferred_element_type=F32,
                         precision=lax.Precision.HIGHEST) + b_ref[...]


def modulation(c_rows, mod_w, mod_b):
    depth, d, n = mod_w.shape
    tn = MOD_COL_TILE
    return pl.pallas_call(
        _mod_body,
        grid=(depth, n // tn),
        in_specs=[pl.BlockSpec((8, d), lambda l, j: (0, 0)),
                  pl.BlockSpec((None, d, tn), lambda l, j: (l, 0, j)),
                  pl.BlockSpec((None, 1, tn), lambda l, j: (l, 0, j))],
        out_specs=pl.BlockSpec((None, 8, tn), lambda l, j: (l, 0, j)),
        out_shape=jax.ShapeDtypeStruct((depth, 8, n), F32),
        compiler_params=_cparams(("parallel", "parallel")),
        name="modulation",
    )(c_rows, mod_w, mod_b.reshape(depth, 1, n))


def _norm_mod(x, nw, shift, scale):
    y = x * lax.rsqrt(jnp.mean(x * x, axis=-1, keepdims=True) + EPS)
    return (y * nw) * (1.0 + scale) + shift


def _resident(shape):
    return pl.BlockSpec(shape, lambda *_: (0,) * len(shape), pipeline_mode=pl.Buffered(1))


def _inproj_body(x_ref, nw_ref, shift_ref, scale_ref, w_ref, o_ref):
    h = _norm_mod(x_ref[...], nw_ref[...], shift_ref[...], scale_ref[...]).astype(BF16)
    o_ref[...] = jnp.dot(h, w_ref[...], preferred_element_type=F32)


def in_projection(x, nw, shift, scale, w, tm):
    b, l, d = x.shape
    n = w.shape[1]
    return pl.pallas_call(
        _inproj_body,
        grid=(b, l // tm),
        in_specs=[pl.BlockSpec((None, tm, d), lambda bi, i: (bi, i, 0)),
                  _resident((1, d)),
                  pl.BlockSpec((None, 1, d), lambda bi, i: (bi, 0, 0)),
                  pl.BlockSpec((None, 1, d), lambda bi, i: (bi, 0, 0)),
                  _resident((d, n))],
        out_specs=pl.BlockSpec((None, tm, n), lambda bi, i: (bi, i, 0)),
        out_shape=jax.ShapeDtypeStruct((b, l, n), F32),
        compiler_params=_cparams(("parallel", "parallel")),
        name="in_projection",
    )(x, nw.reshape(1, d), shift, scale, w)


def _outffn_body(x_ref, oa_ref, ob_ref, wo_ref, g2_ref, nw_ref, sh_ref, sc_ref, g5_ref,
                 wg_ref, wu_ref, wd_ref, y_ref, x1_scr, h_scr, acc_scr, *, th):
    wa = oa_ref.shape[1]
    mix = (jnp.dot(oa_ref[...].astype(BF16), wo_ref[:wa, :], preferred_element_type=F32)
           + jnp.dot(ob_ref[...].astype(BF16), wo_ref[wa:, :], preferred_element_type=F32))
    x1 = x_ref[...] + g2_ref[...] * mix
    x1_scr[...] = x1
    h_scr[...] = _norm_mod(x1, nw_ref[...], sh_ref[...], sc_ref[...]).astype(BF16)
    for k in range(wg_ref.shape[1] // th):
        ks = slice(k * th, (k + 1) * th)
        g = jnp.dot(h_scr[...], wg_ref[:, ks], preferred_element_type=F32)
        u = jnp.dot(h_scr[...], wu_ref[:, ks], preferred_element_type=F32)
        a = (g * jax.nn.sigmoid(g) * u).astype(BF16)
        part = jnp.dot(a, wd_ref[ks, :], preferred_element_type=F32)
        if k == 0:
            acc_scr[...] = part
        else:
            acc_scr[...] += part
    y_ref[...] = x1_scr[...] + g5_ref[...] * acc_scr[...]


def out_projection_ffn(x, oa, ob, wo, g2, nw, shift, scale, g5, wg, wu, wd, tm, th):
    b, l, d = x.shape
    wa, wb = oa.shape[-1], ob.shape[-1]
    hid = wg.shape[1]
    tok = lambda bi, i: (bi, i, 0)
    vec = pl.BlockSpec((None, 1, d), lambda bi, i: (bi, 0, 0))
    return pl.pallas_call(
        functools.partial(_outffn_body, th=th),
        grid=(b, l // tm),
        in_specs=[pl.BlockSpec((None, tm, d), tok),
                  pl.BlockSpec((None, tm, wa), tok),
                  pl.BlockSpec((None, tm, wb), tok),
                  _resident((d, d)), vec, _resident((1, d)), vec, vec, vec,
                  _resident((d, hid)), _resident((d, hid)), _resident((hid, d))],
        out_specs=pl.BlockSpec((None, tm, d), tok),
        out_shape=jax.ShapeDtypeStruct((b, l, d), F32),
        scratch_shapes=[pltpu.VMEM((tm, d), F32), pltpu.VMEM((tm, d), BF16), pltpu.VMEM((tm, d), F32)],
        compiler_params=_cparams(("parallel", "parallel")),
        name="out_projection_ffn",
    )(x, oa, ob, wo, g2, nw.reshape(1, d), shift, scale, g5, wg, wu, wd)


NEG_BIG = -1e30


def _head_rms(x, w):
    return x * lax.rsqrt(jnp.mean(x * x, axis=-1, keepdims=True) + EPS) * w


def _swa_prep_body(*refs, use_rope):
    if use_rope:
        q_ref, k_ref, v_ref, qw_ref, kw_ref, rt_ref, ct_ref, qo_ref, ko_ref, vo_ref = refs
        tl = q_ref.shape[0]
        nr = tl // GRID_W

        def table(kind):
            rowp = jnp.broadcast_to(rt_ref[kind][:, None, :], (nr, GRID_W, SWA_DIM))
            colp = jnp.broadcast_to(ct_ref[kind][None, :, :], (nr, GRID_W, SWA_DIM))
            return (rowp + colp).reshape(tl, SWA_DIM)

        cos, sa, sb = table(0), table(1), table(2)
    else:
        q_ref, k_ref, v_ref, qw_ref, kw_ref, qo_ref, ko_ref, vo_ref = refs

    def prep(x, w):
        y = _head_rms(x, w)
        if use_rope:
            y = y * cos + pltpu.roll(y, SWA_DIM - 32, 1) * sa + pltpu.roll(y, 32, 1) * sb
        return y.astype(BF16)

    for h in range(SWA_HEADS):
        sl = slice(h * SWA_DIM, (h + 1) * SWA_DIM)
        qo_ref[:, sl] = prep(q_ref[:, sl], qw_ref[...])
    for h in range(SWA_KV_HEADS):
        sl = slice(h * SWA_DIM, (h + 1) * SWA_DIM)
        ko_ref[:, sl] = prep(k_ref[:, sl], kw_ref[...])
    vo_ref[...] = v_ref[...].astype(BF16)


def swa_prep(p, col0, qw, kw, rope, tl):
    b, l, _ = p.shape
    nq, nk = SWA_HEADS * SWA_DIM, SWA_KV_HEADS * SWA_DIM
    tok = lambda bi, i: (bi, i, 0)
    in_specs = [pl.BlockSpec((None, tl, nq), lambda bi, i: (bi, i, col0 // nq)),
                pl.BlockSpec((None, tl, nk), lambda bi, i: (bi, i, (col0 + nq) // nk)),
                pl.BlockSpec((None, tl, nk), lambda bi, i: (bi, i, (col0 + nq + nk) // nk)),
                pl.BlockSpec((1, SWA_DIM), lambda bi, i: (0, 0)),
                pl.BlockSpec((1, SWA_DIM), lambda bi, i: (0, 0))]
    args = [p, p, p, qw.reshape(1, SWA_DIM), kw.reshape(1, SWA_DIM)]
    if rope is not None:
        in_specs += [pl.BlockSpec((3, tl // GRID_W, SWA_DIM), lambda bi, i: (0, i, 0)),
                     pl.BlockSpec((3, GRID_W, SWA_DIM), lambda bi, i: (0, 0, 0))]
        args += list(rope)
    return pl.pallas_call(
        functools.partial(_swa_prep_body, use_rope=rope is not None),
        grid=(b, l // tl),
        in_specs=in_specs,
        out_specs=[pl.BlockSpec((None, tl, nq), tok), pl.BlockSpec((None, tl, nk), tok),
                   pl.BlockSpec((None, tl, nk), tok)],
        out_shape=[jax.ShapeDtypeStruct((b, l, nq), BF16), jax.ShapeDtypeStruct((b, l, nk), BF16),
                   jax.ShapeDtypeStruct((b, l, nk), BF16)],
        compiler_params=_cparams(("parallel", "parallel")),
        name="swa_prep",
    )(*args)


def _nt_dot(a, b):
    return lax.dot_general(a, b, (((1,), (1,)), ((), ())), preferred_element_type=F32)


def _swa_attend(problems):
    def lane_tiles(x):
        return [x[:, c * LANES:(c + 1) * LANES] for c in range(x.shape[1] // LANES)]

    def masked_scores(q, keys, masks):
        out = []
        for kk, mask in zip(keys, masks):
            s = _nt_dot(q, kk)
            out.append(s if mask is None else jnp.where(mask, s, NEG_BIG))
        return out

    def row_max(scores, sink_col):
        m = sink_col + jnp.zeros((1, LANES), F32)
        for s in scores:
            for tile in lane_tiles(s):
                m = jnp.maximum(m, tile)
        return jnp.max(m, axis=-1, keepdims=True)

    def weighted(scores, vals, m, sink_col):
        part, acc = None, None
        for s, vv in zip(scores, vals):
            pr = jnp.exp(s - m)
            for tile in lane_tiles(pr):
                part = tile if part is None else part + tile
            o = jnp.dot(pr.astype(BF16), vv, preferred_element_type=F32)
            acc = o if acc is None else acc + o
        return acc, jnp.exp(sink_col - m) + jnp.sum(part, axis=-1, keepdims=True)

    scores = [masked_scores(q, keys, masks) for q, keys, _, masks, _ in problems]
    maxes = [row_max(s, p[4]) for s, p in zip(scores, problems)]
    outs = [weighted(s, p[2], m, p[4]) for s, p, m in zip(scores, problems, maxes)]
    return [acc / den for acc, den in outs]


def _swa_attn_body(sink_ref, q_ref, kp_ref, km_ref, kn_ref, vp_ref, vm_ref, vn_ref, kc_ref, vc_ref,
                   o_ref, *, nb):
    i = pl.program_id(1)
    n_blocks = pl.num_programs(1) * nb
    w, d = SWA_BLOCK, SWA_DIM
    grp = SWA_HEADS // SWA_KV_HEADS
    rows = lax.broadcasted_iota(jnp.int32, (grp * w, w), 0)
    qi = rows % w
    kj = lax.broadcasted_iota(jnp.int32, (grp * w, w), 1)
    sink_cols = []
    for h in range(SWA_KV_HEADS):
        sink_col = jnp.zeros((grp * w, 1), F32)
        for g in range(grp):
            sink_col = jnp.where((rows[:, :1] // w) == g, sink_ref[h * grp + g], sink_col)
        sink_cols.append(sink_col)
    problems, places = [], []
    for r in range(nb):
        blk = i * nb + r
        mask_prev = (kj >= qi) & (blk > 0)
        mask_next = (kj <= qi) & (blk < n_blocks - 1)
        rs = slice(r * w, (r + 1) * w)
        for h in range(SWA_KV_HEADS):
            hs = slice(h * d, (h + 1) * d)
            q = jnp.concatenate([q_ref[rs, (h * grp + g) * d:(h * grp + g + 1) * d] for g in range(grp)], axis=0)
            if r > 0:
                k_prev, v_prev = km_ref[(r - 1) * w:r * w, hs], vm_ref[(r - 1) * w:r * w, hs]
            else:
                k_prev, v_prev = kp_ref[:, hs], vp_ref[:, hs]
            if r < nb - 1:
                k_next, v_next = km_ref[(r + 1) * w:(r + 2) * w, hs], vm_ref[(r + 1) * w:(r + 2) * w, hs]
            else:
                k_next, v_next = kn_ref[:, hs], vn_ref[:, hs]
            problems.append((q, [k_prev, km_ref[rs, hs], k_next, kc_ref[:, hs]],
                             [v_prev, vm_ref[rs, hs], v_next, vc_ref[:, hs]],
                             [mask_prev, None, mask_next, None], sink_cols[h]))
            places.append((rs, h))
    for (rs, h), o in zip(places, _swa_attend(problems)):
        for g in range(grp):
            o_ref[rs, (h * grp + g) * d:(h * grp + g + 1) * d] = o[g * w:(g + 1) * w].astype(o_ref.dtype)


def swa_attention(q, k, v, kc, vc, sink, nb):
    b, l, nq = q.shape
    nk = k.shape[-1]
    n_ctx = kc.shape[1]
    w = SWA_BLOCK
    tq = nb * w
    last = l // w - 1
    main = lambda bi, i, s: (bi, i, 0)
    prev = lambda bi, i, s: (bi, jnp.maximum(i * nb - 1, 0), 0)
    nxt = lambda bi, i, s: (bi, jnp.minimum((i + 1) * nb, last), 0)
    cx = lambda bi, i, s: (bi, 0, 0)
    return pl.pallas_call(
        functools.partial(_swa_attn_body, nb=nb),
        grid_spec=pltpu.PrefetchScalarGridSpec(
            num_scalar_prefetch=1,
            grid=(b, l // tq),
            in_specs=[pl.BlockSpec((None, tq, nq), main),
                      pl.BlockSpec((None, w, nk), prev), pl.BlockSpec((None, tq, nk), main),
                      pl.BlockSpec((None, w, nk), nxt),
                      pl.BlockSpec((None, w, nk), prev), pl.BlockSpec((None, tq, nk), main),
                      pl.BlockSpec((None, w, nk), nxt),
                      pl.BlockSpec((None, n_ctx, nk), cx), pl.BlockSpec((None, n_ctx, nk), cx)],
            out_specs=pl.BlockSpec((None, tq, nq), main)),
        out_shape=jax.ShapeDtypeStruct((b, l, nq), BF16),
        compiler_params=_cparams(("parallel", "parallel")),
        name="swa_attention",
    )(sink, q, k, k, k, v, v, v, kc, vc)


def _ctx_attn_body(sink_ref, q_ref, kc_ref, vc_ref, o_ref):
    n_ctx, d = q_ref.shape[0], SWA_DIM
    grp = SWA_HEADS // SWA_KV_HEADS
    rows = lax.broadcasted_iota(jnp.int32, (grp * n_ctx, 1), 0)
    problems = []
    for h in range(SWA_KV_HEADS):
        hs = slice(h * d, (h + 1) * d)
        q = jnp.concatenate([q_ref[:, (h * grp + g) * d:(h * grp + g + 1) * d] for g in range(grp)], axis=0)
        sink_col = jnp.zeros((grp * n_ctx, 1), F32)
        for g in range(grp):
            sink_col = jnp.where((rows // n_ctx) == g, sink_ref[h * grp + g], sink_col)
        problems.append((q, [kc_ref[:, hs]], [vc_ref[:, hs]], [None], sink_col))
    for h, o in enumerate(_swa_attend(problems)):
        for g in range(grp):
            o_ref[:, (h * grp + g) * d:(h * grp + g + 1) * d] = o[g * n_ctx:(g + 1) * n_ctx].astype(o_ref.dtype)


def ctx_attention(qc, kc, vc, sink):
    b, n_ctx, nq = qc.shape
    nk = kc.shape[-1]
    cx = lambda bi, s: (bi, 0, 0)
    return pl.pallas_call(
        _ctx_attn_body,
        grid_spec=pltpu.PrefetchScalarGridSpec(
            num_scalar_prefetch=1, grid=(b,),
            in_specs=[pl.BlockSpec((None, n_ctx, nq), cx), pl.BlockSpec((None, n_ctx, nk), cx),
                      pl.BlockSpec((None, n_ctx, nk), cx)],
            out_specs=pl.BlockSpec((None, n_ctx, nq), cx)),
        out_shape=jax.ShapeDtypeStruct((b, n_ctx, nq), BF16),
        compiler_params=_cparams(("parallel",)),
        name="ctx_attention",
    )(sink, qc, kc, vc)


def axial_rope_tables(rows):
    m = SWA_DIM // 4
    inv = ROPE_THETA ** (-np.arange(m, dtype=np.float64) / m)

    def tables(count, lane0):
        ang = np.arange(count, dtype=np.float64)[:, None] * inv
        out = np.zeros((3, count, SWA_DIM))
        out[0, :, lane0:lane0 + m] = out[0, :, lane0 + m:lane0 + 2 * m] = np.cos(ang)
        out[1, :, lane0:lane0 + m] = -np.sin(ang)
        out[2, :, lane0 + m:lane0 + 2 * m] = np.sin(ang)
        return out.astype(np.float32)

    return tables(rows, 0), tables(GRID_W, SWA_DIM // 2)


def window_gqa_pallas(p_lat, p_ctx, col0, rope, q_norm_w, k_norm_w, sink, need_ctx):
    qw = q_norm_w.astype(F32) * SWA_DIM ** -0.5
    ql, kl, vl = swa_prep(p_lat, col0, qw, k_norm_w, rope, PREP_TILE)
    qc, kc, vc = swa_prep(p_ctx, col0, qw, k_norm_w, None, p_ctx.shape[1])
    o_lat = swa_attention(ql, kl, vl, kc, vc, sink, SWA_BLOCKS_PER_STEP)
    o_ctx = ctx_attention(qc, kc, vc, sink) if need_ctx else None
    return o_lat, o_ctx


def _tn_dot(a, b):
    return lax.dot_general(a, b, (((0,), (0,)), ((), ())), preferred_element_type=F32)


def _log_sigmoid(x):
    return jnp.minimum(x, 0.0) - jnp.log(1.0 + jnp.exp(-jnp.abs(x)))


RET_STREAMS = 2 * RET_HEADS


RET_SUBS = 4


def _ret_scan_body(logit_ref, qf_ref, kf_ref, vf_ref, rotf_ref, qb_ref, kb_ref, vb_ref, rotb_ref, rotc_ref,
                   s0_ref, of_ref, ob_ref, sf_ref, s_scr):
    n = pl.program_id(1)
    c, dh = RET_CHUNK, RET_DIM

    @pl.when(n == 0)
    def _():
        s_scr[...] = s0_ref[...]

    ii = lax.broadcasted_iota(jnp.int32, (c, c), 0)
    jj = lax.broadcasted_iota(jnp.int32, (c, c), 1)
    i1 = lax.broadcasted_iota(jnp.int32, (c, 1), 0)
    rel = {True: (ii - jj).astype(F32), False: (jj - ii).astype(F32)}
    pos = {True: i1.astype(F32), False: (c - 1 - i1).astype(F32)}
    streams = [(d == 0, h) for d in range(2) for h in range(RET_HEADS)]
    srcs = {True: (qf_ref, kf_ref, vf_ref, rotf_ref), False: (qb_ref, kb_ref, vb_ref, rotb_ref)}
    subs = rotf_ref.shape[0]
    items = [(r, f, h, j, (j if f else subs - 1 - j)) for j in range(subs) for r, (f, h) in enumerate(streams)]
    rows = lambda blk: slice(blk * c, (blk + 1) * c)
    hs = lambda h: slice(h * dh, (h + 1) * dh)
    rng = range(len(items))

    cos_i, sin_i = rotc_ref[0], rotc_ref[1]
    tables = {}
    for f in (True, False):
        for bk in range(subs):
            a = srcs[f][3][bk]
            tables[f, bk] = (a[0:1] * cos_i - a[1:2] * sin_i, a[2:3] * cos_i + a[3:4] * sin_i)

    def rot(x, f, bk):
        cos_t, sin_t = tables[f, bk]
        return x * cos_t + pltpu.roll(x, dh // 2, 1) * sin_t

    lg = [_log_sigmoid(jnp.full((1, 1), logit_ref[r], F32)) for r in range(RET_STREAMS)]
    q = [rot(srcs[f][0][rows(bk), hs(h)], f, bk) for _, f, h, _, bk in items]
    k = [rot(srcs[f][1][rows(bk), hs(h)], f, bk) * dh ** -0.5 for _, f, h, _, bk in items]
    v = [srcs[f][2][rows(bk), hs(h)].astype(BF16) for _, f, h, _, bk in items]
    dmat = [jnp.where(rel[f] >= 0, jnp.exp(lg[r] * jnp.maximum(rel[f], 0.0)), 0.0)
            for r, (f, _) in enumerate(streams)]
    scores = [(_nt_dot(q[i].astype(BF16), k[i].astype(BF16)) * dmat[items[i][0]]).astype(BF16) for i in rng]
    o_in = [jnp.dot(scores[i], v[i], preferred_element_type=F32) for i in rng]
    q_dec = [(q[i] * jnp.exp(lg[items[i][0]] * (pos[items[i][1]] + 1.0))).astype(BF16) for i in rng]
    k_dec = [(k[i] * jnp.exp(lg[items[i][0]] * (c - 1.0 - pos[items[i][1]]))).astype(BF16) for i in rng]
    kv = [_tn_dot(k_dec[i], v[i]) for i in rng]
    chunk_dec = [jnp.exp(lg[r] * c) for r in range(RET_STREAMS)]

    s = [s_scr[r] for r in range(RET_STREAMS)]
    for j in range(subs):
        idx =[i for i in rng if items[i][3] == j]
        o = [o_in[i] + jnp.dot(q_dec[i], s[items[i][0]].astype(BF16), preferred_element_type=F32) for i in idx]
        s = [s[items[i][0]] * chunk_dec[items[i][0]] + kv[i] for i in idx]
        for i, oi in zip(idx, o):
            _, f, h, _, bk = items[i]
            (of_ref if f else ob_ref)[rows(bk), hs(h)] = oi.astype(of_ref.dtype)
    for r in range(RET_STREAMS):
        s_scr[r] = s[r]

    @pl.when(n == pl.num_programs(1) - 1)
    def _():
        sf_ref[...] = s_scr[...]


def retention_scan(p, col0, decay_logit, rot_chunk, rot_in, s0):
    b, l, _ = p.shape
    subs = min(RET_SUBS, l // RET_CHUNK)
    c, dh, w = RET_CHUNK * subs, RET_DIM, RET_WIDTH
    n = l // c
    cb = col0 // w
    fcol = lambda off: (lambda bi, ni, s: (bi, ni, cb + off))
    bcol = lambda off: (lambda bi, ni, s: (bi, n - 1 - ni, cb + off))
    rot = lambda m: pl.BlockSpec((subs, 4, dh), m)
    st = lambda bi, ni, s: (bi, 0, 0, 0)
    tok = lambda col: [pl.BlockSpec((None, c, w), col(j)) for j in range(3)]
    return pl.pallas_call(
        _ret_scan_body,
        grid_spec=pltpu.PrefetchScalarGridSpec(
            num_scalar_prefetch=1,
            grid=(b, n),
            in_specs=tok(fcol) + [rot(lambda bi, ni, s: (ni, 0, 0))]
            + tok(bcol) + [rot(lambda bi, ni, s: (n - 1 - ni, 0, 0))]
            + [pl.BlockSpec((2, RET_CHUNK, dh), lambda bi, ni, s: (0, 0, 0)),
               pl.BlockSpec((None, RET_STREAMS, dh, dh), st)],
            out_specs=[pl.BlockSpec((None, c, w), lambda bi, ni, s: (bi, ni, 0)),
                       pl.BlockSpec((None, c, w), lambda bi, ni, s: (bi, n - 1 - ni, 0)),
                       pl.BlockSpec((None, RET_STREAMS, dh, dh), st)],
            scratch_shapes=[pltpu.VMEM((RET_STREAMS, dh, dh), F32)]),
        out_shape=[jax.ShapeDtypeStruct((b, l, w), BF16), jax.ShapeDtypeStruct((b, l, w), BF16),
                   jax.ShapeDtypeStruct((b, RET_STREAMS, dh, dh), F32)],
        compiler_params=_cparams(("parallel", "arbitrary")),
        name="retention_scan",
    )(decay_logit.reshape(-1), p, p, p, rot_chunk, p, p, p, rot_chunk, rot_in, s0)


def _ret_final_body(of_ref, ob_ref, g_ref, w_ref, y_ref):
    for h in range(RET_HEADS):
        sl = slice(h * RET_DIM, (h + 1) * RET_DIM)
        o = of_ref[:, sl].astype(F32) + ob_ref[:, sl].astype(F32)
        mu = jnp.mean(o, axis=-1, keepdims=True)
        var = jnp.mean(jnp.square(o - mu), axis=-1, keepdims=True)
        y = (o - mu) * lax.rsqrt(var + EPS) * w_ref[:, sl]
        g = g_ref[:, sl]
        y_ref[:, sl] = (y * (g * jax.nn.sigmoid(g))).astype(y_ref.dtype)


def retention_finalize(o_f, o_b, p, gate_col, gn_w, tl):
    b, l, w = o_f.shape
    tok = lambda bi, i: (bi, i, 0)
    return pl.pallas_call(
        _ret_final_body,
        grid=(b, l // tl),
        in_specs=[pl.BlockSpec((None, tl, w), tok),
                  pl.BlockSpec((None, tl, w), tok),
                  pl.BlockSpec((None, tl, w), lambda bi, i: (bi, i, gate_col // w)),
                  pl.BlockSpec((1, w), lambda bi, i: (0, 0))],
        out_specs=pl.BlockSpec((None, tl, w), tok),
        out_shape=jax.ShapeDtypeStruct((b, l, w), BF16),
        compiler_params=_cparams(("parallel", "parallel")),
        name="retention_finalize",
    )(o_f, o_b, p, gn_w.reshape(1, w))


def retention_rope_tables(l):
    inv = ROPE_THETA ** (-np.linspace(0.0, 1.0, RET_DIM // 2))
    inv2 = np.concatenate([inv, inv])
    sign = np.concatenate([-np.ones(RET_DIM // 2), np.ones(RET_DIM // 2)])
    a = (RET_CHUNK * np.arange(l // RET_CHUNK, dtype=np.float64))[:, None] * inv2
    b = np.arange(RET_CHUNK, dtype=np.float64)[:, None] * inv2
    chunk = np.stack([np.cos(a), np.sin(a), sign * np.sin(a), sign * np.cos(a)], axis=1)
    return chunk.astype(np.float32), np.stack([np.cos(b), np.sin(b)]).astype(np.float32)


def retention_pallas(p_lat, p_ctx, decay_logit, gn_w, need_ctx, col0=0):
    b, l, _ = p_lat.shape
    n_ctx = p_ctx.shape[1]
    gate_col = col0 + 3 * RET_WIDTH
    s0 = jnp.zeros((b, RET_STREAMS, RET_DIM, RET_DIM), F32)
    oc_f, oc_b, s_ctx = retention_scan(p_ctx, col0, decay_logit, *retention_rope_tables(n_ctx), s0)
    ol_f, ol_b, _ = retention_scan(p_lat, col0, decay_logit, *retention_rope_tables(l), s_ctx)
    out_lat = retention_finalize(ol_f, ol_b, p_lat, gate_col, gn_w, PREP_TILE)
    out_ctx = retention_finalize(oc_f, oc_b, p_ctx, gate_col, gn_w, n_ctx) if need_ctx else None
    return out_lat, out_ctx


DN_BLOCK = 128
DN_STREAMS = 2 * DN_HEADS
DN_SUBS = 4


def _shift_rows(x, prev_row, next_row):
    n = x.shape[0]
    r = lax.broadcasted_iota(jnp.int32, (SUBLANES, 1), 0)
    down, up = pltpu.roll(x, 1, 0), pltpu.roll(x, n - 1, 0)
    x_prev = jnp.concatenate([jnp.where(r == 0, prev_row, down[:SUBLANES]), down[SUBLANES:]], axis=0)
    x_next = jnp.concatenate([up[:n - SUBLANES], jnp.where(r == SUBLANES - 1, next_row, up[n - SUBLANES:])], axis=0)
    return x_prev, x_next


def _conv3(x_ref, xp_ref, xn_ref, w_ref):
    i = pl.program_id(1)
    prev_row = jnp.where(i > 0, xp_ref[SUBLANES - 1:SUBLANES, :], 0.0)
    next_row = jnp.where(i < pl.num_programs(1) - 1, xn_ref[0:1, :], 0.0)
    x = x_ref[...]
    x_prev, x_next = _shift_rows(x, prev_row, next_row)
    return x_prev * w_ref[0:1, :] + x * w_ref[1:2, :] + x_next * w_ref[2:3, :]


def _dn_prep_body(x_ref, xp_ref, xn_ref, ab_ref, cw_ref, alog_ref, dtb_ref,
                  q_ref, k_ref, v_ref, gb_ref, gam_ref, gamt_ref):
    y = _conv3(x_ref, xp_ref, xn_ref, cw_ref)
    y = y * jax.nn.sigmoid(y)
    for h in range(DN_HEADS):
        for part, (ref, mul) in enumerate(((q_ref, DN_DIM ** -0.5), (k_ref, 1.0))):
            sl = slice(part * DN_WIDTH + h * DN_DIM, part * DN_WIDTH + (h + 1) * DN_DIM)
            t = y[:, sl]
            t = t * lax.rsqrt(jnp.sum(t * t, axis=-1, keepdims=True) + EPS)
            ref[:, h * DN_DIM:(h + 1) * DN_DIM] = t * mul if mul != 1.0 else t
    v_ref[...] = y[:, 2 * DN_WIDTH:]

    ab = ab_ref[...]
    z = ab + dtb_ref[...]
    softplus = jnp.maximum(z, 0.0) + jnp.log(1.0 + jnp.exp(-jnp.abs(z)))
    g = -jnp.exp(alog_ref[...]) * softplus
    lane = lax.broadcasted_iota(jnp.int32, ab.shape, 1)
    gb_ref[...] = jnp.where(lane < DN_STREAMS, g, jax.nn.sigmoid(ab))

    c = DN_BLOCK
    ii = lax.broadcasted_iota(jnp.int32, (c, c), 0)
    jj = lax.broadcasted_iota(jnp.int32, (c, c), 1)
    tri_f = (ii >= jj).astype(F32)
    lane_c = lax.broadcasted_iota(jnp.int32, (c, ab.shape[1]), 1)
    for n in range(x_ref.shape[0] // c):
        gc = g[n * c:(n + 1) * c, :]
        cf = jnp.dot(tri_f, gc, preferred_element_type=F32, precision=lax.Precision.HIGHEST)
        cb = cf[c - 1:c, :] - cf + gc
        gam = jnp.where(lane_c < DN_HEADS, cf, cb)
        gam_ref[n * c:(n + 1) * c, :] = gam
        gamt_ref[n] = gam.T[:DN_STREAMS, :]


def dn_prep(p, ab_col, conv_w, a_log, dt_bias, tl):
    b, l, _ = p.shape
    c3 = 3 * DN_WIDTH
    nblk8 = l // 8
    lanes = 128
    pad8 = lambda a: jnp.pad(a.reshape(1, DN_STREAMS).astype(F32), ((0, 0), (0, lanes - DN_STREAMS)))
    tok = lambda bi, i: (bi, i, 0)
    return pl.pallas_call(
        _dn_prep_body,
        grid=(b, l // tl),
        in_specs=[pl.BlockSpec((None, tl, c3), tok),
                  pl.BlockSpec((None, 8, c3), lambda bi, i: (bi, jnp.maximum(i * (tl // 8) - 1, 0), 0)),
                  pl.BlockSpec((None, 8, c3), lambda bi, i: (bi, jnp.minimum((i + 1) * (tl // 8), nblk8 - 1), 0)),
                  pl.BlockSpec((None, tl, lanes), lambda bi, i: (bi, i, ab_col // lanes)),
                  pl.BlockSpec((3, c3), lambda bi, i: (0, 0)),
                  pl.BlockSpec((1, lanes), lambda bi, i: (0, 0)),
                  pl.BlockSpec((1, lanes), lambda bi, i: (0, 0))],
        out_specs=[pl.BlockSpec((None, tl, DN_WIDTH), tok)] * 3
        + [pl.BlockSpec((None, tl, lanes), tok)] * 2
        + [pl.BlockSpec((None, tl // DN_BLOCK, DN_STREAMS, DN_BLOCK), lambda bi, i: (bi, i, 0, 0))],
        out_shape=[jax.ShapeDtypeStruct((b, l, DN_WIDTH), F32)] * 3
        + [jax.ShapeDtypeStruct((b, l, lanes), F32)] * 2
        + [jax.ShapeDtypeStruct((b, l // DN_BLOCK, DN_STREAMS, DN_BLOCK), F32)],
        compiler_params=_cparams(("parallel", "parallel")),
        name="dn_prep",
    )(p, p, p, p, conv_w, pad8(a_log), pad8(dt_bias))


def _dot16(a, b):
    return jnp.dot(a.astype(BF16), b.astype(BF16), preferred_element_type=F32)


def _unit_triangular_inverses(ms, ii, jj):
    c = ms[0].shape[0]
    eye = (ii == jj).astype(F32)
    pair = (ii // 2) == (jj // 2)
    ts = [eye - jnp.where(pair, m, 0.0) for m in ms]
    s = 2
    while s < c:
        sub = ((ii // (2 * s)) == (jj // (2 * s))) & ((ii // s) != (jj // s))
        t16 = [t.astype(BF16) for t in ts]
        xs = [_dot16(jnp.where(sub, m, 0.0), t) for m, t in zip(ms, t16)]
        ts = [t - _dot16(tb, x) for t, tb, x in zip(ts, t16, xs)]
        s *= 2
    return ts


def _dn_scan_body(qf_ref, kf_ref, vf_ref, gbf_ref, gamf_ref, gamtf_ref,
                  qb_ref, kb_ref, vb_ref, gbb_ref, gamb_ref, gamtb_ref, s0_ref,
                  of_ref, ob_ref, sf_ref, s_scr):
    n = pl.program_id(1)

    @pl.when(n == 0)
    def _():
        s_scr[...] = s0_ref[...]

    c, dh = DN_BLOCK, DN_DIM
    ii = lax.broadcasted_iota(jnp.int32, (c, c), 0)
    jj = lax.broadcasted_iota(jnp.int32, (c, c), 1)
    incl = {True: ii >= jj, False: ii <= jj}
    strict = {True: ii > jj, False: ii < jj}
    srcs = {True: (qf_ref, kf_ref, vf_ref, gbf_ref, gamf_ref, gamtf_ref),
            False: (qb_ref, kb_ref, vb_ref, gbb_ref, gamb_ref, gamtb_ref)}
    streams = [(d == 0, h) for d in range(2) for h in range(DN_HEADS)]
    hs = lambda h: slice(h * dh, (h + 1) * dh)
    subs = gamtf_ref.shape[0]
    items = [(r, f, h, j, (j if f else subs - 1 - j)) for j in range(subs) for r, (f, h) in enumerate(streams)]
    rows = lambda blk: slice(blk * c, (blk + 1) * c)
    rng = range(len(items))

    q = [srcs[f][0][rows(bk), hs(h)] for _, f, h, _, bk in items]
    k = [srcs[f][1][rows(bk), hs(h)] for _, f, h, _, bk in items]
    v = [srcs[f][2][rows(bk), hs(h)] for _, f, h, _, bk in items]
    beta = [srcs[f][3][rows(bk), DN_STREAMS + r:DN_STREAMS + r + 1] for r, f, _, _, bk in items]
    gcol = [srcs[f][4][rows(bk), r:r + 1] for r, f, _, _, bk in items]
    grow = [srcs[f][5][bk, r:r + 1, :] for r, f, _, _, bk in items]
    g_last = [gcol[i][c - 1:c, :] if items[i][1] else gcol[i][0:1, :] for i in rng]

    e = [jnp.exp(jnp.where(incl[items[i][1]], gcol[i] - grow[i], 0.0)) for i in rng]
    kb = [k[i] * beta[i] for i in rng]
    k16 = [k[i].astype(BF16) for i in rng]
    m = [_nt_dot(kb[i].astype(BF16), k16[i]) * jnp.where(strict[items[i][1]], e[i], 0.0) for i in rng]
    attn = [(_nt_dot(q[i].astype(BF16), k16[i]) * jnp.where(incl[items[i][1]], e[i], 0.0)).astype(BF16) for i in rng]
    t = _unit_triangular_inverses(m, ii, jj)
    eg = [jnp.exp(gcol[i]) for i in rng]
    sol = [_dot16(t[i], jnp.concatenate([v[i] * beta[i], kb[i] * eg[i]], axis=-1)) for i in rng]
    u = [sol[i][:, :dh] for i in rng]
    w16 = [sol[i][:, dh:].astype(BF16) for i in rng]
    q_dec = [(q[i] * eg[i]).astype(BF16) for i in rng]
    k_dec = [(k[i] * jnp.exp(g_last[i] - gcol[i])).astype(BF16) for i in rng]
    blk_dec = [jnp.exp(g_last[i]) for i in rng]

    s = [s_scr[r] for r in range(DN_STREAMS)]
    for j in range(subs):
        idx = [i for i in rng if items[i][3] == j]
        s16 =[s[r].astype(BF16) for r in range(DN_STREAMS)]
        v_new = [(u[i] - jnp.dot(w16[i], s16[items[i][0]], preferred_element_type=F32)).astype(BF16) for i in idx]
        o = [jnp.dot(q_dec[i], s16[items[i][0]], preferred_element_type=F32)
             + jnp.dot(attn[i], vn, preferred_element_type=F32) for i, vn in zip(idx, v_new)]
        s = [s[items[i][0]] * blk_dec[i] + _tn_dot(k_dec[i], vn) for i, vn in zip(idx, v_new)]
        for i, oi in zip(idx, o):
            _, f, h, _, bk = items[i]
            (of_ref if f else ob_ref)[rows(bk), hs(h)] = oi.astype(of_ref.dtype)
    for r in range(DN_STREAMS):
        s_scr[r] = s[r]

    @pl.when(n == pl.num_programs(1) - 1)
    def _():
        sf_ref[...] = s_scr[...]


def dn_scan(q, k, v, gb, gam, gamt, s0):
    b, l, w = q.shape
    subs = min(DN_SUBS, l // DN_BLOCK)
    c = DN_BLOCK * subs
    n = l // c
    lanes = gb.shape[-1]
    f3 = lambda bi, ni: (bi, ni, 0)
    b3 = lambda bi, ni: (bi, n - 1 - ni, 0)
    f4 = lambda bi, ni: (bi, ni, 0, 0)
    b4 = lambda bi, ni: (bi, n - 1 - ni, 0, 0)
    st = lambda bi, ni: (bi, 0, 0, 0)

    def specs(m3, m4):
        return [pl.BlockSpec((None, c, w), m3)] * 3 + [pl.BlockSpec((None, c, lanes), m3)] * 2 + [
            pl.BlockSpec((None, subs, DN_STREAMS, DN_BLOCK), m4)]

    return pl.pallas_call(
        _dn_scan_body,
        grid=(b, n),
        in_specs=specs(f3, f4) + specs(b3, b4) + [pl.BlockSpec((None, DN_STREAMS, DN_DIM, DN_DIM), st)],
        out_specs=[pl.BlockSpec((None, c, w), f3), pl.BlockSpec((None, c, w), b3),
                   pl.BlockSpec((None, DN_STREAMS, DN_DIM, DN_DIM), st)],
        out_shape=[jax.ShapeDtypeStruct((b, l, w), BF16), jax.ShapeDtypeStruct((b, l, w), BF16),
                   jax.ShapeDtypeStruct((b, DN_STREAMS, DN_DIM, DN_DIM), F32)],
        scratch_shapes=[pltpu.VMEM((DN_STREAMS, DN_DIM, DN_DIM), F32)],
        compiler_params=_cparams(("parallel", "arbitrary")),
        name="dn_scan",
    )(q, k, v, gb, gam, gamt, q, k, v, gb, gam, gamt, s0)


def _dn_final_body(of_ref, ob_ref, g_ref, w_ref, y_ref):
    for h in range(DN_HEADS):
        sl = slice(h * DN_DIM, (h + 1) * DN_DIM)
        o = of_ref[:, sl].astype(F32) + ob_ref[:, sl].astype(F32)
        y = _head_rms(o, w_ref[...])
        g = g_ref[:, sl]
        y_ref[:, sl] = (y * (g * jax.nn.sigmoid(g))).astype(y_ref.dtype)


def dn_finalize(o_f, o_b, p, gate_col, norm_w, tl):
    b, l, w = o_f.shape
    tok = lambda bi, i: (bi, i, 0)
    return pl.pallas_call(
        _dn_final_body,
        grid=(b, l // tl),
        in_specs=[pl.BlockSpec((None, tl, w), tok), pl.BlockSpec((None, tl, w), tok),
                  pl.BlockSpec((None, tl, w), lambda bi, i: (bi, i, gate_col // w)),
                  pl.BlockSpec((1, DN_DIM), lambda bi, i: (0, 0))],
        out_specs=pl.BlockSpec((None, tl, w), tok),
        out_shape=jax.ShapeDtypeStruct((b, l, w), BF16),
        compiler_params=_cparams(("parallel", "parallel")),
        name="dn_finalize",
    )(o_f, o_b, p, norm_w.reshape(1, DN_DIM))


def deltanet_pallas(p_lat, p_ctx, conv_w, a_log, dt_bias, norm_w, need_ctx, ab_col=3072, gate_col=1536):
    b = p_lat.shape[0]
    n_ctx = p_ctx.shape[1]
    s0 = jnp.zeros((b, DN_STREAMS, DN_DIM, DN_DIM), F32)
    fc = dn_prep(p_ctx, ab_col, conv_w, a_log, dt_bias, n_ctx)
    oc_f, oc_b, s_ctx = dn_scan(*fc, s0)
    fl = dn_prep(p_lat, ab_col, conv_w, a_log, dt_bias, PREP_TILE)
    ol_f, ol_b, _ = dn_scan(*fl, s_ctx)
    out_lat = dn_finalize(ol_f, ol_b, p_lat, gate_col, norm_w, PREP_TILE)
    out_ctx = dn_finalize(oc_f, oc_b, p_ctx, gate_col, norm_w, n_ctx) if need_ctx else None
    return out_lat, out_ctx


def _hy_factors(n):
    n2 = 256 if n >= 32768 else 128
    return n // n2, n2


def _bf16_const(m):
    return m.astype(ml_dtypes.bfloat16)


def _hy_constants(l):
    n = 2 * l
    n1, n2 = _hy_factors(n)
    a1 = 2.0 * np.pi * np.outer(np.arange(n1), np.arange(n1)) / n1
    c1, s1 = np.cos(a1), np.sin(a1)
    h = n1 // 2
    rows_fwd = np.block([[c1[:, :h], s1[:, :h]], [-s1[:, :h], c1[:, :h]]])
    rows_taps = np.concatenate([c1, -s1], axis=0)
    rows_inv = np.block([[c1[:h], -s1[:h]], [s1[:h], c1[:h]]]) / n
    a2 = 2.0 * np.pi * np.outer(np.arange(n2), np.arange(n2)) / n2
    c2, s2 = np.cos(a2), np.sin(a2)
    slab_fwd = np.block([[c2, s2], [-s2, c2]])
    slab_inv = np.block([[c2, -s2], [s2, c2]])
    th = 2.0 * np.pi * np.outer(np.arange(n1), np.arange(n2)) / n
    lanes = lambda t: t[:, :, None].astype(np.float32)
    return dict(n1=n1, n2=n2, rows_fwd=_bf16_const(rows_fwd), rows_taps=_bf16_const(rows_taps),
                rows_inv=_bf16_const(rows_inv), slab_fwd=_bf16_const(slab_fwd), slab_inv=_bf16_const(slab_inv),
                tw_cos=lanes(np.cos(th)), tw_sin=lanes(np.sin(th)))


def _hy_prep_body(x_ref, xp_ref, xn_ref, w_ref, v_ref, x1_ref, x2_ref):
    y = _conv3(x_ref, xp_ref, xn_ref, w_ref)
    v_ref[...] = y[:, :HY_CH]
    x1_ref[...] = y[:, HY_CH:2 * HY_CH]
    x2_ref[...] = y[:, 2 * HY_CH:]


def hy_prep(p, col0, conv_w, tl):
    b, l, _ = p.shape
    c3 = 3 * HY_CH
    nblk8 = l // 8
    cb = col0 // c3
    tok = lambda bi, i: (bi, i, 0)
    return pl.pallas_call(
        _hy_prep_body,
        grid=(b, l // tl),
        in_specs=[pl.BlockSpec((None, tl, c3), lambda bi, i: (bi, i, cb)),
                  pl.BlockSpec((None, 8, c3), lambda bi, i: (bi, jnp.maximum(i * (tl // 8) - 1, 0), cb)),
                  pl.BlockSpec((None, 8, c3), lambda bi, i: (bi, jnp.minimum((i + 1) * (tl // 8), nblk8 - 1), cb)),
                  pl.BlockSpec((3, c3), lambda bi, i: (0, 0))],
        out_specs=[pl.BlockSpec((None, tl, HY_CH), tok)] * 3,
        out_shape=[jax.ShapeDtypeStruct((b, l, HY_CH), F32)] * 3,
        compiler_params=_cparams(("parallel", "parallel")),
        name="hy_prep",
    )(p, p, p, conv_w)


def _split_bf16(a):
    hi = a.astype(BF16)
    return hi, (a - hi.astype(F32)).astype(BF16)


HY_FILTER_GROUP = 8


def _hy_filter_body(t_ref, zt_ref, w1t_ref, b1_ref, w2t_ref, b2_ref, w3_ref, rates_ref, m_ref, *out_refs):
    a_refs, sum_ref = out_refs[:-1], out_refs[-1]
    i = pl.program_id(0)
    hp = lax.Precision.HIGHEST
    h = jnp.sin(HY_SIN_FREQ * (jnp.dot(w1t_ref[...], zt_ref[...], preferred_element_type=F32, precision=hp)
                               + b1_ref[...]))
    h = jnp.sin(HY_SIN_FREQ * (jnp.dot(w2t_ref[...], h, preferred_element_type=F32, precision=hp) + b2_ref[...]))
    half = h.shape[1] // 2
    taps = []
    for d in range(2):
        h_hi, h_lo = _split_bf16(h[:, d * half:(d + 1) * half])
        w_hi, w_lo = _split_bf16(w3_ref[d])
        td = _tn_dot(h_hi, w_hi) + (_tn_dot(h_hi, w_lo) + _tn_dot(h_lo, w_hi))
        taps.append(td * jnp.exp(-t_ref[d * half:(d + 1) * half, :] * rates_ref[...]))

    @pl.when(i == 0)
    def _():
        sum_ref[...] = jnp.zeros_like(sum_ref)

    sum_ref[...] += (jnp.sum(jnp.abs(taps[0]), axis=0, keepdims=True)
                     + jnp.sum(jnp.abs(taps[1]), axis=0, keepdims=True))
    row = lax.broadcasted_iota(jnp.int32, (half, 1), 0)
    taps[1] = jnp.where((row == 0) & (i == 0), 0.0, taps[1])
    nh = half // HY_FILTER_GROUP
    for j in range(HY_FILTER_GROUP):
        for o in range(HY_ORDER):
            cs = slice(o * HY_CH, (o + 1) * HY_CH)
            x = jnp.concatenate([taps[0][j * nh:(j + 1) * nh, cs], taps[1][j * nh:(j + 1) * nh, cs]], axis=0)
            y = _dot16(m_ref[...], x)
            a_refs[o][0, :, j, :] = y[:2 * nh]
            a_refs[o][1, :, j, :] = y[2 * nh:]


def hy_filter_rows(l, consts, w1, b1, w2, b2, w3):
    n = 2 * l
    n1, n2, g = consts["n1"], consts["n2"], HY_FILTER_GROUP
    tr = g * n1
    grp, d, j, m = np.meshgrid(np.arange(n2 // g), np.arange(2), np.arange(g), np.arange(n1 // 2), indexing="ij")
    r = (n2 * (m + (n1 // 2) * d) + g * grp + j).reshape(-1)
    pos = np.where(r < l, r, n - r).astype(np.float64)
    pos[r == l] = 0.0
    t = pos / max(l - 1, 1)
    bands = np.linspace(1e-4, HY_BANDS - 1, HY_BANDS)
    phase = (2.0 * math.pi / l) * pos[:, None] * bands[None, :]
    kf = 32
    z = np.concatenate([t[:, None], np.cos(phase), -np.sin(phase)], axis=-1)
    zt = np.pad(z, ((0, 0), (0, kf - z.shape[1]))).T.astype(np.float32)
    t_col = t[:, None].astype(np.float32)
    w1t = jnp.pad(w1.astype(F32), ((0, kf - w1.shape[0]), (0, 0))).T
    hid = w1.shape[1]
    w3d = w3.astype(F32).reshape(hid, HY_ORDER, 2, HY_CH).transpose(2, 0, 1, 3).reshape(2, hid, HY_ORDER * HY_CH)
    rates = jnp.abs(jnp.linspace(math.log(HY_TARGET) / HY_DECAY_LONG, math.log(HY_TARGET) / HY_DECAY_SHORT,
                                 HY_CH, dtype=F32))
    rates = jnp.tile(rates, HY_ORDER).reshape(1, HY_ORDER * HY_CH)
    fixed = lambda i: (0, 0)
    mat = consts["rows_taps"]
    return pl.pallas_call(
        _hy_filter_body,
        grid=(n // tr,),
        in_specs=[pl.BlockSpec((tr, 1), lambda i: (i, 0)), pl.BlockSpec((kf, tr), lambda i: (0, i)),
                  pl.BlockSpec((hid, kf), fixed), pl.BlockSpec((hid, 1), fixed),
                  pl.BlockSpec((hid, hid), fixed), pl.BlockSpec((hid, 1), fixed),
                  pl.BlockSpec((2, hid, HY_ORDER * HY_CH), lambda i: (0, 0, 0)),
                  pl.BlockSpec((1, HY_ORDER * HY_CH), fixed), pl.BlockSpec(mat.shape, fixed)],
        out_specs=[pl.BlockSpec((2, n1, g, HY_CH), lambda i: (0, 0, i, 0))] * HY_ORDER
        + [pl.BlockSpec((1, HY_ORDER * HY_CH), fixed)],
        out_shape=[jax.ShapeDtypeStruct((2, n1, n2, HY_CH), F32)] * HY_ORDER
        + [jax.ShapeDtypeStruct((1, HY_ORDER * HY_CH), F32)],
        compiler_params=_cparams(("arbitrary",)),
        name="hy_filter_rows",
    )(t_col, zt, w1t, b1.astype(F32).reshape(hid, 1), w2.astype(F32).T, b2.astype(F32).reshape(hid, 1), w3d, rates,
      mat)


HY_ROWS_GROUP = {"fwd": 16, "inv": 8}


def _hy_rows_body(*refs, mode):
    if mode == "fwd":
        m_ref, x_ref, o_ref = refs
    else:
        m_ref, x_ref, xn_ref, z_ref, skip_ref, o_ref = refs
    for j in range(HY_ROWS_GROUP[mode]):
        x = jnp.concatenate([x_ref[0, :, j, :], x_ref[1, :, j, :]], axis=0)
        y = _dot16(m_ref[...], x)
        half = y.shape[0] // 2
        for bi in range(2):
            yb = y[bi * half:(bi + 1) * half]
            if mode == "inv":
                yb = xn_ref[bi, :, j, :] * (yb + z_ref[bi, :, j, :] * skip_ref[...])
            o_ref[bi, :, j, :] = yb


def hy_rows_pass(mode, mat, x, extra=()):
    n2, c = x.shape[-2:]
    g = HY_ROWS_GROUP[mode]
    blk = lambda rows: pl.BlockSpec((2, rows, g, c), lambda j: (0, 0, j, 0))
    mspec = _resident(mat.shape)
    vec = pl.BlockSpec((1, c), lambda j: (0, 0))
    if mode == "fwd":
        n1 = 2 * x.shape[1]
        in_specs = [mspec, blk(n1 // 2)]
        out_rows = n1
    else:
        n1 = x.shape[1]
        in_specs = [mspec, blk(n1), blk(n1 // 2), blk(n1 // 2), vec]
        out_rows = n1 // 2
    return pl.pallas_call(
        functools.partial(_hy_rows_body, mode=mode),
        grid=(n2 // g,),
        in_specs=in_specs,
        out_specs=blk(out_rows),
        out_shape=jax.ShapeDtypeStruct((2, out_rows, n2, c), F32),
        compiler_params=_cparams(("parallel",)),
        name="hy_rows_" + mode,
    )(mat, x, *extra)


def _hy_slab_body(*refs, with_filter):
    if with_filter:
        f_ref, i_ref, tc_ref, ts_ref, a_ref, h_ref, o_ref = refs
    else:
        f_ref, tc_ref, ts_ref, a_ref, sum_ref, o_ref = refs
        scale = 1.0 / (sum_ref[...] + EPS)
    n2, c = a_ref.shape[2:]
    for s in range(a_ref.shape[1]):
        tc = jnp.broadcast_to(tc_ref[s], (n2, c))
        ts = jnp.broadcast_to(ts_ref[s], (n2, c))
        ar, ai = a_ref[0, s], a_ref[1, s]
        x = _dot16(f_ref[...], jnp.concatenate([ar * tc + ai * ts, ai * tc - ar * ts], axis=0))
        xr, xi = x[:n2], x[n2:]
        if not with_filter:
            o_ref[0, s] = (xr * scale).astype(o_ref.dtype)
            o_ref[1, s] = (xi * scale).astype(o_ref.dtype)
            continue
        hr, hi = h_ref[0, s].astype(F32), h_ref[1, s].astype(F32)
        y = _dot16(i_ref[...], jnp.concatenate([xr * hr - xi * hi, xr * hi + xi * hr], axis=0))
        yr, yi = y[:n2], y[n2:]
        o_ref[0, s] = yr * tc - yi * ts
        o_ref[1, s] = yi * tc + yr * ts


HY_SLABS_PER_STEP = 4


def hy_slab_pass(consts, a, h=None, l1_sum=None):
    _, n1, n2, c = a.shape
    kb = HY_SLABS_PER_STEP
    fwd = consts["slab_fwd"]
    mspec = _resident(fwd.shape)
    tw = pl.BlockSpec((kb, n2, 1), lambda k: (k, 0, 0))
    slab = pl.BlockSpec((2, kb, n2, c), lambda k: (0, k, 0, 0))
    if h is None:
        args = [fwd, consts["tw_cos"], consts["tw_sin"], a, l1_sum]
        in_specs = [mspec, tw, tw, slab, pl.BlockSpec((1, c), lambda k: (0, 0))]
    else:
        args = [fwd, consts["slab_inv"], consts["tw_cos"], consts["tw_sin"], a, h]
        in_specs = [mspec, mspec, tw, tw, slab, slab]
    return pl.pallas_call(
        functools.partial(_hy_slab_body, with_filter=h is not None),
        grid=(n1 // kb,),
        in_specs=in_specs,
        out_specs=slab,
        out_shape=jax.ShapeDtypeStruct(a.shape, F32 if h is not None else BF16),
        compiler_params=_cparams(("parallel",)),
        name="hy_slab_conv" if h is not None else "hy_slab_fwd",
    )(*args)


def hyena_pallas(p, conv_w, w1, b1, w2, b2, w3, skip, col0=RET_COLS):
    b, l, _ = p.shape
    assert b == 2, "the two batch rows are packed as one complex signal"
    c = HY_CH
    consts = _hy_constants(l)
    n1, n2 = consts["n1"], consts["n2"]
    *a_h, sums = hy_filter_rows(l, consts, w1, b1, w2, b2, w3)
    spectra = [hy_slab_pass(consts, a_h[o], l1_sum=sums[:, o * c:(o + 1) * c]) for o in range(HY_ORDER)]
    split = lambda t: t.reshape(2, n1 // 2, n2, c)
    v, x1, x2 = hy_prep(p, col0, conv_w, PREP_TILE)
    z = split(v)
    for o, xn in enumerate((x1, x2)):
        a = hy_rows_pass("fwd", consts["rows_fwd"], z)
        bm = hy_slab_pass(consts, a, spectra[o])
        z = hy_rows_pass("inv", consts["rows_inv"], bm,
                         extra=(split(xn), z, skip[o].astype(F32).reshape(1, c)))
    return z.reshape(b, l, c)


def _pad_cols(w, mult):
    n = w.shape[-1]
    pad = (-n) % mult
    return jnp.pad(w, ((0, 0), (0, pad))) if pad else w


def kernel(x, c, ctx, c_ctx, mod_w, mod_b, norm_mix_w, norm_ffn_w, ffn_w_in, ffn_w_out,
           ab_w_in, ab_w_out, dn_conv_w, dn_a_log, dn_dt_bias, dn_norm_w, swa_q_norm_w, swa_k_norm_w,
           swa_sink, cd_w_in, cd_w_out, ret_decay_logit, ret_gn_w, hy_conv_w, hy_f_w1, hy_f_b1,
           hy_f_w2, hy_f_b2, hy_f_w3, hy_bias):
    B, L, D = x.shape
    n_ctx = ctx.shape[1]
    rows = L // GRID_W
    c_rows = jnp.zeros((SUBLANES, D), F32).at[:B].set(c).at[B].set(c_ctx)
    mod_all = modulation(c_rows, mod_w, mod_b)
    hid = ffn_w_out.shape[1]
    h_ctx = ctx
    for layer in range(DEPTH):
        need_ctx = layer != DEPTH - 1
        i = layer // 2
        m = mod_all[layer].reshape(SUBLANES, 6, D)
        mod = [m[:B, j][:, None, :] for j in range(6)]
        mod_c = [jnp.broadcast_to(m[B, j][None, None, :], (B, 1, D)) for j in range(6)]
        if layer % 2 == 0:
            w = ab_w_in[i]
            w_in = jnp.concatenate([w[:, :4 * DN_WIDTH], w[:, DN_COLS:], w[:, 4 * DN_WIDTH:DN_COLS]], axis=1)
            w_out = ab_w_out[i]
        else:
            w = cd_w_in[i]
            w_in = jnp.concatenate([w[:, RET_COLS:], w[:, :RET_COLS]], axis=1)
            w_out = cd_w_out[i]
        w_in_p = _pad_cols(w_in, LANES).astype(BF16)
        p_lat = in_projection(x, norm_mix_w[layer], mod[0], mod[1], w_in_p, INPROJ_TILE)
        p_ctx = in_projection(h_ctx, norm_mix_w[layer], mod_c[0], mod_c[1], w_in_p, n_ctx)
        if layer % 2 == 0:
            swa0 = 4 * DN_WIDTH
            ab0 = swa0 + (SWA_HEADS + 2 * SWA_KV_HEADS) * SWA_DIM
            a_lat, a_ctx = deltanet_pallas(p_lat, p_ctx, dn_conv_w[i], dn_a_log[i], dn_dt_bias[i],
                                           dn_norm_w[i], need_ctx, ab_col=ab0, gate_col=3 * DN_WIDTH)
            b_lat, b_ctx = window_gqa_pallas(p_lat, p_ctx, swa0, axial_rope_tables(rows), swa_q_norm_w[i],
                                             swa_k_norm_w[i], swa_sink[i], need_ctx)
        else:
            a_lat, a_ctx = retention_pallas(p_lat, p_ctx, ret_decay_logit[i], ret_gn_w[i], need_ctx,
                                            col0=(HY_ORDER + 1) * HY_CH)
            assert not need_ctx, "the last layer's context outputs reach no latent token"
            b_lat = hyena_pallas(p_lat, hy_conv_w[i], hy_f_w1[i], hy_f_b1[i], hy_f_w2[i], hy_f_b2[i],
                                 hy_f_w3[i], hy_bias[i], col0=0)
            b_ctx = None
        wo = w_out.astype(BF16)
        wg = ffn_w_in[layer][:, :hid].astype(BF16)
        wu = ffn_w_in[layer][:, hid:].astype(BF16)
        wd = ffn_w_out[layer].astype(BF16)
        x = out_projection_ffn(x, a_lat, b_lat, wo, mod[2], norm_ffn_w[layer], mod[3], mod[4], mod[5],
                               wg, wu, wd, DENSE_TILE, FFN_HIDDEN_CHUNK)
        if need_ctx:
            h_ctx = out_projection_ffn(h_ctx, a_ctx, b_ctx, wo, mod_c[2], norm_ffn_w[layer], mod_c[3], mod_c[4],
                                       mod_c[5], wg, wu, wd, n_ctx, FFN_HIDDEN_CHUNK)
    return x
```

```python
import functools
import math

import jax
import jax.numpy as jnp
import ml_dtypes
import numpy as np
from jax import lax
from jax.experimental import pallas as pl
from jax.experimental.pallas import tpu as pltpu

F32 = jnp.float32
BF16 = jnp.bfloat16
EPS = 1e-6

DEPTH = 2
GRID_W = 64

DN_HEADS = 4
DN_DIM = 128
DN_WIDTH = DN_HEADS * DN_DIM
DN_COLS = 4 * DN_WIDTH + 4 * DN_HEADS
SWA_HEADS = 4
SWA_KV_HEADS = 2
SWA_DIM = 128
SWA_BLOCK = 128
ROPE_THETA = 10000.0
RET_HEADS = 4
RET_DIM = 128
RET_WIDTH = RET_HEADS * RET_DIM
RET_CHUNK = 128
RET_COLS = 4 * RET_WIDTH
HY_CH = 512
HY_ORDER = 2
HY_BANDS = 8
HY_SIN_FREQ = 1.0
HY_TARGET = 1e-2
HY_DECAY_SHORT = 0.3
HY_DECAY_LONG = 1.5

VMEM_LIMIT_BYTES = 56 * 1024 * 1024
SUBLANES, LANES = 8, 128

DENSE_TILE = 512
INPROJ_TILE = 1024
FFN_HIDDEN_CHUNK = 256
PREP_TILE = 1024
SWA_BLOCKS_PER_STEP = 4
MOD_COL_TILE = 1536


def _cparams(sem):
    return pltpu.CompilerParams(dimension_semantics=sem, vmem_limit_bytes=VMEM_LIMIT_BYTES)


def _mod_body(c_ref, w_ref, b_ref, o_ref):
    a = c_ref[...]
    a = a * jax.nn.sigmoid(a)
    o_ref[...] = jnp.dot(a, w_ref[...], preferred_element_type=F32,
                         precision=lax.Precision.HIGHEST) + b_ref[...]


def modulation(c_rows, mod_w, mod_b):
    depth, d, n = mod_w.shape
    tn = MOD_COL_TILE
    return pl.pallas_call(
        _mod_body,
        grid=(depth, n // tn),
        in_specs=[pl.BlockSpec((8, d), lambda l, j: (0, 0)),
                  pl.BlockSpec((None, d, tn), lambda l, j: (l, 0, j)),
                  pl.BlockSpec((None, 1, tn), lambda l, j: (l, 0, j))],
        out_specs=pl.BlockSpec((None, 8, tn), lambda l, j: (l, 0, j)),
        out_shape=jax.ShapeDtypeStruct((depth, 8, n), F32),
        compiler_params=_cparams(("parallel", "parallel")),
        name="modulation",
    )(c_rows, mod_w, mod_b.reshape(depth, 1, n))


def _norm_mod(x, nw, shift, scale):
    y = x * lax.rsqrt(jnp.mean(x * x, axis=-1, keepdims=True) + EPS)
    return (y * nw) * (1.0 + scale) + shift


def _resident(shape):
    return pl.BlockSpec(shape, lambda *_: (0,) * len(shape), pipeline_mode=pl.Buffered(1))


def _inproj_body(x_ref, nw_ref, shift_ref, scale_ref, w_ref, o_ref):
    h = _norm_mod(x_ref[...], nw_ref[...], shift_ref[...], scale_ref[...]).astype(BF16)
    o_ref[...] = jnp.dot(h, w_ref[...], preferred_element_type=F32)


def in_projection(x, nw, shift, scale, w, tm):
    b, l, d = x.shape
    n = w.shape[1]
    return pl.pallas_call(
        _inproj_body,
        grid=(b, l // tm),
        in_specs=[pl.BlockSpec((None, tm, d), lambda bi, i: (bi, i, 0)),
                  _resident((1, d)),
                  pl.BlockSpec((None, 1, d), lambda bi, i: (bi, 0, 0)),
                  pl.BlockSpec((None, 1, d), lambda bi, i: (bi, 0, 0)),
                  _resident((d, n))],
        out_specs=pl.BlockSpec((None, tm, n), lambda bi, i: (bi, i, 0)),
        out_shape=jax.ShapeDtypeStruct((b, l, n), F32),
        compiler_params=_cparams(("parallel", "parallel")),
        name="in_projection",
    )(x, nw.reshape(1, d), shift, scale, w)


def _outffn_body(x_ref, oa_ref, ob_ref, wo_ref, g2_ref, nw_ref, sh_ref, sc_ref, g5_ref,
                 wg_ref, wu_ref, wd_ref, y_ref, x1_scr, h_scr, acc_scr, *, th):
    wa = oa_ref.shape[1]
    mix = (jnp.dot(oa_ref[...].astype(BF16), wo_ref[:wa, :], preferred_element_type=F32)
           + jnp.dot(ob_ref[...].astype(BF16), wo_ref[wa:, :], preferred_element_type=F32))
    x1 = x_ref[...] + g2_ref[...] * mix
    x1_scr[...] = x1
    h_scr[...] = _norm_mod(x1, nw_ref[...], sh_ref[...], sc_ref[...]).astype(BF16)
    for k in range(wg_ref.shape[1] // th):
        ks = slice(k * th, (k + 1) * th)
        g = jnp.dot(h_scr[...], wg_ref[:, ks], preferred_element_type=F32)
        u = jnp.dot(h_scr[...], wu_ref[:, ks], preferred_element_type=F32)
        a = (g * jax.nn.sigmoid(g) * u).astype(BF16)
        part = jnp.dot(a, wd_ref[ks, :], preferred_element_type=F32)
        if k == 0:
            acc_scr[...] = part
        else:
            acc_scr[...] += part
    y_ref[...] = x1_scr[...] + g5_ref[...] * acc_scr[...]


def out_projection_ffn(x, oa, ob, wo, g2, nw, shift, scale, g5, wg, wu, wd, tm, th):
    b, l, d = x.shape
    wa, wb = oa.shape[-1], ob.shape[-1]
    hid = wg.shape[1]
    tok = lambda bi, i: (bi, i, 0)
    vec = pl.BlockSpec((None, 1, d), lambda bi, i: (bi, 0, 0))
    return pl.pallas_call(
        functools.partial(_outffn_body, th=th),
        grid=(b, l // tm),
        in_specs=[pl.BlockSpec((None, tm, d), tok),
                  pl.BlockSpec((None, tm, wa), tok),
                  pl.BlockSpec((None, tm, wb), tok),
                  _resident((d, d)), vec, _resident((1, d)), vec, vec, vec,
                  _resident((d, hid)), _resident((d, hid)), _resident((hid, d))],
        out_specs=pl.BlockSpec((None, tm, d), tok),
        out_shape=jax.ShapeDtypeStruct((b, l, d), F32),
        scratch_shapes=[pltpu.VMEM((tm, d), F32), pltpu.VMEM((tm, d), BF16), pltpu.VMEM((tm, d), F32)],
        compiler_params=_cparams(("parallel", "parallel")),
        name="out_projection_ffn",
    )(x, oa, ob, wo, g2, nw.reshape(1, d), shift, scale, g5, wg, wu, wd)


NEG_BIG = -1e30


def _head_rms(x, w):
    return x * lax.rsqrt(jnp.mean(x * x, axis=-1, keepdims=True) + EPS) * w


def _swa_prep_body(*refs, use_rope):
    if use_rope:
        q_ref, k_ref, v_ref, qw_ref, kw_ref, rt_ref, ct_ref, qo_ref, ko_ref, vo_ref = refs
        tl = q_ref.shape[0]
        nr = tl // GRID_W

        def table(kind):
            rowp = jnp.broadcast_to(rt_ref[kind][:, None, :], (nr, GRID_W, SWA_DIM))
            colp = jnp.broadcast_to(ct_ref[kind][None, :, :], (nr, GRID_W, SWA_DIM))
            return (rowp + colp).reshape(tl, SWA_DIM)

        cos, sa, sb = table(0), table(1), table(2)
    else:
        q_ref, k_ref, v_ref, qw_ref, kw_ref, qo_ref, ko_ref, vo_ref = refs

    def prep(x, w):
        y = _head_rms(x, w)
        if use_rope:
            y = y * cos + pltpu.roll(y, SWA_DIM - 32, 1) * sa + pltpu.roll(y, 32, 1) * sb
        return y.astype(BF16)

    for h in range(SWA_HEADS):
        sl = slice(h * SWA_DIM, (h + 1) * SWA_DIM)
        qo_ref[:, sl] = prep(q_ref[:, sl], qw_ref[...])
    for h in range(SWA_KV_HEADS):
        sl = slice(h * SWA_DIM, (h + 1) * SWA_DIM)
        ko_ref[:, sl] = prep(k_ref[:, sl], kw_ref[...])
    vo_ref[...] = v_ref[...].astype(BF16)


def swa_prep(p, col0, qw, kw, rope, tl):
    b, l, _ = p.shape
    nq, nk = SWA_HEADS * SWA_DIM, SWA_KV_HEADS * SWA_DIM
    tok = lambda bi, i: (bi, i, 0)
    in_specs = [pl.BlockSpec((None, tl, nq), lambda bi, i: (bi, i, col0 // nq)),
                pl.BlockSpec((None, tl, nk), lambda bi, i: (bi, i, (col0 + nq) // nk)),
                pl.BlockSpec((None, tl, nk), lambda bi, i: (bi, i, (col0 + nq + nk) // nk)),
                pl.BlockSpec((1, SWA_DIM), lambda bi, i: (0, 0)),
                pl.BlockSpec((1, SWA_DIM), lambda bi, i: (0, 0))]
    args = [p, p, p, qw.reshape(1, SWA_DIM), kw.reshape(1, SWA_DIM)]
    if rope is not None:
        in_specs += [pl.BlockSpec((3, tl // GRID_W, SWA_DIM), lambda bi, i: (0, i, 0)),
                     pl.BlockSpec((3, GRID_W, SWA_DIM), lambda bi, i: (0, 0, 0))]
        args += list(rope)
    return pl.pallas_call(
        functools.partial(_swa_prep_body, use_rope=rope is not None),
        grid=(b, l // tl),
        in_specs=in_specs,
        out_specs=[pl.BlockSpec((None, tl, nq), tok), pl.BlockSpec((None, tl, nk), tok),
                   pl.BlockSpec((None, tl, nk), tok)],
        out_shape=[jax.ShapeDtypeStruct((b, l, nq), BF16), jax.ShapeDtypeStruct((b, l, nk), BF16),
                   jax.ShapeDtypeStruct((b, l, nk), BF16)],
        compiler_params=_cparams(("parallel", "parallel")),
        name="swa_prep",
    )(*args)


def _nt_dot(a, b):
    return lax.dot_general(a, b, (((1,), (1,)), ((), ())), preferred_element_type=F32)


def _swa_attend(problems):
    def lane_tiles(x):
        return [x[:, c * LANES:(c + 1) * LANES] for c in range(x.shape[1] // LANES)]

    def masked_scores(q, keys, masks):
        out = []
        for kk, mask in zip(keys, masks):
            s = _nt_dot(q, kk)
            out.append(s if mask is None else jnp.where(mask, s, NEG_BIG))
        return out

    def row_max(scores, sink_col):
        m = sink_col + jnp.zeros((1, LANES), F32)
        for s in scores:
            for tile in lane_tiles(s):
                m = jnp.maximum(m, tile)
        return jnp.max(m, axis=-1, keepdims=True)

    def weighted(scores, vals, m, sink_col):
        part, acc = None, None
        for s, vv in zip(scores, vals):
            pr = jnp.exp(s - m)
            for tile in lane_tiles(pr):
                part = tile if part is None else part + tile
            o = jnp.dot(pr.astype(BF16), vv, preferred_element_type=F32)
            acc = o if acc is None else acc + o
        return acc, jnp.exp(sink_col - m) + jnp.sum(part, axis=-1, keepdims=True)

    scores = [masked_scores(q, keys, masks) for q, keys, _, masks, _ in problems]
    maxes = [row_max(s, p[4]) for s, p in zip(scores, problems)]
    outs = [weighted(s, p[2], m, p[4]) for s, p, m in zip(scores, problems, maxes)]
    return [acc / den for acc, den in outs]


def _swa_attn_body(sink_ref, q_ref, kp_ref, km_ref, kn_ref, vp_ref, vm_ref, vn_ref, kc_ref, vc_ref,
                   o_ref, *, nb):
    i = pl.program_id(1)
    n_blocks = pl.num_programs(1) * nb
    w, d = SWA_BLOCK, SWA_DIM
    grp = SWA_HEADS // SWA_KV_HEADS
    rows = lax.broadcasted_iota(jnp.int32, (grp * w, w), 0)
    qi = rows % w
    kj = lax.broadcasted_iota(jnp.int32, (grp * w, w), 1)
    sink_cols = []
    for h in range(SWA_KV_HEADS):
        sink_col = jnp.zeros((grp * w, 1), F32)
        for g in range(grp):
            sink_col = jnp.where((rows[:, :1] // w) == g, sink_ref[h * grp + g], sink_col)
        sink_cols.append(sink_col)
    problems, places = [], []
    for r in range(nb):
        blk = i * nb + r
        mask_prev = (kj >= qi) & (blk > 0)
        mask_next = (kj <= qi) & (blk < n_blocks - 1)
        rs = slice(r * w, (r + 1) * w)
        for h in range(SWA_KV_HEADS):
            hs = slice(h * d, (h + 1) * d)
            q = jnp.concatenate([q_ref[rs, (h * grp + g) * d:(h * grp + g + 1) * d] for g in range(grp)], axis=0)
            if r > 0:
                k_prev, v_prev = km_ref[(r - 1) * w:r * w, hs], vm_ref[(r - 1) * w:r * w, hs]
            else:
                k_prev, v_prev = kp_ref[:, hs], vp_ref[:, hs]
            if r < nb - 1:
                k_next, v_next = km_ref[(r + 1) * w:(r + 2) * w, hs], vm_ref[(r + 1) * w:(r + 2) * w, hs]
            else:
                k_next, v_next = kn_ref[:, hs], vn_ref[:, hs]
            problems.append((q, [k_prev, km_ref[rs, hs], k_next, kc_ref[:, hs]],
                             [v_prev, vm_ref[rs, hs], v_next, vc_ref[:, hs]],
                             [mask_prev, None, mask_next, None], sink_cols[h]))
            places.append((rs, h))
    for (rs, h), o in zip(places, _swa_attend(problems)):
        for g in range(grp):
            o_ref[rs, (h * grp + g) * d:(h * grp + g + 1) * d] = o[g * w:(g + 1) * w].astype(o_ref.dtype)


def swa_attention(q, k, v, kc, vc, sink, nb):
    b, l, nq = q.shape
    nk = k.shape[-1]
    n_ctx = kc.shape[1]
    w = SWA_BLOCK
    tq = nb * w
    last = l // w - 1
    main = lambda bi, i, s: (bi, i, 0)
    prev = lambda bi, i, s: (bi, jnp.maximum(i * nb - 1, 0), 0)
    nxt = lambda bi, i, s: (bi, jnp.minimum((i + 1) * nb, last), 0)
    cx = lambda bi, i, s: (bi, 0, 0)
    return pl.pallas_call(
        functools.partial(_swa_attn_body, nb=nb),
        grid_spec=pltpu.PrefetchScalarGridSpec(
            num_scalar_prefetch=1,
            grid=(b, l // tq),
            in_specs=[pl.BlockSpec((None, tq, nq), main),
                      pl.BlockSpec((None, w, nk), prev), pl.BlockSpec((None, tq, nk), main),
                      pl.BlockSpec((None, w, nk), nxt),
                      pl.BlockSpec((None, w, nk), prev), pl.BlockSpec((None, tq, nk), main),
                      pl.BlockSpec((None, w, nk), nxt),
                      pl.BlockSpec((None, n_ctx, nk), cx), pl.BlockSpec((None, n_ctx, nk), cx)],
            out_specs=pl.BlockSpec((None, tq, nq), main)),
        out_shape=jax.ShapeDtypeStruct((b, l, nq), BF16),
        compiler_params=_cparams(("parallel", "parallel")),
        name="swa_attention",
    )(sink, q, k, k, k, v, v, v, kc, vc)


def _ctx_attn_body(sink_ref, q_ref, kc_ref, vc_ref, o_ref):
    n_ctx, d = q_ref.shape[0], SWA_DIM
    grp = SWA_HEADS // SWA_KV_HEADS
    rows = lax.broadcasted_iota(jnp.int32, (grp * n_ctx, 1), 0)
    problems = []
    for h in range(SWA_KV_HEADS):
        hs = slice(h * d, (h + 1) * d)
        q = jnp.concatenate([q_ref[:, (h * grp + g) * d:(h * grp + g + 1) * d] for g in range(grp)], axis=0)
        sink_col = jnp.zeros((grp * n_ctx, 1), F32)
        for g in range(grp):
            sink_col = jnp.where((rows // n_ctx) == g, sink_ref[h * grp + g], sink_col)
        problems.append((q, [kc_ref[:, hs]], [vc_ref[:, hs]], [None], sink_col))
    for h, o in enumerate(_swa_attend(problems)):
        for g in range(grp):
            o_ref[:, (h * grp + g) * d:(h * grp + g + 1) * d] = o[g * n_ctx:(g + 1) * n_ctx].astype(o_ref.dtype)


def ctx_attention(qc, kc, vc, sink):
    b, n_ctx, nq = qc.shape
    nk = kc.shape[-1]
    cx = lambda bi, s: (bi, 0, 0)
    return pl.pallas_call(
        _ctx_attn_body,
        grid_spec=pltpu.PrefetchScalarGridSpec(
            num_scalar_prefetch=1, grid=(b,),
            in_specs=[pl.BlockSpec((None, n_ctx, nq), cx), pl.BlockSpec((None, n_ctx, nk), cx),
                      pl.BlockSpec((None, n_ctx, nk), cx)],
            out_specs=pl.BlockSpec((None, n_ctx, nq), cx)),
        out_shape=jax.ShapeDtypeStruct((b, n_ctx, nq), BF16),
        compiler_params=_cparams(("parallel",)),
        name="ctx_attention",
    )(sink, qc, kc, vc)


def axial_rope_tables(rows):
    m = SWA_DIM // 4
    inv = ROPE_THETA ** (-np.arange(m, dtype=np.float64) / m)

    def tables(count, lane0):
        ang = np.arange(count, dtype=np.float64)[:, None] * inv
        out = np.zeros((3, count, SWA_DIM))
        out[0, :, lane0:lane0 + m] = out[0, :, lane0 + m:lane0 + 2 * m] = np.cos(ang)
        out[1, :, lane0:lane0 + m] = -np.sin(ang)
        out[2, :, lane0 + m:lane0 + 2 * m] = np.sin(ang)
        return out.astype(np.float32)

    return tables(rows, 0), tables(GRID_W, SWA_DIM // 2)


def window_gqa_pallas(p_lat, p_ctx, col0, rope, q_norm_w, k_norm_w, sink, need_ctx):
    qw = q_norm_w.astype(F32) * SWA_DIM ** -0.5
    ql, kl, vl = swa_prep(p_lat, col0, qw, k_norm_w, rope, PREP_TILE)
    qc, kc, vc = swa_prep(p_ctx, col0, qw, k_norm_w, None, p_ctx.shape[1])
    o_lat = swa_attention(ql, kl, vl, kc, vc, sink, SWA_BLOCKS_PER_STEP)
    o_ctx = ctx_attention(qc, kc, vc, sink) if need_ctx else None
    return o_lat, o_ctx


def _tn_dot(a, b):
    return lax.dot_general(a, b, (((0,), (0,)), ((), ())), preferred_element_type=F32)


def _log_sigmoid(x):
    return jnp.minimum(x, 0.0) - jnp.log(1.0 + jnp.exp(-jnp.abs(x)))


RET_STREAMS = 2 * RET_HEADS


RET_SUBS = 4


def _ret_scan_body(logit_ref, qf_ref, kf_ref, vf_ref, rotf_ref, qb_ref, kb_ref, vb_ref, rotb_ref, rotc_ref,
                   s0_ref, of_ref, ob_ref, sf_ref, s_scr):
    n = pl.program_id(1)
    c, dh = RET_CHUNK, RET_DIM

    @pl.when(n == 0)
    def _():
        s_scr[...] = s0_ref[...]

    ii = lax.broadcasted_iota(jnp.int32, (c, c), 0)
    jj = lax.broadcasted_iota(jnp.int32, (c, c), 1)
    i1 = lax.broadcasted_iota(jnp.int32, (c, 1), 0)
    rel = {True: (ii - jj).astype(F32), False: (jj - ii).astype(F32)}
    pos = {True: i1.astype(F32), False: (c - 1 - i1).astype(F32)}
    streams = [(d == 0, h) for d in range(2) for h in range(RET_HEADS)]
    srcs = {True: (qf_ref, kf_ref, vf_ref, rotf_ref), False: (qb_ref, kb_ref, vb_ref, rotb_ref)}
    subs = rotf_ref.shape[0]
    items = [(r, f, h, j, (j if f else subs - 1 - j)) for j in range(subs) for r, (f, h) in enumerate(streams)]
    rows = lambda blk: slice(blk * c, (blk + 1) * c)
    hs = lambda h: slice(h * dh, (h + 1) * dh)
    rng = range(len(items))

    cos_i, sin_i = rotc_ref[0], rotc_ref[1]
    tables = {}
    for f in (True, False):
        for bk in range(subs):
            a = srcs[f][3][bk]
            tables[f, bk] = (a[0:1] * cos_i - a[1:2] * sin_i, a[2:3] * cos_i + a[3:4] * sin_i)

    def rot(x, f, bk):
        cos_t, sin_t = tables[f, bk]
        return x * cos_t + pltpu.roll(x, dh // 2, 1) * sin_t

    lg = [_log_sigmoid(jnp.full((1, 1), logit_ref[r], F32)) for r in range(RET_STREAMS)]
    q = [rot(srcs[f][0][rows(bk), hs(h)], f, bk) for _, f, h, _, bk in items]
    k = [rot(srcs[f][1][rows(bk), hs(h)], f, bk) * dh ** -0.5 for _, f, h, _, bk in items]
    v = [srcs[f][2][rows(bk), hs(h)].astype(BF16) for _, f, h, _, bk in items]
    dmat = [jnp.where(rel[f] >= 0, jnp.exp(lg[r] * jnp.maximum(rel[f], 0.0)), 0.0)
            for r, (f, _) in enumerate(streams)]
    scores = [(_nt_dot(q[i].astype(BF16), k[i].astype(BF16)) * dmat[items[i][0]]).astype(BF16) for i in rng]
    o_in = [jnp.dot(scores[i], v[i], preferred_element_type=F32) for i in rng]
    q_dec = [(q[i] * jnp.exp(lg[items[i][0]] * (pos[items[i][1]] + 1.0))).astype(BF16) for i in rng]
    k_dec = [(k[i] * jnp.exp(lg[items[i][0]] * (c - 1.0 - pos[items[i][1]]))).astype(BF16) for i in rng]
    kv = [_tn_dot(k_dec[i], v[i]) for i in rng]
    chunk_dec = [jnp.exp(lg[r] * c) for r in range(RET_STREAMS)]

    s = [s_scr[r] for r in range(RET_STREAMS)]
    for j in range(subs):
        idx = [i for i in rng if items[i][3] == j]
        o = [o_in[i] + jnp.dot(q_dec[i], s[items[i][0]].astype(BF16), preferred_element_type=F32) for i in idx]
        s = [s[items[i][0]] * chunk_dec[items[i][0]] + kv[i] for i in idx]
        for i, oi in zip(idx, o):
            _, f, h, _, bk = items[i]
            (of_ref if f else ob_ref)[rows(bk), hs(h)] = oi.astype(of_ref.dtype)
    for r in range(RET_STREAMS):
        s_scr[r] = s[r]

    @pl.when(n == pl.num_programs(1) - 1)
    def _():
        sf_ref[...] = s_scr[...]


def retention_scan(p, col0, decay_logit, rot_chunk, rot_in, s0):
    b, l, _ = p.shape
    subs = min(RET_SUBS, l // RET_CHUNK)
    c, dh, w = RET_CHUNK * subs, RET_DIM, RET_WIDTH
    n = l // c
    cb = col0 // w
    fcol = lambda off: (lambda bi, ni, s: (bi, ni, cb + off))
    bcol = lambda off: (lambda bi, ni, s: (bi, n - 1 - ni, cb + off))
    rot = lambda m: pl.BlockSpec((subs, 4, dh), m)
    st = lambda bi, ni, s: (bi, 0, 0, 0)
    tok = lambda col: [pl.BlockSpec((None, c, w), col(j)) for j in range(3)]
    return pl.pallas_call(
        _ret_scan_body,
        grid_spec=pltpu.PrefetchScalarGridSpec(
            num_scalar_prefetch=1,
            grid=(b, n),
            in_specs=tok(fcol) + [rot(lambda bi, ni, s: (ni, 0, 0))]
            + tok(bcol) + [rot(lambda bi, ni, s: (n - 1 - ni, 0, 0))]
            + [pl.BlockSpec((2, RET_CHUNK, dh), lambda bi, ni, s: (0, 0, 0)),
               pl.BlockSpec((None, RET_STREAMS, dh, dh), st)],
            out_specs=[pl.BlockSpec((None, c, w), lambda bi, ni, s: (bi, ni, 0)),
                       pl.BlockSpec((None, c, w), lambda bi, ni, s: (bi, n - 1 - ni, 0)),
                       pl.BlockSpec((None, RET_STREAMS, dh, dh), st)],
            scratch_shapes=[pltpu.VMEM((RET_STREAMS, dh, dh), F32)]),
        out_shape=[jax.ShapeDtypeStruct((b, l, w), BF16), jax.ShapeDtypeStruct((b, l, w), BF16),
                   jax.ShapeDtypeStruct((b, RET_STREAMS, dh, dh), F32)],
        compiler_params=_cparams(("parallel", "arbitrary")),
        name="retention_scan",
    )(decay_logit.reshape(-1), p, p, p, rot_chunk, p, p, p, rot_chunk, rot_in, s0)


def _ret_final_body(of_ref, ob_ref, g_ref, w_ref, y_ref):
    for h in range(RET_HEADS):
        sl = slice(h * RET_DIM, (h + 1) * RET_DIM)
        o = of_ref[:, sl].astype(F32) + ob_ref[:, sl].astype(F32)
        mu = jnp.mean(o, axis=-1, keepdims=True)
        var = jnp.mean(jnp.square(o - mu), axis=-1, keepdims=True)
        y = (o - mu) * lax.rsqrt(var + EPS) * w_ref[:, sl]
        g = g_ref[:, sl]
        y_ref[:, sl] = (y * (g * jax.nn.sigmoid(g))).astype(y_ref.dtype)


def retention_finalize(o_f, o_b, p, gate_col, gn_w, tl):
    b, l, w = o_f.shape
    tok = lambda bi, i: (bi, i, 0)
    return pl.pallas_call(
        _ret_final_body,
        grid=(b, l // tl),
        in_specs=[pl.BlockSpec((None, tl, w), tok),
                  pl.BlockSpec((None, tl, w), tok),
                  pl.BlockSpec((None, tl, w), lambda bi, i: (bi, i, gate_col // w)),
                  pl.BlockSpec((1, w), lambda bi, i: (0, 0))],
        out_specs=pl.BlockSpec((None, tl, w), tok),
        out_shape=jax.ShapeDtypeStruct((b, l, w), BF16),
        compiler_params=_cparams(("parallel", "parallel")),
        name="retention_finalize",
    )(o_f, o_b, p, gn_w.reshape(1, w))


def retention_rope_tables(l):
    inv = ROPE_THETA ** (-np.linspace(0.0, 1.0, RET_DIM // 2))
    inv2 = np.concatenate([inv, inv])
    sign = np.concatenate([-np.ones(RET_DIM // 2), np.ones(RET_DIM // 2)])
    a = (RET_CHUNK * np.arange(l // RET_CHUNK, dtype=np.float64))[:, None] * inv2
    b = np.arange(RET_CHUNK, dtype=np.float64)[:, None] * inv2
    chunk = np.stack([np.cos(a), np.sin(a), sign * np.sin(a), sign * np.cos(a)], axis=1)
    return chunk.astype(np.float32), np.stack([np.cos(b), np.sin(b)]).astype(np.float32)


def retention_pallas(p_lat, p_ctx, decay_logit, gn_w, need_ctx, col0=0):
    b, l, _ = p_lat.shape
    n_ctx = p_ctx.shape[1]
    gate_col = col0 + 3 * RET_WIDTH
    s0 = jnp.zeros((b, RET_STREAMS, RET_DIM, RET_DIM), F32)
    oc_f, oc_b, s_ctx = retention_scan(p_ctx, col0, decay_logit, *retention_rope_tables(n_ctx), s0)
    ol_f, ol_b, _ = retention_scan(p_lat, col0, decay_logit, *retention_rope_tables(l), s_ctx)
    out_lat = retention_finalize(ol_f, ol_b, p_lat, gate_col, gn_w, PREP_TILE)
    out_ctx = retention_finalize(oc_f, oc_b, p_ctx, gate_col, gn_w, n_ctx) if need_ctx else None
    return out_lat, out_ctx


DN_BLOCK = 128
DN_STREAMS = 2 * DN_HEADS
DN_SUBS = 4


def _shift_rows(x, prev_row, next_row):
    n = x.shape[0]
    r = lax.broadcasted_iota(jnp.int32, (SUBLANES, 1), 0)
    down, up = pltpu.roll(x, 1, 0), pltpu.roll(x, n - 1, 0)
    x_prev = jnp.concatenate([jnp.where(r == 0, prev_row, down[:SUBLANES]), down[SUBLANES:]], axis=0)
    x_next = jnp.concatenate([up[:n - SUBLANES], jnp.where(r == SUBLANES - 1, next_row, up[n - SUBLANES:])], axis=0)
    return x_prev, x_next


def _conv3(x_ref, xp_ref, xn_ref, w_ref):
    i = pl.program_id(1)
    prev_row = jnp.where(i > 0, xp_ref[SUBLANES - 1:SUBLANES, :], 0.0)
    next_row = jnp.where(i < pl.num_programs(1) - 1, xn_ref[0:1, :], 0.0)
    x = x_ref[...]
    x_prev, x_next = _shift_rows(x, prev_row, next_row)
    return x_prev * w_ref[0:1, :] + x * w_ref[1:2, :] + x_next * w_ref[2:3, :]


def _dn_prep_body(x_ref, xp_ref, xn_ref, ab_ref, cw_ref, alog_ref, dtb_ref,
                  q_ref, k_ref, v_ref, gb_ref, gam_ref, gamt_ref):
    y = _conv3(x_ref, xp_ref, xn_ref, cw_ref)
    y = y * jax.nn.sigmoid(y)
    for h in range(DN_HEADS):
        for part, (ref, mul) in enumerate(((q_ref, DN_DIM ** -0.5), (k_ref, 1.0))):
            sl = slice(part * DN_WIDTH + h * DN_DIM, part * DN_WIDTH + (h + 1) * DN_DIM)
            t = y[:, sl]
            t = t * lax.rsqrt(jnp.sum(t * t, axis=-1, keepdims=True) + EPS)
            ref[:, h * DN_DIM:(h + 1) * DN_DIM] = t * mul if mul != 1.0 else t
    v_ref[...] = y[:, 2 * DN_WIDTH:]

    ab = ab_ref[...]
    z = ab + dtb_ref[...]
    softplus = jnp.maximum(z, 0.0) + jnp.log(1.0 + jnp.exp(-jnp.abs(z)))
    g = -jnp.exp(alog_ref[...]) * softplus
    lane = lax.broadcasted_iota(jnp.int32, ab.shape, 1)
    gb_ref[...] = jnp.where(lane < DN_STREAMS, g, jax.nn.sigmoid(ab))

    c = DN_BLOCK
    ii = lax.broadcasted_iota(jnp.int32, (c, c), 0)
    jj = lax.broadcasted_iota(jnp.int32, (c, c), 1)
    tri_f = (ii >= jj).astype(F32)
    lane_c = lax.broadcasted_iota(jnp.int32, (c, ab.shape[1]), 1)
    for n in range(x_ref.shape[0] // c):
        gc = g[n * c:(n + 1) * c, :]
        cf = jnp.dot(tri_f, gc, preferred_element_type=F32, precision=lax.Precision.HIGHEST)
        cb = cf[c - 1:c, :] - cf + gc
        gam = jnp.where(lane_c < DN_HEADS, cf, cb)
        gam_ref[n * c:(n + 1) * c, :] = gam
        gamt_ref[n] = gam.T[:DN_STREAMS, :]


def dn_prep(p, ab_col, conv_w, a_log, dt_bias, tl):
    b, l, _ = p.shape
    c3 = 3 * DN_WIDTH
    nblk8 = l // SUBLANES
    lanes = LANES
    pad8 = lambda a: jnp.pad(a.reshape(1, DN_STREAMS).astype(F32), ((0, 0), (0, lanes - DN_STREAMS)))
    tok = lambda bi, i: (bi, i, 0)
    return pl.pallas_call(
        _dn_prep_body,
        grid=(b, l // tl),
        in_specs=[pl.BlockSpec((None, tl, c3), tok),
                  pl.BlockSpec((None, SUBLANES, c3),
                               lambda bi, i: (bi, jnp.maximum(i * (tl // SUBLANES) - 1, 0), 0)),
                  pl.BlockSpec((None, SUBLANES, c3),
                               lambda bi, i: (bi, jnp.minimum((i + 1) * (tl // SUBLANES), nblk8 - 1), 0)),
                  pl.BlockSpec((None, tl, lanes), lambda bi, i: (bi, i, ab_col // lanes)),
                  pl.BlockSpec((3, c3), lambda bi, i: (0, 0)),
                  pl.BlockSpec((1, lanes), lambda bi, i: (0, 0)),
                  pl.BlockSpec((1, lanes), lambda bi, i: (0, 0))],
        out_specs=[pl.BlockSpec((None, tl, DN_WIDTH), tok)] * 3
        + [pl.BlockSpec((None, tl, lanes), tok)] * 2
        + [pl.BlockSpec((None, tl // DN_BLOCK, DN_STREAMS, DN_BLOCK), lambda bi, i: (bi, i, 0, 0))],
        out_shape=[jax.ShapeDtypeStruct((b, l, DN_WIDTH), F32)] * 3
        + [jax.ShapeDtypeStruct((b, l, lanes), F32)] * 2
        + [jax.ShapeDtypeStruct((b, l // DN_BLOCK, DN_STREAMS, DN_BLOCK), F32)],
        compiler_params=_cparams(("parallel", "parallel")),
        name="dn_prep",
    )(p, p, p, p, conv_w, pad8(a_log), pad8(dt_bias))


def _dot16(a, b):
    return jnp.dot(a.astype(BF16), b.astype(BF16), preferred_element_type=F32)


def _unit_triangular_inverses(ms, ii, jj):
    c = ms[0].shape[0]
    eye = (ii == jj).astype(F32)
    pair = (ii // 2) == (jj // 2)
    ts = [eye - jnp.where(pair, m, 0.0) for m in ms]
    s = 2
    while s < c:
        sub = ((ii // (2 * s)) == (jj // (2 * s))) & ((ii // s) != (jj // s))
        t16 = [t.astype(BF16) for t in ts]
        xs = [_dot16(jnp.where(sub, m, 0.0), t) for m, t in zip(ms, t16)]
        ts = [t - _dot16(tb, x) for t, tb, x in zip(ts, t16, xs)]
        s *= 2
    return ts


def _dn_scan_body(qf_ref, kf_ref, vf_ref, gbf_ref, gamf_ref, gamtf_ref,
                  qb_ref, kb_ref, vb_ref, gbb_ref, gamb_ref, gamtb_ref, s0_ref,
                  of_ref, ob_ref, sf_ref, s_scr):
    n = pl.program_id(1)

    @pl.when(n == 0)
    def _():
        s_scr[...] = s0_ref[...]

    c, dh = DN_BLOCK, DN_DIM
    ii = lax.broadcasted_iota(jnp.int32, (c, c), 0)
    jj = lax.broadcasted_iota(jnp.int32, (c, c), 1)
    incl = {True: ii >= jj, False: ii <= jj}
    strict = {True: ii > jj, False: ii < jj}
    srcs = {True: (qf_ref, kf_ref, vf_ref, gbf_ref, gamf_ref, gamtf_ref),
            False: (qb_ref, kb_ref, vb_ref, gbb_ref, gamb_ref, gamtb_ref)}
    streams = [(d == 0, h) for d in range(2) for h in range(DN_HEADS)]
    hs = lambda h: slice(h * dh, (h + 1) * dh)
    subs = gamtf_ref.shape[0]
    items = [(r, f, h, j, (j if f else subs - 1 - j)) for j in range(subs) for r, (f, h) in enumerate(streams)]
    rows = lambda blk: slice(blk * c, (blk + 1) * c)
    rng = range(len(items))

    q = [srcs[f][0][rows(bk), hs(h)] for _, f, h, _, bk in items]
    k = [srcs[f][1][rows(bk), hs(h)] for _, f, h, _, bk in items]
    v = [srcs[f][2][rows(bk), hs(h)] for _, f, h, _, bk in items]
    beta = [srcs[f][3][rows(bk), DN_STREAMS + r:DN_STREAMS + r + 1] for r, f, _, _, bk in items]
    gcol = [srcs[f][4][rows(bk), r:r + 1] for r, f, _, _, bk in items]
    grow = [srcs[f][5][bk, r:r + 1, :] for r, f, _, _, bk in items]
    g_last = [gcol[i][c - 1:c, :] if items[i][1] else gcol[i][0:1, :] for i in rng]

    e = [jnp.exp(jnp.where(incl[items[i][1]], gcol[i] - grow[i], 0.0)) for i in rng]
    kb = [k[i] * beta[i] for i in rng]
    k16 = [k[i].astype(BF16) for i in rng]
    kq = [_nt_dot(jnp.concatenate([kb[i].astype(BF16), q[i].astype(BF16)], axis=0), k16[i]) for i in rng]
    m = [kq[i][:c] * jnp.where(strict[items[i][1]], e[i], 0.0) for i in rng]
    attn = [(kq[i][c:] * jnp.where(incl[items[i][1]], e[i], 0.0)).astype(BF16) for i in rng]
    t = _unit_triangular_inverses(m, ii, jj)
    eg = [jnp.exp(gcol[i]) for i in rng]
    sol = [_dot16(t[i], jnp.concatenate([v[i] * beta[i], kb[i] * eg[i]], axis=-1)) for i in rng]
    u = [sol[i][:, :dh] for i in rng]
    w16 = [sol[i][:, dh:].astype(BF16) for i in rng]
    q_dec = [(q[i] * eg[i]).astype(BF16) for i in rng]
    k_dec = [(k[i] * jnp.exp(g_last[i] - gcol[i])).astype(BF16) for i in rng]
    blk_dec = [jnp.exp(g_last[i]) for i in rng]

    s = [s_scr[r] for r in range(DN_STREAMS)]
    for j in range(subs):
        idx = [i for i in rng if items[i][3] == j]
        s16 = [s[r].astype(BF16) for r in range(DN_STREAMS)]
        ws = [jnp.dot(jnp.concatenate([w16[i], q_dec[i]], axis=0), s16[items[i][0]], preferred_element_type=F32)
              for i in idx]
        v_new = [(u[i] - x[:c]).astype(BF16) for i, x in zip(idx, ws)]
        o = [x[c:] + jnp.dot(attn[i], vn, preferred_element_type=F32) for i, x, vn in zip(idx, ws, v_new)]
        s = [s[items[i][0]] * blk_dec[i] + _tn_dot(k_dec[i], vn) for i, vn in zip(idx, v_new)]
        for i, oi in zip(idx, o):
            _, f, h, _, bk = items[i]
            (of_ref if f else ob_ref)[rows(bk), hs(h)] = oi.astype(of_ref.dtype)
    for r in range(DN_STREAMS):
        s_scr[r] = s[r]

    @pl.when(n == pl.num_programs(1) - 1)
    def _():
        sf_ref[...] = s_scr[...]


def dn_scan(q, k, v, gb, gam, gamt, s0):
    b, l, w = q.shape
    subs = min(DN_SUBS, l // DN_BLOCK)
    c = DN_BLOCK * subs
    n = l // c
    lanes = gb.shape[-1]
    f3 = lambda bi, ni: (bi, ni, 0)
    b3 = lambda bi, ni: (bi, n - 1 - ni, 0)
    f4 = lambda bi, ni: (bi, ni, 0, 0)
    b4 = lambda bi, ni: (bi, n - 1 - ni, 0, 0)
    st = lambda bi, ni: (bi, 0, 0, 0)

    def specs(m3, m4):
        return [pl.BlockSpec((None, c, w), m3)] * 3 + [pl.BlockSpec((None, c, lanes), m3)] * 2 + [
            pl.BlockSpec((None, subs, DN_STREAMS, DN_BLOCK), m4)]

    return pl.pallas_call(
        _dn_scan_body,
        grid=(b, n),
        in_specs=specs(f3, f4) + specs(b3, b4) + [pl.BlockSpec((None, DN_STREAMS, DN_DIM, DN_DIM), st)],
        out_specs=[pl.BlockSpec((None, c, w), f3), pl.BlockSpec((None, c, w), b3),
                   pl.BlockSpec((None, DN_STREAMS, DN_DIM, DN_DIM), st)],
        out_shape=[jax.ShapeDtypeStruct((b, l, w), BF16), jax.ShapeDtypeStruct((b, l, w), BF16),
                   jax.ShapeDtypeStruct((b, DN_STREAMS, DN_DIM, DN_DIM), F32)],
        scratch_shapes=[pltpu.VMEM((DN_STREAMS, DN_DIM, DN_DIM), F32)],
        compiler_params=_cparams(("parallel", "arbitrary")),
        name="dn_scan",
    )(q, k, v, gb, gam, gamt, q, k, v, gb, gam, gamt, s0)


def _dn_final_body(of_ref, ob_ref, g_ref, w_ref, y_ref):
    for h in range(DN_HEADS):
        sl = slice(h * DN_DIM, (h + 1) * DN_DIM)
        o = of_ref[:, sl].astype(F32) + ob_ref[:, sl].astype(F32)
        y = _head_rms(o, w_ref[...])
        g = g_ref[:, sl]
        y_ref[:, sl] = (y * (g * jax.nn.sigmoid(g))).astype(y_ref.dtype)


def dn_finalize(o_f, o_b, p, gate_col, norm_w, tl):
    b, l, w = o_f.shape
    tok = lambda bi, i: (bi, i, 0)
    return pl.pallas_call(
        _dn_final_body,
        grid=(b, l // tl),
        in_specs=[pl.BlockSpec((None, tl, w), tok), pl.BlockSpec((None, tl, w), tok),
                  pl.BlockSpec((None, tl, w), lambda bi, i: (bi, i, gate_col // w)),
                  pl.BlockSpec((1, DN_DIM), lambda bi, i: (0, 0))],
        out_specs=pl.BlockSpec((None, tl, w), tok),
        out_shape=jax.ShapeDtypeStruct((b, l, w), BF16),
        compiler_params=_cparams(("parallel", "parallel")),
        name="dn_finalize",
    )(o_f, o_b, p, norm_w.reshape(1, DN_DIM))


def deltanet_pallas(p_lat, p_ctx, conv_w, a_log, dt_bias, norm_w, need_ctx, ab_col=3072, gate_col=1536):
    b = p_lat.shape[0]
    n_ctx = p_ctx.shape[1]
    s0 = jnp.zeros((b, DN_STREAMS, DN_DIM, DN_DIM), F32)
    fc = dn_prep(p_ctx, ab_col, conv_w, a_log, dt_bias, n_ctx)
    oc_f, oc_b, s_ctx = dn_scan(*fc, s0)
    fl = dn_prep(p_lat, ab_col, conv_w, a_log, dt_bias, PREP_TILE)
    ol_f, ol_b, _ = dn_scan(*fl, s_ctx)
    out_lat = dn_finalize(ol_f, ol_b, p_lat, gate_col, norm_w, PREP_TILE)
    out_ctx = dn_finalize(oc_f, oc_b, p_ctx, gate_col, norm_w, n_ctx) if need_ctx else None
    return out_lat, out_ctx


def _hy_factors(n):
    n2 = 256 if n >= 32768 else 128
    return n // n2, n2


def _bf16_const(m):
    return m.astype(ml_dtypes.bfloat16)


def _hy_constants(l):
    n = 2 * l
    n1, n2 = _hy_factors(n)
    a1 = 2.0 * np.pi * np.outer(np.arange(n1), np.arange(n1)) / n1
    c1, s1 = np.cos(a1), np.sin(a1)
    h = n1 // 2
    rows_fwd = np.block([[c1[:, :h], s1[:, :h]], [-s1[:, :h], c1[:, :h]]])
    rows_taps = np.concatenate([c1, -s1], axis=0)
    rows_inv = np.block([[c1[:h], -s1[:h]], [s1[:h], c1[:h]]]) / n
    a2 = 2.0 * np.pi * np.outer(np.arange(n2), np.arange(n2)) / n2
    c2, s2 = np.cos(a2), np.sin(a2)
    slab_fwd = np.block([[c2, s2], [-s2, c2]])
    slab_inv = np.block([[c2, -s2], [s2, c2]])
    th = 2.0 * np.pi * np.outer(np.arange(n1), np.arange(n2)) / n
    lanes = lambda t: t[:, :, None].astype(np.float32)
    return dict(n1=n1, n2=n2, rows_fwd=_bf16_const(rows_fwd), rows_taps=_bf16_const(rows_taps),
                rows_inv=_bf16_const(rows_inv), slab_fwd=_bf16_const(slab_fwd), slab_inv=_bf16_const(slab_inv),
                tw_cos=lanes(np.cos(th)), tw_sin=lanes(np.sin(th)))


def _hy_prep_body(x_ref, xp_ref, xn_ref, w_ref, v_ref, x1_ref, x2_ref):
    y = _conv3(x_ref, xp_ref, xn_ref, w_ref)
    v_ref[...] = y[:, :HY_CH]
    x1_ref[...] = y[:, HY_CH:2 * HY_CH]
    x2_ref[...] = y[:, 2 * HY_CH:]


def hy_prep(p, col0, conv_w, tl):
    b, l, _ = p.shape
    c3 = 3 * HY_CH
    nblk8 = l // SUBLANES
    cb = col0 // c3
    tok = lambda bi, i: (bi, i, 0)
    return pl.pallas_call(
        _hy_prep_body,
        grid=(b, l // tl),
        in_specs=[pl.BlockSpec((None, tl, c3), lambda bi, i: (bi, i, cb)),
                  pl.BlockSpec((None, SUBLANES, c3),
                               lambda bi, i: (bi, jnp.maximum(i * (tl // SUBLANES) - 1, 0), cb)),
                  pl.BlockSpec((None, SUBLANES, c3),
                               lambda bi, i: (bi, jnp.minimum((i + 1) * (tl // SUBLANES), nblk8 - 1), cb)),
                  pl.BlockSpec((3, c3), lambda bi, i: (0, 0))],
        out_specs=[pl.BlockSpec((None, tl, HY_CH), tok)] * 3,
        out_shape=[jax.ShapeDtypeStruct((b, l, HY_CH), F32)] * 3,
        compiler_params=_cparams(("parallel", "parallel")),
        name="hy_prep",
    )(p, p, p, conv_w)


def _split_bf16(a):
    hi = a.astype(BF16)
    return hi, (a - hi.astype(F32)).astype(BF16)


HY_FILTER_GROUP = 8


def _hy_filter_body(t_ref, zt_ref, w1t_ref, b1_ref, w2t_ref, b2_ref, w3_ref, rates_ref, m_ref, *out_refs):
    a_refs, sum_ref = out_refs[:-1], out_refs[-1]
    i = pl.program_id(0)
    hp = lax.Precision.HIGHEST
    h = jnp.sin(HY_SIN_FREQ * (jnp.dot(w1t_ref[...], zt_ref[...], preferred_element_type=F32, precision=hp)
                               + b1_ref[...]))
    h = jnp.sin(HY_SIN_FREQ * (jnp.dot(w2t_ref[...], h, preferred_element_type=F32, precision=hp) + b2_ref[...]))
    half = h.shape[1] // 2
    taps = []
    for d in range(2):
        h_hi, h_lo = _split_bf16(h[:, d * half:(d + 1) * half])
        w_hi, w_lo = _split_bf16(w3_ref[d])
        td = _tn_dot(h_hi, w_hi) + (_tn_dot(h_hi, w_lo) + _tn_dot(h_lo, w_hi))
        taps.append(td * jnp.exp(-t_ref[d * half:(d + 1) * half, :] * rates_ref[...]))

    @pl.when(i == 0)
    def _():
        sum_ref[...] = jnp.zeros_like(sum_ref)

    sum_ref[...] += (jnp.sum(jnp.abs(taps[0]), axis=0, keepdims=True)
                     + jnp.sum(jnp.abs(taps[1]), axis=0, keepdims=True))
    row = lax.broadcasted_iota(jnp.int32, (half, 1), 0)
    taps[1] = jnp.where((row == 0) & (i == 0), 0.0, taps[1])
    nh = half // HY_FILTER_GROUP
    for j in range(HY_FILTER_GROUP):
        for o in range(HY_ORDER):
            cs = slice(o * HY_CH, (o + 1) * HY_CH)
            x = jnp.concatenate([taps[0][j * nh:(j + 1) * nh, cs], taps[1][j * nh:(j + 1) * nh, cs]], axis=0)
            y = _dot16(m_ref[...], x)
            a_refs[o][0, :, j, :] = y[:2 * nh]
            a_refs[o][1, :, j, :] = y[2 * nh:]


def hy_filter_rows(l, consts, w1, b1, w2, b2, w3):
    n = 2 * l
    n1, n2, g = consts["n1"], consts["n2"], HY_FILTER_GROUP
    tr = g * n1
    grp, d, j, m = np.meshgrid(np.arange(n2 // g), np.arange(2), np.arange(g), np.arange(n1 // 2), indexing="ij")
    r = (n2 * (m + (n1 // 2) * d) + g * grp + j).reshape(-1)
    pos = np.where(r < l, r, n - r).astype(np.float64)
    pos[r == l] = 0.0
    t = pos / max(l - 1, 1)
    bands = np.linspace(1e-4, HY_BANDS - 1, HY_BANDS)
    phase = (2.0 * math.pi / l) * pos[:, None] * bands[None, :]
    kf = 32
    z = np.concatenate([t[:, None], np.cos(phase), -np.sin(phase)], axis=-1)
    zt = np.pad(z, ((0, 0), (0, kf - z.shape[1]))).T.astype(np.float32)
    t_col = t[:, None].astype(np.float32)
    w1t = jnp.pad(w1.astype(F32), ((0, kf - w1.shape[0]), (0, 0))).T
    hid = w1.shape[1]
    w3d = w3.astype(F32).reshape(hid, HY_ORDER, 2, HY_CH).transpose(2, 0, 1, 3).reshape(2, hid, HY_ORDER * HY_CH)
    rates = jnp.abs(jnp.linspace(math.log(HY_TARGET) / HY_DECAY_LONG, math.log(HY_TARGET) / HY_DECAY_SHORT,
                                 HY_CH, dtype=F32))
    rates = jnp.tile(rates, HY_ORDER).reshape(1, HY_ORDER * HY_CH)
    fixed = lambda i: (0, 0)
    mat = consts["rows_taps"]
    return pl.pallas_call(
        _hy_filter_body,
        grid=(n // tr,),
        in_specs=[pl.BlockSpec((tr, 1), lambda i: (i, 0)), pl.BlockSpec((kf, tr), lambda i: (0, i)),
                  pl.BlockSpec((hid, kf), fixed), pl.BlockSpec((hid, 1), fixed),
                  pl.BlockSpec((hid, hid), fixed), pl.BlockSpec((hid, 1), fixed),
                  pl.BlockSpec((2, hid, HY_ORDER * HY_CH), lambda i: (0, 0, 0)),
                  pl.BlockSpec((1, HY_ORDER * HY_CH), fixed), pl.BlockSpec(mat.shape, fixed)],
        out_specs=[pl.BlockSpec((2, n1, g, HY_CH), lambda i: (0, 0, i, 0))] * HY_ORDER
        + [pl.BlockSpec((1, HY_ORDER * HY_CH), fixed)],
        out_shape=[jax.ShapeDtypeStruct((2, n1, n2, HY_CH), F32)] * HY_ORDER
        + [jax.ShapeDtypeStruct((1, HY_ORDER * HY_CH), F32)],
        compiler_params=_cparams(("arbitrary",)),
        name="hy_filter_rows",
    )(t_col, zt, w1t, b1.astype(F32).reshape(hid, 1), w2.astype(F32).T, b2.astype(F32).reshape(hid, 1), w3d, rates,
      mat)


HY_ROWS_GROUP = {"fwd": 16, "inv": 8}


def _hy_rows_body(*refs, mode):
    if mode == "fwd":
        m_ref, x_ref, o_ref = refs
    else:
        m_ref, x_ref, xn_ref, z_ref, skip_ref, o_ref = refs
    for j in range(HY_ROWS_GROUP[mode]):
        x = jnp.concatenate([x_ref[0, :, j, :], x_ref[1, :, j, :]], axis=0)
        y = _dot16(m_ref[...], x)
        half = y.shape[0] // 2
        for bi in range(2):
            yb = y[bi * half:(bi + 1) * half]
            if mode == "inv":
                yb = xn_ref[bi, :, j, :] * (yb + z_ref[bi, :, j, :] * skip_ref[...])
            o_ref[bi, :, j, :] = yb


def hy_rows_pass(mode, mat, x, extra=()):
    n2, c = x.shape[-2:]
    g = HY_ROWS_GROUP[mode]
    blk = lambda rows: pl.BlockSpec((2, rows, g, c), lambda j: (0, 0, j, 0))
    mspec = _resident(mat.shape)
    vec = pl.BlockSpec((1, c), lambda j: (0, 0))
    if mode == "fwd":
        n1 = 2 * x.shape[1]
        in_specs = [mspec, blk(n1 // 2)]
        out_rows = n1
    else:
        n1 = x.shape[1]
        in_specs = [mspec, blk(n1), blk(n1 // 2), blk(n1 // 2), vec]
        out_rows = n1 // 2
    return pl.pallas_call(
        functools.partial(_hy_rows_body, mode=mode),
        grid=(n2 // g,),
        in_specs=in_specs,
        out_specs=blk(out_rows),
        out_shape=jax.ShapeDtypeStruct((2, out_rows, n2, c), F32),
        compiler_params=_cparams(("parallel",)),
        name="hy_rows_" + mode,
    )(mat, x, *extra)


def _hy_slab_body(*refs, with_filter):
    if with_filter:
        f_ref, i_ref, tc_ref, ts_ref, a_ref, h_ref, o_ref = refs
    else:
        f_ref, tc_ref, ts_ref, a_ref, sum_ref, o_ref = refs
        scale = 1.0 / (sum_ref[...] + EPS)
    n2, c = a_ref.shape[2:]
    for s in range(a_ref.shape[1]):
        tc = jnp.broadcast_to(tc_ref[s], (n2, c))
        ts = jnp.broadcast_to(ts_ref[s], (n2, c))
        ar, ai = a_ref[0, s], a_ref[1, s]
        x = _dot16(f_ref[...], jnp.concatenate([ar * tc + ai * ts, ai * tc - ar * ts], axis=0))
        xr, xi = x[:n2], x[n2:]
        if not with_filter:
            o_ref[0, s] = (xr * scale).astype(o_ref.dtype)
            o_ref[1, s] = (xi * scale).astype(o_ref.dtype)
            continue
        hr, hi = h_ref[0, s].astype(F32), h_ref[1, s].astype(F32)
        y = _dot16(i_ref[...], jnp.concatenate([xr * hr - xi * hi, xr * hi + xi * hr], axis=0))
        yr, yi = y[:n2], y[n2:]
        o_ref[0, s] = yr * tc - yi * ts
        o_ref[1, s] = yi * tc + yr * ts


HY_SLABS_PER_STEP = 4


def hy_slab_pass(consts, a, h=None, l1_sum=None):
    _, n1, n2, c = a.shape
    kb = HY_SLABS_PER_STEP
    fwd = consts["slab_fwd"]
    mspec = _resident(fwd.shape)
    tw = pl.BlockSpec((kb, n2, 1), lambda k: (k, 0, 0))
    slab = pl.BlockSpec((2, kb, n2, c), lambda k: (0, k, 0, 0))
    if h is None:
        args = [fwd, consts["tw_cos"], consts["tw_sin"], a, l1_sum]
        in_specs = [mspec, tw, tw, slab, pl.BlockSpec((1, c), lambda k: (0, 0))]
    else:
        args = [fwd, consts["slab_inv"], consts["tw_cos"], consts["tw_sin"], a, h]
        in_specs = [mspec, mspec, tw, tw, slab, slab]
    return pl.pallas_call(
        functools.partial(_hy_slab_body, with_filter=h is not None),
        grid=(n1 // kb,),
        in_specs=in_specs,
        out_specs=slab,
        out_shape=jax.ShapeDtypeStruct(a.shape, F32 if h is not None else BF16),
        compiler_params=_cparams(("parallel",)),
        name="hy_slab_conv" if h is not None else "hy_slab_fwd",
    )(*args)


def hyena_pallas(p, conv_w, w1, b1, w2, b2, w3, skip, col0=RET_COLS):
    b, l, _ = p.shape
    assert b == 2, "the two batch rows are packed as one complex signal"
    c = HY_CH
    consts = _hy_constants(l)
    n1, n2 = consts["n1"], consts["n2"]
    *a_h, sums = hy_filter_rows(l, consts, w1, b1, w2, b2, w3)
    spectra = [hy_slab_pass(consts, a_h[o], l1_sum=sums[:, o * c:(o + 1) * c]) for o in range(HY_ORDER)]
    split = lambda t: t.reshape(2, n1 // 2, n2, c)
    v, x1, x2 = hy_prep(p, col0, conv_w, PREP_TILE)
    z = split(v)
    for o, xn in enumerate((x1, x2)):
        a = hy_rows_pass("fwd", consts["rows_fwd"], z)
        bm = hy_slab_pass(consts, a, spectra[o])
        z = hy_rows_pass("inv", consts["rows_inv"], bm,
                         extra=(split(xn), z, skip[o].astype(F32).reshape(1, c)))
    return z.reshape(b, l, c)


def _pad_cols(w, mult):
    n = w.shape[-1]
    pad = (-n) % mult
    return jnp.pad(w, ((0, 0), (0, pad))) if pad else w


def kernel(x, c, ctx, c_ctx, mod_w, mod_b, norm_mix_w, norm_ffn_w, ffn_w_in, ffn_w_out,
           ab_w_in, ab_w_out, dn_conv_w, dn_a_log, dn_dt_bias, dn_norm_w, swa_q_norm_w, swa_k_norm_w,
           swa_sink, cd_w_in, cd_w_out, ret_decay_logit, ret_gn_w, hy_conv_w, hy_f_w1, hy_f_b1,
           hy_f_w2, hy_f_b2, hy_f_w3, hy_bias):
    B, L, D = x.shape
    n_ctx = ctx.shape[1]
    rows = L // GRID_W
    c_rows = jnp.zeros((SUBLANES, D), F32).at[:B].set(c).at[B].set(c_ctx)
    mod_all = modulation(c_rows, mod_w, mod_b)
    hid = ffn_w_out.shape[1]
    h_ctx = ctx
    for layer in range(DEPTH):
        need_ctx = layer != DEPTH - 1
        i = layer // 2
        m = mod_all[layer].reshape(SUBLANES, 6, D)
        mod = [m[:B, j][:, None, :] for j in range(6)]
        mod_c = [jnp.broadcast_to(m[B, j][None, None, :], (B, 1, D)) for j in range(6)]
        if layer % 2 == 0:
            w = ab_w_in[i]
            w_in = jnp.concatenate([w[:, :4 * DN_WIDTH], w[:, DN_COLS:], w[:, 4 * DN_WIDTH:DN_COLS]], axis=1)
            w_out = ab_w_out[i]
        else:
            w = cd_w_in[i]
            w_in = jnp.concatenate([w[:, RET_COLS:], w[:, :RET_COLS]], axis=1)
            w_out = cd_w_out[i]
        w_in_p = _pad_cols(w_in, LANES).astype(BF16)
        p_lat = in_projection(x, norm_mix_w[layer], mod[0], mod[1], w_in_p, INPROJ_TILE)
        p_ctx = in_projection(h_ctx, norm_mix_w[layer], mod_c[0], mod_c[1], w_in_p, n_ctx)
        if layer % 2 == 0:
            swa0 = 4 * DN_WIDTH
            ab0 = swa0 + (SWA_HEADS + 2 * SWA_KV_HEADS) * SWA_DIM
            a_lat, a_ctx = deltanet_pallas(p_lat, p_ctx, dn_conv_w[i], dn_a_log[i], dn_dt_bias[i],
                                           dn_norm_w[i], need_ctx, ab_col=ab0, gate_col=3 * DN_WIDTH)
            b_lat, b_ctx = window_gqa_pallas(p_lat, p_ctx, swa0, axial_rope_tables(rows), swa_q_norm_w[i],
                                             swa_k_norm_w[i], swa_sink[i], need_ctx)
        else:
            a_lat, a_ctx = retention_pallas(p_lat, p_ctx, ret_decay_logit[i], ret_gn_w[i], need_ctx,
                                            col0=(HY_ORDER + 1) * HY_CH)
            assert not need_ctx, "the last layer's context outputs reach no latent token"
            b_lat = hyena_pallas(p_lat, hy_conv_w[i], hy_f_w1[i], hy_f_b1[i], hy_f_w2[i], hy_f_b2[i],
                                 hy_f_w3[i], hy_bias[i], col0=0)
            b_ctx = None
        wo = w_out.astype(BF16)
        wg = ffn_w_in[layer][:, :hid].astype(BF16)
        wu = ffn_w_in[layer][:, hid:].astype(BF16)
        wd = ffn_w_out[layer].astype(BF16)
        x = out_projection_ffn(x, a_lat, b_lat, wo, mod[2], norm_ffn_w[layer], mod[3], mod[4], mod[5],
                               wg, wu, wd, DENSE_TILE, FFN_HIDDEN_CHUNK)
        if need_ctx:
            h_ctx = out_projection_ffn(h_ctx, a_ctx, b_ctx, wo, mod_c[2], norm_ffn_w[layer], mod_c[3], mod_c[4],
                                       mod_c[5], wg, wu, wd, n_ctx, FFN_HIDDEN_CHUNK)
    return x
```

```python
import functools
import math

import jax
import jax.numpy as jnp
import ml_dtypes
import numpy as np
from jax import lax
from jax.experimental import pallas as pl
from jax.experimental.pallas import tpu as pltpu

F32 = jnp.float32
BF16 = jnp.bfloat16
EPS = 1e-6

DEPTH = 2
GRID_W = 64

DN_HEADS = 4
DN_DIM = 128
DN_WIDTH = DN_HEADS * DN_DIM
DN_COLS = 4 * DN_WIDTH + 4 * DN_HEADS
SWA_HEADS = 4
SWA_KV_HEADS = 2
SWA_DIM = 128
SWA_BLOCK = 128
ROPE_THETA = 10000.0
RET_HEADS = 4
RET_DIM = 128
RET_WIDTH = RET_HEADS * RET_DIM
RET_CHUNK = 128
RET_COLS = 4 * RET_WIDTH
HY_CH = 512
HY_ORDER = 2
HY_BANDS = 8
HY_SIN_FREQ = 1.0
HY_TARGET = 1e-2
HY_DECAY_SHORT = 0.3
HY_DECAY_LONG = 1.5

VMEM_LIMIT_BYTES = 56 * 1024 * 1024
SUBLANES, LANES = 8, 128

DENSE_TILE = 512
INPROJ_TILE = 1024
FFN_HIDDEN_CHUNK = 256
PREP_TILE = 1024
SWA_BLOCKS_PER_STEP = 4
MOD_COL_TILE = 1536


def _cparams(sem):
    return pltpu.CompilerParams(dimension_semantics=sem, vmem_limit_bytes=VMEM_LIMIT_BYTES)


def _mod_body(c_ref, w_ref, b_ref, o_ref):
    a = c_ref[...]
    a = a * jax.nn.sigmoid(a)
    o_ref[...] = jnp.dot(a, w_ref[...], preferred_element_type=F32,
                         precision=lax.Precision.HIGHEST) + b_ref[...]


def modulation(c_rows, mod_w, mod_b):
    depth, d, n = mod_w.shape
    tn = MOD_COL_TILE
    return pl.pallas_call(
        _mod_body,
        grid=(depth, n // tn),
        in_specs=[pl.BlockSpec((8, d), lambda l, j: (0, 0)),
                  pl.BlockSpec((None, d, tn), lambda l, j: (l, 0, j)),
                  pl.BlockSpec((None, 1, tn), lambda l, j: (l, 0, j))],
        out_specs=pl.BlockSpec((None, 8, tn), lambda l, j: (l, 0, j)),
        out_shape=jax.ShapeDtypeStruct((depth, 8, n), F32),
        compiler_params=_cparams(("parallel", "parallel")),
        name="modulation",
    )(c_rows, mod_w, mod_b.reshape(depth, 1, n))


def _norm_mod(x, nw, shift, scale):
    y = x * lax.rsqrt(jnp.mean(x * x, axis=-1, keepdims=True) + EPS)
    return (y * nw) * (1.0 + scale) + shift


def _resident(shape):
    return pl.BlockSpec(shape, lambda *_: (0,) * len(shape), pipeline_mode=pl.Buffered(1))


def _inproj_body(x_ref, nw_ref, shift_ref, scale_ref, w_ref, o_ref):
    h = _norm_mod(x_ref[...], nw_ref[...], shift_ref[...], scale_ref[...]).astype(BF16)
    o_ref[...] = jnp.dot(h, w_ref[...], preferred_element_type=F32)


def in_projection(x, nw, shift, scale, w, tm):
    b, l, d = x.shape
    n = w.shape[1]
    return pl.pallas_call(
        _inproj_body,
        grid=(b, l // tm),
        in_specs=[pl.BlockSpec((None, tm, d), lambda bi, i: (bi, i, 0)),
                  _resident((1, d)),
                  pl.BlockSpec((None, 1, d), lambda bi, i: (bi, 0, 0)),
                  pl.BlockSpec((None, 1, d), lambda bi, i: (bi, 0, 0)),
                  _resident((d, n))],
        out_specs=pl.BlockSpec((None, tm, n), lambda bi, i: (bi, i, 0)),
        out_shape=jax.ShapeDtypeStruct((b, l, n), F32),
        compiler_params=_cparams(("parallel", "parallel")),
        name="in_projection",
    )(x, nw.reshape(1, d), shift, scale, w)


def _outffn_body(x_ref, oa_ref, ob_ref, wo_ref, g2_ref, nw_ref, sh_ref, sc_ref, g5_ref,
                 wg_ref, wu_ref, wd_ref, y_ref, x1_scr, h_scr, acc_scr, *, th):
    wa = oa_ref.shape[1]
    mix = (jnp.dot(oa_ref[...].astype(BF16), wo_ref[:wa, :], preferred_element_type=F32)
           + jnp.dot(ob_ref[...].astype(BF16), wo_ref[wa:, :], preferred_element_type=F32))
    x1 = x_ref[...] + g2_ref[...] * mix
    x1_scr[...] = x1
    h_scr[...] = _norm_mod(x1, nw_ref[...], sh_ref[...], sc_ref[...]).astype(BF16)
    for k in range(wg_ref.shape[1] // th):
        ks = slice(k * th, (k + 1) * th)
        g = jnp.dot(h_scr[...], wg_ref[:, ks], preferred_element_type=F32)
        u = jnp.dot(h_scr[...], wu_ref[:, ks], preferred_element_type=F32)
        a = (g * jax.nn.sigmoid(g) * u).astype(BF16)
        part = jnp.dot(a, wd_ref[ks, :], preferred_element_type=F32)
        if k == 0:
            acc_scr[...] = part
        else:
            acc_scr[...] += part
    y_ref[...] = x1_scr[...] + g5_ref[...] * acc_scr[...]


def out_projection_ffn(x, oa, ob, wo, g2, nw, shift, scale, g5, wg, wu, wd, tm, th):
    b, l, d = x.shape
    wa, wb = oa.shape[-1], ob.shape[-1]
    hid = wg.shape[1]
    tok = lambda bi, i: (bi, i, 0)
    vec = pl.BlockSpec((None, 1, d), lambda bi, i: (bi, 0, 0))
    return pl.pallas_call(
        functools.partial(_outffn_body, th=th),
        grid=(b, l // tm),
        in_specs=[pl.BlockSpec((None, tm, d), tok),
                  pl.BlockSpec((None, tm, wa), tok),
                  pl.BlockSpec((None, tm, wb), tok),
                  _resident((d, d)), vec, _resident((1, d)), vec, vec, vec,
                  _resident((d, hid)), _resident((d, hid)), _resident((hid, d))],
        out_specs=pl.BlockSpec((None, tm, d), tok),
        out_shape=jax.ShapeDtypeStruct((b, l, d), F32),
        scratch_shapes=[pltpu.VMEM((tm, d), F32), pltpu.VMEM((tm, d), BF16), pltpu.VMEM((tm, d), F32)],
        compiler_params=_cparams(("parallel", "parallel")),
        name="out_projection_ffn",
    )(x, oa, ob, wo, g2, nw.reshape(1, d), shift, scale, g5, wg, wu, wd)


NEG_BIG = -1e30


def _head_rms(x, w):
    return x * lax.rsqrt(jnp.mean(x * x, axis=-1, keepdims=True) + EPS) * w


def _swa_prep_body(*refs, use_rope):
    if use_rope:
        q_ref, k_ref, v_ref, qw_ref, kw_ref, rt_ref, ct_ref, qo_ref, ko_ref, vo_ref = refs
        tl = q_ref.shape[0]
        nr = tl // GRID_W

        def table(kind):
            rowp = jnp.broadcast_to(rt_ref[kind][:, None, :], (nr, GRID_W, SWA_DIM))
            colp = jnp.broadcast_to(ct_ref[kind][None, :, :], (nr, GRID_W, SWA_DIM))
            return (rowp + colp).reshape(tl, SWA_DIM)

        cos, sa, sb = table(0), table(1), table(2)
    else:
        q_ref, k_ref, v_ref, qw_ref, kw_ref, qo_ref, ko_ref, vo_ref = refs

    def prep(x, w):
        y = _head_rms(x, w)
        if use_rope:
            y = y * cos + pltpu.roll(y, SWA_DIM - 32, 1) * sa + pltpu.roll(y, 32, 1) * sb
        return y.astype(BF16)

    for h in range(SWA_HEADS):
        sl = slice(h * SWA_DIM, (h + 1) * SWA_DIM)
        qo_ref[:, sl] = prep(q_ref[:, sl], qw_ref[...])
    for h in range(SWA_KV_HEADS):
        sl = slice(h * SWA_DIM, (h + 1) * SWA_DIM)
        ko_ref[:, sl] = prep(k_ref[:, sl], kw_ref[...])
    vo_ref[...] = v_ref[...].astype(BF16)


def swa_prep(p, col0, qw, kw, rope, tl):
    b, l, _ = p.shape
    nq, nk = SWA_HEADS * SWA_DIM, SWA_KV_HEADS * SWA_DIM
    tok = lambda bi, i: (bi, i, 0)
    in_specs = [pl.BlockSpec((None, tl, nq), lambda bi, i: (bi, i, col0 // nq)),
                pl.BlockSpec((None, tl, nk), lambda bi, i: (bi, i, (col0 + nq) // nk)),
                pl.BlockSpec((None, tl, nk), lambda bi, i: (bi, i, (col0 + nq + nk) // nk)),
                pl.BlockSpec((1, SWA_DIM), lambda bi, i: (0, 0)),
                pl.BlockSpec((1, SWA_DIM), lambda bi, i: (0, 0))]
    args = [p, p, p, qw.reshape(1, SWA_DIM), kw.reshape(1, SWA_DIM)]
    if rope is not None:
        in_specs += [pl.BlockSpec((3, tl // GRID_W, SWA_DIM), lambda bi, i: (0, i, 0)),
                     pl.BlockSpec((3, GRID_W, SWA_DIM), lambda bi, i: (0, 0, 0))]
        args += list(rope)
    return pl.pallas_call(
        functools.partial(_swa_prep_body, use_rope=rope is not None),
        grid=(b, l // tl),
        in_specs=in_specs,
        out_specs=[pl.BlockSpec((None, tl, nq), tok), pl.BlockSpec((None, tl, nk), tok),
                   pl.BlockSpec((None, tl, nk), tok)],
        out_shape=[jax.ShapeDtypeStruct((b, l, nq), BF16), jax.ShapeDtypeStruct((b, l, nk), BF16),
                   jax.ShapeDtypeStruct((b, l, nk), BF16)],
        compiler_params=_cparams(("parallel", "parallel")),
        name="swa_prep",
    )(*args)


def _nt_dot(a, b):
    return lax.dot_general(a, b, (((1,), (1,)), ((), ())), preferred_element_type=F32)


def _swa_attend(problems):
    def lane_tiles(x):
        return [x[:, c * LANES:(c + 1) * LANES] for c in range(x.shape[1] // LANES)]

    def masked_scores(q, keys, masks):
        out = []
        for kk, mask in zip(keys, masks):
            s = _nt_dot(q, kk)
            out.append(s if mask is None else jnp.where(mask, s, NEG_BIG))
        return out

    def row_max(scores, sink_col):
        m = sink_col + jnp.zeros((1, LANES), F32)
        for s in scores:
            for tile in lane_tiles(s):
                m = jnp.maximum(m, tile)
        return jnp.max(m, axis=-1, keepdims=True)

    def weighted(scores, vals, m, sink_col):
        part, acc = None, None
        for s, vv in zip(scores, vals):
            pr = jnp.exp(s - m)
            for tile in lane_tiles(pr):
                part = tile if part is None else part + tile
            o = jnp.dot(pr.astype(BF16), vv, preferred_element_type=F32)
            acc = o if acc is None else acc + o
        return acc, jnp.exp(sink_col - m) + jnp.sum(part, axis=-1, keepdims=True)

    scores = [masked_scores(q, keys, masks) for q, keys, _, masks, _ in problems]
    maxes = [row_max(s, p[4]) for s, p in zip(scores, problems)]
    outs = [weighted(s, p[2], m, p[4]) for s, p, m in zip(scores, problems, maxes)]
    return [acc / den for acc, den in outs]


def _swa_attn_body(sink_ref, q_ref, kp_ref, km_ref, kn_ref, vp_ref, vm_ref, vn_ref, kc_ref, vc_ref,
                   o_ref, *, nb):
    i = pl.program_id(1)
    n_blocks = pl.num_programs(1) * nb
    w, d = SWA_BLOCK, SWA_DIM
    grp = SWA_HEADS // SWA_KV_HEADS
    rows = lax.broadcasted_iota(jnp.int32, (grp * w, w), 0)
    qi = rows % w
    kj = lax.broadcasted_iota(jnp.int32, (grp * w, w), 1)
    sink_cols = []
    for h in range(SWA_KV_HEADS):
        sink_col = jnp.zeros((grp * w, 1), F32)
        for g in range(grp):
            sink_col = jnp.where((rows[:, :1] // w) == g, sink_ref[h * grp + g], sink_col)
        sink_cols.append(sink_col)
    problems, places = [], []
    for r in range(nb):
        blk = i * nb + r
        mask_prev = (kj >= qi) & (blk > 0)
        mask_next = (kj <= qi) & (blk < n_blocks - 1)
        rs = slice(r * w, (r + 1) * w)
        for h in range(SWA_KV_HEADS):
            hs = slice(h * d, (h + 1) * d)
            q = jnp.concatenate([q_ref[rs, (h * grp + g) * d:(h * grp + g + 1) * d] for g in range(grp)], axis=0)
            if r > 0:
                k_prev, v_prev = km_ref[(r - 1) * w:r * w, hs], vm_ref[(r - 1) * w:r * w, hs]
            else:
                k_prev, v_prev = kp_ref[:, hs], vp_ref[:, hs]
            if r < nb - 1:
                k_next, v_next = km_ref[(r + 1) * w:(r + 2) * w, hs], vm_ref[(r + 1) * w:(r + 2) * w, hs]
            else:
                k_next, v_next = kn_ref[:, hs], vn_ref[:, hs]
            problems.append((q, [k_prev, km_ref[rs, hs], k_next, kc_ref[:, hs]],
                             [v_prev, vm_ref[rs, hs], v_next, vc_ref[:, hs]],
                             [mask_prev, None, mask_next, None], sink_cols[h]))
            places.append((rs, h))
    for (rs, h), o in zip(places, _swa_attend(problems)):
        for g in range(grp):
            o_ref[rs, (h * grp + g) * d:(h * grp + g + 1) * d] = o[g * w:(g + 1) * w].astype(o_ref.dtype)


def swa_attention(q, k, v, kc, vc, sink, nb):
    b, l, nq = q.shape
    nk = k.shape[-1]
    n_ctx = kc.shape[1]
    w = SWA_BLOCK
    tq = nb * w
    last = l // w - 1
    main = lambda bi, i, s: (bi, i, 0)
    prev = lambda bi, i, s: (bi, jnp.maximum(i * nb - 1, 0), 0)
    nxt = lambda bi, i, s: (bi, jnp.minimum((i + 1) * nb, last), 0)
    cx = lambda bi, i, s: (bi, 0, 0)
    return pl.pallas_call(
        functools.partial(_swa_attn_body, nb=nb),
        grid_spec=pltpu.PrefetchScalarGridSpec(
            num_scalar_prefetch=1,
            grid=(b, l // tq),
            in_specs=[pl.BlockSpec((None, tq, nq), main),
                      pl.BlockSpec((None, w, nk), prev), pl.BlockSpec((None, tq, nk), main),
                      pl.BlockSpec((None, w, nk), nxt),
                      pl.BlockSpec((None, w, nk), prev), pl.BlockSpec((None, tq, nk), main),
                      pl.BlockSpec((None, w, nk), nxt),
                      pl.BlockSpec((None, n_ctx, nk), cx), pl.BlockSpec((None, n_ctx, nk), cx)],
            out_specs=pl.BlockSpec((None, tq, nq), main)),
        out_shape=jax.ShapeDtypeStruct((b, l, nq), BF16),
        compiler_params=_cparams(("parallel", "parallel")),
        name="swa_attention",
    )(sink, q, k, k, k, v, v, v, kc, vc)


def _ctx_attn_body(sink_ref, q_ref, kc_ref, vc_ref, o_ref):
    n_ctx, d = q_ref.shape[0], SWA_DIM
    grp = SWA_HEADS // SWA_KV_HEADS
    rows = lax.broadcasted_iota(jnp.int32, (grp * n_ctx, 1), 0)
    problems = []
    for h in range(SWA_KV_HEADS):
        hs = slice(h * d, (h + 1) * d)
        q = jnp.concatenate([q_ref[:, (h * grp + g) * d:(h * grp + g + 1) * d] for g in range(grp)], axis=0)
        sink_col = jnp.zeros((grp * n_ctx, 1), F32)
        for g in range(grp):
            sink_col = jnp.where((rows // n_ctx) == g, sink_ref[h * grp + g], sink_col)
        problems.append((q, [kc_ref[:, hs]], [vc_ref[:, hs]], [None], sink_col))
    for h, o in enumerate(_swa_attend(problems)):
        for g in range(grp):
            o_ref[:, (h * grp + g) * d:(h * grp + g + 1) * d] = o[g * n_ctx:(g + 1) * n_ctx].astype(o_ref.dtype)


def ctx_attention(qc, kc, vc, sink):
    b, n_ctx, nq = qc.shape
    nk = kc.shape[-1]
    cx = lambda bi, s: (bi, 0, 0)
    return pl.pallas_call(
        _ctx_attn_body,
        grid_spec=pltpu.PrefetchScalarGridSpec(
            num_scalar_prefetch=1, grid=(b,),
            in_specs=[pl.BlockSpec((None, n_ctx, nq), cx), pl.BlockSpec((None, n_ctx, nk), cx),
                      pl.BlockSpec((None, n_ctx, nk), cx)],
            out_specs=pl.BlockSpec((None, n_ctx, nq), cx)),
        out_shape=jax.ShapeDtypeStruct((b, n_ctx, nq), BF16),
        compiler_params=_cparams(("parallel",)),
        name="ctx_attention",
    )(sink, qc, kc, vc)


def axial_rope_tables(rows):
    m = SWA_DIM // 4
    inv = ROPE_THETA ** (-np.arange(m, dtype=np.float64) / m)

    def tables(count, lane0):
        ang = np.arange(count, dtype=np.float64)[:, None] * inv
        out = np.zeros((3, count, SWA_DIM))
        out[0, :, lane0:lane0 + m] = out[0, :, lane0 + m:lane0 + 2 * m] = np.cos(ang)
        out[1, :, lane0:lane0 + m] = -np.sin(ang)
        out[2, :, lane0 + m:lane0 + 2 * m] = np.sin(ang)
        return out.astype(np.float32)

    return tables(rows, 0), tables(GRID_W, SWA_DIM // 2)


def window_gqa_pallas(p_lat, p_ctx, col0, rope, q_norm_w, k_norm_w, sink, need_ctx):
    qw = q_norm_w.astype(F32) * SWA_DIM ** -0.5
    ql, kl, vl = swa_prep(p_lat, col0, qw, k_norm_w, rope, PREP_TILE)
    qc, kc, vc = swa_prep(p_ctx, col0, qw, k_norm_w, None, p_ctx.shape[1])
    o_lat = swa_attention(ql, kl, vl, kc, vc, sink, SWA_BLOCKS_PER_STEP)
    o_ctx = ctx_attention(qc, kc, vc, sink) if need_ctx else None
    return o_lat, o_ctx


def _tn_dot(a, b):
    return lax.dot_general(a, b, (((0,), (0,)), ((), ())), preferred_element_type=F32)


def _log_sigmoid(x):
    return jnp.minimum(x, 0.0) - jnp.log(1.0 + jnp.exp(-jnp.abs(x)))


RET_STREAMS = 2 * RET_HEADS


RET_SUBS = 4


def _ret_scan_body(logit_ref, qf_ref, kf_ref, vf_ref, rotf_ref, qb_ref, kb_ref, vb_ref, rotb_ref, rotc_ref,
                   s0_ref, of_ref, ob_ref, sf_ref, s_scr):
    n = pl.program_id(1)
    c, dh = RET_CHUNK, RET_DIM

    @pl.when(n == 0)
    def _():
        s_scr[...] = s0_ref[...]

    ii = lax.broadcasted_iota(jnp.int32, (c, c), 0)
    jj = lax.broadcasted_iota(jnp.int32, (c, c), 1)
    i1 = lax.broadcasted_iota(jnp.int32, (c, 1), 0)
    rel = {True: (ii - jj).astype(F32), False: (jj - ii).astype(F32)}
    pos = {True: i1.astype(F32), False: (c - 1 - i1).astype(F32)}
    streams = [(d == 0, h) for d in range(2) for h in range(RET_HEADS)]
    srcs = {True: (qf_ref, kf_ref, vf_ref, rotf_ref), False: (qb_ref, kb_ref, vb_ref, rotb_ref)}
    subs = rotf_ref.shape[0]
    items = [(r, f, h, j, (j if f else subs - 1 - j)) for j in range(subs) for r, (f, h) in enumerate(streams)]
    rows = lambda blk: slice(blk * c, (blk + 1) * c)
    hs = lambda h: slice(h * dh, (h + 1) * dh)
    rng = range(len(items))

    cos_i, sin_i = rotc_ref[0], rotc_ref[1]
    tables = {}
    for f in (True, False):
        for bk in range(subs):
            a = srcs[f][3][bk]
            tables[f, bk] = (a[0:1] * cos_i - a[1:2] * sin_i, a[2:3] * cos_i + a[3:4] * sin_i)

    def rot(x, f, bk):
        cos_t, sin_t = tables[f, bk]
        return x * cos_t + pltpu.roll(x, dh // 2, 1) * sin_t

    lg = [_log_sigmoid(jnp.full((1, 1), logit_ref[r], F32)) for r in range(RET_STREAMS)]
    q = [rot(srcs[f][0][rows(bk), hs(h)], f, bk) for _, f, h, _, bk in items]
    k = [rot(srcs[f][1][rows(bk), hs(h)], f, bk) * dh ** -0.5 for _, f, h, _, bk in items]
    v = [srcs[f][2][rows(bk), hs(h)].astype(BF16) for _, f, h, _, bk in items]
    dmat = [jnp.where(rel[f] >= 0, jnp.exp(lg[r] * jnp.maximum(rel[f], 0.0)), 0.0)
            for r, (f, _) in enumerate(streams)]
    scores = [(_nt_dot(q[i].astype(BF16), k[i].astype(BF16)) * dmat[items[i][0]]).astype(BF16) for i in rng]
    o_in = [jnp.dot(scores[i], v[i], preferred_element_type=F32) for i in rng]
    q_dec = [(q[i] * jnp.exp(lg[items[i][0]] * (pos[items[i][1]] + 1.0))).astype(BF16) for i in rng]
    k_dec = [(k[i] * jnp.exp(lg[items[i][0]] * (c - 1.0 - pos[items[i][1]]))).astype(BF16) for i in rng]
    kv = [_tn_dot(k_dec[i], v[i]) for i in rng]
    chunk_dec = [jnp.exp(lg[r] * c) for r in range(RET_STREAMS)]

    s = [s_scr[r] for r in range(RET_STREAMS)]
    for j in range(subs):
        idx = [i for i in rng if items[i][3] == j]
        o = [o_in[i] + jnp.dot(q_dec[i], s[items[i][0]].astype(BF16), preferred_element_type=F32) for i in idx]
        s = [s[items[i][0]] * chunk_dec[items[i][0]] + kv[i] for i in idx]
        for i, oi in zip(idx, o):
            _, f, h, _, bk = items[i]
            (of_ref if f else ob_ref)[rows(bk), hs(h)] = oi.astype(of_ref.dtype)
    for r in range(RET_STREAMS):
        s_scr[r] = s[r]

    @pl.when(n == pl.num_programs(1) - 1)
    def _():
        sf_ref[...] = s_scr[...]


def retention_scan(p, col0, decay_logit, rot_chunk, rot_in, s0):
    b, l, _ = p.shape
    subs = min(RET_SUBS, l // RET_CHUNK)
    c, dh, w = RET_CHUNK * subs, RET_DIM, RET_WIDTH
    n = l // c
    cb = col0 // w
    fcol = lambda off: (lambda bi, ni, s: (bi, ni, cb + off))
    bcol = lambda off: (lambda bi, ni, s: (bi, n - 1 - ni, cb + off))
    rot = lambda m: pl.BlockSpec((subs, 4, dh), m)
    st = lambda bi, ni, s: (bi, 0, 0, 0)
    tok = lambda col: [pl.BlockSpec((None, c, w), col(j)) for j in range(3)]
    return pl.pallas_call(
        _ret_scan_body,
        grid_spec=pltpu.PrefetchScalarGridSpec(
            num_scalar_prefetch=1,
            grid=(b, n),
            in_specs=tok(fcol) + [rot(lambda bi, ni, s: (ni, 0, 0))]
            + tok(bcol) + [rot(lambda bi, ni, s: (n - 1 - ni, 0, 0))]
            + [pl.BlockSpec((2, RET_CHUNK, dh), lambda bi, ni, s: (0, 0, 0)),
               pl.BlockSpec((None, RET_STREAMS, dh, dh), st)],
            out_specs=[pl.BlockSpec((None, c, w), lambda bi, ni, s: (bi, ni, 0)),
                       pl.BlockSpec((None, c, w), lambda bi, ni, s: (bi, n - 1 - ni, 0)),
                       pl.BlockSpec((None, RET_STREAMS, dh, dh), st)],
            scratch_shapes=[pltpu.VMEM((RET_STREAMS, dh, dh), F32)]),
        out_shape=[jax.ShapeDtypeStruct((b, l, w), BF16), jax.ShapeDtypeStruct((b, l, w), BF16),
                   jax.ShapeDtypeStruct((b, RET_STREAMS, dh, dh), F32)],
        compiler_params=_cparams(("parallel", "arbitrary")),
        name="retention_scan",
    )(decay_logit.reshape(-1), p, p, p, rot_chunk, p, p, p, rot_chunk, rot_in, s0)


def _ret_final_body(of_ref, ob_ref, g_ref, w_ref, y_ref):
    for h in range(RET_HEADS):
        sl = slice(h * RET_DIM, (h + 1) * RET_DIM)
        o = of_ref[:, sl].astype(F32) + ob_ref[:, sl].astype(F32)
        mu = jnp.mean(o, axis=-1, keepdims=True)
        var = jnp.mean(jnp.square(o - mu), axis=-1, keepdims=True)
        y = (o - mu) * lax.rsqrt(var + EPS) * w_ref[:, sl]
        g = g_ref[:, sl]
        y_ref[:, sl] = (y * (g * jax.nn.sigmoid(g))).astype(y_ref.dtype)


def retention_finalize(o_f, o_b, p, gate_col, gn_w, tl):
    b, l, w = o_f.shape
    tok = lambda bi, i: (bi, i, 0)
    return pl.pallas_call(
        _ret_final_body,
        grid=(b, l // tl),
        in_specs=[pl.BlockSpec((None, tl, w), tok),
                  pl.BlockSpec((None, tl, w), tok),
                  pl.BlockSpec((None, tl, w), lambda bi, i: (bi, i, gate_col // w)),
                  pl.BlockSpec((1, w), lambda bi, i: (0, 0))],
        out_specs=pl.BlockSpec((None, tl, w), tok),
        out_shape=jax.ShapeDtypeStruct((b, l, w), BF16),
        compiler_params=_cparams(("parallel", "parallel")),
        name="retention_finalize",
    )(o_f, o_b, p, gn_w.reshape(1, w))


def retention_rope_tables(l):
    inv = ROPE_THETA ** (-np.linspace(0.0, 1.0, RET_DIM // 2))
    inv2 = np.concatenate([inv, inv])
    sign = np.concatenate([-np.ones(RET_DIM // 2), np.ones(RET_DIM // 2)])
    a = (RET_CHUNK * np.arange(l // RET_CHUNK, dtype=np.float64))[:, None] * inv2
    b = np.arange(RET_CHUNK, dtype=np.float64)[:, None] * inv2
    chunk = np.stack([np.cos(a), np.sin(a), sign * np.sin(a), sign * np.cos(a)], axis=1)
    return chunk.astype(np.float32), np.stack([np.cos(b), np.sin(b)]).astype(np.float32)


def retention_pallas(p_lat, p_ctx, decay_logit, gn_w, need_ctx, col0=0):
    b, l, _ = p_lat.shape
    n_ctx = p_ctx.shape[1]
    gate_col = col0 + 3 * RET_WIDTH
    s0 = jnp.zeros((b, RET_STREAMS, RET_DIM, RET_DIM), F32)
    oc_f, oc_b, s_ctx = retention_scan(p_ctx, col0, decay_logit, *retention_rope_tables(n_ctx), s0)
    ol_f, ol_b, _ = retention_scan(p_lat, col0, decay_logit, *retention_rope_tables(l), s_ctx)
    out_lat = retention_finalize(ol_f, ol_b, p_lat, gate_col, gn_w, PREP_TILE)
    out_ctx = retention_finalize(oc_f, oc_b, p_ctx, gate_col, gn_w, n_ctx) if need_ctx else None
    return out_lat, out_ctx


DN_BLOCK = 128
DN_STREAMS = 2 * DN_HEADS
DN_SUBS = 4


def _shift_rows(x, prev_row, next_row):
    n = x.shape[0]
    r = lax.broadcasted_iota(jnp.int32, (SUBLANES, 1), 0)
    down, up = pltpu.roll(x, 1, 0), pltpu.roll(x, n - 1, 0)
    x_prev = jnp.concatenate([jnp.where(r == 0, prev_row, down[:SUBLANES]), down[SUBLANES:]], axis=0)
    x_next = jnp.concatenate([up[:n - SUBLANES], jnp.where(r == SUBLANES - 1, next_row, up[n - SUBLANES:])], axis=0)
    return x_prev, x_next


def _conv3(x_ref, xp_ref, xn_ref, w_ref):
    i = pl.program_id(1)
    prev_row = jnp.where(i > 0, xp_ref[SUBLANES - 1:SUBLANES, :], 0.0)
    next_row = jnp.where(i < pl.num_programs(1) - 1, xn_ref[0:1, :], 0.0)
    x = x_ref[...]
    x_prev, x_next = _shift_rows(x, prev_row, next_row)
    return x_prev * w_ref[0:1, :] + x * w_ref[1:2, :] + x_next * w_ref[2:3, :]


def _dn_prep_body(x_ref, xp_ref, xn_ref, ab_ref, cw_ref, alog_ref, dtb_ref,
                  q_ref, k_ref, v_ref, gb_ref, gam_ref, gamt_ref):
    y = _conv3(x_ref, xp_ref, xn_ref, cw_ref)
    y = y * jax.nn.sigmoid(y)
    for h in range(DN_HEADS):
        for part, (ref, mul) in enumerate(((q_ref, DN_DIM ** -0.5), (k_ref, 1.0))):
            sl = slice(part * DN_WIDTH + h * DN_DIM, part * DN_WIDTH + (h + 1) * DN_DIM)
            t = y[:, sl]
            t = t * lax.rsqrt(jnp.sum(t * t, axis=-1, keepdims=True) + EPS)
            ref[:, h * DN_DIM:(h + 1) * DN_DIM] = t * mul if mul != 1.0 else t
    v_ref[...] = y[:, 2 * DN_WIDTH:]

    ab = ab_ref[...]
    z = ab + dtb_ref[...]
    softplus = jnp.maximum(z, 0.0) + jnp.log(1.0 + jnp.exp(-jnp.abs(z)))
    g = -jnp.exp(alog_ref[...]) * softplus
    lane = lax.broadcasted_iota(jnp.int32, ab.shape, 1)
    gb_ref[...] = jnp.where(lane < DN_STREAMS, g, jax.nn.sigmoid(ab))

    c = DN_BLOCK
    ii = lax.broadcasted_iota(jnp.int32, (c, c), 0)
    jj = lax.broadcasted_iota(jnp.int32, (c, c), 1)
    tri_f = (ii >= jj).astype(F32)
    lane_c = lax.broadcasted_iota(jnp.int32, (c, ab.shape[1]), 1)
    for n in range(x_ref.shape[0] // c):
        gc = g[n * c:(n + 1) * c, :]
        cf = jnp.dot(tri_f, gc, preferred_element_type=F32, precision=lax.Precision.HIGHEST)
        cb = cf[c - 1:c, :] - cf + gc
        gam = jnp.where(lane_c < DN_HEADS, cf, cb)
        gam_ref[n * c:(n + 1) * c, :] = gam
        gamt_ref[n] = gam.T[:DN_STREAMS, :]


def dn_prep(p, ab_col, conv_w, a_log, dt_bias, tl):
    b, l, _ = p.shape
    c3 = 3 * DN_WIDTH
    nblk8 = l // SUBLANES
    lanes = LANES
    pad8 = lambda a: jnp.pad(a.reshape(1, DN_STREAMS).astype(F32), ((0, 0), (0, lanes - DN_STREAMS)))
    tok = lambda bi, i: (bi, i, 0)
    return pl.pallas_call(
        _dn_prep_body,
        grid=(b, l // tl),
        in_specs=[pl.BlockSpec((None, tl, c3), tok),
                  pl.BlockSpec((None, SUBLANES, c3),
                               lambda bi, i: (bi, jnp.maximum(i * (tl // SUBLANES) - 1, 0), 0)),
                  pl.BlockSpec((None, SUBLANES, c3),
                               lambda bi, i: (bi, jnp.minimum((i + 1) * (tl // SUBLANES), nblk8 - 1), 0)),
                  pl.BlockSpec((None, tl, lanes), lambda bi, i: (bi, i, ab_col // lanes)),
                  pl.BlockSpec((3, c3), lambda bi, i: (0, 0)),
                  pl.BlockSpec((1, lanes), lambda bi, i: (0, 0)),
                  pl.BlockSpec((1, lanes), lambda bi, i: (0, 0))],
        out_specs=[pl.BlockSpec((None, tl, DN_WIDTH), tok)] * 3
        + [pl.BlockSpec((None, tl, lanes), tok)] * 2
        + [pl.BlockSpec((None, tl // DN_BLOCK, DN_STREAMS, DN_BLOCK), lambda bi, i: (bi, i, 0, 0))],
        out_shape=[jax.ShapeDtypeStruct((b, l, DN_WIDTH), F32)] * 3
        + [jax.ShapeDtypeStruct((b, l, lanes), F32)] * 2
        + [jax.ShapeDtypeStruct((b, l // DN_BLOCK, DN_STREAMS, DN_BLOCK), F32)],
        compiler_params=_cparams(("parallel", "parallel")),
        name="dn_prep",
    )(p, p, p, p, conv_w, pad8(a_log), pad8(dt_bias))


def _dot16(a, b):
    return jnp.dot(a.astype(BF16), b.astype(BF16), preferred_element_type=F32)


def _unit_triangular_inverses(ms, ii, jj):
    c = ms[0].shape[0]
    eye = (ii == jj).astype(F32)
    pair = (ii // 2) == (jj // 2)
    ts = [eye - jnp.where(pair, m, 0.0) for m in ms]
    s = 2
    while s < c:
        sub = ((ii // (2 * s)) == (jj // (2 * s))) & ((ii // s) != (jj // s))
        t16 = [t.astype(BF16) for t in ts]
        xs = [_dot16(jnp.where(sub, m, 0.0), t) for m, t in zip(ms, t16)]
        ts = [t - _dot16(tb, x) for t, tb, x in zip(ts, t16, xs)]
        s *= 2
    return ts


def _dn_scan_body(qf_ref, kf_ref, vf_ref, gbf_ref, gamf_ref, gamtf_ref,
                  qb_ref, kb_ref, vb_ref, gbb_ref, gamb_ref, gamtb_ref, s0_ref,
                  of_ref, ob_ref, sf_ref, s_scr):
    n = pl.program_id(1)

    @pl.when(n == 0)
    def _():
        s_scr[...] = s0_ref[...]

    c, dh = DN_BLOCK, DN_DIM
    ii = lax.broadcasted_iota(jnp.int32, (c, c), 0)
    jj = lax.broadcasted_iota(jnp.int32, (c, c), 1)
    incl = {True: ii >= jj, False: ii <= jj}
    strict = {True: ii > jj, False: ii < jj}
    srcs = {True: (qf_ref, kf_ref, vf_ref, gbf_ref, gamf_ref, gamtf_ref),
            False: (qb_ref, kb_ref, vb_ref, gbb_ref, gamb_ref, gamtb_ref)}
    streams = [(d == 0, h) for d in range(2) for h in range(DN_HEADS)]
    hs = lambda h: slice(h * dh, (h + 1) * dh)
    subs = gamtf_ref.shape[0]
    items = [(r, f, h, j, (j if f else subs - 1 - j)) for j in range(subs) for r, (f, h) in enumerate(streams)]
    rows = lambda blk: slice(blk * c, (blk + 1) * c)
    rng = range(len(items))

    q = [srcs[f][0][rows(bk), hs(h)] for _, f, h, _, bk in items]
    k = [srcs[f][1][rows(bk), hs(h)] for _, f, h, _, bk in items]
    v = [srcs[f][2][rows(bk), hs(h)] for _, f, h, _, bk in items]
    beta = [srcs[f][3][rows(bk), DN_STREAMS + r:DN_STREAMS + r + 1] for r, f, _, _, bk in items]
    gcol = [srcs[f][4][rows(bk), r:r + 1] for r, f, _, _, bk in items]
    grow = [srcs[f][5][bk, r:r + 1, :] for r, f, _, _, bk in items]
    g_last = [gcol[i][c - 1:c, :] if items[i][1] else gcol[i][0:1, :] for i in rng]

    e = [jnp.exp(jnp.where(incl[items[i][1]], gcol[i] - grow[i], 0.0)) for i in rng]
    kb = [k[i] * beta[i] for i in rng]
    k16 = [k[i].astype(BF16) for i in rng]
    m = [_nt_dot(kb[i].astype(BF16), k16[i]) * jnp.where(strict[items[i][1]], e[i], 0.0) for i in rng]
    attn = [(_nt_dot(q[i].astype(BF16), k16[i]) * jnp.where(incl[items[i][1]], e[i], 0.0)).astype(BF16) for i in rng]
    t = _unit_triangular_inverses(m, ii, jj)
    eg = [jnp.exp(gcol[i]) for i in rng]
    sol = [_dot16(t[i], jnp.concatenate([v[i] * beta[i], kb[i] * eg[i]], axis=-1)) for i in rng]
    u = [sol[i][:, :dh] for i in rng]
    w16 = [sol[i][:, dh:].astype(BF16) for i in rng]
    q_dec = [(q[i] * eg[i]).astype(BF16) for i in rng]
    k_dec = [(k[i] * jnp.exp(g_last[i] - gcol[i])).astype(BF16) for i in rng]
    blk_dec = [jnp.exp(g_last[i]) for i in rng]

    s = [s_scr[r] for r in range(DN_STREAMS)]
    for j in range(subs):
        idx = [i for i in rng if items[i][3] == j]
        s16 = [s[r].astype(BF16) for r in range(DN_STREAMS)]
        v_new = [(u[i] - jnp.dot(w16[i], s16[items[i][0]], preferred_element_type=F32)).astype(BF16) for i in idx]
        o = [jnp.dot(q_dec[i], s16[items[i][0]], preferred_element_type=F32)
             + jnp.dot(attn[i], vn, preferred_element_type=F32) for i, vn in zip(idx, v_new)]
        s = [s[items[i][0]] * blk_dec[i] + _tn_dot(k_dec[i], vn) for i, vn in zip(idx, v_new)]
        for i, oi in zip(idx, o):
            _, f, h, _, bk = items[i]
            (of_ref if f else ob_ref)[rows(bk), hs(h)] = oi.astype(of_ref.dtype)
    for r in range(DN_STREAMS):
        s_scr[r] = s[r]

    @pl.when(n == pl.num_programs(1) - 1)
    def _():
        sf_ref[...] = s_scr[...]


def dn_scan(q, k, v, gb, gam, gamt, s0):
    b, l, w = q.shape
    subs = min(DN_SUBS, l // DN_BLOCK)
    c = DN_BLOCK * subs
    n = l // c
    lanes = gb.shape[-1]
    f3 = lambda bi, ni: (bi, ni, 0)
    b3 = lambda bi, ni: (bi, n - 1 - ni, 0)
    f4 = lambda bi, ni: (bi, ni, 0, 0)
    b4 = lambda bi, ni: (bi, n - 1 - ni, 0, 0)
    st = lambda bi, ni: (bi, 0, 0, 0)

    def specs(m3, m4):
        return [pl.BlockSpec((None, c, w), m3)] * 3 + [pl.BlockSpec((None, c, lanes), m3)] * 2 + [
            pl.BlockSpec((None, subs, DN_STREAMS, DN_BLOCK), m4)]

    return pl.pallas_call(
        _dn_scan_body,
        grid=(b, n),
        in_specs=specs(f3, f4) + specs(b3, b4) + [pl.BlockSpec((None, DN_STREAMS, DN_DIM, DN_DIM), st)],
        out_specs=[pl.BlockSpec((None, c, w), f3), pl.BlockSpec((None, c, w), b3),
                   pl.BlockSpec((None, DN_STREAMS, DN_DIM, DN_DIM), st)],
        out_shape=[jax.ShapeDtypeStruct((b, l, w), BF16), jax.ShapeDtypeStruct((b, l, w), BF16),
                   jax.ShapeDtypeStruct((b, DN_STREAMS, DN_DIM, DN_DIM), F32)],
        scratch_shapes=[pltpu.VMEM((DN_STREAMS, DN_DIM, DN_DIM), F32)],
        compiler_params=_cparams(("parallel", "arbitrary")),
        name="dn_scan",
    )(q, k, v, gb, gam, gamt, q, k, v, gb, gam, gamt, s0)


def _dn_final_body(of_ref, ob_ref, g_ref, w_ref, y_ref):
    for h in range(DN_HEADS):
        sl = slice(h * DN_DIM, (h + 1) * DN_DIM)
        o = of_ref[:, sl].astype(F32) + ob_ref[:, sl].astype(F32)
        y = _head_rms(o, w_ref[...])
        g = g_ref[:, sl]
        y_ref[:, sl] = (y * (g * jax.nn.sigmoid(g))).astype(y_ref.dtype)


def dn_finalize(o_f, o_b, p, gate_col, norm_w, tl):
    b, l, w = o_f.shape
    tok = lambda bi, i: (bi, i, 0)
    return pl.pallas_call(
        _dn_final_body,
        grid=(b, l // tl),
        in_specs=[pl.BlockSpec((None, tl, w), tok), pl.BlockSpec((None, tl, w), tok),
                  pl.BlockSpec((None, tl, w), lambda bi, i: (bi, i, gate_col // w)),
                  pl.BlockSpec((1, DN_DIM), lambda bi, i: (0, 0))],
        out_specs=pl.BlockSpec((None, tl, w), tok),
        out_shape=jax.ShapeDtypeStruct((b, l, w), BF16),
        compiler_params=_cparams(("parallel", "parallel")),
        name="dn_finalize",
    )(o_f, o_b, p, norm_w.reshape(1, DN_DIM))


def deltanet_pallas(p_lat, p_ctx, conv_w, a_log, dt_bias, norm_w, need_ctx, ab_col=3072, gate_col=1536):
    b = p_lat.shape[0]
    n_ctx = p_ctx.shape[1]
    s0 = jnp.zeros((b, DN_STREAMS, DN_DIM, DN_DIM), F32)
    fc = dn_prep(p_ctx, ab_col, conv_w, a_log, dt_bias, n_ctx)
    oc_f, oc_b, s_ctx = dn_scan(*fc, s0)
    fl = dn_prep(p_lat, ab_col, conv_w, a_log, dt_bias, PREP_TILE)
    ol_f, ol_b, _ = dn_scan(*fl, s_ctx)
    out_lat = dn_finalize(ol_f, ol_b, p_lat, gate_col, norm_w, PREP_TILE)
    out_ctx = dn_finalize(oc_f, oc_b, p_ctx, gate_col, norm_w, n_ctx) if need_ctx else None
    return out_lat, out_ctx


def _hy_factors(n):
    n2 = 256 if n >= 32768 else 128
    return n // n2, n2


def _bf16_const(m):
    return m.astype(ml_dtypes.bfloat16)


def _hy_constants(l):
    n = 2 * l
    n1, n2 = _hy_factors(n)
    a1 = 2.0 * np.pi * np.outer(np.arange(n1), np.arange(n1)) / n1
    c1, s1 = np.cos(a1), np.sin(a1)
    h = n1 // 2
    rows_fwd = np.block([[c1[:, :h], s1[:, :h]], [-s1[:, :h], c1[:, :h]]])
    rows_taps = np.concatenate([c1, -s1], axis=0)
    rows_inv = np.block([[c1[:h], -s1[:h]], [s1[:h], c1[:h]]]) / n
    a2 = 2.0 * np.pi * np.outer(np.arange(n2), np.arange(n2)) / n2
    c2, s2 = np.cos(a2), np.sin(a2)
    slab_fwd = np.block([[c2, s2], [-s2, c2]])
    slab_inv = np.block([[c2, -s2], [s2, c2]])
    th = 2.0 * np.pi * np.outer(np.arange(n1), np.arange(n2)) / n
    lanes = lambda t: t[:, :, None].astype(np.float32)
    return dict(n1=n1, n2=n2, rows_fwd=_bf16_const(rows_fwd), rows_taps=_bf16_const(rows_taps),
                rows_inv=_bf16_const(rows_inv), slab_fwd=_bf16_const(slab_fwd), slab_inv=_bf16_const(slab_inv),
                tw_cos=lanes(np.cos(th)), tw_sin=lanes(np.sin(th)))


def _hy_prep_body(x_ref, xp_ref, xn_ref, w_ref, v_ref, x1_ref, x2_ref):
    y = _conv3(x_ref, xp_ref, xn_ref, w_ref)
    v_ref[...] = y[:, :HY_CH]
    x1_ref[...] = y[:, HY_CH:2 * HY_CH]
    x2_ref[...] = y[:, 2 * HY_CH:]


def hy_prep(p, col0, conv_w, tl):
    b, l, _ = p.shape
    c3 = 3 * HY_CH
    nblk8 = l // SUBLANES
    cb = col0 // c3
    tok = lambda bi, i: (bi, i, 0)
    return pl.pallas_call(
        _hy_prep_body,
        grid=(b, l // tl),
        in_specs=[pl.BlockSpec((None, tl, c3), lambda bi, i: (bi, i, cb)),
                  pl.BlockSpec((None, SUBLANES, c3),
                               lambda bi, i: (bi, jnp.maximum(i * (tl // SUBLANES) - 1, 0), cb)),
                  pl.BlockSpec((None, SUBLANES, c3),
                               lambda bi, i: (bi, jnp.minimum((i + 1) * (tl // SUBLANES), nblk8 - 1), cb)),
                  pl.BlockSpec((3, c3), lambda bi, i: (0, 0))],
        out_specs=[pl.BlockSpec((None, tl, HY_CH), tok)] * 3,
        out_shape=[jax.ShapeDtypeStruct((b, l, HY_CH), F32)] * 3,
        compiler_params=_cparams(("parallel", "parallel")),
        name="hy_prep",
    )(p, p, p, conv_w)


def _split_bf16(a):
    hi = a.astype(BF16)
    return hi, (a - hi.astype(F32)).astype(BF16)


HY_FILTER_GROUP = 8


def _hy_filter_body(t_ref, zt_ref, w1t_ref, b1_ref, w2t_ref, b2_ref, w3_ref, rates_ref, m_ref, *out_refs):
    a_refs, sum_ref = out_refs[:-1], out_refs[-1]
    i = pl.program_id(0)
    hp = lax.Precision.HIGHEST
    h = jnp.sin(HY_SIN_FREQ * (jnp.dot(w1t_ref[...], zt_ref[...], preferred_element_type=F32, precision=hp)
                               + b1_ref[...]))
    h = jnp.sin(HY_SIN_FREQ * (jnp.dot(w2t_ref[...], h, preferred_element_type=F32, precision=hp) + b2_ref[...]))
    half = h.shape[1] // 2
    taps = []
    for d in range(2):
        h_hi, h_lo = _split_bf16(h[:, d * half:(d + 1) * half])
        w_hi, w_lo = _split_bf16(w3_ref[d])
        td = _tn_dot(h_hi, w_hi) + (_tn_dot(h_hi, w_lo) + _tn_dot(h_lo, w_hi))
        taps.append(td * jnp.exp(-t_ref[d * half:(d + 1) * half, :] * rates_ref[...]))

    @pl.when(i == 0)
    def _():
        sum_ref[...] = jnp.zeros_like(sum_ref)

    sum_ref[...] += (jnp.sum(jnp.abs(taps[0]), axis=0, keepdims=True)
                     + jnp.sum(jnp.abs(taps[1]), axis=0, keepdims=True))
    row = lax.broadcasted_iota(jnp.int32, (half, 1), 0)
    taps[1] = jnp.where((row == 0) & (i == 0), 0.0, taps[1])
    nh = half // HY_FILTER_GROUP
    for j in range(HY_FILTER_GROUP):
        for o in range(HY_ORDER):
            cs = slice(o * HY_CH, (o + 1) * HY_CH)
            x = jnp.concatenate([taps[0][j * nh:(j + 1) * nh, cs], taps[1][j * nh:(j + 1) * nh, cs]], axis=0)
            y = _dot16(m_ref[...], x)
            a_refs[o][0, :, j, :] = y[:2 * nh]
            a_refs[o][1, :, j, :] = y[2 * nh:]


def hy_filter_rows(l, consts, w1, b1, w2, b2, w3):
    n = 2 * l
    n1, n2, g = consts["n1"], consts["n2"], HY_FILTER_GROUP
    tr = g * n1
    grp, d, j, m = np.meshgrid(np.arange(n2 // g), np.arange(2), np.arange(g), np.arange(n1 // 2), indexing="ij")
    r = (n2 * (m + (n1 // 2) * d) + g * grp + j).reshape(-1)
    pos = np.where(r < l, r, n - r).astype(np.float64)
    pos[r == l] = 0.0
    t = pos / max(l - 1, 1)
    bands = np.linspace(1e-4, HY_BANDS - 1, HY_BANDS)
    phase = (2.0 * math.pi / l) * pos[:, None] * bands[None, :]
    kf = 32
    z = np.concatenate([t[:, None], np.cos(phase), -np.sin(phase)], axis=-1)
    zt = np.pad(z, ((0, 0), (0, kf - z.shape[1]))).T.astype(np.float32)
    t_col = t[:, None].astype(np.float32)
    w1t = jnp.pad(w1.astype(F32), ((0, kf - w1.shape[0]), (0, 0))).T
    hid = w1.shape[1]
    w3d = w3.astype(F32).reshape(hid, HY_ORDER, 2, HY_CH).transpose(2, 0, 1, 3).reshape(2, hid, HY_ORDER * HY_CH)
    rates = jnp.abs(jnp.linspace(math.log(HY_TARGET) / HY_DECAY_LONG, math.log(HY_TARGET) / HY_DECAY_SHORT,
                                 HY_CH, dtype=F32))
    rates = jnp.tile(rates, HY_ORDER).reshape(1, HY_ORDER * HY_CH)
    fixed = lambda i: (0, 0)
    mat = consts["rows_taps"]
    return pl.pallas_call(
        _hy_filter_body,
        grid=(n // tr,),
        in_specs=[pl.BlockSpec((tr, 1), lambda i: (i, 0)), pl.BlockSpec((kf, tr), lambda i: (0, i)),
                  pl.BlockSpec((hid, kf), fixed), pl.BlockSpec((hid, 1), fixed),
                  pl.BlockSpec((hid, hid), fixed), pl.BlockSpec((hid, 1), fixed),
                  pl.BlockSpec((2, hid, HY_ORDER * HY_CH), lambda i: (0, 0, 0)),
                  pl.BlockSpec((1, HY_ORDER * HY_CH), fixed), pl.BlockSpec(mat.shape, fixed)],
        out_specs=[pl.BlockSpec((2, n1, g, HY_CH), lambda i: (0, 0, i, 0))] * HY_ORDER
        + [pl.BlockSpec((1, HY_ORDER * HY_CH), fixed)],
        out_shape=[jax.ShapeDtypeStruct((2, n1, n2, HY_CH), F32)] * HY_ORDER
        + [jax.ShapeDtypeStruct((1, HY_ORDER * HY_CH), F32)],
        compiler_params=_cparams(("arbitrary",)),
        name="hy_filter_rows",
    )(t_col, zt, w1t, b1.astype(F32).reshape(hid, 1), w2.astype(F32).T, b2.astype(F32).reshape(hid, 1), w3d, rates,
      mat)


HY_ROWS_GROUP = {"fwd": 16, "inv": 8}


def _hy_rows_body(*refs, mode):
    if mode == "fwd":
        m_ref, x_ref, o_ref = refs
    else:
        m_ref, x_ref, xn_ref, z_ref, skip_ref, o_ref = refs
    for j in range(HY_ROWS_GROUP[mode]):
        x = jnp.concatenate([x_ref[0, :, j, :], x_ref[1, :, j, :]], axis=0)
        y = _dot16(m_ref[...], x)
        half = y.shape[0] // 2
        for bi in range(2):
            yb = y[bi * half:(bi + 1) * half]
            if mode == "inv":
                yb = xn_ref[bi, :, j, :] * (yb + z_ref[bi, :, j, :] * skip_ref[...])
            o_ref[bi, :, j, :] = yb


def hy_rows_pass(mode, mat, x, extra=()):
    n2, c = x.shape[-2:]
    g = HY_ROWS_GROUP[mode]
    blk = lambda rows: pl.BlockSpec((2, rows, g, c), lambda j: (0, 0, j, 0))
    mspec = _resident(mat.shape)
    vec = pl.BlockSpec((1, c), lambda j: (0, 0))
    if mode == "fwd":
        n1 = 2 * x.shape[1]
        in_specs = [mspec, blk(n1 // 2)]
        out_rows = n1
    else:
        n1 = x.shape[1]
        in_specs = [mspec, blk(n1), blk(n1 // 2), blk(n1 // 2), vec]
        out_rows = n1 // 2
    return pl.pallas_call(
        functools.partial(_hy_rows_body, mode=mode),
        grid=(n2 // g,),
        in_specs=in_specs,
        out_specs=blk(out_rows),
        out_shape=jax.ShapeDtypeStruct((2, out_rows, n2, c), F32),
        compiler_params=_cparams(("parallel",)),
        name="hy_rows_" + mode,
    )(mat, x, *extra)


def _hy_slab_body(*refs, with_filter):
    if with_filter:
        f_ref, i_ref, tc_ref, ts_ref, a_ref, h_ref, o_ref = refs
    else:
        f_ref, tc_ref, ts_ref, a_ref, sum_ref, o_ref = refs
        scale = 1.0 / (sum_ref[...] + EPS)
    n2, c = a_ref.shape[2:]
    for s in range(a_ref.shape[1]):
        tc = jnp.broadcast_to(tc_ref[s], (n2, c))
        ts = jnp.broadcast_to(ts_ref[s], (n2, c))
        ar, ai = a_ref[0, s], a_ref[1, s]
        x = _dot16(f_ref[...], jnp.concatenate([ar * tc + ai * ts, ai * tc - ar * ts], axis=0))
        xr, xi = x[:n2], x[n2:]
        if not with_filter:
            o_ref[0, s] = (xr * scale).astype(o_ref.dtype)
            o_ref[1, s] = (xi * scale).astype(o_ref.dtype)
            continue
        hr, hi = h_ref[0, s].astype(F32), h_ref[1, s].astype(F32)
        y = _dot16(i_ref[...], jnp.concatenate([xr * hr - xi * hi, xr * hi + xi * hr], axis=0))
        yr, yi = y[:n2], y[n2:]
        o_ref[0, s] = yr * tc - yi * ts
        o_ref[1, s] = yi * tc + yr * ts


HY_SLABS_PER_STEP = 4


def hy_slab_pass(consts, a, h=None, l1_sum=None):
    _, n1, n2, c = a.shape
    kb = HY_SLABS_PER_STEP
    fwd = consts["slab_fwd"]
    mspec = _resident(fwd.shape)
    tw = pl.BlockSpec((kb, n2, 1), lambda k: (k, 0, 0))
    slab = pl.BlockSpec((2, kb, n2, c), lambda k: (0, k, 0, 0))
    if h is None:
        args = [fwd, consts["tw_cos"], consts["tw_sin"], a, l1_sum]
        in_specs = [mspec, tw, tw, slab, pl.BlockSpec((1, c), lambda k: (0, 0))]
    else:
        args = [fwd, consts["slab_inv"], consts["tw_cos"], consts["tw_sin"], a, h]
        in_specs = [mspec, mspec, tw, tw, slab, slab]
    return pl.pallas_call(
        functools.partial(_hy_slab_body, with_filter=h is not None),
        grid=(n1 // kb,),
        in_specs=in_specs,
        out_specs=slab,
        out_shape=jax.ShapeDtypeStruct(a.shape, F32 if h is not None else BF16),
        compiler_params=_cparams(("parallel",)),
        name="hy_slab_conv" if h is not None else "hy_slab_fwd",
    )(*args)


def hyena_pallas(p, conv_w, w1, b1, w2, b2, w3, skip, col0=RET_COLS):
    b, l, _ = p.shape
    assert b == 2, "the two batch rows are packed as one complex signal"
    c = HY_CH
    consts = _hy_constants(l)
    n1, n2 = consts["n1"], consts["n2"]
    *a_h, sums = hy_filter_rows(l, consts, w1, b1, w2, b2, w3)
    spectra = [hy_slab_pass(consts, a_h[o], l1_sum=sums[:, o * c:(o + 1) * c]) for o in range(HY_ORDER)]
    split = lambda t: t.reshape(2, n1 // 2, n2, c)
    v, x1, x2 = hy_prep(p, col0, conv_w, PREP_TILE)
    z = split(v)
    for o, xn in enumerate((x1, x2)):
        a = hy_rows_pass("fwd", consts["rows_fwd"], z)
        bm = hy_slab_pass(consts, a, spectra[o])
        z = hy_rows_pass("inv", consts["rows_inv"], bm,
                         extra=(split(xn), z, skip[o].astype(F32).reshape(1, c)))
    return z.reshape(b, l, c)


def _pad_cols(w, mult):
    n = w.shape[-1]
    pad = (-n) % mult
    return jnp.pad(w, ((0, 0), (0, pad))) if pad else w


def kernel(x, c, ctx, c_ctx, mod_w, mod_b, norm_mix_w, norm_ffn_w, ffn_w_in, ffn_w_out,
           ab_w_in, ab_w_out, dn_conv_w, dn_a_log, dn_dt_bias, dn_norm_w, swa_q_norm_w, swa_k_norm_w,
           swa_sink, cd_w_in, cd_w_out, ret_decay_logit, ret_gn_w, hy_conv_w, hy_f_w1, hy_f_b1,
           hy_f_w2, hy_f_b2, hy_f_w3, hy_bias):
    B, L, D = x.shape
    n_ctx = ctx.shape[1]
    rows = L // GRID_W
    c_rows = jnp.zeros((SUBLANES, D), F32).at[:B].set(c).at[B].set(c_ctx)
    mod_all = modulation(c_rows, mod_w, mod_b)
    hid = ffn_w_out.shape[1]
    h_ctx = ctx
    for layer in range(DEPTH):
        need_ctx = layer != DEPTH - 1
        i = layer // 2
        m = mod_all[layer].reshape(SUBLANES, 6, D)
        mod = [m[:B, j][:, None, :] for j in range(6)]
        mod_c = [jnp.broadcast_to(m[B, j][None, None, :], (B, 1, D)) for j in range(6)]
        if layer % 2 == 0:
            w = ab_w_in[i]
            w_in = jnp.concatenate([w[:, :4 * DN_WIDTH], w[:, DN_COLS:], w[:, 4 * DN_WIDTH:DN_COLS]], axis=1)
            w_out = ab_w_out[i]
        else:
            w = cd_w_in[i]
            w_in = jnp.concatenate([w[:, RET_COLS:], w[:, :RET_COLS]], axis=1)
            w_out = cd_w_out[i]
        w_in_p = _pad_cols(w_in, LANES).astype(BF16)
        p_lat = in_projection(x, norm_mix_w[layer], mod[0], mod[1], w_in_p, INPROJ_TILE)
        p_ctx = in_projection(h_ctx, norm_mix_w[layer], mod_c[0], mod_c[1], w_in_p, n_ctx)
        if layer % 2 == 0:
            swa0 = 4 * DN_WIDTH
            ab0 = swa0 + (SWA_HEADS + 2 * SWA_KV_HEADS) * SWA_DIM
            a_lat, a_ctx = deltanet_pallas(p_lat, p_ctx, dn_conv_w[i], dn_a_log[i], dn_dt_bias[i],
                                           dn_norm_w[i], need_ctx, ab_col=ab0, gate_col=3 * DN_WIDTH)
            b_lat, b_ctx = window_gqa_pallas(p_lat, p_ctx, swa0, axial_rope_tables(rows), swa_q_norm_w[i],
                                             swa_k_norm_w[i], swa_sink[i], need_ctx)
        else:
            a_lat, a_ctx = retention_pallas(p_lat, p_ctx, ret_decay_logit[i], ret_gn_w[i], need_ctx,
                                            col0=(HY_ORDER + 1) * HY_CH)
            assert not need_ctx, "the last layer's context outputs reach no latent token"
            b_lat = hyena_pallas(p_lat, hy_conv_w[i], hy_f_w1[i], hy_f_b1[i], hy_f_w2[i], hy_f_b2[i],
                                 hy_f_w3[i], hy_bias[i], col0=0)
            b_ctx = None
        wo = w_out.astype(BF16)
        wg = ffn_w_in[layer][:, :hid].astype(BF16)
        wu = ffn_w_in[layer][:, hid:].astype(BF16)
        wd = ffn_w_out[layer].astype(BF16)
        x = out_projection_ffn(x, a_lat, b_lat, wo, mod[2], norm_ffn_w[layer], mod[3], mod[4], mod[5],
                               wg, wu, wd, DENSE_TILE, FFN_HIDDEN_CHUNK)
        if need_ctx:
            h_ctx = out_projection_ffn(h_ctx, a_ctx, b_ctx, wo, mod_c[2], norm_ffn_w[layer], mod_c[3], mod_c[4],
                                       mod_c[5], wg, wu, wd, n_ctx, FFN_HIDDEN_CHUNK)
    return x
```

```python
import functools
import math

import jax
import jax.numpy as jnp
import ml_dtypes
import numpy as np
from jax import lax
from jax.experimental import pallas as pl
from jax.experimental.pallas import tpu as pltpu

F32 = jnp.float32
BF16 = jnp.bfloat16
EPS = 1e-6

DEPTH = 2
GRID_W = 64

DN_HEADS = 4
DN_DIM = 128
DN_WIDTH = DN_HEADS * DN_DIM
DN_COLS = 4 * DN_WIDTH + 4 * DN_HEADS
SWA_HEADS = 4
SWA_KV_HEADS = 2
SWA_DIM = 128
SWA_BLOCK = 128
ROPE_THETA = 10000.0
RET_HEADS = 4
RET_DIM = 128
RET_WIDTH = RET_HEADS * RET_DIM
RET_CHUNK = 128
RET_COLS = 4 * RET_WIDTH
HY_CH = 512
HY_ORDER = 2
HY_BANDS = 8
HY_SIN_FREQ = 1.0
HY_TARGET = 1e-2
HY_DECAY_SHORT = 0.3
HY_DECAY_LONG = 1.5

VMEM_LIMIT_BYTES = 56 * 1024 * 1024
SUBLANES, LANES = 8, 128

DENSE_TILE = 512
INPROJ_TILE = 1024
FFN_HIDDEN_CHUNK = 256
PREP_TILE = 1024
SWA_BLOCKS_PER_STEP = 4
MOD_COL_TILE = 1536


def _cparams(sem):
    return pltpu.CompilerParams(dimension_semantics=sem, vmem_limit_bytes=VMEM_LIMIT_BYTES)


def _mod_body(c_ref, w_ref, b_ref, o_ref):
    a = c_ref[...]
    a = a * jax.nn.sigmoid(a)
    o_ref[...] = jnp.dot(a, w_ref[...], preferred_element_type=F32,
                         precision=lax.Precision.HIGHEST) + b_ref[...]


def modulation(c_rows, mod_w, mod_b):
    depth, d, n = mod_w.shape
    tn = MOD_COL_TILE
    return pl.pallas_call(
        _mod_body,
        grid=(depth, n // tn),
        in_specs=[pl.BlockSpec((8, d), lambda l, j: (0, 0)),
                  pl.BlockSpec((None, d, tn), lambda l, j: (l, 0, j)),
                  pl.BlockSpec((None, 1, tn), lambda l, j: (l, 0, j))],
        out_specs=pl.BlockSpec((None, 8, tn), lambda l, j: (l, 0, j)),
        out_shape=jax.ShapeDtypeStruct((depth, 8, n), F32),
        compiler_params=_cparams(("parallel", "parallel")),
        name="modulation",
    )(c_rows, mod_w, mod_b.reshape(depth, 1, n))


def _norm_mod(x, nw, shift, scale):
    y = x * lax.rsqrt(jnp.mean(x * x, axis=-1, keepdims=True) + EPS)
    return (y * nw) * (1.0 + scale) + shift


def _resident(shape):
    return pl.BlockSpec(shape, lambda *_: (0,) * len(shape), pipeline_mode=pl.Buffered(1))


def _inproj_body(x_ref, nw_ref, shift_ref, scale_ref, w_ref, o_ref):
    h = _norm_mod(x_ref[...], nw_ref[...], shift_ref[...], scale_ref[...]).astype(BF16)
    o_ref[...] = jnp.dot(h, w_ref[...], preferred_element_type=F32)


def in_projection(x, nw, shift, scale, w, tm):
    b, l, d = x.shape
    n = w.shape[1]
    return pl.pallas_call(
        _inproj_body,
        grid=(b, l // tm),
        in_specs=[pl.BlockSpec((None, tm, d), lambda bi, i: (bi, i, 0)),
                  _resident((1, d)),
                  pl.BlockSpec((None, 1, d), lambda bi, i: (bi, 0, 0)),
                  pl.BlockSpec((None, 1, d), lambda bi, i: (bi, 0, 0)),
                  _resident((d, n))],
        out_specs=pl.BlockSpec((None, tm, n), lambda bi, i: (bi, i, 0)),
        out_shape=jax.ShapeDtypeStruct((b, l, n), F32),
        compiler_params=_cparams(("parallel", "parallel")),
        name="in_projection",
    )(x, nw.reshape(1, d), shift, scale, w)


def _outffn_body(x_ref, oa_ref, ob_ref, wo_ref, g2_ref, nw_ref, sh_ref, sc_ref, g5_ref,
                 wg_ref, wu_ref, wd_ref, y_ref, x1_scr, h_scr, acc_scr, *, th):
    wa = oa_ref.shape[1]
    mix = (jnp.dot(oa_ref[...].astype(BF16), wo_ref[:wa, :], preferred_element_type=F32)
           + jnp.dot(ob_ref[...].astype(BF16), wo_ref[wa:, :], preferred_element_type=F32))
    x1 = x_ref[...] + g2_ref[...] * mix
    x1_scr[...] = x1
    h_scr[...] = _norm_mod(x1, nw_ref[...], sh_ref[...], sc_ref[...]).astype(BF16)
    for k in range(wg_ref.shape[1] // th):
        ks = slice(k * th, (k + 1) * th)
        g = jnp.dot(h_scr[...], wg_ref[:, ks], preferred_element_type=F32)
        u = jnp.dot(h_scr[...], wu_ref[:, ks], preferred_element_type=F32)
        a = (g * jax.nn.sigmoid(g) * u).astype(BF16)
        part = jnp.dot(a, wd_ref[ks, :], preferred_element_type=F32)
        if k == 0:
            acc_scr[...] = part
        else:
            acc_scr[...] += part
    y_ref[...] = x1_scr[...] + g5_ref[...] * acc_scr[...]


def out_projection_ffn(x, oa, ob, wo, g2, nw, shift, scale, g5, wg, wu, wd, tm, th):
    b, l, d = x.shape
    wa, wb = oa.shape[-1], ob.shape[-1]
    hid = wg.shape[1]
    tok = lambda bi, i: (bi, i, 0)
    vec = pl.BlockSpec((None, 1, d), lambda bi, i: (bi, 0, 0))
    return pl.pallas_call(
        functools.partial(_outffn_body, th=th),
        grid=(b, l // tm),
        in_specs=[pl.BlockSpec((None, tm, d), tok),
                  pl.BlockSpec((None, tm, wa), tok),
                  pl.BlockSpec((None, tm, wb), tok),
                  _resident((d, d)), vec, _resident((1, d)), vec, vec, vec,
                  _resident((d, hid)), _resident((d, hid)), _resident((hid, d))],
        out_specs=pl.BlockSpec((None, tm, d), tok),
        out_shape=jax.ShapeDtypeStruct((b, l, d), F32),
        scratch_shapes=[pltpu.VMEM((tm, d), F32), pltpu.VMEM((tm, d), BF16), pltpu.VMEM((tm, d), F32)],
        compiler_params=_cparams(("parallel", "parallel")),
        name="out_projection_ffn",
    )(x, oa, ob, wo, g2, nw.reshape(1, d), shift, scale, g5, wg, wu, wd)


NEG_BIG = -1e30


def _head_rms(x, w):
    return x * lax.rsqrt(jnp.mean(x * x, axis=-1, keepdims=True) + EPS) * w


def _swa_prep_body(*refs, use_rope):
    if use_rope:
        q_ref, k_ref, v_ref, qw_ref, kw_ref, rt_ref, ct_ref, qo_ref, ko_ref, vo_ref = refs
        tl = q_ref.shape[0]
        nr = tl // GRID_W

        def table(kind):
            rowp = jnp.broadcast_to(rt_ref[kind][:, None, :], (nr, GRID_W, SWA_DIM))
            colp = jnp.broadcast_to(ct_ref[kind][None, :, :], (nr, GRID_W, SWA_DIM))
            return (rowp + colp).reshape(tl, SWA_DIM)

        cos, sa, sb = table(0), table(1), table(2)
    else:
        q_ref, k_ref, v_ref, qw_ref, kw_ref, qo_ref, ko_ref, vo_ref = refs

    def prep(x, w):
        y = _head_rms(x, w)
        if use_rope:
            y = y * cos + pltpu.roll(y, SWA_DIM - 32, 1) * sa + pltpu.roll(y, 32, 1) * sb
        return y.astype(BF16)

    for h in range(SWA_HEADS):
        sl = slice(h * SWA_DIM, (h + 1) * SWA_DIM)
        qo_ref[:, sl] = prep(q_ref[:, sl], qw_ref[...])
    for h in range(SWA_KV_HEADS):
        sl = slice(h * SWA_DIM, (h + 1) * SWA_DIM)
        ko_ref[:, sl] = prep(k_ref[:, sl], kw_ref[...])
    vo_ref[...] = v_ref[...].astype(BF16)


def swa_prep(p, col0, qw, kw, rope, tl):
    b, l, _ = p.shape
    nq, nk = SWA_HEADS * SWA_DIM, SWA_KV_HEADS * SWA_DIM
    tok = lambda bi, i: (bi, i, 0)
    in_specs = [pl.BlockSpec((None, tl, nq), lambda bi, i: (bi, i, col0 // nq)),
                pl.BlockSpec((None, tl, nk), lambda bi, i: (bi, i, (col0 + nq) // nk)),
                pl.BlockSpec((None, tl, nk), lambda bi, i: (bi, i, (col0 + nq + nk) // nk)),
                pl.BlockSpec((1, SWA_DIM), lambda bi, i: (0, 0)),
                pl.BlockSpec((1, SWA_DIM), lambda bi, i: (0, 0))]
    args = [p, p, p, qw.reshape(1, SWA_DIM), kw.reshape(1, SWA_DIM)]
    if rope is not None:
        in_specs += [pl.BlockSpec((3, tl // GRID_W, SWA_DIM), lambda bi, i: (0, i, 0)),
                     pl.BlockSpec((3, GRID_W, SWA_DIM), lambda bi, i: (0, 0, 0))]
        args += list(rope)
    return pl.pallas_call(
        functools.partial(_swa_prep_body, use_rope=rope is not None),
        grid=(b, l // tl),
        in_specs=in_specs,
        out_specs=[pl.BlockSpec((None, tl, nq), tok), pl.BlockSpec((None, tl, nk), tok),
                   pl.BlockSpec((None, tl, nk), tok)],
        out_shape=[jax.ShapeDtypeStruct((b, l, nq), BF16), jax.ShapeDtypeStruct((b, l, nk), BF16),
                   jax.ShapeDtypeStruct((b, l, nk), BF16)],
        compiler_params=_cparams(("parallel", "parallel")),
        name="swa_prep",
    )(*args)


def _nt_dot(a, b):
    return lax.dot_general(a, b, (((1,), (1,)), ((), ())), preferred_element_type=F32)


def _swa_attend(problems):
    def lane_tiles(x):
        return [x[:, c * LANES:(c + 1) * LANES] for c in range(x.shape[1] // LANES)]

    def masked_scores(q, keys, masks):
        out = []
        for kk, mask in zip(keys, masks):
            s = _nt_dot(q, kk)
            out.append(s if mask is None else jnp.where(mask, s, NEG_BIG))
        return out

    def row_max(scores, sink_col):
        m = sink_col + jnp.zeros((1, LANES), F32)
        for s in scores:
            for tile in lane_tiles(s):
                m = jnp.maximum(m, tile)
        return jnp.max(m, axis=-1, keepdims=True)

    def weighted(scores, vals, m, sink_col):
        part, acc = None, None
        for s, vv in zip(scores, vals):
            pr = jnp.exp(s - m)
            for tile in lane_tiles(pr):
                part = tile if part is None else part + tile
            o = jnp.dot(pr.astype(BF16), vv, preferred_element_type=F32)
            acc = o if acc is None else acc + o
        return acc, jnp.exp(sink_col - m) + jnp.sum(part, axis=-1, keepdims=True)

    scores = [masked_scores(q, keys, masks) for q, keys, _, masks, _ in problems]
    maxes = [row_max(s, p[4]) for s, p in zip(scores, problems)]
    outs = [weighted(s, p[2], m, p[4]) for s, p, m in zip(scores, problems, maxes)]
    return [acc / den for acc, den in outs]


def _swa_attn_body(sink_ref, q_ref, kp_ref, km_ref, kn_ref, vp_ref, vm_ref, vn_ref, kc_ref, vc_ref,
                   o_ref, *, nb):
    i = pl.program_id(1)
    n_blocks = pl.num_programs(1) * nb
    w, d = SWA_BLOCK, SWA_DIM
    grp = SWA_HEADS // SWA_KV_HEADS
    rows = lax.broadcasted_iota(jnp.int32, (grp * w, w), 0)
    qi = rows % w
    kj = lax.broadcasted_iota(jnp.int32, (grp * w, w), 1)
    sink_cols = []
    for h in range(SWA_KV_HEADS):
        sink_col = jnp.zeros((grp * w, 1), F32)
        for g in range(grp):
            sink_col = jnp.where((rows[:, :1] // w) == g, sink_ref[h * grp + g], sink_col)
        sink_cols.append(sink_col)
    problems, places = [], []
    for r in range(nb):
        blk = i * nb + r
        mask_prev = (kj >= qi) & (blk > 0)
        mask_next = (kj <= qi) & (blk < n_blocks - 1)
        rs = slice(r * w, (r + 1) * w)
        for h in range(SWA_KV_HEADS):
            hs = slice(h * d, (h + 1) * d)
            q = jnp.concatenate([q_ref[rs, (h * grp + g) * d:(h * grp + g + 1) * d] for g in range(grp)], axis=0)
            if r > 0:
                k_prev, v_prev = km_ref[(r - 1) * w:r * w, hs], vm_ref[(r - 1) * w:r * w, hs]
            else:
                k_prev, v_prev = kp_ref[:, hs], vp_ref[:, hs]
            if r < nb - 1:
                k_next, v_next = km_ref[(r + 1) * w:(r + 2) * w, hs], vm_ref[(r + 1) * w:(r + 2) * w, hs]
            else:
                k_next, v_next = kn_ref[:, hs], vn_ref[:, hs]
            problems.append((q, [k_prev, km_ref[rs, hs], k_next, kc_ref[:, hs]],
                             [v_prev, vm_ref[rs, hs], v_next, vc_ref[:, hs]],
                             [mask_prev, None, mask_next, None], sink_cols[h]))
            places.append((rs, h))
    for (rs, h), o in zip(places, _swa_attend(problems)):
        for g in range(grp):
            o_ref[rs, (h * grp + g) * d:(h * grp + g + 1) * d] = o[g * w:(g + 1) * w].astype(o_ref.dtype)


def swa_attention(q, k, v, kc, vc, sink, nb):
    b, l, nq = q.shape
    nk = k.shape[-1]
    n_ctx = kc.shape[1]
    w = SWA_BLOCK
    tq = nb * w
    last = l // w - 1
    main = lambda bi, i, s: (bi, i, 0)
    prev = lambda bi, i, s: (bi, jnp.maximum(i * nb - 1, 0), 0)
    nxt = lambda bi, i, s: (bi, jnp.minimum((i + 1) * nb, last), 0)
    cx = lambda bi, i, s: (bi, 0, 0)
    return pl.pallas_call(
        functools.partial(_swa_attn_body, nb=nb),
        grid_spec=pltpu.PrefetchScalarGridSpec(
            num_scalar_prefetch=1,
            grid=(b, l // tq),
            in_specs=[pl.BlockSpec((None, tq, nq), main),
                      pl.BlockSpec((None, w, nk), prev), pl.BlockSpec((None, tq, nk), main),
                      pl.BlockSpec((None, w, nk), nxt),
                      pl.BlockSpec((None, w, nk), prev), pl.BlockSpec((None, tq, nk), main),
                      pl.BlockSpec((None, w, nk), nxt),
                      pl.BlockSpec((None, n_ctx, nk), cx), pl.BlockSpec((None, n_ctx, nk), cx)],
            out_specs=pl.BlockSpec((None, tq, nq), main)),
        out_shape=jax.ShapeDtypeStruct((b, l, nq), BF16),
        compiler_params=_cparams(("parallel", "parallel")),
        name="swa_attention",
    )(sink, q, k, k, k, v, v, v, kc, vc)


def _ctx_attn_body(sink_ref, q_ref, kc_ref, vc_ref, o_ref):
    n_ctx, d = q_ref.shape[0], SWA_DIM
    grp = SWA_HEADS // SWA_KV_HEADS
    rows = lax.broadcasted_iota(jnp.int32, (grp * n_ctx, 1), 0)
    problems = []
    for h in range(SWA_KV_HEADS):
        hs = slice(h * d, (h + 1) * d)
        q = jnp.concatenate([q_ref[:, (h * grp + g) * d:(h * grp + g + 1) * d] for g in range(grp)], axis=0)
        sink_col = jnp.zeros((grp * n_ctx, 1), F32)
        for g in range(grp):
            sink_col = jnp.where((rows // n_ctx) == g, sink_ref[h * grp + g], sink_col)
        problems.append((q, [kc_ref[:, hs]], [vc_ref[:, hs]], [None], sink_col))
    for h, o in enumerate(_swa_attend(problems)):
        for g in range(grp):
            o_ref[:, (h * grp + g) * d:(h * grp + g + 1) * d] = o[g * n_ctx:(g + 1) * n_ctx].astype(o_ref.dtype)


def ctx_attention(qc, kc, vc, sink):
    b, n_ctx, nq = qc.shape
    nk = kc.shape[-1]
    cx = lambda bi, s: (bi, 0, 0)
    return pl.pallas_call(
        _ctx_attn_body,
        grid_spec=pltpu.PrefetchScalarGridSpec(
            num_scalar_prefetch=1, grid=(b,),
            in_specs=[pl.BlockSpec((None, n_ctx, nq), cx), pl.BlockSpec((None, n_ctx, nk), cx),
                      pl.BlockSpec((None, n_ctx, nk), cx)],
            out_specs=pl.BlockSpec((None, n_ctx, nq), cx)),
        out_shape=jax.ShapeDtypeStruct((b, n_ctx, nq), BF16),
        compiler_params=_cparams(("parallel",)),
        name="ctx_attention",
    )(sink, qc, kc, vc)


def axial_rope_tables(rows):
    m = SWA_DIM // 4
    inv = ROPE_THETA ** (-np.arange(m, dtype=np.float64) / m)

    def tables(count, lane0):
        ang = np.arange(count, dtype=np.float64)[:, None] * inv
        out = np.zeros((3, count, SWA_DIM))
        out[0, :, lane0:lane0 + m] = out[0, :, lane0 + m:lane0 + 2 * m] = np.cos(ang)
        out[1, :, lane0:lane0 + m] = -np.sin(ang)
        out[2, :, lane0 + m:lane0 + 2 * m] = np.sin(ang)
        return out.astype(np.float32)

    return tables(rows, 0), tables(GRID_W, SWA_DIM // 2)


def window_gqa_pallas(p_lat, p_ctx, col0, rope, q_norm_w, k_norm_w, sink, need_ctx):
    qw = q_norm_w.astype(F32) * SWA_DIM ** -0.5
    ql, kl, vl = swa_prep(p_lat, col0, qw, k_norm_w, rope, PREP_TILE)
    qc, kc, vc = swa_prep(p_ctx, col0, qw, k_norm_w, None, p_ctx.shape[1])
    o_lat = swa_attention(ql, kl, vl, kc, vc, sink, SWA_BLOCKS_PER_STEP)
    o_ctx = ctx_attention(qc, kc, vc, sink) if need_ctx else None
    return o_lat, o_ctx


def _tn_dot(a, b):
    return lax.dot_general(a, b, (((0,), (0,)), ((), ())), preferred_element_type=F32)


def _log_sigmoid(x):
    return jnp.minimum(x, 0.0) - jnp.log(1.0 + jnp.exp(-jnp.abs(x)))


RET_STREAMS = 2 * RET_HEADS


RET_SUBS = 4


def _ret_scan_body(logit_ref, qf_ref, kf_ref, vf_ref, rotf_ref, qb_ref, kb_ref, vb_ref, rotb_ref, rotc_ref,
                   s0_ref, of_ref, ob_ref, sf_ref, s_scr):
    n = pl.program_id(1)
    c, dh = RET_CHUNK, RET_DIM

    @pl.when(n == 0)
    def _():
        s_scr[...] = s0_ref[...]

    ii = lax.broadcasted_iota(jnp.int32, (c, c), 0)
    jj = lax.broadcasted_iota(jnp.int32, (c, c), 1)
    i1 = lax.broadcasted_iota(jnp.int32, (c, 1), 0)
    rel = {True: (ii - jj).astype(F32), False: (jj - ii).astype(F32)}
    pos = {True: i1.astype(F32), False: (c - 1 - i1).astype(F32)}
    streams = [(d == 0, h) for d in range(2) for h in range(RET_HEADS)]
    srcs = {True: (qf_ref, kf_ref, vf_ref, rotf_ref), False: (qb_ref, kb_ref, vb_ref, rotb_ref)}
    subs = rotf_ref.shape[0]
    items = [(r, f, h, j, (j if f else subs - 1 - j)) for j in range(subs) for r, (f, h) in enumerate(streams)]
    rows = lambda blk: slice(blk * c, (blk + 1) * c)
    hs = lambda h: slice(h * dh, (h + 1) * dh)
    rng = range(len(items))

    cos_i, sin_i = rotc_ref[0], rotc_ref[1]
    tables = {}
    for f in (True, False):
        for bk in range(subs):
            a = srcs[f][3][bk]
            tables[f, bk] = (a[0:1] * cos_i - a[1:2] * sin_i, a[2:3] * cos_i + a[3:4] * sin_i)

    def rot(x, f, bk):
        cos_t, sin_t = tables[f, bk]
        return x * cos_t + pltpu.roll(x, dh // 2, 1) * sin_t

    lg = [_log_sigmoid(jnp.full((1, 1), logit_ref[r], F32)) for r in range(RET_STREAMS)]
    q = [rot(srcs[f][0][rows(bk), hs(h)], f, bk) for _, f, h, _, bk in items]
    k = [rot(srcs[f][1][rows(bk), hs(h)], f, bk) * dh ** -0.5 for _, f, h, _, bk in items]
    v = [srcs[f][2][rows(bk), hs(h)].astype(BF16) for _, f, h, _, bk in items]
    dmat = [jnp.where(rel[f] >= 0, jnp.exp(lg[r] * jnp.maximum(rel[f], 0.0)), 0.0)
            for r, (f, _) in enumerate(streams)]
    scores = [(_nt_dot(q[i].astype(BF16), k[i].astype(BF16)) * dmat[items[i][0]]).astype(BF16) for i in rng]
    o_in = [jnp.dot(scores[i], v[i], preferred_element_type=F32) for i in rng]
    q_dec = [(q[i] * jnp.exp(lg[items[i][0]] * (pos[items[i][1]] + 1.0))).astype(BF16) for i in rng]
    k_dec = [(k[i] * jnp.exp(lg[items[i][0]] * (c - 1.0 - pos[items[i][1]]))).astype(BF16) for i in rng]
    kv = [_tn_dot(k_dec[i], v[i]) for i in rng]
    chunk_dec = [jnp.exp(lg[r] * c) for r in range(RET_STREAMS)]

    s = [s_scr[r] for r in range(RET_STREAMS)]
    for j in range(subs):
        idx = [i for i in rng if items[i][3] == j]
        o = [o_in[i] + jnp.dot(q_dec[i], s[items[i][0]].astype(BF16), preferred_element_type=F32) for i in idx]
        s = [s[items[i][0]] * chunk_dec[items[i][0]] + kv[i] for i in idx]
        for i, oi in zip(idx, o):
            _, f, h, _, bk = items[i]
            (of_ref if f else ob_ref)[rows(bk), hs(h)] = oi.astype(of_ref.dtype)
    for r in range(RET_STREAMS):
        s_scr[r] = s[r]

    @pl.when(n == pl.num_programs(1) - 1)
    def _():
        sf_ref[...] = s_scr[...]


def retention_scan(p, col0, decay_logit, rot_chunk, rot_in, s0):
    b, l, _ = p.shape
    subs = min(RET_SUBS, l // RET_CHUNK)
    c, dh, w = RET_CHUNK * subs, RET_DIM, RET_WIDTH
    n = l // c
    cb = col0 // w
    fcol = lambda off: (lambda bi, ni, s: (bi, ni, cb + off))
    bcol = lambda off: (lambda bi, ni, s: (bi, n - 1 - ni, cb + off))
    rot = lambda m: pl.BlockSpec((subs, 4, dh), m)
    st = lambda bi, ni, s: (bi, 0, 0, 0)
    tok = lambda col: [pl.BlockSpec((None, c, w), col(j)) for j in range(3)]
    return pl.pallas_call(
        _ret_scan_body,
        grid_spec=pltpu.PrefetchScalarGridSpec(
            num_scalar_prefetch=1,
            grid=(b, n),
            in_specs=tok(fcol) + [rot(lambda bi, ni, s: (ni, 0, 0))]
            + tok(bcol) + [rot(lambda bi, ni, s: (n - 1 - ni, 0, 0))]
            + [pl.BlockSpec((2, RET_CHUNK, dh), lambda bi, ni, s: (0, 0, 0)),
               pl.BlockSpec((None, RET_STREAMS, dh, dh), st)],
            out_specs=[pl.BlockSpec((None, c, w), lambda bi, ni, s: (bi, ni, 0)),
                       pl.BlockSpec((None, c, w), lambda bi, ni, s: (bi, n - 1 - ni, 0)),
                       pl.BlockSpec((None, RET_STREAMS, dh, dh), st)],
            scratch_shapes=[pltpu.VMEM((RET_STREAMS, dh, dh), F32)]),
        out_shape=[jax.ShapeDtypeStruct((b, l, w), BF16), jax.ShapeDtypeStruct((b, l, w), BF16),
                   jax.ShapeDtypeStruct((b, RET_STREAMS, dh, dh), F32)],
        compiler_params=_cparams(("parallel", "arbitrary")),
        name="retention_scan",
    )(decay_logit.reshape(-1), p, p, p, rot_chunk, p, p, p, rot_chunk, rot_in, s0)


def _ret_final_body(of_ref, ob_ref, g_ref, w_ref, y_ref):
    for h in range(RET_HEADS):
        sl = slice(h * RET_DIM, (h + 1) * RET_DIM)
        o = of_ref[:, sl].astype(F32) + ob_ref[:, sl].astype(F32)
        mu = jnp.mean(o, axis=-1, keepdims=True)
        var = jnp.mean(jnp.square(o - mu), axis=-1, keepdims=True)
        y = (o - mu) * lax.rsqrt(var + EPS) * w_ref[:, sl]
        g = g_ref[:, sl]
        y_ref[:, sl] = (y * (g * jax.nn.sigmoid(g))).astype(y_ref.dtype)


def retention_finalize(o_f, o_b, p, gate_col, gn_w, tl):
    b, l, w = o_f.shape
    tok = lambda bi, i: (bi, i, 0)
    return pl.pallas_call(
        _ret_final_body,
        grid=(b, l // tl),
        in_specs=[pl.BlockSpec((None, tl, w), tok),
                  pl.BlockSpec((None, tl, w), tok),
                  pl.BlockSpec((None, tl, w), lambda bi, i: (bi, i, gate_col // w)),
                  pl.BlockSpec((1, w), lambda bi, i: (0, 0))],
        out_specs=pl.BlockSpec((None, tl, w), tok),
        out_shape=jax.ShapeDtypeStruct((b, l, w), BF16),
        compiler_params=_cparams(("parallel", "parallel")),
        name="retention_finalize",
    )(o_f, o_b, p, gn_w.reshape(1, w))


def retention_rope_tables(l):
    inv = ROPE_THETA ** (-np.linspace(0.0, 1.0, RET_DIM // 2))
    inv2 = np.concatenate([inv, inv])
    sign = np.concatenate([-np.ones(RET_DIM // 2), np.ones(RET_DIM // 2)])
    a = (RET_CHUNK * np.arange(l // RET_CHUNK, dtype=np.float64))[:, None] * inv2
    b = np.arange(RET_CHUNK, dtype=np.float64)[:, None] * inv2
    chunk = np.stack([np.cos(a), np.sin(a), sign * np.sin(a), sign * np.cos(a)], axis=1)
    return chunk.astype(np.float32), np.stack([np.cos(b), np.sin(b)]).astype(np.float32)


def retention_pallas(p_lat, p_ctx, decay_logit, gn_w, need_ctx, col0=0):
    b, l, _ = p_lat.shape
    n_ctx = p_ctx.shape[1]
    gate_col = col0 + 3 * RET_WIDTH
    s0 = jnp.zeros((b, RET_STREAMS, RET_DIM, RET_DIM), F32)
    oc_f, oc_b, s_ctx = retention_scan(p_ctx, col0, decay_logit, *retention_rope_tables(n_ctx), s0)
    ol_f, ol_b, _ = retention_scan(p_lat, col0, decay_logit, *retention_rope_tables(l), s_ctx)
    out_lat = retention_finalize(ol_f, ol_b, p_lat, gate_col, gn_w, PREP_TILE)
    out_ctx = retention_finalize(oc_f, oc_b, p_ctx, gate_col, gn_w, n_ctx) if need_ctx else None
    return out_lat, out_ctx


DN_BLOCK = 128
DN_STREAMS = 2 * DN_HEADS
DN_SUBS = 4


def _shift_rows(x, prev_row, next_row):
    n = x.shape[0]
    r = lax.broadcasted_iota(jnp.int32, (SUBLANES, 1), 0)
    down, up = pltpu.roll(x, 1, 0), pltpu.roll(x, n - 1, 0)
    x_prev = jnp.concatenate([jnp.where(r == 0, prev_row, down[:SUBLANES]), down[SUBLANES:]], axis=0)
    x_next = jnp.concatenate([up[:n - SUBLANES], jnp.where(r == SUBLANES - 1, next_row, up[n - SUBLANES:])], axis=0)
    return x_prev, x_next


def _conv3(x_ref, xp_ref, xn_ref, w_ref):
    i = pl.program_id(1)
    prev_row = jnp.where(i > 0, xp_ref[SUBLANES - 1:SUBLANES, :], 0.0)
    next_row = jnp.where(i < pl.num_programs(1) - 1, xn_ref[0:1, :], 0.0)
    x = x_ref[...]
    x_prev, x_next = _shift_rows(x, prev_row, next_row)
    return x_prev * w_ref[0:1, :] + x * w_ref[1:2, :] + x_next * w_ref[2:3, :]


def _dn_prep_body(x_ref, xp_ref, xn_ref, ab_ref, cw_ref, alog_ref, dtb_ref,
                  q_ref, k_ref, v_ref, gb_ref, gam_ref, gamt_ref):
    y = _conv3(x_ref, xp_ref, xn_ref, cw_ref)
    y = y * jax.nn.sigmoid(y)
    for h in range(DN_HEADS):
        for part, (ref, mul) in enumerate(((q_ref, DN_DIM ** -0.5), (k_ref, 1.0))):
            sl = slice(part * DN_WIDTH + h * DN_DIM, part * DN_WIDTH + (h + 1) * DN_DIM)
            t = y[:, sl]
            t = t * lax.rsqrt(jnp.sum(t * t, axis=-1, keepdims=True) + EPS)
            ref[:, h * DN_DIM:(h + 1) * DN_DIM] = t * mul if mul != 1.0 else t
    v_ref[...] = y[:, 2 * DN_WIDTH:]

    ab = ab_ref[...]
    z = ab + dtb_ref[...]
    softplus = jnp.maximum(z, 0.0) + jnp.log(1.0 + jnp.exp(-jnp.abs(z)))
    g = -jnp.exp(alog_ref[...]) * softplus
    lane = lax.broadcasted_iota(jnp.int32, ab.shape, 1)
    gb_ref[...] = jnp.where(lane < DN_STREAMS, g, jax.nn.sigmoid(ab))

    c = DN_BLOCK
    ii = lax.broadcasted_iota(jnp.int32, (c, c), 0)
    jj = lax.broadcasted_iota(jnp.int32, (c, c), 1)
    tri_f = (ii >= jj).astype(F32)
    lane_c = lax.broadcasted_iota(jnp.int32, (c, ab.shape[1]), 1)
    for n in range(x_ref.shape[0] // c):
        gc = g[n * c:(n + 1) * c, :]
        cf = jnp.dot(tri_f, gc, preferred_element_type=F32, precision=lax.Precision.HIGHEST)
        cb = cf[c - 1:c, :] - cf + gc
        gam = jnp.where(lane_c < DN_HEADS, cf, cb)
        gam_ref[n * c:(n + 1) * c, :] = gam
        gamt_ref[n] = gam.T[:DN_STREAMS, :]


def dn_prep(p, ab_col, conv_w, a_log, dt_bias, tl):
    b, l, _ = p.shape
    c3 = 3 * DN_WIDTH
    nblk8 = l // SUBLANES
    lanes = LANES
    pad8 = lambda a: jnp.pad(a.reshape(1, DN_STREAMS).astype(F32), ((0, 0), (0, lanes - DN_STREAMS)))
    tok = lambda bi, i: (bi, i, 0)
    return pl.pallas_call(
        _dn_prep_body,
        grid=(b, l // tl),
        in_specs=[pl.BlockSpec((None, tl, c3), tok),
                  pl.BlockSpec((None, SUBLANES, c3),
                               lambda bi, i: (bi, jnp.maximum(i * (tl // SUBLANES) - 1, 0), 0)),
                  pl.BlockSpec((None, SUBLANES, c3),
                               lambda bi, i: (bi, jnp.minimum((i + 1) * (tl // SUBLANES), nblk8 - 1), 0)),
                  pl.BlockSpec((None, tl, lanes), lambda bi, i: (bi, i, ab_col // lanes)),
                  pl.BlockSpec((3, c3), lambda bi, i: (0, 0)),
                  pl.BlockSpec((1, lanes), lambda bi, i: (0, 0)),
                  pl.BlockSpec((1, lanes), lambda bi, i: (0, 0))],
        out_specs=[pl.BlockSpec((None, tl, DN_WIDTH), tok)] * 3
        + [pl.BlockSpec((None, tl, lanes), tok)] * 2
        + [pl.BlockSpec((None, tl // DN_BLOCK, DN_STREAMS, DN_BLOCK), lambda bi, i: (bi, i, 0, 0))],
        out_shape=[jax.ShapeDtypeStruct((b, l, DN_WIDTH), F32)] * 3
        + [jax.ShapeDtypeStruct((b, l, lanes), F32)] * 2
        + [jax.ShapeDtypeStruct((b, l // DN_BLOCK, DN_STREAMS, DN_BLOCK), F32)],
        compiler_params=_cparams(("parallel", "parallel")),
        name="dn_prep",
    )(p, p, p, p, conv_w, pad8(a_log), pad8(dt_bias))


def _dot16(a, b):
    return jnp.dot(a.astype(BF16), b.astype(BF16), preferred_element_type=F32)


def _unit_triangular_inverses(ms, ii, jj):
    c = ms[0].shape[0]
    eye = (ii == jj).astype(F32)
    pair = (ii // 2) == (jj // 2)
    ts = [eye - jnp.where(pair, m, 0.0) for m in ms]
    s = 2
    while s < c:
        sub = ((ii // (2 * s)) == (jj // (2 * s))) & ((ii // s) != (jj // s))
        t16 = [t.astype(BF16) for t in ts]
        xs = [_dot16(jnp.where(sub, m, 0.0), t) for m, t in zip(ms, t16)]
        ts = [t - _dot16(tb, x) for t, tb, x in zip(ts, t16, xs)]
        s *= 2
    return ts


def _dn_scan_body(qf_ref, kf_ref, vf_ref, gbf_ref, gamf_ref, gamtf_ref,
                  qb_ref, kb_ref, vb_ref, gbb_ref, gamb_ref, gamtb_ref, s0_ref,
                  of_ref, ob_ref, sf_ref, s_scr):
    n = pl.program_id(1)

    @pl.when(n == 0)
    def _():
        s_scr[...] = s0_ref[...]

    c, dh = DN_BLOCK, DN_DIM
    ii = lax.broadcasted_iota(jnp.int32, (c, c), 0)
    jj = lax.broadcasted_iota(jnp.int32, (c, c), 1)
    incl = {True: ii >= jj, False: ii <= jj}
    strict = {True: ii > jj, False: ii < jj}
    srcs = {True: (qf_ref, kf_ref, vf_ref, gbf_ref, gamf_ref, gamtf_ref),
            False: (qb_ref, kb_ref, vb_ref, gbb_ref, gamb_ref, gamtb_ref)}
    streams = [(d == 0, h) for d in range(2) for h in range(DN_HEADS)]
    hs = lambda h: slice(h * dh, (h + 1) * dh)
    subs = gamtf_ref.shape[0]
    items = [(r, f, h, j, (j if f else subs - 1 - j)) for j in range(subs) for r, (f, h) in enumerate(streams)]
    rows = lambda blk: slice(blk * c, (blk + 1) * c)
    rng = range(len(items))

    q = [srcs[f][0][rows(bk), hs(h)] for _, f, h, _, bk in items]
    k = [srcs[f][1][rows(bk), hs(h)] for _, f, h, _, bk in items]
    v = [srcs[f][2][rows(bk), hs(h)] for _, f, h, _, bk in items]
    beta = [srcs[f][3][rows(bk), DN_STREAMS + r:DN_STREAMS + r + 1] for r, f, _, _, bk in items]
    gcol = [srcs[f][4][rows(bk), r:r + 1] for r, f, _, _, bk in items]
    grow = [srcs[f][5][bk, r:r + 1, :] for r, f, _, _, bk in items]
    g_last = [gcol[i][c - 1:c, :] if items[i][1] else gcol[i][0:1, :] for i in rng]

    e = [jnp.exp(jnp.where(incl[items[i][1]], gcol[i] - grow[i], 0.0)) for i in rng]
    kb = [k[i] * beta[i] for i in rng]
    k16 = [k[i].astype(BF16) for i in rng]
    m = [_nt_dot(kb[i].astype(BF16), k16[i]) * jnp.where(strict[items[i][1]], e[i], 0.0) for i in rng]
    attn = [(_nt_dot(q[i].astype(BF16), k16[i]) * jnp.where(incl[items[i][1]], e[i], 0.0)).astype(BF16) for i in rng]
    t = _unit_triangular_inverses(m, ii, jj)
    eg = [jnp.exp(gcol[i]) for i in rng]
    sol = [_dot16(t[i], jnp.concatenate([v[i] * beta[i], kb[i] * eg[i]], axis=-1)) for i in rng]
    u = [sol[i][:, :dh] for i in rng]
    w16 = [sol[i][:, dh:].astype(BF16) for i in rng]
    q_dec = [(q[i] * eg[i]).astype(BF16) for i in rng]
    k_dec = [(k[i] * jnp.exp(g_last[i] - gcol[i])).astype(BF16) for i in rng]
    blk_dec = [jnp.exp(g_last[i]) for i in rng]

    s = [s_scr[r] for r in range(DN_STREAMS)]
    for j in range(subs):
        idx = [i for i in rng if items[i][3] == j]
        s16 = [s[r].astype(BF16) for r in range(DN_STREAMS)]
        v_new = [(u[i] - jnp.dot(w16[i], s16[items[i][0]], preferred_element_type=F32)).astype(BF16) for i in idx]
        o = [jnp.dot(q_dec[i], s16[items[i][0]], preferred_element_type=F32)
             + jnp.dot(attn[i], vn, preferred_element_type=F32) for i, vn in zip(idx, v_new)]
        s = [s[items[i][0]] * blk_dec[i] + _tn_dot(k_dec[i], vn) for i, vn in zip(idx, v_new)]
        for i, oi in zip(idx, o):
            _, f, h, _, bk = items[i]
            (of_ref if f else ob_ref)[rows(bk), hs(h)] = oi.astype(of_ref.dtype)
    for r in range(DN_STREAMS):
        s_scr[r] = s[r]

    @pl.when(n == pl.num_programs(1) - 1)
    def _():
        sf_ref[...] = s_scr[...]


def dn_scan(q, k, v, gb, gam, gamt, s0):
    b, l, w = q.shape
    subs = min(DN_SUBS, l // DN_BLOCK)
    c = DN_BLOCK * subs
    n = l // c
    lanes = gb.shape[-1]
    f3 = lambda bi, ni: (bi, ni, 0)
    b3 = lambda bi, ni: (bi, n - 1 - ni, 0)
    f4 = lambda bi, ni: (bi, ni, 0, 0)
    b4 = lambda bi, ni: (bi, n - 1 - ni, 0, 0)
    st = lambda bi, ni: (bi, 0, 0, 0)

    def specs(m3, m4):
        return [pl.BlockSpec((None, c, w), m3)] * 3 + [pl.BlockSpec((None, c, lanes), m3)] * 2 + [
            pl.BlockSpec((None, subs, DN_STREAMS, DN_BLOCK), m4)]

    return pl.pallas_call(
        _dn_scan_body,
        grid=(b, n),
        in_specs=specs(f3, f4) + specs(b3, b4) + [pl.BlockSpec((None, DN_STREAMS, DN_DIM, DN_DIM), st)],
        out_specs=[pl.BlockSpec((None, c, w), f3), pl.BlockSpec((None, c, w), b3),
                   pl.BlockSpec((None, DN_STREAMS, DN_DIM, DN_DIM), st)],
        out_shape=[jax.ShapeDtypeStruct((b, l, w), BF16), jax.ShapeDtypeStruct((b, l, w), BF16),
                   jax.ShapeDtypeStruct((b, DN_STREAMS, DN_DIM, DN_DIM), F32)],
        scratch_shapes=[pltpu.VMEM((DN_STREAMS, DN_DIM, DN_DIM), F32)],
        compiler_params=_cparams(("parallel", "arbitrary")),
        name="dn_scan",
    )(q, k, v, gb, gam, gamt, q, k, v, gb, gam, gamt, s0)


def _dn_final_body(of_ref, ob_ref, g_ref, w_ref, y_ref):
    for h in range(DN_HEADS):
        sl = slice(h * DN_DIM, (h + 1) * DN_DIM)
        o = of_ref[:, sl].astype(F32) + ob_ref[:, sl].astype(F32)
        y = _head_rms(o, w_ref[...])
        g = g_ref[:, sl]
        y_ref[:, sl] = (y * (g * jax.nn.sigmoid(g))).astype(y_ref.dtype)


def dn_finalize(o_f, o_b, p, gate_col, norm_w, tl):
    b, l, w = o_f.shape
    tok = lambda bi, i: (bi, i, 0)
    return pl.pallas_call(
        _dn_final_body,
        grid=(b, l // tl),
        in_specs=[pl.BlockSpec((None, tl, w), tok), pl.BlockSpec((None, tl, w), tok),
                  pl.BlockSpec((None, tl, w), lambda bi, i: (bi, i, gate_col // w)),
                  pl.BlockSpec((1, DN_DIM), lambda bi, i: (0, 0))],
        out_specs=pl.BlockSpec((None, tl, w), tok),
        out_shape=jax.ShapeDtypeStruct((b, l, w), BF16),
        compiler_params=_cparams(("parallel", "parallel")),
        name="dn_finalize",
    )(o_f, o_b, p, norm_w.reshape(1, DN_DIM))


def deltanet_pallas(p_lat, p_ctx, conv_w, a_log, dt_bias, norm_w, need_ctx, ab_col=3072, gate_col=1536):
    b = p_lat.shape[0]
    n_ctx = p_ctx.shape[1]
    s0 = jnp.zeros((b, DN_STREAMS, DN_DIM, DN_DIM), F32)
    fc = dn_prep(p_ctx, ab_col, conv_w, a_log, dt_bias, n_ctx)
    oc_f, oc_b, s_ctx = dn_scan(*fc, s0)
    fl = dn_prep(p_lat, ab_col, conv_w, a_log, dt_bias, PREP_TILE)
    ol_f, ol_b, _ = dn_scan(*fl, s_ctx)
    out_lat = dn_finalize(ol_f, ol_b, p_lat, gate_col, norm_w, PREP_TILE)
    out_ctx = dn_finalize(oc_f, oc_b, p_ctx, gate_col, norm_w, n_ctx) if need_ctx else None
    return out_lat, out_ctx


def _hy_factors(n):
    n2 = 256 if n >= 32768 else 128
    return n // n2, n2


def _bf16_const(m):
    return m.astype(ml_dtypes.bfloat16)


def _hy_constants(l):
    n = 2 * l
    n1, n2 = _hy_factors(n)
    a1 = 2.0 * np.pi * np.outer(np.arange(n1), np.arange(n1)) / n1
    c1, s1 = np.cos(a1), np.sin(a1)
    h = n1 // 2
    rows_fwd = np.block([[c1[:, :h], s1[:, :h]], [-s1[:, :h], c1[:, :h]]])
    rows_taps = np.concatenate([c1, -s1], axis=0)
    rows_inv = np.block([[c1[:h], -s1[:h]], [s1[:h], c1[:h]]]) / n
    a2 = 2.0 * np.pi * np.outer(np.arange(n2), np.arange(n2)) / n2
    c2, s2 = np.cos(a2), np.sin(a2)
    slab_fwd = np.block([[c2, s2], [-s2, c2]])
    slab_inv = np.block([[c2, -s2], [s2, c2]])
    th = 2.0 * np.pi * np.outer(np.arange(n1), np.arange(n2)) / n
    lanes = lambda t: t[:, :, None].astype(np.float32)
    return dict(n1=n1, n2=n2, rows_fwd=_bf16_const(rows_fwd), rows_taps=_bf16_const(rows_taps),
                rows_inv=_bf16_const(rows_inv), slab_fwd=_bf16_const(slab_fwd), slab_inv=_bf16_const(slab_inv),
                tw_cos=lanes(np.cos(th)), tw_sin=lanes(np.sin(th)))


def _hy_prep_body(x_ref, xp_ref, xn_ref, w_ref, v_ref, x1_ref, x2_ref):
    y = _conv3(x_ref, xp_ref, xn_ref, w_ref)
    v_ref[...] = y[:, :HY_CH]
    x1_ref[...] = y[:, HY_CH:2 * HY_CH]
    x2_ref[...] = y[:, 2 * HY_CH:]


def hy_prep(p, col0, conv_w, tl):
    b, l, _ = p.shape
    c3 = 3 * HY_CH
    nblk8 = l // SUBLANES
    cb = col0 // c3
    tok = lambda bi, i: (bi, i, 0)
    return pl.pallas_call(
        _hy_prep_body,
        grid=(b, l // tl),
        in_specs=[pl.BlockSpec((None, tl, c3), lambda bi, i: (bi, i, cb)),
                  pl.BlockSpec((None, SUBLANES, c3),
                               lambda bi, i: (bi, jnp.maximum(i * (tl // SUBLANES) - 1, 0), cb)),
                  pl.BlockSpec((None, SUBLANES, c3),
                               lambda bi, i: (bi, jnp.minimum((i + 1) * (tl // SUBLANES), nblk8 - 1), cb)),
                  pl.BlockSpec((3, c3), lambda bi, i: (0, 0))],
        out_specs=[pl.BlockSpec((None, tl, HY_CH), tok)] * 3,
        out_shape=[jax.ShapeDtypeStruct((b, l, HY_CH), F32)] * 3,
        compiler_params=_cparams(("parallel", "parallel")),
        name="hy_prep",
    )(p, p, p, conv_w)


def _split_bf16(a):
    hi = a.astype(BF16)
    return hi, (a - hi.astype(F32)).astype(BF16)


HY_FILTER_GROUP = 8


def _hy_filter_body(t_ref, zt_ref, w1t_ref, b1_ref, w2t_ref, b2_ref, w3_ref, rates_ref, m_ref, *out_refs):
    a_refs, sum_ref = out_refs[:-1], out_refs[-1]
    i = pl.program_id(0)
    hp = lax.Precision.HIGHEST
    h = jnp.sin(HY_SIN_FREQ * (jnp.dot(w1t_ref[...], zt_ref[...], preferred_element_type=F32, precision=hp)
                               + b1_ref[...]))
    h = jnp.sin(HY_SIN_FREQ * (jnp.dot(w2t_ref[...], h, preferred_element_type=F32, precision=hp) + b2_ref[...]))
    half = h.shape[1] // 2
    taps = []
    for d in range(2):
        h_hi, h_lo = _split_bf16(h[:, d * half:(d + 1) * half])
        w_hi, w_lo = _split_bf16(w3_ref[d])
        td = _tn_dot(h_hi, w_hi) + (_tn_dot(h_hi, w_lo) + _tn_dot(h_lo, w_hi))
        taps.append(td * jnp.exp(-t_ref[d * half:(d + 1) * half, :] * rates_ref[...]))

    @pl.when(i == 0)
    def _():
        sum_ref[...] = jnp.zeros_like(sum_ref)

    sum_ref[...] += (jnp.sum(jnp.abs(taps[0]), axis=0, keepdims=True)
                     + jnp.sum(jnp.abs(taps[1]), axis=0, keepdims=True))
    row = lax.broadcasted_iota(jnp.int32, (half, 1), 0)
    taps[1] = jnp.where((row == 0) & (i == 0), 0.0, taps[1])
    nh = half // HY_FILTER_GROUP
    for j in range(HY_FILTER_GROUP):
        for o in range(HY_ORDER):
            cs = slice(o * HY_CH, (o + 1) * HY_CH)
            x = jnp.concatenate([taps[0][j * nh:(j + 1) * nh, cs], taps[1][j * nh:(j + 1) * nh, cs]], axis=0)
            y = _dot16(m_ref[...], x)
            a_refs[o][0, :, j, :] = y[:2 * nh]
            a_refs[o][1, :, j, :] = y[2 * nh:]


def hy_filter_rows(l, consts, w1, b1, w2, b2, w3):
    n = 2 * l
    n1, n2, g = consts["n1"], consts["n2"], HY_FILTER_GROUP
    tr = g * n1
    grp, d, j, m = np.meshgrid(np.arange(n2 // g), np.arange(2), np.arange(g), np.arange(n1 // 2), indexing="ij")
    r = (n2 * (m + (n1 // 2) * d) + g * grp + j).reshape(-1)
    pos = np.where(r < l, r, n - r).astype(np.float64)
    pos[r == l] = 0.0
    t = pos / max(l - 1, 1)
    bands = np.linspace(1e-4, HY_BANDS - 1, HY_BANDS)
    phase = (2.0 * math.pi / l) * pos[:, None] * bands[None, :]
    kf = 32
    z = np.concatenate([t[:, None], np.cos(phase), -np.sin(phase)], axis=-1)
    zt = np.pad(z, ((0, 0), (0, kf - z.shape[1]))).T.astype(np.float32)
    t_col = t[:, None].astype(np.float32)
    w1t = jnp.pad(w1.astype(F32), ((0, kf - w1.shape[0]), (0, 0))).T
    hid = w1.shape[1]
    w3d = w3.astype(F32).reshape(hid, HY_ORDER, 2, HY_CH).transpose(2, 0, 1, 3).reshape(2, hid, HY_ORDER * HY_CH)
    rates = jnp.abs(jnp.linspace(math.log(HY_TARGET) / HY_DECAY_LONG, math.log(HY_TARGET) / HY_DECAY_SHORT,
                                 HY_CH, dtype=F32))
    rates = jnp.tile(rates, HY_ORDER).reshape(1, HY_ORDER * HY_CH)
    fixed = lambda i: (0, 0)
    mat = consts["rows_taps"]
    return pl.pallas_call(
        _hy_filter_body,
        grid=(n // tr,),
        in_specs=[pl.BlockSpec((tr, 1), lambda i: (i, 0)), pl.BlockSpec((kf, tr), lambda i: (0, i)),
                  pl.BlockSpec((hid, kf), fixed), pl.BlockSpec((hid, 1), fixed),
                  pl.BlockSpec((hid, hid), fixed), pl.BlockSpec((hid, 1), fixed),
                  pl.BlockSpec((2, hid, HY_ORDER * HY_CH), lambda i: (0, 0, 0)),
                  pl.BlockSpec((1, HY_ORDER * HY_CH), fixed), pl.BlockSpec(mat.shape, fixed)],
        out_specs=[pl.BlockSpec((2, n1, g, HY_CH), lambda i: (0, 0, i, 0))] * HY_ORDER
        + [pl.BlockSpec((1, HY_ORDER * HY_CH), fixed)],
        out_shape=[jax.ShapeDtypeStruct((2, n1, n2, HY_CH), F32)] * HY_ORDER
        + [jax.ShapeDtypeStruct((1, HY_ORDER * HY_CH), F32)],
        compiler_params=_cparams(("arbitrary",)),
        name="hy_filter_rows",
    )(t_col, zt, w1t, b1.astype(F32).reshape(hid, 1), w2.astype(F32).T, b2.astype(F32).reshape(hid, 1), w3d, rates,
      mat)


HY_ROWS_GROUP = {"fwd": 16, "inv": 8}


def _hy_rows_body(*refs, mode):
    if mode == "fwd":
        m_ref, x_ref, o_ref = refs
    else:
        m_ref, x_ref, xn_ref, z_ref, skip_ref, o_ref = refs
    for j in range(HY_ROWS_GROUP[mode]):
        x = jnp.concatenate([x_ref[0, :, j, :], x_ref[1, :, j, :]], axis=0)
        y = _dot16(m_ref[...], x)
        half = y.shape[0] // 2
        for bi in range(2):
            yb = y[bi * half:(bi + 1) * half]
            if mode == "inv":
                yb = xn_ref[bi, :, j, :] * (yb + z_ref[bi, :, j, :] * skip_ref[...])
            o_ref[bi, :, j, :] = yb


def hy_rows_pass(mode, mat, x, extra=()):
    n2, c = x.shape[-2:]
    g = HY_ROWS_GROUP[mode]
    blk = lambda rows: pl.BlockSpec((2, rows, g, c), lambda j: (0, 0, j, 0))
    mspec = _resident(mat.shape)
    vec = pl.BlockSpec((1, c), lambda j: (0, 0))
    if mode == "fwd":
        n1 = 2 * x.shape[1]
        in_specs = [mspec, blk(n1 // 2)]
        out_rows = n1
    else:
        n1 = x.shape[1]
        in_specs = [mspec, blk(n1), blk(n1 // 2), blk(n1 // 2), vec]
        out_rows = n1 // 2
    return pl.pallas_call(
        functools.partial(_hy_rows_body, mode=mode),
        grid=(n2 // g,),
        in_specs=in_specs,
        out_specs=blk(out_rows),
        out_shape=jax.ShapeDtypeStruct((2, out_rows, n2, c), F32),
        compiler_params=_cparams(("parallel",)),
        name="hy_rows_" + mode,
    )(mat, x, *extra)


def _hy_slab_body(*refs, with_filter):
    if with_filter:
        f_ref, i_ref, tc_ref, ts_ref, a_hbm, h_ref, o_ref, ring, sem = refs
        k, steps, kb = pl.program_id(0), pl.num_programs(0), h_ref.shape[1]

        def fetch(step):
            slot = lax.rem(step, HY_RING_SLOTS)
            return pltpu.make_async_copy(a_hbm.at[:, pl.ds(step * kb, kb)], ring.at[slot], sem.at[slot])

        @pl.when(k == 0)
        def _():
            for first in range(HY_RING_SLOTS - 1):
                @pl.when(first < steps)
                def _():
                    fetch(first).start()

        @pl.when(k + HY_RING_SLOTS - 1 < steps)
        def _():
            fetch(k + HY_RING_SLOTS - 1).start()

        fetch(k).wait()
        a_ref = ring.at[lax.rem(k, HY_RING_SLOTS)]
    else:
        f_ref, tc_ref, ts_ref, a_ref, sum_ref, o_ref = refs
        scale = 1.0 / (sum_ref[...] + EPS)
    n2, c = a_ref.shape[2:]
    for s in range(a_ref.shape[1]):
        tc = jnp.broadcast_to(tc_ref[s], (n2, c))
        ts = jnp.broadcast_to(ts_ref[s], (n2, c))
        ar, ai = a_ref[0, s], a_ref[1, s]
        x = _dot16(f_ref[...], jnp.concatenate([ar * tc + ai * ts, ai * tc - ar * ts], axis=0))
        xr, xi = x[:n2], x[n2:]
        if not with_filter:
            o_ref[0, s] = (xr * scale).astype(o_ref.dtype)
            o_ref[1, s] = (xi * scale).astype(o_ref.dtype)
            continue
        hr, hi = h_ref[0, s].astype(F32), h_ref[1, s].astype(F32)
        y = _dot16(i_ref[...], jnp.concatenate([xr * hr - xi * hi, xr * hi + xi * hr], axis=0))
        yr, yi = y[:n2], y[n2:]
        o_ref[0, s] = yr * tc - yi * ts
        o_ref[1, s] = yi * tc + yr * ts


HY_SLABS_PER_STEP = 4
HY_RING_SLOTS = 3


def hy_slab_pass(consts, a, h=None, l1_sum=None):
    _, n1, n2, c = a.shape
    kb = HY_SLABS_PER_STEP
    fwd = consts["slab_fwd"]
    mspec = _resident(fwd.shape)
    tw = pl.BlockSpec((kb, n2, 1), lambda k: (k, 0, 0))
    slab = pl.BlockSpec((2, kb, n2, c), lambda k: (0, k, 0, 0))
    if h is None:
        args = [fwd, consts["tw_cos"], consts["tw_sin"], a, l1_sum]
        in_specs = [mspec, tw, tw, slab, pl.BlockSpec((1, c), lambda k: (0, 0))]
    else:
        args = [fwd, consts["slab_inv"], consts["tw_cos"], consts["tw_sin"], a, h]
        in_specs = [mspec, mspec, tw, tw, pl.BlockSpec(memory_space=pl.ANY), slab]
    ring = [] if h is None else [pltpu.VMEM((HY_RING_SLOTS, 2, kb, n2, c), F32),
                                 pltpu.SemaphoreType.DMA((HY_RING_SLOTS,))]
    return pl.pallas_call(
        functools.partial(_hy_slab_body, with_filter=h is not None),
        grid=(n1 // kb,),
        in_specs=in_specs,
        out_specs=slab,
        out_shape=jax.ShapeDtypeStruct(a.shape, F32 if h is not None else BF16),
        scratch_shapes=ring,
        compiler_params=_cparams(("arbitrary",) if h is not None else ("parallel",)),
        name="hy_slab_conv" if h is not None else "hy_slab_fwd",
    )(*args)


def hyena_pallas(p, conv_w, w1, b1, w2, b2, w3, skip, col0=RET_COLS):
    b, l, _ = p.shape
    assert b == 2, "the two batch rows are packed as one complex signal"
    c = HY_CH
    consts = _hy_constants(l)
    n1, n2 = consts["n1"], consts["n2"]
    *a_h, sums = hy_filter_rows(l, consts, w1, b1, w2, b2, w3)
    spectra = [hy_slab_pass(consts, a_h[o], l1_sum=sums[:, o * c:(o + 1) * c]) for o in range(HY_ORDER)]
    split = lambda t: t.reshape(2, n1 // 2, n2, c)
    v, x1, x2 = hy_prep(p, col0, conv_w, PREP_TILE)
    z = split(v)
    for o, xn in enumerate((x1, x2)):
        a = hy_rows_pass("fwd", consts["rows_fwd"], z)
        bm = hy_slab_pass(consts, a, spectra[o])
        z = hy_rows_pass("inv", consts["rows_inv"], bm,
                         extra=(split(xn), z, skip[o].astype(F32).reshape(1, c)))
    return z.reshape(b, l, c)


def _pad_cols(w, mult):
    n = w.shape[-1]
    pad = (-n) % mult
    return jnp.pad(w, ((0, 0), (0, pad))) if pad else w


def kernel(x, c, ctx, c_ctx, mod_w, mod_b, norm_mix_w, norm_ffn_w, ffn_w_in, ffn_w_out,
           ab_w_in, ab_w_out, dn_conv_w, dn_a_log, dn_dt_bias, dn_norm_w, swa_q_norm_w, swa_k_norm_w,
           swa_sink, cd_w_in, cd_w_out, ret_decay_logit, ret_gn_w, hy_conv_w, hy_f_w1, hy_f_b1,
           hy_f_w2, hy_f_b2, hy_f_w3, hy_bias):
    B, L, D = x.shape
    n_ctx = ctx.shape[1]
    rows = L // GRID_W
    c_rows = jnp.zeros((SUBLANES, D), F32).at[:B].set(c).at[B].set(c_ctx)
    mod_all = modulation(c_rows, mod_w, mod_b)
    hid = ffn_w_out.shape[1]
    h_ctx = ctx
    for layer in range(DEPTH):
        need_ctx = layer != DEPTH - 1
        i = layer // 2
        m = mod_all[layer].reshape(SUBLANES, 6, D)
        mod = [m[:B, j][:, None, :] for j in range(6)]
        mod_c = [jnp.broadcast_to(m[B, j][None, None, :], (B, 1, D)) for j in range(6)]
        if layer % 2 == 0:
            w = ab_w_in[i]
            w_in = jnp.concatenate([w[:, :4 * DN_WIDTH], w[:, DN_COLS:], w[:, 4 * DN_WIDTH:DN_COLS]], axis=1)
            w_out = ab_w_out[i]
        else:
            w = cd_w_in[i]
            w_in = jnp.concatenate([w[:, RET_COLS:], w[:, :RET_COLS]], axis=1)
            w_out = cd_w_out[i]
        w_in_p = _pad_cols(w_in, LANES).astype(BF16)
        p_lat = in_projection(x, norm_mix_w[layer], mod[0], mod[1], w_in_p, INPROJ_TILE)
        p_ctx = in_projection(h_ctx, norm_mix_w[layer], mod_c[0], mod_c[1], w_in_p, n_ctx)
        if layer % 2 == 0:
            swa0 = 4 * DN_WIDTH
            ab0 = swa0 + (SWA_HEADS + 2 * SWA_KV_HEADS) * SWA_DIM
            a_lat, a_ctx = deltanet_pallas(p_lat, p_ctx, dn_conv_w[i], dn_a_log[i], dn_dt_bias[i],
                                           dn_norm_w[i], need_ctx, ab_col=ab0, gate_col=3 * DN_WIDTH)
            b_lat, b_ctx = window_gqa_pallas(p_lat, p_ctx, swa0, axial_rope_tables(rows), swa_q_norm_w[i],
                                             swa_k_norm_w[i], swa_sink[i], need_ctx)
        else:
            a_lat, a_ctx = retention_pallas(p_lat, p_ctx, ret_decay_logit[i], ret_gn_w[i], need_ctx,
                                            col0=(HY_ORDER + 1) * HY_CH)
            assert not need_ctx, "the last layer's context outputs reach no latent token"
            b_lat = hyena_pallas(p_lat, hy_conv_w[i], hy_f_w1[i], hy_f_b1[i], hy_f_w2[i], hy_f_b2[i],
                                 hy_f_w3[i], hy_bias[i], col0=0)
            b_ctx = None
        wo = w_out.astype(BF16)
        wg = ffn_w_in[layer][:, :hid].astype(BF16)
        wu = ffn_w_in[layer][:, hid:].astype(BF16)
        wd = ffn_w_out[layer].astype(BF16)
        x = out_projection_ffn(x, a_lat, b_lat, wo, mod[2], norm_ffn_w[layer], mod[3], mod[4], mod[5],
                               wg, wu, wd, DENSE_TILE, FFN_HIDDEN_CHUNK)
        if need_ctx:
            h_ctx = out_projection_ffn(h_ctx, a_ctx, b_ctx, wo, mod_c[2], norm_ffn_w[layer], mod_c[3], mod_c[4],
                                       mod_c[5], wg, wu, wd, n_ctx, FFN_HIDDEN_CHUNK)
    return x
```
